```python
import math
import jax, jax.numpy as jnp
from jax import lax
import numpy as np

D_MODEL = 1024
BATCH = 2
SEQ = 8192
DEPTH = 1

D_MIX = D_MODEL
SSM_WIDTH = D_MIX // 2
SSM_GROUP = 16
SSM_GROUPS = SSM_WIDTH // SSM_GROUP
SSM_STATE = 64
DT_MIN = 1e-3
DT_MAX = 1e-1

MLA_HEADS = 8
QK_NOPE = 64
QK_ROPE = 32
QK_HEAD = QK_NOPE + QK_ROPE
V_HEAD = 64
MLA_WIDTH = MLA_HEADS * V_HEAD
Q_LORA = 384
KV_LORA = 256
ROPE_THETA = 10000.0
Q_BLOCK = 128
EPS = 1e-6
NEG_INF = -1e30

D_IN = SSM_WIDTH + SSM_WIDTH + Q_LORA + KV_LORA + QK_ROPE + MLA_WIDTH

kernel_name = "hymba_s5_mla_hybrid_block"


def rms_norm(x, g):
    xf = x.astype(jnp.float32)
    y = xf * lax.rsqrt(jnp.mean(xf * xf, axis=-1, keepdims=True) + EPS)
    return (y * g.astype(jnp.float32)).astype(x.dtype)


def rope_tables(positions):
    half = QK_ROPE // 2
    inv_freq = ROPE_THETA ** (-jnp.arange(half, dtype=jnp.float32) * 2.0 / QK_ROPE)
    ang = positions.astype(jnp.float32)[..., None] * inv_freq
    return jnp.cos(ang), jnp.sin(ang)


def apply_rope(x, cos, sin):
    half = x.shape[-1] // 2
    xf = x.astype(jnp.float32)
    x1, x2 = xf[..., :half], xf[..., half:]
    out = jnp.concatenate([x1 * cos - x2 * sin, x2 * cos + x1 * sin], axis=-1)
    return out.astype(x.dtype)


def s5_mixer(u, log_dt, lam_re, lam_im, b_re, b_im, c_re, c_im, d_skip, w_glu, b_glu):
    bsz, seq, _ = u.shape
    uf = u.astype(jnp.float32).reshape(bsz, seq, SSM_GROUPS, SSM_GROUP)
    lam = lax.complex(lam_re.astype(jnp.float32), lam_im.astype(jnp.float32))
    dt = jnp.exp(log_dt.astype(jnp.float32))[:, None]
    lam_bar = jnp.exp(lam * dt)
    b_c = lax.complex(b_re.astype(jnp.float32), b_im.astype(jnp.float32))
    b_bar = ((lam_bar - 1.0) / lam)[..., None] * b_c
    c_c = lax.complex(c_re.astype(jnp.float32), c_im.astype(jnp.float32))
    bu = jnp.einsum('blgh,gph->blgp', uf.astype(jnp.complex64), b_bar)
    a = jnp.broadcast_to(lam_bar, bu.shape)

    def combine(left, right):
        a_l, b_l = left
        a_r, b_r = right
        return a_r * a_l, a_r * b_l + b_r

    _, states = lax.associative_scan(combine, (a, bu), axis=1)
    y = jnp.real(jnp.einsum('blgp,ghp->blgh', states, c_c)) + d_skip.astype(jnp.float32) * uf
    y = jax.nn.gelu(y.reshape(bsz, seq, SSM_WIDTH))
    y = y * jax.nn.sigmoid(y @ w_glu.astype(jnp.float32) + b_glu.astype(jnp.float32))
    return y.astype(u.dtype)


def mla_mixer(c_q, c_kv, k_r, cos, sin, q_a_g, w_q_b, kv_a_g, w_kv_b, q_norm_g, k_norm_g):
    bsz, seq, _ = c_q.shape
    q = (rms_norm(c_q, q_a_g) @ w_q_b).reshape(bsz, seq, MLA_HEADS, QK_HEAD)
    kv = (rms_norm(c_kv, kv_a_g) @ w_kv_b).reshape(bsz, seq, MLA_HEADS, QK_NOPE + V_HEAD)
    k_nope, v = kv[..., :QK_NOPE], kv[..., QK_NOPE:]
    k_rope = jnp.broadcast_to(k_r[:, :, None, :], (bsz, seq, MLA_HEADS, QK_ROPE))
    k = jnp.concatenate([k_nope, k_rope.astype(k_nope.dtype)], axis=-1)
    q = rms_norm(q, q_norm_g)
    k = rms_norm(k, k_norm_g)
    cos_h, sin_h = cos[:, :, None, :], sin[:, :, None, :]
    q = jnp.concatenate([q[..., :QK_NOPE], apply_rope(q[..., QK_NOPE:], cos_h, sin_h)], axis=-1)
    k = jnp.concatenate([k[..., :QK_NOPE], apply_rope(k[..., QK_NOPE:], cos_h, sin_h)], axis=-1)

    qh = q.transpose(0, 2, 1, 3)
    kh = k.transpose(0, 2, 1, 3)
    vh = v.transpose(0, 2, 1, 3)
    n_blocks = seq // Q_BLOCK
    qb = qh.reshape(bsz, MLA_HEADS, n_blocks, Q_BLOCK, QK_HEAD).transpose(2, 0, 1, 3, 4)
    scale = 1.0 / math.sqrt(QK_HEAD)
    k_idx = jnp.arange(seq)

    def one_block(args):
        qi, bi = args
        s = jnp.einsum('bhqd,bhkd->bhqk', qi, kh).astype(jnp.float32) * scale
        q_idx = bi * Q_BLOCK + jnp.arange(Q_BLOCK)
        mask = k_idx[None, :] <= q_idx[:, None]
        p = jax.nn.softmax(jnp.where(mask, s, NEG_INF), axis=-1)
        return jnp.einsum('bhqk,bhkd->bhqd', p.astype(vh.dtype), vh)

    ob = lax.map(one_block, (qb, jnp.arange(n_blocks)))
    return ob.transpose(1, 0, 3, 2, 4).reshape(bsz, seq, MLA_WIDTH)


def setup_inputs(seed: int = 0) -> dict:
    key = jax.random.key(seed)
    ks = jax.random.split(key, 32)
    f32 = jnp.float32

    def nrm(k, shape, std):
        return jax.random.normal(k, shape, f32) * std

    x = nrm(ks[0], (BATCH, SEQ, D_MODEL), 1.0)
    c = nrm(ks[1], (BATCH, D_MODEL), 1.0)
    offsets = jax.random.randint(ks[2], (BATCH, 1), 0, 1024, dtype=jnp.int32)
    positions = offsets + jnp.arange(SEQ, dtype=jnp.int32)[None, :]

    w_ada = nrm(ks[3], (DEPTH, D_MODEL, 3 * D_MODEL), 0.5 * D_MODEL ** -0.5)
    b_ada = nrm(ks[4], (DEPTH, 3 * D_MODEL), 0.02)
    norm_g = 1.0 + nrm(ks[5], (DEPTH, D_MODEL), 0.02)
    w_in = nrm(ks[6], (DEPTH, D_MODEL, D_IN), D_MODEL ** -0.5)

    log_dt = jax.random.uniform(ks[7], (DEPTH, SSM_GROUPS), f32, math.log(DT_MIN), math.log(DT_MAX))
    n_idx = jnp.arange(SSM_STATE, dtype=f32)
    lam_re = -0.5 + nrm(ks[8], (DEPTH, SSM_GROUPS, SSM_STATE), 0.01)
    lam_im = math.pi * n_idx + nrm(ks[9], (DEPTH, SSM_GROUPS, SSM_STATE), 0.01)
    b_std = (2.0 * SSM_GROUP) ** -0.5
    c_std = (2.0 * SSM_STATE) ** -0.5
    b_re = nrm(ks[10], (DEPTH, SSM_GROUPS, SSM_STATE, SSM_GROUP), b_std)
    b_im = nrm(ks[11], (DEPTH, SSM_GROUPS, SSM_STATE, SSM_GROUP), b_std)
    c_re = nrm(ks[12], (DEPTH, SSM_GROUPS, SSM_GROUP, SSM_STATE), c_std)
    c_im = nrm(ks[13], (DEPTH, SSM_GROUPS, SSM_GROUP, SSM_STATE), c_std)
    d_skip = nrm(ks[14], (DEPTH, SSM_GROUPS, SSM_GROUP), 1.0)
    w_glu = nrm(ks[15], (DEPTH, SSM_WIDTH, SSM_WIDTH), SSM_WIDTH ** -0.5)
    b_glu = nrm(ks[16], (DEPTH, SSM_WIDTH), 0.02)

    q_a_g = 1.0 + nrm(ks[17], (DEPTH, Q_LORA), 0.02)
    w_q_b = nrm(ks[18], (DEPTH, Q_LORA, MLA_HEADS * QK_HEAD), Q_LORA ** -0.5)
    kv_a_g = 1.0 + nrm(ks[19], (DEPTH, KV_LORA), 0.02)
    w_kv_b = nrm(ks[20], (DEPTH, KV_LORA, MLA_HEADS * (QK_NOPE + V_HEAD)), KV_LORA ** -0.5)
    q_norm_g = 1.0 + nrm(ks[21], (DEPTH, QK_HEAD), 0.02)
    k_norm_g = 1.0 + nrm(ks[22], (DEPTH, QK_HEAD), 0.02)

    w_out = nrm(ks[23], (DEPTH, D_MIX, D_MODEL), D_MIX ** -0.5)

    return {"x": x, "c": c, "positions": positions,
            "w_ada": w_ada, "b_ada": b_ada, "norm_g": norm_g, "w_in": w_in,
            "log_dt": log_dt, "lam_re": lam_re, "lam_im": lam_im,
            "b_re": b_re, "b_im": b_im, "c_re": c_re, "c_im": c_im,
            "d_skip": d_skip, "w_glu": w_glu, "b_glu": b_glu,
            "q_a_g": q_a_g, "w_q_b": w_q_b, "kv_a_g": kv_a_g, "w_kv_b": w_kv_b,
            "q_norm_g": q_norm_g, "k_norm_g": k_norm_g, "w_out": w_out}


def reference(x, c, positions, w_ada, b_ada, norm_g, w_in,
              log_dt, lam_re, lam_im, b_re, b_im, c_re, c_im, d_skip, w_glu, b_glu,
              q_a_g, w_q_b, kv_a_g, w_kv_b, q_norm_g, k_norm_g, w_out):
    cos, sin = rope_tables(positions)
    c_act = jax.nn.silu(c)
    o1 = SSM_WIDTH
    o2 = o1 + SSM_WIDTH
    o3 = o2 + Q_LORA
    o4 = o3 + KV_LORA
    o5 = o4 + QK_ROPE
    for l in range(DEPTH):
        mod = c_act @ w_ada[l] + b_ada[l]
        shift, scale, gate = jnp.split(mod, 3, axis=-1)
        h = rms_norm(x, norm_g[l]) * (1.0 + scale[:, None, :]) + shift[:, None, :]
        proj = h @ w_in[l]
        u_ssm = proj[..., :o1]
        z_ssm = proj[..., o1:o2]
        c_q = proj[..., o2:o3]
        c_kv = proj[..., o3:o4]
        k_r = proj[..., o4:o5]
        z_mla = proj[..., o5:]
        y_ssm = s5_mixer(u_ssm, log_dt[l], lam_re[l], lam_im[l], b_re[l], b_im[l],
                         c_re[l], c_im[l], d_skip[l], w_glu[l], b_glu[l]) * jax.nn.silu(z_ssm)
        y_mla = mla_mixer(c_q, c_kv, k_r, cos, sin, q_a_g[l], w_q_b[l], kv_a_g[l], w_kv_b[l],
                          q_norm_g[l], k_norm_g[l]) * jax.nn.silu(z_mla)
        y = jnp.concatenate([y_ssm, y_mla], axis=-1) @ w_out[l]
        x = x + gate[:, None, :] * y
    return x
```

```python
import functools
import math

import numpy as np
import jax
import jax.numpy as jnp
from jax import lax
from jax.experimental import pallas as pl
from jax.experimental.pallas import tpu as pltpu

F32 = jnp.float32
BF16 = jnp.bfloat16
HIGHEST = lax.Precision.HIGHEST

D_MODEL = 1024
SSM_WIDTH = 512
SSM_GROUP = 16
SSM_GROUPS = 32
SSM_STATE = 64
MLA_HEADS = 8
QK_NOPE = 64
QK_ROPE = 32
QK_HEAD = QK_NOPE + QK_ROPE
V_HEAD = 64
MLA_WIDTH = MLA_HEADS * V_HEAD
Q_LORA = 384
KV_LORA = 256
ROPE_THETA = 10000.0
EPS = 1e-6
NEG_BIG = -1e30

LANES = 128
CHUNK = 16
SUPER = 4
ROPE_HALF = QK_ROPE // 2

_NT = (((1,), (1,)), ((), ()))
_TN = (((0,), (0,)), ((), ()))


def _silu(v):
    return v * jax.nn.sigmoid(v)


def _rms(v, gain):
    return v * lax.rsqrt(jnp.mean(v * v, axis=-1, keepdims=True) + EPS) * gain


def _mod_kernel(c_ref, w_ref, b_ref, o_ref):
    act = _silu(c_ref[...])
    o_ref[...] = jnp.dot(act, w_ref[...], preferred_element_type=F32, precision=HIGHEST) + b_ref[...]


def _modulation(c, w, b):
    n_b, d = c.shape
    rows = 8
    c_pad = jnp.zeros((rows, d), F32).at[:n_b].set(c)
    tn = 512
    out = pl.pallas_call(
        _mod_kernel,
        out_shape=jax.ShapeDtypeStruct((rows, 3 * d), F32),
        grid=(3 * d // tn,),
        in_specs=[pl.BlockSpec((rows, d), lambda n: (0, 0)),
                  pl.BlockSpec((d, tn), lambda n: (0, n)),
                  pl.BlockSpec((1, tn), lambda n: (0, n))],
        out_specs=pl.BlockSpec((rows, tn), lambda n: (0, n)),
        name="adaln_modulation",
    )(c_pad, w, b.reshape(1, -1))
    return out[:n_b]


def _rope_kernel(pos_ref, freq_ref, cos_ref, sin_ref):
    ang = freq_ref[...] * pos_ref[...].astype(F32)
    cos_ref[...] = jnp.cos(ang)
    sin_ref[...] = jnp.sin(ang)


def _rope_tables(positions):
    t = positions.size
    tt = min(t, 2048)
    inv_freq = (ROPE_THETA ** (-np.arange(ROPE_HALF, dtype=np.float64) * 2.0 / QK_ROPE)).astype(np.float32)
    cos_t, sin_t = pl.pallas_call(
        _rope_kernel,
        out_shape=(jax.ShapeDtypeStruct((ROPE_HALF, t), F32),) * 2,
        grid=(t // tt,),
        in_specs=[pl.BlockSpec((1, tt), lambda i: (0, i)),
                  pl.BlockSpec((ROPE_HALF, 1), lambda i: (0, 0))],
        out_specs=(pl.BlockSpec((ROPE_HALF, tt), lambda i: (0, i)),) * 2,
        name="rope_tables",
    )(positions.reshape(1, t), jnp.asarray(inv_freq).reshape(ROPE_HALF, 1))
    cos, sin = cos_t.T, sin_t.T
    one = jnp.ones((t, LANES // 2 - ROPE_HALF), F32)
    zero = jnp.zeros_like(one)
    return (jnp.concatenate([cos, one, cos, one], axis=1),
            jnp.concatenate([-sin, zero, sin, zero], axis=1))


def _slab_source():
    src = np.full((LANES,), -1, np.int64)
    src[0:16] = QK_NOPE + np.arange(16)
    src[16:64] = np.arange(48)
    src[64:80] = QK_NOPE + 16 + np.arange(16)
    src[80:96] = 48 + np.arange(16)
    return src


def _to_slabs(w, per_head, src):
    k = w.shape[0]
    w3 = w.reshape(k, MLA_HEADS, per_head)
    valid = src >= 0
    cols = jnp.where(valid[None, None, :], w3[:, :, np.clip(src, 0, per_head - 1)], 0.0)
    return cols.reshape(k, MLA_HEADS * LANES)


def _mla_proj_kernel(x_ref, mod_ref, ng_ref, w1_ref, qag_ref, kvag_ref, wq_ref, wk_ref, wv_ref,
                     gq_ref, gk_ref, ones_ref, cos_ref, sin_ref, q_ref, k_ref, v_ref, zs_ref):
    d = D_MODEL
    x = x_ref[...]
    mod = mod_ref[...]
    shift, scale = mod[:, :d], mod[:, d:2 * d]
    h = (_rms(x, ng_ref[...]) * (1.0 + scale) + shift).astype(BF16)
    p1 = jnp.dot(h, w1_ref[...], preferred_element_type=F32)
    o1, o2, o3 = Q_LORA, Q_LORA + KV_LORA, Q_LORA + KV_LORA + MLA_WIDTH
    cqn = _rms(p1[:, :o1], qag_ref[...]).astype(BF16)
    ckvn = _rms(p1[:, o1:o2], kvag_ref[...]).astype(BF16)
    z = p1[:, o2:o3]
    kr = p1[:, o3:o3 + LANES]
    zs_ref[...] = _silu(z).astype(BF16)
    q = jnp.dot(cqn, wq_ref[...], preferred_element_type=F32)
    kn = jnp.dot(ckvn, wk_ref[...], preferred_element_type=F32)
    v = jnp.dot(ckvn, wv_ref[...], preferred_element_type=F32)
    cos, sin = cos_ref[...], sin_ref[...]
    ones = ones_ref[...]
    kr2 = jnp.concatenate([kr, kr], axis=1)
    inv_dim = 1.0 / QK_HEAD
    for pair in range(MLA_HEADS // 2):
        sl = slice(2 * LANES * pair, 2 * LANES * (pair + 1))
        qp = q[:, sl]
        kp = kn[:, sl] + kr2
        ssq = jnp.dot((qp * qp).astype(BF16), ones, preferred_element_type=F32)
        ssk = jnp.dot((kp * kp).astype(BF16), ones, preferred_element_type=F32)
        qn = qp * lax.rsqrt(ssq * inv_dim + EPS) * gq_ref[:, sl]
        kk = kp * lax.rsqrt(ssk * inv_dim + EPS) * gk_ref[:, sl]
        for e in range(2):
            head = 2 * pair + e
            s2 = slice(LANES * e, LANES * (e + 1))
            qh, kh = qn[:, s2], kk[:, s2]
            q_ref[head] = (qh * cos + pltpu.roll(qh, LANES // 2, 1) * sin).astype(BF16)
            k_ref[head] = (kh * cos + pltpu.roll(kh, LANES // 2, 1) * sin).astype(BF16)
    for head in range(MLA_HEADS):
        v_ref[head] = v[:, V_HEAD * head:V_HEAD * (head + 1)].astype(BF16)


def _mla_projection(x2, mod3, norm_g, w_in, q_a_g, w_q_b, kv_a_g, w_kv_b, q_norm_g, k_norm_g, cos_tab, sin_tab,
                    n_b, seq):
    t, d = x2.shape
    tm = min(256, seq)
    per_b = seq // tm
    o2 = 2 * SSM_WIDTH
    o3, o4, o5 = o2 + Q_LORA, o2 + Q_LORA + KV_LORA, o2 + Q_LORA + KV_LORA + QK_ROPE
    src = _slab_source()
    rope_src = np.where(src >= QK_NOPE, src - QK_NOPE, -1)
    kr_cols = jnp.where((rope_src >= 0)[None, :], w_in[:, o4:o5][:, np.clip(rope_src, 0, QK_ROPE - 1)], 0.0)
    w1 = jnp.concatenate([w_in[:, o2:o3], w_in[:, o3:o4], w_in[:, o5:], kr_cols], axis=1).astype(BF16)
    wq = _to_slabs(w_q_b, QK_HEAD, src).astype(BF16)
    nope_src = np.where((src >= 0) & (src < QK_NOPE), src, -1)
    wk = _to_slabs(w_kv_b, QK_NOPE + V_HEAD, nope_src).astype(BF16)
    wv = w_kv_b.reshape(KV_LORA, MLA_HEADS, QK_NOPE + V_HEAD)[:, :, QK_NOPE:].reshape(KV_LORA, MLA_WIDTH).astype(BF16)
    valid = (src >= 0)
    gq = jnp.tile(jnp.where(valid, q_norm_g[np.clip(src, 0, QK_HEAD - 1)], 0.0) * (1.0 / math.sqrt(QK_HEAD)),
                  MLA_HEADS).reshape(1, -1)
    gk = jnp.tile(jnp.where(valid, k_norm_g[np.clip(src, 0, QK_HEAD - 1)], 0.0), MLA_HEADS).reshape(1, -1)
    blk = np.kron(np.eye(2), np.ones((LANES, LANES)))
    ones = jnp.asarray(blk, BF16)
    hw = MLA_HEADS * LANES
    const = lambda shape: pl.BlockSpec(shape, lambda i: (0,) * len(shape))
    head_spec = lambda w: pl.BlockSpec((None, MLA_HEADS, tm, w), lambda i: (i // per_b, 0, i % per_b, 0))
    return pl.pallas_call(
        _mla_proj_kernel,
        out_shape=(jax.ShapeDtypeStruct((n_b, MLA_HEADS, seq, LANES), BF16),
                   jax.ShapeDtypeStruct((n_b, MLA_HEADS, seq, LANES), BF16),
                   jax.ShapeDtypeStruct((n_b, MLA_HEADS, seq, V_HEAD), BF16),
                   jax.ShapeDtypeStruct((t, MLA_WIDTH), BF16)),
        grid=(t // tm,),
        in_specs=[pl.BlockSpec((tm, d), lambda i: (i, 0)),
                  pl.BlockSpec((None, 1, 3 * d), lambda i: (i // per_b, 0, 0)),
                  const((1, d)), const(w1.shape), const((1, Q_LORA)), const((1, KV_LORA)),
                  const(wq.shape), const(wk.shape), const(wv.shape), const((1, hw)), const((1, hw)),
                  const(ones.shape),
                  pl.BlockSpec((tm, LANES), lambda i: (i, 0)),
                  pl.BlockSpec((tm, LANES), lambda i: (i, 0))],
        out_specs=(head_spec(LANES), head_spec(LANES), head_spec(V_HEAD),
                   pl.BlockSpec((tm, MLA_WIDTH), lambda i: (i, 0))),
        compiler_params=pltpu.CompilerParams(dimension_semantics=("arbitrary",),
                                             vmem_limit_bytes=48 * 1024 * 1024),
        name="mla_projection",
    )(x2, mod3, norm_g.reshape(1, d), w1, q_a_g.reshape(1, -1), kv_a_g.reshape(1, -1), wq, wk, wv, gq, gk, ones,
      cos_tab, sin_tab)


def _attn_kernel(q_ref, k_ref, v_ref, o_ref, *, tq):
    qi = pl.program_id(2)
    q = q_ref[...]

    def step(j, carry, masked):
        m, l, acc = carry
        start = pl.multiple_of(j * tq, tq)
        k = k_ref[pl.ds(start, tq), :]
        v = v_ref[pl.ds(start, tq), :]
        s = lax.dot_general(q, k, _NT, preferred_element_type=F32)
        if masked:
            row = lax.broadcasted_iota(jnp.int32, s.shape, 0)
            col = lax.broadcasted_iota(jnp.int32, s.shape, 1)
            s = jnp.where(col <= row, s, NEG_BIG)
        m_new = jnp.maximum(m, jnp.max(s, axis=-1, keepdims=True))
        p = jnp.exp(s - m_new)
        alpha = jnp.exp(m - m_new)
        l = alpha * l + jnp.sum(p, axis=-1, keepdims=True)
        acc = alpha * acc + jnp.dot(p.astype(BF16), v, preferred_element_type=F32)
        return m_new, l, acc

    init = (jnp.full((tq, 1), NEG_BIG, F32), jnp.zeros((tq, 1), F32), jnp.zeros((tq, V_HEAD), F32))
    carry = lax.fori_loop(0, qi, lambda j, c: step(j, c, False), init)
    _, l, acc = step(qi, carry, True)
    o_ref[...] = (acc / l).astype(o_ref.dtype)


def _attention(q, k, v):
    n_b, n_h, seq, _ = q.shape
    tq = min(256, seq)
    return pl.pallas_call(
        functools.partial(_attn_kernel, tq=tq),
        out_shape=jax.ShapeDtypeStruct((n_b, n_h, seq, V_HEAD), BF16),
        grid=(n_b, n_h, seq // tq),
        in_specs=[pl.BlockSpec((None, None, tq, LANES), lambda b, h, i: (b, h, i, 0)),
                  pl.BlockSpec((None, None, seq, LANES), lambda b, h, i: (b, h, 0, 0)),
                  pl.BlockSpec((None, None, seq, V_HEAD), lambda b, h, i: (b, h, 0, 0))],
        out_specs=pl.BlockSpec((None, None, tq, V_HEAD), lambda b, h, i: (b, h, i, 0)),
        compiler_params=pltpu.CompilerParams(dimension_semantics=("arbitrary", "arbitrary", "arbitrary"),
                                             vmem_limit_bytes=48 * 1024 * 1024),
        name="causal_attention",
    )(q, k, v)


def _s5_coef_kernel(lre_ref, lim_ref, ldt_ref, bre_ref, bim_ref, cre_ref, cim_ref, d_ref,
                    kmat_ref, winre_ref, winim_ref, cpre_ref, cpim_ref, apre_ref, apim_ref, *, n_top):
    lre, lim = lre_ref[...], lim_ref[...]
    dt = jnp.exp(ldt_ref[...])
    zr, zi = lre * dt, lim * dt

    def power(kk):
        mag = jnp.exp(kk * zr)
        return mag * jnp.cos(kk * zi), mag * jnp.sin(kk * zi)

    lb_re, lb_im = power(1.0)
    nr, ni = lb_re - 1.0, lb_im
    den = lre * lre + lim * lim
    f_re = (nr * lre + ni * lim) / den
    f_im = (ni * lre - nr * lim) / den
    bre, bim = bre_ref[...], bim_ref[...]
    bb_re = f_re * bre - f_im * bim
    bb_im = f_re * bim + f_im * bre
    cre, cim = cre_ref[...], cim_ref[...]
    cp_re, cp_im = [], []
    for kk in range(CHUNK + 1):
        pr, pi = power(float(kk))
        cr, ci = cre * pr - cim * pi, cre * pi + cim * pr
        cpre_ref[kk] = cr
        cpim_ref[kk] = ci
        cp_re.append(cr)
        cp_im.append(ci)
    all_re = jnp.concatenate(cp_re[:CHUNK], axis=0)
    all_im = jnp.concatenate(cp_im[:CHUNK], axis=0)
    kmat = (lax.dot_general(all_re, bb_re, _NT, preferred_element_type=F32, precision=HIGHEST)
            - lax.dot_general(all_im, bb_im, _NT, preferred_element_type=F32, precision=HIGHEST))
    h = SSM_GROUP
    eye = lax.broadcasted_iota(jnp.int32, (h, h), 0) == lax.broadcasted_iota(jnp.int32, (h, h), 1)
    kmat_ref[0:h, :] = kmat[0:h] + jnp.where(eye, d_ref[...], 0.0)
    kmat_ref[h:, :] = kmat[h:]
    for j in range(CHUNK):
        pr, pi = power(float(CHUNK - 1 - j))
        winre_ref[j] = pr * bb_re - pi * bb_im
        winim_ref[j] = pr * bb_im + pi * bb_re
    steps = [float(CHUNK * m) for m in range(SUPER)] + [float(CHUNK * SUPER * (1 << i)) for i in range(n_top)]
    for i, kk in enumerate(steps):
        pr, pi = power(kk)
        apre_ref[i:i + 1, :] = pr
        apim_ref[i:i + 1, :] = pi


def _s5_coefficients(log_dt, lam_re, lam_im, b_re, b_im, c_re, c_im, d_skip, n_top):
    g, p, h = SSM_GROUPS, SSM_STATE, SSM_GROUP
    n_ap = SUPER + n_top
    row = lambda a: a.reshape(g, 1, p)
    grp = lambda *shape: pl.BlockSpec((None,) + shape, lambda i: (i,) + (0,) * len(shape))
    outs = pl.pallas_call(
        functools.partial(_s5_coef_kernel, n_top=n_top),
        out_shape=(jax.ShapeDtypeStruct((g, CHUNK * h, h), F32),
                   jax.ShapeDtypeStruct((g, CHUNK, h, p), F32), jax.ShapeDtypeStruct((g, CHUNK, h, p), F32),
                   jax.ShapeDtypeStruct((g, CHUNK + 1, h, p), F32), jax.ShapeDtypeStruct((g, CHUNK + 1, h, p), F32),
                   jax.ShapeDtypeStruct((g, n_ap, p), F32), jax.ShapeDtypeStruct((g, n_ap, p), F32)),
        grid=(g,),
        in_specs=[grp(1, p), grp(1, p), grp(1, p), grp(h, p), grp(h, p), grp(h, p), grp(h, p), grp(h, 1)],
        out_specs=(grp(CHUNK * h, h), grp(CHUNK, h, p), grp(CHUNK, h, p), grp(CHUNK + 1, h, p),
                   grp(CHUNK + 1, h, p), grp(n_ap, p), grp(n_ap, p)),
        name="s5_coefficients",
    )(row(lam_re), row(lam_im), jnp.broadcast_to(log_dt[:, None, None], (g, 1, p)),
      b_re.transpose(0, 2, 1), b_im.transpose(0, 2, 1), c_re, c_im, d_skip.reshape(g, h, 1))
    kmat, win_re, win_im, cp_re, cp_im, ap_re, ap_im = outs
    lag = np.arange(CHUNK)[:, None] - np.arange(CHUNK)[None, :]
    toe = kmat.reshape(g, CHUNK, h, h)[:, np.clip(lag, 0, None)]
    toe = jnp.where((lag >= 0)[None, :, :, None, None], toe, 0.0)
    m_t = toe.transpose(0, 1, 3, 2, 4).reshape(g, CHUNK * h, CHUNK * h)
    win = jnp.concatenate([win_re.transpose(0, 3, 1, 2).reshape(g, p, CHUNK * h),
                           win_im.transpose(0, 3, 1, 2).reshape(g, p, CHUNK * h)], axis=1)
    lhs = jnp.concatenate([m_t, win], axis=1).astype(BF16)
    wout = jnp.concatenate([cp_re[:, 1:], -cp_im[:, 1:]], axis=-1).reshape(g, CHUNK * h, 2 * p).astype(BF16)
    ap = jnp.concatenate([ap_re, ap_im], axis=-1)
    ap = jnp.broadcast_to(ap[..., None], (g, n_ap, 2 * p, LANES))
    return lhs, wout, ap


def _ssm_proj_kernel(x_ref, mod_ref, ng_ref, w_ref, u_ref, z_ref, *, n_b):
    d = D_MODEL
    hs = []
    for b in range(n_b):
        mod = mod_ref[b]
        shift, scale = mod[:, :d], mod[:, d:2 * d]
        hs.append((_rms(x_ref[b], ng_ref[...]) * (1.0 + scale) + shift).astype(BF16))
    h = jnp.concatenate(hs, axis=0)
    r = lax.dot_general(w_ref[...], h, _NT, preferred_element_type=F32)
    u_ref[...] = r[:SSM_WIDTH].astype(BF16)
    z_ref[...] = _silu(r[SSM_WIDTH:]).astype(BF16)


def _ssm_projection(x5, mod3, norm_g, w_in):
    n_b, c2, _ = x5.shape
    d = D_MODEL
    lanes = n_b * c2
    w_t = w_in[:, :2 * SSM_WIDTH].T.astype(BF16)
    out = jax.ShapeDtypeStruct((CHUNK, SSM_WIDTH, SUPER * lanes), BF16)
    out_spec = pl.BlockSpec((None, SSM_WIDTH, lanes), lambda j2, j1: (j1, 0, j2))
    return pl.pallas_call(
        functools.partial(_ssm_proj_kernel, n_b=n_b),
        out_shape=(out, out),
        grid=(SUPER, CHUNK),
        in_specs=[pl.BlockSpec((n_b, c2, d), lambda j2, j1: (0, 0, j2 * CHUNK + j1)),
                  pl.BlockSpec((n_b, 1, 3 * d), lambda j2, j1: (0, 0, 0)),
                  pl.BlockSpec((1, d), lambda j2, j1: (0, 0)),
                  pl.BlockSpec(w_t.shape, lambda j2, j1: (0, 0))],
        out_specs=(out_spec, out_spec),
        compiler_params=pltpu.CompilerParams(dimension_semantics=("arbitrary", "arbitrary")),
        name="ssm_projection",
    )(x5, mod3, norm_g.reshape(1, d), w_t)


def _cmul(ar, ai, xr, xi):
    return ar * xr - ai * xi, ar * xi + ai * xr


def _s5_group_kernel(x_ref, lhs_ref, wout_ref, ap_ref, y_ref, *, n_b, c2, n_top):
    p = SSM_STATE
    rows = CHUNK * SSM_GROUP
    n = SUPER * n_b * c2
    x = x_ref[...].reshape(rows, n)
    r = jnp.dot(lhs_ref[...], x, preferred_element_type=F32)
    y_intra, c_re, c_im = r[:rows], r[rows:rows + p], r[rows + p:]

    def mult(i):
        return ap_ref[i, :p, :c2], ap_ref[i, p:, :c2]

    lane = lax.broadcasted_iota(jnp.int32, (p, c2), 1)

    def shifted(a, sh):
        return jnp.where(lane >= sh, pltpu.roll(a, sh, 1), 0.0)

    pieces = [[None] * n_b for _ in range(SUPER)]
    for b in range(n_b):
        piece = lambda a, j2: a[:, (j2 * n_b + b) * c2:(j2 * n_b + b + 1) * c2]
        a_re, a_im = mult(1)
        e_re = jnp.zeros((p, c2), F32)
        e_im = jnp.zeros((p, c2), F32)
        local = []
        for j2 in range(SUPER):
            local.append((e_re, e_im))
            t_re, t_im = _cmul(a_re, a_im, e_re, e_im)
            e_re, e_im = t_re + piece(c_re, j2), t_im + piece(c_im, j2)
        for i in range(n_top):
            m_re, m_im = mult(SUPER + i)
            t_re, t_im = _cmul(m_re, m_im, shifted(e_re, 1 << i), shifted(e_im, 1 << i))
            e_re, e_im = e_re + t_re, e_im + t_im
        s_re, s_im = shifted(e_re, 1), shifted(e_im, 1)
        for j2 in range(SUPER):
            if j2 == 0:
                f_re, f_im = s_re, s_im
            else:
                m_re, m_im = mult(j2)
                t_re, t_im = _cmul(m_re, m_im, s_re, s_im)
                f_re, f_im = local[j2][0] + t_re, local[j2][1] + t_im
            pieces[j2][b] = jnp.concatenate([f_re, f_im], axis=0)
    s_in = jnp.concatenate([pieces[j2][b] for j2 in range(SUPER) for b in range(n_b)], axis=1)
    y = y_intra + jnp.dot(wout_ref[...], s_in.astype(BF16), preferred_element_type=F32)
    y_ref[...] = y.reshape(CHUNK, SSM_GROUP, n).astype(y_ref.dtype)


def _s5_scan(u_t, lhs, wout, ap, n_b, c2, n_top):
    g, h = SSM_GROUPS, SSM_GROUP
    n = u_t.shape[-1]
    grp = lambda *shape: pl.BlockSpec((None,) + shape, lambda i: (i,) + (0,) * len(shape))
    return pl.pallas_call(
        functools.partial(_s5_group_kernel, n_b=n_b, c2=c2, n_top=n_top),
        out_shape=jax.ShapeDtypeStruct((g, CHUNK, h, n), BF16),
        grid=(g,),
        in_specs=[pl.BlockSpec((CHUNK, h, n), lambda i: (0, i, 0)),
                  grp(*lhs.shape[1:]), grp(*wout.shape[1:]), grp(*ap.shape[1:])],
        out_specs=grp(CHUNK, h, n),
        compiler_params=pltpu.CompilerParams(dimension_semantics=("arbitrary",),
                                             vmem_limit_bytes=48 * 1024 * 1024),
        name="s5_chunk_scan",
    )(u_t, lhs, wout, ap)


def _glu_kernel(y_ref, zs_ref, w_ref, b_ref, o_ref):
    g, h, n = y_ref.shape
    y = jax.nn.gelu(y_ref[...].reshape(g * h, n).astype(F32))
    t = jnp.dot(w_ref[...], y.astype(BF16), preferred_element_type=F32) + b_ref[...]
    o_ref[...] = (y * jax.nn.sigmoid(t) * zs_ref[...].astype(F32)).astype(o_ref.dtype)


def _glu(y_t, zs_t, w_glu, b_glu):
    g, _, h, n = y_t.shape
    tn = min(n, 512)
    w_t = w_glu.T.astype(BF16)
    return pl.pallas_call(
        _glu_kernel,
        out_shape=jax.ShapeDtypeStruct((CHUNK, SSM_WIDTH, n), BF16),
        grid=(CHUNK, n // tn),
        in_specs=[pl.BlockSpec((g, None, h, tn), lambda j, i: (0, j, 0, i)),
                  pl.BlockSpec((None, SSM_WIDTH, tn), lambda j, i: (j, 0, i)),
                  pl.BlockSpec(w_t.shape, lambda j, i: (0, 0)),
                  pl.BlockSpec((SSM_WIDTH, 1), lambda j, i: (0, 0))],
        out_specs=pl.BlockSpec((None, SSM_WIDTH, tn), lambda j, i: (j, 0, i)),
        compiler_params=pltpu.CompilerParams(dimension_semantics=("arbitrary", "arbitrary")),
        name="s5_glu",
    )(y_t, zs_t, w_t, b_glu.reshape(SSM_WIDTH, 1))


def _out_proj_kernel(yg_ref, ym_ref, zm_ref, x_ref, mod_ref, wa_ref, wb_ref, o_ref, *, n_b, c2):
    d = D_MODEL
    a = lax.dot_general(yg_ref[...], wa_ref[...], _TN, preferred_element_type=F32)
    for b in range(n_b):
        gated = (ym_ref[b].astype(F32) * zm_ref[b].astype(F32)).astype(BF16)
        y = a[b * c2:(b + 1) * c2] + jnp.dot(gated, wb_ref[...], preferred_element_type=F32)
        gate = mod_ref[b][:, 2 * d:]
        o_ref[b] = x_ref[b] + gate * y


def _out_projection(yg_t, ym5, zm5, x5, mod3, w_out):
    n_b, c2, _ = x5.shape
    d = D_MODEL
    lanes = n_b * c2
    wa = w_out[:SSM_WIDTH].astype(BF16)
    wb = w_out[SSM_WIDTH:].astype(BF16)
    tok = lambda w: pl.BlockSpec((n_b, c2, w), lambda j2, j1: (0, 0, j2 * CHUNK + j1))
    return pl.pallas_call(
        functools.partial(_out_proj_kernel, n_b=n_b, c2=c2),
        out_shape=jax.ShapeDtypeStruct(x5.shape, F32),
        grid=(SUPER, CHUNK),
        in_specs=[pl.BlockSpec((None, SSM_WIDTH, lanes), lambda j2, j1: (j1, 0, j2)),
                  tok(MLA_WIDTH), tok(MLA_WIDTH), tok(d),
                  pl.BlockSpec((n_b, 1, 3 * d), lambda j2, j1: (0, 0, 0)),
                  pl.BlockSpec(wa.shape, lambda j2, j1: (0, 0)),
                  pl.BlockSpec(wb.shape, lambda j2, j1: (0, 0))],
        out_specs=tok(d),
        compiler_params=pltpu.CompilerParams(dimension_semantics=("arbitrary", "arbitrary")),
        name="output_projection",
    )(yg_t, ym5, zm5, x5, mod3, wa, wb)


def kernel(x, c, positions, w_ada, b_ada, norm_g, w_in, log_dt, lam_re, lam_im, b_re, b_im, c_re, c_im, d_skip,
           w_glu, b_glu, q_a_g, w_q_b, kv_a_g, w_kv_b, q_norm_g, k_norm_g, w_out):
    n_b, seq, d = x.shape
    depth = w_ada.shape[0]
    c2 = seq // (CHUNK * SUPER)
    n_top = max(int(math.log2(c2)), 0)
    assert c2 * CHUNK * SUPER == seq and (1 << n_top) == c2
    cos_tab, sin_tab = _rope_tables(positions)
    for l in range(depth):
        mod3 = _modulation(c, w_ada[l], b_ada[l]).reshape(n_b, 1, 3 * d)
        x5 = x.reshape(n_b, c2, SUPER * CHUNK * d)
        lhs, wout, ap = _s5_coefficients(log_dt[l], lam_re[l], lam_im[l], b_re[l], b_im[l], c_re[l], c_im[l],
                                         d_skip[l], n_top)
        u_t, zs_t = _ssm_projection(x5, mod3, norm_g[l], w_in[l])
        y_t = _s5_scan(u_t, lhs, wout, ap, n_b, c2, n_top)
        yg_t = _glu(y_t, zs_t, w_glu[l], b_glu[l])
        q, k, v, zm = _mla_projection(x.reshape(n_b * seq, d), mod3, norm_g[l], w_in[l], q_a_g[l], w_q_b[l],
                                      kv_a_g[l], w_kv_b[l], q_norm_g[l], k_norm_g[l], cos_tab, sin_tab, n_b, seq)
        o = _attention(q, k, v)
        ym = o.transpose(0, 2, 1, 3).reshape(n_b, c2, SUPER * CHUNK * MLA_WIDTH)
        zm5 = zm.reshape(n_b, c2, SUPER * CHUNK * MLA_WIDTH)
        x = _out_projection(yg_t, ym, zm5, x5, mod3, w_out[l]).reshape(n_b, seq, d)
    return x
```

```python
import functools
import math

import numpy as np
import jax
import jax.numpy as jnp
from jax import lax
from jax.experimental import pallas as pl
from jax.experimental.pallas import tpu as pltpu

F32 = jnp.float32
BF16 = jnp.bfloat16
HIGHEST = lax.Precision.HIGHEST

D_MODEL = 1024
SSM_WIDTH = 512
SSM_GROUP = 16
SSM_GROUPS = 32
SSM_STATE = 64
MLA_HEADS = 8
QK_NOPE = 64
QK_ROPE = 32
QK_HEAD = QK_NOPE + QK_ROPE
V_HEAD = 64
MLA_WIDTH = MLA_HEADS * V_HEAD
Q_LORA = 384
KV_LORA = 256
ROPE_THETA = 10000.0
EPS = 1e-6
NEG_BIG = -1e30

LANES = 128
CHUNK = 16
SUPER = 4
ROPE_HALF = QK_ROPE // 2

_NT = (((1,), (1,)), ((), ()))
_TN = (((0,), (0,)), ((), ()))


def _silu(v):
    return v * jax.nn.sigmoid(v)


def _rms(v, gain):
    return v * lax.rsqrt(jnp.mean(v * v, axis=-1, keepdims=True) + EPS) * gain


def _mod_kernel(c_ref, w_ref, b_ref, o_ref):
    act = _silu(c_ref[...])
    o_ref[...] = jnp.dot(act, w_ref[...], preferred_element_type=F32, precision=HIGHEST) + b_ref[...]


def _modulation(c, w, b):
    n_b, d = c.shape
    rows = 8
    c_pad = jnp.zeros((rows, d), F32).at[:n_b].set(c)
    tn = 512
    out = pl.pallas_call(
        _mod_kernel,
        out_shape=jax.ShapeDtypeStruct((rows, 3 * d), F32),
        grid=(3 * d // tn,),
        in_specs=[pl.BlockSpec((rows, d), lambda n: (0, 0)),
                  pl.BlockSpec((d, tn), lambda n: (0, n)),
                  pl.BlockSpec((1, tn), lambda n: (0, n))],
        out_specs=pl.BlockSpec((rows, tn), lambda n: (0, n)),
        name="adaln_modulation",
    )(c_pad, w, b.reshape(1, -1))
    return out[:n_b]


def _rope_kernel(pos_ref, freq_ref, cos_ref, sin_ref):
    ang = freq_ref[...] * pos_ref[...].astype(F32)
    cos_ref[...] = jnp.cos(ang)
    sin_ref[...] = jnp.sin(ang)


def _rope_tables(positions):
    t = positions.size
    tt = min(t, 2048)
    inv_freq = (ROPE_THETA ** (-np.arange(ROPE_HALF, dtype=np.float64) * 2.0 / QK_ROPE)).astype(np.float32)
    cos_t, sin_t = pl.pallas_call(
        _rope_kernel,
        out_shape=(jax.ShapeDtypeStruct((ROPE_HALF, t), F32),) * 2,
        grid=(t // tt,),
        in_specs=[pl.BlockSpec((1, tt), lambda i: (0, i)),
                  pl.BlockSpec((ROPE_HALF, 1), lambda i: (0, 0))],
        out_specs=(pl.BlockSpec((ROPE_HALF, tt), lambda i: (0, i)),) * 2,
        name="rope_tables",
    )(positions.reshape(1, t), jnp.asarray(inv_freq).reshape(ROPE_HALF, 1))
    cos, sin = cos_t.T, sin_t.T
    one = jnp.ones((t, LANES // 2 - ROPE_HALF), F32)
    zero = jnp.zeros_like(one)
    return (cos_t, sin_t, jnp.concatenate([cos, one, cos, one], axis=1),
            jnp.concatenate([-sin, zero, sin, zero], axis=1))


def _slab_source():
    src = np.full((LANES,), -1, np.int64)
    src[0:16] = QK_NOPE + np.arange(16)
    src[16:64] = np.arange(48)
    src[64:80] = QK_NOPE + 16 + np.arange(16)
    src[80:96] = 48 + np.arange(16)
    return src


def _to_slabs(w, per_head, src):
    k = w.shape[0]
    w3 = w.reshape(k, MLA_HEADS, per_head)
    valid = src >= 0
    cols = jnp.where(valid[None, None, :], w3[:, :, np.clip(src, 0, per_head - 1)], 0.0)
    return cols.reshape(k, MLA_HEADS * LANES)


V_ROWS = V_HEAD + 16


def _mla_proj_kernel(x_ref, mod_ref, ng_ref, w1_ref, qag_ref, kvag_ref, wqt_ref, wk_ref, wvt_ref,
                     gq_ref, gk_ref, ones_ref, cos_ref, sin_ref, cost_ref, sint_ref,
                     qt_ref, k_ref, vt_ref, zs_ref):
    d = D_MODEL
    tm = x_ref.shape[0]
    x = x_ref[...]
    mod = mod_ref[...]
    shift, scale = mod[:, :d], mod[:, d:2 * d]
    h = (_rms(x, ng_ref[...]) * (1.0 + scale) + shift).astype(BF16)
    p1 = jnp.dot(h, w1_ref[...], preferred_element_type=F32)
    o1, o2, o3 = Q_LORA, Q_LORA + KV_LORA, Q_LORA + KV_LORA + MLA_WIDTH
    cqn = _rms(p1[:, :o1], qag_ref[...]).astype(BF16)
    ckvn = _rms(p1[:, o1:o2], kvag_ref[...]).astype(BF16)
    zs_ref[...] = _silu(p1[:, o2:o3]).astype(BF16)
    kr = p1[:, o3:o3 + LANES]
    inv_dim = 1.0 / QK_HEAD

    kn = jnp.dot(ckvn, wk_ref[...], preferred_element_type=F32)
    cos, sin = cos_ref[...], sin_ref[...]
    ones = ones_ref[...]
    kr2 = jnp.concatenate([kr, kr], axis=1)
    for pair in range(MLA_HEADS // 2):
        sl = slice(2 * LANES * pair, 2 * LANES * (pair + 1))
        kp = kn[:, sl] + kr2
        ssk = jnp.dot((kp * kp).astype(BF16), ones, preferred_element_type=F32)
        kk = kp * lax.rsqrt(ssk * inv_dim + EPS) * gk_ref[:, sl]
        for e in range(2):
            kh = kk[:, LANES * e:LANES * (e + 1)]
            k_ref[2 * pair + e] = (kh * cos + pltpu.roll(kh, LANES // 2, 1) * sin).astype(BF16)

    qt = lax.dot_general(wqt_ref[...], cqn, _NT, preferred_element_type=F32)
    vt = lax.dot_general(wvt_ref[...], ckvn, _NT, preferred_element_type=F32)
    cos_t, sin_t = cost_ref[...], sint_ref[...]
    gq = jnp.concatenate([gq_ref[...]] * (tm // LANES), axis=1)
    first = lax.broadcasted_iota(jnp.int32, (V_ROWS - V_HEAD, tm), 0) == 0
    ones_rows = jnp.where(first, 1.0, 0.0).astype(BF16)
    r = ROPE_HALF
    for head in range(MLA_HEADS):
        slab = qt[LANES * head:LANES * (head + 1)]
        ss = jnp.sum(slab * slab, axis=0, keepdims=True)
        qn = slab * lax.rsqrt(ss * inv_dim + EPS) * gq
        x1, x2 = qn[0:r], qn[LANES // 2:LANES // 2 + r]
        qt_ref[head] = jnp.concatenate([x1 * cos_t - x2 * sin_t, qn[r:LANES // 2],
                                        x2 * cos_t + x1 * sin_t, qn[LANES // 2 + r:]], axis=0).astype(BF16)
        vt_ref[head, 0:V_HEAD, :] = vt[V_HEAD * head:V_HEAD * (head + 1)].astype(BF16)
        vt_ref[head, V_HEAD:, :] = ones_rows


def _mla_projection(x2, mod3, norm_g, w_in, q_a_g, w_q_b, kv_a_g, w_kv_b, q_norm_g, k_norm_g, rope, n_b, seq):
    t, d = x2.shape
    cos_t, sin_t, cos_tab, sin_tab = rope
    tm = min(256, seq)
    per_b = seq // tm
    o2 = 2 * SSM_WIDTH
    o3, o4, o5 = o2 + Q_LORA, o2 + Q_LORA + KV_LORA, o2 + Q_LORA + KV_LORA + QK_ROPE
    src = _slab_source()
    rope_src = np.where(src >= QK_NOPE, src - QK_NOPE, -1)
    kr_cols = jnp.where((rope_src >= 0)[None, :], w_in[:, o4:o5][:, np.clip(rope_src, 0, QK_ROPE - 1)], 0.0)
    w1 = jnp.concatenate([w_in[:, o2:o3], w_in[:, o3:o4], w_in[:, o5:], kr_cols], axis=1).astype(BF16)
    wqt = _to_slabs(w_q_b, QK_HEAD, src).T.astype(BF16)
    nope_src = np.where((src >= 0) & (src < QK_NOPE), src, -1)
    wk = _to_slabs(w_kv_b, QK_NOPE + V_HEAD, nope_src).astype(BF16)
    wvt = w_kv_b.reshape(KV_LORA, MLA_HEADS, QK_NOPE + V_HEAD)[:, :, QK_NOPE:].reshape(KV_LORA, MLA_WIDTH).T.astype(BF16)
    valid = (src >= 0)
    q_scale = math.log2(math.e) / math.sqrt(QK_HEAD)
    gq = jnp.where(valid, q_norm_g[np.clip(src, 0, QK_HEAD - 1)], 0.0) * q_scale
    gq = jnp.broadcast_to(gq[:, None], (LANES, LANES))
    gk = jnp.tile(jnp.where(valid, k_norm_g[np.clip(src, 0, QK_HEAD - 1)], 0.0), MLA_HEADS).reshape(1, -1)
    ones = jnp.asarray(np.kron(np.eye(2), np.ones((LANES, LANES))), BF16)
    hw = MLA_HEADS * LANES
    const = lambda shape: pl.BlockSpec(shape, lambda i: (0,) * len(shape))
    tok = lambda i: (i // per_b, 0, 0, i % per_b)
    return pl.pallas_call(
        _mla_proj_kernel,
        out_shape=(jax.ShapeDtypeStruct((n_b, MLA_HEADS, LANES, seq), BF16),
                   jax.ShapeDtypeStruct((n_b, MLA_HEADS, seq, LANES), BF16),
                   jax.ShapeDtypeStruct((n_b, MLA_HEADS, V_ROWS, seq), BF16),
                   jax.ShapeDtypeStruct((t, MLA_WIDTH), BF16)),
        grid=(t // tm,),
        in_specs=[pl.BlockSpec((tm, d), lambda i: (i, 0)),
                  pl.BlockSpec((None, 1, 3 * d), lambda i: (i // per_b, 0, 0)),
                  const((1, d)), const(w1.shape), const((1, Q_LORA)), const((1, KV_LORA)),
                  const(wqt.shape), const(wk.shape), const(wvt.shape), const((LANES, LANES)), const((1, hw)),
                  const(ones.shape),
                  pl.BlockSpec((tm, LANES), lambda i: (i, 0)),
                  pl.BlockSpec((tm, LANES), lambda i: (i, 0)),
                  pl.BlockSpec((ROPE_HALF, tm), lambda i: (0, i)),
                  pl.BlockSpec((ROPE_HALF, tm), lambda i: (0, i))],
        out_specs=(pl.BlockSpec((None, MLA_HEADS, LANES, tm), tok),
                   pl.BlockSpec((None, MLA_HEADS, tm, LANES), lambda i: (i // per_b, 0, i % per_b, 0)),
                   pl.BlockSpec((None, MLA_HEADS, V_ROWS, tm), tok),
                   pl.BlockSpec((tm, MLA_WIDTH), lambda i: (i, 0))),
        compiler_params=pltpu.CompilerParams(dimension_semantics=("arbitrary",),
                                             vmem_limit_bytes=48 * 1024 * 1024),
        name="mla_projection",
    )(x2, mod3, norm_g.reshape(1, d), w1, q_a_g.reshape(1, -1), kv_a_g.reshape(1, -1), wqt, wk, wvt, gq, gk, ones,
      cos_tab, sin_tab, cos_t, sin_t)


ATTN_TQ = 512
ATTN_TK = 256
HEADS_PER_STEP = 2


def _attn_kernel(qt_ref, k_ref, vt_ref, zs_ref, o_ref, s_scr):
    tq, tk = ATTN_TQ, ATTN_TK
    assert tq == 2 * tk
    qi = pl.program_id(2)
    q0 = qi * tq

    def scores(slot, t):
        start = pl.multiple_of(t * tk, tk)
        for e in range(HEADS_PER_STEP):
            s_scr[slot, e] = jnp.dot(k_ref[e, pl.ds(start, tk), :], qt_ref[e],
                                     preferred_element_type=F32)

    def process(slot, t, carry, masked):
        start = pl.multiple_of(t * tk, tk)
        out = []
        for e in range(HEADS_PER_STEP):
            m, acc = carry[e]
            s = s_scr[slot, e]
            if masked:
                key = start + lax.broadcasted_iota(jnp.int32, s.shape, 0)
                qry = q0 + lax.broadcasted_iota(jnp.int32, s.shape, 1)
                s = jnp.where(key <= qry, s, NEG_BIG)
            m_new = jnp.maximum(m, jnp.max(s, axis=0, keepdims=True))
            p = jnp.exp2(s - m_new).astype(BF16)
            acc = (jnp.exp2(m - m_new) * acc
                   + jnp.dot(vt_ref[e, :, pl.ds(start, tk)], p, preferred_element_type=F32))
            out.append((m_new, acc))
        return tuple(out)

    def body(i, carry):
        t = 2 * i
        scores(1, t + 1)
        carry = process(0, t, carry, False)
        scores(0, t + 2)
        return process(1, t + 1, carry, False)

    init = tuple((jnp.full((1, tq), NEG_BIG, F32), jnp.zeros((V_ROWS, tq), F32)) for _ in range(HEADS_PER_STEP))
    scores(0, 0)
    carry = lax.fori_loop(0, qi, body, init)
    t = 2 * qi
    scores(1, t + 1)
    carry = process(0, t, carry, True)
    carry = process(1, t + 1, carry, True)
    outs = []
    for e in range(HEADS_PER_STEP):
        acc = carry[e][1]
        outs.append((acc[:V_HEAD] / acc[V_HEAD:V_HEAD + 1]).T)
    o_ref[...] = (jnp.concatenate(outs, axis=1) * zs_ref[...].astype(F32)).astype(o_ref.dtype)


def _attention(qt, k, vt, zs):
    n_b, n_h, seq, _ = k.shape
    tq = ATTN_TQ
    nq = seq // tq
    hp = HEADS_PER_STEP
    width = hp * V_HEAD
    return pl.pallas_call(
        _attn_kernel,
        out_shape=jax.ShapeDtypeStruct(zs.shape, BF16),
        grid=(n_b, n_h // hp, nq),
        in_specs=[pl.BlockSpec((None, hp, LANES, tq), lambda b, h, i: (b, h, 0, i)),
                  pl.BlockSpec((None, hp, seq, LANES), lambda b, h, i: (b, h, 0, 0)),
                  pl.BlockSpec((None, hp, V_ROWS, seq), lambda b, h, i: (b, h, 0, 0)),
                  pl.BlockSpec((tq, width), lambda b, h, i: (b * nq + i, h))],
        out_specs=pl.BlockSpec((tq, width), lambda b, h, i: (b * nq + i, h)),
        scratch_shapes=[pltpu.VMEM((2, hp, ATTN_TK, tq), F32)],
        compiler_params=pltpu.CompilerParams(dimension_semantics=("arbitrary", "arbitrary", "arbitrary"),
                                             vmem_limit_bytes=48 * 1024 * 1024),
        name="causal_attention",
    )(qt, k, vt, zs)


def _s5_coef_kernel(lre_ref, lim_ref, ldt_ref, bre_ref, bim_ref, cre_ref, cim_ref, d_ref,
                    kmat_ref, winre_ref, winim_ref, cpre_ref, cpim_ref, apre_ref, apim_ref, *, n_top):
    lre, lim = lre_ref[...], lim_ref[...]
    dt = jnp.exp(ldt_ref[...])
    zr, zi = lre * dt, lim * dt

    def power(kk):
        mag = jnp.exp(kk * zr)
        return mag * jnp.cos(kk * zi), mag * jnp.sin(kk * zi)

    lb_re, lb_im = power(1.0)
    nr, ni = lb_re - 1.0, lb_im
    den = lre * lre + lim * lim
    f_re = (nr * lre + ni * lim) / den
    f_im = (ni * lre - nr * lim) / den
    bre, bim = bre_ref[...], bim_ref[...]
    bb_re = f_re * bre - f_im * bim
    bb_im = f_re * bim + f_im * bre
    cre, cim = cre_ref[...], cim_ref[...]
    cp_re, cp_im = [], []
    for kk in range(CHUNK + 1):
        pr, pi = power(float(kk))
        cr, ci = cre * pr - cim * pi, cre * pi + cim * pr
        cpre_ref[kk] = cr
        cpim_ref[kk] = ci
        cp_re.append(cr)
        cp_im.append(ci)
    all_re = jnp.concatenate(cp_re[:CHUNK], axis=0)
    all_im = jnp.concatenate(cp_im[:CHUNK], axis=0)
    kmat = (lax.dot_general(all_re, bb_re, _NT, preferred_element_type=F32, precision=HIGHEST)
            - lax.dot_general(all_im, bb_im, _NT, preferred_element_type=F32, precision=HIGHEST))
    h = SSM_GROUP
    eye = lax.broadcasted_iota(jnp.int32, (h, h), 0) == lax.broadcasted_iota(jnp.int32, (h, h), 1)
    kmat_ref[0:h, :] = kmat[0:h] + jnp.where(eye, d_ref[...], 0.0)
    kmat_ref[h:, :] = kmat[h:]
    for j in range(CHUNK):
        pr, pi = power(float(CHUNK - 1 - j))
        winre_ref[j] = pr * bb_re - pi * bb_im
        winim_ref[j] = pr * bb_im + pi * bb_re
    steps = [float(CHUNK * m) for m in range(SUPER)] + [float(CHUNK * SUPER * (1 << i)) for i in range(n_top)]
    for i, kk in enumerate(steps):
        pr, pi = power(kk)
        apre_ref[i:i + 1, :] = pr
        apim_ref[i:i + 1, :] = pi


def _s5_coefficients(log_dt, lam_re, lam_im, b_re, b_im, c_re, c_im, d_skip, n_top):
    g, p, h = SSM_GROUPS, SSM_STATE, SSM_GROUP
    n_ap = SUPER + n_top
    row = lambda a: a.reshape(g, 1, p)
    grp = lambda *shape: pl.BlockSpec((None,) + shape, lambda i: (i,) + (0,) * len(shape))
    outs = pl.pallas_call(
        functools.partial(_s5_coef_kernel, n_top=n_top),
        out_shape=(jax.ShapeDtypeStruct((g, CHUNK * h, h), F32),
                   jax.ShapeDtypeStruct((g, CHUNK, h, p), F32), jax.ShapeDtypeStruct((g, CHUNK, h, p), F32),
                   jax.ShapeDtypeStruct((g, CHUNK + 1, h, p), F32), jax.ShapeDtypeStruct((g, CHUNK + 1, h, p), F32),
                   jax.ShapeDtypeStruct((g, n_ap, p), F32), jax.ShapeDtypeStruct((g, n_ap, p), F32)),
        grid=(g,),
        in_specs=[grp(1, p), grp(1, p), grp(1, p), grp(h, p), grp(h, p), grp(h, p), grp(h, p), grp(h, 1)],
        out_specs=(grp(CHUNK * h, h), grp(CHUNK, h, p), grp(CHUNK, h, p), grp(CHUNK + 1, h, p),
                   grp(CHUNK + 1, h, p), grp(n_ap, p), grp(n_ap, p)),
        name="s5_coefficients",
    )(row(lam_re), row(lam_im), jnp.broadcast_to(log_dt[:, None, None], (g, 1, p)),
      b_re.transpose(0, 2, 1), b_im.transpose(0, 2, 1), c_re, c_im, d_skip.reshape(g, h, 1))
    kmat, win_re, win_im, cp_re, cp_im, ap_re, ap_im = outs
    lag = np.arange(CHUNK)[:, None] - np.arange(CHUNK)[None, :]
    toe = kmat.reshape(g, CHUNK, h, h)[:, np.clip(lag, 0, None)]
    toe = jnp.where((lag >= 0)[None, :, :, None, None], toe, 0.0)
    m_t = toe.transpose(0, 1, 3, 2, 4).reshape(g, CHUNK * h, CHUNK * h)
    win = jnp.concatenate([win_re.transpose(0, 3, 1, 2).reshape(g, p, CHUNK * h),
                           win_im.transpose(0, 3, 1, 2).reshape(g, p, CHUNK * h)], axis=1)
    lhs = jnp.concatenate([m_t, win], axis=1).astype(BF16)
    wout = jnp.concatenate([cp_re[:, 1:], -cp_im[:, 1:]], axis=-1).reshape(g, CHUNK * h, 2 * p).astype(BF16)
    ap = jnp.concatenate([ap_re, ap_im], axis=-1)
    ap = jnp.broadcast_to(ap[..., None], (g, n_ap, 2 * p, LANES))
    return lhs, wout, ap


def _ssm_proj_kernel(x_ref, mod_ref, ng_ref, w_ref, u_ref, z_ref, *, n_b):
    d = D_MODEL
    hs = []
    for b in range(n_b):
        mod = mod_ref[b]
        shift, scale = mod[:, :d], mod[:, d:2 * d]
        hs.append((_rms(x_ref[b], ng_ref[...]) * (1.0 + scale) + shift).astype(BF16))
    h = jnp.concatenate(hs, axis=0)
    r = lax.dot_general(w_ref[...], h, _NT, preferred_element_type=F32)
    u_ref[...] = r[:SSM_WIDTH].astype(BF16)
    z_ref[...] = _silu(r[SSM_WIDTH:]).astype(BF16)


def _ssm_projection(x5, mod3, norm_g, w_in):
    n_b, c2, _ = x5.shape
    d = D_MODEL
    lanes = n_b * c2
    w_t = w_in[:, :2 * SSM_WIDTH].T.astype(BF16)
    out = jax.ShapeDtypeStruct((CHUNK, SSM_WIDTH, SUPER * lanes), BF16)
    out_spec = pl.BlockSpec((None, SSM_WIDTH, lanes), lambda j2, j1: (j1, 0, j2))
    return pl.pallas_call(
        functools.partial(_ssm_proj_kernel, n_b=n_b),
        out_shape=(out, out),
        grid=(SUPER, CHUNK),
        in_specs=[pl.BlockSpec((n_b, c2, d), lambda j2, j1: (0, 0, j2 * CHUNK + j1)),
                  pl.BlockSpec((n_b, 1, 3 * d), lambda j2, j1: (0, 0, 0)),
                  pl.BlockSpec((1, d), lambda j2, j1: (0, 0)),
                  pl.BlockSpec(w_t.shape, lambda j2, j1: (0, 0))],
        out_specs=(out_spec, out_spec),
        compiler_params=pltpu.CompilerParams(dimension_semantics=("arbitrary", "arbitrary")),
        name="ssm_projection",
    )(x5, mod3, norm_g.reshape(1, d), w_t)


def _cmul(ar, ai, xr, xi):
    return ar * xr - ai * xi, ar * xi + ai * xr


def _s5_group_kernel(x_ref, lhs_ref, wout_ref, ap_ref, y_ref, *, n_b, c2, n_top):
    p = SSM_STATE
    rows = CHUNK * SSM_GROUP
    n = SUPER * n_b * c2
    x = x_ref[...].reshape(rows, n)
    r = jnp.dot(lhs_ref[...], x, preferred_element_type=F32)
    y_intra, c_re, c_im = r[:rows], r[rows:rows + p], r[rows + p:]

    def mult(i):
        return ap_ref[i, :p, :c2], ap_ref[i, p:, :c2]

    lane = lax.broadcasted_iota(jnp.int32, (p, c2), 1)

    def shifted(a, sh):
        return jnp.where(lane >= sh, pltpu.roll(a, sh, 1), 0.0)

    pieces = [[None] * n_b for _ in range(SUPER)]
    for b in range(n_b):
        piece = lambda a, j2: a[:, (j2 * n_b + b) * c2:(j2 * n_b + b + 1) * c2]
        a_re, a_im = mult(1)
        e_re = jnp.zeros((p, c2), F32)
        e_im = jnp.zeros((p, c2), F32)
        local = []
        for j2 in range(SUPER):
            local.append((e_re, e_im))
            t_re, t_im = _cmul(a_re, a_im, e_re, e_im)
            e_re, e_im = t_re + piece(c_re, j2), t_im + piece(c_im, j2)
        for i in range(n_top):
            m_re, m_im = mult(SUPER + i)
            t_re, t_im = _cmul(m_re, m_im, shifted(e_re, 1 << i), shifted(e_im, 1 << i))
            e_re, e_im = e_re + t_re, e_im + t_im
        s_re, s_im = shifted(e_re, 1), shifted(e_im, 1)
        for j2 in range(SUPER):
            if j2 == 0:
                f_re, f_im = s_re, s_im
            else:
                m_re, m_im = mult(j2)
                t_re, t_im = _cmul(m_re, m_im, s_re, s_im)
                f_re, f_im = local[j2][0] + t_re, local[j2][1] + t_im
            pieces[j2][b] = jnp.concatenate([f_re, f_im], axis=0)
    s_in = jnp.concatenate([pieces[j2][b] for j2 in range(SUPER) for b in range(n_b)], axis=1)
    y = y_intra + jnp.dot(wout_ref[...], s_in.astype(BF16), preferred_element_type=F32)
    y_ref[...] = y.reshape(CHUNK, SSM_GROUP, n).astype(y_ref.dtype)


def _s5_scan(u_t, lhs, wout, ap, n_b, c2, n_top):
    g, h = SSM_GROUPS, SSM_GROUP
    n = u_t.shape[-1]
    grp = lambda *shape: pl.BlockSpec((None,) + shape, lambda i: (i,) + (0,) * len(shape))
    return pl.pallas_call(
        functools.partial(_s5_group_kernel, n_b=n_b, c2=c2, n_top=n_top),
        out_shape=jax.ShapeDtypeStruct((g, CHUNK, h, n), BF16),
        grid=(g,),
        in_specs=[pl.BlockSpec((CHUNK, h, n), lambda i: (0, i, 0)),
                  grp(*lhs.shape[1:]), grp(*wout.shape[1:]), grp(*ap.shape[1:])],
        out_specs=grp(CHUNK, h, n),
        compiler_params=pltpu.CompilerParams(dimension_semantics=("arbitrary",),
                                             vmem_limit_bytes=48 * 1024 * 1024),
        name="s5_chunk_scan",
    )(u_t, lhs, wout, ap)


def _glu_kernel(y_ref, zs_ref, w_ref, b_ref, o_ref):
    g, h, n = y_ref.shape
    y = jax.nn.gelu(y_ref[...].reshape(g * h, n).astype(F32))
    t = jnp.dot(w_ref[...], y.astype(BF16), preferred_element_type=F32) + b_ref[...]
    o_ref[...] = (y * jax.nn.sigmoid(t) * zs_ref[...].astype(F32)).astype(o_ref.dtype)


def _glu(y_t, zs_t, w_glu, b_glu):
    g, _, h, n = y_t.shape
    tn = min(n, 512)
    w_t = w_glu.T.astype(BF16)
    return pl.pallas_call(
        _glu_kernel,
        out_shape=jax.ShapeDtypeStruct((CHUNK, SSM_WIDTH, n), BF16),
        grid=(CHUNK, n // tn),
        in_specs=[pl.BlockSpec((g, None, h, tn), lambda j, i: (0, j, 0, i)),
                  pl.BlockSpec((None, SSM_WIDTH, tn), lambda j, i: (j, 0, i)),
                  pl.BlockSpec(w_t.shape, lambda j, i: (0, 0)),
                  pl.BlockSpec((SSM_WIDTH, 1), lambda j, i: (0, 0))],
        out_specs=pl.BlockSpec((None, SSM_WIDTH, tn), lambda j, i: (j, 0, i)),
        compiler_params=pltpu.CompilerParams(dimension_semantics=("arbitrary", "arbitrary")),
        name="s5_glu",
    )(y_t, zs_t, w_t, b_glu.reshape(SSM_WIDTH, 1))


def _out_proj_kernel(yg_ref, ym_ref, x_ref, mod_ref, wa_ref, wb_ref, o_ref, *, n_b, c2):
    d = D_MODEL
    a = lax.dot_general(yg_ref[...], wa_ref[...], _TN, preferred_element_type=F32)
    for b in range(n_b):
        y = a[b * c2:(b + 1) * c2] + jnp.dot(ym_ref[b], wb_ref[...], preferred_element_type=F32)
        gate = mod_ref[b][:, 2 * d:]
        o_ref[b] = x_ref[b] + gate * y


def _out_projection(yg_t, ym5, x5, mod3, w_out):
    n_b, c2, _ = x5.shape
    d = D_MODEL
    lanes = n_b * c2
    wa = w_out[:SSM_WIDTH].astype(BF16)
    wb = w_out[SSM_WIDTH:].astype(BF16)
    tok = lambda w: pl.BlockSpec((n_b, c2, w), lambda j2, j1: (0, 0, j2 * CHUNK + j1))
    return pl.pallas_call(
        functools.partial(_out_proj_kernel, n_b=n_b, c2=c2),
        out_shape=jax.ShapeDtypeStruct(x5.shape, F32),
        grid=(SUPER, CHUNK),
        in_specs=[pl.BlockSpec((None, SSM_WIDTH, lanes), lambda j2, j1: (j1, 0, j2)),
                  tok(MLA_WIDTH), tok(d),
                  pl.BlockSpec((n_b, 1, 3 * d), lambda j2, j1: (0, 0, 0)),
                  pl.BlockSpec(wa.shape, lambda j2, j1: (0, 0)),
                  pl.BlockSpec(wb.shape, lambda j2, j1: (0, 0))],
        out_specs=tok(d),
        compiler_params=pltpu.CompilerParams(dimension_semantics=("arbitrary", "arbitrary")),
        name="output_projection",
    )(yg_t, ym5, x5, mod3, wa, wb)


def kernel(x, c, positions, w_ada, b_ada, norm_g, w_in, log_dt, lam_re, lam_im, b_re, b_im, c_re, c_im, d_skip,
           w_glu, b_glu, q_a_g, w_q_b, kv_a_g, w_kv_b, q_norm_g, k_norm_g, w_out):
    n_b, seq, d = x.shape
    depth = w_ada.shape[0]
    c2 = seq // (CHUNK * SUPER)
    n_top = max(int(math.log2(c2)), 0)
    assert c2 * CHUNK * SUPER == seq and (1 << n_top) == c2
    rope = _rope_tables(positions)
    for l in range(depth):
        mod3 = _modulation(c, w_ada[l], b_ada[l]).reshape(n_b, 1, 3 * d)
        x5 = x.reshape(n_b, c2, SUPER * CHUNK * d)
        lhs, wout, ap = _s5_coefficients(log_dt[l], lam_re[l], lam_im[l], b_re[l], b_im[l], c_re[l], c_im[l],
                                         d_skip[l], n_top)
        u_t, zs_t = _ssm_projection(x5, mod3, norm_g[l], w_in[l])
        y_t = _s5_scan(u_t, lhs, wout, ap, n_b, c2, n_top)
        yg_t = _glu(y_t, zs_t, w_glu[l], b_glu[l])
        qt, k, vt, zm = _mla_projection(x.reshape(n_b * seq, d), mod3, norm_g[l], w_in[l], q_a_g[l], w_q_b[l],
                                        kv_a_g[l], w_kv_b[l], q_norm_g[l], k_norm_g[l], rope, n_b, seq)
        ym = _attention(qt, k, vt, zm).reshape(n_b, c2, SUPER * CHUNK * MLA_WIDTH)
        x = _out_projection(yg_t, ym, x5, mod3, w_out[l]).reshape(n_b, seq, d)
    return x
```

```python
import functools
import math

import numpy as np
import jax
import jax.numpy as jnp
from jax import lax
from jax.experimental import pallas as pl
from jax.experimental.pallas import tpu as pltpu

F32 = jnp.float32
BF16 = jnp.bfloat16
HIGHEST = lax.Precision.HIGHEST

D_MODEL = 1024
SSM_WIDTH = 512
SSM_GROUP = 16
SSM_GROUPS = 32
SSM_STATE = 64
MLA_HEADS = 8
QK_NOPE = 64
QK_ROPE = 32
QK_HEAD = QK_NOPE + QK_ROPE
V_HEAD = 64
MLA_WIDTH = MLA_HEADS * V_HEAD
Q_LORA = 384
KV_LORA = 256
ROPE_THETA = 10000.0
EPS = 1e-6
NEG_BIG = -1e30

LANES = 128
CHUNK = 16
SUPER = 4
ROPE_HALF = QK_ROPE // 2

_NT = (((1,), (1,)), ((), ()))
_TN = (((0,), (0,)), ((), ()))


def _silu(v):
    return v * jax.nn.sigmoid(v)


def _rms(v, gain):
    return v * lax.rsqrt(jnp.mean(v * v, axis=-1, keepdims=True) + EPS) * gain


def _mod_kernel(c_ref, w_ref, b_ref, o_ref):
    act = _silu(c_ref[...])
    o_ref[...] = jnp.dot(act, w_ref[...], preferred_element_type=F32, precision=HIGHEST) + b_ref[...]


def _modulation(c, w, b):
    n_b, d = c.shape
    rows = 8
    c_pad = jnp.zeros((rows, d), F32).at[:n_b].set(c)
    tn = 512
    out = pl.pallas_call(
        _mod_kernel,
        out_shape=jax.ShapeDtypeStruct((rows, 3 * d), F32),
        grid=(3 * d // tn,),
        in_specs=[pl.BlockSpec((rows, d), lambda n: (0, 0)),
                  pl.BlockSpec((d, tn), lambda n: (0, n)),
                  pl.BlockSpec((1, tn), lambda n: (0, n))],
        out_specs=pl.BlockSpec((rows, tn), lambda n: (0, n)),
        name="adaln_modulation",
    )(c_pad, w, b.reshape(1, -1))
    return out[:n_b]


def _rope_kernel(pos_ref, freq_ref, cos_ref, sin_ref):
    ang = freq_ref[...] * pos_ref[...].astype(F32)
    cos_ref[...] = jnp.cos(ang)
    sin_ref[...] = jnp.sin(ang)


def _rope_tables(positions):
    t = positions.size
    tt = min(t, 2048)
    inv_freq = (ROPE_THETA ** (-np.arange(ROPE_HALF, dtype=np.float64) * 2.0 / QK_ROPE)).astype(np.float32)
    cos_t, sin_t = pl.pallas_call(
        _rope_kernel,
        out_shape=(jax.ShapeDtypeStruct((ROPE_HALF, t), F32),) * 2,
        grid=(t // tt,),
        in_specs=[pl.BlockSpec((1, tt), lambda i: (0, i)),
                  pl.BlockSpec((ROPE_HALF, 1), lambda i: (0, 0))],
        out_specs=(pl.BlockSpec((ROPE_HALF, tt), lambda i: (0, i)),) * 2,
        name="rope_tables",
    )(positions.reshape(1, t), jnp.asarray(inv_freq).reshape(ROPE_HALF, 1))
    cos, sin = cos_t.T, sin_t.T
    one = jnp.ones((t, LANES // 2 - ROPE_HALF), F32)
    zero = jnp.zeros_like(one)
    return (cos_t, sin_t, jnp.concatenate([cos, one, cos, one], axis=1),
            jnp.concatenate([-sin, zero, sin, zero], axis=1))


def _slab_source():
    src = np.full((LANES,), -1, np.int64)
    src[0:16] = QK_NOPE + np.arange(16)
    src[16:64] = np.arange(48)
    src[64:80] = QK_NOPE + 16 + np.arange(16)
    src[80:96] = 48 + np.arange(16)
    return src


def _to_slabs(w, per_head, src):
    k = w.shape[0]
    w3 = w.reshape(k, MLA_HEADS, per_head)
    valid = src >= 0
    cols = jnp.where(valid[None, None, :], w3[:, :, np.clip(src, 0, per_head - 1)], 0.0)
    return cols.reshape(k, MLA_HEADS * LANES)


V_ROWS = V_HEAD + 16


def _mla_proj_kernel(x_ref, mod_ref, ng_ref, w1_ref, qag_ref, kvag_ref, wqt_ref, wk_ref, wvt_ref,
                     gq_ref, gk_ref, ones_ref, cos_ref, sin_ref, cost_ref, sint_ref,
                     qt_ref, k_ref, vt_ref, zs_ref):
    d = D_MODEL
    tm = x_ref.shape[0]
    x = x_ref[...]
    mod = mod_ref[...]
    shift, scale = mod[:, :d], mod[:, d:2 * d]
    h = (_rms(x, ng_ref[...]) * (1.0 + scale) + shift).astype(BF16)
    p1 = jnp.dot(h, w1_ref[...], preferred_element_type=F32)
    o1, o2, o3 = Q_LORA, Q_LORA + KV_LORA, Q_LORA + KV_LORA + MLA_WIDTH
    cqn = _rms(p1[:, :o1], qag_ref[...]).astype(BF16)
    ckvn = _rms(p1[:, o1:o2], kvag_ref[...]).astype(BF16)
    zs_ref[...] = _silu(p1[:, o2:o3]).astype(BF16)
    kr = p1[:, o3:o3 + LANES]
    inv_dim = 1.0 / QK_HEAD

    kn = jnp.dot(ckvn, wk_ref[...], preferred_element_type=F32)
    cos, sin = cos_ref[...], sin_ref[...]
    ones = ones_ref[...]
    kr2 = jnp.concatenate([kr, kr], axis=1)
    for pair in range(MLA_HEADS // 2):
        sl = slice(2 * LANES * pair, 2 * LANES * (pair + 1))
        kp = kn[:, sl] + kr2
        ssk = jnp.dot((kp * kp).astype(BF16), ones, preferred_element_type=F32)
        kk = kp * lax.rsqrt(ssk * inv_dim + EPS) * gk_ref[:, sl]
        for e in range(2):
            kh = kk[:, LANES * e:LANES * (e + 1)]
            k_ref[2 * pair + e] = (kh * cos + pltpu.roll(kh, LANES // 2, 1) * sin).astype(BF16)

    qt = lax.dot_general(wqt_ref[...], cqn, _NT, preferred_element_type=F32)
    vt = lax.dot_general(wvt_ref[...], ckvn, _NT, preferred_element_type=F32)
    cos_t, sin_t = cost_ref[...], sint_ref[...]
    gq = jnp.concatenate([gq_ref[...]] * (tm // LANES), axis=1)
    first = lax.broadcasted_iota(jnp.int32, (V_ROWS - V_HEAD, tm), 0) == 0
    ones_rows = jnp.where(first, 1.0, 0.0).astype(BF16)
    r = ROPE_HALF
    for head in range(MLA_HEADS):
        slab = qt[LANES * head:LANES * (head + 1)]
        ss = jnp.sum(slab * slab, axis=0, keepdims=True)
        qn = slab * lax.rsqrt(ss * inv_dim + EPS) * gq
        x1, x2 = qn[0:r], qn[LANES // 2:LANES // 2 + r]
        qt_ref[head] = jnp.concatenate([x1 * cos_t - x2 * sin_t, qn[r:LANES // 2],
                                        x2 * cos_t + x1 * sin_t, qn[LANES // 2 + r:]], axis=0).astype(BF16)
        vt_ref[head, 0:V_HEAD, :] = vt[V_HEAD * head:V_HEAD * (head + 1)].astype(BF16)
        vt_ref[head, V_HEAD:, :] = ones_rows


def _mla_projection(x2, mod3, norm_g, w_in, q_a_g, w_q_b, kv_a_g, w_kv_b, q_norm_g, k_norm_g, rope, n_b, seq):
    t, d = x2.shape
    cos_t, sin_t, cos_tab, sin_tab = rope
    tm = min(256, seq)
    per_b = seq // tm
    o2 = 2 * SSM_WIDTH
    o3, o4, o5 = o2 + Q_LORA, o2 + Q_LORA + KV_LORA, o2 + Q_LORA + KV_LORA + QK_ROPE
    src = _slab_source()
    rope_src = np.where(src >= QK_NOPE, src - QK_NOPE, -1)
    kr_cols = jnp.where((rope_src >= 0)[None, :], w_in[:, o4:o5][:, np.clip(rope_src, 0, QK_ROPE - 1)], 0.0)
    w1 = jnp.concatenate([w_in[:, o2:o3], w_in[:, o3:o4], w_in[:, o5:], kr_cols], axis=1).astype(BF16)
    wqt = _to_slabs(w_q_b, QK_HEAD, src).T.astype(BF16)
    nope_src = np.where((src >= 0) & (src < QK_NOPE), src, -1)
    wk = _to_slabs(w_kv_b, QK_NOPE + V_HEAD, nope_src).astype(BF16)
    wvt = w_kv_b.reshape(KV_LORA, MLA_HEADS, QK_NOPE + V_HEAD)[:, :, QK_NOPE:].reshape(KV_LORA, MLA_WIDTH).T.astype(BF16)
    valid = (src >= 0)
    q_scale = math.log2(math.e) / math.sqrt(QK_HEAD)
    gq = jnp.where(valid, q_norm_g[np.clip(src, 0, QK_HEAD - 1)], 0.0) * q_scale
    gq = jnp.broadcast_to(gq[:, None], (LANES, LANES))
    gk = jnp.tile(jnp.where(valid, k_norm_g[np.clip(src, 0, QK_HEAD - 1)], 0.0), MLA_HEADS).reshape(1, -1)
    ones = jnp.asarray(np.kron(np.eye(2), np.ones((LANES, LANES))), BF16)
    hw = MLA_HEADS * LANES
    const = lambda shape: pl.BlockSpec(shape, lambda i: (0,) * len(shape))
    tok = lambda i: (i // per_b, 0, 0, i % per_b)
    return pl.pallas_call(
        _mla_proj_kernel,
        out_shape=(jax.ShapeDtypeStruct((n_b, MLA_HEADS, LANES, seq), BF16),
                   jax.ShapeDtypeStruct((n_b, MLA_HEADS, seq, LANES), BF16),
                   jax.ShapeDtypeStruct((n_b, MLA_HEADS, V_ROWS, seq), BF16),
                   jax.ShapeDtypeStruct((t, MLA_WIDTH), BF16)),
        grid=(t // tm,),
        in_specs=[pl.BlockSpec((tm, d), lambda i: (i, 0)),
                  pl.BlockSpec((None, 1, 3 * d), lambda i: (i // per_b, 0, 0)),
                  const((1, d)), const(w1.shape), const((1, Q_LORA)), const((1, KV_LORA)),
                  const(wqt.shape), const(wk.shape), const(wvt.shape), const((LANES, LANES)), const((1, hw)),
                  const(ones.shape),
                  pl.BlockSpec((tm, LANES), lambda i: (i, 0)),
                  pl.BlockSpec((tm, LANES), lambda i: (i, 0)),
                  pl.BlockSpec((ROPE_HALF, tm), lambda i: (0, i)),
                  pl.BlockSpec((ROPE_HALF, tm), lambda i: (0, i))],
        out_specs=(pl.BlockSpec((None, MLA_HEADS, LANES, tm), tok),
                   pl.BlockSpec((None, MLA_HEADS, tm, LANES), lambda i: (i // per_b, 0, i % per_b, 0)),
                   pl.BlockSpec((None, MLA_HEADS, V_ROWS, tm), tok),
                   pl.BlockSpec((tm, MLA_WIDTH), lambda i: (i, 0))),
        compiler_params=pltpu.CompilerParams(dimension_semantics=("arbitrary",),
                                             vmem_limit_bytes=48 * 1024 * 1024),
        name="mla_projection",
    )(x2, mod3, norm_g.reshape(1, d), w1, q_a_g.reshape(1, -1), kv_a_g.reshape(1, -1), wqt, wk, wvt, gq, gk, ones,
      cos_tab, sin_tab, cos_t, sin_t)


ATTN_TQ = 512
ATTN_TK = 256
HEADS_PER_STEP = 2


def _attn_kernel(qt_ref, k_ref, vt_ref, zs_ref, o_ref, s_scr):
    tq, tk = ATTN_TQ, ATTN_TK
    assert tq == 2 * tk
    qi = pl.program_id(2)
    q0 = qi * tq

    def scores(slot, t):
        start = pl.multiple_of(t * tk, tk)
        for e in range(HEADS_PER_STEP):
            s_scr[slot, e] = jnp.dot(k_ref[e, pl.ds(start, tk), :], qt_ref[e],
                                     preferred_element_type=F32)

    def process(slot, t, carry, masked):
        start = pl.multiple_of(t * tk, tk)
        out = []
        for e in range(HEADS_PER_STEP):
            m, acc = carry[e]
            s = s_scr[slot, e]
            if masked:
                key = start + lax.broadcasted_iota(jnp.int32, s.shape, 0)
                qry = q0 + lax.broadcasted_iota(jnp.int32, s.shape, 1)
                s = jnp.where(key <= qry, s, NEG_BIG)
            m_new = jnp.maximum(m, jnp.max(s, axis=0, keepdims=True))
            p = jnp.exp2(s - m_new).astype(BF16)
            acc = (jnp.exp2(m - m_new) * acc
                   + jnp.dot(vt_ref[e, :, pl.ds(start, tk)], p, preferred_element_type=F32))
            out.append((m_new, acc))
        return tuple(out)

    def body(i, carry):
        t = 2 * i
        scores(1, t + 1)
        carry = process(0, t, carry, False)
        scores(0, t + 2)
        return process(1, t + 1, carry, False)

    init = tuple((jnp.full((1, tq), NEG_BIG, F32), jnp.zeros((V_ROWS, tq), F32)) for _ in range(HEADS_PER_STEP))
    scores(0, 0)
    carry = lax.fori_loop(0, qi, body, init)
    t = 2 * qi
    scores(1, t + 1)
    carry = process(0, t, carry, True)
    carry = process(1, t + 1, carry, True)
    outs = []
    for e in range(HEADS_PER_STEP):
        acc = carry[e][1]
        outs.append((acc[:V_HEAD] / acc[V_HEAD:V_HEAD + 1]).T)
    o_ref[...] = (jnp.concatenate(outs, axis=1) * zs_ref[...].astype(F32)).astype(o_ref.dtype)


def _attention(qt, k, vt, zs):
    n_b, n_h, seq, _ = k.shape
    tq = ATTN_TQ
    nq = seq // tq
    hp = HEADS_PER_STEP
    width = hp * V_HEAD
    return pl.pallas_call(
        _attn_kernel,
        out_shape=jax.ShapeDtypeStruct(zs.shape, BF16),
        grid=(n_b, n_h // hp, nq),
        in_specs=[pl.BlockSpec((None, hp, LANES, tq), lambda b, h, i: (b, h, 0, i)),
                  pl.BlockSpec((None, hp, seq, LANES), lambda b, h, i: (b, h, 0, 0)),
                  pl.BlockSpec((None, hp, V_ROWS, seq), lambda b, h, i: (b, h, 0, 0)),
                  pl.BlockSpec((tq, width), lambda b, h, i: (b * nq + i, h))],
        out_specs=pl.BlockSpec((tq, width), lambda b, h, i: (b * nq + i, h)),
        scratch_shapes=[pltpu.VMEM((2, hp, ATTN_TK, tq), F32)],
        compiler_params=pltpu.CompilerParams(dimension_semantics=("arbitrary", "arbitrary", "arbitrary"),
                                             vmem_limit_bytes=48 * 1024 * 1024),
        name="causal_attention",
    )(qt, k, vt, zs)


def _s5_coef_kernel(lre_ref, lim_ref, ldt_ref, bre_ref, bim_ref, cre_ref, cim_ref, d_ref,
                    kmat_ref, winre_ref, winim_ref, cpre_ref, cpim_ref, apre_ref, apim_ref, *, n_top):
    lre, lim = lre_ref[...], lim_ref[...]
    dt = jnp.exp(ldt_ref[...])
    zr, zi = lre * dt, lim * dt

    def power(kk):
        mag = jnp.exp(kk * zr)
        return mag * jnp.cos(kk * zi), mag * jnp.sin(kk * zi)

    lb_re, lb_im = power(1.0)
    nr, ni = lb_re - 1.0, lb_im
    den = lre * lre + lim * lim
    f_re = (nr * lre + ni * lim) / den
    f_im = (ni * lre - nr * lim) / den
    bre, bim = bre_ref[...], bim_ref[...]
    bb_re = f_re * bre - f_im * bim
    bb_im = f_re * bim + f_im * bre
    cre, cim = cre_ref[...], cim_ref[...]
    cp_re, cp_im = [], []
    for kk in range(CHUNK + 1):
        pr, pi = power(float(kk))
        cr, ci = cre * pr - cim * pi, cre * pi + cim * pr
        cpre_ref[kk] = cr
        cpim_ref[kk] = ci
        cp_re.append(cr)
        cp_im.append(ci)
    all_re = jnp.concatenate(cp_re[:CHUNK], axis=0)
    all_im = jnp.concatenate(cp_im[:CHUNK], axis=0)
    kmat = (lax.dot_general(all_re, bb_re, _NT, preferred_element_type=F32, precision=HIGHEST)
            - lax.dot_general(all_im, bb_im, _NT, preferred_element_type=F32, precision=HIGHEST))
    h = SSM_GROUP
    eye = lax.broadcasted_iota(jnp.int32, (h, h), 0) == lax.broadcasted_iota(jnp.int32, (h, h), 1)
    kmat_ref[0:h, :] = kmat[0:h] + jnp.where(eye, d_ref[...], 0.0)
    kmat_ref[h:, :] = kmat[h:]
    for j in range(CHUNK):
        pr, pi = power(float(CHUNK - 1 - j))
        winre_ref[j] = pr * bb_re - pi * bb_im
        winim_ref[j] = pr * bb_im + pi * bb_re
    steps = [float(CHUNK * m) for m in range(SUPER)] + [float(CHUNK * SUPER * (1 << i)) for i in range(n_top)]
    for i, kk in enumerate(steps):
        pr, pi = power(kk)
        apre_ref[i:i + 1, :] = pr
        apim_ref[i:i + 1, :] = pi


def _s5_coefficients(log_dt, lam_re, lam_im, b_re, b_im, c_re, c_im, d_skip, n_top):
    g, p, h = SSM_GROUPS, SSM_STATE, SSM_GROUP
    n_ap = SUPER + n_top
    row = lambda a: a.reshape(g, 1, p)
    grp = lambda *shape: pl.BlockSpec((None,) + shape, lambda i: (i,) + (0,) * len(shape))
    outs = pl.pallas_call(
        functools.partial(_s5_coef_kernel, n_top=n_top),
        out_shape=(jax.ShapeDtypeStruct((g, CHUNK * h, h), F32),
                   jax.ShapeDtypeStruct((g, CHUNK, h, p), F32), jax.ShapeDtypeStruct((g, CHUNK, h, p), F32),
                   jax.ShapeDtypeStruct((g, CHUNK + 1, h, p), F32), jax.ShapeDtypeStruct((g, CHUNK + 1, h, p), F32),
                   jax.ShapeDtypeStruct((g, n_ap, p), F32), jax.ShapeDtypeStruct((g, n_ap, p), F32)),
        grid=(g,),
        in_specs=[grp(1, p), grp(1, p), grp(1, p), grp(h, p), grp(h, p), grp(h, p), grp(h, p), grp(h, 1)],
        out_specs=(grp(CHUNK * h, h), grp(CHUNK, h, p), grp(CHUNK, h, p), grp(CHUNK + 1, h, p),
                   grp(CHUNK + 1, h, p), grp(n_ap, p), grp(n_ap, p)),
        name="s5_coefficients",
    )(row(lam_re), row(lam_im), jnp.broadcast_to(log_dt[:, None, None], (g, 1, p)),
      b_re.transpose(0, 2, 1), b_im.transpose(0, 2, 1), c_re, c_im, d_skip.reshape(g, h, 1))
    kmat, win_re, win_im, cp_re, cp_im, ap_re, ap_im = outs
    lag = np.arange(CHUNK)[:, None] - np.arange(CHUNK)[None, :]
    toe = kmat.reshape(g, CHUNK, h, h)[:, np.clip(lag, 0, None)]
    toe = jnp.where((lag >= 0)[None, :, :, None, None], toe, 0.0)
    m_t = toe.transpose(0, 1, 3, 2, 4).reshape(g, CHUNK * h, CHUNK * h)
    win = jnp.concatenate([win_re.transpose(0, 3, 1, 2).reshape(g, p, CHUNK * h),
                           win_im.transpose(0, 3, 1, 2).reshape(g, p, CHUNK * h)], axis=1)
    lhs = jnp.concatenate([m_t, win], axis=1).astype(BF16)
    wout = jnp.concatenate([cp_re[:, 1:], -cp_im[:, 1:]], axis=-1).reshape(g, CHUNK * h, 2 * p).astype(BF16)
    ap = jnp.concatenate([ap_re, ap_im], axis=-1)
    ap = jnp.broadcast_to(ap[..., None], (g, n_ap, 2 * p, LANES))
    return lhs, wout, ap


PHASES = 8


def _ssm_proj_kernel(x_ref, mod_ref, ng_ref, w_ref, u_ref, z_ref, rows_scr, *, n_b):
    d = D_MODEL
    for jj in range(PHASES):
        hs = []
        for b in range(n_b):
            mod = mod_ref[b]
            shift, scale = mod[:, :d], mod[:, d:2 * d]
            rows_scr[b] = x_ref[b, :, jj, :]
            hs.append((_rms(rows_scr[b], ng_ref[...]) * (1.0 + scale) + shift).astype(BF16))
        h = jnp.concatenate(hs, axis=0)
        r = lax.dot_general(w_ref[...], h, _NT, preferred_element_type=F32)
        u_ref[jj] = r[:SSM_WIDTH].astype(BF16)
        z_ref[jj] = _silu(r[SSM_WIDTH:]).astype(BF16)


def _ssm_projection(x4, mod3, norm_g, w_in):
    n_b, c2, _, d = x4.shape
    lanes = n_b * c2
    halves = CHUNK // PHASES
    w_t = w_in[:, :2 * SSM_WIDTH].T.astype(BF16)
    out = jax.ShapeDtypeStruct((CHUNK, SSM_WIDTH, SUPER * lanes), BF16)
    out_spec = pl.BlockSpec((PHASES, SSM_WIDTH, lanes), lambda j2, hf: (hf, 0, j2))
    return pl.pallas_call(
        functools.partial(_ssm_proj_kernel, n_b=n_b),
        out_shape=(out, out),
        grid=(SUPER, halves),
        in_specs=[pl.BlockSpec((n_b, c2, PHASES, d), lambda j2, hf: (0, 0, j2 * halves + hf, 0)),
                  pl.BlockSpec((n_b, 1, 3 * d), lambda j2, hf: (0, 0, 0)),
                  pl.BlockSpec((1, d), lambda j2, hf: (0, 0)),
                  pl.BlockSpec(w_t.shape, lambda j2, hf: (0, 0))],
        out_specs=(out_spec, out_spec),
        scratch_shapes=[pltpu.VMEM((n_b, c2, d), F32)],
        compiler_params=pltpu.CompilerParams(dimension_semantics=("arbitrary", "arbitrary"),
                                             vmem_limit_bytes=56 * 1024 * 1024),
        name="ssm_projection",
    )(x4, mod3, norm_g.reshape(1, d), w_t)


def _cmul(ar, ai, xr, xi):
    return ar * xr - ai * xi, ar * xi + ai * xr


def _s5_group_kernel(x_ref, lhs_ref, wout_ref, ap_ref, y_ref, *, n_b, c2, n_top):
    p = SSM_STATE
    rows = CHUNK * SSM_GROUP
    n = SUPER * n_b * c2
    x = x_ref[...].reshape(rows, n)
    r = jnp.dot(lhs_ref[...], x, preferred_element_type=F32)
    y_intra, c_re, c_im = r[:rows], r[rows:rows + p], r[rows + p:]

    def mult(i):
        return ap_ref[i, :p, :c2], ap_ref[i, p:, :c2]

    lane = lax.broadcasted_iota(jnp.int32, (p, c2), 1)

    def shifted(a, sh):
        return jnp.where(lane >= sh, pltpu.roll(a, sh, 1), 0.0)

    pieces = [[None] * n_b for _ in range(SUPER)]
    for b in range(n_b):
        piece = lambda a, j2: a[:, (j2 * n_b + b) * c2:(j2 * n_b + b + 1) * c2]
        a_re, a_im = mult(1)
        e_re = jnp.zeros((p, c2), F32)
        e_im = jnp.zeros((p, c2), F32)
        local = []
        for j2 in range(SUPER):
            local.append((e_re, e_im))
            t_re, t_im = _cmul(a_re, a_im, e_re, e_im)
            e_re, e_im = t_re + piece(c_re, j2), t_im + piece(c_im, j2)
        for i in range(n_top):
            m_re, m_im = mult(SUPER + i)
            t_re, t_im = _cmul(m_re, m_im, shifted(e_re, 1 << i), shifted(e_im, 1 << i))
            e_re, e_im = e_re + t_re, e_im + t_im
        s_re, s_im = shifted(e_re, 1), shifted(e_im, 1)
        for j2 in range(SUPER):
            if j2 == 0:
                f_re, f_im = s_re, s_im
            else:
                m_re, m_im = mult(j2)
                t_re, t_im = _cmul(m_re, m_im, s_re, s_im)
                f_re, f_im = local[j2][0] + t_re, local[j2][1] + t_im
            pieces[j2][b] = jnp.concatenate([f_re, f_im], axis=0)
    s_in = jnp.concatenate([pieces[j2][b] for j2 in range(SUPER) for b in range(n_b)], axis=1)
    y = y_intra + jnp.dot(wout_ref[...], s_in.astype(BF16), preferred_element_type=F32)
    y_ref[...] = y.reshape(CHUNK, SSM_GROUP, n).astype(y_ref.dtype)


def _s5_scan(u_t, lhs, wout, ap, n_b, c2, n_top):
    g, h = SSM_GROUPS, SSM_GROUP
    n = u_t.shape[-1]
    grp = lambda *shape: pl.BlockSpec((None,) + shape, lambda i: (i,) + (0,) * len(shape))
    return pl.pallas_call(
        functools.partial(_s5_group_kernel, n_b=n_b, c2=c2, n_top=n_top),
        out_shape=jax.ShapeDtypeStruct((g, CHUNK, h, n), BF16),
        grid=(g,),
        in_specs=[pl.BlockSpec((CHUNK, h, n), lambda i: (0, i, 0)),
                  grp(*lhs.shape[1:]), grp(*wout.shape[1:]), grp(*ap.shape[1:])],
        out_specs=grp(CHUNK, h, n),
        compiler_params=pltpu.CompilerParams(dimension_semantics=("arbitrary",),
                                             vmem_limit_bytes=48 * 1024 * 1024),
        name="s5_chunk_scan",
    )(u_t, lhs, wout, ap)


def _glu_kernel(y_ref, zs_ref, w_ref, b_ref, o_ref, *, n_b, c2):
    g, _, h, n = y_ref.shape
    for jj in range(PHASES):
        y = jax.nn.gelu(y_ref[:, jj].reshape(g * h, n).astype(F32))
        t = jnp.dot(w_ref[...], y.astype(BF16), preferred_element_type=F32) + b_ref[...]
        gated = (y * jax.nn.sigmoid(t) * zs_ref[jj].astype(F32)).T
        for b in range(n_b):
            o_ref[b, :, jj, :] = gated[b * c2:(b + 1) * c2]


def _glu(y_t, zs_t, w_glu, b_glu, n_b, c2):
    g, _, h, _ = y_t.shape
    lanes = n_b * c2
    halves = CHUNK // PHASES
    w_t = w_glu.T.astype(BF16)
    return pl.pallas_call(
        functools.partial(_glu_kernel, n_b=n_b, c2=c2),
        out_shape=jax.ShapeDtypeStruct((n_b, c2, SUPER * CHUNK, SSM_WIDTH), F32),
        grid=(SUPER, halves),
        in_specs=[pl.BlockSpec((g, PHASES, h, lanes), lambda j2, hf: (0, hf, 0, j2)),
                  pl.BlockSpec((PHASES, SSM_WIDTH, lanes), lambda j2, hf: (hf, 0, j2)),
                  pl.BlockSpec(w_t.shape, lambda j2, hf: (0, 0)),
                  pl.BlockSpec((SSM_WIDTH, 1), lambda j2, hf: (0, 0))],
        out_specs=pl.BlockSpec((n_b, c2, PHASES, SSM_WIDTH), lambda j2, hf: (0, 0, j2 * halves + hf, 0)),
        compiler_params=pltpu.CompilerParams(dimension_semantics=("arbitrary", "arbitrary"),
                                             vmem_limit_bytes=48 * 1024 * 1024),
        name="s5_glu",
    )(y_t, zs_t, w_t, b_glu.reshape(SSM_WIDTH, 1))


def _out_proj_kernel(ys_ref, ym_ref, x_ref, mod_ref, wa_ref, wb_ref, o_ref):
    y = (jnp.dot(ys_ref[...].astype(BF16), wa_ref[...], preferred_element_type=F32)
         + jnp.dot(ym_ref[...], wb_ref[...], preferred_element_type=F32))
    gate = mod_ref[...][:, 2 * D_MODEL:]
    o_ref[...] = x_ref[...] + gate * y


def _out_projection(ys, ym, x2, mod3, w_out, seq):
    t, d = x2.shape
    tm = min(512, seq)
    per_b = seq // tm
    wa = w_out[:SSM_WIDTH].astype(BF16)
    wb = w_out[SSM_WIDTH:].astype(BF16)
    tok = lambda w: pl.BlockSpec((tm, w), lambda i: (i, 0))
    return pl.pallas_call(
        _out_proj_kernel,
        out_shape=jax.ShapeDtypeStruct(x2.shape, F32),
        grid=(t // tm,),
        in_specs=[tok(SSM_WIDTH), tok(MLA_WIDTH), tok(d),
                  pl.BlockSpec((None, 1, 3 * d), lambda i: (i // per_b, 0, 0)),
                  pl.BlockSpec(wa.shape, lambda i: (0, 0)),
                  pl.BlockSpec(wb.shape, lambda i: (0, 0))],
        out_specs=tok(d),
        compiler_params=pltpu.CompilerParams(dimension_semantics=("arbitrary",)),
        name="output_projection",
    )(ys, ym, x2, mod3, wa, wb)


def kernel(x, c, positions, w_ada, b_ada, norm_g, w_in, log_dt, lam_re, lam_im, b_re, b_im, c_re, c_im, d_skip,
           w_glu, b_glu, q_a_g, w_q_b, kv_a_g, w_kv_b, q_norm_g, k_norm_g, w_out):
    n_b, seq, d = x.shape
    depth = w_ada.shape[0]
    c2 = seq // (CHUNK * SUPER)
    n_top = max(int(math.log2(c2)), 0)
    assert c2 * CHUNK * SUPER == seq and (1 << n_top) == c2
    rope = _rope_tables(positions)
    for l in range(depth):
        mod3 = _modulation(c, w_ada[l], b_ada[l]).reshape(n_b, 1, 3 * d)
        x2 = x.reshape(n_b * seq, d)
        lhs, wout, ap = _s5_coefficients(log_dt[l], lam_re[l], lam_im[l], b_re[l], b_im[l], c_re[l], c_im[l],
                                         d_skip[l], n_top)
        u_t, zs_t = _ssm_projection(x.reshape(n_b, c2, SUPER * CHUNK, d), mod3, norm_g[l], w_in[l])
        y_t = _s5_scan(u_t, lhs, wout, ap, n_b, c2, n_top)
        ys = _glu(y_t, zs_t, w_glu[l], b_glu[l], n_b, c2).reshape(n_b * seq, SSM_WIDTH)
        qt, k, vt, zm = _mla_projection(x2, mod3, norm_g[l], w_in[l], q_a_g[l], w_q_b[l],
                                        kv_a_g[l], w_kv_b[l], q_norm_g[l], k_norm_g[l], rope, n_b, seq)
        ym = _attention(qt, k, vt, zm)
        x = _out_projection(ys, ym, x2, mod3, w_out[l], seq).reshape(n_b, seq, d)
    return x
```

```python
import functools
import math

import numpy as np
import jax
import jax.numpy as jnp
from jax import lax
from jax.experimental import pallas as pl
from jax.experimental.pallas import tpu as pltpu

F32 = jnp.float32
BF16 = jnp.bfloat16
HIGHEST = lax.Precision.HIGHEST

D_MODEL = 1024
SSM_WIDTH = 512
SSM_GROUP = 16
SSM_GROUPS = 32
SSM_STATE = 64
MLA_HEADS = 8
QK_NOPE = 64
QK_ROPE = 32
QK_HEAD = QK_NOPE + QK_ROPE
V_HEAD = 64
MLA_WIDTH = MLA_HEADS * V_HEAD
Q_LORA = 384
KV_LORA = 256
ROPE_THETA = 10000.0
EPS = 1e-6
NEG_BIG = -1e30

LANES = 128
CHUNK = 16
SUPER = 4
ROPE_HALF = QK_ROPE // 2

_NT = (((1,), (1,)), ((), ()))
_TN = (((0,), (0,)), ((), ()))


def _silu(v):
    return v * jax.nn.sigmoid(v)


def _rms(v, gain):
    return v * lax.rsqrt(jnp.mean(v * v, axis=-1, keepdims=True) + EPS) * gain


def _mod_kernel(c_ref, w_ref, b_ref, o_ref):
    act = _silu(c_ref[...])
    o_ref[...] = jnp.dot(act, w_ref[...], preferred_element_type=F32, precision=HIGHEST) + b_ref[...]


def _modulation(c, w, b):
    n_b, d = c.shape
    rows = 8
    c_pad = jnp.zeros((rows, d), F32).at[:n_b].set(c)
    tn = 512
    out = pl.pallas_call(
        _mod_kernel,
        out_shape=jax.ShapeDtypeStruct((rows, 3 * d), F32),
        grid=(3 * d // tn,),
        in_specs=[pl.BlockSpec((rows, d), lambda n: (0, 0)),
                  pl.BlockSpec((d, tn), lambda n: (0, n)),
                  pl.BlockSpec((1, tn), lambda n: (0, n))],
        out_specs=pl.BlockSpec((rows, tn), lambda n: (0, n)),
        name="adaln_modulation",
    )(c_pad, w, b.reshape(1, -1))
    return out[:n_b]


def _rope_kernel(pos_ref, freq_ref, cos_ref, sin_ref):
    ang = freq_ref[...] * pos_ref[...].astype(F32)
    cos_ref[...] = jnp.cos(ang)
    sin_ref[...] = jnp.sin(ang)


def _rope_tables(positions):
    t = positions.size
    tt = min(t, 2048)
    inv_freq = (ROPE_THETA ** (-np.arange(ROPE_HALF, dtype=np.float64) * 2.0 / QK_ROPE)).astype(np.float32)
    cos_t, sin_t = pl.pallas_call(
        _rope_kernel,
        out_shape=(jax.ShapeDtypeStruct((ROPE_HALF, t), F32),) * 2,
        grid=(t // tt,),
        in_specs=[pl.BlockSpec((1, tt), lambda i: (0, i)),
                  pl.BlockSpec((ROPE_HALF, 1), lambda i: (0, 0))],
        out_specs=(pl.BlockSpec((ROPE_HALF, tt), lambda i: (0, i)),) * 2,
        name="rope_tables",
    )(positions.reshape(1, t), jnp.asarray(inv_freq).reshape(ROPE_HALF, 1))
    cos, sin = cos_t.T, sin_t.T
    one = jnp.ones((t, LANES // 2 - ROPE_HALF), F32)
    zero = jnp.zeros_like(one)
    return (cos_t, sin_t, jnp.concatenate([cos, one, cos, one], axis=1),
            jnp.concatenate([-sin, zero, sin, zero], axis=1))


def _slab_source():
    src = np.full((LANES,), -1, np.int64)
    src[0:16] = QK_NOPE + np.arange(16)
    src[16:64] = np.arange(48)
    src[64:80] = QK_NOPE + 16 + np.arange(16)
    src[80:96] = 48 + np.arange(16)
    return src


def _to_slabs(w, per_head, src):
    k = w.shape[0]
    w3 = w.reshape(k, MLA_HEADS, per_head)
    valid = src >= 0
    cols = jnp.where(valid[None, None, :], w3[:, :, np.clip(src, 0, per_head - 1)], 0.0)
    return cols.reshape(k, MLA_HEADS * LANES)


V_ROWS = V_HEAD + 16


def _mla_proj_kernel(x_ref, mod_ref, ng_ref, w1_ref, qag_ref, kvag_ref, wqt_ref, wk_ref, wvt_ref,
                     gq_ref, gk_ref, ones_ref, cos_ref, sin_ref, cost_ref, sint_ref,
                     qt_ref, k_ref, vt_ref, zs_ref):
    d = D_MODEL
    tm = x_ref.shape[0]
    x = x_ref[...]
    mod = mod_ref[...]
    shift, scale = mod[:, :d], mod[:, d:2 * d]
    h = (_rms(x, ng_ref[...]) * (1.0 + scale) + shift).astype(BF16)
    p1 = jnp.dot(h, w1_ref[...], preferred_element_type=F32)
    o1, o2, o3 = Q_LORA, Q_LORA + KV_LORA, Q_LORA + KV_LORA + MLA_WIDTH
    cqn = _rms(p1[:, :o1], qag_ref[...]).astype(BF16)
    ckvn = _rms(p1[:, o1:o2], kvag_ref[...]).astype(BF16)
    zs_ref[...] = _silu(p1[:, o2:o3]).astype(BF16)
    kr = p1[:, o3:o3 + LANES]
    inv_dim = 1.0 / QK_HEAD

    kn = jnp.dot(ckvn, wk_ref[...], preferred_element_type=F32)
    cos, sin = cos_ref[...], sin_ref[...]
    ones = ones_ref[...]
    kr2 = jnp.concatenate([kr, kr], axis=1)
    for pair in range(MLA_HEADS // 2):
        sl = slice(2 * LANES * pair, 2 * LANES * (pair + 1))
        kp = kn[:, sl] + kr2
        ssk = jnp.dot((kp * kp).astype(BF16), ones, preferred_element_type=F32)
        kk = kp * lax.rsqrt(ssk * inv_dim + EPS) * gk_ref[:, sl]
        for e in range(2):
            kh = kk[:, LANES * e:LANES * (e + 1)]
            k_ref[2 * pair + e] = (kh * cos + pltpu.roll(kh, LANES // 2, 1) * sin).astype(BF16)

    qt = lax.dot_general(wqt_ref[...], cqn, _NT, preferred_element_type=F32)
    vt = lax.dot_general(wvt_ref[...], ckvn, _NT, preferred_element_type=F32)
    cos_t, sin_t = cost_ref[...], sint_ref[...]
    gq = jnp.concatenate([gq_ref[...]] * (tm // LANES), axis=1)
    first = lax.broadcasted_iota(jnp.int32, (V_ROWS - V_HEAD, tm), 0) == 0
    ones_rows = jnp.where(first, 1.0, 0.0).astype(BF16)
    r = ROPE_HALF
    for head in range(MLA_HEADS):
        slab = qt[LANES * head:LANES * (head + 1)]
        ss = jnp.sum(slab * slab, axis=0, keepdims=True)
        qn = slab * lax.rsqrt(ss * inv_dim + EPS) * gq
        x1, x2 = qn[0:r], qn[LANES // 2:LANES // 2 + r]
        qt_ref[head] = jnp.concatenate([x1 * cos_t - x2 * sin_t, qn[r:LANES // 2],
                                        x2 * cos_t + x1 * sin_t, qn[LANES // 2 + r:]], axis=0).astype(BF16)
        vt_ref[head, 0:V_HEAD, :] = vt[V_HEAD * head:V_HEAD * (head + 1)].astype(BF16)
        vt_ref[head, V_HEAD:, :] = ones_rows


def _mla_projection(x2, mod3, norm_g, w_in, q_a_g, w_q_b, kv_a_g, w_kv_b, q_norm_g, k_norm_g, rope, n_b, seq):
    t, d = x2.shape
    cos_t, sin_t, cos_tab, sin_tab = rope
    tm = min(256, seq)
    per_b = seq // tm
    o2 = 2 * SSM_WIDTH
    o3, o4, o5 = o2 + Q_LORA, o2 + Q_LORA + KV_LORA, o2 + Q_LORA + KV_LORA + QK_ROPE
    src = _slab_source()
    rope_src = np.where(src >= QK_NOPE, src - QK_NOPE, -1)
    kr_cols = jnp.where((rope_src >= 0)[None, :], w_in[:, o4:o5][:, np.clip(rope_src, 0, QK_ROPE - 1)], 0.0)
    w1 = jnp.concatenate([w_in[:, o2:o3], w_in[:, o3:o4], w_in[:, o5:], kr_cols], axis=1).astype(BF16)
    wqt = _to_slabs(w_q_b, QK_HEAD, src).T.astype(BF16)
    nope_src = np.where((src >= 0) & (src < QK_NOPE), src, -1)
    wk = _to_slabs(w_kv_b, QK_NOPE + V_HEAD, nope_src).astype(BF16)
    wvt = w_kv_b.reshape(KV_LORA, MLA_HEADS, QK_NOPE + V_HEAD)[:, :, QK_NOPE:].reshape(KV_LORA, MLA_WIDTH).T.astype(BF16)
    valid = (src >= 0)
    q_scale = math.log2(math.e) / math.sqrt(QK_HEAD)
    gq = jnp.where(valid, q_norm_g[np.clip(src, 0, QK_HEAD - 1)], 0.0) * q_scale
    gq = jnp.broadcast_to(gq[:, None], (LANES, LANES))
    gk = jnp.tile(jnp.where(valid, k_norm_g[np.clip(src, 0, QK_HEAD - 1)], 0.0), MLA_HEADS).reshape(1, -1)
    ones = jnp.asarray(np.kron(np.eye(2), np.ones((LANES, LANES))), BF16)
    hw = MLA_HEADS * LANES
    const = lambda shape: pl.BlockSpec(shape, lambda i: (0,) * len(shape))
    tok = lambda i: (i // per_b, 0, 0, i % per_b)
    return pl.pallas_call(
        _mla_proj_kernel,
        out_shape=(jax.ShapeDtypeStruct((n_b, MLA_HEADS, LANES, seq), BF16),
                   jax.ShapeDtypeStruct((n_b, MLA_HEADS, seq, LANES), BF16),
                   jax.ShapeDtypeStruct((n_b, MLA_HEADS, V_ROWS, seq), BF16),
                   jax.ShapeDtypeStruct((t, MLA_WIDTH), BF16)),
        grid=(t // tm,),
        in_specs=[pl.BlockSpec((tm, d), lambda i: (i, 0)),
                  pl.BlockSpec((None, 1, 3 * d), lambda i: (i // per_b, 0, 0)),
                  const((1, d)), const(w1.shape), const((1, Q_LORA)), const((1, KV_LORA)),
                  const(wqt.shape), const(wk.shape), const(wvt.shape), const((LANES, LANES)), const((1, hw)),
                  const(ones.shape),
                  pl.BlockSpec((tm, LANES), lambda i: (i, 0)),
                  pl.BlockSpec((tm, LANES), lambda i: (i, 0)),
                  pl.BlockSpec((ROPE_HALF, tm), lambda i: (0, i)),
                  pl.BlockSpec((ROPE_HALF, tm), lambda i: (0, i))],
        out_specs=(pl.BlockSpec((None, MLA_HEADS, LANES, tm), tok),
                   pl.BlockSpec((None, MLA_HEADS, tm, LANES), lambda i: (i // per_b, 0, i % per_b, 0)),
                   pl.BlockSpec((None, MLA_HEADS, V_ROWS, tm), tok),
                   pl.BlockSpec((tm, MLA_WIDTH), lambda i: (i, 0))),
        compiler_params=pltpu.CompilerParams(dimension_semantics=("arbitrary",),
                                             vmem_limit_bytes=48 * 1024 * 1024),
        name="mla_projection",
    )(x2, mod3, norm_g.reshape(1, d), w1, q_a_g.reshape(1, -1), kv_a_g.reshape(1, -1), wqt, wk, wvt, gq, gk, ones,
      cos_tab, sin_tab, cos_t, sin_t)


ATTN_TQ = 1024
ATTN_TK = 256
HEADS_PER_STEP = 2


def _attn_kernel(qt_ref, k_ref, vt_ref, zs_ref, o_ref, s_scr):
    tq, tk = ATTN_TQ, ATTN_TK
    r = tq // tk
    assert r * tk == tq and r % 2 == 0
    qi = pl.program_id(2)

    def scores(slot, t, lo=0):
        start = pl.multiple_of(t * tk, tk)
        for e in range(HEADS_PER_STEP):
            s_scr[slot, e, :, lo:] = jnp.dot(k_ref[e, pl.ds(start, tk), :], qt_ref[e, :, lo:],
                                             preferred_element_type=F32)

    def process(slot, t, carry, lo=0, diagonal=False):
        start = pl.multiple_of(t * tk, tk)
        out = []
        for e in range(HEADS_PER_STEP):
            m, acc = carry[e]
            s = s_scr[slot, e, :, lo:]
            if diagonal:
                blk = s[:, :tk]
                ok = lax.broadcasted_iota(jnp.int32, blk.shape, 0) <= lax.broadcasted_iota(jnp.int32, blk.shape, 1)
                blk = jnp.where(ok, blk, NEG_BIG)
                s = jnp.concatenate([blk, s[:, tk:]], axis=1) if s.shape[1] > tk else blk
            m_old = m[:, lo:]
            m_new = jnp.maximum(m_old, jnp.max(s, axis=0, keepdims=True))
            p = jnp.exp2(s - m_new).astype(BF16)
            acc_new = (jnp.exp2(m_old - m_new) * acc[:, lo:]
                       + jnp.dot(vt_ref[e, :, pl.ds(start, tk)], p, preferred_element_type=F32))
            if lo:
                m_new = jnp.concatenate([m[:, :lo], m_new], axis=1)
                acc_new = jnp.concatenate([acc[:, :lo], acc_new], axis=1)
            out.append((m_new, acc_new))
        return tuple(out)

    def body(i, carry):
        base = i * r
        for u in range(r):
            scores((u + 1) % 2, base + u + 1)
            carry = process(u % 2, base + u, carry)
        return carry

    init = tuple((jnp.full((1, tq), NEG_BIG, F32), jnp.zeros((V_ROWS, tq), F32)) for _ in range(HEADS_PER_STEP))
    scores(0, 0)
    carry = lax.fori_loop(0, qi, body, init)
    base = qi * r
    for u in range(r):
        if u + 1 < r:
            scores((u + 1) % 2, base + u + 1, lo=(u + 1) * tk)
        carry = process(u % 2, base + u, carry, lo=u * tk, diagonal=True)
    outs = []
    for e in range(HEADS_PER_STEP):
        acc = carry[e][1]
        outs.append((acc[:V_HEAD] / acc[V_HEAD:V_HEAD + 1]).T)
    o_ref[...] = (jnp.concatenate(outs, axis=1) * zs_ref[...].astype(F32)).astype(o_ref.dtype)


def _attention(qt, k, vt, zs):
    n_b, n_h, seq, _ = k.shape
    tq = ATTN_TQ
    nq = seq // tq
    hp = HEADS_PER_STEP
    width = hp * V_HEAD
    return pl.pallas_call(
        _attn_kernel,
        out_shape=jax.ShapeDtypeStruct(zs.shape, BF16),
        grid=(n_b, n_h // hp, nq),
        in_specs=[pl.BlockSpec((None, hp, LANES, tq), lambda b, h, i: (b, h, 0, i)),
                  pl.BlockSpec((None, hp, seq, LANES), lambda b, h, i: (b, h, 0, 0)),
                  pl.BlockSpec((None, hp, V_ROWS, seq), lambda b, h, i: (b, h, 0, 0)),
                  pl.BlockSpec((tq, width), lambda b, h, i: (b * nq + i, h))],
        out_specs=pl.BlockSpec((tq, width), lambda b, h, i: (b * nq + i, h)),
        scratch_shapes=[pltpu.VMEM((2, hp, ATTN_TK, tq), F32)],
        compiler_params=pltpu.CompilerParams(dimension_semantics=("arbitrary", "arbitrary", "arbitrary"),
                                             vmem_limit_bytes=48 * 1024 * 1024),
        name="causal_attention",
    )(qt, k, vt, zs)


def _s5_coef_kernel(lre_ref, lim_ref, ldt_ref, bre_ref, bim_ref, cre_ref, cim_ref, d_ref,
                    kmat_ref, winre_ref, winim_ref, cpre_ref, cpim_ref, apre_ref, apim_ref, *, n_top):
    lre, lim = lre_ref[...], lim_ref[...]
    dt = jnp.exp(ldt_ref[...])
    zr, zi = lre * dt, lim * dt

    def power(kk):
        mag = jnp.exp(kk * zr)
        return mag * jnp.cos(kk * zi), mag * jnp.sin(kk * zi)

    lb_re, lb_im = power(1.0)
    nr, ni = lb_re - 1.0, lb_im
    den = lre * lre + lim * lim
    f_re = (nr * lre + ni * lim) / den
    f_im = (ni * lre - nr * lim) / den
    bre, bim = bre_ref[...], bim_ref[...]
    bb_re = f_re * bre - f_im * bim
    bb_im = f_re * bim + f_im * bre
    cre, cim = cre_ref[...], cim_ref[...]
    cp_re, cp_im = [], []
    for kk in range(CHUNK + 1):
        pr, pi = power(float(kk))
        cr, ci = cre * pr - cim * pi, cre * pi + cim * pr
        cpre_ref[kk] = cr
        cpim_ref[kk] = ci
        cp_re.append(cr)
        cp_im.append(ci)
    all_re = jnp.concatenate(cp_re[:CHUNK], axis=0)
    all_im = jnp.concatenate(cp_im[:CHUNK], axis=0)
    kmat = (lax.dot_general(all_re, bb_re, _NT, preferred_element_type=F32, precision=HIGHEST)
            - lax.dot_general(all_im, bb_im, _NT, preferred_element_type=F32, precision=HIGHEST))
    h = SSM_GROUP
    eye = lax.broadcasted_iota(jnp.int32, (h, h), 0) == lax.broadcasted_iota(jnp.int32, (h, h), 1)
    kmat_ref[0:h, :] = kmat[0:h] + jnp.where(eye, d_ref[...], 0.0)
    kmat_ref[h:, :] = kmat[h:]
    for j in range(CHUNK):
        pr, pi = power(float(CHUNK - 1 - j))
        winre_ref[j] = pr * bb_re - pi * bb_im
        winim_ref[j] = pr * bb_im + pi * bb_re
    steps = [float(CHUNK * m) for m in range(SUPER)] + [float(CHUNK * SUPER * (1 << i)) for i in range(n_top)]
    for i, kk in enumerate(steps):
        pr, pi = power(kk)
        apre_ref[i:i + 1, :] = pr
        apim_ref[i:i + 1, :] = pi


def _s5_coefficients(log_dt, lam_re, lam_im, b_re, b_im, c_re, c_im, d_skip, n_top):
    g, p, h = SSM_GROUPS, SSM_STATE, SSM_GROUP
    n_ap = SUPER + n_top
    row = lambda a: a.reshape(g, 1, p)
    grp = lambda *shape: pl.BlockSpec((None,) + shape, lambda i: (i,) + (0,) * len(shape))
    outs = pl.pallas_call(
        functools.partial(_s5_coef_kernel, n_top=n_top),
        out_shape=(jax.ShapeDtypeStruct((g, CHUNK * h, h), F32),
                   jax.ShapeDtypeStruct((g, CHUNK, h, p), F32), jax.ShapeDtypeStruct((g, CHUNK, h, p), F32),
                   jax.ShapeDtypeStruct((g, CHUNK + 1, h, p), F32), jax.ShapeDtypeStruct((g, CHUNK + 1, h, p), F32),
                   jax.ShapeDtypeStruct((g, n_ap, p), F32), jax.ShapeDtypeStruct((g, n_ap, p), F32)),
        grid=(g,),
        in_specs=[grp(1, p), grp(1, p), grp(1, p), grp(h, p), grp(h, p), grp(h, p), grp(h, p), grp(h, 1)],
        out_specs=(grp(CHUNK * h, h), grp(CHUNK, h, p), grp(CHUNK, h, p), grp(CHUNK + 1, h, p),
                   grp(CHUNK + 1, h, p), grp(n_ap, p), grp(n_ap, p)),
        name="s5_coefficients",
    )(row(lam_re), row(lam_im), jnp.broadcast_to(log_dt[:, None, None], (g, 1, p)),
      b_re.transpose(0, 2, 1), b_im.transpose(0, 2, 1), c_re, c_im, d_skip.reshape(g, h, 1))
    kmat, win_re, win_im, cp_re, cp_im, ap_re, ap_im = outs
    lag = np.arange(CHUNK)[:, None] - np.arange(CHUNK)[None, :]
    toe = kmat.reshape(g, CHUNK, h, h)[:, np.clip(lag, 0, None)]
    toe = jnp.where((lag >= 0)[None, :, :, None, None], toe, 0.0)
    m_t = toe.transpose(0, 1, 3, 2, 4).reshape(g, CHUNK * h, CHUNK * h)
    win = jnp.concatenate([win_re.transpose(0, 3, 1, 2).reshape(g, p, CHUNK * h),
                           win_im.transpose(0, 3, 1, 2).reshape(g, p, CHUNK * h)], axis=1)
    lhs = jnp.concatenate([m_t, win], axis=1).astype(BF16)
    wout = jnp.concatenate([cp_re[:, 1:], -cp_im[:, 1:]], axis=-1).reshape(g, CHUNK * h, 2 * p).astype(BF16)
    ap = jnp.concatenate([ap_re, ap_im], axis=-1)
    ap = jnp.broadcast_to(ap[..., None], (g, n_ap, 2 * p, LANES))
    return lhs, wout, ap


PHASES = 8


def _ssm_proj_kernel(x_ref, mod_ref, ng_ref, w_ref, u_ref, z_ref, rows_scr, *, n_b):
    d = D_MODEL
    for jj in range(PHASES):
        hs = []
        for b in range(n_b):
            mod = mod_ref[b]
            shift, scale = mod[:, :d], mod[:, d:2 * d]
            rows_scr[b] = x_ref[b, :, jj, :]
            hs.append((_rms(rows_scr[b], ng_ref[...]) * (1.0 + scale) + shift).astype(BF16))
        h = jnp.concatenate(hs, axis=0)
        r = lax.dot_general(w_ref[...], h, _NT, preferred_element_type=F32)
        u_ref[jj] = r[:SSM_WIDTH].astype(BF16)
        z_ref[jj] = _silu(r[SSM_WIDTH:]).astype(BF16)


def _ssm_projection(x4, mod3, norm_g, w_in):
    n_b, c2, _, d = x4.shape
    lanes = n_b * c2
    halves = CHUNK // PHASES
    w_t = w_in[:, :2 * SSM_WIDTH].T.astype(BF16)
    out = jax.ShapeDtypeStruct((CHUNK, SSM_WIDTH, SUPER * lanes), BF16)
    out_spec = pl.BlockSpec((PHASES, SSM_WIDTH, lanes), lambda j2, hf: (hf, 0, j2))
    return pl.pallas_call(
        functools.partial(_ssm_proj_kernel, n_b=n_b),
        out_shape=(out, out),
        grid=(SUPER, halves),
        in_specs=[pl.BlockSpec((n_b, c2, PHASES, d), lambda j2, hf: (0, 0, j2 * halves + hf, 0)),
                  pl.BlockSpec((n_b, 1, 3 * d), lambda j2, hf: (0, 0, 0)),
                  pl.BlockSpec((1, d), lambda j2, hf: (0, 0)),
                  pl.BlockSpec(w_t.shape, lambda j2, hf: (0, 0))],
        out_specs=(out_spec, out_spec),
        scratch_shapes=[pltpu.VMEM((n_b, c2, d), F32)],
        compiler_params=pltpu.CompilerParams(dimension_semantics=("arbitrary", "arbitrary"),
                                             vmem_limit_bytes=56 * 1024 * 1024),
        name="ssm_projection",
    )(x4, mod3, norm_g.reshape(1, d), w_t)


def _cmul(ar, ai, xr, xi):
    return ar * xr - ai * xi, ar * xi + ai * xr


def _s5_group_kernel(x_ref, lhs_ref, wout_ref, ap_ref, y_ref, *, n_b, c2, n_top):
    p = SSM_STATE
    rows = CHUNK * SSM_GROUP
    n = SUPER * n_b * c2
    x = x_ref[...].reshape(rows, n)
    r = jnp.dot(lhs_ref[...], x, preferred_element_type=F32)
    y_intra, c_re, c_im = r[:rows], r[rows:rows + p], r[rows + p:]

    def mult(i):
        return ap_ref[i, :p, :c2], ap_ref[i, p:, :c2]

    lane = lax.broadcasted_iota(jnp.int32, (p, c2), 1)

    def shifted(a, sh):
        return jnp.where(lane >= sh, pltpu.roll(a, sh, 1), 0.0)

    pieces = [[None] * n_b for _ in range(SUPER)]
    for b in range(n_b):
        piece = lambda a, j2: a[:, (j2 * n_b + b) * c2:(j2 * n_b + b + 1) * c2]
        a_re, a_im = mult(1)
        e_re = jnp.zeros((p, c2), F32)
        e_im = jnp.zeros((p, c2), F32)
        local = []
        for j2 in range(SUPER):
            local.append((e_re, e_im))
            t_re, t_im = _cmul(a_re, a_im, e_re, e_im)
            e_re, e_im = t_re + piece(c_re, j2), t_im + piece(c_im, j2)
        for i in range(n_top):
            m_re, m_im = mult(SUPER + i)
            t_re, t_im = _cmul(m_re, m_im, shifted(e_re, 1 << i), shifted(e_im, 1 << i))
            e_re, e_im = e_re + t_re, e_im + t_im
        s_re, s_im = shifted(e_re, 1), shifted(e_im, 1)
        for j2 in range(SUPER):
            if j2 == 0:
                f_re, f_im = s_re, s_im
            else:
                m_re, m_im = mult(j2)
                t_re, t_im = _cmul(m_re, m_im, s_re, s_im)
                f_re, f_im = local[j2][0] + t_re, local[j2][1] + t_im
            pieces[j2][b] = jnp.concatenate([f_re, f_im], axis=0)
    s_in = jnp.concatenate([pieces[j2][b] for j2 in range(SUPER) for b in range(n_b)], axis=1)
    y = y_intra + jnp.dot(wout_ref[...], s_in.astype(BF16), preferred_element_type=F32)
    y_ref[...] = y.reshape(CHUNK, SSM_GROUP, n).astype(y_ref.dtype)


def _s5_scan(u_t, lhs, wout, ap, n_b, c2, n_top):
    g, h = SSM_GROUPS, SSM_GROUP
    n = u_t.shape[-1]
    grp = lambda *shape: pl.BlockSpec((None,) + shape, lambda i: (i,) + (0,) * len(shape))
    return pl.pallas_call(
        functools.partial(_s5_group_kernel, n_b=n_b, c2=c2, n_top=n_top),
        out_shape=jax.ShapeDtypeStruct((g, CHUNK, h, n), BF16),
        grid=(g,),
        in_specs=[pl.BlockSpec((CHUNK, h, n), lambda i: (0, i, 0)),
                  grp(*lhs.shape[1:]), grp(*wout.shape[1:]), grp(*ap.shape[1:])],
        out_specs=grp(CHUNK, h, n),
        compiler_params=pltpu.CompilerParams(dimension_semantics=("arbitrary",),
                                             vmem_limit_bytes=48 * 1024 * 1024),
        name="s5_chunk_scan",
    )(u_t, lhs, wout, ap)


def _glu_kernel(y_ref, zs_ref, w_ref, b_ref, o_ref, *, n_b, c2):
    g, _, h, n = y_ref.shape
    for jj in range(PHASES):
        y = jax.nn.gelu(y_ref[:, jj].reshape(g * h, n).astype(F32))
        t = jnp.dot(w_ref[...], y.astype(BF16), preferred_element_type=F32) + b_ref[...]
        gated = (y * jax.nn.sigmoid(t) * zs_ref[jj].astype(F32)).T
        for b in range(n_b):
            o_ref[b, :, jj, :] = gated[b * c2:(b + 1) * c2]


def _glu(y_t, zs_t, w_glu, b_glu, n_b, c2):
    g, _, h, _ = y_t.shape
    lanes = n_b * c2
    halves = CHUNK // PHASES
    w_t = w_glu.T.astype(BF16)
    return pl.pallas_call(
        functools.partial(_glu_kernel, n_b=n_b, c2=c2),
        out_shape=jax.ShapeDtypeStruct((n_b, c2, SUPER * CHUNK, SSM_WIDTH), F32),
        grid=(SUPER, halves),
        in_specs=[pl.BlockSpec((g, PHASES, h, lanes), lambda j2, hf: (0, hf, 0, j2)),
                  pl.BlockSpec((PHASES, SSM_WIDTH, lanes), lambda j2, hf: (hf, 0, j2)),
                  pl.BlockSpec(w_t.shape, lambda j2, hf: (0, 0)),
                  pl.BlockSpec((SSM_WIDTH, 1), lambda j2, hf: (0, 0))],
        out_specs=pl.BlockSpec((n_b, c2, PHASES, SSM_WIDTH), lambda j2, hf: (0, 0, j2 * halves + hf, 0)),
        compiler_params=pltpu.CompilerParams(dimension_semantics=("arbitrary", "arbitrary"),
                                             vmem_limit_bytes=48 * 1024 * 1024),
        name="s5_glu",
    )(y_t, zs_t, w_t, b_glu.reshape(SSM_WIDTH, 1))


def _out_proj_kernel(ys_ref, ym_ref, x_ref, mod_ref, wa_ref, wb_ref, o_ref):
    y = (jnp.dot(ys_ref[...].astype(BF16), wa_ref[...], preferred_element_type=F32)
         + jnp.dot(ym_ref[...], wb_ref[...], preferred_element_type=F32))
    gate = mod_ref[...][:, 2 * D_MODEL:]
    o_ref[...] = x_ref[...] + gate * y


def _out_projection(ys, ym, x2, mod3, w_out, seq):
    t, d = x2.shape
    tm = min(512, seq)
    per_b = seq // tm
    wa = w_out[:SSM_WIDTH].astype(BF16)
    wb = w_out[SSM_WIDTH:].astype(BF16)
    tok = lambda w: pl.BlockSpec((tm, w), lambda i: (i, 0))
    return pl.pallas_call(
        _out_proj_kernel,
        out_shape=jax.ShapeDtypeStruct(x2.shape, F32),
        grid=(t // tm,),
        in_specs=[tok(SSM_WIDTH), tok(MLA_WIDTH), tok(d),
                  pl.BlockSpec((None, 1, 3 * d), lambda i: (i // per_b, 0, 0)),
                  pl.BlockSpec(wa.shape, lambda i: (0, 0)),
                  pl.BlockSpec(wb.shape, lambda i: (0, 0))],
        out_specs=tok(d),
        compiler_params=pltpu.CompilerParams(dimension_semantics=("arbitrary",)),
        name="output_projection",
    )(ys, ym, x2, mod3, wa, wb)


def kernel(x, c, positions, w_ada, b_ada, norm_g, w_in, log_dt, lam_re, lam_im, b_re, b_im, c_re, c_im, d_skip,
           w_glu, b_glu, q_a_g, w_q_b, kv_a_g, w_kv_b, q_norm_g, k_norm_g, w_out):
    n_b, seq, d = x.shape
    depth = w_ada.shape[0]
    c2 = seq // (CHUNK * SUPER)
    n_top = max(int(math.log2(c2)), 0)
    assert c2 * CHUNK * SUPER == seq and (1 << n_top) == c2
    rope = _rope_tables(positions)
    for l in range(depth):
        mod3 = _modulation(c, w_ada[l], b_ada[l]).reshape(n_b, 1, 3 * d)
        x2 = x.reshape(n_b * seq, d)
        lhs, wout, ap = _s5_coefficients(log_dt[l], lam_re[l], lam_im[l], b_re[l], b_im[l], c_re[l], c_im[l],
                                         d_skip[l], n_top)
        u_t, zs_t = _ssm_projection(x.reshape(n_b, c2, SUPER * CHUNK, d), mod3, norm_g[l], w_in[l])
        y_t = _s5_scan(u_t, lhs, wout, ap, n_b, c2, n_top)
        ys = _glu(y_t, zs_t, w_glu[l], b_glu[l], n_b, c2).reshape(n_b * seq, SSM_WIDTH)
        qt, k, vt, zm = _mla_projection(x2, mod3, norm_g[l], w_in[l], q_a_g[l], w_q_b[l],
                                        kv_a_g[l], w_kv_b[l], q_norm_g[l], k_norm_g[l], rope, n_b, seq)
        ym = _attention(qt, k, vt, zm)
        x = _out_projection(ys, ym, x2, mod3, w_out[l], seq).reshape(n_b, seq, d)
    return x
```

```python
import functools
import math

import numpy as np
import jax
import jax.numpy as jnp
from jax import lax
from jax.experimental import pallas as pl
from jax.experimental.pallas import tpu as pltpu

F32 = jnp.float32
BF16 = jnp.bfloat16
HIGHEST = lax.Precision.HIGHEST

D_MODEL = 1024
SSM_WIDTH = 512
SSM_GROUP = 16
SSM_GROUPS = 32
SSM_STATE = 64
MLA_HEADS = 8
QK_NOPE = 64
QK_ROPE = 32
QK_HEAD = QK_NOPE + QK_ROPE
V_HEAD = 64
MLA_WIDTH = MLA_HEADS * V_HEAD
Q_LORA = 384
KV_LORA = 256
ROPE_THETA = 10000.0
EPS = 1e-6
NEG_BIG = -1e30

LANES = 128
CHUNK = 16
SUPER = 4
ROPE_HALF = QK_ROPE // 2

_NT = (((1,), (1,)), ((), ()))
_TN = (((0,), (0,)), ((), ()))


def _silu(v):
    return v * jax.nn.sigmoid(v)


def _rms(v, gain):
    return v * lax.rsqrt(jnp.mean(v * v, axis=-1, keepdims=True) + EPS) * gain


def _mod_kernel(c_ref, w_ref, b_ref, o_ref):
    act = _silu(c_ref[...])
    o_ref[...] = jnp.dot(act, w_ref[...], preferred_element_type=F32, precision=HIGHEST) + b_ref[...]


def _modulation(c, w, b):
    n_b, d = c.shape
    rows = 8
    c_pad = jnp.zeros((rows, d), F32).at[:n_b].set(c)
    tn = 512
    out = pl.pallas_call(
        _mod_kernel,
        out_shape=jax.ShapeDtypeStruct((rows, 3 * d), F32),
        grid=(3 * d // tn,),
        in_specs=[pl.BlockSpec((rows, d), lambda n: (0, 0)),
                  pl.BlockSpec((d, tn), lambda n: (0, n)),
                  pl.BlockSpec((1, tn), lambda n: (0, n))],
        out_specs=pl.BlockSpec((rows, tn), lambda n: (0, n)),
        name="adaln_modulation",
    )(c_pad, w, b.reshape(1, -1))
    return out[:n_b]


def _rope_kernel(pos_ref, freq_ref, cos_ref, sin_ref):
    ang = freq_ref[...] * pos_ref[...].astype(F32)
    cos_ref[...] = jnp.cos(ang)
    sin_ref[...] = jnp.sin(ang)


def _rope_tables(positions):
    t = positions.size
    tt = min(t, 2048)
    inv_freq = (ROPE_THETA ** (-np.arange(ROPE_HALF, dtype=np.float64) * 2.0 / QK_ROPE)).astype(np.float32)
    cos_t, sin_t = pl.pallas_call(
        _rope_kernel,
        out_shape=(jax.ShapeDtypeStruct((ROPE_HALF, t), F32),) * 2,
        grid=(t // tt,),
        in_specs=[pl.BlockSpec((1, tt), lambda i: (0, i)),
                  pl.BlockSpec((ROPE_HALF, 1), lambda i: (0, 0))],
        out_specs=(pl.BlockSpec((ROPE_HALF, tt), lambda i: (0, i)),) * 2,
        name="rope_tables",
    )(positions.reshape(1, t), jnp.asarray(inv_freq).reshape(ROPE_HALF, 1))
    cos, sin = cos_t.T, sin_t.T
    one = jnp.ones((t, LANES // 2 - ROPE_HALF), F32)
    zero = jnp.zeros_like(one)
    return (cos_t, sin_t, jnp.concatenate([cos, one, cos, one], axis=1),
            jnp.concatenate([-sin, zero, sin, zero], axis=1))


def _slab_source():
    src = np.full((LANES,), -1, np.int64)
    src[0:16] = QK_NOPE + np.arange(16)
    src[16:64] = np.arange(48)
    src[64:80] = QK_NOPE + 16 + np.arange(16)
    src[80:96] = 48 + np.arange(16)
    return src


def _to_slabs(w, per_head, src):
    k = w.shape[0]
    w3 = w.reshape(k, MLA_HEADS, per_head)
    valid = src >= 0
    cols = jnp.where(valid[None, None, :], w3[:, :, np.clip(src, 0, per_head - 1)], 0.0)
    return cols.reshape(k, MLA_HEADS * LANES)


V_ROWS = V_HEAD + 16


def _mla_proj_kernel(x_ref, mod_ref, ng_ref, w1_ref, qag_ref, kvag_ref, wqt_ref, wk_ref, wvt_ref,
                     gq_ref, gk_ref, ones_ref, cos_ref, sin_ref, cost_ref, sint_ref,
                     qt_ref, k_ref, vt_ref, zs_ref):
    d = D_MODEL
    tm = x_ref.shape[0]
    x = x_ref[...]
    mod = mod_ref[...]
    shift, scale = mod[:, :d], mod[:, d:2 * d]
    h = (_rms(x, ng_ref[...]) * (1.0 + scale) + shift).astype(BF16)
    p1 = jnp.dot(h, w1_ref[...], preferred_element_type=F32)
    o1, o2, o3 = Q_LORA, Q_LORA + KV_LORA, Q_LORA + KV_LORA + MLA_WIDTH
    cqn = _rms(p1[:, :o1], qag_ref[...]).astype(BF16)
    ckvn = _rms(p1[:, o1:o2], kvag_ref[...]).astype(BF16)
    zs_ref[...] = _silu(p1[:, o2:o3]).astype(BF16)
    kr = p1[:, o3:o3 + LANES]
    inv_dim = 1.0 / QK_HEAD

    kn = jnp.dot(ckvn, wk_ref[...], preferred_element_type=F32)
    cos, sin = cos_ref[...], sin_ref[...]
    ones = ones_ref[...]
    kr2 = jnp.concatenate([kr, kr], axis=1)
    for pair in range(MLA_HEADS // 2):
        sl = slice(2 * LANES * pair, 2 * LANES * (pair + 1))
        kp = kn[:, sl] + kr2
        ssk = jnp.dot((kp * kp).astype(BF16), ones, preferred_element_type=F32)
        kk = kp * lax.rsqrt(ssk * inv_dim + EPS) * gk_ref[:, sl]
        for e in range(2):
            kh = kk[:, LANES * e:LANES * (e + 1)]
            k_ref[2 * pair + e] = (kh * cos + pltpu.roll(kh, LANES // 2, 1) * sin).astype(BF16)

    qt = lax.dot_general(wqt_ref[...], cqn, _NT, preferred_element_type=F32)
    vt = lax.dot_general(wvt_ref[...], ckvn, _NT, preferred_element_type=F32)
    cos_t, sin_t = cost_ref[...], sint_ref[...]
    gq = jnp.concatenate([gq_ref[...]] * (tm // LANES), axis=1)
    first = lax.broadcasted_iota(jnp.int32, (V_ROWS - V_HEAD, tm), 0) == 0
    ones_rows = jnp.where(first, 1.0, 0.0).astype(BF16)
    r = ROPE_HALF
    for head in range(MLA_HEADS):
        slab = qt[LANES * head:LANES * (head + 1)]
        ss = jnp.sum(slab * slab, axis=0, keepdims=True)
        qn = slab * lax.rsqrt(ss * inv_dim + EPS) * gq
        x1, x2 = qn[0:r], qn[LANES // 2:LANES // 2 + r]
        qt_ref[head] = jnp.concatenate([x1 * cos_t - x2 * sin_t, qn[r:LANES // 2],
                                        x2 * cos_t + x1 * sin_t, qn[LANES // 2 + r:]], axis=0).astype(BF16)
        vt_ref[head, 0:V_HEAD, :] = vt[V_HEAD * head:V_HEAD * (head + 1)].astype(BF16)
        vt_ref[head, V_HEAD:, :] = ones_rows


def _mla_projection(x2, mod3, norm_g, w_in, q_a_g, w_q_b, kv_a_g, w_kv_b, q_norm_g, k_norm_g, rope, n_b, seq):
    t, d = x2.shape
    cos_t, sin_t, cos_tab, sin_tab = rope
    tm = min(256, seq)
    per_b = seq // tm
    o2 = 2 * SSM_WIDTH
    o3, o4, o5 = o2 + Q_LORA, o2 + Q_LORA + KV_LORA, o2 + Q_LORA + KV_LORA + QK_ROPE
    src = _slab_source()
    rope_src = np.where(src >= QK_NOPE, src - QK_NOPE, -1)
    kr_cols = jnp.where((rope_src >= 0)[None, :], w_in[:, o4:o5][:, np.clip(rope_src, 0, QK_ROPE - 1)], 0.0)
    w1 = jnp.concatenate([w_in[:, o2:o3], w_in[:, o3:o4], w_in[:, o5:], kr_cols], axis=1).astype(BF16)
    wqt = _to_slabs(w_q_b, QK_HEAD, src).T.astype(BF16)
    nope_src = np.where((src >= 0) & (src < QK_NOPE), src, -1)
    wk = _to_slabs(w_kv_b, QK_NOPE + V_HEAD, nope_src).astype(BF16)
    wvt = w_kv_b.reshape(KV_LORA, MLA_HEADS, QK_NOPE + V_HEAD)[:, :, QK_NOPE:].reshape(KV_LORA, MLA_WIDTH).T.astype(BF16)
    valid = (src >= 0)
    q_scale = math.log2(math.e) / math.sqrt(QK_HEAD)
    gq = jnp.where(valid, q_norm_g[np.clip(src, 0, QK_HEAD - 1)], 0.0) * q_scale
    gq = jnp.broadcast_to(gq[:, None], (LANES, LANES))
    gk = jnp.tile(jnp.where(valid, k_norm_g[np.clip(src, 0, QK_HEAD - 1)], 0.0), MLA_HEADS).reshape(1, -1)
    ones = jnp.asarray(np.kron(np.eye(2), np.ones((LANES, LANES))), BF16)
    hw = MLA_HEADS * LANES
    const = lambda shape: pl.BlockSpec(shape, lambda i: (0,) * len(shape))
    tok = lambda i: (i // per_b, 0, 0, i % per_b)
    return pl.pallas_call(
        _mla_proj_kernel,
        out_shape=(jax.ShapeDtypeStruct((n_b, MLA_HEADS, LANES, seq), BF16),
                   jax.ShapeDtypeStruct((n_b, MLA_HEADS, seq, LANES), BF16),
                   jax.ShapeDtypeStruct((n_b, MLA_HEADS, V_ROWS, seq), BF16),
                   jax.ShapeDtypeStruct((t, MLA_WIDTH), BF16)),
        grid=(t // tm,),
        in_specs=[pl.BlockSpec((tm, d), lambda i: (i, 0)),
                  pl.BlockSpec((None, 1, 3 * d), lambda i: (i // per_b, 0, 0)),
                  const((1, d)), const(w1.shape), const((1, Q_LORA)), const((1, KV_LORA)),
                  const(wqt.shape), const(wk.shape), const(wvt.shape), const((LANES, LANES)), const((1, hw)),
                  const(ones.shape),
                  pl.BlockSpec((tm, LANES), lambda i: (i, 0)),
                  pl.BlockSpec((tm, LANES), lambda i: (i, 0)),
                  pl.BlockSpec((ROPE_HALF, tm), lambda i: (0, i)),
                  pl.BlockSpec((ROPE_HALF, tm), lambda i: (0, i))],
        out_specs=(pl.BlockSpec((None, MLA_HEADS, LANES, tm), tok),
                   pl.BlockSpec((None, MLA_HEADS, tm, LANES), lambda i: (i // per_b, 0, i % per_b, 0)),
                   pl.BlockSpec((None, MLA_HEADS, V_ROWS, tm), tok),
                   pl.BlockSpec((tm, MLA_WIDTH), lambda i: (i, 0))),
        compiler_params=pltpu.CompilerParams(dimension_semantics=("arbitrary",),
                                             vmem_limit_bytes=48 * 1024 * 1024),
        name="mla_projection",
    )(x2, mod3, norm_g.reshape(1, d), w1, q_a_g.reshape(1, -1), kv_a_g.reshape(1, -1), wqt, wk, wvt, gq, gk, ones,
      cos_tab, sin_tab, cos_t, sin_t)


ATTN_TQ = 1024
ATTN_TK = 256
HEADS_PER_STEP = 2


def _attn_kernel(qt_ref, k_ref, vt_ref, zs_ref, o_ref, s_scr):
    tq, tk = ATTN_TQ, ATTN_TK
    r = tq // tk
    assert r * tk == tq and r % 2 == 0
    qi = pl.program_id(2)

    def scores(slot, t, lo=0):
        start = pl.multiple_of(t * tk, tk)
        for e in range(HEADS_PER_STEP):
            s_scr[slot, e, :, lo:] = jnp.dot(k_ref[e, pl.ds(start, tk), :], qt_ref[e, :, lo:],
                                             preferred_element_type=F32)

    def process(slot, t, carry, lo=0, diagonal=False):
        start = pl.multiple_of(t * tk, tk)
        out = []
        for e in range(HEADS_PER_STEP):
            m, acc = carry[e]
            s = s_scr[slot, e, :, lo:]
            if diagonal:
                blk = s[:, :tk]
                ok = lax.broadcasted_iota(jnp.int32, blk.shape, 0) <= lax.broadcasted_iota(jnp.int32, blk.shape, 1)
                blk = jnp.where(ok, blk, NEG_BIG)
                s = jnp.concatenate([blk, s[:, tk:]], axis=1) if s.shape[1] > tk else blk
            m_old = m[:, lo:]
            m_new = jnp.maximum(m_old, jnp.max(s, axis=0, keepdims=True))
            p = jnp.exp2(s - m_new).astype(BF16)
            acc_new = (jnp.exp2(m_old - m_new) * acc[:, lo:]
                       + jnp.dot(vt_ref[e, :, pl.ds(start, tk)], p, preferred_element_type=F32))
            if lo:
                m_new = jnp.concatenate([m[:, :lo], m_new], axis=1)
                acc_new = jnp.concatenate([acc[:, :lo], acc_new], axis=1)
            out.append((m_new, acc_new))
        return tuple(out)

    def body(i, carry):
        base = i * r
        for u in range(r):
            scores((u + 1) % 2, base + u + 1)
            carry = process(u % 2, base + u, carry)
        return carry

    init = tuple((jnp.full((1, tq), NEG_BIG, F32), jnp.zeros((V_ROWS, tq), F32)) for _ in range(HEADS_PER_STEP))
    scores(0, 0)
    carry = lax.fori_loop(0, qi, body, init)
    base = qi * r
    for u in range(r):
        if u + 1 < r:
            scores((u + 1) % 2, base + u + 1, lo=(u + 1) * tk)
        carry = process(u % 2, base + u, carry, lo=u * tk, diagonal=True)
    _attn_finish(tuple(c[1] for c in carry), zs_ref, o_ref)


def _attn_finish(accs, zs_ref, o_ref):
    outs = [(acc[:V_HEAD] / acc[V_HEAD:V_HEAD + 1]).T for acc in accs]
    o_ref[...] = (jnp.concatenate(outs, axis=1) * zs_ref[...].astype(F32)).astype(o_ref.dtype)


def _attn_bounded_kernel(qt_ref, k_ref, vt_ref, zs_ref, o_ref, s_scr):
    tq, tk = ATTN_TQ, ATTN_TK
    r = tq // tk
    assert r * tk == tq and r % 2 == 0
    qi = pl.program_id(2)

    def scores(slot, t, lo=0):
        start = pl.multiple_of(t * tk, tk)
        for e in range(HEADS_PER_STEP):
            s_scr[slot, e, :, lo:] = jnp.dot(k_ref[e, pl.ds(start, tk), :], qt_ref[e, :, lo:],
                                             preferred_element_type=F32)

    def process(slot, t, accs, lo=0, diagonal=False):
        start = pl.multiple_of(t * tk, tk)
        out = []
        for e in range(HEADS_PER_STEP):
            p = jnp.exp2(s_scr[slot, e, :, lo:])
            if diagonal:
                blk = p[:, :tk]
                ok = lax.broadcasted_iota(jnp.int32, blk.shape, 0) <= lax.broadcasted_iota(jnp.int32, blk.shape, 1)
                blk = jnp.where(ok, blk, 0.0)
                p = jnp.concatenate([blk, p[:, tk:]], axis=1) if p.shape[1] > tk else blk
            pv = jnp.dot(vt_ref[e, :, pl.ds(start, tk)], p.astype(BF16), preferred_element_type=F32)
            acc = accs[e]
            out.append(acc + pv if lo == 0 else jnp.concatenate([acc[:, :lo], acc[:, lo:] + pv], axis=1))
        return tuple(out)

    def body(i, accs):
        base = i * r
        for u in range(r):
            scores((u + 1) % 2, base + u + 1)
            accs = process(u % 2, base + u, accs)
        return accs

    scores(0, 0)
    accs = lax.fori_loop(0, qi, body, tuple(jnp.zeros((V_ROWS, tq), F32) for _ in range(HEADS_PER_STEP)))
    base = qi * r
    for u in range(r):
        if u + 1 < r:
            scores((u + 1) % 2, base + u + 1, lo=(u + 1) * tk)
        accs = process(u % 2, base + u, accs, lo=u * tk, diagonal=True)
    _attn_finish(accs, zs_ref, o_ref)


SCORE_BOUND_LOG2 = 60.0


def _attention(qt, k, vt, zs, score_bound):
    n_b, n_h, seq, _ = k.shape
    tq = ATTN_TQ
    nq = seq // tq
    hp = HEADS_PER_STEP
    width = hp * V_HEAD

    def call(body, scratch):
        return pl.pallas_call(
            body,
            out_shape=jax.ShapeDtypeStruct(zs.shape, BF16),
            grid=(n_b, n_h // hp, nq),
            in_specs=[pl.BlockSpec((None, hp, LANES, tq), lambda b, h, i: (b, h, 0, i)),
                      pl.BlockSpec((None, hp, seq, LANES), lambda b, h, i: (b, h, 0, 0)),
                      pl.BlockSpec((None, hp, V_ROWS, seq), lambda b, h, i: (b, h, 0, 0)),
                      pl.BlockSpec((tq, width), lambda b, h, i: (b * nq + i, h))],
            out_specs=pl.BlockSpec((tq, width), lambda b, h, i: (b * nq + i, h)),
            scratch_shapes=scratch,
            compiler_params=pltpu.CompilerParams(dimension_semantics=("arbitrary", "arbitrary", "arbitrary"),
                                                 vmem_limit_bytes=48 * 1024 * 1024),
            name="causal_attention",
        )(qt, k, vt, zs)

    scratch = [pltpu.VMEM((2, hp, ATTN_TK, tq), F32)]
    return lax.cond(score_bound <= SCORE_BOUND_LOG2,
                    lambda: call(_attn_bounded_kernel, scratch),
                    lambda: call(_attn_kernel, scratch))


def _s5_coef_kernel(lre_ref, lim_ref, ldt_ref, bre_ref, bim_ref, cre_ref, cim_ref, d_ref,
                    kmat_ref, winre_ref, winim_ref, cpre_ref, cpim_ref, apre_ref, apim_ref, *, n_top):
    lre, lim = lre_ref[...], lim_ref[...]
    dt = jnp.exp(ldt_ref[...])
    zr, zi = lre * dt, lim * dt

    def power(kk):
        mag = jnp.exp(kk * zr)
        return mag * jnp.cos(kk * zi), mag * jnp.sin(kk * zi)

    lb_re, lb_im = power(1.0)
    nr, ni = lb_re - 1.0, lb_im
    den = lre * lre + lim * lim
    f_re = (nr * lre + ni * lim) / den
    f_im = (ni * lre - nr * lim) / den
    bre, bim = bre_ref[...], bim_ref[...]
    bb_re = f_re * bre - f_im * bim
    bb_im = f_re * bim + f_im * bre
    cre, cim = cre_ref[...], cim_ref[...]
    cp_re, cp_im = [], []
    for kk in range(CHUNK + 1):
        pr, pi = power(float(kk))
        cr, ci = cre * pr - cim * pi, cre * pi + cim * pr
        cpre_ref[kk] = cr
        cpim_ref[kk] = ci
        cp_re.append(cr)
        cp_im.append(ci)
    all_re = jnp.concatenate(cp_re[:CHUNK], axis=0)
    all_im = jnp.concatenate(cp_im[:CHUNK], axis=0)
    kmat = (lax.dot_general(all_re, bb_re, _NT, preferred_element_type=F32, precision=HIGHEST)
            - lax.dot_general(all_im, bb_im, _NT, preferred_element_type=F32, precision=HIGHEST))
    h = SSM_GROUP
    eye = lax.broadcasted_iota(jnp.int32, (h, h), 0) == lax.broadcasted_iota(jnp.int32, (h, h), 1)
    kmat_ref[0:h, :] = kmat[0:h] + jnp.where(eye, d_ref[...], 0.0)
    kmat_ref[h:, :] = kmat[h:]
    for j in range(CHUNK):
        pr, pi = power(float(CHUNK - 1 - j))
        winre_ref[j] = pr * bb_re - pi * bb_im
        winim_ref[j] = pr * bb_im + pi * bb_re
    steps = [float(CHUNK * m) for m in range(SUPER)] + [float(CHUNK * SUPER * (1 << i)) for i in range(n_top)]
    for i, kk in enumerate(steps):
        pr, pi = power(kk)
        apre_ref[i:i + 1, :] = pr
        apim_ref[i:i + 1, :] = pi


def _s5_coefficients(log_dt, lam_re, lam_im, b_re, b_im, c_re, c_im, d_skip, n_top):
    g, p, h = SSM_GROUPS, SSM_STATE, SSM_GROUP
    n_ap = SUPER + n_top
    row = lambda a: a.reshape(g, 1, p)
    grp = lambda *shape: pl.BlockSpec((None,) + shape, lambda i: (i,) + (0,) * len(shape))
    outs = pl.pallas_call(
        functools.partial(_s5_coef_kernel, n_top=n_top),
        out_shape=(jax.ShapeDtypeStruct((g, CHUNK * h, h), F32),
                   jax.ShapeDtypeStruct((g, CHUNK, h, p), F32), jax.ShapeDtypeStruct((g, CHUNK, h, p), F32),
                   jax.ShapeDtypeStruct((g, CHUNK + 1, h, p), F32), jax.ShapeDtypeStruct((g, CHUNK + 1, h, p), F32),
                   jax.ShapeDtypeStruct((g, n_ap, p), F32), jax.ShapeDtypeStruct((g, n_ap, p), F32)),
        grid=(g,),
        in_specs=[grp(1, p), grp(1, p), grp(1, p), grp(h, p), grp(h, p), grp(h, p), grp(h, p), grp(h, 1)],
        out_specs=(grp(CHUNK * h, h), grp(CHUNK, h, p), grp(CHUNK, h, p), grp(CHUNK + 1, h, p),
                   grp(CHUNK + 1, h, p), grp(n_ap, p), grp(n_ap, p)),
        name="s5_coefficients",
    )(row(lam_re), row(lam_im), jnp.broadcast_to(log_dt[:, None, None], (g, 1, p)),
      b_re.transpose(0, 2, 1), b_im.transpose(0, 2, 1), c_re, c_im, d_skip.reshape(g, h, 1))
    kmat, win_re, win_im, cp_re, cp_im, ap_re, ap_im = outs
    lag = np.arange(CHUNK)[:, None] - np.arange(CHUNK)[None, :]
    toe = kmat.reshape(g, CHUNK, h, h)[:, np.clip(lag, 0, None)]
    toe = jnp.where((lag >= 0)[None, :, :, None, None], toe, 0.0)
    m_t = toe.transpose(0, 1, 3, 2, 4).reshape(g, CHUNK * h, CHUNK * h)
    win = jnp.concatenate([win_re.transpose(0, 3, 1, 2).reshape(g, p, CHUNK * h),
                           win_im.transpose(0, 3, 1, 2).reshape(g, p, CHUNK * h)], axis=1)
    lhs = jnp.concatenate([m_t, win], axis=1).astype(BF16)
    wout = jnp.concatenate([cp_re[:, 1:], -cp_im[:, 1:]], axis=-1).reshape(g, CHUNK * h, 2 * p).astype(BF16)
    ap = jnp.concatenate([ap_re, ap_im], axis=-1)
    ap = jnp.broadcast_to(ap[..., None], (g, n_ap, 2 * p, LANES))
    return lhs, wout, ap


PHASES = 8


def _ssm_proj_kernel(x_ref, mod_ref, ng_ref, w_ref, u_ref, z_ref, rows_scr, *, n_b):
    d = D_MODEL
    for jj in range(PHASES):
        hs = []
        for b in range(n_b):
            mod = mod_ref[b]
            shift, scale = mod[:, :d], mod[:, d:2 * d]
            rows_scr[b] = x_ref[b, :, jj, :]
            hs.append((_rms(rows_scr[b], ng_ref[...]) * (1.0 + scale) + shift).astype(BF16))
        h = jnp.concatenate(hs, axis=0)
        r = lax.dot_general(w_ref[...], h, _NT, preferred_element_type=F32)
        u_ref[jj] = r[:SSM_WIDTH].astype(BF16)
        z_ref[jj] = _silu(r[SSM_WIDTH:]).astype(BF16)


def _ssm_projection(x4, mod3, norm_g, w_in):
    n_b, c2, _, d = x4.shape
    lanes = n_b * c2
    halves = CHUNK // PHASES
    w_t = w_in[:, :2 * SSM_WIDTH].T.astype(BF16)
    out = jax.ShapeDtypeStruct((CHUNK, SSM_WIDTH, SUPER * lanes), BF16)
    out_spec = pl.BlockSpec((PHASES, SSM_WIDTH, lanes), lambda j2, hf: (hf, 0, j2))
    return pl.pallas_call(
        functools.partial(_ssm_proj_kernel, n_b=n_b),
        out_shape=(out, out),
        grid=(SUPER, halves),
        in_specs=[pl.BlockSpec((n_b, c2, PHASES, d), lambda j2, hf: (0, 0, j2 * halves + hf, 0)),
                  pl.BlockSpec((n_b, 1, 3 * d), lambda j2, hf: (0, 0, 0)),
                  pl.BlockSpec((1, d), lambda j2, hf: (0, 0)),
                  pl.BlockSpec(w_t.shape, lambda j2, hf: (0, 0))],
        out_specs=(out_spec, out_spec),
        scratch_shapes=[pltpu.VMEM((n_b, c2, d), F32)],
        compiler_params=pltpu.CompilerParams(dimension_semantics=("arbitrary", "arbitrary"),
                                             vmem_limit_bytes=56 * 1024 * 1024),
        name="ssm_projection",
    )(x4, mod3, norm_g.reshape(1, d), w_t)


def _cmul(ar, ai, xr, xi):
    return ar * xr - ai * xi, ar * xi + ai * xr


def _s5_group_kernel(x_ref, lhs_ref, wout_ref, ap_ref, y_ref, *, n_b, c2, n_top):
    p = SSM_STATE
    rows = CHUNK * SSM_GROUP
    n = SUPER * n_b * c2
    x = x_ref[...].reshape(rows, n)
    r = jnp.dot(lhs_ref[...], x, preferred_element_type=F32)
    y_intra, c_re, c_im = r[:rows], r[rows:rows + p], r[rows + p:]

    def mult(i):
        return ap_ref[i, :p, :c2], ap_ref[i, p:, :c2]

    lane = lax.broadcasted_iota(jnp.int32, (p, c2), 1)

    def shifted(a, sh):
        return jnp.where(lane >= sh, pltpu.roll(a, sh, 1), 0.0)

    pieces = [[None] * n_b for _ in range(SUPER)]
    for b in range(n_b):
        piece = lambda a, j2: a[:, (j2 * n_b + b) * c2:(j2 * n_b + b + 1) * c2]
        a_re, a_im = mult(1)
        e_re = jnp.zeros((p, c2), F32)
        e_im = jnp.zeros((p, c2), F32)
        local = []
        for j2 in range(SUPER):
            local.append((e_re, e_im))
            t_re, t_im = _cmul(a_re, a_im, e_re, e_im)
            e_re, e_im = t_re + piece(c_re, j2), t_im + piece(c_im, j2)
        for i in range(n_top):
            m_re, m_im = mult(SUPER + i)
            t_re, t_im = _cmul(m_re, m_im, shifted(e_re, 1 << i), shifted(e_im, 1 << i))
            e_re, e_im = e_re + t_re, e_im + t_im
        s_re, s_im = shifted(e_re, 1), shifted(e_im, 1)
        for j2 in range(SUPER):
            if j2 == 0:
                f_re, f_im = s_re, s_im
            else:
                m_re, m_im = mult(j2)
                t_re, t_im = _cmul(m_re, m_im, s_re, s_im)
                f_re, f_im = local[j2][0] + t_re, local[j2][1] + t_im
            pieces[j2][b] = jnp.concatenate([f_re, f_im], axis=0)
    s_in = jnp.concatenate([pieces[j2][b] for j2 in range(SUPER) for b in range(n_b)], axis=1)
    y = y_intra + jnp.dot(wout_ref[...], s_in.astype(BF16), preferred_element_type=F32)
    y_ref[...] = y.reshape(CHUNK, SSM_GROUP, n).astype(y_ref.dtype)


def _s5_scan(u_t, lhs, wout, ap, n_b, c2, n_top):
    g, h = SSM_GROUPS, SSM_GROUP
    n = u_t.shape[-1]
    grp = lambda *shape: pl.BlockSpec((None,) + shape, lambda i: (i,) + (0,) * len(shape))
    return pl.pallas_call(
        functools.partial(_s5_group_kernel, n_b=n_b, c2=c2, n_top=n_top),
        out_shape=jax.ShapeDtypeStruct((g, CHUNK, h, n), BF16),
        grid=(g,),
        in_specs=[pl.BlockSpec((CHUNK, h, n), lambda i: (0, i, 0)),
                  grp(*lhs.shape[1:]), grp(*wout.shape[1:]), grp(*ap.shape[1:])],
        out_specs=grp(CHUNK, h, n),
        compiler_params=pltpu.CompilerParams(dimension_semantics=("arbitrary",),
                                             vmem_limit_bytes=48 * 1024 * 1024),
        name="s5_chunk_scan",
    )(u_t, lhs, wout, ap)


def _glu_kernel(y_ref, zs_ref, w_ref, b_ref, o_ref, *, n_b, c2):
    g, _, h, n = y_ref.shape
    for jj in range(PHASES):
        y = jax.nn.gelu(y_ref[:, jj].reshape(g * h, n).astype(F32))
        t = jnp.dot(w_ref[...], y.astype(BF16), preferred_element_type=F32) + b_ref[...]
        gated = (y * jax.nn.sigmoid(t) * zs_ref[jj].astype(F32)).T
        for b in range(n_b):
            o_ref[b, :, jj, :] = gated[b * c2:(b + 1) * c2]


def _glu(y_t, zs_t, w_glu, b_glu, n_b, c2):
    g, _, h, _ = y_t.shape
    lanes = n_b * c2
    halves = CHUNK // PHASES
    w_t = w_glu.T.astype(BF16)
    return pl.pallas_call(
        functools.partial(_glu_kernel, n_b=n_b, c2=c2),
        out_shape=jax.ShapeDtypeStruct((n_b, c2, SUPER * CHUNK, SSM_WIDTH), F32),
        grid=(SUPER, halves),
        in_specs=[pl.BlockSpec((g, PHASES, h, lanes), lambda j2, hf: (0, hf, 0, j2)),
                  pl.BlockSpec((PHASES, SSM_WIDTH, lanes), lambda j2, hf: (hf, 0, j2)),
                  pl.BlockSpec(w_t.shape, lambda j2, hf: (0, 0)),
                  pl.BlockSpec((SSM_WIDTH, 1), lambda j2, hf: (0, 0))],
        out_specs=pl.BlockSpec((n_b, c2, PHASES, SSM_WIDTH), lambda j2, hf: (0, 0, j2 * halves + hf, 0)),
        compiler_params=pltpu.CompilerParams(dimension_semantics=("arbitrary", "arbitrary"),
                                             vmem_limit_bytes=48 * 1024 * 1024),
        name="s5_glu",
    )(y_t, zs_t, w_t, b_glu.reshape(SSM_WIDTH, 1))


def _out_proj_kernel(ys_ref, ym_ref, x_ref, mod_ref, wa_ref, wb_ref, o_ref):
    y = (jnp.dot(ys_ref[...].astype(BF16), wa_ref[...], preferred_element_type=F32)
         + jnp.dot(ym_ref[...], wb_ref[...], preferred_element_type=F32))
    gate = mod_ref[...][:, 2 * D_MODEL:]
    o_ref[...] = x_ref[...] + gate * y


def _out_projection(ys, ym, x2, mod3, w_out, seq):
    t, d = x2.shape
    tm = min(512, seq)
    per_b = seq // tm
    wa = w_out[:SSM_WIDTH].astype(BF16)
    wb = w_out[SSM_WIDTH:].astype(BF16)
    tok = lambda w: pl.BlockSpec((tm, w), lambda i: (i, 0))
    return pl.pallas_call(
        _out_proj_kernel,
        out_shape=jax.ShapeDtypeStruct(x2.shape, F32),
        grid=(t // tm,),
        in_specs=[tok(SSM_WIDTH), tok(MLA_WIDTH), tok(d),
                  pl.BlockSpec((None, 1, 3 * d), lambda i: (i // per_b, 0, 0)),
                  pl.BlockSpec(wa.shape, lambda i: (0, 0)),
                  pl.BlockSpec(wb.shape, lambda i: (0, 0))],
        out_specs=tok(d),
        compiler_params=pltpu.CompilerParams(dimension_semantics=("arbitrary",)),
        name="output_projection",
    )(ys, ym, x2, mod3, wa, wb)


def kernel(x, c, positions, w_ada, b_ada, norm_g, w_in, log_dt, lam_re, lam_im, b_re, b_im, c_re, c_im, d_skip,
           w_glu, b_glu, q_a_g, w_q_b, kv_a_g, w_kv_b, q_norm_g, k_norm_g, w_out):
    n_b, seq, d = x.shape
    depth = w_ada.shape[0]
    c2 = seq // (CHUNK * SUPER)
    n_top = max(int(math.log2(c2)), 0)
    assert c2 * CHUNK * SUPER == seq and (1 << n_top) == c2
    rope = _rope_tables(positions)
    for l in range(depth):
        mod3 = _modulation(c, w_ada[l], b_ada[l]).reshape(n_b, 1, 3 * d)
        x2 = x.reshape(n_b * seq, d)
        lhs, wout, ap = _s5_coefficients(log_dt[l], lam_re[l], lam_im[l], b_re[l], b_im[l], c_re[l], c_im[l],
                                         d_skip[l], n_top)
        u_t, zs_t = _ssm_projection(x.reshape(n_b, c2, SUPER * CHUNK, d), mod3, norm_g[l], w_in[l])
        y_t = _s5_scan(u_t, lhs, wout, ap, n_b, c2, n_top)
        ys = _glu(y_t, zs_t, w_glu[l], b_glu[l], n_b, c2).reshape(n_b * seq, SSM_WIDTH)
        qt, k, vt, zm = _mla_projection(x2, mod3, norm_g[l], w_in[l], q_a_g[l], w_q_b[l],
                                        kv_a_g[l], w_kv_b[l], q_norm_g[l], k_norm_g[l], rope, n_b, seq)
        score_bound = (1.05 * math.sqrt(QK_HEAD) * math.log2(math.e)
                       * jnp.max(jnp.abs(q_norm_g[l])) * jnp.max(jnp.abs(k_norm_g[l])))
        ym = _attention(qt, k, vt, zm, score_bound)
        x = _out_projection(ys, ym, x2, mod3, w_out[l], seq).reshape(n_b, seq, d)
    return x
```

```python
import functools
import math

import numpy as np
import jax
import jax.numpy as jnp
from jax import lax
from jax.experimental import pallas as pl
from jax.experimental.pallas import tpu as pltpu

F32 = jnp.float32
BF16 = jnp.bfloat16
HIGHEST = lax.Precision.HIGHEST

D_MODEL = 1024
SSM_WIDTH = 512
SSM_GROUP = 16
SSM_GROUPS = 32
SSM_STATE = 64
MLA_HEADS = 8
QK_NOPE = 64
QK_ROPE = 32
QK_HEAD = QK_NOPE + QK_ROPE
V_HEAD = 64
MLA_WIDTH = MLA_HEADS * V_HEAD
Q_LORA = 384
KV_LORA = 256
ROPE_THETA = 10000.0
EPS = 1e-6
NEG_BIG = -1e30

LANES = 128
CHUNK = 16
SUPER = 4
ROPE_HALF = QK_ROPE // 2

_NT = (((1,), (1,)), ((), ()))
_TN = (((0,), (0,)), ((), ()))


def _silu(v):
    return v * jax.nn.sigmoid(v)


def _rms(v, gain):
    return v * lax.rsqrt(jnp.mean(v * v, axis=-1, keepdims=True) + EPS) * gain


def _mod_kernel(c_ref, w_ref, b_ref, o_ref):
    act = _silu(c_ref[...])
    o_ref[...] = jnp.dot(act, w_ref[...], preferred_element_type=F32, precision=HIGHEST) + b_ref[...]


def _modulation(c, w, b):
    n_b, d = c.shape
    rows = 8
    c_pad = jnp.zeros((rows, d), F32).at[:n_b].set(c)
    tn = 512
    out = pl.pallas_call(
        _mod_kernel,
        out_shape=jax.ShapeDtypeStruct((rows, 3 * d), F32),
        grid=(3 * d // tn,),
        in_specs=[pl.BlockSpec((rows, d), lambda n: (0, 0)),
                  pl.BlockSpec((d, tn), lambda n: (0, n)),
                  pl.BlockSpec((1, tn), lambda n: (0, n))],
        out_specs=pl.BlockSpec((rows, tn), lambda n: (0, n)),
        name="adaln_modulation",
    )(c_pad, w, b.reshape(1, -1))
    return out[:n_b]


def _rope_kernel(pos_ref, freq_ref, cos_ref, sin_ref):
    ang = freq_ref[...] * pos_ref[...].astype(F32)
    cos_ref[...] = jnp.cos(ang)
    sin_ref[...] = jnp.sin(ang)


def _rope_tables(positions):
    t = positions.size
    tt = min(t, 2048)
    inv_freq = (ROPE_THETA ** (-np.arange(ROPE_HALF, dtype=np.float64) * 2.0 / QK_ROPE)).astype(np.float32)
    return pl.pallas_call(
        _rope_kernel,
        out_shape=(jax.ShapeDtypeStruct((ROPE_HALF, t), F32),) * 2,
        grid=(t // tt,),
        in_specs=[pl.BlockSpec((1, tt), lambda i: (0, i)),
                  pl.BlockSpec((ROPE_HALF, 1), lambda i: (0, 0))],
        out_specs=(pl.BlockSpec((ROPE_HALF, tt), lambda i: (0, i)),) * 2,
        name="rope_tables",
    )(positions.reshape(1, t), jnp.asarray(inv_freq).reshape(ROPE_HALF, 1))


def _slab_source():
    src = np.full((LANES,), -1, np.int64)
    src[0:16] = QK_NOPE + np.arange(16)
    src[16:64] = np.arange(48)
    src[64:80] = QK_NOPE + 16 + np.arange(16)
    src[80:96] = 48 + np.arange(16)
    return src


def _to_slabs(w, per_head, src):
    k = w.shape[0]
    w3 = w.reshape(k, MLA_HEADS, per_head)
    valid = src >= 0
    cols = jnp.where(valid[None, None, :], w3[:, :, np.clip(src, 0, per_head - 1)], 0.0)
    return cols.reshape(k, MLA_HEADS * LANES)


V_ROWS = V_HEAD + 16


def _mla_proj_kernel(x_ref, mod_ref, ng_ref, w1_ref, qag_ref, kvag_ref, wqt_ref, wkt_ref, wkrt_ref, wvt_ref,
                     gq_ref, gk_ref, cost_ref, sint_ref, qt_ref, k_ref, vt_ref, zs_ref):
    d = D_MODEL
    tm = x_ref.shape[0]
    x = x_ref[...]
    mod = mod_ref[...]
    shift, scale = mod[:, :d], mod[:, d:2 * d]
    h = (_rms(x, ng_ref[...]) * (1.0 + scale) + shift).astype(BF16)
    p1 = jnp.dot(h, w1_ref[...], preferred_element_type=F32)
    o1, o2 = Q_LORA, Q_LORA + KV_LORA
    cqn = _rms(p1[:, :o1], qag_ref[...]).astype(BF16)
    ckvn = _rms(p1[:, o1:o2], kvag_ref[...]).astype(BF16)
    zs_ref[...] = _silu(p1[:, o2:]).astype(BF16)

    qt = lax.dot_general(wqt_ref[...], cqn, _NT, preferred_element_type=F32)
    kt = lax.dot_general(wkt_ref[...], ckvn, _NT, preferred_element_type=F32)
    krt = lax.dot_general(wkrt_ref[...], h, _NT, preferred_element_type=F32)
    vt = lax.dot_general(wvt_ref[...], ckvn, _NT, preferred_element_type=F32)
    cos_t, sin_t = cost_ref[...], sint_ref[...]
    gq = jnp.concatenate([gq_ref[...]] * (tm // LANES), axis=1)
    gk = jnp.concatenate([gk_ref[...]] * (tm // LANES), axis=1)
    first = lax.broadcasted_iota(jnp.int32, (V_ROWS - V_HEAD, tm), 0) == 0
    ones_rows = jnp.where(first, 1.0, 0.0).astype(BF16)
    r, half = ROPE_HALF, LANES // 2

    def norm_rope(slab, gain):
        ss = jnp.sum(slab * slab, axis=0, keepdims=True)
        n = slab * lax.rsqrt(ss * (1.0 / QK_HEAD) + EPS) * gain
        x1, x2 = n[0:r], n[half:half + r]
        return jnp.concatenate([x1 * cos_t - x2 * sin_t, n[r:half], x2 * cos_t + x1 * sin_t, n[half + r:]], axis=0)

    for head in range(MLA_HEADS):
        rows = slice(LANES * head, LANES * (head + 1))
        qt_ref[head] = norm_rope(qt[rows], gq).astype(BF16)
        k_ref[head] = norm_rope(kt[rows] + krt, gk).T.astype(BF16)
        vt_ref[head, 0:V_HEAD, :] = vt[V_HEAD * head:V_HEAD * (head + 1)].astype(BF16)
        vt_ref[head, V_HEAD:, :] = ones_rows


def _mla_projection(x2, mod3, norm_g, w_in, q_a_g, w_q_b, kv_a_g, w_kv_b, q_norm_g, k_norm_g, rope, n_b, seq):
    t, d = x2.shape
    cos_t, sin_t = rope
    tm = min(512, seq)
    per_b = seq // tm
    o2 = 2 * SSM_WIDTH
    o4, o5 = o2 + Q_LORA + KV_LORA, o2 + Q_LORA + KV_LORA + QK_ROPE
    src = _slab_source()
    valid = src >= 0
    rope_src = np.where(src >= QK_NOPE, src - QK_NOPE, -1)
    nope_src = np.where(valid & (src < QK_NOPE), src, -1)
    w1 = jnp.concatenate([w_in[:, o2:o4], w_in[:, o5:]], axis=1).astype(BF16)
    wkrt = jnp.where((rope_src >= 0)[:, None], w_in[:, o4:o5].T[np.clip(rope_src, 0, QK_ROPE - 1)], 0.0).astype(BF16)
    wqt = _to_slabs(w_q_b, QK_HEAD, src).T.astype(BF16)
    wkt = _to_slabs(w_kv_b, QK_NOPE + V_HEAD, nope_src).T.astype(BF16)
    wvt = w_kv_b.reshape(KV_LORA, MLA_HEADS, QK_NOPE + V_HEAD)[:, :, QK_NOPE:].reshape(KV_LORA, MLA_WIDTH).T.astype(BF16)
    q_scale = math.log2(math.e) / math.sqrt(QK_HEAD)
    slab_gain = lambda g: jnp.broadcast_to(jnp.where(valid, g[np.clip(src, 0, QK_HEAD - 1)], 0.0)[:, None],
                                           (LANES, LANES))
    gq = slab_gain(q_norm_g * q_scale)
    gk = slab_gain(k_norm_g)
    const = lambda shape: pl.BlockSpec(shape, lambda i: (0,) * len(shape))
    tok = lambda i: (i // per_b, 0, 0, i % per_b)
    return pl.pallas_call(
        _mla_proj_kernel,
        out_shape=(jax.ShapeDtypeStruct((n_b, MLA_HEADS, LANES, seq), BF16),
                   jax.ShapeDtypeStruct((n_b, MLA_HEADS, seq, LANES), BF16),
                   jax.ShapeDtypeStruct((n_b, MLA_HEADS, V_ROWS, seq), BF16),
                   jax.ShapeDtypeStruct((t, MLA_WIDTH), BF16)),
        grid=(t // tm,),
        in_specs=[pl.BlockSpec((tm, d), lambda i: (i, 0)),
                  pl.BlockSpec((None, 1, 3 * d), lambda i: (i // per_b, 0, 0)),
                  const((1, d)), const(w1.shape), const((1, Q_LORA)), const((1, KV_LORA)),
                  const(wqt.shape), const(wkt.shape), const(wkrt.shape), const(wvt.shape),
                  const((LANES, LANES)), const((LANES, LANES)),
                  pl.BlockSpec((ROPE_HALF, tm), lambda i: (0, i)),
                  pl.BlockSpec((ROPE_HALF, tm), lambda i: (0, i))],
        out_specs=(pl.BlockSpec((None, MLA_HEADS, LANES, tm), tok),
                   pl.BlockSpec((None, MLA_HEADS, tm, LANES), lambda i: (i // per_b, 0, i % per_b, 0)),
                   pl.BlockSpec((None, MLA_HEADS, V_ROWS, tm), tok),
                   pl.BlockSpec((tm, MLA_WIDTH), lambda i: (i, 0))),
        compiler_params=pltpu.CompilerParams(dimension_semantics=("arbitrary",),
                                             vmem_limit_bytes=48 * 1024 * 1024),
        name="mla_projection",
    )(x2, mod3, norm_g.reshape(1, d), w1, q_a_g.reshape(1, -1), kv_a_g.reshape(1, -1), wqt, wkt, wkrt, wvt, gq, gk,
      cos_t, sin_t)


ATTN_TQ = 1024
ATTN_TK = 256
HEADS_PER_STEP = 2


def _attn_kernel(qt_ref, k_ref, vt_ref, zs_ref, o_ref, s_scr):
    tq, tk = ATTN_TQ, ATTN_TK
    r = tq // tk
    assert r * tk == tq and r % 2 == 0
    qi = pl.program_id(2)

    def scores(slot, t, lo=0):
        start = pl.multiple_of(t * tk, tk)
        for e in range(HEADS_PER_STEP):
            s_scr[slot, e, :, lo:] = jnp.dot(k_ref[e, pl.ds(start, tk), :], qt_ref[e, :, lo:],
                                             preferred_element_type=F32)

    def process(slot, t, carry, lo=0, diagonal=False):
        start = pl.multiple_of(t * tk, tk)
        out = []
        for e in range(HEADS_PER_STEP):
            m, acc = carry[e]
            s = s_scr[slot, e, :, lo:]
            if diagonal:
                blk = s[:, :tk]
                ok = lax.broadcasted_iota(jnp.int32, blk.shape, 0) <= lax.broadcasted_iota(jnp.int32, blk.shape, 1)
                blk = jnp.where(ok, blk, NEG_BIG)
                s = jnp.concatenate([blk, s[:, tk:]], axis=1) if s.shape[1] > tk else blk
            m_old = m[:, lo:]
            m_new = jnp.maximum(m_old, jnp.max(s, axis=0, keepdims=True))
            p = jnp.exp2(s - m_new).astype(BF16)
            acc_new = (jnp.exp2(m_old - m_new) * acc[:, lo:]
                       + jnp.dot(vt_ref[e, :, pl.ds(start, tk)], p, preferred_element_type=F32))
            if lo:
                m_new = jnp.concatenate([m[:, :lo], m_new], axis=1)
                acc_new = jnp.concatenate([acc[:, :lo], acc_new], axis=1)
            out.append((m_new, acc_new))
        return tuple(out)

    def body(i, carry):
        base = i * r
        for u in range(r):
            scores((u + 1) % 2, base + u + 1)
            carry = process(u % 2, base + u, carry)
        return carry

    init = tuple((jnp.full((1, tq), NEG_BIG, F32), jnp.zeros((V_ROWS, tq), F32)) for _ in range(HEADS_PER_STEP))
    scores(0, 0)
    carry = lax.fori_loop(0, qi, body, init)
    base = qi * r
    for u in range(r):
        if u + 1 < r:
            scores((u + 1) % 2, base + u + 1, lo=(u + 1) * tk)
        carry = process(u % 2, base + u, carry, lo=u * tk, diagonal=True)
    _attn_finish(tuple(c[1] for c in carry), zs_ref, o_ref)


def _attn_finish(accs, zs_ref, o_ref):
    outs = [(acc[:V_HEAD] / acc[V_HEAD:V_HEAD + 1]).T for acc in accs]
    o_ref[...] = (jnp.concatenate(outs, axis=1) * zs_ref[...].astype(F32)).astype(o_ref.dtype)


def _attn_bounded_kernel(qt_ref, k_ref, vt_ref, zs_ref, o_ref, s_scr):
    tq, tk = ATTN_TQ, ATTN_TK
    r = tq // tk
    assert r * tk == tq and r % 2 == 0
    qi = pl.program_id(2)

    def scores(slot, t, lo=0):
        start = pl.multiple_of(t * tk, tk)
        for e in range(HEADS_PER_STEP):
            s_scr[slot, e, :, lo:] = jnp.dot(k_ref[e, pl.ds(start, tk), :], qt_ref[e, :, lo:],
                                             preferred_element_type=F32)

    def process(slot, t, accs, lo=0, diagonal=False):
        start = pl.multiple_of(t * tk, tk)
        out = []
        for e in range(HEADS_PER_STEP):
            p = jnp.exp2(s_scr[slot, e, :, lo:])
            if diagonal:
                blk = p[:, :tk]
                ok = lax.broadcasted_iota(jnp.int32, blk.shape, 0) <= lax.broadcasted_iota(jnp.int32, blk.shape, 1)
                blk = jnp.where(ok, blk, 0.0)
                p = jnp.concatenate([blk, p[:, tk:]], axis=1) if p.shape[1] > tk else blk
            pv = jnp.dot(vt_ref[e, :, pl.ds(start, tk)], p.astype(BF16), preferred_element_type=F32)
            acc = accs[e]
            out.append(acc + pv if lo == 0 else jnp.concatenate([acc[:, :lo], acc[:, lo:] + pv], axis=1))
        return tuple(out)

    def body(i, accs):
        base = i * r
        for u in range(r):
            scores((u + 1) % 2, base + u + 1)
            accs = process(u % 2, base + u, accs)
        return accs

    scores(0, 0)
    accs = lax.fori_loop(0, qi, body, tuple(jnp.zeros((V_ROWS, tq), F32) for _ in range(HEADS_PER_STEP)))
    base = qi * r
    for u in range(r):
        if u + 1 < r:
            scores((u + 1) % 2, base + u + 1, lo=(u + 1) * tk)
        accs = process(u % 2, base + u, accs, lo=u * tk, diagonal=True)
    _attn_finish(accs, zs_ref, o_ref)


SCORE_BOUND_LOG2 = 60.0


def _attention(qt, k, vt, zs, score_bound):
    n_b, n_h, seq, _ = k.shape
    tq = ATTN_TQ
    nq = seq // tq
    hp = HEADS_PER_STEP
    width = hp * V_HEAD

    def call(body, scratch):
        return pl.pallas_call(
            body,
            out_shape=jax.ShapeDtypeStruct(zs.shape, BF16),
            grid=(n_b, n_h // hp, nq),
            in_specs=[pl.BlockSpec((None, hp, LANES, tq), lambda b, h, i: (b, h, 0, i)),
                      pl.BlockSpec((None, hp, seq, LANES), lambda b, h, i: (b, h, 0, 0)),
                      pl.BlockSpec((None, hp, V_ROWS, seq), lambda b, h, i: (b, h, 0, 0)),
                      pl.BlockSpec((tq, width), lambda b, h, i: (b * nq + i, h))],
            out_specs=pl.BlockSpec((tq, width), lambda b, h, i: (b * nq + i, h)),
            scratch_shapes=scratch,
            compiler_params=pltpu.CompilerParams(dimension_semantics=("arbitrary", "arbitrary", "arbitrary"),
                                                 vmem_limit_bytes=48 * 1024 * 1024),
            name="causal_attention",
        )(qt, k, vt, zs)

    scratch = [pltpu.VMEM((2, hp, ATTN_TK, tq), F32)]
    return lax.cond(score_bound <= SCORE_BOUND_LOG2,
                    lambda: call(_attn_bounded_kernel, scratch),
                    lambda: call(_attn_kernel, scratch))


def _s5_scan_steps(n_top):
    return [CHUNK * m for m in range(SUPER)] + [CHUNK * SUPER * (1 << i) for i in range(n_top)]


def _s5_exponents(n_top):
    return sorted(set(range(CHUNK + 1)) | set(_s5_scan_steps(n_top)))


def _s5_coef_kernel(kk_ref, lre_ref, lim_ref, ldt_ref, bre_ref, bim_ref, cre_ref, cim_ref, d_ref,
                    lhs_ref, wout_ref, ap_ref, *, n_top):
    h, p, n = SSM_GROUP, SSM_STATE, CHUNK
    lre, lim = lre_ref[...], lim_ref[...]
    dt = jnp.exp(ldt_ref[...])
    exps = _s5_exponents(n_top)
    kcol = kk_ref[...]
    mag = jnp.exp(kcol * (lre * dt))
    ang = kcol * (lim * dt)
    pow_re, pow_im = mag * jnp.cos(ang), mag * jnp.sin(ang)

    def power(kk):
        i = exps.index(kk)
        return pow_re[i:i + 1], pow_im[i:i + 1]

    lb_re, lb_im = power(1)
    nr, ni = lb_re - 1.0, lb_im
    den = lre * lre + lim * lim
    f_re = (nr * lre + ni * lim) / den
    f_im = (ni * lre - nr * lim) / den
    bre, bim = bre_ref[...].T, bim_ref[...].T
    bb_re = f_re * bre - f_im * bim
    bb_im = f_re * bim + f_im * bre
    cre, cim = cre_ref[...], cim_ref[...]
    cp_re, cp_im = [], []
    for kk in range(n + 1):
        pr, pi = power(kk)
        cp_re.append(cre * pr - cim * pi)
        cp_im.append(cre * pi + cim * pr)

    cpw = [jnp.concatenate([cp_re[kk], -cp_im[kk]], axis=1) for kk in range(n + 1)]
    bb_a = jnp.concatenate([bb_re, bb_im], axis=1)
    kw = lax.dot_general(jnp.concatenate(cpw[:n], axis=0), jnp.concatenate([bb_a] * n, axis=0), _NT,
                         preferred_element_type=F32, precision=HIGHEST)
    lane = lax.broadcasted_iota(jnp.int32, (h, n * h), 1)
    row = lax.broadcasted_iota(jnp.int32, (h, n * h), 0)
    col_blk = lax.shift_right_logical(lane, 4)
    d_tiled = jnp.concatenate([d_ref[...]] * n, axis=1)
    lag = [kw[h * kk:h * (kk + 1)] for kk in range(n)]
    lag[0] = lag[0] + jnp.where((lane & (h - 1)) == row, d_tiled, 0.0)
    rows = []
    for j in range(n):
        acc = jnp.zeros((h, n * h), F32)
        for jp in range(j + 1):
            acc = jnp.where(col_blk == jp, lag[j - jp], acc)
        rows.append(acc)
    lhs_ref[0:n * h, :] = jnp.concatenate(rows, axis=0).astype(lhs_ref.dtype)

    bb_b = jnp.concatenate([-bb_im, bb_re], axis=1)
    win = []
    for j in range(n):
        pr, pi = power(n - 1 - j)
        win.append(jnp.concatenate([pr, pr], axis=1) * bb_a + jnp.concatenate([pi, pi], axis=1) * bb_b)
    lhs_ref[n * h:, :] = jnp.concatenate(win, axis=0).T.astype(lhs_ref.dtype)
    wout_ref[...] = jnp.concatenate(cpw[1:], axis=0).astype(wout_ref.dtype)
    for i, kk in enumerate(_s5_scan_steps(n_top)):
        pr, pi = power(kk)
        ap_ref[i:i + 1, :] = jnp.concatenate([pr, pi], axis=1)


def _s5_coefficients(log_dt, lam_re, lam_im, b_re, b_im, c_re, c_im, d_skip, n_top):
    g, p, h = SSM_GROUPS, SSM_STATE, SSM_GROUP
    n_ap = SUPER + n_top
    grp = lambda *shape: pl.BlockSpec((None,) + shape, lambda i: (i,) + (0,) * len(shape))
    exps = np.asarray(_s5_exponents(n_top), np.float32)
    kk = np.zeros((-(-exps.size // 8) * 8, 1), np.float32)
    kk[:exps.size, 0] = exps
    return pl.pallas_call(
        functools.partial(_s5_coef_kernel, n_top=n_top),
        out_shape=(jax.ShapeDtypeStruct((g, CHUNK * h + 2 * p, CHUNK * h), BF16),
                   jax.ShapeDtypeStruct((g, CHUNK * h, 2 * p), BF16),
                   jax.ShapeDtypeStruct((g, n_ap, 2 * p), F32)),
        grid=(g,),
        in_specs=[pl.BlockSpec(kk.shape, lambda i: (0, 0)),
                  grp(1, p), grp(1, p), grp(1, 1), grp(p, h), grp(p, h), grp(h, p), grp(h, p), grp(1, h)],
        out_specs=(grp(CHUNK * h + 2 * p, CHUNK * h), grp(CHUNK * h, 2 * p), grp(n_ap, 2 * p)),
        name="s5_coefficients",
    )(jnp.asarray(kk), lam_re.reshape(g, 1, p), lam_im.reshape(g, 1, p), log_dt.reshape(g, 1, 1), b_re, b_im,
      c_re, c_im, d_skip.reshape(g, 1, h))


PHASES = 8


def _ssm_proj_kernel(x_ref, mod_ref, ng_ref, w_ref, u_ref, z_ref, rows_scr, *, n_b):
    d = D_MODEL
    for jj in range(PHASES):
        hs = []
        for b in range(n_b):
            mod = mod_ref[b]
            shift, scale = mod[:, :d], mod[:, d:2 * d]
            rows_scr[b] = x_ref[b, :, jj, :]
            hs.append((_rms(rows_scr[b], ng_ref[...]) * (1.0 + scale) + shift).astype(BF16))
        h = jnp.concatenate(hs, axis=0)
        r = lax.dot_general(w_ref[...], h, _NT, preferred_element_type=F32)
        u_ref[jj] = r[:SSM_WIDTH].astype(BF16)
        z_ref[jj] = _silu(r[SSM_WIDTH:]).astype(BF16)


def _ssm_projection(x4, mod3, norm_g, w_in):
    n_b, c2, _, d = x4.shape
    lanes = n_b * c2
    halves = CHUNK // PHASES
    w_t = w_in[:, :2 * SSM_WIDTH].T.astype(BF16)
    out = jax.ShapeDtypeStruct((CHUNK, SSM_WIDTH, SUPER * lanes), BF16)
    out_spec = pl.BlockSpec((PHASES, SSM_WIDTH, lanes), lambda j2, hf: (hf, 0, j2))
    return pl.pallas_call(
        functools.partial(_ssm_proj_kernel, n_b=n_b),
        out_shape=(out, out),
        grid=(SUPER, halves),
        in_specs=[pl.BlockSpec((n_b, c2, PHASES, d), lambda j2, hf: (0, 0, j2 * halves + hf, 0)),
                  pl.BlockSpec((n_b, 1, 3 * d), lambda j2, hf: (0, 0, 0)),
                  pl.BlockSpec((1, d), lambda j2, hf: (0, 0)),
                  pl.BlockSpec(w_t.shape, lambda j2, hf: (0, 0))],
        out_specs=(out_spec, out_spec),
        scratch_shapes=[pltpu.VMEM((n_b, c2, d), F32)],
        compiler_params=pltpu.CompilerParams(dimension_semantics=("arbitrary", "arbitrary"),
                                             vmem_limit_bytes=56 * 1024 * 1024),
        name="ssm_projection",
    )(x4, mod3, norm_g.reshape(1, d), w_t)


def _cmul(ar, ai, xr, xi):
    return ar * xr - ai * xi, ar * xi + ai * xr


def _s5_group_kernel(x_ref, lhs_ref, wout_ref, ap_ref, y_ref, *, n_b, c2, n_top):
    p = SSM_STATE
    rows = CHUNK * SSM_GROUP
    n = SUPER * n_b * c2
    x = x_ref[...].reshape(rows, n)
    r = jnp.dot(lhs_ref[...], x, preferred_element_type=F32)
    y_intra, c_re, c_im = r[:rows], r[rows:rows + p], r[rows + p:]

    def mult(i):
        rep = jnp.broadcast_to(ap_ref[i:i + 1, :], (2 * p, 2 * p)).T
        return rep[:p, :c2], rep[p:, :c2]

    lane = lax.broadcasted_iota(jnp.int32, (p, c2), 1)

    def shifted(a, sh):
        return jnp.where(lane >= sh, pltpu.roll(a, sh, 1), 0.0)

    pieces = [[None] * n_b for _ in range(SUPER)]
    for b in range(n_b):
        piece = lambda a, j2: a[:, (j2 * n_b + b) * c2:(j2 * n_b + b + 1) * c2]
        a_re, a_im = mult(1)
        e_re = jnp.zeros((p, c2), F32)
        e_im = jnp.zeros((p, c2), F32)
        local = []
        for j2 in range(SUPER):
            local.append((e_re, e_im))
            t_re, t_im = _cmul(a_re, a_im, e_re, e_im)
            e_re, e_im = t_re + piece(c_re, j2), t_im + piece(c_im, j2)
        for i in range(n_top):
            m_re, m_im = mult(SUPER + i)
            t_re, t_im = _cmul(m_re, m_im, shifted(e_re, 1 << i), shifted(e_im, 1 << i))
            e_re, e_im = e_re + t_re, e_im + t_im
        s_re, s_im = shifted(e_re, 1), shifted(e_im, 1)
        for j2 in range(SUPER):
            if j2 == 0:
                f_re, f_im = s_re, s_im
            else:
                m_re, m_im = mult(j2)
                t_re, t_im = _cmul(m_re, m_im, s_re, s_im)
                f_re, f_im = local[j2][0] + t_re, local[j2][1] + t_im
            pieces[j2][b] = jnp.concatenate([f_re, f_im], axis=0)
    s_in = jnp.concatenate([pieces[j2][b] for j2 in range(SUPER) for b in range(n_b)], axis=1)
    y = y_intra + jnp.dot(wout_ref[...], s_in.astype(BF16), preferred_element_type=F32)
    y_ref[...] = y.reshape(CHUNK, SSM_GROUP, n).astype(y_ref.dtype)


def _s5_scan(u_t, lhs, wout, ap, n_b, c2, n_top):
    g, h = SSM_GROUPS, SSM_GROUP
    n = u_t.shape[-1]
    grp = lambda *shape: pl.BlockSpec((None,) + shape, lambda i: (i,) + (0,) * len(shape))
    return pl.pallas_call(
        functools.partial(_s5_group_kernel, n_b=n_b, c2=c2, n_top=n_top),
        out_shape=jax.ShapeDtypeStruct((g, CHUNK, h, n), BF16),
        grid=(g,),
        in_specs=[pl.BlockSpec((CHUNK, h, n), lambda i: (0, i, 0)),
                  grp(*lhs.shape[1:]), grp(*wout.shape[1:]), grp(*ap.shape[1:])],
        out_specs=grp(CHUNK, h, n),
        compiler_params=pltpu.CompilerParams(dimension_semantics=("arbitrary",),
                                             vmem_limit_bytes=48 * 1024 * 1024),
        name="s5_chunk_scan",
    )(u_t, lhs, wout, ap)


def _glu_kernel(y_ref, zs_ref, w_ref, b_ref, o_ref, *, n_b, c2):
    g, _, h, n = y_ref.shape
    for jj in range(PHASES):
        y = jax.nn.gelu(y_ref[:, jj].reshape(g * h, n).astype(F32))
        t = jnp.dot(w_ref[...], y.astype(BF16), preferred_element_type=F32) + b_ref[...]
        gated = (y * jax.nn.sigmoid(t) * zs_ref[jj].astype(F32)).T
        for b in range(n_b):
            o_ref[b, :, jj, :] = gated[b * c2:(b + 1) * c2]


def _glu(y_t, zs_t, w_glu, b_glu, n_b, c2):
    g, _, h, _ = y_t.shape
    lanes = n_b * c2
    halves = CHUNK // PHASES
    w_t = w_glu.T.astype(BF16)
    return pl.pallas_call(
        functools.partial(_glu_kernel, n_b=n_b, c2=c2),
        out_shape=jax.ShapeDtypeStruct((n_b, c2, SUPER * CHUNK, SSM_WIDTH), F32),
        grid=(SUPER, halves),
        in_specs=[pl.BlockSpec((g, PHASES, h, lanes), lambda j2, hf: (0, hf, 0, j2)),
                  pl.BlockSpec((PHASES, SSM_WIDTH, lanes), lambda j2, hf: (hf, 0, j2)),
                  pl.BlockSpec(w_t.shape, lambda j2, hf: (0, 0)),
                  pl.BlockSpec((SSM_WIDTH, 1), lambda j2, hf: (0, 0))],
        out_specs=pl.BlockSpec((n_b, c2, PHASES, SSM_WIDTH), lambda j2, hf: (0, 0, j2 * halves + hf, 0)),
        compiler_params=pltpu.CompilerParams(dimension_semantics=("arbitrary", "arbitrary"),
                                             vmem_limit_bytes=48 * 1024 * 1024),
        name="s5_glu",
    )(y_t, zs_t, w_t, b_glu.reshape(SSM_WIDTH, 1))


def _out_proj_kernel(ys_ref, ym_ref, x_ref, mod_ref, wa_ref, wb_ref, o_ref):
    y = (jnp.dot(ys_ref[...].astype(BF16), wa_ref[...], preferred_element_type=F32)
         + jnp.dot(ym_ref[...], wb_ref[...], preferred_element_type=F32))
    gate = mod_ref[...][:, 2 * D_MODEL:]
    o_ref[...] = x_ref[...] + gate * y


def _out_projection(ys, ym, x2, mod3, w_out, seq):
    t, d = x2.shape
    tm = min(512, seq)
    per_b = seq // tm
    wa = w_out[:SSM_WIDTH].astype(BF16)
    wb = w_out[SSM_WIDTH:].astype(BF16)
    tok = lambda w: pl.BlockSpec((tm, w), lambda i: (i, 0))
    return pl.pallas_call(
        _out_proj_kernel,
        out_shape=jax.ShapeDtypeStruct(x2.shape, F32),
        grid=(t // tm,),
        in_specs=[tok(SSM_WIDTH), tok(MLA_WIDTH), tok(d),
                  pl.BlockSpec((None, 1, 3 * d), lambda i: (i // per_b, 0, 0)),
                  pl.BlockSpec(wa.shape, lambda i: (0, 0)),
                  pl.BlockSpec(wb.shape, lambda i: (0, 0))],
        out_specs=tok(d),
        compiler_params=pltpu.CompilerParams(dimension_semantics=("arbitrary",)),
        name="output_projection",
    )(ys, ym, x2, mod3, wa, wb)


def kernel(x, c, positions, w_ada, b_ada, norm_g, w_in, log_dt, lam_re, lam_im, b_re, b_im, c_re, c_im, d_skip,
           w_glu, b_glu, q_a_g, w_q_b, kv_a_g, w_kv_b, q_norm_g, k_norm_g, w_out):
    n_b, seq, d = x.shape
    depth = w_ada.shape[0]
    c2 = seq // (CHUNK * SUPER)
    n_top = max(int(math.log2(c2)), 0)
    assert c2 * CHUNK * SUPER == seq and (1 << n_top) == c2
    rope = _rope_tables(positions)
    for l in range(depth):
        mod3 = _modulation(c, w_ada[l], b_ada[l]).reshape(n_b, 1, 3 * d)
        x2 = x.reshape(n_b * seq, d)
        lhs, wout, ap = _s5_coefficients(log_dt[l], lam_re[l], lam_im[l], b_re[l], b_im[l], c_re[l], c_im[l],
                                         d_skip[l], n_top)
        u_t, zs_t = _ssm_projection(x.reshape(n_b, c2, SUPER * CHUNK, d), mod3, norm_g[l], w_in[l])
        y_t = _s5_scan(u_t, lhs, wout, ap, n_b, c2, n_top)
        ys = _glu(y_t, zs_t, w_glu[l], b_glu[l], n_b, c2).reshape(n_b * seq, SSM_WIDTH)
        qt, k, vt, zm = _mla_projection(x2, mod3, norm_g[l], w_in[l], q_a_g[l], w_q_b[l],
                                        kv_a_g[l], w_kv_b[l], q_norm_g[l], k_norm_g[l], rope, n_b, seq)
        score_bound = (1.05 * math.sqrt(QK_HEAD) * math.log2(math.e)
                       * jnp.max(jnp.abs(q_norm_g[l])) * jnp.max(jnp.abs(k_norm_g[l])))
        ym = _attention(qt, k, vt, zm, score_bound)
        x = _out_projection(ys, ym, x2, mod3, w_out[l], seq).reshape(n_b, seq, d)
    return x
```

```python
import functools
import math

import numpy as np
import jax
import jax.numpy as jnp
from jax import lax
from jax.experimental import pallas as pl
from jax.experimental.pallas import tpu as pltpu

F32 = jnp.float32
BF16 = jnp.bfloat16
HIGHEST = lax.Precision.HIGHEST

D_MODEL = 1024
SSM_WIDTH = 512
SSM_GROUP = 16
SSM_GROUPS = 32
SSM_STATE = 64
MLA_HEADS = 8
QK_NOPE = 64
QK_ROPE = 32
QK_HEAD = QK_NOPE + QK_ROPE
V_HEAD = 64
MLA_WIDTH = MLA_HEADS * V_HEAD
Q_LORA = 384
KV_LORA = 256
ROPE_THETA = 10000.0
EPS = 1e-6
NEG_BIG = -1e30

LANES = 128
CHUNK = 16
SUPER = 4
ROPE_HALF = QK_ROPE // 2

_NT = (((1,), (1,)), ((), ()))
_TN = (((0,), (0,)), ((), ()))


def _silu(v):
    return v * jax.nn.sigmoid(v)


def _rms(v, gain):
    return v * lax.rsqrt(jnp.mean(v * v, axis=-1, keepdims=True) + EPS) * gain


def _mod_kernel(c_ref, w_ref, b_ref, o_ref):
    act = _silu(c_ref[...])
    o_ref[...] = jnp.dot(act, w_ref[...], preferred_element_type=F32, precision=HIGHEST) + b_ref[...]


def _modulation(c, w, b):
    n_b, d = c.shape
    rows = 8
    c_pad = jnp.zeros((rows, d), F32).at[:n_b].set(c)
    tn = 512
    out = pl.pallas_call(
        _mod_kernel,
        out_shape=jax.ShapeDtypeStruct((rows, 3 * d), F32),
        grid=(3 * d // tn,),
        in_specs=[pl.BlockSpec((rows, d), lambda n: (0, 0)),
                  pl.BlockSpec((d, tn), lambda n: (0, n)),
                  pl.BlockSpec((1, tn), lambda n: (0, n))],
        out_specs=pl.BlockSpec((rows, tn), lambda n: (0, n)),
        name="adaln_modulation",
    )(c_pad, w, b.reshape(1, -1))
    return out[:n_b]


def _rope_kernel(pos_ref, freq_ref, cos_ref, sin_ref):
    ang = freq_ref[...] * pos_ref[...].astype(F32)
    cos_ref[...] = jnp.cos(ang)
    sin_ref[...] = jnp.sin(ang)


def _rope_tables(positions):
    t = positions.size
    tt = min(t, 2048)
    inv_freq = (ROPE_THETA ** (-np.arange(ROPE_HALF, dtype=np.float64) * 2.0 / QK_ROPE)).astype(np.float32)
    return pl.pallas_call(
        _rope_kernel,
        out_shape=(jax.ShapeDtypeStruct((ROPE_HALF, t), F32),) * 2,
        grid=(t // tt,),
        in_specs=[pl.BlockSpec((1, tt), lambda i: (0, i)),
                  pl.BlockSpec((ROPE_HALF, 1), lambda i: (0, 0))],
        out_specs=(pl.BlockSpec((ROPE_HALF, tt), lambda i: (0, i)),) * 2,
        name="rope_tables",
    )(positions.reshape(1, t), jnp.asarray(inv_freq).reshape(ROPE_HALF, 1))


def _slab_source():
    src = np.full((LANES,), -1, np.int64)
    src[0:16] = QK_NOPE + np.arange(16)
    src[16:64] = np.arange(48)
    src[64:80] = QK_NOPE + 16 + np.arange(16)
    src[80:96] = 48 + np.arange(16)
    return src


def _to_slabs(w, per_head, src):
    k = w.shape[0]
    w3 = w.reshape(k, MLA_HEADS, per_head)
    valid = src >= 0
    cols = jnp.where(valid[None, None, :], w3[:, :, np.clip(src, 0, per_head - 1)], 0.0)
    return cols.reshape(k, MLA_HEADS * LANES)


V_ROWS = V_HEAD + 16


def _mla_proj_kernel(x_ref, mod_ref, ng_ref, w1_ref, qag_ref, kvag_ref, wqt_ref, wkt_ref, wkrt_ref, wvt_ref,
                     gq_ref, gk_ref, cost_ref, sint_ref, qt_ref, k_ref, vt_ref, zs_ref):
    d = D_MODEL
    tm = x_ref.shape[0]
    x = x_ref[...]
    mod = mod_ref[...]
    shift, scale = mod[:, :d], mod[:, d:2 * d]
    h = (_rms(x, ng_ref[...]) * (1.0 + scale) + shift).astype(BF16)
    p1 = jnp.dot(h, w1_ref[...], preferred_element_type=F32)
    o1, o2 = Q_LORA, Q_LORA + KV_LORA
    cqn = _rms(p1[:, :o1], qag_ref[...]).astype(BF16)
    ckvn = _rms(p1[:, o1:o2], kvag_ref[...]).astype(BF16)
    zs_ref[...] = _silu(p1[:, o2:]).astype(BF16)

    qt = lax.dot_general(wqt_ref[...], cqn, _NT, preferred_element_type=F32)
    kt = lax.dot_general(wkt_ref[...], ckvn, _NT, preferred_element_type=F32)
    krt = lax.dot_general(wkrt_ref[...], h, _NT, preferred_element_type=F32)
    vt = lax.dot_general(wvt_ref[...], ckvn, _NT, preferred_element_type=F32)
    cos_t, sin_t = cost_ref[...], sint_ref[...]
    gq = jnp.concatenate([gq_ref[...]] * (tm // LANES), axis=1)
    gk = jnp.concatenate([gk_ref[...]] * (tm // LANES), axis=1)
    first = lax.broadcasted_iota(jnp.int32, (V_ROWS - V_HEAD, tm), 0) == 0
    ones_rows = jnp.where(first, 1.0, 0.0).astype(BF16)
    r, half = ROPE_HALF, LANES // 2

    def norm_rope(slab, gain):
        ss = jnp.sum(slab * slab, axis=0, keepdims=True)
        n = slab * lax.rsqrt(ss * (1.0 / QK_HEAD) + EPS) * gain
        x1, x2 = n[0:r], n[half:half + r]
        return jnp.concatenate([x1 * cos_t - x2 * sin_t, n[r:half], x2 * cos_t + x1 * sin_t, n[half + r:]], axis=0)

    for head in range(MLA_HEADS):
        rows = slice(LANES * head, LANES * (head + 1))
        qt_ref[head] = norm_rope(qt[rows], gq).astype(BF16)
        k_ref[head] = norm_rope(kt[rows] + krt, gk).T.astype(BF16)
        vt_ref[head, 0:V_HEAD, :] = vt[V_HEAD * head:V_HEAD * (head + 1)].astype(BF16)
        vt_ref[head, V_HEAD:, :] = ones_rows


def _mla_projection(x2, mod3, norm_g, w_in, q_a_g, w_q_b, kv_a_g, w_kv_b, q_norm_g, k_norm_g, rope, n_b, seq):
    t, d = x2.shape
    cos_t, sin_t = rope
    tm = min(512, seq)
    per_b = seq // tm
    o2 = 2 * SSM_WIDTH
    o4, o5 = o2 + Q_LORA + KV_LORA, o2 + Q_LORA + KV_LORA + QK_ROPE
    src = _slab_source()
    valid = src >= 0
    rope_src = np.where(src >= QK_NOPE, src - QK_NOPE, -1)
    nope_src = np.where(valid & (src < QK_NOPE), src, -1)
    w1 = jnp.concatenate([w_in[:, o2:o4], w_in[:, o5:]], axis=1).astype(BF16)
    wkrt = jnp.where((rope_src >= 0)[:, None], w_in[:, o4:o5].T[np.clip(rope_src, 0, QK_ROPE - 1)], 0.0).astype(BF16)
    wqt = _to_slabs(w_q_b, QK_HEAD, src).T.astype(BF16)
    wkt = _to_slabs(w_kv_b, QK_NOPE + V_HEAD, nope_src).T.astype(BF16)
    wvt = w_kv_b.reshape(KV_LORA, MLA_HEADS, QK_NOPE + V_HEAD)[:, :, QK_NOPE:].reshape(KV_LORA, MLA_WIDTH).T.astype(BF16)
    q_scale = math.log2(math.e) / math.sqrt(QK_HEAD)
    slab_gain = lambda g: jnp.broadcast_to(jnp.where(valid, g[np.clip(src, 0, QK_HEAD - 1)], 0.0)[:, None],
                                           (LANES, LANES))
    gq = slab_gain(q_norm_g * q_scale)
    gk = slab_gain(k_norm_g)
    const = lambda shape: pl.BlockSpec(shape, lambda i: (0,) * len(shape))
    tok = lambda i: (i // per_b, 0, 0, i % per_b)
    return pl.pallas_call(
        _mla_proj_kernel,
        out_shape=(jax.ShapeDtypeStruct((n_b, MLA_HEADS, LANES, seq), BF16),
                   jax.ShapeDtypeStruct((n_b, MLA_HEADS, seq, LANES), BF16),
                   jax.ShapeDtypeStruct((n_b, MLA_HEADS, V_ROWS, seq), BF16),
                   jax.ShapeDtypeStruct((t, MLA_WIDTH), BF16)),
        grid=(t // tm,),
        in_specs=[pl.BlockSpec((tm, d), lambda i: (i, 0)),
                  pl.BlockSpec((None, 1, 3 * d), lambda i: (i // per_b, 0, 0)),
                  const((1, d)), const(w1.shape), const((1, Q_LORA)), const((1, KV_LORA)),
                  const(wqt.shape), const(wkt.shape), const(wkrt.shape), const(wvt.shape),
                  const((LANES, LANES)), const((LANES, LANES)),
                  pl.BlockSpec((ROPE_HALF, tm), lambda i: (0, i)),
                  pl.BlockSpec((ROPE_HALF, tm), lambda i: (0, i))],
        out_specs=(pl.BlockSpec((None, MLA_HEADS, LANES, tm), tok),
                   pl.BlockSpec((None, MLA_HEADS, tm, LANES), lambda i: (i // per_b, 0, i % per_b, 0)),
                   pl.BlockSpec((None, MLA_HEADS, V_ROWS, tm), tok),
                   pl.BlockSpec((tm, MLA_WIDTH), lambda i: (i, 0))),
        compiler_params=pltpu.CompilerParams(dimension_semantics=("arbitrary",),
                                             vmem_limit_bytes=48 * 1024 * 1024),
        name="mla_projection",
    )(x2, mod3, norm_g.reshape(1, d), w1, q_a_g.reshape(1, -1), kv_a_g.reshape(1, -1), wqt, wkt, wkrt, wvt, gq, gk,
      cos_t, sin_t)


ATTN_TQ = 1024
ATTN_TK = 256
HEADS_PER_STEP = 2


def _attn_kernel(qt_ref, k_ref, vt_ref, zs_ref, o_ref, s_scr):
    tq, tk = ATTN_TQ, ATTN_TK
    r = tq // tk
    assert r * tk == tq and r % 2 == 0
    qi = pl.program_id(2)

    def scores(slot, t, lo=0):
        start = pl.multiple_of(t * tk, tk)
        for e in range(HEADS_PER_STEP):
            s_scr[slot, e, :, lo:] = jnp.dot(k_ref[e, pl.ds(start, tk), :], qt_ref[e, :, lo:],
                                             preferred_element_type=F32)

    def process(slot, t, carry, lo=0, diagonal=False):
        start = pl.multiple_of(t * tk, tk)
        out = []
        for e in range(HEADS_PER_STEP):
            m, acc = carry[e]
            s = s_scr[slot, e, :, lo:]
            if diagonal:
                blk = s[:, :tk]
                ok = lax.broadcasted_iota(jnp.int32, blk.shape, 0) <= lax.broadcasted_iota(jnp.int32, blk.shape, 1)
                blk = jnp.where(ok, blk, NEG_BIG)
                s = jnp.concatenate([blk, s[:, tk:]], axis=1) if s.shape[1] > tk else blk
            m_old = m[:, lo:]
            m_new = jnp.maximum(m_old, jnp.max(s, axis=0, keepdims=True))
            p = jnp.exp2(s - m_new).astype(BF16)
            acc_new = (jnp.exp2(m_old - m_new) * acc[:, lo:]
                       + jnp.dot(vt_ref[e, :, pl.ds(start, tk)], p, preferred_element_type=F32))
            if lo:
                m_new = jnp.concatenate([m[:, :lo], m_new], axis=1)
                acc_new = jnp.concatenate([acc[:, :lo], acc_new], axis=1)
            out.append((m_new, acc_new))
        return tuple(out)

    def body(i, carry):
        base = i * r
        for u in range(r):
            scores((u + 1) % 2, base + u + 1)
            carry = process(u % 2, base + u, carry)
        return carry

    init = tuple((jnp.full((1, tq), NEG_BIG, F32), jnp.zeros((V_ROWS, tq), F32)) for _ in range(HEADS_PER_STEP))
    scores(0, 0)
    carry = lax.fori_loop(0, qi, body, init)
    base = qi * r
    for u in range(r):
        if u + 1 < r:
            scores((u + 1) % 2, base + u + 1, lo=(u + 1) * tk)
        carry = process(u % 2, base + u, carry, lo=u * tk, diagonal=True)
    _attn_finish(tuple(c[1] for c in carry), zs_ref, o_ref)


def _attn_finish(accs, zs_ref, o_ref):
    outs = [(acc[:V_HEAD] / acc[V_HEAD:V_HEAD + 1]).T for acc in accs]
    o_ref[...] = (jnp.concatenate(outs, axis=1) * zs_ref[...].astype(F32)).astype(o_ref.dtype)


def _attn_bounded_kernel(qt_ref, k_ref, vt_ref, zs_ref, o_ref, s_scr):
    tq, tk = ATTN_TQ, ATTN_TK
    r = tq // tk
    assert r * tk == tq and r % 2 == 0
    qi = pl.program_id(2)

    def scores(slot, t, lo=0):
        start = pl.multiple_of(t * tk, tk)
        for e in range(HEADS_PER_STEP):
            s_scr[slot, e, :, lo:] = jnp.dot(k_ref[e, pl.ds(start, tk), :], qt_ref[e, :, lo:],
                                             preferred_element_type=F32)

    def process(slot, t, accs, lo=0, diagonal=False):
        start = pl.multiple_of(t * tk, tk)
        out = []
        for e in range(HEADS_PER_STEP):
            p = jnp.exp2(s_scr[slot, e, :, lo:])
            if diagonal:
                blk = p[:, :tk]
                ok = lax.broadcasted_iota(jnp.int32, blk.shape, 0) <= lax.broadcasted_iota(jnp.int32, blk.shape, 1)
                blk = jnp.where(ok, blk, 0.0)
                p = jnp.concatenate([blk, p[:, tk:]], axis=1) if p.shape[1] > tk else blk
            pv = jnp.dot(vt_ref[e, :, pl.ds(start, tk)], p.astype(BF16), preferred_element_type=F32)
            acc = accs[e]
            out.append(acc + pv if lo == 0 else jnp.concatenate([acc[:, :lo], acc[:, lo:] + pv], axis=1))
        return tuple(out)

    def body(i, accs):
        base = i * r
        for u in range(r):
            scores((u + 1) % 2, base + u + 1)
            accs = process(u % 2, base + u, accs)
        return accs

    scores(0, 0)
    accs = lax.fori_loop(0, qi, body, tuple(jnp.zeros((V_ROWS, tq), F32) for _ in range(HEADS_PER_STEP)))
    base = qi * r
    for u in range(r):
        if u + 1 < r:
            scores((u + 1) % 2, base + u + 1, lo=(u + 1) * tk)
        accs = process(u % 2, base + u, accs, lo=u * tk, diagonal=True)
    _attn_finish(accs, zs_ref, o_ref)


SCORE_BOUND_LOG2 = 60.0


def _attention(qt, k, vt, zs, score_bound):
    n_b, n_h, seq, _ = k.shape
    tq = ATTN_TQ
    nq = seq // tq
    hp = HEADS_PER_STEP
    width = hp * V_HEAD

    def call(body, scratch):
        return pl.pallas_call(
            body,
            out_shape=jax.ShapeDtypeStruct(zs.shape, BF16),
            grid=(n_b, n_h // hp, nq),
            in_specs=[pl.BlockSpec((None, hp, LANES, tq), lambda b, h, i: (b, h, 0, i)),
                      pl.BlockSpec((None, hp, seq, LANES), lambda b, h, i: (b, h, 0, 0)),
                      pl.BlockSpec((None, hp, V_ROWS, seq), lambda b, h, i: (b, h, 0, 0)),
                      pl.BlockSpec((tq, width), lambda b, h, i: (b * nq + i, h))],
            out_specs=pl.BlockSpec((tq, width), lambda b, h, i: (b * nq + i, h)),
            scratch_shapes=scratch,
            compiler_params=pltpu.CompilerParams(dimension_semantics=("arbitrary", "arbitrary", "arbitrary"),
                                                 vmem_limit_bytes=48 * 1024 * 1024),
            name="causal_attention",
        )(qt, k, vt, zs)

    scratch = [pltpu.VMEM((2, hp, ATTN_TK, tq), F32)]
    return lax.cond(score_bound <= SCORE_BOUND_LOG2,
                    lambda: call(_attn_bounded_kernel, scratch),
                    lambda: call(_attn_kernel, scratch))


def _s5_scan_steps(n_top):
    return [CHUNK * m for m in range(SUPER)] + [CHUNK * SUPER * (1 << i) for i in range(n_top)]


def _s5_exponents(n_top):
    return sorted(set(range(CHUNK + 1)) | set(_s5_scan_steps(n_top)))


def _s5_coef_kernel(kk_ref, *refs, n_top):
    for gi in range(SCAN_GROUPS):
        _s5_coef_one_group(kk_ref, *(r.at[gi] for r in refs), n_top=n_top)


def _s5_coef_one_group(kk_ref, lre_ref, lim_ref, ldt_ref, bre_ref, bim_ref, cre_ref, cim_ref, d_ref,
                       lhs_ref, wout_ref, ap_ref, *, n_top):
    h, p, n = SSM_GROUP, SSM_STATE, CHUNK
    lre, lim = lre_ref[...], lim_ref[...]
    dt = jnp.exp(ldt_ref[...])
    exps = _s5_exponents(n_top)
    kcol = kk_ref[...]
    mag = jnp.exp(kcol * (lre * dt))
    ang = kcol * (lim * dt)
    pow_re, pow_im = mag * jnp.cos(ang), mag * jnp.sin(ang)

    def power(kk):
        i = exps.index(kk)
        return pow_re[i:i + 1], pow_im[i:i + 1]

    lb_re, lb_im = power(1)
    nr, ni = lb_re - 1.0, lb_im
    den = lre * lre + lim * lim
    f_re = (nr * lre + ni * lim) / den
    f_im = (ni * lre - nr * lim) / den
    bre, bim = bre_ref[...].T, bim_ref[...].T
    bb_re = f_re * bre - f_im * bim
    bb_im = f_re * bim + f_im * bre
    cre, cim = cre_ref[...], cim_ref[...]
    cp_re, cp_im = [], []
    for kk in range(n + 1):
        pr, pi = power(kk)
        cp_re.append(cre * pr - cim * pi)
        cp_im.append(cre * pi + cim * pr)

    cpw = [jnp.concatenate([cp_re[kk], -cp_im[kk]], axis=1) for kk in range(n + 1)]
    bb_a = jnp.concatenate([bb_re, bb_im], axis=1)
    kw = lax.dot_general(jnp.concatenate(cpw[:n], axis=0), jnp.concatenate([bb_a] * n, axis=0), _NT,
                         preferred_element_type=F32, precision=HIGHEST)
    lane = lax.broadcasted_iota(jnp.int32, (h, n * h), 1)
    row = lax.broadcasted_iota(jnp.int32, (h, n * h), 0)
    col_blk = lax.shift_right_logical(lane, 4)
    d_tiled = jnp.concatenate([d_ref[...]] * n, axis=1)
    lag = [kw[h * kk:h * (kk + 1)] for kk in range(n)]
    lag[0] = lag[0] + jnp.where((lane & (h - 1)) == row, d_tiled, 0.0)
    rows = []
    for j in range(n):
        acc = jnp.zeros((h, n * h), F32)
        for jp in range(j + 1):
            acc = jnp.where(col_blk == jp, lag[j - jp], acc)
        rows.append(acc)
    lhs_ref[0:n * h, :] = jnp.concatenate(rows, axis=0).astype(lhs_ref.dtype)

    bb_b = jnp.concatenate([-bb_im, bb_re], axis=1)
    win = []
    for j in range(n):
        pr, pi = power(n - 1 - j)
        win.append(jnp.concatenate([pr, pr], axis=1) * bb_a + jnp.concatenate([pi, pi], axis=1) * bb_b)
    lhs_ref[n * h:, :] = jnp.concatenate(win, axis=0).T.astype(lhs_ref.dtype)
    wout_ref[...] = jnp.concatenate(cpw[1:], axis=0).astype(wout_ref.dtype)
    for i, kk in enumerate(_s5_scan_steps(n_top)):
        pr, pi = power(kk)
        ap_ref[i:i + 1, :] = jnp.concatenate([pr, pi], axis=1)


def _s5_coefficients(log_dt, lam_re, lam_im, b_re, b_im, c_re, c_im, d_skip, n_top):
    g, p, h = SSM_GROUPS, SSM_STATE, SSM_GROUP
    n_ap = SUPER + n_top
    gb = SCAN_GROUPS
    grp = lambda *shape: pl.BlockSpec((gb,) + shape, lambda i: (i,) + (0,) * len(shape))
    exps = np.asarray(_s5_exponents(n_top), np.float32)
    kk = np.zeros((-(-exps.size // 8) * 8, 1), np.float32)
    kk[:exps.size, 0] = exps
    return pl.pallas_call(
        functools.partial(_s5_coef_kernel, n_top=n_top),
        out_shape=(jax.ShapeDtypeStruct((g, CHUNK * h + 2 * p, CHUNK * h), BF16),
                   jax.ShapeDtypeStruct((g, CHUNK * h, 2 * p), BF16),
                   jax.ShapeDtypeStruct((g, n_ap, 2 * p), F32)),
        grid=(g // gb,),
        in_specs=[pl.BlockSpec(kk.shape, lambda i: (0, 0)),
                  grp(1, p), grp(1, p), grp(1, 1), grp(p, h), grp(p, h), grp(h, p), grp(h, p), grp(1, h)],
        out_specs=(grp(CHUNK * h + 2 * p, CHUNK * h), grp(CHUNK * h, 2 * p), grp(n_ap, 2 * p)),
        name="s5_coefficients",
    )(jnp.asarray(kk), lam_re.reshape(g, 1, p), lam_im.reshape(g, 1, p), log_dt.reshape(g, 1, 1), b_re, b_im,
      c_re, c_im, d_skip.reshape(g, 1, h))


PHASES = 8


def _ssm_proj_kernel(x_ref, mod_ref, ng_ref, w_ref, u_ref, z_ref, rows_scr, *, n_b):
    d = D_MODEL
    for jj in range(PHASES):
        hs = []
        for b in range(n_b):
            mod = mod_ref[b]
            shift, scale = mod[:, :d], mod[:, d:2 * d]
            rows_scr[b] = x_ref[b, :, jj, :]
            hs.append((_rms(rows_scr[b], ng_ref[...]) * (1.0 + scale) + shift).astype(BF16))
        h = jnp.concatenate(hs, axis=0)
        r = lax.dot_general(w_ref[...], h, _NT, preferred_element_type=F32)
        u_ref[jj] = r[:SSM_WIDTH].astype(BF16)
        z_ref[jj] = _silu(r[SSM_WIDTH:]).astype(BF16)


def _ssm_projection(x4, mod3, norm_g, w_in):
    n_b, c2, _, d = x4.shape
    lanes = n_b * c2
    halves = CHUNK // PHASES
    w_t = w_in[:, :2 * SSM_WIDTH].T.astype(BF16)
    out = jax.ShapeDtypeStruct((CHUNK, SSM_WIDTH, SUPER * lanes), BF16)
    out_spec = pl.BlockSpec((PHASES, SSM_WIDTH, lanes), lambda j2, hf: (hf, 0, j2))
    return pl.pallas_call(
        functools.partial(_ssm_proj_kernel, n_b=n_b),
        out_shape=(out, out),
        grid=(SUPER, halves),
        in_specs=[pl.BlockSpec((n_b, c2, PHASES, d), lambda j2, hf: (0, 0, j2 * halves + hf, 0)),
                  pl.BlockSpec((n_b, 1, 3 * d), lambda j2, hf: (0, 0, 0)),
                  pl.BlockSpec((1, d), lambda j2, hf: (0, 0)),
                  pl.BlockSpec(w_t.shape, lambda j2, hf: (0, 0))],
        out_specs=(out_spec, out_spec),
        scratch_shapes=[pltpu.VMEM((n_b, c2, d), F32)],
        compiler_params=pltpu.CompilerParams(dimension_semantics=("arbitrary", "arbitrary"),
                                             vmem_limit_bytes=56 * 1024 * 1024),
        name="ssm_projection",
    )(x4, mod3, norm_g.reshape(1, d), w_t)


def _cmul(ar, ai, xr, xi):
    return ar * xr - ai * xi, ar * xi + ai * xr


SCAN_GROUPS = 4


def _s5_group_kernel(x_ref, lhs_ref, wout_ref, ap_ref, y_ref, *, n_b, c2, n_top):
    p, gb = SSM_STATE, SCAN_GROUPS
    rows = CHUNK * SSM_GROUP
    n = SUPER * n_b * c2
    r = [jnp.dot(lhs_ref[gi], x_ref[:, SSM_GROUP * gi:SSM_GROUP * (gi + 1), :].reshape(rows, n),
                 preferred_element_type=F32) for gi in range(gb)]
    c_re = jnp.concatenate([r[gi][rows:rows + p] for gi in range(gb)], axis=0)
    c_im = jnp.concatenate([r[gi][rows + p:] for gi in range(gb)], axis=0)

    def mult(i):
        rep = [jnp.broadcast_to(ap_ref[gi, i:i + 1, :], (2 * p, 2 * p)).T for gi in range(gb)]
        return (jnp.concatenate([m[:p, :c2] for m in rep], axis=0),
                jnp.concatenate([m[p:, :c2] for m in rep], axis=0))

    lane = lax.broadcasted_iota(jnp.int32, (gb * p, c2), 1)

    def shifted(a, sh):
        return jnp.where(lane >= sh, pltpu.roll(a, sh, 1), 0.0)

    pieces = [[None] * n_b for _ in range(SUPER)]
    for b in range(n_b):
        piece = lambda a, j2: a[:, (j2 * n_b + b) * c2:(j2 * n_b + b + 1) * c2]
        a_re, a_im = mult(1)
        e_re = jnp.zeros((gb * p, c2), F32)
        e_im = jnp.zeros((gb * p, c2), F32)
        local = []
        for j2 in range(SUPER):
            local.append((e_re, e_im))
            t_re, t_im = _cmul(a_re, a_im, e_re, e_im)
            e_re, e_im = t_re + piece(c_re, j2), t_im + piece(c_im, j2)
        for i in range(n_top):
            m_re, m_im = mult(SUPER + i)
            t_re, t_im = _cmul(m_re, m_im, shifted(e_re, 1 << i), shifted(e_im, 1 << i))
            e_re, e_im = e_re + t_re, e_im + t_im
        s_re, s_im = shifted(e_re, 1), shifted(e_im, 1)
        for j2 in range(SUPER):
            if j2 == 0:
                pieces[j2][b] = (s_re, s_im)
            else:
                m_re, m_im = mult(j2)
                t_re, t_im = _cmul(m_re, m_im, s_re, s_im)
                pieces[j2][b] = (local[j2][0] + t_re, local[j2][1] + t_im)
    for gi in range(gb):
        sl = slice(gi * p, (gi + 1) * p)
        s_in = jnp.concatenate([jnp.concatenate([pieces[j2][b][0][sl], pieces[j2][b][1][sl]], axis=0)
                                for j2 in range(SUPER) for b in range(n_b)], axis=1)
        y = r[gi][:rows] + jnp.dot(wout_ref[gi], s_in.astype(BF16), preferred_element_type=F32)
        y_ref[gi] = y.reshape(CHUNK, SSM_GROUP, n).astype(y_ref.dtype)


def _s5_scan(u_t, lhs, wout, ap, n_b, c2, n_top):
    g, h = SSM_GROUPS, SSM_GROUP
    n = u_t.shape[-1]
    gb = SCAN_GROUPS
    grp = lambda *shape: pl.BlockSpec((gb,) + shape, lambda i: (i,) + (0,) * len(shape))
    return pl.pallas_call(
        functools.partial(_s5_group_kernel, n_b=n_b, c2=c2, n_top=n_top),
        out_shape=jax.ShapeDtypeStruct((g, CHUNK, h, n), BF16),
        grid=(g // gb,),
        in_specs=[pl.BlockSpec((CHUNK, gb * h, n), lambda i: (0, i, 0)),
                  grp(*lhs.shape[1:]), grp(*wout.shape[1:]), grp(*ap.shape[1:])],
        out_specs=grp(CHUNK, h, n),
        compiler_params=pltpu.CompilerParams(dimension_semantics=("arbitrary",),
                                             vmem_limit_bytes=48 * 1024 * 1024),
        name="s5_chunk_scan",
    )(u_t, lhs, wout, ap)


def _glu_kernel(y_ref, zs_ref, w_ref, b_ref, o_ref, *, n_b, c2):
    g, _, h, n = y_ref.shape
    for jj in range(PHASES):
        y = jax.nn.gelu(y_ref[:, jj].reshape(g * h, n).astype(F32))
        t = jnp.dot(w_ref[...], y.astype(BF16), preferred_element_type=F32) + b_ref[...]
        gated = (y * jax.nn.sigmoid(t) * zs_ref[jj].astype(F32)).T
        for b in range(n_b):
            o_ref[b, :, jj, :] = gated[b * c2:(b + 1) * c2]


def _glu(y_t, zs_t, w_glu, b_glu, n_b, c2):
    g, _, h, _ = y_t.shape
    lanes = n_b * c2
    halves = CHUNK // PHASES
    w_t = w_glu.T.astype(BF16)
    return pl.pallas_call(
        functools.partial(_glu_kernel, n_b=n_b, c2=c2),
        out_shape=jax.ShapeDtypeStruct((n_b, c2, SUPER * CHUNK, SSM_WIDTH), F32),
        grid=(SUPER, halves),
        in_specs=[pl.BlockSpec((g, PHASES, h, lanes), lambda j2, hf: (0, hf, 0, j2)),
                  pl.BlockSpec((PHASES, SSM_WIDTH, lanes), lambda j2, hf: (hf, 0, j2)),
                  pl.BlockSpec(w_t.shape, lambda j2, hf: (0, 0)),
                  pl.BlockSpec((SSM_WIDTH, 1), lambda j2, hf: (0, 0))],
        out_specs=pl.BlockSpec((n_b, c2, PHASES, SSM_WIDTH), lambda j2, hf: (0, 0, j2 * halves + hf, 0)),
        compiler_params=pltpu.CompilerParams(dimension_semantics=("arbitrary", "arbitrary"),
                                             vmem_limit_bytes=48 * 1024 * 1024),
        name="s5_glu",
    )(y_t, zs_t, w_t, b_glu.reshape(SSM_WIDTH, 1))


def _out_proj_kernel(ys_ref, ym_ref, x_ref, mod_ref, wa_ref, wb_ref, o_ref):
    y = (jnp.dot(ys_ref[...].astype(BF16), wa_ref[...], preferred_element_type=F32)
         + jnp.dot(ym_ref[...], wb_ref[...], preferred_element_type=F32))
    gate = mod_ref[...][:, 2 * D_MODEL:]
    o_ref[...] = x_ref[...] + gate * y


def _out_projection(ys, ym, x2, mod3, w_out, seq):
    t, d = x2.shape
    tm = min(512, seq)
    per_b = seq // tm
    wa = w_out[:SSM_WIDTH].astype(BF16)
    wb = w_out[SSM_WIDTH:].astype(BF16)
    tok = lambda w: pl.BlockSpec((tm, w), lambda i: (i, 0))
    return pl.pallas_call(
        _out_proj_kernel,
        out_shape=jax.ShapeDtypeStruct(x2.shape, F32),
        grid=(t // tm,),
        in_specs=[tok(SSM_WIDTH), tok(MLA_WIDTH), tok(d),
                  pl.BlockSpec((None, 1, 3 * d), lambda i: (i // per_b, 0, 0)),
                  pl.BlockSpec(wa.shape, lambda i: (0, 0)),
                  pl.BlockSpec(wb.shape, lambda i: (0, 0))],
        out_specs=tok(d),
        compiler_params=pltpu.CompilerParams(dimension_semantics=("arbitrary",)),
        name="output_projection",
    )(ys, ym, x2, mod3, wa, wb)


def kernel(x, c, positions, w_ada, b_ada, norm_g, w_in, log_dt, lam_re, lam_im, b_re, b_im, c_re, c_im, d_skip,
           w_glu, b_glu, q_a_g, w_q_b, kv_a_g, w_kv_b, q_norm_g, k_norm_g, w_out):
    n_b, seq, d = x.shape
    depth = w_ada.shape[0]
    c2 = seq // (CHUNK * SUPER)
    n_top = max(int(math.log2(c2)), 0)
    assert c2 * CHUNK * SUPER == seq and (1 << n_top) == c2
    rope = _rope_tables(positions)
    for l in range(depth):
        mod3 = _modulation(c, w_ada[l], b_ada[l]).reshape(n_b, 1, 3 * d)
        x2 = x.reshape(n_b * seq, d)
        lhs, wout, ap = _s5_coefficients(log_dt[l], lam_re[l], lam_im[l], b_re[l], b_im[l], c_re[l], c_im[l],
                                         d_skip[l], n_top)
        u_t, zs_t = _ssm_projection(x.reshape(n_b, c2, SUPER * CHUNK, d), mod3, norm_g[l], w_in[l])
        y_t = _s5_scan(u_t, lhs, wout, ap, n_b, c2, n_top)
        ys = _glu(y_t, zs_t, w_glu[l], b_glu[l], n_b, c2).reshape(n_b * seq, SSM_WIDTH)
        qt, k, vt, zm = _mla_projection(x2, mod3, norm_g[l], w_in[l], q_a_g[l], w_q_b[l],
                                        kv_a_g[l], w_kv_b[l], q_norm_g[l], k_norm_g[l], rope, n_b, seq)
        score_bound = (1.05 * math.sqrt(QK_HEAD) * math.log2(math.e)
                       * jnp.max(jnp.abs(q_norm_g[l])) * jnp.max(jnp.abs(k_norm_g[l])))
        ym = _attention(qt, k, vt, zm, score_bound)
        x = _out_projection(ys, ym, x2, mod3, w_out[l], seq).reshape(n_b, seq, d)
    return x
```

```python
import functools
import math

import numpy as np
import jax
import jax.numpy as jnp
from jax import lax
from jax.experimental import pallas as pl
from jax.experimental.pallas import tpu as pltpu

F32 = jnp.float32
BF16 = jnp.bfloat16
HIGHEST = lax.Precision.HIGHEST

D_MODEL = 1024
SSM_WIDTH = 512
SSM_GROUP = 16
SSM_GROUPS = 32
SSM_STATE = 64
MLA_HEADS = 8
QK_NOPE = 64
QK_ROPE = 32
QK_HEAD = QK_NOPE + QK_ROPE
V_HEAD = 64
MLA_WIDTH = MLA_HEADS * V_HEAD
Q_LORA = 384
KV_LORA = 256
ROPE_THETA = 10000.0
EPS = 1e-6
NEG_BIG = -1e30

LANES = 128
CHUNK = 16
SUPER = 4
ROPE_HALF = QK_ROPE // 2

_NT = (((1,), (1,)), ((), ()))
_TN = (((0,), (0,)), ((), ()))


def _silu(v):
    return v * jax.nn.sigmoid(v)


def _rms(v, gain):
    return v * lax.rsqrt(jnp.mean(v * v, axis=-1, keepdims=True) + EPS) * gain


def _mod_kernel(c_ref, w_ref, b_ref, o_ref):
    act = _silu(c_ref[...])
    o_ref[...] = jnp.dot(act, w_ref[...], preferred_element_type=F32, precision=HIGHEST) + b_ref[...]


def _modulation(c, w, b):
    n_b, d = c.shape
    rows = 8
    c_pad = jnp.zeros((rows, d), F32).at[:n_b].set(c)
    tn = 512
    out = pl.pallas_call(
        _mod_kernel,
        out_shape=jax.ShapeDtypeStruct((rows, 3 * d), F32),
        grid=(3 * d // tn,),
        in_specs=[pl.BlockSpec((rows, d), lambda n: (0, 0)),
                  pl.BlockSpec((d, tn), lambda n: (0, n)),
                  pl.BlockSpec((1, tn), lambda n: (0, n))],
        out_specs=pl.BlockSpec((rows, tn), lambda n: (0, n)),
        name="adaln_modulation",
    )(c_pad, w, b.reshape(1, -1))
    return out[:n_b]


def _rope_kernel(pos_ref, freq_ref, cos_ref, sin_ref):
    ang = freq_ref[...] * pos_ref[...].astype(F32)
    cos_ref[...] = jnp.cos(ang)
    sin_ref[...] = jnp.sin(ang)


def _rope_tables(positions):
    t = positions.size
    tt = min(t, 2048)
    inv_freq = (ROPE_THETA ** (-np.arange(ROPE_HALF, dtype=np.float64) * 2.0 / QK_ROPE)).astype(np.float32)
    return pl.pallas_call(
        _rope_kernel,
        out_shape=(jax.ShapeDtypeStruct((ROPE_HALF, t), F32),) * 2,
        grid=(t // tt,),
        in_specs=[pl.BlockSpec((1, tt), lambda i: (0, i)),
                  pl.BlockSpec((ROPE_HALF, 1), lambda i: (0, 0))],
        out_specs=(pl.BlockSpec((ROPE_HALF, tt), lambda i: (0, i)),) * 2,
        name="rope_tables",
    )(positions.reshape(1, t), jnp.asarray(inv_freq).reshape(ROPE_HALF, 1))


def _slab_source():
    src = np.full((LANES,), -1, np.int64)
    src[0:16] = QK_NOPE + np.arange(16)
    src[16:64] = np.arange(48)
    src[64:80] = QK_NOPE + 16 + np.arange(16)
    src[80:96] = 48 + np.arange(16)
    return src


def _to_slabs(w, per_head, src):
    k = w.shape[0]
    w3 = w.reshape(k, MLA_HEADS, per_head)
    valid = src >= 0
    cols = jnp.where(valid[None, None, :], w3[:, :, np.clip(src, 0, per_head - 1)], 0.0)
    return cols.reshape(k, MLA_HEADS * LANES)


V_ROWS = V_HEAD + 16


def _mla_proj_kernel(x_ref, mod_ref, ng_ref, w1_ref, qag_ref, kvag_ref, wqt_ref, wkt_ref, wkrt_ref, wvt_ref,
                     gq_ref, gk_ref, cost_ref, sint_ref, qt_ref, k_ref, vt_ref, zs_ref):
    d = D_MODEL
    tm = x_ref.shape[0]
    x = x_ref[...]
    mod = mod_ref[...]
    shift, scale = mod[:, :d], mod[:, d:2 * d]
    h = (_rms(x, ng_ref[...]) * (1.0 + scale) + shift).astype(BF16)
    p1 = jnp.dot(h, w1_ref[...], preferred_element_type=F32)
    o1, o2 = Q_LORA, Q_LORA + KV_LORA
    cqn = _rms(p1[:, :o1], qag_ref[...]).astype(BF16)
    ckvn = _rms(p1[:, o1:o2], kvag_ref[...]).astype(BF16)
    zs_ref[...] = _silu(p1[:, o2:]).astype(BF16)

    qt = lax.dot_general(wqt_ref[...], cqn, _NT, preferred_element_type=F32)
    kt = lax.dot_general(wkt_ref[...], ckvn, _NT, preferred_element_type=F32)
    krt = lax.dot_general(wkrt_ref[...], h, _NT, preferred_element_type=F32)
    vt = lax.dot_general(wvt_ref[...], ckvn, _NT, preferred_element_type=F32)
    cos_t, sin_t = cost_ref[...], sint_ref[...]
    gq = jnp.concatenate([gq_ref[...]] * (tm // LANES), axis=1)
    gk = jnp.concatenate([gk_ref[...]] * (tm // LANES), axis=1)
    first = lax.broadcasted_iota(jnp.int32, (V_ROWS - V_HEAD, tm), 0) == 0
    ones_rows = jnp.where(first, 1.0, 0.0).astype(BF16)
    r, half = ROPE_HALF, LANES // 2

    def norm_rope(slab, gain):
        ss = jnp.sum(slab * slab, axis=0, keepdims=True)
        n = slab * lax.rsqrt(ss * (1.0 / QK_HEAD) + EPS) * gain
        x1, x2 = n[0:r], n[half:half + r]
        return jnp.concatenate([x1 * cos_t - x2 * sin_t, n[r:half], x2 * cos_t + x1 * sin_t, n[half + r:]], axis=0)

    for head in range(MLA_HEADS):
        rows = slice(LANES * head, LANES * (head + 1))
        qt_ref[head] = norm_rope(qt[rows], gq).astype(BF16)
        k_ref[head] = norm_rope(kt[rows] + krt, gk).T.astype(BF16)
        vt_ref[head, 0:V_HEAD, :] = vt[V_HEAD * head:V_HEAD * (head + 1)].astype(BF16)
        vt_ref[head, V_HEAD:, :] = ones_rows


def _mla_projection(x2, mod3, norm_g, w_in, q_a_g, w_q_b, kv_a_g, w_kv_b, q_norm_g, k_norm_g, rope, n_b, seq):
    t, d = x2.shape
    cos_t, sin_t = rope
    tm = min(512, seq)
    per_b = seq // tm
    o2 = 2 * SSM_WIDTH
    o4, o5 = o2 + Q_LORA + KV_LORA, o2 + Q_LORA + KV_LORA + QK_ROPE
    src = _slab_source()
    valid = src >= 0
    rope_src = np.where(src >= QK_NOPE, src - QK_NOPE, -1)
    nope_src = np.where(valid & (src < QK_NOPE), src, -1)
    w1 = jnp.concatenate([w_in[:, o2:o4], w_in[:, o5:]], axis=1).astype(BF16)
    wkrt = jnp.where((rope_src >= 0)[:, None], w_in[:, o4:o5].T[np.clip(rope_src, 0, QK_ROPE - 1)], 0.0).astype(BF16)
    wqt = _to_slabs(w_q_b, QK_HEAD, src).T.astype(BF16)
    wkt = _to_slabs(w_kv_b, QK_NOPE + V_HEAD, nope_src).T.astype(BF16)
    wvt = w_kv_b.reshape(KV_LORA, MLA_HEADS, QK_NOPE + V_HEAD)[:, :, QK_NOPE:].reshape(KV_LORA, MLA_WIDTH).T.astype(BF16)
    q_scale = math.log2(math.e) / math.sqrt(QK_HEAD)
    slab_gain = lambda g: jnp.broadcast_to(jnp.where(valid, g[np.clip(src, 0, QK_HEAD - 1)], 0.0)[:, None],
                                           (LANES, LANES))
    gq = slab_gain(q_norm_g * q_scale)
    gk = slab_gain(k_norm_g)
    const = lambda shape: pl.BlockSpec(shape, lambda i: (0,) * len(shape))
    tok = lambda i: (i // per_b, 0, 0, i % per_b)
    return pl.pallas_call(
        _mla_proj_kernel,
        out_shape=(jax.ShapeDtypeStruct((n_b, MLA_HEADS, LANES, seq), BF16),
                   jax.ShapeDtypeStruct((n_b, MLA_HEADS, seq, LANES), BF16),
                   jax.ShapeDtypeStruct((n_b, MLA_HEADS, V_ROWS, seq), BF16),
                   jax.ShapeDtypeStruct((t, MLA_WIDTH), BF16)),
        grid=(t // tm,),
        in_specs=[pl.BlockSpec((tm, d), lambda i: (i, 0)),
                  pl.BlockSpec((None, 1, 3 * d), lambda i: (i // per_b, 0, 0)),
                  const((1, d)), const(w1.shape), const((1, Q_LORA)), const((1, KV_LORA)),
                  const(wqt.shape), const(wkt.shape), const(wkrt.shape), const(wvt.shape),
                  const((LANES, LANES)), const((LANES, LANES)),
                  pl.BlockSpec((ROPE_HALF, tm), lambda i: (0, i)),
                  pl.BlockSpec((ROPE_HALF, tm), lambda i: (0, i))],
        out_specs=(pl.BlockSpec((None, MLA_HEADS, LANES, tm), tok),
                   pl.BlockSpec((None, MLA_HEADS, tm, LANES), lambda i: (i // per_b, 0, i % per_b, 0)),
                   pl.BlockSpec((None, MLA_HEADS, V_ROWS, tm), tok),
                   pl.BlockSpec((tm, MLA_WIDTH), lambda i: (i, 0))),
        compiler_params=pltpu.CompilerParams(dimension_semantics=("arbitrary",),
                                             vmem_limit_bytes=48 * 1024 * 1024),
        name="mla_projection",
    )(x2, mod3, norm_g.reshape(1, d), w1, q_a_g.reshape(1, -1), kv_a_g.reshape(1, -1), wqt, wkt, wkrt, wvt, gq, gk,
      cos_t, sin_t)


ATTN_TQ = 2048
ATTN_TK = 256
HEADS_PER_STEP = 2


def _attn_kernel(qt_ref, k_ref, vt_ref, zs_ref, o_ref, s_scr):
    tq, tk = ATTN_TQ, ATTN_TK
    r = tq // tk
    assert r * tk == tq and r % 2 == 0
    qi = pl.program_id(2)

    def scores(slot, t, lo=0):
        start = pl.multiple_of(t * tk, tk)
        for e in range(HEADS_PER_STEP):
            s_scr[slot, e, :, lo:] = jnp.dot(k_ref[e, pl.ds(start, tk), :], qt_ref[e, :, lo:],
                                             preferred_element_type=F32)

    def process(slot, t, carry, lo=0, diagonal=False):
        start = pl.multiple_of(t * tk, tk)
        out = []
        for e in range(HEADS_PER_STEP):
            m, acc = carry[e]
            s = s_scr[slot, e, :, lo:]
            if diagonal:
                blk = s[:, :tk]
                ok = lax.broadcasted_iota(jnp.int32, blk.shape, 0) <= lax.broadcasted_iota(jnp.int32, blk.shape, 1)
                blk = jnp.where(ok, blk, NEG_BIG)
                s = jnp.concatenate([blk, s[:, tk:]], axis=1) if s.shape[1] > tk else blk
            m_old = m[:, lo:]
            m_new = jnp.maximum(m_old, jnp.max(s, axis=0, keepdims=True))
            p = jnp.exp2(s - m_new).astype(BF16)
            acc_new = (jnp.exp2(m_old - m_new) * acc[:, lo:]
                       + jnp.dot(vt_ref[e, :, pl.ds(start, tk)], p, preferred_element_type=F32))
            if lo:
                m_new = jnp.concatenate([m[:, :lo], m_new], axis=1)
                acc_new = jnp.concatenate([acc[:, :lo], acc_new], axis=1)
            out.append((m_new, acc_new))
        return tuple(out)

    def body(i, carry):
        base = i * r
        for u in range(r):
            scores((u + 1) % 2, base + u + 1)
            carry = process(u % 2, base + u, carry)
        return carry

    init = tuple((jnp.full((1, tq), NEG_BIG, F32), jnp.zeros((V_ROWS, tq), F32)) for _ in range(HEADS_PER_STEP))
    scores(0, 0)
    carry = lax.fori_loop(0, qi, body, init)
    base = qi * r
    for u in range(r):
        if u + 1 < r:
            scores((u + 1) % 2, base + u + 1, lo=(u + 1) * tk)
        carry = process(u % 2, base + u, carry, lo=u * tk, diagonal=True)
    _attn_finish(tuple(c[1] for c in carry), zs_ref, o_ref)


def _attn_finish(accs, zs_ref, o_ref):
    outs = [(acc[:V_HEAD] / acc[V_HEAD:V_HEAD + 1]).T for acc in accs]
    o_ref[...] = (jnp.concatenate(outs, axis=1) * zs_ref[...].astype(F32)).astype(o_ref.dtype)


def _attn_bounded_kernel(qt_ref, k_ref, vt_ref, zs_ref, o_ref, p_scr):
    tq, tk = ATTN_TQ, ATTN_TK
    r = tq // tk
    assert r * tk == tq and r % 2 == 0
    qi = pl.program_id(2)
    q0 = qi * tq

    def probs(slot, t, lo=0, may_cross=False):
        start = pl.multiple_of(t * tk, tk)
        for e in range(HEADS_PER_STEP):
            p = jnp.exp2(jnp.dot(k_ref[e, pl.ds(start, tk), :], qt_ref[e, :, lo:], preferred_element_type=F32))
            if may_cross:
                blk = p[:, :tk]
                key = start + lax.broadcasted_iota(jnp.int32, blk.shape, 0)
                qry = q0 + lo + lax.broadcasted_iota(jnp.int32, blk.shape, 1)
                blk = jnp.where(key <= qry, blk, 0.0)
                p = jnp.concatenate([blk, p[:, tk:]], axis=1) if p.shape[1] > tk else blk
            p_scr[slot, e, :, lo:] = p.astype(BF16)

    def accumulate(slot, t, accs, lo=0):
        start = pl.multiple_of(t * tk, tk)
        out = []
        for e in range(HEADS_PER_STEP):
            pv = jnp.dot(vt_ref[e, :, pl.ds(start, tk)], p_scr[slot, e, :, lo:], preferred_element_type=F32)
            acc = accs[e]
            out.append(acc + pv if lo == 0 else jnp.concatenate([acc[:, :lo], acc[:, lo:] + pv], axis=1))
        return tuple(out)

    def body(i, accs):
        base = i * r
        for u in range(r):
            probs((u + 1) % 2, base + u + 1, may_cross=(u == r - 1))
            accs = accumulate(u % 2, base + u, accs)
        return accs

    probs(0, 0, may_cross=True)
    accs = lax.fori_loop(0, qi, body, tuple(jnp.zeros((V_ROWS, tq), F32) for _ in range(HEADS_PER_STEP)))
    base = qi * r
    for u in range(r):
        if u + 1 < r:
            probs((u + 1) % 2, base + u + 1, lo=(u + 1) * tk, may_cross=True)
        accs = accumulate(u % 2, base + u, accs, lo=u * tk)
    _attn_finish(accs, zs_ref, o_ref)


SCORE_BOUND_LOG2 = 60.0


def _attention(qt, k, vt, zs, score_bound):
    n_b, n_h, seq, _ = k.shape
    tq = ATTN_TQ
    nq = seq // tq
    hp = HEADS_PER_STEP
    width = hp * V_HEAD

    def call(body, scratch):
        return pl.pallas_call(
            body,
            out_shape=jax.ShapeDtypeStruct(zs.shape, BF16),
            grid=(n_b, n_h // hp, nq),
            in_specs=[pl.BlockSpec((None, hp, LANES, tq), lambda b, h, i: (b, h, 0, i)),
                      pl.BlockSpec((None, hp, seq, LANES), lambda b, h, i: (b, h, 0, 0)),
                      pl.BlockSpec((None, hp, V_ROWS, seq), lambda b, h, i: (b, h, 0, 0)),
                      pl.BlockSpec((tq, width), lambda b, h, i: (b * nq + i, h))],
            out_specs=pl.BlockSpec((tq, width), lambda b, h, i: (b * nq + i, h)),
            scratch_shapes=scratch,
            compiler_params=pltpu.CompilerParams(dimension_semantics=("arbitrary", "arbitrary", "arbitrary"),
                                                 vmem_limit_bytes=48 * 1024 * 1024),
            name="causal_attention",
        )(qt, k, vt, zs)

    tiles = (2, hp, ATTN_TK, tq)
    return lax.cond(score_bound <= SCORE_BOUND_LOG2,
                    lambda: call(_attn_bounded_kernel, [pltpu.VMEM(tiles, BF16)]),
                    lambda: call(_attn_kernel, [pltpu.VMEM(tiles, F32)]))


def _s5_scan_steps(n_top):
    return [CHUNK * m for m in range(SUPER)] + [CHUNK * SUPER * (1 << i) for i in range(n_top)]


def _s5_exponents(n_top):
    return sorted(set(range(CHUNK + 1)) | set(_s5_scan_steps(n_top)))


def _s5_coef_kernel(kk_ref, *refs, n_top):
    for gi in range(SCAN_GROUPS):
        _s5_coef_one_group(kk_ref, *(r.at[gi] for r in refs), n_top=n_top)


def _s5_coef_one_group(kk_ref, lre_ref, lim_ref, ldt_ref, bre_ref, bim_ref, cre_ref, cim_ref, d_ref,
                       lhs_ref, wout_ref, ap_ref, *, n_top):
    h, p, n = SSM_GROUP, SSM_STATE, CHUNK
    lre, lim = lre_ref[...], lim_ref[...]
    dt = jnp.exp(ldt_ref[...])
    exps = _s5_exponents(n_top)
    kcol = kk_ref[...]
    mag = jnp.exp(kcol * (lre * dt))
    ang = kcol * (lim * dt)
    pow_re, pow_im = mag * jnp.cos(ang), mag * jnp.sin(ang)

    def power(kk):
        i = exps.index(kk)
        return pow_re[i:i + 1], pow_im[i:i + 1]

    lb_re, lb_im = power(1)
    nr, ni = lb_re - 1.0, lb_im
    den = lre * lre + lim * lim
    f_re = (nr * lre + ni * lim) / den
    f_im = (ni * lre - nr * lim) / den
    bre, bim = bre_ref[...].T, bim_ref[...].T
    bb_re = f_re * bre - f_im * bim
    bb_im = f_re * bim + f_im * bre
    cre, cim = cre_ref[...], cim_ref[...]
    cp_re, cp_im = [], []
    for kk in range(n + 1):
        pr, pi = power(kk)
        cp_re.append(cre * pr - cim * pi)
        cp_im.append(cre * pi + cim * pr)

    cpw = [jnp.concatenate([cp_re[kk], -cp_im[kk]], axis=1) for kk in range(n + 1)]
    bb_a = jnp.concatenate([bb_re, bb_im], axis=1)
    kw = lax.dot_general(jnp.concatenate(cpw[:n], axis=0), jnp.concatenate([bb_a] * n, axis=0), _NT,
                         preferred_element_type=F32, precision=HIGHEST)
    lane = lax.broadcasted_iota(jnp.int32, (h, n * h), 1)
    row = lax.broadcasted_iota(jnp.int32, (h, n * h), 0)
    col_blk = lax.shift_right_logical(lane, 4)
    d_tiled = jnp.concatenate([d_ref[...]] * n, axis=1)
    lag = [kw[h * kk:h * (kk + 1)] for kk in range(n)]
    lag[0] = lag[0] + jnp.where((lane & (h - 1)) == row, d_tiled, 0.0)
    rows = []
    for j in range(n):
        acc = jnp.zeros((h, n * h), F32)
        for jp in range(j + 1):
            acc = jnp.where(col_blk == jp, lag[j - jp], acc)
        rows.append(acc)
    lhs_ref[0:n * h, :] = jnp.concatenate(rows, axis=0).astype(lhs_ref.dtype)

    bb_b = jnp.concatenate([-bb_im, bb_re], axis=1)
    win = []
    for j in range(n):
        pr, pi = power(n - 1 - j)
        win.append(jnp.concatenate([pr, pr], axis=1) * bb_a + jnp.concatenate([pi, pi], axis=1) * bb_b)
    lhs_ref[n * h:, :] = jnp.concatenate(win, axis=0).T.astype(lhs_ref.dtype)
    wout_ref[...] = jnp.concatenate(cpw[1:], axis=0).astype(wout_ref.dtype)
    for i, kk in enumerate(_s5_scan_steps(n_top)):
        pr, pi = power(kk)
        ap_ref[i:i + 1, :] = jnp.concatenate([pr, pi], axis=1)


def _s5_coefficients(log_dt, lam_re, lam_im, b_re, b_im, c_re, c_im, d_skip, n_top):
    g, p, h = SSM_GROUPS, SSM_STATE, SSM_GROUP
    n_ap = SUPER + n_top
    gb = SCAN_GROUPS
    grp = lambda *shape: pl.BlockSpec((gb,) + shape, lambda i: (i,) + (0,) * len(shape))
    exps = np.asarray(_s5_exponents(n_top), np.float32)
    kk = np.zeros((-(-exps.size // 8) * 8, 1), np.float32)
    kk[:exps.size, 0] = exps
    return pl.pallas_call(
        functools.partial(_s5_coef_kernel, n_top=n_top),
        out_shape=(jax.ShapeDtypeStruct((g, CHUNK * h + 2 * p, CHUNK * h), BF16),
                   jax.ShapeDtypeStruct((g, CHUNK * h, 2 * p), BF16),
                   jax.ShapeDtypeStruct((g, n_ap, 2 * p), F32)),
        grid=(g // gb,),
        in_specs=[pl.BlockSpec(kk.shape, lambda i: (0, 0)),
                  grp(1, p), grp(1, p), grp(1, 1), grp(p, h), grp(p, h), grp(h, p), grp(h, p), grp(1, h)],
        out_specs=(grp(CHUNK * h + 2 * p, CHUNK * h), grp(CHUNK * h, 2 * p), grp(n_ap, 2 * p)),
        name="s5_coefficients",
    )(jnp.asarray(kk), lam_re.reshape(g, 1, p), lam_im.reshape(g, 1, p), log_dt.reshape(g, 1, 1), b_re, b_im,
      c_re, c_im, d_skip.reshape(g, 1, h))


PHASES = 8


def _ssm_proj_kernel(x_ref, mod_ref, ng_ref, w_ref, u_ref, z_ref, rows_scr, *, n_b):
    d = D_MODEL
    for jj in range(PHASES):
        hs = []
        for b in range(n_b):
            mod = mod_ref[b]
            shift, scale = mod[:, :d], mod[:, d:2 * d]
            rows_scr[b] = x_ref[b, :, jj, :]
            hs.append((_rms(rows_scr[b], ng_ref[...]) * (1.0 + scale) + shift).astype(BF16))
        h = jnp.concatenate(hs, axis=0)
        r = lax.dot_general(w_ref[...], h, _NT, preferred_element_type=F32)
        u_ref[jj] = r[:SSM_WIDTH].astype(BF16)
        z_ref[jj] = _silu(r[SSM_WIDTH:]).astype(BF16)


def _ssm_projection(x4, mod3, norm_g, w_in):
    n_b, c2, _, d = x4.shape
    lanes = n_b * c2
    halves = CHUNK // PHASES
    w_t = w_in[:, :2 * SSM_WIDTH].T.astype(BF16)
    out = jax.ShapeDtypeStruct((CHUNK, SSM_WIDTH, SUPER * lanes), BF16)
    out_spec = pl.BlockSpec((PHASES, SSM_WIDTH, lanes), lambda j2, hf: (hf, 0, j2))
    return pl.pallas_call(
        functools.partial(_ssm_proj_kernel, n_b=n_b),
        out_shape=(out, out),
        grid=(SUPER, halves),
        in_specs=[pl.BlockSpec((n_b, c2, PHASES, d), lambda j2, hf: (0, 0, j2 * halves + hf, 0)),
                  pl.BlockSpec((n_b, 1, 3 * d), lambda j2, hf: (0, 0, 0)),
                  pl.BlockSpec((1, d), lambda j2, hf: (0, 0)),
                  pl.BlockSpec(w_t.shape, lambda j2, hf: (0, 0))],
        out_specs=(out_spec, out_spec),
        scratch_shapes=[pltpu.VMEM((n_b, c2, d), F32)],
        compiler_params=pltpu.CompilerParams(dimension_semantics=("arbitrary", "arbitrary"),
                                             vmem_limit_bytes=56 * 1024 * 1024),
        name="ssm_projection",
    )(x4, mod3, norm_g.reshape(1, d), w_t)


def _cmul(ar, ai, xr, xi):
    return ar * xr - ai * xi, ar * xi + ai * xr


SCAN_GROUPS = 4


def _s5_group_kernel(x_ref, lhs_ref, wout_ref, ap_ref, y_ref, *, n_b, c2, n_top):
    p, gb = SSM_STATE, SCAN_GROUPS
    rows = CHUNK * SSM_GROUP
    n = SUPER * n_b * c2
    r = [jnp.dot(lhs_ref[gi], x_ref[:, SSM_GROUP * gi:SSM_GROUP * (gi + 1), :].reshape(rows, n),
                 preferred_element_type=F32) for gi in range(gb)]
    c_re = jnp.concatenate([r[gi][rows:rows + p] for gi in range(gb)], axis=0)
    c_im = jnp.concatenate([r[gi][rows + p:] for gi in range(gb)], axis=0)

    def mult(i):
        rep = [jnp.broadcast_to(ap_ref[gi, i:i + 1, :], (2 * p, 2 * p)).T for gi in range(gb)]
        return (jnp.concatenate([m[:p, :c2] for m in rep], axis=0),
                jnp.concatenate([m[p:, :c2] for m in rep], axis=0))

    lane = lax.broadcasted_iota(jnp.int32, (gb * p, c2), 1)

    def shifted(a, sh):
        return jnp.where(lane >= sh, pltpu.roll(a, sh, 1), 0.0)

    pieces = [[None] * n_b for _ in range(SUPER)]
    for b in range(n_b):
        piece = lambda a, j2: a[:, (j2 * n_b + b) * c2:(j2 * n_b + b + 1) * c2]
        a_re, a_im = mult(1)
        e_re = jnp.zeros((gb * p, c2), F32)
        e_im = jnp.zeros((gb * p, c2), F32)
        local = []
        for j2 in range(SUPER):
            local.append((e_re, e_im))
            t_re, t_im = _cmul(a_re, a_im, e_re, e_im)
            e_re, e_im = t_re + piece(c_re, j2), t_im + piece(c_im, j2)
        for i in range(n_top):
            m_re, m_im = mult(SUPER + i)
            t_re, t_im = _cmul(m_re, m_im, shifted(e_re, 1 << i), shifted(e_im, 1 << i))
            e_re, e_im = e_re + t_re, e_im + t_im
        s_re, s_im = shifted(e_re, 1), shifted(e_im, 1)
        for j2 in range(SUPER):
            if j2 == 0:
                pieces[j2][b] = (s_re, s_im)
            else:
                m_re, m_im = mult(j2)
                t_re, t_im = _cmul(m_re, m_im, s_re, s_im)
                pieces[j2][b] = (local[j2][0] + t_re, local[j2][1] + t_im)
    for gi in range(gb):
        sl = slice(gi * p, (gi + 1) * p)
        s_in = jnp.concatenate([jnp.concatenate([pieces[j2][b][0][sl], pieces[j2][b][1][sl]], axis=0)
                                for j2 in range(SUPER) for b in range(n_b)], axis=1)
        y = r[gi][:rows] + jnp.dot(wout_ref[gi], s_in.astype(BF16), preferred_element_type=F32)
        y_ref[gi] = y.reshape(CHUNK, SSM_GROUP, n).astype(y_ref.dtype)


def _s5_scan(u_t, lhs, wout, ap, n_b, c2, n_top):
    g, h = SSM_GROUPS, SSM_GROUP
    n = u_t.shape[-1]
    gb = SCAN_GROUPS
    grp = lambda *shape: pl.BlockSpec((gb,) + shape, lambda i: (i,) + (0,) * len(shape))
    return pl.pallas_call(
        functools.partial(_s5_group_kernel, n_b=n_b, c2=c2, n_top=n_top),
        out_shape=jax.ShapeDtypeStruct((g, CHUNK, h, n), BF16),
        grid=(g // gb,),
        in_specs=[pl.BlockSpec((CHUNK, gb * h, n), lambda i: (0, i, 0)),
                  grp(*lhs.shape[1:]), grp(*wout.shape[1:]), grp(*ap.shape[1:])],
        out_specs=grp(CHUNK, h, n),
        compiler_params=pltpu.CompilerParams(dimension_semantics=("arbitrary",),
                                             vmem_limit_bytes=48 * 1024 * 1024),
        name="s5_chunk_scan",
    )(u_t, lhs, wout, ap)


def _glu_kernel(y_ref, zs_ref, w_ref, b_ref, o_ref, *, n_b, c2):
    g, _, h, n = y_ref.shape
    for jj in range(PHASES):
        y = jax.nn.gelu(y_ref[:, jj].reshape(g * h, n).astype(F32))
        t = jnp.dot(w_ref[...], y.astype(BF16), preferred_element_type=F32) + b_ref[...]
        gated = (y * jax.nn.sigmoid(t) * zs_ref[jj].astype(F32)).T
        for b in range(n_b):
            o_ref[b, :, jj, :] = gated[b * c2:(b + 1) * c2]


def _glu(y_t, zs_t, w_glu, b_glu, n_b, c2):
    g, _, h, _ = y_t.shape
    lanes = n_b * c2
    halves = CHUNK // PHASES
    w_t = w_glu.T.astype(BF16)
    return pl.pallas_call(
        functools.partial(_glu_kernel, n_b=n_b, c2=c2),
        out_shape=jax.ShapeDtypeStruct((n_b, c2, SUPER * CHUNK, SSM_WIDTH), F32),
        grid=(SUPER, halves),
        in_specs=[pl.BlockSpec((g, PHASES, h, lanes), lambda j2, hf: (0, hf, 0, j2)),
                  pl.BlockSpec((PHASES, SSM_WIDTH, lanes), lambda j2, hf: (hf, 0, j2)),
                  pl.BlockSpec(w_t.shape, lambda j2, hf: (0, 0)),
                  pl.BlockSpec((SSM_WIDTH, 1), lambda j2, hf: (0, 0))],
        out_specs=pl.BlockSpec((n_b, c2, PHASES, SSM_WIDTH), lambda j2, hf: (0, 0, j2 * halves + hf, 0)),
        compiler_params=pltpu.CompilerParams(dimension_semantics=("arbitrary", "arbitrary"),
                                             vmem_limit_bytes=48 * 1024 * 1024),
        name="s5_glu",
    )(y_t, zs_t, w_t, b_glu.reshape(SSM_WIDTH, 1))


def _out_proj_kernel(ys_ref, ym_ref, x_ref, mod_ref, wa_ref, wb_ref, o_ref):
    y = (jnp.dot(ys_ref[...].astype(BF16), wa_ref[...], preferred_element_type=F32)
         + jnp.dot(ym_ref[...], wb_ref[...], preferred_element_type=F32))
    gate = mod_ref[...][:, 2 * D_MODEL:]
    o_ref[...] = x_ref[...] + gate * y


def _out_projection(ys, ym, x2, mod3, w_out, seq):
    t, d = x2.shape
    tm = min(512, seq)
    per_b = seq // tm
    wa = w_out[:SSM_WIDTH].astype(BF16)
    wb = w_out[SSM_WIDTH:].astype(BF16)
    tok = lambda w: pl.BlockSpec((tm, w), lambda i: (i, 0))
    return pl.pallas_call(
        _out_proj_kernel,
        out_shape=jax.ShapeDtypeStruct(x2.shape, F32),
        grid=(t // tm,),
        in_specs=[tok(SSM_WIDTH), tok(MLA_WIDTH), tok(d),
                  pl.BlockSpec((None, 1, 3 * d), lambda i: (i // per_b, 0, 0)),
                  pl.BlockSpec(wa.shape, lambda i: (0, 0)),
                  pl.BlockSpec(wb.shape, lambda i: (0, 0))],
        out_specs=tok(d),
        compiler_params=pltpu.CompilerParams(dimension_semantics=("arbitrary",)),
        name="output_projection",
    )(ys, ym, x2, mod3, wa, wb)


def kernel(x, c, positions, w_ada, b_ada, norm_g, w_in, log_dt, lam_re, lam_im, b_re, b_im, c_re, c_im, d_skip,
           w_glu, b_glu, q_a_g, w_q_b, kv_a_g, w_kv_b, q_norm_g, k_norm_g, w_out):
    n_b, seq, d = x.shape
    depth = w_ada.shape[0]
    c2 = seq // (CHUNK * SUPER)
    n_top = max(int(math.log2(c2)), 0)
    assert c2 * CHUNK * SUPER == seq and (1 << n_top) == c2
    rope = _rope_tables(positions)
    for l in range(depth):
        mod3 = _modulation(c, w_ada[l], b_ada[l]).reshape(n_b, 1, 3 * d)
        x2 = x.reshape(n_b * seq, d)
        lhs, wout, ap = _s5_coefficients(log_dt[l], lam_re[l], lam_im[l], b_re[l], b_im[l], c_re[l], c_im[l],
                                         d_skip[l], n_top)
        u_t, zs_t = _ssm_projection(x.reshape(n_b, c2, SUPER * CHUNK, d), mod3, norm_g[l], w_in[l])
        y_t = _s5_scan(u_t, lhs, wout, ap, n_b, c2, n_top)
        ys = _glu(y_t, zs_t, w_glu[l], b_glu[l], n_b, c2).reshape(n_b * seq, SSM_WIDTH)
        qt, k, vt, zm = _mla_projection(x2, mod3, norm_g[l], w_in[l], q_a_g[l], w_q_b[l],
                                        kv_a_g[l], w_kv_b[l], q_norm_g[l], k_norm_g[l], rope, n_b, seq)
        score_bound = (1.05 * math.sqrt(QK_HEAD) * math.log2(math.e)
                       * jnp.max(jnp.abs(q_norm_g[l])) * jnp.max(jnp.abs(k_norm_g[l])))
        ym = _attention(qt, k, vt, zm, score_bound)
        x = _out_projection(ys, ym, x2, mod3, w_out[l], seq).reshape(n_b, seq, d)
    return x
```

```python
import functools
import math

import numpy as np
import jax
import jax.numpy as jnp
from jax import lax
from jax.experimental import pallas as pl
from jax.experimental.pallas import tpu as pltpu

F32 = jnp.float32
BF16 = jnp.bfloat16
HIGHEST = lax.Precision.HIGHEST

D_MODEL = 1024
SSM_WIDTH = 512
SSM_GROUP = 16
SSM_GROUPS = 32
SSM_STATE = 64
MLA_HEADS = 8
QK_NOPE = 64
QK_ROPE = 32
QK_HEAD = QK_NOPE + QK_ROPE
V_HEAD = 64
MLA_WIDTH = MLA_HEADS * V_HEAD
Q_LORA = 384
KV_LORA = 256
ROPE_THETA = 10000.0
EPS = 1e-6
NEG_BIG = -1e30

LANES = 128
CHUNK = 16
SUPER = 4
ROPE_HALF = QK_ROPE // 2

_NT = (((1,), (1,)), ((), ()))
_TN = (((0,), (0,)), ((), ()))


def _silu(v):
    return v * jax.nn.sigmoid(v)


def _rms(v, gain):
    return v * lax.rsqrt(jnp.mean(v * v, axis=-1, keepdims=True) + EPS) * gain


def _mod_kernel(c_ref, w_ref, b_ref, o_ref):
    act = _silu(c_ref[...])
    o_ref[...] = jnp.dot(act, w_ref[...], preferred_element_type=F32, precision=HIGHEST) + b_ref[...]


def _modulation(c, w, b):
    n_b, d = c.shape
    rows = 8
    c_pad = jnp.zeros((rows, d), F32).at[:n_b].set(c)
    tn = 512
    out = pl.pallas_call(
        _mod_kernel,
        out_shape=jax.ShapeDtypeStruct((rows, 3 * d), F32),
        grid=(3 * d // tn,),
        in_specs=[pl.BlockSpec((rows, d), lambda n: (0, 0)),
                  pl.BlockSpec((d, tn), lambda n: (0, n)),
                  pl.BlockSpec((1, tn), lambda n: (0, n))],
        out_specs=pl.BlockSpec((rows, tn), lambda n: (0, n)),
        name="adaln_modulation",
    )(c_pad, w, b.reshape(1, -1))
    return out[:n_b]


def _rope_kernel(pos_ref, freq_ref, cos_ref, sin_ref):
    ang = freq_ref[...] * pos_ref[...].astype(F32)
    cos_ref[...] = jnp.cos(ang)
    sin_ref[...] = jnp.sin(ang)


def _rope_tables(positions):
    t = positions.size
    tt = min(t, 2048)
    inv_freq = (ROPE_THETA ** (-np.arange(ROPE_HALF, dtype=np.float64) * 2.0 / QK_ROPE)).astype(np.float32)
    return pl.pallas_call(
        _rope_kernel,
        out_shape=(jax.ShapeDtypeStruct((ROPE_HALF, t), F32),) * 2,
        grid=(t // tt,),
        in_specs=[pl.BlockSpec((1, tt), lambda i: (0, i)),
                  pl.BlockSpec((ROPE_HALF, 1), lambda i: (0, 0))],
        out_specs=(pl.BlockSpec((ROPE_HALF, tt), lambda i: (0, i)),) * 2,
        name="rope_tables",
    )(positions.reshape(1, t), jnp.asarray(inv_freq).reshape(ROPE_HALF, 1))


def _slab_source():
    src = np.full((LANES,), -1, np.int64)
    src[0:16] = QK_NOPE + np.arange(16)
    src[16:64] = np.arange(48)
    src[64:80] = QK_NOPE + 16 + np.arange(16)
    src[80:96] = 48 + np.arange(16)
    return src


def _to_slabs(w, per_head, src):
    k = w.shape[0]
    w3 = w.reshape(k, MLA_HEADS, per_head)
    valid = src >= 0
    cols = jnp.where(valid[None, None, :], w3[:, :, np.clip(src, 0, per_head - 1)], 0.0)
    return cols.reshape(k, MLA_HEADS * LANES)


V_ROWS = V_HEAD + 16


def _mla_proj_kernel(x_ref, mod_ref, ng_ref, w1_ref, qag_ref, kvag_ref, wqt_ref, wkt_ref, wvt_ref,
                     gq_ref, gk_ref, cost_ref, sint_ref, qt_ref, k_ref, vt_ref, zs_ref):
    d = D_MODEL
    tm = x_ref.shape[0]
    x = x_ref[...]
    mod = mod_ref[...]
    shift, scale = mod[:, :d], mod[:, d:2 * d]
    h = (_rms(x, ng_ref[...] * (1.0 + scale)) + shift).astype(BF16)
    p1 = jnp.dot(h, w1_ref[...], preferred_element_type=F32)
    o1, o2, o3 = Q_LORA, Q_LORA + KV_LORA, Q_LORA + KV_LORA + MLA_WIDTH
    cqn = _rms(p1[:, :o1], qag_ref[...]).astype(BF16)
    ckvn = _rms(p1[:, o1:o2], kvag_ref[...]).astype(BF16)
    zs_ref[...] = _silu(p1[:, o2:o3]).astype(BF16)

    qt = lax.dot_general(wqt_ref[...], cqn, _NT, preferred_element_type=F32)
    kt = lax.dot_general(wkt_ref[...], ckvn, _NT, preferred_element_type=F32)
    krt = p1[:, o3:].T
    vt = lax.dot_general(wvt_ref[...], ckvn, _NT, preferred_element_type=F32)
    cos_t, sin_t = cost_ref[...], sint_ref[...]
    gq = jnp.concatenate([gq_ref[...]] * (tm // LANES), axis=1)
    gk = jnp.concatenate([gk_ref[...]] * (tm // LANES), axis=1)
    first = lax.broadcasted_iota(jnp.int32, (V_ROWS - V_HEAD, tm), 0) == 0
    ones_rows = jnp.where(first, 1.0, 0.0).astype(BF16)
    r, half = ROPE_HALF, LANES // 2

    def norm_rope(slab, gain):
        ss = jnp.sum(slab * slab, axis=0, keepdims=True)
        n = slab * lax.rsqrt(ss * (1.0 / QK_HEAD) + EPS) * gain
        x1, x2 = n[0:r], n[half:half + r]
        return jnp.concatenate([x1 * cos_t - x2 * sin_t, n[r:half], x2 * cos_t + x1 * sin_t, n[half + r:]], axis=0)

    for head in range(MLA_HEADS):
        rows = slice(LANES * head, LANES * (head + 1))
        qt_ref[head] = norm_rope(qt[rows], gq).astype(BF16)
        k_ref[head] = norm_rope(kt[rows] + krt, gk).T.astype(BF16)
        vt_ref[head, 0:V_HEAD, :] = vt[V_HEAD * head:V_HEAD * (head + 1)].astype(BF16)
        vt_ref[head, V_HEAD:, :] = ones_rows


def _mla_projection(x2, mod3, norm_g, w_in, q_a_g, w_q_b, kv_a_g, w_kv_b, q_norm_g, k_norm_g, rope, n_b, seq):
    t, d = x2.shape
    cos_t, sin_t = rope
    tm = min(512, seq)
    per_b = seq // tm
    o2 = 2 * SSM_WIDTH
    o4, o5 = o2 + Q_LORA + KV_LORA, o2 + Q_LORA + KV_LORA + QK_ROPE
    src = _slab_source()
    valid = src >= 0
    rope_src = np.where(src >= QK_NOPE, src - QK_NOPE, -1)
    nope_src = np.where(valid & (src < QK_NOPE), src, -1)
    wkr = jnp.where((rope_src >= 0)[None, :], w_in[:, o4:o5][:, np.clip(rope_src, 0, QK_ROPE - 1)], 0.0)
    w1 = jnp.concatenate([w_in[:, o2:o4], w_in[:, o5:], wkr], axis=1).astype(BF16)
    wqt = _to_slabs(w_q_b, QK_HEAD, src).T.astype(BF16)
    wkt = _to_slabs(w_kv_b, QK_NOPE + V_HEAD, nope_src).T.astype(BF16)
    wvt = w_kv_b.reshape(KV_LORA, MLA_HEADS, QK_NOPE + V_HEAD)[:, :, QK_NOPE:].reshape(KV_LORA, MLA_WIDTH).T.astype(BF16)
    q_scale = math.log2(math.e) / math.sqrt(QK_HEAD)
    slab_gain = lambda g: jnp.broadcast_to(jnp.where(valid, g[np.clip(src, 0, QK_HEAD - 1)], 0.0)[:, None],
                                           (LANES, LANES))
    gq = slab_gain(q_norm_g * q_scale)
    gk = slab_gain(k_norm_g)
    const = lambda shape: pl.BlockSpec(shape, lambda i: (0,) * len(shape))
    tok = lambda i: (i // per_b, 0, 0, i % per_b)
    return pl.pallas_call(
        _mla_proj_kernel,
        out_shape=(jax.ShapeDtypeStruct((n_b, MLA_HEADS, LANES, seq), BF16),
                   jax.ShapeDtypeStruct((n_b, MLA_HEADS, seq, LANES), BF16),
                   jax.ShapeDtypeStruct((n_b, MLA_HEADS, V_ROWS, seq), BF16),
                   jax.ShapeDtypeStruct((t, MLA_WIDTH), BF16)),
        grid=(t // tm,),
        in_specs=[pl.BlockSpec((tm, d), lambda i: (i, 0)),
                  pl.BlockSpec((None, 1, 3 * d), lambda i: (i // per_b, 0, 0)),
                  const((1, d)), const(w1.shape), const((1, Q_LORA)), const((1, KV_LORA)),
                  const(wqt.shape), const(wkt.shape), const(wvt.shape),
                  const((LANES, LANES)), const((LANES, LANES)),
                  pl.BlockSpec((ROPE_HALF, tm), lambda i: (0, i)),
                  pl.BlockSpec((ROPE_HALF, tm), lambda i: (0, i))],
        out_specs=(pl.BlockSpec((None, MLA_HEADS, LANES, tm), tok),
                   pl.BlockSpec((None, MLA_HEADS, tm, LANES), lambda i: (i // per_b, 0, i % per_b, 0)),
                   pl.BlockSpec((None, MLA_HEADS, V_ROWS, tm), tok),
                   pl.BlockSpec((tm, MLA_WIDTH), lambda i: (i, 0))),
        compiler_params=pltpu.CompilerParams(dimension_semantics=("arbitrary",),
                                             vmem_limit_bytes=48 * 1024 * 1024),
        name="mla_projection",
    )(x2, mod3, norm_g.reshape(1, d), w1, q_a_g.reshape(1, -1), kv_a_g.reshape(1, -1), wqt, wkt, wvt, gq, gk,
      cos_t, sin_t)


ATTN_TQ = 2048
ATTN_TK = 256
HEADS_PER_STEP = 2


def _attn_kernel(qt_ref, k_ref, vt_ref, zs_ref, o_ref, s_scr):
    tq, tk = ATTN_TQ, ATTN_TK
    r = tq // tk
    assert r * tk == tq and r % 2 == 0
    qi = pl.program_id(2)

    def scores(slot, t, lo=0):
        start = pl.multiple_of(t * tk, tk)
        for e in range(HEADS_PER_STEP):
            s_scr[slot, e, :, lo:] = jnp.dot(k_ref[e, pl.ds(start, tk), :], qt_ref[e, :, lo:],
                                             preferred_element_type=F32)

    def process(slot, t, carry, lo=0, diagonal=False):
        start = pl.multiple_of(t * tk, tk)
        out = []
        for e in range(HEADS_PER_STEP):
            m, acc = carry[e]
            s = s_scr[slot, e, :, lo:]
            if diagonal:
                blk = s[:, :tk]
                ok = lax.broadcasted_iota(jnp.int32, blk.shape, 0) <= lax.broadcasted_iota(jnp.int32, blk.shape, 1)
                blk = jnp.where(ok, blk, NEG_BIG)
                s = jnp.concatenate([blk, s[:, tk:]], axis=1) if s.shape[1] > tk else blk
            m_old = m[:, lo:]
            m_new = jnp.maximum(m_old, jnp.max(s, axis=0, keepdims=True))
            p = jnp.exp2(s - m_new).astype(BF16)
            acc_new = (jnp.exp2(m_old - m_new) * acc[:, lo:]
                       + jnp.dot(vt_ref[e, :, pl.ds(start, tk)], p, preferred_element_type=F32))
            if lo:
                m_new = jnp.concatenate([m[:, :lo], m_new], axis=1)
                acc_new = jnp.concatenate([acc[:, :lo], acc_new], axis=1)
            out.append((m_new, acc_new))
        return tuple(out)

    def body(i, carry):
        base = i * r
        for u in range(r):
            scores((u + 1) % 2, base + u + 1)
            carry = process(u % 2, base + u, carry)
        return carry

    init = tuple((jnp.full((1, tq), NEG_BIG, F32), jnp.zeros((V_ROWS, tq), F32)) for _ in range(HEADS_PER_STEP))
    scores(0, 0)
    carry = lax.fori_loop(0, qi, body, init)
    base = qi * r
    for u in range(r):
        if u + 1 < r:
            scores((u + 1) % 2, base + u + 1, lo=(u + 1) * tk)
        carry = process(u % 2, base + u, carry, lo=u * tk, diagonal=True)
    _attn_finish(tuple(c[1] for c in carry), zs_ref, o_ref)


def _attn_finish(accs, zs_ref, o_ref):
    outs = [(acc[:V_HEAD] / acc[V_HEAD:V_HEAD + 1]).T for acc in accs]
    o_ref[...] = (jnp.concatenate(outs, axis=1) * zs_ref[...].astype(F32)).astype(o_ref.dtype)


def _attn_bounded_kernel(qt_ref, k_ref, vt_ref, zs_ref, o_ref, p_scr):
    tq, tk = ATTN_TQ, ATTN_TK
    r = tq // tk
    assert r * tk == tq and r % 2 == 0
    qi = pl.program_id(2)
    q0 = qi * tq

    def probs(slot, t, lo=0, may_cross=False):
        start = pl.multiple_of(t * tk, tk)
        for e in range(HEADS_PER_STEP):
            p = jnp.exp2(jnp.dot(k_ref[e, pl.ds(start, tk), :], qt_ref[e, :, lo:], preferred_element_type=F32))
            if may_cross:
                blk = p[:, :tk]
                key = start + lax.broadcasted_iota(jnp.int32, blk.shape, 0)
                qry = q0 + lo + lax.broadcasted_iota(jnp.int32, blk.shape, 1)
                blk = jnp.where(key <= qry, blk, 0.0)
                p = jnp.concatenate([blk, p[:, tk:]], axis=1) if p.shape[1] > tk else blk
            p_scr[slot, e, :, lo:] = p.astype(BF16)

    def accumulate(slot, t, accs, lo=0):
        start = pl.multiple_of(t * tk, tk)
        out = []
        for e in range(HEADS_PER_STEP):
            pv = jnp.dot(vt_ref[e, :, pl.ds(start, tk)], p_scr[slot, e, :, lo:], preferred_element_type=F32)
            acc = accs[e]
            out.append(acc + pv if lo == 0 else jnp.concatenate([acc[:, :lo], acc[:, lo:] + pv], axis=1))
        return tuple(out)

    def body(i, accs):
        base = i * r
        for u in range(r):
            probs((u + 1) % 2, base + u + 1, may_cross=(u == r - 1))
            accs = accumulate(u % 2, base + u, accs)
        return accs

    probs(0, 0, may_cross=True)
    accs = lax.fori_loop(0, qi, body, tuple(jnp.zeros((V_ROWS, tq), F32) for _ in range(HEADS_PER_STEP)))
    base = qi * r
    for u in range(r):
        if u + 1 < r:
            probs((u + 1) % 2, base + u + 1, lo=(u + 1) * tk, may_cross=True)
        accs = accumulate(u % 2, base + u, accs, lo=u * tk)
    _attn_finish(accs, zs_ref, o_ref)


SCORE_BOUND_LOG2 = 60.0


def _attention(qt, k, vt, zs, score_bound):
    n_b, n_h, seq, _ = k.shape
    tq = ATTN_TQ
    nq = seq // tq
    hp = HEADS_PER_STEP
    width = hp * V_HEAD

    def call(body, scratch):
        return pl.pallas_call(
            body,
            out_shape=jax.ShapeDtypeStruct(zs.shape, BF16),
            grid=(n_b, n_h // hp, nq),
            in_specs=[pl.BlockSpec((None, hp, LANES, tq), lambda b, h, i: (b, h, 0, i)),
                      pl.BlockSpec((None, hp, seq, LANES), lambda b, h, i: (b, h, 0, 0)),
                      pl.BlockSpec((None, hp, V_ROWS, seq), lambda b, h, i: (b, h, 0, 0)),
                      pl.BlockSpec((tq, width), lambda b, h, i: (b * nq + i, h))],
            out_specs=pl.BlockSpec((tq, width), lambda b, h, i: (b * nq + i, h)),
            scratch_shapes=scratch,
            compiler_params=pltpu.CompilerParams(dimension_semantics=("arbitrary", "arbitrary", "arbitrary"),
                                                 vmem_limit_bytes=48 * 1024 * 1024),
            name="causal_attention",
        )(qt, k, vt, zs)

    tiles = (2, hp, ATTN_TK, tq)
    return lax.cond(score_bound <= SCORE_BOUND_LOG2,
                    lambda: call(_attn_bounded_kernel, [pltpu.VMEM(tiles, BF16)]),
                    lambda: call(_attn_kernel, [pltpu.VMEM(tiles, F32)]))


def _s5_scan_steps(n_top):
    return [CHUNK * m for m in range(SUPER)] + [CHUNK * SUPER * (1 << i) for i in range(n_top)]


def _s5_exponents(n_top):
    return sorted(set(range(CHUNK + 1)) | set(_s5_scan_steps(n_top)))


def _s5_coef_kernel(kk_ref, *refs, n_top):
    for gi in range(SCAN_GROUPS):
        _s5_coef_one_group(kk_ref, *(r.at[gi] for r in refs), n_top=n_top)


def _s5_coef_one_group(kk_ref, lre_ref, lim_ref, ldt_ref, bre_ref, bim_ref, cre_ref, cim_ref, d_ref,
                       lhs_ref, wout_ref, ap_ref, *, n_top):
    h, p, n = SSM_GROUP, SSM_STATE, CHUNK
    lre, lim = lre_ref[...], lim_ref[...]
    dt = jnp.exp(ldt_ref[...])
    exps = _s5_exponents(n_top)
    kcol = kk_ref[...]
    mag = jnp.exp(kcol * (lre * dt))
    ang = kcol * (lim * dt)
    pow_re, pow_im = mag * jnp.cos(ang), mag * jnp.sin(ang)

    def power(kk):
        i = exps.index(kk)
        return pow_re[i:i + 1], pow_im[i:i + 1]

    lb_re, lb_im = power(1)
    nr, ni = lb_re - 1.0, lb_im
    den = lre * lre + lim * lim
    f_re = (nr * lre + ni * lim) / den
    f_im = (ni * lre - nr * lim) / den
    bre, bim = bre_ref[...].T, bim_ref[...].T
    bb_re = f_re * bre - f_im * bim
    bb_im = f_re * bim + f_im * bre
    cre, cim = cre_ref[...], cim_ref[...]
    cp_re, cp_im = [], []
    for kk in range(n + 1):
        pr, pi = power(kk)
        cp_re.append(cre * pr - cim * pi)
        cp_im.append(cre * pi + cim * pr)

    cpw = [jnp.concatenate([cp_re[kk], -cp_im[kk]], axis=1) for kk in range(n + 1)]
    bb_a = jnp.concatenate([bb_re, bb_im], axis=1)
    kw = lax.dot_general(jnp.concatenate(cpw[:n], axis=0), jnp.concatenate([bb_a] * n, axis=0), _NT,
                         preferred_element_type=F32, precision=HIGHEST)
    lane = lax.broadcasted_iota(jnp.int32, (h, n * h), 1)
    row = lax.broadcasted_iota(jnp.int32, (h, n * h), 0)
    col_blk = lax.shift_right_logical(lane, 4)
    d_tiled = jnp.concatenate([d_ref[...]] * n, axis=1)
    lag = [kw[h * kk:h * (kk + 1)] for kk in range(n)]
    lag[0] = lag[0] + jnp.where((lane & (h - 1)) == row, d_tiled, 0.0)
    rows = []
    for j in range(n):
        acc = jnp.zeros((h, n * h), F32)
        for jp in range(j + 1):
            acc = jnp.where(col_blk == jp, lag[j - jp], acc)
        rows.append(acc)
    lhs_ref[0:n * h, :] = jnp.concatenate(rows, axis=0).astype(lhs_ref.dtype)

    bb_b = jnp.concatenate([-bb_im, bb_re], axis=1)
    win = []
    for j in range(n):
        pr, pi = power(n - 1 - j)
        win.append(jnp.concatenate([pr, pr], axis=1) * bb_a + jnp.concatenate([pi, pi], axis=1) * bb_b)
    lhs_ref[n * h:, :] = jnp.concatenate(win, axis=0).T.astype(lhs_ref.dtype)
    wout_ref[...] = jnp.concatenate(cpw[1:], axis=0).astype(wout_ref.dtype)
    for i, kk in enumerate(_s5_scan_steps(n_top)):
        pr, pi = power(kk)
        ap_ref[i:i + 1, :] = jnp.concatenate([pr, pi], axis=1)


def _s5_coefficients(log_dt, lam_re, lam_im, b_re, b_im, c_re, c_im, d_skip, n_top):
    g, p, h = SSM_GROUPS, SSM_STATE, SSM_GROUP
    n_ap = SUPER + n_top
    gb = SCAN_GROUPS
    grp = lambda *shape: pl.BlockSpec((gb,) + shape, lambda i: (i,) + (0,) * len(shape))
    exps = np.asarray(_s5_exponents(n_top), np.float32)
    kk = np.zeros((-(-exps.size // 8) * 8, 1), np.float32)
    kk[:exps.size, 0] = exps
    return pl.pallas_call(
        functools.partial(_s5_coef_kernel, n_top=n_top),
        out_shape=(jax.ShapeDtypeStruct((g, CHUNK * h + 2 * p, CHUNK * h), BF16),
                   jax.ShapeDtypeStruct((g, CHUNK * h, 2 * p), BF16),
                   jax.ShapeDtypeStruct((g, n_ap, 2 * p), F32)),
        grid=(g // gb,),
        in_specs=[pl.BlockSpec(kk.shape, lambda i: (0, 0)),
                  grp(1, p), grp(1, p), grp(1, 1), grp(p, h), grp(p, h), grp(h, p), grp(h, p), grp(1, h)],
        out_specs=(grp(CHUNK * h + 2 * p, CHUNK * h), grp(CHUNK * h, 2 * p), grp(n_ap, 2 * p)),
        name="s5_coefficients",
    )(jnp.asarray(kk), lam_re.reshape(g, 1, p), lam_im.reshape(g, 1, p), log_dt.reshape(g, 1, 1), b_re, b_im,
      c_re, c_im, d_skip.reshape(g, 1, h))


PHASES = 8


def _ssm_proj_kernel(x_ref, mod_ref, ng_ref, w_ref, u_ref, z_ref, rows_scr, *, n_b):
    d = D_MODEL
    for jj in range(PHASES):
        hs = []
        for b in range(n_b):
            mod = mod_ref[b]
            shift, scale = mod[:, :d], mod[:, d:2 * d]
            rows_scr[b] = x_ref[b, :, jj, :]
            hs.append((_rms(rows_scr[b], ng_ref[...] * (1.0 + scale)) + shift).astype(BF16))
        h = jnp.concatenate(hs, axis=0)
        r = lax.dot_general(w_ref[...], h, _NT, preferred_element_type=F32)
        u_ref[jj] = r[:SSM_WIDTH].astype(BF16)
        z_ref[jj] = _silu(r[SSM_WIDTH:]).astype(BF16)


def _ssm_projection(x4, mod3, norm_g, w_in):
    n_b, c2, _, d = x4.shape
    lanes = n_b * c2
    halves = CHUNK // PHASES
    w_t = w_in[:, :2 * SSM_WIDTH].T.astype(BF16)
    out = jax.ShapeDtypeStruct((CHUNK, SSM_WIDTH, SUPER * lanes), BF16)
    out_spec = pl.BlockSpec((PHASES, SSM_WIDTH, lanes), lambda j2, hf: (hf, 0, j2))
    return pl.pallas_call(
        functools.partial(_ssm_proj_kernel, n_b=n_b),
        out_shape=(out, out),
        grid=(SUPER, halves),
        in_specs=[pl.BlockSpec((n_b, c2, PHASES, d), lambda j2, hf: (0, 0, j2 * halves + hf, 0)),
                  pl.BlockSpec((n_b, 1, 3 * d), lambda j2, hf: (0, 0, 0)),
                  pl.BlockSpec((1, d), lambda j2, hf: (0, 0)),
                  pl.BlockSpec(w_t.shape, lambda j2, hf: (0, 0))],
        out_specs=(out_spec, out_spec),
        scratch_shapes=[pltpu.VMEM((n_b, c2, d), F32)],
        compiler_params=pltpu.CompilerParams(dimension_semantics=("arbitrary", "arbitrary"),
                                             vmem_limit_bytes=56 * 1024 * 1024),
        name="ssm_projection",
    )(x4, mod3, norm_g.reshape(1, d), w_t)


def _cmul(ar, ai, xr, xi):
    return ar * xr - ai * xi, ar * xi + ai * xr


SCAN_GROUPS = 4


def _s5_group_kernel(x_ref, lhs_ref, wout_ref, ap_ref, y_ref, *, n_b, c2, n_top):
    p, gb = SSM_STATE, SCAN_GROUPS
    rows = CHUNK * SSM_GROUP
    n = SUPER * n_b * c2
    r = [jnp.dot(lhs_ref[gi], x_ref[:, SSM_GROUP * gi:SSM_GROUP * (gi + 1), :].reshape(rows, n),
                 preferred_element_type=F32) for gi in range(gb)]
    c_re = jnp.concatenate([r[gi][rows:rows + p] for gi in range(gb)], axis=0)
    c_im = jnp.concatenate([r[gi][rows + p:] for gi in range(gb)], axis=0)

    def mult(i):
        rep = [jnp.broadcast_to(ap_ref[gi, i:i + 1, :], (2 * p, 2 * p)).T for gi in range(gb)]
        return (jnp.concatenate([m[:p, :c2] for m in rep], axis=0),
                jnp.concatenate([m[p:, :c2] for m in rep], axis=0))

    lane = lax.broadcasted_iota(jnp.int32, (gb * p, c2), 1)

    def shifted(a, sh):
        return jnp.where(lane >= sh, pltpu.roll(a, sh, 1), 0.0)

    pieces = [[None] * n_b for _ in range(SUPER)]
    for b in range(n_b):
        piece = lambda a, j2: a[:, (j2 * n_b + b) * c2:(j2 * n_b + b + 1) * c2]
        a_re, a_im = mult(1)
        e_re = jnp.zeros((gb * p, c2), F32)
        e_im = jnp.zeros((gb * p, c2), F32)
        local = []
        for j2 in range(SUPER):
            local.append((e_re, e_im))
            t_re, t_im = _cmul(a_re, a_im, e_re, e_im)
            e_re, e_im = t_re + piece(c_re, j2), t_im + piece(c_im, j2)
        for i in range(n_top):
            m_re, m_im = mult(SUPER + i)
            t_re, t_im = _cmul(m_re, m_im, shifted(e_re, 1 << i), shifted(e_im, 1 << i))
            e_re, e_im = e_re + t_re, e_im + t_im
        s_re, s_im = shifted(e_re, 1), shifted(e_im, 1)
        for j2 in range(SUPER):
            if j2 == 0:
                pieces[j2][b] = (s_re, s_im)
            else:
                m_re, m_im = mult(j2)
                t_re, t_im = _cmul(m_re, m_im, s_re, s_im)
                pieces[j2][b] = (local[j2][0] + t_re, local[j2][1] + t_im)
    for gi in range(gb):
        sl = slice(gi * p, (gi + 1) * p)
        s_in = jnp.concatenate([jnp.concatenate([pieces[j2][b][0][sl], pieces[j2][b][1][sl]], axis=0)
                                for j2 in range(SUPER) for b in range(n_b)], axis=1)
        y = r[gi][:rows] + jnp.dot(wout_ref[gi], s_in.astype(BF16), preferred_element_type=F32)
        y_ref[gi] = y.reshape(CHUNK, SSM_GROUP, n).astype(y_ref.dtype)


def _s5_scan(u_t, lhs, wout, ap, n_b, c2, n_top):
    g, h = SSM_GROUPS, SSM_GROUP
    n = u_t.shape[-1]
    gb = SCAN_GROUPS
    grp = lambda *shape: pl.BlockSpec((gb,) + shape, lambda i: (i,) + (0,) * len(shape))
    return pl.pallas_call(
        functools.partial(_s5_group_kernel, n_b=n_b, c2=c2, n_top=n_top),
        out_shape=jax.ShapeDtypeStruct((g, CHUNK, h, n), BF16),
        grid=(g // gb,),
        in_specs=[pl.BlockSpec((CHUNK, gb * h, n), lambda i: (0, i, 0)),
                  grp(*lhs.shape[1:]), grp(*wout.shape[1:]), grp(*ap.shape[1:])],
        out_specs=grp(CHUNK, h, n),
        compiler_params=pltpu.CompilerParams(dimension_semantics=("arbitrary",),
                                             vmem_limit_bytes=48 * 1024 * 1024),
        name="s5_chunk_scan",
    )(u_t, lhs, wout, ap)


def _glu_kernel(y_ref, zs_ref, w_ref, b_ref, o_ref, *, n_b, c2):
    g, _, h, n = y_ref.shape
    for jj in range(PHASES):
        y = jax.nn.gelu(y_ref[:, jj].reshape(g * h, n).astype(F32))
        t = jnp.dot(w_ref[...], y.astype(BF16), preferred_element_type=F32) + b_ref[...]
        gated = (y * jax.nn.sigmoid(t) * zs_ref[jj].astype(F32)).T
        for b in range(n_b):
            o_ref[b, :, jj, :] = gated[b * c2:(b + 1) * c2]


def _glu(y_t, zs_t, w_glu, b_glu, n_b, c2):
    g, _, h, _ = y_t.shape
    lanes = n_b * c2
    halves = CHUNK // PHASES
    w_t = w_glu.T.astype(BF16)
    return pl.pallas_call(
        functools.partial(_glu_kernel, n_b=n_b, c2=c2),
        out_shape=jax.ShapeDtypeStruct((n_b, c2, SUPER * CHUNK, SSM_WIDTH), F32),
        grid=(SUPER, halves),
        in_specs=[pl.BlockSpec((g, PHASES, h, lanes), lambda j2, hf: (0, hf, 0, j2)),
                  pl.BlockSpec((PHASES, SSM_WIDTH, lanes), lambda j2, hf: (hf, 0, j2)),
                  pl.BlockSpec(w_t.shape, lambda j2, hf: (0, 0)),
                  pl.BlockSpec((SSM_WIDTH, 1), lambda j2, hf: (0, 0))],
        out_specs=pl.BlockSpec((n_b, c2, PHASES, SSM_WIDTH), lambda j2, hf: (0, 0, j2 * halves + hf, 0)),
        compiler_params=pltpu.CompilerParams(dimension_semantics=("arbitrary", "arbitrary"),
                                             vmem_limit_bytes=48 * 1024 * 1024),
        name="s5_glu",
    )(y_t, zs_t, w_t, b_glu.reshape(SSM_WIDTH, 1))


def _out_proj_kernel(ys_ref, ym_ref, x_ref, mod_ref, wa_ref, wb_ref, o_ref):
    y = (jnp.dot(ys_ref[...].astype(BF16), wa_ref[...], preferred_element_type=F32)
         + jnp.dot(ym_ref[...], wb_ref[...], preferred_element_type=F32))
    gate = mod_ref[...][:, 2 * D_MODEL:]
    o_ref[...] = x_ref[...] + gate * y


def _out_projection(ys, ym, x2, mod3, w_out, seq):
    t, d = x2.shape
    tm = min(512, seq)
    per_b = seq // tm
    wa = w_out[:SSM_WIDTH].astype(BF16)
    wb = w_out[SSM_WIDTH:].astype(BF16)
    tok = lambda w: pl.BlockSpec((tm, w), lambda i: (i, 0))
    return pl.pallas_call(
        _out_proj_kernel,
        out_shape=jax.ShapeDtypeStruct(x2.shape, F32),
        grid=(t // tm,),
        in_specs=[tok(SSM_WIDTH), tok(MLA_WIDTH), tok(d),
                  pl.BlockSpec((None, 1, 3 * d), lambda i: (i // per_b, 0, 0)),
                  pl.BlockSpec(wa.shape, lambda i: (0, 0)),
                  pl.BlockSpec(wb.shape, lambda i: (0, 0))],
        out_specs=tok(d),
        compiler_params=pltpu.CompilerParams(dimension_semantics=("arbitrary",)),
        name="output_projection",
    )(ys, ym, x2, mod3, wa, wb)


def kernel(x, c, positions, w_ada, b_ada, norm_g, w_in, log_dt, lam_re, lam_im, b_re, b_im, c_re, c_im, d_skip,
           w_glu, b_glu, q_a_g, w_q_b, kv_a_g, w_kv_b, q_norm_g, k_norm_g, w_out):
    n_b, seq, d = x.shape
    depth = w_ada.shape[0]
    c2 = seq // (CHUNK * SUPER)
    n_top = max(int(math.log2(c2)), 0)
    assert c2 * CHUNK * SUPER == seq and (1 << n_top) == c2
    rope = _rope_tables(positions)
    for l in range(depth):
        mod3 = _modulation(c, w_ada[l], b_ada[l]).reshape(n_b, 1, 3 * d)
        x2 = x.reshape(n_b * seq, d)
        lhs, wout, ap = _s5_coefficients(log_dt[l], lam_re[l], lam_im[l], b_re[l], b_im[l], c_re[l], c_im[l],
                                         d_skip[l], n_top)
        u_t, zs_t = _ssm_projection(x.reshape(n_b, c2, SUPER * CHUNK, d), mod3, norm_g[l], w_in[l])
        y_t = _s5_scan(u_t, lhs, wout, ap, n_b, c2, n_top)
        ys = _glu(y_t, zs_t, w_glu[l], b_glu[l], n_b, c2).reshape(n_b * seq, SSM_WIDTH)
        qt, k, vt, zm = _mla_projection(x2, mod3, norm_g[l], w_in[l], q_a_g[l], w_q_b[l],
                                        kv_a_g[l], w_kv_b[l], q_norm_g[l], k_norm_g[l], rope, n_b, seq)
        score_bound = (1.05 * math.sqrt(QK_HEAD) * math.log2(math.e)
                       * jnp.max(jnp.abs(q_norm_g[l])) * jnp.max(jnp.abs(k_norm_g[l])))
        ym = _attention(qt, k, vt, zm, score_bound)
        x = _out_projection(ys, ym, x2, mod3, w_out[l], seq).reshape(n_b, seq, d)
    return x
```

```python
import functools
import math

import numpy as np
import jax
import jax.numpy as jnp
from jax import lax
from jax.experimental import pallas as pl
from jax.experimental.pallas import tpu as pltpu

F32 = jnp.float32
BF16 = jnp.bfloat16
HIGHEST = lax.Precision.HIGHEST

D_MODEL = 1024
SSM_WIDTH = 512
SSM_GROUP = 16
SSM_GROUPS = 32
SSM_STATE = 64
MLA_HEADS = 8
QK_NOPE = 64
QK_ROPE = 32
QK_HEAD = QK_NOPE + QK_ROPE
V_HEAD = 64
MLA_WIDTH = MLA_HEADS * V_HEAD
Q_LORA = 384
KV_LORA = 256
ROPE_THETA = 10000.0
EPS = 1e-6
NEG_BIG = -1e30

LANES = 128
CHUNK = 16
SUPER = 4
ROPE_HALF = QK_ROPE // 2

_NT = (((1,), (1,)), ((), ()))
_TN = (((0,), (0,)), ((), ()))


def _silu(v):
    return v * jax.nn.sigmoid(v)


def _rms(v, gain):
    return v * lax.rsqrt(jnp.mean(v * v, axis=-1, keepdims=True) + EPS) * gain


def _mod_kernel(c_ref, w_ref, b_ref, o_ref):
    act = _silu(c_ref[...])
    o_ref[...] = jnp.dot(act, w_ref[...], preferred_element_type=F32, precision=HIGHEST) + b_ref[...]


def _modulation(c, w, b):
    n_b, d = c.shape
    rows = 8
    c_pad = jnp.zeros((rows, d), F32).at[:n_b].set(c)
    tn = 512
    out = pl.pallas_call(
        _mod_kernel,
        out_shape=jax.ShapeDtypeStruct((rows, 3 * d), F32),
        grid=(3 * d // tn,),
        in_specs=[pl.BlockSpec((rows, d), lambda n: (0, 0)),
                  pl.BlockSpec((d, tn), lambda n: (0, n)),
                  pl.BlockSpec((1, tn), lambda n: (0, n))],
        out_specs=pl.BlockSpec((rows, tn), lambda n: (0, n)),
        name="adaln_modulation",
    )(c_pad, w, b.reshape(1, -1))
    return out[:n_b]


def _rope_kernel(pos_ref, freq_ref, cos_ref, sin_ref):
    ang = freq_ref[...] * pos_ref[...].astype(F32)
    cos_ref[...] = jnp.cos(ang)
    sin_ref[...] = jnp.sin(ang)


def _rope_tables(positions):
    t = positions.size
    tt = min(t, 2048)
    inv_freq = (ROPE_THETA ** (-np.arange(ROPE_HALF, dtype=np.float64) * 2.0 / QK_ROPE)).astype(np.float32)
    return pl.pallas_call(
        _rope_kernel,
        out_shape=(jax.ShapeDtypeStruct((ROPE_HALF, t), F32),) * 2,
        grid=(t // tt,),
        in_specs=[pl.BlockSpec((1, tt), lambda i: (0, i)),
                  pl.BlockSpec((ROPE_HALF, 1), lambda i: (0, 0))],
        out_specs=(pl.BlockSpec((ROPE_HALF, tt), lambda i: (0, i)),) * 2,
        name="rope_tables",
    )(positions.reshape(1, t), jnp.asarray(inv_freq).reshape(ROPE_HALF, 1))


def _slab_source():
    src = np.full((LANES,), -1, np.int64)
    src[0:16] = QK_NOPE + np.arange(16)
    src[16:64] = np.arange(48)
    src[64:80] = QK_NOPE + 16 + np.arange(16)
    src[80:96] = 48 + np.arange(16)
    return src


def _to_slabs(w, per_head, src):
    k = w.shape[0]
    w3 = w.reshape(k, MLA_HEADS, per_head)
    valid = src >= 0
    cols = jnp.where(valid[None, None, :], w3[:, :, np.clip(src, 0, per_head - 1)], 0.0)
    return cols.reshape(k, MLA_HEADS * LANES)


V_ROWS = V_HEAD + 16


def _mla_proj_kernel(x_ref, mod_ref, ng_ref, w1_ref, qag_ref, kvag_ref, wqt_ref, wkt_ref, wvt_ref,
                     gq_ref, gk_ref, cost_ref, sint_ref, qt_ref, k_ref, vt_ref, zs_ref):
    d = D_MODEL
    tm = x_ref.shape[0]
    x = x_ref[...]
    mod = mod_ref[...]
    shift, scale = mod[:, :d], mod[:, d:2 * d]
    h = (_rms(x, ng_ref[...] * (1.0 + scale)) + shift).astype(BF16)
    p1 = jnp.dot(h, w1_ref[...], preferred_element_type=F32)
    o1, o2, o3 = Q_LORA, Q_LORA + KV_LORA, Q_LORA + KV_LORA + MLA_WIDTH
    cqn = _rms(p1[:, :o1], qag_ref[...]).astype(BF16)
    ckvn = _rms(p1[:, o1:o2], kvag_ref[...]).astype(BF16)
    zs_ref[...] = _silu(p1[:, o2:o3]).astype(BF16)

    qt = lax.dot_general(wqt_ref[...], cqn, _NT, preferred_element_type=F32)
    kt = lax.dot_general(wkt_ref[...], ckvn, _NT, preferred_element_type=F32)
    krt = p1[:, o3:].T
    vt = lax.dot_general(wvt_ref[...], ckvn, _NT, preferred_element_type=F32)
    cos_t, sin_t = cost_ref[...], sint_ref[...]
    gq = jnp.concatenate([gq_ref[...]] * (tm // LANES), axis=1)
    gk = jnp.concatenate([gk_ref[...]] * (tm // LANES), axis=1)
    first = lax.broadcasted_iota(jnp.int32, (V_ROWS - V_HEAD, tm), 0) == 0
    ones_rows = jnp.where(first, 1.0, 0.0).astype(BF16)
    r, half = ROPE_HALF, LANES // 2

    def norm_rope(slab, gain):
        ss = jnp.sum(slab * slab, axis=0, keepdims=True)
        n = slab * lax.rsqrt(ss * (1.0 / QK_HEAD) + EPS) * gain
        x1, x2 = n[0:r], n[half:half + r]
        return jnp.concatenate([x1 * cos_t - x2 * sin_t, n[r:half], x2 * cos_t + x1 * sin_t, n[half + r:]], axis=0)

    for head in range(MLA_HEADS):
        rows = slice(LANES * head, LANES * (head + 1))
        qt_ref[head] = norm_rope(qt[rows], gq).astype(BF16)
        k_ref[head] = norm_rope(kt[rows] + krt, gk).T.astype(BF16)
        vt_ref[head, 0:V_HEAD, :] = vt[V_HEAD * head:V_HEAD * (head + 1)].astype(BF16)
        vt_ref[head, V_HEAD:, :] = ones_rows


def _mla_projection(x2, mod3, norm_g, w_in, q_a_g, w_q_b, kv_a_g, w_kv_b, q_norm_g, k_norm_g, rope, n_b, seq):
    t, d = x2.shape
    cos_t, sin_t = rope
    tm = min(512, seq)
    per_b = seq // tm
    o2 = 2 * SSM_WIDTH
    o4, o5 = o2 + Q_LORA + KV_LORA, o2 + Q_LORA + KV_LORA + QK_ROPE
    src = _slab_source()
    valid = src >= 0
    rope_src = np.where(src >= QK_NOPE, src - QK_NOPE, -1)
    nope_src = np.where(valid & (src < QK_NOPE), src, -1)
    wkr = jnp.where((rope_src >= 0)[None, :], w_in[:, o4:o5][:, np.clip(rope_src, 0, QK_ROPE - 1)], 0.0)
    w1 = jnp.concatenate([w_in[:, o2:o4], w_in[:, o5:], wkr], axis=1).astype(BF16)
    wqt = _to_slabs(w_q_b, QK_HEAD, src).T.astype(BF16)
    wkt = _to_slabs(w_kv_b, QK_NOPE + V_HEAD, nope_src).T.astype(BF16)
    wvt = w_kv_b.reshape(KV_LORA, MLA_HEADS, QK_NOPE + V_HEAD)[:, :, QK_NOPE:].reshape(KV_LORA, MLA_WIDTH).T.astype(BF16)
    q_scale = math.log2(math.e) / math.sqrt(QK_HEAD)
    slab_gain = lambda g: jnp.broadcast_to(jnp.where(valid, g[np.clip(src, 0, QK_HEAD - 1)], 0.0)[:, None],
                                           (LANES, LANES))
    gq = slab_gain(q_norm_g * q_scale)
    gk = slab_gain(k_norm_g)
    const = lambda shape: pl.BlockSpec(shape, lambda i: (0,) * len(shape))
    tok = lambda i: (i // per_b, 0, 0, i % per_b)
    return pl.pallas_call(
        _mla_proj_kernel,
        out_shape=(jax.ShapeDtypeStruct((n_b, MLA_HEADS, LANES, seq), BF16),
                   jax.ShapeDtypeStruct((n_b, MLA_HEADS, seq, LANES), BF16),
                   jax.ShapeDtypeStruct((n_b, MLA_HEADS, V_ROWS, seq), BF16),
                   jax.ShapeDtypeStruct((t, MLA_WIDTH), BF16)),
        grid=(t // tm,),
        in_specs=[pl.BlockSpec((tm, d), lambda i: (i, 0)),
                  pl.BlockSpec((None, 1, 3 * d), lambda i: (i // per_b, 0, 0)),
                  const((1, d)), const(w1.shape), const((1, Q_LORA)), const((1, KV_LORA)),
                  const(wqt.shape), const(wkt.shape), const(wvt.shape),
                  const((LANES, LANES)), const((LANES, LANES)),
                  pl.BlockSpec((ROPE_HALF, tm), lambda i: (0, i)),
                  pl.BlockSpec((ROPE_HALF, tm), lambda i: (0, i))],
        out_specs=(pl.BlockSpec((None, MLA_HEADS, LANES, tm), tok),
                   pl.BlockSpec((None, MLA_HEADS, tm, LANES), lambda i: (i // per_b, 0, i % per_b, 0)),
                   pl.BlockSpec((None, MLA_HEADS, V_ROWS, tm), tok),
                   pl.BlockSpec((tm, MLA_WIDTH), lambda i: (i, 0))),
        compiler_params=pltpu.CompilerParams(dimension_semantics=("arbitrary",),
                                             vmem_limit_bytes=48 * 1024 * 1024),
        name="mla_projection",
    )(x2, mod3, norm_g.reshape(1, d), w1, q_a_g.reshape(1, -1), kv_a_g.reshape(1, -1), wqt, wkt, wvt, gq, gk,
      cos_t, sin_t)


ATTN_TQ = 2048
ATTN_TK = 256
HEADS_PER_STEP = 2


def _attn_kernel(qt_ref, k_ref, vt_ref, zs_ref, o_ref, s_scr):
    tq, tk = ATTN_TQ, ATTN_TK
    r = tq // tk
    assert r * tk == tq and r % 2 == 0
    qi = pl.program_id(2)

    def scores(slot, t, lo=0):
        start = pl.multiple_of(t * tk, tk)
        for e in range(HEADS_PER_STEP):
            s_scr[slot, e, :, lo:] = jnp.dot(k_ref[e, pl.ds(start, tk), :], qt_ref[e, :, lo:],
                                             preferred_element_type=F32)

    def process(slot, t, carry, lo=0, diagonal=False):
        start = pl.multiple_of(t * tk, tk)
        out = []
        for e in range(HEADS_PER_STEP):
            m, acc = carry[e]
            s = s_scr[slot, e, :, lo:]
            if diagonal:
                blk = s[:, :tk]
                ok = lax.broadcasted_iota(jnp.int32, blk.shape, 0) <= lax.broadcasted_iota(jnp.int32, blk.shape, 1)
                blk = jnp.where(ok, blk, NEG_BIG)
                s = jnp.concatenate([blk, s[:, tk:]], axis=1) if s.shape[1] > tk else blk
            m_old = m[:, lo:]
            m_new = jnp.maximum(m_old, jnp.max(s, axis=0, keepdims=True))
            p = jnp.exp2(s - m_new).astype(BF16)
            acc_new = (jnp.exp2(m_old - m_new) * acc[:, lo:]
                       + jnp.dot(vt_ref[e, :, pl.ds(start, tk)], p, preferred_element_type=F32))
            if lo:
                m_new = jnp.concatenate([m[:, :lo], m_new], axis=1)
                acc_new = jnp.concatenate([acc[:, :lo], acc_new], axis=1)
            out.append((m_new, acc_new))
        return tuple(out)

    def body(i, carry):
        base = i * r
        for u in range(r):
            scores((u + 1) % 2, base + u + 1)
            carry = process(u % 2, base + u, carry)
        return carry

    init = tuple((jnp.full((1, tq), NEG_BIG, F32), jnp.zeros((V_ROWS, tq), F32)) for _ in range(HEADS_PER_STEP))
    scores(0, 0)
    carry = lax.fori_loop(0, qi, body, init)
    base = qi * r
    for u in range(r):
        if u + 1 < r:
            scores((u + 1) % 2, base + u + 1, lo=(u + 1) * tk)
        carry = process(u % 2, base + u, carry, lo=u * tk, diagonal=True)
    _attn_finish(tuple(c[1] for c in carry), zs_ref, o_ref)


def _attn_finish(accs, zs_ref, o_ref):
    outs = [(acc[:V_HEAD] / acc[V_HEAD:V_HEAD + 1]).T for acc in accs]
    o_ref[...] = (jnp.concatenate(outs, axis=1) * zs_ref[...].astype(F32)).astype(o_ref.dtype)


def _attn_bounded_kernel(qt_ref, k_ref, vt_ref, zs_ref, o_ref, p_scr):
    tq, tk = ATTN_TQ, ATTN_TK
    r = tq // tk
    assert r * tk == tq and r % 2 == 0
    qi = pl.program_id(2)
    q0 = qi * tq

    def probs(slot, t, lo=0, may_cross=False):
        start = pl.multiple_of(t * tk, tk)
        for e in range(HEADS_PER_STEP):
            p = jnp.exp2(jnp.dot(k_ref[e, pl.ds(start, tk), :], qt_ref[e, :, lo:], preferred_element_type=F32))
            if may_cross:
                blk = p[:, :tk]
                key = start + lax.broadcasted_iota(jnp.int32, blk.shape, 0)
                qry = q0 + lo + lax.broadcasted_iota(jnp.int32, blk.shape, 1)
                blk = jnp.where(key <= qry, blk, 0.0)
                p = jnp.concatenate([blk, p[:, tk:]], axis=1) if p.shape[1] > tk else blk
            p_scr[slot, e, :, lo:] = p.astype(BF16)

    def accumulate(slot, t, accs, lo=0):
        start = pl.multiple_of(t * tk, tk)
        out = []
        for e in range(HEADS_PER_STEP):
            pv = jnp.dot(vt_ref[e, :, pl.ds(start, tk)], p_scr[slot, e, :, lo:], preferred_element_type=F32)
            acc = accs[e]
            out.append(acc + pv if lo == 0 else jnp.concatenate([acc[:, :lo], acc[:, lo:] + pv], axis=1))
        return tuple(out)

    def body(i, accs):
        base = i * r
        for u in range(r):
            probs((u + 1) % 2, base + u + 1, may_cross=(u == r - 1))
            accs = accumulate(u % 2, base + u, accs)
        return accs

    probs(0, 0, may_cross=True)
    accs = lax.fori_loop(0, qi, body, tuple(jnp.zeros((V_ROWS, tq), F32) for _ in range(HEADS_PER_STEP)))
    base = qi * r
    for u in range(r):
        if u + 1 < r:
            probs((u + 1) % 2, base + u + 1, lo=(u + 1) * tk, may_cross=True)
        accs = accumulate(u % 2, base + u, accs, lo=u * tk)
    _attn_finish(accs, zs_ref, o_ref)


SCORE_BOUND_LOG2 = 60.0


def _attention(qt, k, vt, zs, score_bound):
    n_b, n_h, seq, _ = k.shape
    tq = ATTN_TQ
    nq = seq // tq
    hp = HEADS_PER_STEP
    width = hp * V_HEAD

    def body(bounded_ref, qt_ref, k_ref, vt_ref, zs_ref, o_ref, p_scr, s_scr):
        @pl.when(bounded_ref[0] != 0)
        def _():
            _attn_bounded_kernel(qt_ref, k_ref, vt_ref, zs_ref, o_ref, p_scr)

        @pl.when(bounded_ref[0] == 0)
        def _():
            _attn_kernel(qt_ref, k_ref, vt_ref, zs_ref, o_ref, s_scr)

    tiles = (2, hp, ATTN_TK, tq)
    bounded = (score_bound <= SCORE_BOUND_LOG2).astype(jnp.int32).reshape(1)
    return pl.pallas_call(
        body,
        out_shape=jax.ShapeDtypeStruct(zs.shape, BF16),
        grid_spec=pltpu.PrefetchScalarGridSpec(
            num_scalar_prefetch=1,
            grid=(n_b, n_h // hp, nq),
            in_specs=[pl.BlockSpec((None, hp, LANES, tq), lambda b, h, i, _: (b, h, 0, i)),
                      pl.BlockSpec((None, hp, seq, LANES), lambda b, h, i, _: (b, h, 0, 0)),
                      pl.BlockSpec((None, hp, V_ROWS, seq), lambda b, h, i, _: (b, h, 0, 0)),
                      pl.BlockSpec((tq, width), lambda b, h, i, _: (b * nq + i, h))],
            out_specs=pl.BlockSpec((tq, width), lambda b, h, i, _: (b * nq + i, h)),
            scratch_shapes=[pltpu.VMEM(tiles, BF16), pltpu.VMEM(tiles, F32)]),
        compiler_params=pltpu.CompilerParams(dimension_semantics=("arbitrary", "arbitrary", "arbitrary"),
                                             vmem_limit_bytes=48 * 1024 * 1024),
        name="causal_attention",
    )(bounded, qt, k, vt, zs)


def _s5_scan_steps(n_top):
    return [CHUNK * m for m in range(SUPER)] + [CHUNK * SUPER * (1 << i) for i in range(n_top)]


def _s5_exponents(n_top):
    return sorted(set(range(CHUNK + 1)) | set(_s5_scan_steps(n_top)))


def _s5_coef_kernel(kk_ref, *refs, n_top):
    for gi in range(SCAN_GROUPS):
        _s5_coef_one_group(kk_ref, *(r.at[gi] for r in refs), n_top=n_top)


def _s5_coef_one_group(kk_ref, lre_ref, lim_ref, ldt_ref, bre_ref, bim_ref, cre_ref, cim_ref, d_ref,
                       lhs_ref, wout_ref, ap_ref, *, n_top):
    h, p, n = SSM_GROUP, SSM_STATE, CHUNK
    lre, lim = lre_ref[...], lim_ref[...]
    dt = jnp.exp(ldt_ref[...])
    exps = _s5_exponents(n_top)
    kcol = kk_ref[...]
    mag = jnp.exp(kcol * (lre * dt))
    ang = kcol * (lim * dt)
    pow_re, pow_im = mag * jnp.cos(ang), mag * jnp.sin(ang)

    def power(kk):
        i = exps.index(kk)
        return pow_re[i:i + 1], pow_im[i:i + 1]

    lb_re, lb_im = power(1)
    nr, ni = lb_re - 1.0, lb_im
    den = lre * lre + lim * lim
    f_re = (nr * lre + ni * lim) / den
    f_im = (ni * lre - nr * lim) / den
    bre, bim = bre_ref[...].T, bim_ref[...].T
    bb_re = f_re * bre - f_im * bim
    bb_im = f_re * bim + f_im * bre
    cre, cim = cre_ref[...], cim_ref[...]
    cp_re, cp_im = [], []
    for kk in range(n + 1):
        pr, pi = power(kk)
        cp_re.append(cre * pr - cim * pi)
        cp_im.append(cre * pi + cim * pr)

    cpw = [jnp.concatenate([cp_re[kk], -cp_im[kk]], axis=1) for kk in range(n + 1)]
    bb_a = jnp.concatenate([bb_re, bb_im], axis=1)
    kw = lax.dot_general(jnp.concatenate(cpw[:n], axis=0), jnp.concatenate([bb_a] * n, axis=0), _NT,
                         preferred_element_type=F32, precision=HIGHEST)
    lane = lax.broadcasted_iota(jnp.int32, (h, n * h), 1)
    row = lax.broadcasted_iota(jnp.int32, (h, n * h), 0)
    col_blk = lax.shift_right_logical(lane, 4)
    d_tiled = jnp.concatenate([d_ref[...]] * n, axis=1)
    lag = [kw[h * kk:h * (kk + 1)] for kk in range(n)]
    lag[0] = lag[0] + jnp.where((lane & (h - 1)) == row, d_tiled, 0.0)
    rows = []
    for j in range(n):
        acc = jnp.zeros((h, n * h), F32)
        for jp in range(j + 1):
            acc = jnp.where(col_blk == jp, lag[j - jp], acc)
        rows.append(acc)
    lhs_ref[0:n * h, :] = jnp.concatenate(rows, axis=0).astype(lhs_ref.dtype)

    bb_b = jnp.concatenate([-bb_im, bb_re], axis=1)
    win = []
    for j in range(n):
        pr, pi = power(n - 1 - j)
        win.append(jnp.concatenate([pr, pr], axis=1) * bb_a + jnp.concatenate([pi, pi], axis=1) * bb_b)
    lhs_ref[n * h:, :] = jnp.concatenate(win, axis=0).T.astype(lhs_ref.dtype)
    wout_ref[...] = jnp.concatenate(cpw[1:], axis=0).astype(wout_ref.dtype)
    for i, kk in enumerate(_s5_scan_steps(n_top)):
        pr, pi = power(kk)
        ap_ref[i:i + 1, :] = jnp.concatenate([pr, pi], axis=1)


def _s5_coefficients(log_dt, lam_re, lam_im, b_re, b_im, c_re, c_im, d_skip, n_top):
    g, p, h = SSM_GROUPS, SSM_STATE, SSM_GROUP
    n_ap = SUPER + n_top
    gb = SCAN_GROUPS
    grp = lambda *shape: pl.BlockSpec((gb,) + shape, lambda i: (i,) + (0,) * len(shape))
    exps = np.asarray(_s5_exponents(n_top), np.float32)
    kk = np.zeros((-(-exps.size // 8) * 8, 1), np.float32)
    kk[:exps.size, 0] = exps
    return pl.pallas_call(
        functools.partial(_s5_coef_kernel, n_top=n_top),
        out_shape=(jax.ShapeDtypeStruct((g, CHUNK * h + 2 * p, CHUNK * h), BF16),
                   jax.ShapeDtypeStruct((g, CHUNK * h, 2 * p), BF16),
                   jax.ShapeDtypeStruct((g, n_ap, 2 * p), F32)),
        grid=(g // gb,),
        in_specs=[pl.BlockSpec(kk.shape, lambda i: (0, 0)),
                  grp(1, p), grp(1, p), grp(1, 1), grp(p, h), grp(p, h), grp(h, p), grp(h, p), grp(1, h)],
        out_specs=(grp(CHUNK * h + 2 * p, CHUNK * h), grp(CHUNK * h, 2 * p), grp(n_ap, 2 * p)),
        name="s5_coefficients",
    )(jnp.asarray(kk), lam_re.reshape(g, 1, p), lam_im.reshape(g, 1, p), log_dt.reshape(g, 1, 1), b_re, b_im,
      c_re, c_im, d_skip.reshape(g, 1, h))


PHASES = 8


def _ssm_proj_kernel(x_ref, mod_ref, ng_ref, w_ref, u_ref, z_ref, rows_scr, *, n_b):
    d = D_MODEL
    for jj in range(PHASES):
        hs = []
        for b in range(n_b):
            mod = mod_ref[b]
            shift, scale = mod[:, :d], mod[:, d:2 * d]
            rows_scr[b] = x_ref[b, :, jj, :]
            hs.append((_rms(rows_scr[b], ng_ref[...] * (1.0 + scale)) + shift).astype(BF16))
        h = jnp.concatenate(hs, axis=0)
        r = lax.dot_general(w_ref[...], h, _NT, preferred_element_type=F32)
        u_ref[jj] = r[:SSM_WIDTH].astype(BF16)
        z_ref[jj] = _silu(r[SSM_WIDTH:]).astype(BF16)


def _ssm_projection(x4, mod3, norm_g, w_in):
    n_b, c2, _, d = x4.shape
    lanes = n_b * c2
    halves = CHUNK // PHASES
    w_t = w_in[:, :2 * SSM_WIDTH].T.astype(BF16)
    out = jax.ShapeDtypeStruct((CHUNK, SSM_WIDTH, SUPER * lanes), BF16)
    out_spec = pl.BlockSpec((PHASES, SSM_WIDTH, lanes), lambda j2, hf: (hf, 0, j2))
    return pl.pallas_call(
        functools.partial(_ssm_proj_kernel, n_b=n_b),
        out_shape=(out, out),
        grid=(SUPER, halves),
        in_specs=[pl.BlockSpec((n_b, c2, PHASES, d), lambda j2, hf: (0, 0, j2 * halves + hf, 0)),
                  pl.BlockSpec((n_b, 1, 3 * d), lambda j2, hf: (0, 0, 0)),
                  pl.BlockSpec((1, d), lambda j2, hf: (0, 0)),
                  pl.BlockSpec(w_t.shape, lambda j2, hf: (0, 0))],
        out_specs=(out_spec, out_spec),
        scratch_shapes=[pltpu.VMEM((n_b, c2, d), F32)],
        compiler_params=pltpu.CompilerParams(dimension_semantics=("arbitrary", "arbitrary"),
                                             vmem_limit_bytes=56 * 1024 * 1024),
        name="ssm_projection",
    )(x4, mod3, norm_g.reshape(1, d), w_t)


def _cmul(ar, ai, xr, xi):
    return ar * xr - ai * xi, ar * xi + ai * xr


SCAN_GROUPS = 4


def _s5_group_kernel(x_ref, lhs_ref, wout_ref, ap_ref, y_ref, *, n_b, c2, n_top):
    p, gb = SSM_STATE, SCAN_GROUPS
    rows = CHUNK * SSM_GROUP
    n = SUPER * n_b * c2
    r = [jnp.dot(lhs_ref[gi], x_ref[:, SSM_GROUP * gi:SSM_GROUP * (gi + 1), :].reshape(rows, n),
                 preferred_element_type=F32) for gi in range(gb)]
    c_re = jnp.concatenate([r[gi][rows:rows + p] for gi in range(gb)], axis=0)
    c_im = jnp.concatenate([r[gi][rows + p:] for gi in range(gb)], axis=0)

    def mult(i):
        rep = [jnp.broadcast_to(ap_ref[gi, i:i + 1, :], (2 * p, 2 * p)).T for gi in range(gb)]
        return (jnp.concatenate([m[:p, :c2] for m in rep], axis=0),
                jnp.concatenate([m[p:, :c2] for m in rep], axis=0))

    lane = lax.broadcasted_iota(jnp.int32, (gb * p, c2), 1)

    def shifted(a, sh):
        return jnp.where(lane >= sh, pltpu.roll(a, sh, 1), 0.0)

    pieces = [[None] * n_b for _ in range(SUPER)]
    for b in range(n_b):
        piece = lambda a, j2: a[:, (j2 * n_b + b) * c2:(j2 * n_b + b + 1) * c2]
        a_re, a_im = mult(1)
        e_re = jnp.zeros((gb * p, c2), F32)
        e_im = jnp.zeros((gb * p, c2), F32)
        local = []
        for j2 in range(SUPER):
            local.append((e_re, e_im))
            t_re, t_im = _cmul(a_re, a_im, e_re, e_im)
            e_re, e_im = t_re + piece(c_re, j2), t_im + piece(c_im, j2)
        for i in range(n_top):
            m_re, m_im = mult(SUPER + i)
            t_re, t_im = _cmul(m_re, m_im, shifted(e_re, 1 << i), shifted(e_im, 1 << i))
            e_re, e_im = e_re + t_re, e_im + t_im
        s_re, s_im = shifted(e_re, 1), shifted(e_im, 1)
        for j2 in range(SUPER):
            if j2 == 0:
                pieces[j2][b] = (s_re, s_im)
            else:
                m_re, m_im = mult(j2)
                t_re, t_im = _cmul(m_re, m_im, s_re, s_im)
                pieces[j2][b] = (local[j2][0] + t_re, local[j2][1] + t_im)
    for gi in range(gb):
        sl = slice(gi * p, (gi + 1) * p)
        s_in = jnp.concatenate([jnp.concatenate([pieces[j2][b][0][sl], pieces[j2][b][1][sl]], axis=0)
                                for j2 in range(SUPER) for b in range(n_b)], axis=1)
        y = r[gi][:rows] + jnp.dot(wout_ref[gi], s_in.astype(BF16), preferred_element_type=F32)
        y_ref[gi] = y.reshape(CHUNK, SSM_GROUP, n).astype(y_ref.dtype)


def _s5_scan(u_t, lhs, wout, ap, n_b, c2, n_top):
    g, h = SSM_GROUPS, SSM_GROUP
    n = u_t.shape[-1]
    gb = SCAN_GROUPS
    grp = lambda *shape: pl.BlockSpec((gb,) + shape, lambda i: (i,) + (0,) * len(shape))
    return pl.pallas_call(
        functools.partial(_s5_group_kernel, n_b=n_b, c2=c2, n_top=n_top),
        out_shape=jax.ShapeDtypeStruct((g, CHUNK, h, n), BF16),
        grid=(g // gb,),
        in_specs=[pl.BlockSpec((CHUNK, gb * h, n), lambda i: (0, i, 0)),
                  grp(*lhs.shape[1:]), grp(*wout.shape[1:]), grp(*ap.shape[1:])],
        out_specs=grp(CHUNK, h, n),
        compiler_params=pltpu.CompilerParams(dimension_semantics=("arbitrary",),
                                             vmem_limit_bytes=48 * 1024 * 1024),
        name="s5_chunk_scan",
    )(u_t, lhs, wout, ap)


def _glu_kernel(y_ref, zs_ref, w_ref, b_ref, o_ref, *, n_b, c2):
    g, _, h, n = y_ref.shape
    for jj in range(PHASES):
        y = jax.nn.gelu(y_ref[:, jj].reshape(g * h, n).astype(F32))
        t = jnp.dot(w_ref[...], y.astype(BF16), preferred_element_type=F32) + b_ref[...]
        gated = (y * jax.nn.sigmoid(t) * zs_ref[jj].astype(F32)).T
        for b in range(n_b):
            o_ref[b, :, jj, :] = gated[b * c2:(b + 1) * c2]


def _glu(y_t, zs_t, w_glu, b_glu, n_b, c2):
    g, _, h, _ = y_t.shape
    lanes = n_b * c2
    halves = CHUNK // PHASES
    w_t = w_glu.T.astype(BF16)
    return pl.pallas_call(
        functools.partial(_glu_kernel, n_b=n_b, c2=c2),
        out_shape=jax.ShapeDtypeStruct((n_b, c2, SUPER * CHUNK, SSM_WIDTH), F32),
        grid=(SUPER, halves),
        in_specs=[pl.BlockSpec((g, PHASES, h, lanes), lambda j2, hf: (0, hf, 0, j2)),
                  pl.BlockSpec((PHASES, SSM_WIDTH, lanes), lambda j2, hf: (hf, 0, j2)),
                  pl.BlockSpec(w_t.shape, lambda j2, hf: (0, 0)),
                  pl.BlockSpec((SSM_WIDTH, 1), lambda j2, hf: (0, 0))],
        out_specs=pl.BlockSpec((n_b, c2, PHASES, SSM_WIDTH), lambda j2, hf: (0, 0, j2 * halves + hf, 0)),
        compiler_params=pltpu.CompilerParams(dimension_semantics=("arbitrary", "arbitrary"),
                                             vmem_limit_bytes=48 * 1024 * 1024),
        name="s5_glu",
    )(y_t, zs_t, w_t, b_glu.reshape(SSM_WIDTH, 1))


def _out_proj_kernel(ys_ref, ym_ref, x_ref, mod_ref, wa_ref, wb_ref, o_ref):
    y = (jnp.dot(ys_ref[...].astype(BF16), wa_ref[...], preferred_element_type=F32)
         + jnp.dot(ym_ref[...], wb_ref[...], preferred_element_type=F32))
    gate = mod_ref[...][:, 2 * D_MODEL:]
    o_ref[...] = x_ref[...] + gate * y


def _out_projection(ys, ym, x2, mod3, w_out, seq):
    t, d = x2.shape
    tm = min(1024, seq)
    per_b = seq // tm
    wa = w_out[:SSM_WIDTH].astype(BF16)
    wb = w_out[SSM_WIDTH:].astype(BF16)
    tok = lambda w: pl.BlockSpec((tm, w), lambda i: (i, 0))
    return pl.pallas_call(
        _out_proj_kernel,
        out_shape=jax.ShapeDtypeStruct(x2.shape, F32),
        grid=(t // tm,),
        in_specs=[tok(SSM_WIDTH), tok(MLA_WIDTH), tok(d),
                  pl.BlockSpec((None, 1, 3 * d), lambda i: (i // per_b, 0, 0)),
                  pl.BlockSpec(wa.shape, lambda i: (0, 0)),
                  pl.BlockSpec(wb.shape, lambda i: (0, 0))],
        out_specs=tok(d),
        compiler_params=pltpu.CompilerParams(dimension_semantics=("arbitrary",),
                                             vmem_limit_bytes=48 * 1024 * 1024),
        name="output_projection",
    )(ys, ym, x2, mod3, wa, wb)


def kernel(x, c, positions, w_ada, b_ada, norm_g, w_in, log_dt, lam_re, lam_im, b_re, b_im, c_re, c_im, d_skip,
           w_glu, b_glu, q_a_g, w_q_b, kv_a_g, w_kv_b, q_norm_g, k_norm_g, w_out):
    n_b, seq, d = x.shape
    depth = w_ada.shape[0]
    c2 = seq // (CHUNK * SUPER)
    n_top = max(int(math.log2(c2)), 0)
    assert c2 * CHUNK * SUPER == seq and (1 << n_top) == c2
    rope = _rope_tables(positions)
    for l in range(depth):
        mod3 = _modulation(c, w_ada[l], b_ada[l]).reshape(n_b, 1, 3 * d)
        x2 = x.reshape(n_b * seq, d)
        lhs, wout, ap = _s5_coefficients(log_dt[l], lam_re[l], lam_im[l], b_re[l], b_im[l], c_re[l], c_im[l],
                                         d_skip[l], n_top)
        u_t, zs_t = _ssm_projection(x.reshape(n_b, c2, SUPER * CHUNK, d), mod3, norm_g[l], w_in[l])
        y_t = _s5_scan(u_t, lhs, wout, ap, n_b, c2, n_top)
        ys = _glu(y_t, zs_t, w_glu[l], b_glu[l], n_b, c2).reshape(n_b * seq, SSM_WIDTH)
        qt, k, vt, zm = _mla_projection(x2, mod3, norm_g[l], w_in[l], q_a_g[l], w_q_b[l],
                                        kv_a_g[l], w_kv_b[l], q_norm_g[l], k_norm_g[l], rope, n_b, seq)
        score_bound = (1.05 * math.sqrt(QK_HEAD) * math.log2(math.e)
                       * jnp.max(jnp.abs(q_norm_g[l])) * jnp.max(jnp.abs(k_norm_g[l])))
        ym = _attention(qt, k, vt, zm, score_bound)
        x = _out_projection(ys, ym, x2, mod3, w_out[l], seq).reshape(n_b, seq, d)
    return x
```

```python
import functools
import math

import numpy as np
import jax
import jax.numpy as jnp
from jax import lax
from jax.experimental import pallas as pl
from jax.experimental.pallas import tpu as pltpu

F32 = jnp.float32
BF16 = jnp.bfloat16
HIGHEST = lax.Precision.HIGHEST

D_MODEL = 1024
SSM_WIDTH = 512
SSM_GROUP = 16
SSM_GROUPS = 32
SSM_STATE = 64
MLA_HEADS = 8
QK_NOPE = 64
QK_ROPE = 32
QK_HEAD = QK_NOPE + QK_ROPE
V_HEAD = 64
MLA_WIDTH = MLA_HEADS * V_HEAD
Q_LORA = 384
KV_LORA = 256
ROPE_THETA = 10000.0
EPS = 1e-6
NEG_BIG = -1e30

LANES = 128
CHUNK = 16
SUPER = 4
ROPE_HALF = QK_ROPE // 2

_NT = (((1,), (1,)), ((), ()))
_TN = (((0,), (0,)), ((), ()))


def _silu(v):
    return v * jax.nn.sigmoid(v)


def _rms(v, gain):
    return v * lax.rsqrt(jnp.mean(v * v, axis=-1, keepdims=True) + EPS) * gain


def _mod_kernel(c_ref, w_ref, b_ref, o_ref):
    act = _silu(c_ref[...])
    o_ref[...] = jnp.dot(act, w_ref[...], preferred_element_type=F32, precision=HIGHEST) + b_ref[...]


def _modulation(c, w, b):
    n_b, d = c.shape
    rows = 8
    c_pad = jnp.zeros((rows, d), F32).at[:n_b].set(c)
    tn = 512
    out = pl.pallas_call(
        _mod_kernel,
        out_shape=jax.ShapeDtypeStruct((rows, 3 * d), F32),
        grid=(3 * d // tn,),
        in_specs=[pl.BlockSpec((rows, d), lambda n: (0, 0)),
                  pl.BlockSpec((d, tn), lambda n: (0, n)),
                  pl.BlockSpec((1, tn), lambda n: (0, n))],
        out_specs=pl.BlockSpec((rows, tn), lambda n: (0, n)),
        name="adaln_modulation",
    )(c_pad, w, b.reshape(1, -1))
    return out[:n_b]


def _rope_kernel(pos_ref, freq_ref, cos_ref, sin_ref):
    ang = freq_ref[...] * pos_ref[...].astype(F32)
    cos_ref[...] = jnp.cos(ang)
    sin_ref[...] = jnp.sin(ang)


def _rope_tables(positions):
    t = positions.size
    tt = min(t, 2048)
    inv_freq = (ROPE_THETA ** (-np.arange(ROPE_HALF, dtype=np.float64) * 2.0 / QK_ROPE)).astype(np.float32)
    return pl.pallas_call(
        _rope_kernel,
        out_shape=(jax.ShapeDtypeStruct((ROPE_HALF, t), F32),) * 2,
        grid=(t // tt,),
        in_specs=[pl.BlockSpec((1, tt), lambda i: (0, i)),
                  pl.BlockSpec((ROPE_HALF, 1), lambda i: (0, 0))],
        out_specs=(pl.BlockSpec((ROPE_HALF, tt), lambda i: (0, i)),) * 2,
        name="rope_tables",
    )(positions.reshape(1, t), jnp.asarray(inv_freq).reshape(ROPE_HALF, 1))


def _slab_source():
    src = np.full((LANES,), -1, np.int64)
    src[0:16] = QK_NOPE + np.arange(16)
    src[16:64] = np.arange(48)
    src[64:80] = QK_NOPE + 16 + np.arange(16)
    src[80:96] = 48 + np.arange(16)
    return src


def _to_slabs(w, per_head, src):
    k = w.shape[0]
    w3 = w.reshape(k, MLA_HEADS, per_head)
    valid = src >= 0
    cols = jnp.where(valid[None, None, :], w3[:, :, np.clip(src, 0, per_head - 1)], 0.0)
    return cols.reshape(k, MLA_HEADS * LANES)


V_ROWS = V_HEAD + 16


def _mla_proj_kernel(x_ref, mod_ref, ng_ref, w1_ref, qag_ref, kvag_ref, wqt_ref, wkt_ref, wvt_ref,
                     gq_ref, gk_ref, cost_ref, sint_ref, qt_ref, k_ref, vt_ref, zs_ref):
    d = D_MODEL
    tm = x_ref.shape[0]
    x = x_ref[...]
    mod = mod_ref[...]
    shift, scale = mod[:, :d], mod[:, d:2 * d]
    h = (_rms(x, ng_ref[...] * (1.0 + scale)) + shift).astype(BF16)
    p1 = jnp.dot(h, w1_ref[...], preferred_element_type=F32)
    o1, o2, o3 = Q_LORA, Q_LORA + KV_LORA, Q_LORA + KV_LORA + MLA_WIDTH
    cqn = _rms(p1[:, :o1], qag_ref[...]).astype(BF16)
    ckvn = _rms(p1[:, o1:o2], kvag_ref[...]).astype(BF16)
    zs_ref[...] = _silu(p1[:, o2:o3]).astype(BF16)

    qt = lax.dot_general(wqt_ref[...], cqn, _NT, preferred_element_type=F32)
    kt = lax.dot_general(wkt_ref[...], ckvn, _NT, preferred_element_type=F32)
    krt = p1[:, o3:].T
    vt = lax.dot_general(wvt_ref[...], ckvn, _NT, preferred_element_type=F32)
    cos_t, sin_t = cost_ref[...], sint_ref[...]
    gq = jnp.concatenate([gq_ref[...]] * (tm // LANES), axis=1)
    gk = jnp.concatenate([gk_ref[...]] * (tm // LANES), axis=1)
    first = lax.broadcasted_iota(jnp.int32, (V_ROWS - V_HEAD, tm), 0) == 0
    ones_rows = jnp.where(first, 1.0, 0.0).astype(BF16)
    r, half = ROPE_HALF, LANES // 2

    def norm_rope(slab, gain):
        ss = jnp.sum(slab * slab, axis=0, keepdims=True)
        n = slab * lax.rsqrt(ss * (1.0 / QK_HEAD) + EPS) * gain
        x1, x2 = n[0:r], n[half:half + r]
        return jnp.concatenate([x1 * cos_t - x2 * sin_t, n[r:half], x2 * cos_t + x1 * sin_t, n[half + r:]], axis=0)

    for head in range(MLA_HEADS):
        rows = slice(LANES * head, LANES * (head + 1))
        qt_ref[head] = norm_rope(qt[rows], gq).astype(BF16)
        k_ref[head] = norm_rope(kt[rows] + krt, gk).T.astype(BF16)
        vt_ref[head, 0:V_HEAD, :] = vt[V_HEAD * head:V_HEAD * (head + 1)].astype(BF16)
        vt_ref[head, V_HEAD:, :] = ones_rows


def _mla_projection(x2, mod3, norm_g, w_in, q_a_g, w_q_b, kv_a_g, w_kv_b, q_norm_g, k_norm_g, rope, n_b, seq):
    t, d = x2.shape
    cos_t, sin_t = rope
    tm = min(512, seq)
    per_b = seq // tm
    o2 = 2 * SSM_WIDTH
    o4, o5 = o2 + Q_LORA + KV_LORA, o2 + Q_LORA + KV_LORA + QK_ROPE
    src = _slab_source()
    valid = src >= 0
    rope_src = np.where(src >= QK_NOPE, src - QK_NOPE, -1)
    nope_src = np.where(valid & (src < QK_NOPE), src, -1)
    wkr = jnp.where((rope_src >= 0)[None, :], w_in[:, o4:o5][:, np.clip(rope_src, 0, QK_ROPE - 1)], 0.0)
    w1 = jnp.concatenate([w_in[:, o2:o4], w_in[:, o5:], wkr], axis=1).astype(BF16)
    wqt = _to_slabs(w_q_b, QK_HEAD, src).T.astype(BF16)
    wkt = _to_slabs(w_kv_b, QK_NOPE + V_HEAD, nope_src).T.astype(BF16)
    wvt = w_kv_b.reshape(KV_LORA, MLA_HEADS, QK_NOPE + V_HEAD)[:, :, QK_NOPE:].reshape(KV_LORA, MLA_WIDTH).T.astype(BF16)
    q_scale = math.log2(math.e) / math.sqrt(QK_HEAD)
    slab_gain = lambda g: jnp.broadcast_to(jnp.where(valid, g[np.clip(src, 0, QK_HEAD - 1)], 0.0)[:, None],
                                           (LANES, LANES))
    gq = slab_gain(q_norm_g * q_scale)
    gk = slab_gain(k_norm_g)
    const = lambda shape: pl.BlockSpec(shape, lambda i: (0,) * len(shape))
    tok = lambda i: (i // per_b, 0, 0, i % per_b)
    return pl.pallas_call(
        _mla_proj_kernel,
        out_shape=(jax.ShapeDtypeStruct((n_b, MLA_HEADS, LANES, seq), BF16),
                   jax.ShapeDtypeStruct((n_b, MLA_HEADS, seq, LANES), BF16),
                   jax.ShapeDtypeStruct((n_b, MLA_HEADS, V_ROWS, seq), BF16),
                   jax.ShapeDtypeStruct((t, MLA_WIDTH), BF16)),
        grid=(t // tm,),
        in_specs=[pl.BlockSpec((tm, d), lambda i: (i, 0)),
                  pl.BlockSpec((None, 1, 3 * d), lambda i: (i // per_b, 0, 0)),
                  const((1, d)), const(w1.shape), const((1, Q_LORA)), const((1, KV_LORA)),
                  const(wqt.shape), const(wkt.shape), const(wvt.shape),
                  const((LANES, LANES)), const((LANES, LANES)),
                  pl.BlockSpec((ROPE_HALF, tm), lambda i: (0, i)),
                  pl.BlockSpec((ROPE_HALF, tm), lambda i: (0, i))],
        out_specs=(pl.BlockSpec((None, MLA_HEADS, LANES, tm), tok),
                   pl.BlockSpec((None, MLA_HEADS, tm, LANES), lambda i: (i // per_b, 0, i % per_b, 0)),
                   pl.BlockSpec((None, MLA_HEADS, V_ROWS, tm), tok),
                   pl.BlockSpec((tm, MLA_WIDTH), lambda i: (i, 0))),
        compiler_params=pltpu.CompilerParams(dimension_semantics=("arbitrary",),
                                             vmem_limit_bytes=48 * 1024 * 1024),
        name="mla_projection",
    )(x2, mod3, norm_g.reshape(1, d), w1, q_a_g.reshape(1, -1), kv_a_g.reshape(1, -1), wqt, wkt, wvt, gq, gk,
      cos_t, sin_t)


ATTN_TQ = 2048
ATTN_TK = 256
HEADS_PER_STEP = 2


def _attn_kernel(qt_ref, k_ref, vt_ref, zs_ref, o_ref, s_scr):
    tq, tk = ATTN_TQ, ATTN_TK
    r = tq // tk
    assert r * tk == tq and r % 2 == 0
    qi = pl.program_id(2)

    def scores(slot, t, lo=0):
        start = pl.multiple_of(t * tk, tk)
        for e in range(HEADS_PER_STEP):
            s_scr[slot, e, :, lo:] = jnp.dot(k_ref[e, pl.ds(start, tk), :], qt_ref[e, :, lo:],
                                             preferred_element_type=F32)

    def process(slot, t, carry, lo=0, diagonal=False):
        start = pl.multiple_of(t * tk, tk)
        out = []
        for e in range(HEADS_PER_STEP):
            m, acc = carry[e]
            s = s_scr[slot, e, :, lo:]
            if diagonal:
                blk = s[:, :tk]
                ok = lax.broadcasted_iota(jnp.int32, blk.shape, 0) <= lax.broadcasted_iota(jnp.int32, blk.shape, 1)
                blk = jnp.where(ok, blk, NEG_BIG)
                s = jnp.concatenate([blk, s[:, tk:]], axis=1) if s.shape[1] > tk else blk
            m_old = m[:, lo:]
            m_new = jnp.maximum(m_old, jnp.max(s, axis=0, keepdims=True))
            p = jnp.exp2(s - m_new).astype(BF16)
            acc_new = (jnp.exp2(m_old - m_new) * acc[:, lo:]
                       + jnp.dot(vt_ref[e, :, pl.ds(start, tk)], p, preferred_element_type=F32))
            if lo:
                m_new = jnp.concatenate([m[:, :lo], m_new], axis=1)
                acc_new = jnp.concatenate([acc[:, :lo], acc_new], axis=1)
            out.append((m_new, acc_new))
        return tuple(out)

    def body(i, carry):
        base = i * r
        for u in range(r):
            scores((u + 1) % 2, base + u + 1)
            carry = process(u % 2, base + u, carry)
        return carry

    init = tuple((jnp.full((1, tq), NEG_BIG, F32), jnp.zeros((V_ROWS, tq), F32)) for _ in range(HEADS_PER_STEP))
    scores(0, 0)
    carry = lax.fori_loop(0, qi, body, init)
    base = qi * r
    for u in range(r):
        if u + 1 < r:
            scores((u + 1) % 2, base + u + 1, lo=(u + 1) * tk)
        carry = process(u % 2, base + u, carry, lo=u * tk, diagonal=True)
    _attn_finish(tuple(c[1] for c in carry), zs_ref, o_ref)


def _attn_finish(accs, zs_ref, o_ref):
    outs = [(acc[:V_HEAD] / acc[V_HEAD:V_HEAD + 1]).T for acc in accs]
    o_ref[...] = (jnp.concatenate(outs, axis=1) * zs_ref[...].astype(F32)).astype(o_ref.dtype)


def _attn_bounded_kernel(qt_ref, k_ref, vt_ref, zs_ref, o_ref, p_scr):
    tq, tk = ATTN_TQ, ATTN_TK
    r = tq // tk
    assert r * tk == tq and r % 2 == 0
    qi = pl.program_id(2)
    q0 = qi * tq

    def probs(slot, t, lo=0, may_cross=False):
        start = pl.multiple_of(t * tk, tk)
        for e in range(HEADS_PER_STEP):
            p = jnp.exp2(jnp.dot(k_ref[e, pl.ds(start, tk), :], qt_ref[e, :, lo:], preferred_element_type=F32))
            if may_cross:
                blk = p[:, :tk]
                key = start + lax.broadcasted_iota(jnp.int32, blk.shape, 0)
                qry = q0 + lo + lax.broadcasted_iota(jnp.int32, blk.shape, 1)
                blk = jnp.where(key <= qry, blk, 0.0)
                p = jnp.concatenate([blk, p[:, tk:]], axis=1) if p.shape[1] > tk else blk
            p_scr[slot, e, :, lo:] = p.astype(BF16)

    def accumulate(slot, t, accs, lo=0):
        start = pl.multiple_of(t * tk, tk)
        out = []
        for e in range(HEADS_PER_STEP):
            pv = jnp.dot(vt_ref[e, :, pl.ds(start, tk)], p_scr[slot, e, :, lo:], preferred_element_type=F32)
            acc = accs[e]
            out.append(acc + pv if lo == 0 else jnp.concatenate([acc[:, :lo], acc[:, lo:] + pv], axis=1))
        return tuple(out)

    def body(i, accs):
        base = i * r
        for u in range(r):
            probs((u + 1) % 2, base + u + 1, may_cross=(u == r - 1))
            accs = accumulate(u % 2, base + u, accs)
        return accs

    probs(0, 0, may_cross=True)
    accs = lax.fori_loop(0, qi, body, tuple(jnp.zeros((V_ROWS, tq), F32) for _ in range(HEADS_PER_STEP)))
    base = qi * r
    for u in range(r):
        if u + 1 < r:
            probs((u + 1) % 2, base + u + 1, lo=(u + 1) * tk, may_cross=True)
        accs = accumulate(u % 2, base + u, accs, lo=u * tk)
    _attn_finish(accs, zs_ref, o_ref)


SCORE_BOUND_LOG2 = 60.0


def _attention(qt, k, vt, zs, score_bound):
    n_b, n_h, seq, _ = k.shape
    tq = ATTN_TQ
    nq = seq // tq
    hp = HEADS_PER_STEP
    width = hp * V_HEAD

    def call(body, scratch):
        return pl.pallas_call(
            body,
            out_shape=jax.ShapeDtypeStruct(zs.shape, BF16),
            grid=(n_b, n_h // hp, nq),
            in_specs=[pl.BlockSpec((None, hp, LANES, tq), lambda b, h, i: (b, h, 0, i)),
                      pl.BlockSpec((None, hp, seq, LANES), lambda b, h, i: (b, h, 0, 0)),
                      pl.BlockSpec((None, hp, V_ROWS, seq), lambda b, h, i: (b, h, 0, 0)),
                      pl.BlockSpec((tq, width), lambda b, h, i: (b * nq + i, h))],
            out_specs=pl.BlockSpec((tq, width), lambda b, h, i: (b * nq + i, h)),
            scratch_shapes=scratch,
            compiler_params=pltpu.CompilerParams(dimension_semantics=("arbitrary", "arbitrary", "arbitrary"),
                                                 vmem_limit_bytes=48 * 1024 * 1024),
            name="causal_attention",
        )(qt, k, vt, zs)

    tiles = (2, hp, ATTN_TK, tq)
    return lax.cond(score_bound <= SCORE_BOUND_LOG2,
                    lambda: call(_attn_bounded_kernel, [pltpu.VMEM(tiles, BF16)]),
                    lambda: call(_attn_kernel, [pltpu.VMEM(tiles, F32)]))


def _s5_scan_steps(n_top):
    return [CHUNK * m for m in range(SUPER)] + [CHUNK * SUPER * (1 << i) for i in range(n_top)]


def _s5_exponents(n_top):
    return sorted(set(range(CHUNK + 1)) | set(_s5_scan_steps(n_top)))


def _s5_coef_kernel(kk_ref, *refs, n_top):
    for gi in range(SCAN_GROUPS):
        _s5_coef_one_group(kk_ref, *(r.at[gi] for r in refs), n_top=n_top)


def _s5_coef_one_group(kk_ref, lre_ref, lim_ref, ldt_ref, bre_ref, bim_ref, cre_ref, cim_ref, d_ref,
                       lhs_ref, wout_ref, ap_ref, *, n_top):
    h, p, n = SSM_GROUP, SSM_STATE, CHUNK
    lre, lim = lre_ref[...], lim_ref[...]
    dt = jnp.exp(ldt_ref[...])
    exps = _s5_exponents(n_top)
    kcol = kk_ref[...]
    mag = jnp.exp(kcol * (lre * dt))
    ang = kcol * (lim * dt)
    pow_re, pow_im = mag * jnp.cos(ang), mag * jnp.sin(ang)

    def power(kk):
        i = exps.index(kk)
        return pow_re[i:i + 1], pow_im[i:i + 1]

    lb_re, lb_im = power(1)
    nr, ni = lb_re - 1.0, lb_im
    den = lre * lre + lim * lim
    f_re = (nr * lre + ni * lim) / den
    f_im = (ni * lre - nr * lim) / den
    bre, bim = bre_ref[...].T, bim_ref[...].T
    bb_re = f_re * bre - f_im * bim
    bb_im = f_re * bim + f_im * bre
    cre, cim = cre_ref[...], cim_ref[...]
    cp_re, cp_im = [], []
    for kk in range(n + 1):
        pr, pi = power(kk)
        cp_re.append(cre * pr - cim * pi)
        cp_im.append(cre * pi + cim * pr)

    cpw = [jnp.concatenate([cp_re[kk], -cp_im[kk]], axis=1) for kk in range(n + 1)]
    bb_a = jnp.concatenate([bb_re, bb_im], axis=1)
    kw = lax.dot_general(jnp.concatenate(cpw[:n], axis=0), jnp.concatenate([bb_a] * n, axis=0), _NT,
                         preferred_element_type=F32, precision=HIGHEST)
    lane = lax.broadcasted_iota(jnp.int32, (h, n * h), 1)
    row = lax.broadcasted_iota(jnp.int32, (h, n * h), 0)
    col_blk = lax.shift_right_logical(lane, 4)
    d_tiled = jnp.concatenate([d_ref[...]] * n, axis=1)
    lag = [kw[h * kk:h * (kk + 1)] for kk in range(n)]
    lag[0] = lag[0] + jnp.where((lane & (h - 1)) == row, d_tiled, 0.0)
    rows = []
    for j in range(n):
        acc = jnp.zeros((h, n * h), F32)
        for jp in range(j + 1):
            acc = jnp.where(col_blk == jp, lag[j - jp], acc)
        rows.append(acc)
    lhs_ref[0:n * h, :] = jnp.concatenate(rows, axis=0).astype(lhs_ref.dtype)

    bb_b = jnp.concatenate([-bb_im, bb_re], axis=1)
    win = []
    for j in range(n):
        pr, pi = power(n - 1 - j)
        win.append(jnp.concatenate([pr, pr], axis=1) * bb_a + jnp.concatenate([pi, pi], axis=1) * bb_b)
    lhs_ref[n * h:, :] = jnp.concatenate(win, axis=0).T.astype(lhs_ref.dtype)
    wout_ref[...] = jnp.concatenate(cpw[1:], axis=0).astype(wout_ref.dtype)
    for i, kk in enumerate(_s5_scan_steps(n_top)):
        pr, pi = power(kk)
        ap_ref[i:i + 1, :] = jnp.concatenate([pr, pi], axis=1)


def _s5_coefficients(log_dt, lam_re, lam_im, b_re, b_im, c_re, c_im, d_skip, n_top):
    g, p, h = SSM_GROUPS, SSM_STATE, SSM_GROUP
    n_ap = SUPER + n_top
    gb = SCAN_GROUPS
    grp = lambda *shape: pl.BlockSpec((gb,) + shape, lambda i: (i,) + (0,) * len(shape))
    exps = np.asarray(_s5_exponents(n_top), np.float32)
    kk = np.zeros((-(-exps.size // 8) * 8, 1), np.float32)
    kk[:exps.size, 0] = exps
    return pl.pallas_call(
        functools.partial(_s5_coef_kernel, n_top=n_top),
        out_shape=(jax.ShapeDtypeStruct((g, CHUNK * h + 2 * p, CHUNK * h), BF16),
                   jax.ShapeDtypeStruct((g, CHUNK * h, 2 * p), BF16),
                   jax.ShapeDtypeStruct((g, n_ap, 2 * p), F32)),
        grid=(g // gb,),
        in_specs=[pl.BlockSpec(kk.shape, lambda i: (0, 0)),
                  grp(1, p), grp(1, p), grp(1, 1), grp(p, h), grp(p, h), grp(h, p), grp(h, p), grp(1, h)],
        out_specs=(grp(CHUNK * h + 2 * p, CHUNK * h), grp(CHUNK * h, 2 * p), grp(n_ap, 2 * p)),
        name="s5_coefficients",
    )(jnp.asarray(kk), lam_re.reshape(g, 1, p), lam_im.reshape(g, 1, p), log_dt.reshape(g, 1, 1), b_re, b_im,
      c_re, c_im, d_skip.reshape(g, 1, h))


PHASES = 8


def _ssm_proj_kernel(x_ref, mod_ref, ng_ref, w_ref, u_ref, z_ref, rows_scr, *, n_b):
    d = D_MODEL
    for jj in range(PHASES):
        hs = []
        for b in range(n_b):
            mod = mod_ref[b]
            shift, scale = mod[:, :d], mod[:, d:2 * d]
            rows_scr[b] = x_ref[b, :, jj, :]
            hs.append((_rms(rows_scr[b], ng_ref[...] * (1.0 + scale)) + shift).astype(BF16))
        h = jnp.concatenate(hs, axis=0)
        r = lax.dot_general(w_ref[...], h, _NT, preferred_element_type=F32)
        u_ref[jj] = r[:SSM_WIDTH].astype(BF16)
        z_ref[jj] = _silu(r[SSM_WIDTH:]).astype(BF16)


def _ssm_projection(x4, mod3, norm_g, w_in):
    n_b, c2, _, d = x4.shape
    lanes = n_b * c2
    halves = CHUNK // PHASES
    w_t = w_in[:, :2 * SSM_WIDTH].T.astype(BF16)
    out = jax.ShapeDtypeStruct((CHUNK, SSM_WIDTH, SUPER * lanes), BF16)
    out_spec = pl.BlockSpec((PHASES, SSM_WIDTH, lanes), lambda j2, hf: (hf, 0, j2))
    return pl.pallas_call(
        functools.partial(_ssm_proj_kernel, n_b=n_b),
        out_shape=(out, out),
        grid=(SUPER, halves),
        in_specs=[pl.BlockSpec((n_b, c2, PHASES, d), lambda j2, hf: (0, 0, j2 * halves + hf, 0)),
                  pl.BlockSpec((n_b, 1, 3 * d), lambda j2, hf: (0, 0, 0)),
                  pl.BlockSpec((1, d), lambda j2, hf: (0, 0)),
                  pl.BlockSpec(w_t.shape, lambda j2, hf: (0, 0))],
        out_specs=(out_spec, out_spec),
        scratch_shapes=[pltpu.VMEM((n_b, c2, d), F32)],
        compiler_params=pltpu.CompilerParams(dimension_semantics=("arbitrary", "arbitrary"),
                                             vmem_limit_bytes=56 * 1024 * 1024),
        name="ssm_projection",
    )(x4, mod3, norm_g.reshape(1, d), w_t)


def _cmul(ar, ai, xr, xi):
    return ar * xr - ai * xi, ar * xi + ai * xr


SCAN_GROUPS = 4


def _s5_group_kernel(x_ref, lhs_ref, wout_ref, ap_ref, y_ref, *, n_b, c2, n_top):
    p, gb = SSM_STATE, SCAN_GROUPS
    rows = CHUNK * SSM_GROUP
    n = SUPER * n_b * c2
    r = [jnp.dot(lhs_ref[gi], x_ref[:, SSM_GROUP * gi:SSM_GROUP * (gi + 1), :].reshape(rows, n),
                 preferred_element_type=F32) for gi in range(gb)]
    c_re = jnp.concatenate([r[gi][rows:rows + p] for gi in range(gb)], axis=0)
    c_im = jnp.concatenate([r[gi][rows + p:] for gi in range(gb)], axis=0)

    def mult(i):
        rep = [jnp.broadcast_to(ap_ref[gi, i:i + 1, :], (2 * p, 2 * p)).T for gi in range(gb)]
        return (jnp.concatenate([m[:p, :c2] for m in rep], axis=0),
                jnp.concatenate([m[p:, :c2] for m in rep], axis=0))

    lane = lax.broadcasted_iota(jnp.int32, (gb * p, c2), 1)

    def shifted(a, sh):
        return jnp.where(lane >= sh, pltpu.roll(a, sh, 1), 0.0)

    pieces = [[None] * n_b for _ in range(SUPER)]
    for b in range(n_b):
        piece = lambda a, j2: a[:, (j2 * n_b + b) * c2:(j2 * n_b + b + 1) * c2]
        a_re, a_im = mult(1)
        e_re = jnp.zeros((gb * p, c2), F32)
        e_im = jnp.zeros((gb * p, c2), F32)
        local = []
        for j2 in range(SUPER):
            local.append((e_re, e_im))
            t_re, t_im = _cmul(a_re, a_im, e_re, e_im)
            e_re, e_im = t_re + piece(c_re, j2), t_im + piece(c_im, j2)
        for i in range(n_top):
            m_re, m_im = mult(SUPER + i)
            t_re, t_im = _cmul(m_re, m_im, shifted(e_re, 1 << i), shifted(e_im, 1 << i))
            e_re, e_im = e_re + t_re, e_im + t_im
        s_re, s_im = shifted(e_re, 1), shifted(e_im, 1)
        for j2 in range(SUPER):
            if j2 == 0:
                pieces[j2][b] = (s_re, s_im)
            else:
                m_re, m_im = mult(j2)
                t_re, t_im = _cmul(m_re, m_im, s_re, s_im)
                pieces[j2][b] = (local[j2][0] + t_re, local[j2][1] + t_im)
    for gi in range(gb):
        sl = slice(gi * p, (gi + 1) * p)
        s_in = jnp.concatenate([jnp.concatenate([pieces[j2][b][0][sl], pieces[j2][b][1][sl]], axis=0)
                                for j2 in range(SUPER) for b in range(n_b)], axis=1)
        y = r[gi][:rows] + jnp.dot(wout_ref[gi], s_in.astype(BF16), preferred_element_type=F32)
        y_ref[gi] = y.reshape(CHUNK, SSM_GROUP, n).astype(y_ref.dtype)


def _s5_scan(u_t, lhs, wout, ap, n_b, c2, n_top):
    g, h = SSM_GROUPS, SSM_GROUP
    n = u_t.shape[-1]
    gb = SCAN_GROUPS
    grp = lambda *shape: pl.BlockSpec((gb,) + shape, lambda i: (i,) + (0,) * len(shape))
    return pl.pallas_call(
        functools.partial(_s5_group_kernel, n_b=n_b, c2=c2, n_top=n_top),
        out_shape=jax.ShapeDtypeStruct((g, CHUNK, h, n), BF16),
        grid=(g // gb,),
        in_specs=[pl.BlockSpec((CHUNK, gb * h, n), lambda i: (0, i, 0)),
                  grp(*lhs.shape[1:]), grp(*wout.shape[1:]), grp(*ap.shape[1:])],
        out_specs=grp(CHUNK, h, n),
        compiler_params=pltpu.CompilerParams(dimension_semantics=("arbitrary",),
                                             vmem_limit_bytes=48 * 1024 * 1024),
        name="s5_chunk_scan",
    )(u_t, lhs, wout, ap)


def _glu_kernel(y_ref, zs_ref, w_ref, b_ref, o_ref, *, n_b, c2):
    g, _, h, n = y_ref.shape
    for jj in range(PHASES):
        y = jax.nn.gelu(y_ref[:, jj].reshape(g * h, n).astype(F32))
        t = jnp.dot(w_ref[...], y.astype(BF16), preferred_element_type=F32) + b_ref[...]
        gated = (y * jax.nn.sigmoid(t) * zs_ref[jj].astype(F32)).T
        for b in range(n_b):
            o_ref[b, :, jj, :] = gated[b * c2:(b + 1) * c2]


def _glu(y_t, zs_t, w_glu, b_glu, n_b, c2):
    g, _, h, _ = y_t.shape
    lanes = n_b * c2
    halves = CHUNK // PHASES
    w_t = w_glu.T.astype(BF16)
    return pl.pallas_call(
        functools.partial(_glu_kernel, n_b=n_b, c2=c2),
        out_shape=jax.ShapeDtypeStruct((n_b, c2, SUPER * CHUNK, SSM_WIDTH), F32),
        grid=(SUPER, halves),
        in_specs=[pl.BlockSpec((g, PHASES, h, lanes), lambda j2, hf: (0, hf, 0, j2)),
                  pl.BlockSpec((PHASES, SSM_WIDTH, lanes), lambda j2, hf: (hf, 0, j2)),
                  pl.BlockSpec(w_t.shape, lambda j2, hf: (0, 0)),
                  pl.BlockSpec((SSM_WIDTH, 1), lambda j2, hf: (0, 0))],
        out_specs=pl.BlockSpec((n_b, c2, PHASES, SSM_WIDTH), lambda j2, hf: (0, 0, j2 * halves + hf, 0)),
        compiler_params=pltpu.CompilerParams(dimension_semantics=("arbitrary", "arbitrary"),
                                             vmem_limit_bytes=48 * 1024 * 1024),
        name="s5_glu",
    )(y_t, zs_t, w_t, b_glu.reshape(SSM_WIDTH, 1))


def _out_proj_kernel(ys_ref, ym_ref, x_ref, mod_ref, wa_ref, wb_ref, o_ref):
    y = (jnp.dot(ys_ref[...].astype(BF16), wa_ref[...], preferred_element_type=F32)
         + jnp.dot(ym_ref[...], wb_ref[...], preferred_element_type=F32))
    gate = mod_ref[...][:, 2 * D_MODEL:]
    o_ref[...] = x_ref[...] + gate * y


def _out_projection(ys, ym, x2, mod3, w_out, seq):
    t, d = x2.shape
    tm = min(1024, seq)
    per_b = seq // tm
    wa = w_out[:SSM_WIDTH].astype(BF16)
    wb = w_out[SSM_WIDTH:].astype(BF16)
    tok = lambda w: pl.BlockSpec((tm, w), lambda i: (i, 0))
    return pl.pallas_call(
        _out_proj_kernel,
        out_shape=jax.ShapeDtypeStruct(x2.shape, F32),
        grid=(t // tm,),
        in_specs=[tok(SSM_WIDTH), tok(MLA_WIDTH), tok(d),
                  pl.BlockSpec((None, 1, 3 * d), lambda i: (i // per_b, 0, 0)),
                  pl.BlockSpec(wa.shape, lambda i: (0, 0)),
                  pl.BlockSpec(wb.shape, lambda i: (0, 0))],
        out_specs=tok(d),
        compiler_params=pltpu.CompilerParams(dimension_semantics=("arbitrary",),
                                             vmem_limit_bytes=48 * 1024 * 1024),
        name="output_projection",
    )(ys, ym, x2, mod3, wa, wb)


def kernel(x, c, positions, w_ada, b_ada, norm_g, w_in, log_dt, lam_re, lam_im, b_re, b_im, c_re, c_im, d_skip,
           w_glu, b_glu, q_a_g, w_q_b, kv_a_g, w_kv_b, q_norm_g, k_norm_g, w_out):
    n_b, seq, d = x.shape
    depth = w_ada.shape[0]
    c2 = seq // (CHUNK * SUPER)
    n_top = max(int(math.log2(c2)), 0)
    assert c2 * CHUNK * SUPER == seq and (1 << n_top) == c2
    rope = _rope_tables(positions)
    for l in range(depth):
        mod3 = _modulation(c, w_ada[l], b_ada[l]).reshape(n_b, 1, 3 * d)
        x2 = x.reshape(n_b * seq, d)
        lhs, wout, ap = _s5_coefficients(log_dt[l], lam_re[l], lam_im[l], b_re[l], b_im[l], c_re[l], c_im[l],
                                         d_skip[l], n_top)
        u_t, zs_t = _ssm_projection(x.reshape(n_b, c2, SUPER * CHUNK, d), mod3, norm_g[l], w_in[l])
        y_t = _s5_scan(u_t, lhs, wout, ap, n_b, c2, n_top)
        ys = _glu(y_t, zs_t, w_glu[l], b_glu[l], n_b, c2).reshape(n_b * seq, SSM_WIDTH)
        qt, k, vt, zm = _mla_projection(x2, mod3, norm_g[l], w_in[l], q_a_g[l], w_q_b[l],
                                        kv_a_g[l], w_kv_b[l], q_norm_g[l], k_norm_g[l], rope, n_b, seq)
        score_bound = (1.05 * math.sqrt(QK_HEAD) * math.log2(math.e)
                       * jnp.max(jnp.abs(q_norm_g[l])) * jnp.max(jnp.abs(k_norm_g[l])))
        ym = _attention(qt, k, vt, zm, score_bound)
        x = _out_projection(ys, ym, x2, mod3, w_out[l], seq).reshape(n_b, seq, d)
    return x
```

```python
import functools
import math

import numpy as np
import jax
import jax.numpy as jnp
from jax import lax
from jax.experimental import pallas as pl
from jax.experimental.pallas import tpu as pltpu

F32 = jnp.float32
BF16 = jnp.bfloat16
HIGHEST = lax.Precision.HIGHEST

D_MODEL = 1024
SSM_WIDTH = 512
SSM_GROUP = 16
SSM_GROUPS = 32
SSM_STATE = 64
MLA_HEADS = 8
QK_NOPE = 64
QK_ROPE = 32
QK_HEAD = QK_NOPE + QK_ROPE
V_HEAD = 64
MLA_WIDTH = MLA_HEADS * V_HEAD
Q_LORA = 384
KV_LORA = 256
ROPE_THETA = 10000.0
EPS = 1e-6
NEG_BIG = -1e30

LANES = 128
CHUNK = 16
SUPER = 4
ROPE_HALF = QK_ROPE // 2

_NT = (((1,), (1,)), ((), ()))
_TN = (((0,), (0,)), ((), ()))


def _silu(v):
    return v * jax.nn.sigmoid(v)


def _rms(v, gain):
    return v * lax.rsqrt(jnp.mean(v * v, axis=-1, keepdims=True) + EPS) * gain


def _mod_kernel(c_ref, w_ref, b_ref, o_ref):
    act = _silu(c_ref[...])
    o_ref[...] = jnp.dot(act, w_ref[...], preferred_element_type=F32, precision=HIGHEST) + b_ref[...]


def _modulation(c, w, b):
    n_b, d = c.shape
    rows = 8
    c_pad = jnp.zeros((rows, d), F32).at[:n_b].set(c)
    tn = 512
    out = pl.pallas_call(
        _mod_kernel,
        out_shape=jax.ShapeDtypeStruct((rows, 3 * d), F32),
        grid=(3 * d // tn,),
        in_specs=[pl.BlockSpec((rows, d), lambda n: (0, 0)),
                  pl.BlockSpec((d, tn), lambda n: (0, n)),
                  pl.BlockSpec((1, tn), lambda n: (0, n))],
        out_specs=pl.BlockSpec((rows, tn), lambda n: (0, n)),
        name="adaln_modulation",
    )(c_pad, w, b.reshape(1, -1))
    return out[:n_b]


def _rope_kernel(pos_ref, freq_ref, cos_ref, sin_ref):
    ang = freq_ref[...] * pos_ref[...].astype(F32)
    cos_ref[...] = jnp.cos(ang)
    sin_ref[...] = jnp.sin(ang)


def _rope_tables(positions):
    t = positions.size
    tt = min(t, 2048)
    inv_freq = (ROPE_THETA ** (-np.arange(ROPE_HALF, dtype=np.float64) * 2.0 / QK_ROPE)).astype(np.float32)
    return pl.pallas_call(
        _rope_kernel,
        out_shape=(jax.ShapeDtypeStruct((ROPE_HALF, t), F32),) * 2,
        grid=(t // tt,),
        in_specs=[pl.BlockSpec((1, tt), lambda i: (0, i)),
                  pl.BlockSpec((ROPE_HALF, 1), lambda i: (0, 0))],
        out_specs=(pl.BlockSpec((ROPE_HALF, tt), lambda i: (0, i)),) * 2,
        name="rope_tables",
    )(positions.reshape(1, t), jnp.asarray(inv_freq).reshape(ROPE_HALF, 1))


def _slab_source():
    src = np.full((LANES,), -1, np.int64)
    src[0:16] = QK_NOPE + np.arange(16)
    src[16:64] = np.arange(48)
    src[64:80] = QK_NOPE + 16 + np.arange(16)
    src[80:96] = 48 + np.arange(16)
    return src


def _to_slabs(w, per_head, src):
    k = w.shape[0]
    w3 = w.reshape(k, MLA_HEADS, per_head)
    valid = src >= 0
    cols = jnp.where(valid[None, None, :], w3[:, :, np.clip(src, 0, per_head - 1)], 0.0)
    return cols.reshape(k, MLA_HEADS * LANES)


V_ROWS = V_HEAD + 16


def _mla_proj_kernel(x_ref, mod_ref, ng_ref, w1_ref, qag_ref, kvag_ref, wqt_ref, wkt_ref, wvt_ref,
                     gq_ref, gk_ref, cost_ref, sint_ref, qt_ref, k_ref, vt_ref, zs_ref):
    d = D_MODEL
    tm = x_ref.shape[0]
    x = x_ref[...]
    mod = mod_ref[...]
    shift, scale = mod[:, :d], mod[:, d:2 * d]
    h = (_rms(x, ng_ref[...] * (1.0 + scale)) + shift).astype(BF16)
    p1 = jnp.dot(h, w1_ref[...], preferred_element_type=F32)
    o1, o2, o3 = Q_LORA, Q_LORA + KV_LORA, Q_LORA + KV_LORA + MLA_WIDTH
    cqn = _rms(p1[:, :o1], qag_ref[...]).astype(BF16)
    ckvn = _rms(p1[:, o1:o2], kvag_ref[...]).astype(BF16)
    zs_ref[...] = _silu(p1[:, o2:o3]).astype(BF16)

    qt = lax.dot_general(wqt_ref[...], cqn, _NT, preferred_element_type=F32)
    kt = lax.dot_general(wkt_ref[...], ckvn, _NT, preferred_element_type=F32)
    krt = p1[:, o3:].T
    vt = lax.dot_general(wvt_ref[...], ckvn, _NT, preferred_element_type=F32)
    cos_t, sin_t = cost_ref[...], sint_ref[...]
    gq = jnp.concatenate([gq_ref[...]] * (tm // LANES), axis=1)
    gk = jnp.concatenate([gk_ref[...]] * (tm // LANES), axis=1)
    first = lax.broadcasted_iota(jnp.int32, (V_ROWS - V_HEAD, tm), 0) == 0
    ones_rows = jnp.where(first, 1.0, 0.0).astype(BF16)
    r, half = ROPE_HALF, LANES // 2

    def norm_rope(slab, gain):
        ss = jnp.sum(slab * slab, axis=0, keepdims=True)
        n = slab * lax.rsqrt(ss * (1.0 / QK_HEAD) + EPS) * gain
        x1, x2 = n[0:r], n[half:half + r]
        return jnp.concatenate([x1 * cos_t - x2 * sin_t, n[r:half], x2 * cos_t + x1 * sin_t, n[half + r:]], axis=0)

    for head in range(MLA_HEADS):
        rows = slice(LANES * head, LANES * (head + 1))
        qt_ref[head] = norm_rope(qt[rows], gq).astype(BF16)
        k_ref[head] = norm_rope(kt[rows] + krt, gk).T.astype(BF16)
        vt_ref[head, 0:V_HEAD, :] = vt[V_HEAD * head:V_HEAD * (head + 1)].astype(BF16)
        vt_ref[head, V_HEAD:, :] = ones_rows


def _mla_projection(x2, mod3, norm_g, w_in, q_a_g, w_q_b, kv_a_g, w_kv_b, q_norm_g, k_norm_g, rope, n_b, seq):
    t, d = x2.shape
    cos_t, sin_t = rope
    tm = min(512, seq)
    per_b = seq // tm
    o2 = 2 * SSM_WIDTH
    o4, o5 = o2 + Q_LORA + KV_LORA, o2 + Q_LORA + KV_LORA + QK_ROPE
    src = _slab_source()
    valid = src >= 0
    rope_src = np.where(src >= QK_NOPE, src - QK_NOPE, -1)
    nope_src = np.where(valid & (src < QK_NOPE), src, -1)
    wkr = jnp.where((rope_src >= 0)[None, :], w_in[:, o4:o5][:, np.clip(rope_src, 0, QK_ROPE - 1)], 0.0)
    w1 = jnp.concatenate([w_in[:, o2:o4], w_in[:, o5:], wkr], axis=1).astype(BF16)
    wqt = _to_slabs(w_q_b, QK_HEAD, src).T.astype(BF16)
    wkt = _to_slabs(w_kv_b, QK_NOPE + V_HEAD, nope_src).T.astype(BF16)
    wvt = w_kv_b.reshape(KV_LORA, MLA_HEADS, QK_NOPE + V_HEAD)[:, :, QK_NOPE:].reshape(KV_LORA, MLA_WIDTH).T.astype(BF16)
    q_scale = math.log2(math.e) / math.sqrt(QK_HEAD)
    slab_gain = lambda g: jnp.broadcast_to(jnp.where(valid, g[np.clip(src, 0, QK_HEAD - 1)], 0.0)[:, None],
                                           (LANES, LANES))
    gq = slab_gain(q_norm_g * q_scale)
    gk = slab_gain(k_norm_g)
    const = lambda shape: pl.BlockSpec(shape, lambda i: (0,) * len(shape))
    tok = lambda i: (i // per_b, 0, 0, i % per_b)
    return pl.pallas_call(
        _mla_proj_kernel,
        out_shape=(jax.ShapeDtypeStruct((n_b, MLA_HEADS, LANES, seq), BF16),
                   jax.ShapeDtypeStruct((n_b, MLA_HEADS, seq, LANES), BF16),
                   jax.ShapeDtypeStruct((n_b, MLA_HEADS, V_ROWS, seq), BF16),
                   jax.ShapeDtypeStruct((t, MLA_WIDTH), BF16)),
        grid=(t // tm,),
        in_specs=[pl.BlockSpec((tm, d), lambda i: (i, 0)),
                  pl.BlockSpec((None, 1, 3 * d), lambda i: (i // per_b, 0, 0)),
                  const((1, d)), const(w1.shape), const((1, Q_LORA)), const((1, KV_LORA)),
                  const(wqt.shape), const(wkt.shape), const(wvt.shape),
                  const((LANES, LANES)), const((LANES, LANES)),
                  pl.BlockSpec((ROPE_HALF, tm), lambda i: (0, i)),
                  pl.BlockSpec((ROPE_HALF, tm), lambda i: (0, i))],
        out_specs=(pl.BlockSpec((None, MLA_HEADS, LANES, tm), tok),
                   pl.BlockSpec((None, MLA_HEADS, tm, LANES), lambda i: (i // per_b, 0, i % per_b, 0)),
                   pl.BlockSpec((None, MLA_HEADS, V_ROWS, tm), tok),
                   pl.BlockSpec((tm, MLA_WIDTH), lambda i: (i, 0))),
        compiler_params=pltpu.CompilerParams(dimension_semantics=("arbitrary",),
                                             vmem_limit_bytes=48 * 1024 * 1024),
        name="mla_projection",
    )(x2, mod3, norm_g.reshape(1, d), w1, q_a_g.reshape(1, -1), kv_a_g.reshape(1, -1), wqt, wkt, wvt, gq, gk,
      cos_t, sin_t)


ATTN_TQ = 2048
ATTN_TK = 256
HEADS_PER_STEP = 2


def _attn_kernel(qt_ref, k_ref, vt_ref, zs_ref, o_ref, s_scr):
    tq, tk = ATTN_TQ, ATTN_TK
    r = tq // tk
    assert r * tk == tq and r % 2 == 0
    qi = pl.program_id(2)

    def scores(slot, t, lo=0):
        start = pl.multiple_of(t * tk, tk)
        for e in range(HEADS_PER_STEP):
            s_scr[slot, e, :, lo:] = jnp.dot(k_ref[e, pl.ds(start, tk), :], qt_ref[e, :, lo:],
                                             preferred_element_type=F32)

    def process(slot, t, carry, lo=0, diagonal=False):
        start = pl.multiple_of(t * tk, tk)
        out = []
        for e in range(HEADS_PER_STEP):
            m, acc = carry[e]
            s = s_scr[slot, e, :, lo:]
            if diagonal:
                blk = s[:, :tk]
                ok = lax.broadcasted_iota(jnp.int32, blk.shape, 0) <= lax.broadcasted_iota(jnp.int32, blk.shape, 1)
                blk = jnp.where(ok, blk, NEG_BIG)
                s = jnp.concatenate([blk, s[:, tk:]], axis=1) if s.shape[1] > tk else blk
            m_old = m[:, lo:]
            m_new = jnp.maximum(m_old, jnp.max(s, axis=0, keepdims=True))
            p = jnp.exp2(s - m_new).astype(BF16)
            acc_new = (jnp.exp2(m_old - m_new) * acc[:, lo:]
                       + jnp.dot(vt_ref[e, :, pl.ds(start, tk)], p, preferred_element_type=F32))
            if lo:
                m_new = jnp.concatenate([m[:, :lo], m_new], axis=1)
                acc_new = jnp.concatenate([acc[:, :lo], acc_new], axis=1)
            out.append((m_new, acc_new))
        return tuple(out)

    def body(i, carry):
        base = i * r
        for u in range(r):
            scores((u + 1) % 2, base + u + 1)
            carry = process(u % 2, base + u, carry)
        return carry

    init = tuple((jnp.full((1, tq), NEG_BIG, F32), jnp.zeros((V_ROWS, tq), F32)) for _ in range(HEADS_PER_STEP))
    scores(0, 0)
    carry = lax.fori_loop(0, qi, body, init)
    base = qi * r
    for u in range(r):
        if u + 1 < r:
            scores((u + 1) % 2, base + u + 1, lo=(u + 1) * tk)
        carry = process(u % 2, base + u, carry, lo=u * tk, diagonal=True)
    _attn_finish(tuple(c[1] for c in carry), zs_ref, o_ref)


def _attn_finish(accs, zs_ref, o_ref):
    outs = [(acc[:V_HEAD] / acc[V_HEAD:V_HEAD + 1]).T for acc in accs]
    o_ref[...] = (jnp.concatenate(outs, axis=1) * zs_ref[...].astype(F32)).astype(o_ref.dtype)


def _attn_bounded_kernel(qt_ref, k_ref, vt_ref, zs_ref, o_ref, p_scr):
    tq, tk = ATTN_TQ, ATTN_TK
    r = tq // tk
    assert r * tk == tq and r % 2 == 0
    qi = pl.program_id(2)
    q0 = qi * tq

    def probs(slot, t, lo=0, may_cross=False):
        start = pl.multiple_of(t * tk, tk)
        for e in range(HEADS_PER_STEP):
            p = jnp.exp2(jnp.dot(k_ref[e, pl.ds(start, tk), :], qt_ref[e, :, lo:], preferred_element_type=F32))
            if may_cross:
                blk = p[:, :tk]
                key = start + lax.broadcasted_iota(jnp.int32, blk.shape, 0)
                qry = q0 + lo + lax.broadcasted_iota(jnp.int32, blk.shape, 1)
                blk = jnp.where(key <= qry, blk, 0.0)
                p = jnp.concatenate([blk, p[:, tk:]], axis=1) if p.shape[1] > tk else blk
            p_scr[slot, e, :, lo:] = p.astype(BF16)

    def accumulate(slot, t, accs, lo=0):
        start = pl.multiple_of(t * tk, tk)
        out = []
        for e in range(HEADS_PER_STEP):
            pv = jnp.dot(vt_ref[e, :, pl.ds(start, tk)], p_scr[slot, e, :, lo:], preferred_element_type=F32)
            acc = accs[e]
            out.append(acc + pv if lo == 0 else jnp.concatenate([acc[:, :lo], acc[:, lo:] + pv], axis=1))
        return tuple(out)

    def body(i, accs):
        base = i * r
        for u in range(r):
            probs((u + 1) % 2, base + u + 1, may_cross=(u == r - 1))
            accs = accumulate(u % 2, base + u, accs)
        return accs

    probs(0, 0, may_cross=True)
    accs = lax.fori_loop(0, qi, body, tuple(jnp.zeros((V_ROWS, tq), F32) for _ in range(HEADS_PER_STEP)))
    base = qi * r
    for u in range(r):
        if u + 1 < r:
            probs((u + 1) % 2, base + u + 1, lo=(u + 1) * tk, may_cross=True)
        accs = accumulate(u % 2, base + u, accs, lo=u * tk)
    _attn_finish(accs, zs_ref, o_ref)


SCORE_BOUND_LOG2 = 60.0


def _attention(qt, k, vt, zs, score_bound):
    n_b, n_h, seq, _ = k.shape
    tq = ATTN_TQ
    nq = seq // tq
    hp = HEADS_PER_STEP
    width = hp * V_HEAD

    def call(body, scratch):
        return pl.pallas_call(
            body,
            out_shape=jax.ShapeDtypeStruct(zs.shape, BF16),
            grid=(n_b, n_h // hp, nq),
            in_specs=[pl.BlockSpec((None, hp, LANES, tq), lambda b, h, i: (b, h, 0, i)),
                      pl.BlockSpec((None, hp, seq, LANES), lambda b, h, i: (b, h, 0, 0)),
                      pl.BlockSpec((None, hp, V_ROWS, seq), lambda b, h, i: (b, h, 0, 0)),
                      pl.BlockSpec((tq, width), lambda b, h, i: (b * nq + i, h))],
            out_specs=pl.BlockSpec((tq, width), lambda b, h, i: (b * nq + i, h)),
            scratch_shapes=scratch,
            compiler_params=pltpu.CompilerParams(dimension_semantics=("arbitrary", "arbitrary", "arbitrary"),
                                                 vmem_limit_bytes=48 * 1024 * 1024),
            name="causal_attention",
        )(qt, k, vt, zs)

    tiles = (2, hp, ATTN_TK, tq)
    return lax.cond(score_bound <= SCORE_BOUND_LOG2,
                    lambda: call(_attn_bounded_kernel, [pltpu.VMEM(tiles, BF16)]),
                    lambda: call(_attn_kernel, [pltpu.VMEM(tiles, F32)]))


def _s5_scan_steps(n_top):
    return [CHUNK * m for m in range(SUPER)] + [CHUNK * SUPER * (1 << i) for i in range(n_top)]


def _s5_exponents(n_top):
    return sorted(set(range(CHUNK + 1)) | set(_s5_scan_steps(n_top)))


def _s5_coef_kernel(kk_ref, *refs, n_top):
    for gi in range(SCAN_GROUPS):
        _s5_coef_one_group(kk_ref, *(r.at[gi] for r in refs), n_top=n_top)


def _s5_coef_one_group(kk_ref, lre_ref, lim_ref, ldt_ref, bre_ref, bim_ref, cre_ref, cim_ref, d_ref,
                       lhs_ref, wout_ref, ap_ref, *, n_top):
    h, p, n = SSM_GROUP, SSM_STATE, CHUNK
    lre, lim = lre_ref[...], lim_ref[...]
    dt = jnp.exp(ldt_ref[...])
    exps = _s5_exponents(n_top)
    kcol = kk_ref[...]
    mag = jnp.exp(kcol * (lre * dt))
    ang = kcol * (lim * dt)
    pow_re, pow_im = mag * jnp.cos(ang), mag * jnp.sin(ang)

    def power(kk):
        i = exps.index(kk)
        return pow_re[i:i + 1], pow_im[i:i + 1]

    lb_re, lb_im = power(1)
    nr, ni = lb_re - 1.0, lb_im
    den = lre * lre + lim * lim
    f_re = (nr * lre + ni * lim) / den
    f_im = (ni * lre - nr * lim) / den
    bre, bim = bre_ref[...].T, bim_ref[...].T
    bb_re = f_re * bre - f_im * bim
    bb_im = f_re * bim + f_im * bre
    cre, cim = cre_ref[...], cim_ref[...]
    cp_re, cp_im = [], []
    for kk in range(n + 1):
        pr, pi = power(kk)
        cp_re.append(cre * pr - cim * pi)
        cp_im.append(cre * pi + cim * pr)

    cpw = [jnp.concatenate([cp_re[kk], -cp_im[kk]], axis=1) for kk in range(n + 1)]
    bb_a = jnp.concatenate([bb_re, bb_im], axis=1)
    kw = lax.dot_general(jnp.concatenate(cpw[:n], axis=0), jnp.concatenate([bb_a] * n, axis=0), _NT,
                         preferred_element_type=F32, precision=HIGHEST)
    lane = lax.broadcasted_iota(jnp.int32, (h, n * h), 1)
    row = lax.broadcasted_iota(jnp.int32, (h, n * h), 0)
    col_blk = lax.shift_right_logical(lane, 4)
    d_tiled = jnp.concatenate([d_ref[...]] * n, axis=1)
    lag = [kw[h * kk:h * (kk + 1)] for kk in range(n)]
    lag[0] = lag[0] + jnp.where((lane & (h - 1)) == row, d_tiled, 0.0)
    rows = []
    for j in range(n):
        acc = jnp.zeros((h, n * h), F32)
        for jp in range(j + 1):
            acc = jnp.where(col_blk == jp, lag[j - jp], acc)
        rows.append(acc)
    lhs_ref[0:n * h, :] = jnp.concatenate(rows, axis=0).astype(lhs_ref.dtype)

    bb_b = jnp.concatenate([-bb_im, bb_re], axis=1)
    win = []
    for j in range(n):
        pr, pi = power(n - 1 - j)
        win.append(jnp.concatenate([pr, pr], axis=1) * bb_a + jnp.concatenate([pi, pi], axis=1) * bb_b)
    lhs_ref[n * h:, :] = jnp.concatenate(win, axis=0).T.astype(lhs_ref.dtype)
    wout_ref[...] = jnp.concatenate(cpw[1:], axis=0).astype(wout_ref.dtype)
    for i, kk in enumerate(_s5_scan_steps(n_top)):
        pr, pi = power(kk)
        ap_ref[i:i + 1, :] = jnp.concatenate([pr, pi], axis=1)


def _s5_coefficients(log_dt, lam_re, lam_im, b_re, b_im, c_re, c_im, d_skip, n_top):
    g, p, h = SSM_GROUPS, SSM_STATE, SSM_GROUP
    n_ap = SUPER + n_top
    gb = SCAN_GROUPS
    grp = lambda *shape: pl.BlockSpec((gb,) + shape, lambda i: (i,) + (0,) * len(shape))
    exps = np.asarray(_s5_exponents(n_top), np.float32)
    kk = np.zeros((-(-exps.size // 8) * 8, 1), np.float32)
    kk[:exps.size, 0] = exps
    return pl.pallas_call(
        functools.partial(_s5_coef_kernel, n_top=n_top),
        out_shape=(jax.ShapeDtypeStruct((g, CHUNK * h + 2 * p, CHUNK * h), BF16),
                   jax.ShapeDtypeStruct((g, CHUNK * h, 2 * p), BF16),
                   jax.ShapeDtypeStruct((g, n_ap, 2 * p), F32)),
        grid=(g // gb,),
        in_specs=[pl.BlockSpec(kk.shape, lambda i: (0, 0)),
                  grp(1, p), grp(1, p), grp(1, 1), grp(p, h), grp(p, h), grp(h, p), grp(h, p), grp(1, h)],
        out_specs=(grp(CHUNK * h + 2 * p, CHUNK * h), grp(CHUNK * h, 2 * p), grp(n_ap, 2 * p)),
        name="s5_coefficients",
    )(jnp.asarray(kk), lam_re.reshape(g, 1, p), lam_im.reshape(g, 1, p), log_dt.reshape(g, 1, 1), b_re, b_im,
      c_re, c_im, d_skip.reshape(g, 1, h))


PHASES = 8


def _ssm_proj_kernel(x_ref, mod_ref, ng_ref, w_ref, u_ref, z_ref, rows_scr, *, n_b):
    d = D_MODEL
    for jj in range(PHASES):
        hs = []
        for b in range(n_b):
            mod = mod_ref[b]
            shift, scale = mod[:, :d], mod[:, d:2 * d]
            rows_scr[b] = x_ref[b, :, jj, :]
            hs.append((_rms(rows_scr[b], ng_ref[...] * (1.0 + scale)) + shift).astype(BF16))
        h = jnp.concatenate(hs, axis=0)
        r = lax.dot_general(w_ref[...], h, _NT, preferred_element_type=F32)
        u_ref[jj] = r[:SSM_WIDTH].astype(BF16)
        z_ref[jj] = _silu(r[SSM_WIDTH:]).astype(BF16)


def _ssm_projection(x4, mod3, norm_g, w_in):
    n_b, c2, _, d = x4.shape
    lanes = n_b * c2
    halves = CHUNK // PHASES
    w_t = w_in[:, :2 * SSM_WIDTH].T.astype(BF16)
    out = jax.ShapeDtypeStruct((CHUNK, SSM_WIDTH, SUPER * lanes), BF16)
    out_spec = pl.BlockSpec((PHASES, SSM_WIDTH, lanes), lambda j2, hf: (hf, 0, j2))
    return pl.pallas_call(
        functools.partial(_ssm_proj_kernel, n_b=n_b),
        out_shape=(out, out),
        grid=(SUPER, halves),
        in_specs=[pl.BlockSpec((n_b, c2, PHASES, d), lambda j2, hf: (0, 0, j2 * halves + hf, 0)),
                  pl.BlockSpec((n_b, 1, 3 * d), lambda j2, hf: (0, 0, 0)),
                  pl.BlockSpec((1, d), lambda j2, hf: (0, 0)),
                  pl.BlockSpec(w_t.shape, lambda j2, hf: (0, 0))],
        out_specs=(out_spec, out_spec),
        scratch_shapes=[pltpu.VMEM((n_b, c2, d), F32)],
        compiler_params=pltpu.CompilerParams(dimension_semantics=("arbitrary", "arbitrary"),
                                             vmem_limit_bytes=56 * 1024 * 1024),
        name="ssm_projection",
    )(x4, mod3, norm_g.reshape(1, d), w_t)


def _cmul(ar, ai, xr, xi):
    return ar * xr - ai * xi, ar * xi + ai * xr


SCAN_GROUPS = 8


def _s5_group_kernel(x_ref, lhs_ref, wout_ref, ap_ref, y_ref, *, n_b, c2, n_top):
    p, gb = SSM_STATE, SCAN_GROUPS
    rows = CHUNK * SSM_GROUP
    n = SUPER * n_b * c2
    r = [jnp.dot(lhs_ref[gi], x_ref[:, SSM_GROUP * gi:SSM_GROUP * (gi + 1), :].reshape(rows, n),
                 preferred_element_type=F32) for gi in range(gb)]
    c_re = jnp.concatenate([r[gi][rows:rows + p] for gi in range(gb)], axis=0)
    c_im = jnp.concatenate([r[gi][rows + p:] for gi in range(gb)], axis=0)

    def mult(i):
        rep = [jnp.broadcast_to(ap_ref[gi, i:i + 1, :], (2 * p, 2 * p)).T for gi in range(gb)]
        return (jnp.concatenate([m[:p, :c2] for m in rep], axis=0),
                jnp.concatenate([m[p:, :c2] for m in rep], axis=0))

    lane = lax.broadcasted_iota(jnp.int32, (gb * p, c2), 1)

    def shifted(a, sh):
        return jnp.where(lane >= sh, pltpu.roll(a, sh, 1), 0.0)

    pieces = [[None] * n_b for _ in range(SUPER)]
    for b in range(n_b):
        piece = lambda a, j2: a[:, (j2 * n_b + b) * c2:(j2 * n_b + b + 1) * c2]
        a_re, a_im = mult(1)
        e_re = jnp.zeros((gb * p, c2), F32)
        e_im = jnp.zeros((gb * p, c2), F32)
        local = []
        for j2 in range(SUPER):
            local.append((e_re, e_im))
            t_re, t_im = _cmul(a_re, a_im, e_re, e_im)
            e_re, e_im = t_re + piece(c_re, j2), t_im + piece(c_im, j2)
        for i in range(n_top):
            m_re, m_im = mult(SUPER + i)
            t_re, t_im = _cmul(m_re, m_im, shifted(e_re, 1 << i), shifted(e_im, 1 << i))
            e_re, e_im = e_re + t_re, e_im + t_im
        s_re, s_im = shifted(e_re, 1), shifted(e_im, 1)
        for j2 in range(SUPER):
            if j2 == 0:
                pieces[j2][b] = (s_re, s_im)
            else:
                m_re, m_im = mult(j2)
                t_re, t_im = _cmul(m_re, m_im, s_re, s_im)
                pieces[j2][b] = (local[j2][0] + t_re, local[j2][1] + t_im)
    for gi in range(gb):
        sl = slice(gi * p, (gi + 1) * p)
        s_in = jnp.concatenate([jnp.concatenate([pieces[j2][b][0][sl], pieces[j2][b][1][sl]], axis=0)
                                for j2 in range(SUPER) for b in range(n_b)], axis=1)
        y = r[gi][:rows] + jnp.dot(wout_ref[gi], s_in.astype(BF16), preferred_element_type=F32)
        y_ref[gi] = y.reshape(CHUNK, SSM_GROUP, n).astype(y_ref.dtype)


def _s5_scan(u_t, lhs, wout, ap, n_b, c2, n_top):
    g, h = SSM_GROUPS, SSM_GROUP
    n = u_t.shape[-1]
    gb = SCAN_GROUPS
    grp = lambda *shape: pl.BlockSpec((gb,) + shape, lambda i: (i,) + (0,) * len(shape))
    return pl.pallas_call(
        functools.partial(_s5_group_kernel, n_b=n_b, c2=c2, n_top=n_top),
        out_shape=jax.ShapeDtypeStruct((g, CHUNK, h, n), BF16),
        grid=(g // gb,),
        in_specs=[pl.BlockSpec((CHUNK, gb * h, n), lambda i: (0, i, 0)),
                  grp(*lhs.shape[1:]), grp(*wout.shape[1:]), grp(*ap.shape[1:])],
        out_specs=grp(CHUNK, h, n),
        compiler_params=pltpu.CompilerParams(dimension_semantics=("arbitrary",),
                                             vmem_limit_bytes=48 * 1024 * 1024),
        name="s5_chunk_scan",
    )(u_t, lhs, wout, ap)


def _glu_kernel(y_ref, zs_ref, w_ref, b_ref, o_ref, *, n_b, c2):
    g, _, h, n = y_ref.shape
    for jj in range(PHASES):
        y = jax.nn.gelu(y_ref[:, jj].reshape(g * h, n).astype(F32))
        t = jnp.dot(w_ref[...], y.astype(BF16), preferred_element_type=F32) + b_ref[...]
        gated = (y * jax.nn.sigmoid(t) * zs_ref[jj].astype(F32)).T
        for b in range(n_b):
            o_ref[b, :, jj, :] = gated[b * c2:(b + 1) * c2]


def _glu(y_t, zs_t, w_glu, b_glu, n_b, c2):
    g, _, h, _ = y_t.shape
    lanes = n_b * c2
    halves = CHUNK // PHASES
    w_t = w_glu.T.astype(BF16)
    return pl.pallas_call(
        functools.partial(_glu_kernel, n_b=n_b, c2=c2),
        out_shape=jax.ShapeDtypeStruct((n_b, c2, SUPER * CHUNK, SSM_WIDTH), F32),
        grid=(SUPER, halves),
        in_specs=[pl.BlockSpec((g, PHASES, h, lanes), lambda j2, hf: (0, hf, 0, j2)),
                  pl.BlockSpec((PHASES, SSM_WIDTH, lanes), lambda j2, hf: (hf, 0, j2)),
                  pl.BlockSpec(w_t.shape, lambda j2, hf: (0, 0)),
                  pl.BlockSpec((SSM_WIDTH, 1), lambda j2, hf: (0, 0))],
        out_specs=pl.BlockSpec((n_b, c2, PHASES, SSM_WIDTH), lambda j2, hf: (0, 0, j2 * halves + hf, 0)),
        compiler_params=pltpu.CompilerParams(dimension_semantics=("arbitrary", "arbitrary"),
                                             vmem_limit_bytes=48 * 1024 * 1024),
        name="s5_glu",
    )(y_t, zs_t, w_t, b_glu.reshape(SSM_WIDTH, 1))


def _out_proj_kernel(ys_ref, ym_ref, x_ref, mod_ref, wa_ref, wb_ref, o_ref):
    y = (jnp.dot(ys_ref[...].astype(BF16), wa_ref[...], preferred_element_type=F32)
         + jnp.dot(ym_ref[...], wb_ref[...], preferred_element_type=F32))
    gate = mod_ref[...][:, 2 * D_MODEL:]
    o_ref[...] = x_ref[...] + gate * y


def _out_projection(ys, ym, x2, mod3, w_out, seq):
    t, d = x2.shape
    tm = min(1024, seq)
    per_b = seq // tm
    wa = w_out[:SSM_WIDTH].astype(BF16)
    wb = w_out[SSM_WIDTH:].astype(BF16)
    tok = lambda w: pl.BlockSpec((tm, w), lambda i: (i, 0))
    return pl.pallas_call(
        _out_proj_kernel,
        out_shape=jax.ShapeDtypeStruct(x2.shape, F32),
        grid=(t // tm,),
        in_specs=[tok(SSM_WIDTH), tok(MLA_WIDTH), tok(d),
                  pl.BlockSpec((None, 1, 3 * d), lambda i: (i // per_b, 0, 0)),
                  pl.BlockSpec(wa.shape, lambda i: (0, 0)),
                  pl.BlockSpec(wb.shape, lambda i: (0, 0))],
        out_specs=tok(d),
        compiler_params=pltpu.CompilerParams(dimension_semantics=("arbitrary",),
                                             vmem_limit_bytes=48 * 1024 * 1024),
        name="output_projection",
    )(ys, ym, x2, mod3, wa, wb)


def kernel(x, c, positions, w_ada, b_ada, norm_g, w_in, log_dt, lam_re, lam_im, b_re, b_im, c_re, c_im, d_skip,
           w_glu, b_glu, q_a_g, w_q_b, kv_a_g, w_kv_b, q_norm_g, k_norm_g, w_out):
    n_b, seq, d = x.shape
    depth = w_ada.shape[0]
    c2 = seq // (CHUNK * SUPER)
    n_top = max(int(math.log2(c2)), 0)
    assert c2 * CHUNK * SUPER == seq and (1 << n_top) == c2
    rope = _rope_tables(positions)
    for l in range(depth):
        mod3 = _modulation(c, w_ada[l], b_ada[l]).reshape(n_b, 1, 3 * d)
        x2 = x.reshape(n_b * seq, d)
        lhs, wout, ap = _s5_coefficients(log_dt[l], lam_re[l], lam_im[l], b_re[l], b_im[l], c_re[l], c_im[l],
                                         d_skip[l], n_top)
        u_t, zs_t = _ssm_projection(x.reshape(n_b, c2, SUPER * CHUNK, d), mod3, norm_g[l], w_in[l])
        y_t = _s5_scan(u_t, lhs, wout, ap, n_b, c2, n_top)
        ys = _glu(y_t, zs_t, w_glu[l], b_glu[l], n_b, c2).reshape(n_b * seq, SSM_WIDTH)
        qt, k, vt, zm = _mla_projection(x2, mod3, norm_g[l], w_in[l], q_a_g[l], w_q_b[l],
                                        kv_a_g[l], w_kv_b[l], q_norm_g[l], k_norm_g[l], rope, n_b, seq)
        score_bound = (1.05 * math.sqrt(QK_HEAD) * math.log2(math.e)
                       * jnp.max(jnp.abs(q_norm_g[l])) * jnp.max(jnp.abs(k_norm_g[l])))
        ym = _attention(qt, k, vt, zm, score_bound)
        x = _out_projection(ys, ym, x2, mod3, w_out[l], seq).reshape(n_b, seq, d)
    return x
```

```python
import functools
import math

import numpy as np
import jax
import jax.numpy as jnp
from jax import lax
from jax.experimental import pallas as pl
from jax.experimental.pallas import tpu as pltpu

F32 = jnp.float32
BF16 = jnp.bfloat16
HIGHEST = lax.Precision.HIGHEST

D_MODEL = 1024
SSM_WIDTH = 512
SSM_GROUP = 16
SSM_GROUPS = 32
SSM_STATE = 64
MLA_HEADS = 8
QK_NOPE = 64
QK_ROPE = 32
QK_HEAD = QK_NOPE + QK_ROPE
V_HEAD = 64
MLA_WIDTH = MLA_HEADS * V_HEAD
Q_LORA = 384
KV_LORA = 256
ROPE_THETA = 10000.0
EPS = 1e-6
NEG_BIG = -1e30

LANES = 128
CHUNK = 16
SUPER = 4
ROPE_HALF = QK_ROPE // 2

_NT = (((1,), (1,)), ((), ()))
_TN = (((0,), (0,)), ((), ()))


def _silu(v):
    return v * jax.nn.sigmoid(v)


def _rms(v, gain):
    return v * lax.rsqrt(jnp.mean(v * v, axis=-1, keepdims=True) + EPS) * gain


def _mod_kernel(c_ref, w_ref, b_ref, o_ref):
    act = _silu(c_ref[...])
    o_ref[...] = jnp.dot(act, w_ref[...], preferred_element_type=F32, precision=HIGHEST) + b_ref[...]


def _modulation(c, w, b):
    n_b, d = c.shape
    rows = 8
    c_pad = jnp.zeros((rows, d), F32).at[:n_b].set(c)
    tn = 512
    out = pl.pallas_call(
        _mod_kernel,
        out_shape=jax.ShapeDtypeStruct((rows, 3 * d), F32),
        grid=(3 * d // tn,),
        in_specs=[pl.BlockSpec((rows, d), lambda n: (0, 0)),
                  pl.BlockSpec((d, tn), lambda n: (0, n)),
                  pl.BlockSpec((1, tn), lambda n: (0, n))],
        out_specs=pl.BlockSpec((rows, tn), lambda n: (0, n)),
        name="adaln_modulation",
    )(c_pad, w, b.reshape(1, -1))
    return out[:n_b]


def _rope_kernel(pos_ref, freq_ref, cos_ref, sin_ref):
    ang = freq_ref[...] * pos_ref[...].astype(F32)
    cos_ref[...] = jnp.cos(ang)
    sin_ref[...] = jnp.sin(ang)


def _rope_tables(positions):
    t = positions.size
    tt = min(t, 2048)
    inv_freq = (ROPE_THETA ** (-np.arange(ROPE_HALF, dtype=np.float64) * 2.0 / QK_ROPE)).astype(np.float32)
    return pl.pallas_call(
        _rope_kernel,
        out_shape=(jax.ShapeDtypeStruct((ROPE_HALF, t), F32),) * 2,
        grid=(t // tt,),
        in_specs=[pl.BlockSpec((1, tt), lambda i: (0, i)),
                  pl.BlockSpec((ROPE_HALF, 1), lambda i: (0, 0))],
        out_specs=(pl.BlockSpec((ROPE_HALF, tt), lambda i: (0, i)),) * 2,
        name="rope_tables",
    )(positions.reshape(1, t), jnp.asarray(inv_freq).reshape(ROPE_HALF, 1))


def _slab_source():
    src = np.full((LANES,), -1, np.int64)
    src[0:16] = QK_NOPE + np.arange(16)
    src[16:64] = np.arange(48)
    src[64:80] = QK_NOPE + 16 + np.arange(16)
    src[80:96] = 48 + np.arange(16)
    return src


def _to_slabs(w, per_head, src):
    k = w.shape[0]
    w3 = w.reshape(k, MLA_HEADS, per_head)
    valid = src >= 0
    cols = jnp.where(valid[None, None, :], w3[:, :, np.clip(src, 0, per_head - 1)], 0.0)
    return cols.reshape(k, MLA_HEADS * LANES)


V_ROWS = V_HEAD + 16


def _mla_proj_kernel(x_ref, mod_ref, ng_ref, w1_ref, qag_ref, kvag_ref, wqt_ref, wkt_ref, wvt_ref,
                     gq_ref, gk_ref, cost_ref, sint_ref, qt_ref, k_ref, vt_ref, zs_ref, h_ref):
    d = D_MODEL
    tm = x_ref.shape[0]
    x = x_ref[...]
    mod = mod_ref[...]
    shift, scale = mod[:, :d], mod[:, d:2 * d]
    h_ref[...] = _rms(x, ng_ref[...] * (1.0 + scale)) + shift
    h = h_ref[...].astype(BF16)
    p1 = jnp.dot(h, w1_ref[...], preferred_element_type=F32)
    o1, o2, o3 = Q_LORA, Q_LORA + KV_LORA, Q_LORA + KV_LORA + MLA_WIDTH
    cqn = _rms(p1[:, :o1], qag_ref[...]).astype(BF16)
    ckvn = _rms(p1[:, o1:o2], kvag_ref[...]).astype(BF16)
    zs_ref[...] = _silu(p1[:, o2:o3]).astype(BF16)

    qt = lax.dot_general(wqt_ref[...], cqn, _NT, preferred_element_type=F32)
    kt = lax.dot_general(wkt_ref[...], ckvn, _NT, preferred_element_type=F32)
    krt = p1[:, o3:].T
    vt = lax.dot_general(wvt_ref[...], ckvn, _NT, preferred_element_type=F32)
    cos_t, sin_t = cost_ref[...], sint_ref[...]
    gq = jnp.concatenate([gq_ref[...]] * (tm // LANES), axis=1)
    gk = jnp.concatenate([gk_ref[...]] * (tm // LANES), axis=1)
    first = lax.broadcasted_iota(jnp.int32, (V_ROWS - V_HEAD, tm), 0) == 0
    ones_rows = jnp.where(first, 1.0, 0.0).astype(BF16)
    r, half = ROPE_HALF, LANES // 2

    def norm_rope(slab, gain):
        ss = jnp.sum(slab * slab, axis=0, keepdims=True)
        n = slab * lax.rsqrt(ss * (1.0 / QK_HEAD) + EPS) * gain
        x1, x2 = n[0:r], n[half:half + r]
        return jnp.concatenate([x1 * cos_t - x2 * sin_t, n[r:half], x2 * cos_t + x1 * sin_t, n[half + r:]], axis=0)

    for head in range(MLA_HEADS):
        rows = slice(LANES * head, LANES * (head + 1))
        qt_ref[head] = norm_rope(qt[rows], gq).astype(BF16)
        k_ref[head] = norm_rope(kt[rows] + krt, gk).T.astype(BF16)
        vt_ref[head, 0:V_HEAD, :] = vt[V_HEAD * head:V_HEAD * (head + 1)].astype(BF16)
        vt_ref[head, V_HEAD:, :] = ones_rows


def _mla_projection(x2, mod3, norm_g, w_in, q_a_g, w_q_b, kv_a_g, w_kv_b, q_norm_g, k_norm_g, rope, n_b, seq):
    t, d = x2.shape
    cos_t, sin_t = rope
    tm = min(512, seq)
    per_b = seq // tm
    o2 = 2 * SSM_WIDTH
    o4, o5 = o2 + Q_LORA + KV_LORA, o2 + Q_LORA + KV_LORA + QK_ROPE
    src = _slab_source()
    valid = src >= 0
    rope_src = np.where(src >= QK_NOPE, src - QK_NOPE, -1)
    nope_src = np.where(valid & (src < QK_NOPE), src, -1)
    wkr = jnp.where((rope_src >= 0)[None, :], w_in[:, o4:o5][:, np.clip(rope_src, 0, QK_ROPE - 1)], 0.0)
    w1 = jnp.concatenate([w_in[:, o2:o4], w_in[:, o5:], wkr], axis=1).astype(BF16)
    wqt = _to_slabs(w_q_b, QK_HEAD, src).T.astype(BF16)
    wkt = _to_slabs(w_kv_b, QK_NOPE + V_HEAD, nope_src).T.astype(BF16)
    wvt = w_kv_b.reshape(KV_LORA, MLA_HEADS, QK_NOPE + V_HEAD)[:, :, QK_NOPE:].reshape(KV_LORA, MLA_WIDTH).T.astype(BF16)
    q_scale = math.log2(math.e) / math.sqrt(QK_HEAD)
    slab_gain = lambda g: jnp.broadcast_to(jnp.where(valid, g[np.clip(src, 0, QK_HEAD - 1)], 0.0)[:, None],
                                           (LANES, LANES))
    gq = slab_gain(q_norm_g * q_scale)
    gk = slab_gain(k_norm_g)
    const = lambda shape: pl.BlockSpec(shape, lambda i: (0,) * len(shape))
    tok = lambda i: (i // per_b, 0, 0, i % per_b)
    return pl.pallas_call(
        _mla_proj_kernel,
        out_shape=(jax.ShapeDtypeStruct((n_b, MLA_HEADS, LANES, seq), BF16),
                   jax.ShapeDtypeStruct((n_b, MLA_HEADS, seq, LANES), BF16),
                   jax.ShapeDtypeStruct((n_b, MLA_HEADS, V_ROWS, seq), BF16),
                   jax.ShapeDtypeStruct((t, MLA_WIDTH), BF16),
                   jax.ShapeDtypeStruct((t, d), F32)),
        grid=(t // tm,),
        in_specs=[pl.BlockSpec((tm, d), lambda i: (i, 0)),
                  pl.BlockSpec((None, 1, 3 * d), lambda i: (i // per_b, 0, 0)),
                  const((1, d)), const(w1.shape), const((1, Q_LORA)), const((1, KV_LORA)),
                  const(wqt.shape), const(wkt.shape), const(wvt.shape),
                  const((LANES, LANES)), const((LANES, LANES)),
                  pl.BlockSpec((ROPE_HALF, tm), lambda i: (0, i)),
                  pl.BlockSpec((ROPE_HALF, tm), lambda i: (0, i))],
        out_specs=(pl.BlockSpec((None, MLA_HEADS, LANES, tm), tok),
                   pl.BlockSpec((None, MLA_HEADS, tm, LANES), lambda i: (i // per_b, 0, i % per_b, 0)),
                   pl.BlockSpec((None, MLA_HEADS, V_ROWS, tm), tok),
                   pl.BlockSpec((tm, MLA_WIDTH), lambda i: (i, 0)),
                   pl.BlockSpec((tm, d), lambda i: (i, 0))),
        compiler_params=pltpu.CompilerParams(dimension_semantics=("arbitrary",),
                                             vmem_limit_bytes=48 * 1024 * 1024),
        name="mla_projection",
    )(x2, mod3, norm_g.reshape(1, d), w1, q_a_g.reshape(1, -1), kv_a_g.reshape(1, -1), wqt, wkt, wvt, gq, gk,
      cos_t, sin_t)


ATTN_TQ = 2048
ATTN_TK = 256
HEADS_PER_STEP = 2


def _attn_kernel(qt_ref, k_ref, vt_ref, zs_ref, o_ref, s_scr):
    tq, tk = ATTN_TQ, ATTN_TK
    r = tq // tk
    assert r * tk == tq and r % 2 == 0
    qi = pl.program_id(2)

    def scores(slot, t, lo=0):
        start = pl.multiple_of(t * tk, tk)
        for e in range(HEADS_PER_STEP):
            s_scr[slot, e, :, lo:] = jnp.dot(k_ref[e, pl.ds(start, tk), :], qt_ref[e, :, lo:],
                                             preferred_element_type=F32)

    def process(slot, t, carry, lo=0, diagonal=False):
        start = pl.multiple_of(t * tk, tk)
        out = []
        for e in range(HEADS_PER_STEP):
            m, acc = carry[e]
            s = s_scr[slot, e, :, lo:]
            if diagonal:
                blk = s[:, :tk]
                ok = lax.broadcasted_iota(jnp.int32, blk.shape, 0) <= lax.broadcasted_iota(jnp.int32, blk.shape, 1)
                blk = jnp.where(ok, blk, NEG_BIG)
                s = jnp.concatenate([blk, s[:, tk:]], axis=1) if s.shape[1] > tk else blk
            m_old = m[:, lo:]
            m_new = jnp.maximum(m_old, jnp.max(s, axis=0, keepdims=True))
            p = jnp.exp2(s - m_new).astype(BF16)
            acc_new = (jnp.exp2(m_old - m_new) * acc[:, lo:]
                       + jnp.dot(vt_ref[e, :, pl.ds(start, tk)], p, preferred_element_type=F32))
            if lo:
                m_new = jnp.concatenate([m[:, :lo], m_new], axis=1)
                acc_new = jnp.concatenate([acc[:, :lo], acc_new], axis=1)
            out.append((m_new, acc_new))
        return tuple(out)

    def body(i, carry):
        base = i * r
        for u in range(r):
            scores((u + 1) % 2, base + u + 1)
            carry = process(u % 2, base + u, carry)
        return carry

    init = tuple((jnp.full((1, tq), NEG_BIG, F32), jnp.zeros((V_ROWS, tq), F32)) for _ in range(HEADS_PER_STEP))
    scores(0, 0)
    carry = lax.fori_loop(0, qi, body, init)
    base = qi * r
    for u in range(r):
        if u + 1 < r:
            scores((u + 1) % 2, base + u + 1, lo=(u + 1) * tk)
        carry = process(u % 2, base + u, carry, lo=u * tk, diagonal=True)
    _attn_finish(tuple(c[1] for c in carry), zs_ref, o_ref)


def _attn_finish(accs, zs_ref, o_ref):
    outs = [(acc[:V_HEAD] / acc[V_HEAD:V_HEAD + 1]).T for acc in accs]
    o_ref[...] = (jnp.concatenate(outs, axis=1) * zs_ref[...].astype(F32)).astype(o_ref.dtype)


def _attn_bounded_kernel(qt_ref, k_ref, vt_ref, zs_ref, o_ref, p_scr):
    tq, tk = ATTN_TQ, ATTN_TK
    r = tq // tk
    assert r * tk == tq and r % 2 == 0
    qi = pl.program_id(2)
    q0 = qi * tq

    def probs(slot, t, lo=0, may_cross=False):
        start = pl.multiple_of(t * tk, tk)
        for e in range(HEADS_PER_STEP):
            p = jnp.exp2(jnp.dot(k_ref[e, pl.ds(start, tk), :], qt_ref[e, :, lo:], preferred_element_type=F32))
            if may_cross:
                blk = p[:, :tk]
                key = start + lax.broadcasted_iota(jnp.int32, blk.shape, 0)
                qry = q0 + lo + lax.broadcasted_iota(jnp.int32, blk.shape, 1)
                blk = jnp.where(key <= qry, blk, 0.0)
                p = jnp.concatenate([blk, p[:, tk:]], axis=1) if p.shape[1] > tk else blk
            p_scr[slot, e, :, lo:] = p.astype(BF16)

    def accumulate(slot, t, accs, lo=0):
        start = pl.multiple_of(t * tk, tk)
        out = []
        for e in range(HEADS_PER_STEP):
            pv = jnp.dot(vt_ref[e, :, pl.ds(start, tk)], p_scr[slot, e, :, lo:], preferred_element_type=F32)
            acc = accs[e]
            out.append(acc + pv if lo == 0 else jnp.concatenate([acc[:, :lo], acc[:, lo:] + pv], axis=1))
        return tuple(out)

    def body(i, accs):
        base = i * r
        for u in range(r):
            probs((u + 1) % 2, base + u + 1, may_cross=(u == r - 1))
            accs = accumulate(u % 2, base + u, accs)
        return accs

    probs(0, 0, may_cross=True)
    accs = lax.fori_loop(0, qi, body, tuple(jnp.zeros((V_ROWS, tq), F32) for _ in range(HEADS_PER_STEP)))
    base = qi * r
    for u in range(r):
        if u + 1 < r:
            probs((u + 1) % 2, base + u + 1, lo=(u + 1) * tk, may_cross=True)
        accs = accumulate(u % 2, base + u, accs, lo=u * tk)
    _attn_finish(accs, zs_ref, o_ref)


SCORE_BOUND_LOG2 = 60.0


def _attention(qt, k, vt, zs, score_bound):
    n_b, n_h, seq, _ = k.shape
    tq = ATTN_TQ
    assert seq % tq == 0 and n_h % HEADS_PER_STEP == 0
    nq = seq // tq
    hp = HEADS_PER_STEP
    width = hp * V_HEAD

    def call(body, scratch):
        return pl.pallas_call(
            body,
            out_shape=jax.ShapeDtypeStruct(zs.shape, BF16),
            grid=(n_b, n_h // hp, nq),
            in_specs=[pl.BlockSpec((None, hp, LANES, tq), lambda b, h, i: (b, h, 0, i)),
                      pl.BlockSpec((None, hp, seq, LANES), lambda b, h, i: (b, h, 0, 0)),
                      pl.BlockSpec((None, hp, V_ROWS, seq), lambda b, h, i: (b, h, 0, 0)),
                      pl.BlockSpec((tq, width), lambda b, h, i: (b * nq + i, h))],
            out_specs=pl.BlockSpec((tq, width), lambda b, h, i: (b * nq + i, h)),
            scratch_shapes=scratch,
            compiler_params=pltpu.CompilerParams(dimension_semantics=("arbitrary", "arbitrary", "arbitrary"),
                                                 vmem_limit_bytes=48 * 1024 * 1024),
            name="causal_attention",
        )(qt, k, vt, zs)

    tiles = (2, hp, ATTN_TK, tq)
    return lax.cond(score_bound <= SCORE_BOUND_LOG2,
                    lambda: call(_attn_bounded_kernel, [pltpu.VMEM(tiles, BF16)]),
                    lambda: call(_attn_kernel, [pltpu.VMEM(tiles, F32)]))


def _s5_scan_steps(n_top):
    return [CHUNK * m for m in range(SUPER)] + [CHUNK * SUPER * (1 << i) for i in range(n_top)]


def _s5_exponents(n_top):
    return sorted(set(range(CHUNK + 1)) | set(_s5_scan_steps(n_top)))


def _s5_coef_kernel(kk_ref, *refs, n_top):
    for gi in range(SCAN_GROUPS):
        _s5_coef_one_group(kk_ref, *(r.at[gi] for r in refs), n_top=n_top)


def _s5_coef_one_group(kk_ref, lre_ref, lim_ref, ldt_ref, bre_ref, bim_ref, cre_ref, cim_ref, d_ref,
                       lhs_ref, wout_ref, ap_ref, *, n_top):
    h, p, n = SSM_GROUP, SSM_STATE, CHUNK
    lre, lim = lre_ref[...], lim_ref[...]
    dt = jnp.exp(ldt_ref[...])
    exps = _s5_exponents(n_top)
    kcol = kk_ref[...]
    mag = jnp.exp(kcol * (lre * dt))
    ang = kcol * (lim * dt)
    pow_re, pow_im = mag * jnp.cos(ang), mag * jnp.sin(ang)

    def power(kk):
        i = exps.index(kk)
        return pow_re[i:i + 1], pow_im[i:i + 1]

    lb_re, lb_im = power(1)
    nr, ni = lb_re - 1.0, lb_im
    den = lre * lre + lim * lim
    f_re = (nr * lre + ni * lim) / den
    f_im = (ni * lre - nr * lim) / den
    bre, bim = bre_ref[...].T, bim_ref[...].T
    bb_re = f_re * bre - f_im * bim
    bb_im = f_re * bim + f_im * bre
    cre, cim = cre_ref[...], cim_ref[...]
    cp_re, cp_im = [], []
    for kk in range(n + 1):
        pr, pi = power(kk)
        cp_re.append(cre * pr - cim * pi)
        cp_im.append(cre * pi + cim * pr)

    cpw = [jnp.concatenate([cp_re[kk], -cp_im[kk]], axis=1) for kk in range(n + 1)]
    bb_a = jnp.concatenate([bb_re, bb_im], axis=1)
    kw = lax.dot_general(jnp.concatenate(cpw[:n], axis=0), jnp.concatenate([bb_a] * n, axis=0), _NT,
                         preferred_element_type=F32, precision=HIGHEST)
    lane = lax.broadcasted_iota(jnp.int32, (h, n * h), 1)
    row = lax.broadcasted_iota(jnp.int32, (h, n * h), 0)
    col_blk = lax.shift_right_logical(lane, 4)
    d_tiled = jnp.concatenate([d_ref[...]] * n, axis=1)
    lag = [kw[h * kk:h * (kk + 1)] for kk in range(n)]
    lag[0] = lag[0] + jnp.where((lane & (h - 1)) == row, d_tiled, 0.0)
    rows = []
    for j in range(n):
        acc = jnp.zeros((h, n * h), F32)
        for jp in range(j + 1):
            acc = jnp.where(col_blk == jp, lag[j - jp], acc)
        rows.append(acc)
    lhs_ref[0:n * h, :] = jnp.concatenate(rows, axis=0).astype(lhs_ref.dtype)

    bb_b = jnp.concatenate([-bb_im, bb_re], axis=1)
    win = []
    for j in range(n):
        pr, pi = power(n - 1 - j)
        win.append(jnp.concatenate([pr, pr], axis=1) * bb_a + jnp.concatenate([pi, pi], axis=1) * bb_b)
    lhs_ref[n * h:, :] = jnp.concatenate(win, axis=0).T.astype(lhs_ref.dtype)
    wout_ref[...] = jnp.concatenate(cpw[1:], axis=0).astype(wout_ref.dtype)
    for i, kk in enumerate(_s5_scan_steps(n_top)):
        pr, pi = power(kk)
        ap_ref[i:i + 1, :] = jnp.concatenate([pr, pi], axis=1)


def _s5_coefficients(log_dt, lam_re, lam_im, b_re, b_im, c_re, c_im, d_skip, n_top):
    g, p, h = SSM_GROUPS, SSM_STATE, SSM_GROUP
    n_ap = SUPER + n_top
    gb = SCAN_GROUPS
    grp = lambda *shape: pl.BlockSpec((gb,) + shape, lambda i: (i,) + (0,) * len(shape))
    exps = np.asarray(_s5_exponents(n_top), np.float32)
    kk = np.zeros((-(-exps.size // 8) * 8, 1), np.float32)
    kk[:exps.size, 0] = exps
    return pl.pallas_call(
        functools.partial(_s5_coef_kernel, n_top=n_top),
        out_shape=(jax.ShapeDtypeStruct((g, CHUNK * h + 2 * p, CHUNK * h), BF16),
                   jax.ShapeDtypeStruct((g, CHUNK * h, 2 * p), BF16),
                   jax.ShapeDtypeStruct((g, n_ap, 2 * p), F32)),
        grid=(g // gb,),
        in_specs=[pl.BlockSpec(kk.shape, lambda i: (0, 0)),
                  grp(1, p), grp(1, p), grp(1, 1), grp(p, h), grp(p, h), grp(h, p), grp(h, p), grp(1, h)],
        out_specs=(grp(CHUNK * h + 2 * p, CHUNK * h), grp(CHUNK * h, 2 * p), grp(n_ap, 2 * p)),
        name="s5_coefficients",
    )(jnp.asarray(kk), lam_re.reshape(g, 1, p), lam_im.reshape(g, 1, p), log_dt.reshape(g, 1, 1), b_re, b_im,
      c_re, c_im, d_skip.reshape(g, 1, h))


PHASES = 8


def _ssm_proj_kernel(h_ref, w_ref, u_ref, z_ref, rows_scr, *, n_b):
    for jj in range(PHASES):
        for b in range(n_b):
            rows_scr[b] = h_ref[b, :, jj, :]
        h = jnp.concatenate([rows_scr[b].astype(BF16) for b in range(n_b)], axis=0)
        r = lax.dot_general(w_ref[...], h, _NT, preferred_element_type=F32)
        u_ref[jj] = r[:SSM_WIDTH].astype(BF16)
        z_ref[jj] = _silu(r[SSM_WIDTH:]).astype(BF16)


def _ssm_projection(h4, w_in):
    n_b, c2, _, d = h4.shape
    lanes = n_b * c2
    halves = CHUNK // PHASES
    w_t = w_in[:, :2 * SSM_WIDTH].T.astype(BF16)
    out = jax.ShapeDtypeStruct((CHUNK, SSM_WIDTH, SUPER * lanes), BF16)
    out_spec = pl.BlockSpec((PHASES, SSM_WIDTH, lanes), lambda j2, hf: (hf, 0, j2))
    return pl.pallas_call(
        functools.partial(_ssm_proj_kernel, n_b=n_b),
        out_shape=(out, out),
        grid=(SUPER, halves),
        in_specs=[pl.BlockSpec((n_b, c2, PHASES, d), lambda j2, hf: (0, 0, j2 * halves + hf, 0)),
                  pl.BlockSpec(w_t.shape, lambda j2, hf: (0, 0))],
        out_specs=(out_spec, out_spec),
        scratch_shapes=[pltpu.VMEM((n_b, c2, d), F32)],
        compiler_params=pltpu.CompilerParams(dimension_semantics=("arbitrary", "arbitrary"),
                                             vmem_limit_bytes=56 * 1024 * 1024),
        name="ssm_projection",
    )(h4, w_t)


def _cmul(ar, ai, xr, xi):
    return ar * xr - ai * xi, ar * xi + ai * xr


SCAN_GROUPS = 8


def _s5_group_kernel(x_ref, lhs_ref, wout_ref, ap_ref, y_ref, *, n_b, c2, n_top):
    p, gb = SSM_STATE, SCAN_GROUPS
    rows = CHUNK * SSM_GROUP
    n = SUPER * n_b * c2
    r = [jnp.dot(lhs_ref[gi], x_ref[:, SSM_GROUP * gi:SSM_GROUP * (gi + 1), :].reshape(rows, n),
                 preferred_element_type=F32) for gi in range(gb)]
    c_re = jnp.concatenate([r[gi][rows:rows + p] for gi in range(gb)], axis=0)
    c_im = jnp.concatenate([r[gi][rows + p:] for gi in range(gb)], axis=0)

    def mult(i):
        rep = [jnp.broadcast_to(ap_ref[gi, i:i + 1, :], (2 * p, 2 * p)).T for gi in range(gb)]
        return (jnp.concatenate([m[:p, :c2] for m in rep], axis=0),
                jnp.concatenate([m[p:, :c2] for m in rep], axis=0))

    lane = lax.broadcasted_iota(jnp.int32, (gb * p, c2), 1)

    def shifted(a, sh):
        return jnp.where(lane >= sh, pltpu.roll(a, sh, 1), 0.0)

    pieces = [[None] * n_b for _ in range(SUPER)]
    for b in range(n_b):
        piece = lambda a, j2: a[:, (j2 * n_b + b) * c2:(j2 * n_b + b + 1) * c2]
        a_re, a_im = mult(1)
        e_re = jnp.zeros((gb * p, c2), F32)
        e_im = jnp.zeros((gb * p, c2), F32)
        local = []
        for j2 in range(SUPER):
            local.append((e_re, e_im))
            t_re, t_im = _cmul(a_re, a_im, e_re, e_im)
            e_re, e_im = t_re + piece(c_re, j2), t_im + piece(c_im, j2)
        for i in range(n_top):
            m_re, m_im = mult(SUPER + i)
            t_re, t_im = _cmul(m_re, m_im, shifted(e_re, 1 << i), shifted(e_im, 1 << i))
            e_re, e_im = e_re + t_re, e_im + t_im
        s_re, s_im = shifted(e_re, 1), shifted(e_im, 1)
        for j2 in range(SUPER):
            if j2 == 0:
                pieces[j2][b] = (s_re, s_im)
            else:
                m_re, m_im = mult(j2)
                t_re, t_im = _cmul(m_re, m_im, s_re, s_im)
                pieces[j2][b] = (local[j2][0] + t_re, local[j2][1] + t_im)
    for gi in range(gb):
        sl = slice(gi * p, (gi + 1) * p)
        s_in = jnp.concatenate([jnp.concatenate([pieces[j2][b][0][sl], pieces[j2][b][1][sl]], axis=0)
                                for j2 in range(SUPER) for b in range(n_b)], axis=1)
        y = r[gi][:rows] + jnp.dot(wout_ref[gi], s_in.astype(BF16), preferred_element_type=F32)
        y_ref[gi] = y.reshape(CHUNK, SSM_GROUP, n).astype(y_ref.dtype)


def _s5_scan(u_t, lhs, wout, ap, n_b, c2, n_top):
    g, h = SSM_GROUPS, SSM_GROUP
    n = u_t.shape[-1]
    gb = SCAN_GROUPS
    grp = lambda *shape: pl.BlockSpec((gb,) + shape, lambda i: (i,) + (0,) * len(shape))
    return pl.pallas_call(
        functools.partial(_s5_group_kernel, n_b=n_b, c2=c2, n_top=n_top),
        out_shape=jax.ShapeDtypeStruct((g, CHUNK, h, n), BF16),
        grid=(g // gb,),
        in_specs=[pl.BlockSpec((CHUNK, gb * h, n), lambda i: (0, i, 0)),
                  grp(*lhs.shape[1:]), grp(*wout.shape[1:]), grp(*ap.shape[1:])],
        out_specs=grp(CHUNK, h, n),
        compiler_params=pltpu.CompilerParams(dimension_semantics=("arbitrary",),
                                             vmem_limit_bytes=48 * 1024 * 1024),
        name="s5_chunk_scan",
    )(u_t, lhs, wout, ap)


def _glu_kernel(y_ref, zs_ref, w_ref, b_ref, o_ref, *, n_b, c2):
    g, _, h, n = y_ref.shape
    for jj in range(PHASES):
        y = jax.nn.gelu(y_ref[:, jj].reshape(g * h, n).astype(F32))
        t = jnp.dot(w_ref[...], y.astype(BF16), preferred_element_type=F32) + b_ref[...]
        gated = (y * jax.nn.sigmoid(t) * zs_ref[jj].astype(F32)).T
        for b in range(n_b):
            o_ref[b, :, jj, :] = gated[b * c2:(b + 1) * c2]


def _glu(y_t, zs_t, w_glu, b_glu, n_b, c2):
    g, _, h, _ = y_t.shape
    lanes = n_b * c2
    halves = CHUNK // PHASES
    w_t = w_glu.T.astype(BF16)
    return pl.pallas_call(
        functools.partial(_glu_kernel, n_b=n_b, c2=c2),
        out_shape=jax.ShapeDtypeStruct((n_b, c2, SUPER * CHUNK, SSM_WIDTH), F32),
        grid=(SUPER, halves),
        in_specs=[pl.BlockSpec((g, PHASES, h, lanes), lambda j2, hf: (0, hf, 0, j2)),
                  pl.BlockSpec((PHASES, SSM_WIDTH, lanes), lambda j2, hf: (hf, 0, j2)),
                  pl.BlockSpec(w_t.shape, lambda j2, hf: (0, 0)),
                  pl.BlockSpec((SSM_WIDTH, 1), lambda j2, hf: (0, 0))],
        out_specs=pl.BlockSpec((n_b, c2, PHASES, SSM_WIDTH), lambda j2, hf: (0, 0, j2 * halves + hf, 0)),
        compiler_params=pltpu.CompilerParams(dimension_semantics=("arbitrary", "arbitrary"),
                                             vmem_limit_bytes=48 * 1024 * 1024),
        name="s5_glu",
    )(y_t, zs_t, w_t, b_glu.reshape(SSM_WIDTH, 1))


def _out_proj_kernel(ys_ref, ym_ref, x_ref, mod_ref, wa_ref, wb_ref, o_ref):
    y = (jnp.dot(ys_ref[...].astype(BF16), wa_ref[...], preferred_element_type=F32)
         + jnp.dot(ym_ref[...], wb_ref[...], preferred_element_type=F32))
    gate = mod_ref[...][:, 2 * D_MODEL:]
    o_ref[...] = x_ref[...] + gate * y


def _out_projection(ys, ym, x2, mod3, w_out, seq):
    t, d = x2.shape
    tm = min(1024, seq)
    per_b = seq // tm
    wa = w_out[:SSM_WIDTH].astype(BF16)
    wb = w_out[SSM_WIDTH:].astype(BF16)
    tok = lambda w: pl.BlockSpec((tm, w), lambda i: (i, 0))
    return pl.pallas_call(
        _out_proj_kernel,
        out_shape=jax.ShapeDtypeStruct(x2.shape, F32),
        grid=(t // tm,),
        in_specs=[tok(SSM_WIDTH), tok(MLA_WIDTH), tok(d),
                  pl.BlockSpec((None, 1, 3 * d), lambda i: (i // per_b, 0, 0)),
                  pl.BlockSpec(wa.shape, lambda i: (0, 0)),
                  pl.BlockSpec(wb.shape, lambda i: (0, 0))],
        out_specs=tok(d),
        compiler_params=pltpu.CompilerParams(dimension_semantics=("arbitrary",),
                                             vmem_limit_bytes=48 * 1024 * 1024),
        name="output_projection",
    )(ys, ym, x2, mod3, wa, wb)


def kernel(x, c, positions, w_ada, b_ada, norm_g, w_in, log_dt, lam_re, lam_im, b_re, b_im, c_re, c_im, d_skip,
           w_glu, b_glu, q_a_g, w_q_b, kv_a_g, w_kv_b, q_norm_g, k_norm_g, w_out):
    n_b, seq, d = x.shape
    depth = w_ada.shape[0]
    c2 = seq // (CHUNK * SUPER)
    n_top = max(int(math.log2(c2)), 0)
    assert c2 * CHUNK * SUPER == seq and (1 << n_top) == c2
    rope = _rope_tables(positions)
    for l in range(depth):
        mod3 = _modulation(c, w_ada[l], b_ada[l]).reshape(n_b, 1, 3 * d)
        x2 = x.reshape(n_b * seq, d)
        lhs, wout, ap = _s5_coefficients(log_dt[l], lam_re[l], lam_im[l], b_re[l], b_im[l], c_re[l], c_im[l],
                                         d_skip[l], n_top)
        qt, k, vt, zm, h = _mla_projection(x2, mod3, norm_g[l], w_in[l], q_a_g[l], w_q_b[l],
                                           kv_a_g[l], w_kv_b[l], q_norm_g[l], k_norm_g[l], rope, n_b, seq)
        u_t, zs_t = _ssm_projection(h.reshape(n_b, c2, SUPER * CHUNK, d), w_in[l])
        y_t = _s5_scan(u_t, lhs, wout, ap, n_b, c2, n_top)
        ys = _glu(y_t, zs_t, w_glu[l], b_glu[l], n_b, c2).reshape(n_b * seq, SSM_WIDTH)
        score_bound = (1.05 * math.sqrt(QK_HEAD) * math.log2(math.e)
                       * jnp.max(jnp.abs(q_norm_g[l])) * jnp.max(jnp.abs(k_norm_g[l])))
        ym = _attention(qt, k, vt, zm, score_bound)
        x = _out_projection(ys, ym, x2, mod3, w_out[l], seq).reshape(n_b, seq, d)
    return x
```

```python
import functools
import math

import numpy as np
import jax
import jax.numpy as jnp
from jax import lax
from jax.experimental import pallas as pl
from jax.experimental.pallas import tpu as pltpu

F32 = jnp.float32
BF16 = jnp.bfloat16
HIGHEST = lax.Precision.HIGHEST

D_MODEL = 1024
SSM_WIDTH = 512
SSM_GROUP = 16
SSM_GROUPS = 32
SSM_STATE = 64
MLA_HEADS = 8
QK_NOPE = 64
QK_ROPE = 32
QK_HEAD = QK_NOPE + QK_ROPE
V_HEAD = 64
MLA_WIDTH = MLA_HEADS * V_HEAD
Q_LORA = 384
KV_LORA = 256
ROPE_THETA = 10000.0
EPS = 1e-6
NEG_BIG = -1e30

LANES = 128
CHUNK = 16
SUPER = 4
ROPE_HALF = QK_ROPE // 2

_NT = (((1,), (1,)), ((), ()))
_TN = (((0,), (0,)), ((), ()))


def _silu(v):
    return v * jax.nn.sigmoid(v)


def _rms(v, gain):
    return v * lax.rsqrt(jnp.mean(v * v, axis=-1, keepdims=True) + EPS) * gain


def _mod_kernel(c_ref, w_ref, b_ref, o_ref):
    act = _silu(c_ref[...])
    o_ref[...] = jnp.dot(act, w_ref[...], preferred_element_type=F32, precision=HIGHEST) + b_ref[...]


def _modulation(c, w, b):
    n_b, d = c.shape
    rows = 8
    c_pad = jnp.zeros((rows, d), F32).at[:n_b].set(c)
    tn = 512
    out = pl.pallas_call(
        _mod_kernel,
        out_shape=jax.ShapeDtypeStruct((rows, 3 * d), F32),
        grid=(3 * d // tn,),
        in_specs=[pl.BlockSpec((rows, d), lambda n: (0, 0)),
                  pl.BlockSpec((d, tn), lambda n: (0, n)),
                  pl.BlockSpec((1, tn), lambda n: (0, n))],
        out_specs=pl.BlockSpec((rows, tn), lambda n: (0, n)),
        name="adaln_modulation",
    )(c_pad, w, b.reshape(1, -1))
    return out[:n_b]


def _rope_kernel(pos_ref, freq_ref, cos_ref, sin_ref):
    ang = freq_ref[...] * pos_ref[...].astype(F32)
    cos_ref[...] = jnp.cos(ang)
    sin_ref[...] = jnp.sin(ang)


def _rope_tables(positions):
    t = positions.size
    tt = min(t, 2048)
    inv_freq = (ROPE_THETA ** (-np.arange(ROPE_HALF, dtype=np.float64) * 2.0 / QK_ROPE)).astype(np.float32)
    return pl.pallas_call(
        _rope_kernel,
        out_shape=(jax.ShapeDtypeStruct((ROPE_HALF, t), F32),) * 2,
        grid=(t // tt,),
        in_specs=[pl.BlockSpec((1, tt), lambda i: (0, i)),
                  pl.BlockSpec((ROPE_HALF, 1), lambda i: (0, 0))],
        out_specs=(pl.BlockSpec((ROPE_HALF, tt), lambda i: (0, i)),) * 2,
        name="rope_tables",
    )(positions.reshape(1, t), jnp.asarray(inv_freq).reshape(ROPE_HALF, 1))


def _slab_source():
    src = np.full((LANES,), -1, np.int64)
    src[0:16] = QK_NOPE + np.arange(16)
    src[16:64] = np.arange(48)
    src[64:80] = QK_NOPE + 16 + np.arange(16)
    src[80:96] = 48 + np.arange(16)
    return src


def _to_slabs(w, per_head, src):
    k = w.shape[0]
    w3 = w.reshape(k, MLA_HEADS, per_head)
    valid = src >= 0
    cols = jnp.where(valid[None, None, :], w3[:, :, np.clip(src, 0, per_head - 1)], 0.0)
    return cols.reshape(k, MLA_HEADS * LANES)


V_ROWS = V_HEAD + 16


def _mla_proj_kernel(x_ref, mod_ref, ng_ref, w1_ref, wzt_ref, qag_ref, kvag_ref, wqt_ref, wkt_ref, wvt_ref,
                     gq_ref, gk_ref, cost_ref, sint_ref, qt_ref, k_ref, vt_ref, zs_ref, h_ref):
    d = D_MODEL
    tm = x_ref.shape[0]
    x = x_ref[...]
    mod = mod_ref[...]
    shift, scale = mod[:, :d], mod[:, d:2 * d]
    h_ref[...] = _rms(x, ng_ref[...] * (1.0 + scale)) + shift
    h = h_ref[...].astype(BF16)
    p1 = jnp.dot(h, w1_ref[...], preferred_element_type=F32)
    o1, o2 = Q_LORA, Q_LORA + KV_LORA
    cqn = _rms(p1[:, :o1], qag_ref[...]).astype(BF16)
    ckvn = _rms(p1[:, o1:o2], kvag_ref[...]).astype(BF16)
    zs_ref[...] = _silu(lax.dot_general(wzt_ref[...], h, _NT, preferred_element_type=F32)).astype(BF16)

    qt = lax.dot_general(wqt_ref[...], cqn, _NT, preferred_element_type=F32)
    kt = lax.dot_general(wkt_ref[...], ckvn, _NT, preferred_element_type=F32)
    krt = p1[:, o2:].T
    vt = lax.dot_general(wvt_ref[...], ckvn, _NT, preferred_element_type=F32)
    cos_t, sin_t = cost_ref[...], sint_ref[...]
    gq = jnp.concatenate([gq_ref[...]] * (tm // LANES), axis=1)
    gk = jnp.concatenate([gk_ref[...]] * (tm // LANES), axis=1)
    first = lax.broadcasted_iota(jnp.int32, (V_ROWS - V_HEAD, tm), 0) == 0
    ones_rows = jnp.where(first, 1.0, 0.0).astype(BF16)
    r, half = ROPE_HALF, LANES // 2

    def norm_rope(slab, gain):
        ss = jnp.sum(slab * slab, axis=0, keepdims=True)
        n = slab * lax.rsqrt(ss * (1.0 / QK_HEAD) + EPS) * gain
        x1, x2 = n[0:r], n[half:half + r]
        return jnp.concatenate([x1 * cos_t - x2 * sin_t, n[r:half], x2 * cos_t + x1 * sin_t, n[half + r:]], axis=0)

    for head in range(MLA_HEADS):
        rows = slice(LANES * head, LANES * (head + 1))
        qt_ref[head] = norm_rope(qt[rows], gq).astype(BF16)
        k_ref[head] = norm_rope(kt[rows] + krt, gk).T.astype(BF16)
        vt_ref[head, 0:V_HEAD, :] = vt[V_HEAD * head:V_HEAD * (head + 1)].astype(BF16)
        vt_ref[head, V_HEAD:, :] = ones_rows


def _mla_projection(x2, mod3, norm_g, w_in, q_a_g, w_q_b, kv_a_g, w_kv_b, q_norm_g, k_norm_g, rope, n_b, seq):
    t, d = x2.shape
    cos_t, sin_t = rope
    tm = min(512, seq)
    per_b = seq // tm
    o2 = 2 * SSM_WIDTH
    o4, o5 = o2 + Q_LORA + KV_LORA, o2 + Q_LORA + KV_LORA + QK_ROPE
    src = _slab_source()
    valid = src >= 0
    rope_src = np.where(src >= QK_NOPE, src - QK_NOPE, -1)
    nope_src = np.where(valid & (src < QK_NOPE), src, -1)
    wkr = jnp.where((rope_src >= 0)[None, :], w_in[:, o4:o5][:, np.clip(rope_src, 0, QK_ROPE - 1)], 0.0)
    w1 = jnp.concatenate([w_in[:, o2:o4], wkr], axis=1).astype(BF16)
    wzt = w_in[:, o5:].T.astype(BF16)
    wqt = _to_slabs(w_q_b, QK_HEAD, src).T.astype(BF16)
    wkt = _to_slabs(w_kv_b, QK_NOPE + V_HEAD, nope_src).T.astype(BF16)
    wvt = w_kv_b.reshape(KV_LORA, MLA_HEADS, QK_NOPE + V_HEAD)[:, :, QK_NOPE:].reshape(KV_LORA, MLA_WIDTH).T.astype(BF16)
    q_scale = math.log2(math.e) / math.sqrt(QK_HEAD)
    slab_gain = lambda g: jnp.broadcast_to(jnp.where(valid, g[np.clip(src, 0, QK_HEAD - 1)], 0.0)[:, None],
                                           (LANES, LANES))
    gq = slab_gain(q_norm_g * q_scale)
    gk = slab_gain(k_norm_g)
    const = lambda shape: pl.BlockSpec(shape, lambda i: (0,) * len(shape))
    tok = lambda i: (i // per_b, 0, 0, i % per_b)
    return pl.pallas_call(
        _mla_proj_kernel,
        out_shape=(jax.ShapeDtypeStruct((n_b, MLA_HEADS, LANES, seq), BF16),
                   jax.ShapeDtypeStruct((n_b, MLA_HEADS, seq, LANES), BF16),
                   jax.ShapeDtypeStruct((n_b, MLA_HEADS, V_ROWS, seq), BF16),
                   jax.ShapeDtypeStruct((MLA_WIDTH, t), BF16),
                   jax.ShapeDtypeStruct((t, d), F32)),
        grid=(t // tm,),
        in_specs=[pl.BlockSpec((tm, d), lambda i: (i, 0)),
                  pl.BlockSpec((None, 1, 3 * d), lambda i: (i // per_b, 0, 0)),
                  const((1, d)), const(w1.shape), const(wzt.shape), const((1, Q_LORA)), const((1, KV_LORA)),
                  const(wqt.shape), const(wkt.shape), const(wvt.shape),
                  const((LANES, LANES)), const((LANES, LANES)),
                  pl.BlockSpec((ROPE_HALF, tm), lambda i: (0, i)),
                  pl.BlockSpec((ROPE_HALF, tm), lambda i: (0, i))],
        out_specs=(pl.BlockSpec((None, MLA_HEADS, LANES, tm), tok),
                   pl.BlockSpec((None, MLA_HEADS, tm, LANES), lambda i: (i // per_b, 0, i % per_b, 0)),
                   pl.BlockSpec((None, MLA_HEADS, V_ROWS, tm), tok),
                   pl.BlockSpec((MLA_WIDTH, tm), lambda i: (0, i)),
                   pl.BlockSpec((tm, d), lambda i: (i, 0))),
        compiler_params=pltpu.CompilerParams(dimension_semantics=("arbitrary",),
                                             vmem_limit_bytes=48 * 1024 * 1024),
        name="mla_projection",
    )(x2, mod3, norm_g.reshape(1, d), w1, wzt, q_a_g.reshape(1, -1), kv_a_g.reshape(1, -1), wqt, wkt, wvt, gq, gk,
      cos_t, sin_t)


ATTN_TQ = 2048
ATTN_TK = 256
HEADS_PER_STEP = 2


def _attn_kernel(qt_ref, k_ref, vt_ref, zs_ref, o_ref, s_scr):
    tq, tk = ATTN_TQ, ATTN_TK
    r = tq // tk
    assert r * tk == tq and r % 2 == 0
    qi = pl.program_id(2)

    def scores(slot, t, lo=0):
        start = pl.multiple_of(t * tk, tk)
        for e in range(HEADS_PER_STEP):
            s_scr[slot, e, :, lo:] = jnp.dot(k_ref[e, pl.ds(start, tk), :], qt_ref[e, :, lo:],
                                             preferred_element_type=F32)

    def process(slot, t, carry, lo=0, diagonal=False):
        start = pl.multiple_of(t * tk, tk)
        out = []
        for e in range(HEADS_PER_STEP):
            m, acc = carry[e]
            s = s_scr[slot, e, :, lo:]
            if diagonal:
                blk = s[:, :tk]
                ok = lax.broadcasted_iota(jnp.int32, blk.shape, 0) <= lax.broadcasted_iota(jnp.int32, blk.shape, 1)
                blk = jnp.where(ok, blk, NEG_BIG)
                s = jnp.concatenate([blk, s[:, tk:]], axis=1) if s.shape[1] > tk else blk
            m_old = m[:, lo:]
            m_new = jnp.maximum(m_old, jnp.max(s, axis=0, keepdims=True))
            p = jnp.exp2(s - m_new).astype(BF16)
            acc_new = (jnp.exp2(m_old - m_new) * acc[:, lo:]
                       + jnp.dot(vt_ref[e, :, pl.ds(start, tk)], p, preferred_element_type=F32))
            if lo:
                m_new = jnp.concatenate([m[:, :lo], m_new], axis=1)
                acc_new = jnp.concatenate([acc[:, :lo], acc_new], axis=1)
            out.append((m_new, acc_new))
        return tuple(out)

    def body(i, carry):
        base = i * r
        for u in range(r):
            scores((u + 1) % 2, base + u + 1)
            carry = process(u % 2, base + u, carry)
        return carry

    init = tuple((jnp.full((1, tq), NEG_BIG, F32), jnp.zeros((V_ROWS, tq), F32)) for _ in range(HEADS_PER_STEP))
    scores(0, 0)
    carry = lax.fori_loop(0, qi, body, init)
    base = qi * r
    for u in range(r):
        if u + 1 < r:
            scores((u + 1) % 2, base + u + 1, lo=(u + 1) * tk)
        carry = process(u % 2, base + u, carry, lo=u * tk, diagonal=True)
    _attn_finish(tuple(c[1] for c in carry), zs_ref, o_ref)


def _attn_finish(accs, zs_ref, o_ref):
    outs = [acc[:V_HEAD] / acc[V_HEAD:V_HEAD + 1] for acc in accs]
    o_ref[...] = (jnp.concatenate(outs, axis=0) * zs_ref[...].astype(F32)).astype(o_ref.dtype)


def _attn_bounded_kernel(qt_ref, k_ref, vt_ref, zs_ref, o_ref, p_scr):
    tq, tk = ATTN_TQ, ATTN_TK
    r = tq // tk
    assert r * tk == tq and r % 2 == 0
    qi = pl.program_id(2)
    q0 = qi * tq

    def probs(slot, t, lo=0, may_cross=False):
        start = pl.multiple_of(t * tk, tk)
        for e in range(HEADS_PER_STEP):
            p = jnp.exp2(jnp.dot(k_ref[e, pl.ds(start, tk), :], qt_ref[e, :, lo:], preferred_element_type=F32))
            if may_cross:
                blk = p[:, :tk]
                key = start + lax.broadcasted_iota(jnp.int32, blk.shape, 0)
                qry = q0 + lo + lax.broadcasted_iota(jnp.int32, blk.shape, 1)
                blk = jnp.where(key <= qry, blk, 0.0)
                p = jnp.concatenate([blk, p[:, tk:]], axis=1) if p.shape[1] > tk else blk
            p_scr[slot, e, :, lo:] = p.astype(BF16)

    def accumulate(slot, t, accs, lo=0):
        start = pl.multiple_of(t * tk, tk)
        out = []
        for e in range(HEADS_PER_STEP):
            pv = jnp.dot(vt_ref[e, :, pl.ds(start, tk)], p_scr[slot, e, :, lo:], preferred_element_type=F32)
            acc = accs[e]
            out.append(acc + pv if lo == 0 else jnp.concatenate([acc[:, :lo], acc[:, lo:] + pv], axis=1))
        return tuple(out)

    def body(i, accs):
        base = i * r
        for u in range(r):
            probs((u + 1) % 2, base + u + 1, may_cross=(u == r - 1))
            accs = accumulate(u % 2, base + u, accs)
        return accs

    probs(0, 0, may_cross=True)
    accs = lax.fori_loop(0, qi, body, tuple(jnp.zeros((V_ROWS, tq), F32) for _ in range(HEADS_PER_STEP)))
    base = qi * r
    for u in range(r):
        if u + 1 < r:
            probs((u + 1) % 2, base + u + 1, lo=(u + 1) * tk, may_cross=True)
        accs = accumulate(u % 2, base + u, accs, lo=u * tk)
    _attn_finish(accs, zs_ref, o_ref)


SCORE_BOUND_LOG2 = 60.0


def _attention(qt, k, vt, zs, score_bound):
    n_b, n_h, seq, _ = k.shape
    tq = ATTN_TQ
    assert seq % tq == 0 and n_h % HEADS_PER_STEP == 0
    nq = seq // tq
    hp = HEADS_PER_STEP
    width = hp * V_HEAD

    def call(body, scratch):
        return pl.pallas_call(
            body,
            out_shape=jax.ShapeDtypeStruct(zs.shape, BF16),
            grid=(n_b, n_h // hp, nq),
            in_specs=[pl.BlockSpec((None, hp, LANES, tq), lambda b, h, i: (b, h, 0, i)),
                      pl.BlockSpec((None, hp, seq, LANES), lambda b, h, i: (b, h, 0, 0)),
                      pl.BlockSpec((None, hp, V_ROWS, seq), lambda b, h, i: (b, h, 0, 0)),
                      pl.BlockSpec((width, tq), lambda b, h, i: (h, b * nq + i))],
            out_specs=pl.BlockSpec((width, tq), lambda b, h, i: (h, b * nq + i)),
            scratch_shapes=scratch,
            compiler_params=pltpu.CompilerParams(dimension_semantics=("arbitrary", "arbitrary", "arbitrary"),
                                                 vmem_limit_bytes=48 * 1024 * 1024),
            name="causal_attention",
        )(qt, k, vt, zs)

    tiles = (2, hp, ATTN_TK, tq)
    return lax.cond(score_bound <= SCORE_BOUND_LOG2,
                    lambda: call(_attn_bounded_kernel, [pltpu.VMEM(tiles, BF16)]),
                    lambda: call(_attn_kernel, [pltpu.VMEM(tiles, F32)]))


def _s5_scan_steps(n_top):
    return [CHUNK * m for m in range(SUPER)] + [CHUNK * SUPER * (1 << i) for i in range(n_top)]


def _s5_exponents(n_top):
    return sorted(set(range(CHUNK + 1)) | set(_s5_scan_steps(n_top)))


def _s5_coef_kernel(kk_ref, *refs, n_top):
    for gi in range(SCAN_GROUPS):
        _s5_coef_one_group(kk_ref, *(r.at[gi] for r in refs), n_top=n_top)


def _s5_coef_one_group(kk_ref, lre_ref, lim_ref, ldt_ref, bre_ref, bim_ref, cre_ref, cim_ref, d_ref,
                       lhs_ref, wout_ref, ap_ref, *, n_top):
    h, p, n = SSM_GROUP, SSM_STATE, CHUNK
    lre, lim = lre_ref[...], lim_ref[...]
    dt = jnp.exp(ldt_ref[...])
    exps = _s5_exponents(n_top)
    kcol = kk_ref[...]
    mag = jnp.exp(kcol * (lre * dt))
    ang = kcol * (lim * dt)
    pow_re, pow_im = mag * jnp.cos(ang), mag * jnp.sin(ang)

    def power(kk):
        i = exps.index(kk)
        return pow_re[i:i + 1], pow_im[i:i + 1]

    lb_re, lb_im = power(1)
    nr, ni = lb_re - 1.0, lb_im
    den = lre * lre + lim * lim
    f_re = (nr * lre + ni * lim) / den
    f_im = (ni * lre - nr * lim) / den
    bre, bim = bre_ref[...].T, bim_ref[...].T
    bb_re = f_re * bre - f_im * bim
    bb_im = f_re * bim + f_im * bre
    cre, cim = cre_ref[...], cim_ref[...]
    cp_re, cp_im = [], []
    for kk in range(n + 1):
        pr, pi = power(kk)
        cp_re.append(cre * pr - cim * pi)
        cp_im.append(cre * pi + cim * pr)

    cpw = [jnp.concatenate([cp_re[kk], -cp_im[kk]], axis=1) for kk in range(n + 1)]
    bb_a = jnp.concatenate([bb_re, bb_im], axis=1)
    kw = lax.dot_general(jnp.concatenate(cpw[:n], axis=0), jnp.concatenate([bb_a] * n, axis=0), _NT,
                         preferred_element_type=F32, precision=HIGHEST)
    lane = lax.broadcasted_iota(jnp.int32, (h, n * h), 1)
    row = lax.broadcasted_iota(jnp.int32, (h, n * h), 0)
    col_blk = lax.shift_right_logical(lane, 4)
    d_tiled = jnp.concatenate([d_ref[...]] * n, axis=1)
    lag = [kw[h * kk:h * (kk + 1)] for kk in range(n)]
    lag[0] = lag[0] + jnp.where((lane & (h - 1)) == row, d_tiled, 0.0)
    rows = []
    for j in range(n):
        acc = jnp.zeros((h, n * h), F32)
        for jp in range(j + 1):
            acc = jnp.where(col_blk == jp, lag[j - jp], acc)
        rows.append(acc)
    lhs_ref[0:n * h, :] = jnp.concatenate(rows, axis=0).astype(lhs_ref.dtype)

    bb_b = jnp.concatenate([-bb_im, bb_re], axis=1)
    win = []
    for j in range(n):
        pr, pi = power(n - 1 - j)
        win.append(jnp.concatenate([pr, pr], axis=1) * bb_a + jnp.concatenate([pi, pi], axis=1) * bb_b)
    lhs_ref[n * h:, :] = jnp.concatenate(win, axis=0).T.astype(lhs_ref.dtype)
    wout_ref[...] = jnp.concatenate(cpw[1:], axis=0).astype(wout_ref.dtype)
    for i, kk in enumerate(_s5_scan_steps(n_top)):
        pr, pi = power(kk)
        ap_ref[i:i + 1, :] = jnp.concatenate([pr, pi], axis=1)


def _s5_coefficients(log_dt, lam_re, lam_im, b_re, b_im, c_re, c_im, d_skip, n_top):
    g, p, h = SSM_GROUPS, SSM_STATE, SSM_GROUP
    n_ap = SUPER + n_top
    gb = SCAN_GROUPS
    grp = lambda *shape: pl.BlockSpec((gb,) + shape, lambda i: (i,) + (0,) * len(shape))
    exps = np.asarray(_s5_exponents(n_top), np.float32)
    kk = np.zeros((-(-exps.size // 8) * 8, 1), np.float32)
    kk[:exps.size, 0] = exps
    return pl.pallas_call(
        functools.partial(_s5_coef_kernel, n_top=n_top),
        out_shape=(jax.ShapeDtypeStruct((g, CHUNK * h + 2 * p, CHUNK * h), BF16),
                   jax.ShapeDtypeStruct((g, CHUNK * h, 2 * p), BF16),
                   jax.ShapeDtypeStruct((g, n_ap, 2 * p), F32)),
        grid=(g // gb,),
        in_specs=[pl.BlockSpec(kk.shape, lambda i: (0, 0)),
                  grp(1, p), grp(1, p), grp(1, 1), grp(p, h), grp(p, h), grp(h, p), grp(h, p), grp(1, h)],
        out_specs=(grp(CHUNK * h + 2 * p, CHUNK * h), grp(CHUNK * h, 2 * p), grp(n_ap, 2 * p)),
        name="s5_coefficients",
    )(jnp.asarray(kk), lam_re.reshape(g, 1, p), lam_im.reshape(g, 1, p), log_dt.reshape(g, 1, 1), b_re, b_im,
      c_re, c_im, d_skip.reshape(g, 1, h))


PHASES = 8


def _ssm_proj_kernel(h_ref, w_ref, u_ref, z_ref, rows_scr, *, n_b):
    for jj in range(PHASES):
        for b in range(n_b):
            rows_scr[b] = h_ref[b, :, jj, :]
        h = jnp.concatenate([rows_scr[b].astype(BF16) for b in range(n_b)], axis=0)
        r = lax.dot_general(w_ref[...], h, _NT, preferred_element_type=F32)
        u_ref[jj] = r[:SSM_WIDTH].astype(BF16)
        z_ref[jj] = _silu(r[SSM_WIDTH:]).astype(BF16)


def _ssm_projection(h4, w_in):
    n_b, c2, _, d = h4.shape
    lanes = n_b * c2
    halves = CHUNK // PHASES
    w_t = w_in[:, :2 * SSM_WIDTH].T.astype(BF16)
    out = jax.ShapeDtypeStruct((CHUNK, SSM_WIDTH, SUPER * lanes), BF16)
    out_spec = pl.BlockSpec((PHASES, SSM_WIDTH, lanes), lambda j2, hf: (hf, 0, j2))
    return pl.pallas_call(
        functools.partial(_ssm_proj_kernel, n_b=n_b),
        out_shape=(out, out),
        grid=(SUPER, halves),
        in_specs=[pl.BlockSpec((n_b, c2, PHASES, d), lambda j2, hf: (0, 0, j2 * halves + hf, 0)),
                  pl.BlockSpec(w_t.shape, lambda j2, hf: (0, 0))],
        out_specs=(out_spec, out_spec),
        scratch_shapes=[pltpu.VMEM((n_b, c2, d), F32)],
        compiler_params=pltpu.CompilerParams(dimension_semantics=("arbitrary", "arbitrary"),
                                             vmem_limit_bytes=56 * 1024 * 1024),
        name="ssm_projection",
    )(h4, w_t)


def _cmul(ar, ai, xr, xi):
    return ar * xr - ai * xi, ar * xi + ai * xr


SCAN_GROUPS = 8


def _s5_group_kernel(x_ref, lhs_ref, wout_ref, ap_ref, y_ref, *, n_b, c2, n_top):
    p, gb = SSM_STATE, SCAN_GROUPS
    rows = CHUNK * SSM_GROUP
    n = SUPER * n_b * c2
    r = [jnp.dot(lhs_ref[gi], x_ref[:, SSM_GROUP * gi:SSM_GROUP * (gi + 1), :].reshape(rows, n),
                 preferred_element_type=F32) for gi in range(gb)]
    c_re = jnp.concatenate([r[gi][rows:rows + p] for gi in range(gb)], axis=0)
    c_im = jnp.concatenate([r[gi][rows + p:] for gi in range(gb)], axis=0)

    def mult(i):
        rep = [jnp.broadcast_to(ap_ref[gi, i:i + 1, :], (2 * p, 2 * p)).T for gi in range(gb)]
        return (jnp.concatenate([m[:p, :c2] for m in rep], axis=0),
                jnp.concatenate([m[p:, :c2] for m in rep], axis=0))

    lane = lax.broadcasted_iota(jnp.int32, (gb * p, c2), 1)

    def shifted(a, sh):
        return jnp.where(lane >= sh, pltpu.roll(a, sh, 1), 0.0)

    pieces = [[None] * n_b for _ in range(SUPER)]
    for b in range(n_b):
        piece = lambda a, j2: a[:, (j2 * n_b + b) * c2:(j2 * n_b + b + 1) * c2]
        a_re, a_im = mult(1)
        e_re = jnp.zeros((gb * p, c2), F32)
        e_im = jnp.zeros((gb * p, c2), F32)
        local = []
        for j2 in range(SUPER):
            local.append((e_re, e_im))
            t_re, t_im = _cmul(a_re, a_im, e_re, e_im)
            e_re, e_im = t_re + piece(c_re, j2), t_im + piece(c_im, j2)
        for i in range(n_top):
            m_re, m_im = mult(SUPER + i)
            t_re, t_im = _cmul(m_re, m_im, shifted(e_re, 1 << i), shifted(e_im, 1 << i))
            e_re, e_im = e_re + t_re, e_im + t_im
        s_re, s_im = shifted(e_re, 1), shifted(e_im, 1)
        for j2 in range(SUPER):
            if j2 == 0:
                pieces[j2][b] = (s_re, s_im)
            else:
                m_re, m_im = mult(j2)
                t_re, t_im = _cmul(m_re, m_im, s_re, s_im)
                pieces[j2][b] = (local[j2][0] + t_re, local[j2][1] + t_im)
    for gi in range(gb):
        sl = slice(gi * p, (gi + 1) * p)
        s_in = jnp.concatenate([jnp.concatenate([pieces[j2][b][0][sl], pieces[j2][b][1][sl]], axis=0)
                                for j2 in range(SUPER) for b in range(n_b)], axis=1)
        y = r[gi][:rows] + jnp.dot(wout_ref[gi], s_in.astype(BF16), preferred_element_type=F32)
        y_ref[gi] = y.reshape(CHUNK, SSM_GROUP, n).astype(y_ref.dtype)


def _s5_scan(u_t, lhs, wout, ap, n_b, c2, n_top):
    g, h = SSM_GROUPS, SSM_GROUP
    n = u_t.shape[-1]
    gb = SCAN_GROUPS
    grp = lambda *shape: pl.BlockSpec((gb,) + shape, lambda i: (i,) + (0,) * len(shape))
    return pl.pallas_call(
        functools.partial(_s5_group_kernel, n_b=n_b, c2=c2, n_top=n_top),
        out_shape=jax.ShapeDtypeStruct((g, CHUNK, h, n), BF16),
        grid=(g // gb,),
        in_specs=[pl.BlockSpec((CHUNK, gb * h, n), lambda i: (0, i, 0)),
                  grp(*lhs.shape[1:]), grp(*wout.shape[1:]), grp(*ap.shape[1:])],
        out_specs=grp(CHUNK, h, n),
        compiler_params=pltpu.CompilerParams(dimension_semantics=("arbitrary",),
                                             vmem_limit_bytes=48 * 1024 * 1024),
        name="s5_chunk_scan",
    )(u_t, lhs, wout, ap)


def _glu_kernel(y_ref, zs_ref, w_ref, b_ref, o_ref, *, n_b, c2):
    g, _, h, n = y_ref.shape
    for jj in range(PHASES):
        y = jax.nn.gelu(y_ref[:, jj].reshape(g * h, n).astype(F32))
        t = jnp.dot(w_ref[...], y.astype(BF16), preferred_element_type=F32) + b_ref[...]
        gated = (y * jax.nn.sigmoid(t) * zs_ref[jj].astype(F32)).T
        for b in range(n_b):
            o_ref[b, :, jj, :] = gated[b * c2:(b + 1) * c2]


def _glu(y_t, zs_t, w_glu, b_glu, n_b, c2):
    g, _, h, _ = y_t.shape
    lanes = n_b * c2
    halves = CHUNK // PHASES
    w_t = w_glu.T.astype(BF16)
    return pl.pallas_call(
        functools.partial(_glu_kernel, n_b=n_b, c2=c2),
        out_shape=jax.ShapeDtypeStruct((n_b, c2, SUPER * CHUNK, SSM_WIDTH), F32),
        grid=(SUPER, halves),
        in_specs=[pl.BlockSpec((g, PHASES, h, lanes), lambda j2, hf: (0, hf, 0, j2)),
                  pl.BlockSpec((PHASES, SSM_WIDTH, lanes), lambda j2, hf: (hf, 0, j2)),
                  pl.BlockSpec(w_t.shape, lambda j2, hf: (0, 0)),
                  pl.BlockSpec((SSM_WIDTH, 1), lambda j2, hf: (0, 0))],
        out_specs=pl.BlockSpec((n_b, c2, PHASES, SSM_WIDTH), lambda j2, hf: (0, 0, j2 * halves + hf, 0)),
        compiler_params=pltpu.CompilerParams(dimension_semantics=("arbitrary", "arbitrary"),
                                             vmem_limit_bytes=48 * 1024 * 1024),
        name="s5_glu",
    )(y_t, zs_t, w_t, b_glu.reshape(SSM_WIDTH, 1))


def _out_proj_kernel(ys_ref, ym_ref, x_ref, mod_ref, wa_ref, wb_ref, o_ref):
    y = (jnp.dot(ys_ref[...].astype(BF16), wa_ref[...], preferred_element_type=F32)
         + lax.dot_general(ym_ref[...], wb_ref[...], _TN, preferred_element_type=F32))
    gate = mod_ref[...][:, 2 * D_MODEL:]
    o_ref[...] = x_ref[...] + gate * y


def _out_projection(ys, ym, x2, mod3, w_out, seq):
    t, d = x2.shape
    tm = min(1024, seq)
    per_b = seq // tm
    wa = w_out[:SSM_WIDTH].astype(BF16)
    wb = w_out[SSM_WIDTH:].astype(BF16)
    tok = lambda w: pl.BlockSpec((tm, w), lambda i: (i, 0))
    return pl.pallas_call(
        _out_proj_kernel,
        out_shape=jax.ShapeDtypeStruct(x2.shape, F32),
        grid=(t // tm,),
        in_specs=[tok(SSM_WIDTH), pl.BlockSpec((MLA_WIDTH, tm), lambda i: (0, i)), tok(d),
                  pl.BlockSpec((None, 1, 3 * d), lambda i: (i // per_b, 0, 0)),
                  pl.BlockSpec(wa.shape, lambda i: (0, 0)),
                  pl.BlockSpec(wb.shape, lambda i: (0, 0))],
        out_specs=tok(d),
        compiler_params=pltpu.CompilerParams(dimension_semantics=("arbitrary",),
                                             vmem_limit_bytes=48 * 1024 * 1024),
        name="output_projection",
    )(ys, ym, x2, mod3, wa, wb)


def kernel(x, c, positions, w_ada, b_ada, norm_g, w_in, log_dt, lam_re, lam_im, b_re, b_im, c_re, c_im, d_skip,
           w_glu, b_glu, q_a_g, w_q_b, kv_a_g, w_kv_b, q_norm_g, k_norm_g, w_out):
    n_b, seq, d = x.shape
    depth = w_ada.shape[0]
    c2 = seq // (CHUNK * SUPER)
    n_top = max(int(math.log2(c2)), 0)
    assert c2 * CHUNK * SUPER == seq and (1 << n_top) == c2
    rope = _rope_tables(positions)
    for l in range(depth):
        mod3 = _modulation(c, w_ada[l], b_ada[l]).reshape(n_b, 1, 3 * d)
        x2 = x.reshape(n_b * seq, d)
        lhs, wout, ap = _s5_coefficients(log_dt[l], lam_re[l], lam_im[l], b_re[l], b_im[l], c_re[l], c_im[l],
                                         d_skip[l], n_top)
        qt, k, vt, zm, h = _mla_projection(x2, mod3, norm_g[l], w_in[l], q_a_g[l], w_q_b[l],
                                           kv_a_g[l], w_kv_b[l], q_norm_g[l], k_norm_g[l], rope, n_b, seq)
        u_t, zs_t = _ssm_projection(h.reshape(n_b, c2, SUPER * CHUNK, d), w_in[l])
        y_t = _s5_scan(u_t, lhs, wout, ap, n_b, c2, n_top)
        ys = _glu(y_t, zs_t, w_glu[l], b_glu[l], n_b, c2).reshape(n_b * seq, SSM_WIDTH)
        score_bound = (1.05 * math.sqrt(QK_HEAD) * math.log2(math.e)
                       * jnp.max(jnp.abs(q_norm_g[l])) * jnp.max(jnp.abs(k_norm_g[l])))
        ym = _attention(qt, k, vt, zm, score_bound)
        x = _out_projection(ys, ym, x2, mod3, w_out[l], seq).reshape(n_b, seq, d)
    return x
```

```python
import functools
import math

import numpy as np
import jax
import jax.numpy as jnp
from jax import lax
from jax.experimental import pallas as pl
from jax.experimental.pallas import tpu as pltpu

F32 = jnp.float32
BF16 = jnp.bfloat16
HIGHEST = lax.Precision.HIGHEST

D_MODEL = 1024
SSM_WIDTH = 512
SSM_GROUP = 16
SSM_GROUPS = 32
SSM_STATE = 64
MLA_HEADS = 8
QK_NOPE = 64
QK_ROPE = 32
QK_HEAD = QK_NOPE + QK_ROPE
V_HEAD = 64
MLA_WIDTH = MLA_HEADS * V_HEAD
Q_LORA = 384
KV_LORA = 256
ROPE_THETA = 10000.0
EPS = 1e-6
NEG_BIG = -1e30

LANES = 128
VMEM_LIMIT = 60 * 1024 * 1024
CHUNK = 16
SUPER = 4
ROPE_HALF = QK_ROPE // 2

_NT = (((1,), (1,)), ((), ()))
_TN = (((0,), (0,)), ((), ()))


def _silu(v):
    return v * jax.nn.sigmoid(v)


def _rms(v, gain):
    return v * lax.rsqrt(jnp.mean(v * v, axis=-1, keepdims=True) + EPS) * gain


def _mod_kernel(c_ref, w_ref, b_ref, o_ref):
    act = _silu(c_ref[...])
    o_ref[...] = jnp.dot(act, w_ref[...], preferred_element_type=F32, precision=HIGHEST) + b_ref[...]


def _modulation(c, w, b):
    n_b, d = c.shape
    rows = 8
    c_pad = jnp.zeros((rows, d), F32).at[:n_b].set(c)
    tn = 512
    out = pl.pallas_call(
        _mod_kernel,
        out_shape=jax.ShapeDtypeStruct((rows, 3 * d), F32),
        grid=(3 * d // tn,),
        in_specs=[pl.BlockSpec((rows, d), lambda n: (0, 0)),
                  pl.BlockSpec((d, tn), lambda n: (0, n)),
                  pl.BlockSpec((1, tn), lambda n: (0, n))],
        out_specs=pl.BlockSpec((rows, tn), lambda n: (0, n)),
        name="adaln_modulation",
    )(c_pad, w, b.reshape(1, -1))
    return out[:n_b]


def _rope_kernel(pos_ref, freq_ref, cos_ref, sin_ref):
    ang = freq_ref[...] * pos_ref[...].astype(F32)
    cos_ref[...] = jnp.cos(ang)
    sin_ref[...] = jnp.sin(ang)


def _rope_tables(positions):
    t = positions.size
    tt = min(t, 2048)
    inv_freq = (ROPE_THETA ** (-np.arange(ROPE_HALF, dtype=np.float64) * 2.0 / QK_ROPE)).astype(np.float32)
    return pl.pallas_call(
        _rope_kernel,
        out_shape=(jax.ShapeDtypeStruct((ROPE_HALF, t), F32),) * 2,
        grid=(t // tt,),
        in_specs=[pl.BlockSpec((1, tt), lambda i: (0, i)),
                  pl.BlockSpec((ROPE_HALF, 1), lambda i: (0, 0))],
        out_specs=(pl.BlockSpec((ROPE_HALF, tt), lambda i: (0, i)),) * 2,
        name="rope_tables",
    )(positions.reshape(1, t), jnp.asarray(inv_freq).reshape(ROPE_HALF, 1))


def _slab_source():
    src = np.full((LANES,), -1, np.int64)
    src[0:16] = QK_NOPE + np.arange(16)
    src[16:64] = np.arange(48)
    src[64:80] = QK_NOPE + 16 + np.arange(16)
    src[80:96] = 48 + np.arange(16)
    return src


def _to_slabs(w, per_head, src):
    k = w.shape[0]
    w3 = w.reshape(k, MLA_HEADS, per_head)
    valid = src >= 0
    cols = jnp.where(valid[None, None, :], w3[:, :, np.clip(src, 0, per_head - 1)], 0.0)
    return cols.reshape(k, MLA_HEADS * LANES)


V_ROWS = V_HEAD + 16


def _mla_proj_kernel(x_ref, mod_ref, ng_ref, w1_ref, wzt_ref, qag_ref, kvag_ref, wqt_ref, wkt_ref, wvt_ref,
                     gq_ref, gk_ref, cost_ref, sint_ref, qt_ref, k_ref, vt_ref, zs_ref, h_ref):
    d = D_MODEL
    tm = x_ref.shape[0]
    x = x_ref[...]
    mod = mod_ref[...]
    shift, scale = mod[:, :d], mod[:, d:2 * d]
    h_ref[...] = _rms(x, ng_ref[...] * (1.0 + scale)) + shift
    h = h_ref[...].astype(BF16)
    p1 = jnp.dot(h, w1_ref[...], preferred_element_type=F32)
    o1, o2 = Q_LORA, Q_LORA + KV_LORA
    cqn = _rms(p1[:, :o1], qag_ref[...]).astype(BF16)
    ckvn = _rms(p1[:, o1:o2], kvag_ref[...]).astype(BF16)
    zs_ref[...] = _silu(lax.dot_general(wzt_ref[...], h, _NT, preferred_element_type=F32)).astype(BF16)

    qt = lax.dot_general(wqt_ref[...], cqn, _NT, preferred_element_type=F32)
    kt = lax.dot_general(wkt_ref[...], ckvn, _NT, preferred_element_type=F32)
    krt = p1[:, o2:].T
    vt = lax.dot_general(wvt_ref[...], ckvn, _NT, preferred_element_type=F32)
    cos_t, sin_t = cost_ref[...], sint_ref[...]
    gq = jnp.concatenate([gq_ref[...]] * (tm // LANES), axis=1)
    gk = jnp.concatenate([gk_ref[...]] * (tm // LANES), axis=1)
    first = lax.broadcasted_iota(jnp.int32, (V_ROWS - V_HEAD, tm), 0) == 0
    ones_rows = jnp.where(first, 1.0, 0.0).astype(BF16)
    r, half = ROPE_HALF, LANES // 2

    def norm_rope(slab, gain):
        ss = jnp.sum(slab * slab, axis=0, keepdims=True)
        n = slab * lax.rsqrt(ss * (1.0 / QK_HEAD) + EPS) * gain
        x1, x2 = n[0:r], n[half:half + r]
        return jnp.concatenate([x1 * cos_t - x2 * sin_t, n[r:half], x2 * cos_t + x1 * sin_t, n[half + r:]], axis=0)

    for head in range(MLA_HEADS):
        rows = slice(LANES * head, LANES * (head + 1))
        qt_ref[head] = norm_rope(qt[rows], gq).astype(BF16)
        k_ref[head] = norm_rope(kt[rows] + krt, gk).T.astype(BF16)
        vt_ref[head, 0:V_HEAD, :] = vt[V_HEAD * head:V_HEAD * (head + 1)].astype(BF16)
        vt_ref[head, V_HEAD:, :] = ones_rows


def _mla_projection(x2, mod3, norm_g, w_in, q_a_g, w_q_b, kv_a_g, w_kv_b, q_norm_g, k_norm_g, rope, n_b, seq):
    t, d = x2.shape
    cos_t, sin_t = rope
    tm = min(512, seq)
    per_b = seq // tm
    o2 = 2 * SSM_WIDTH
    o4, o5 = o2 + Q_LORA + KV_LORA, o2 + Q_LORA + KV_LORA + QK_ROPE
    src = _slab_source()
    valid = src >= 0
    rope_src = np.where(src >= QK_NOPE, src - QK_NOPE, -1)
    nope_src = np.where(valid & (src < QK_NOPE), src, -1)
    wkr = jnp.where((rope_src >= 0)[None, :], w_in[:, o4:o5][:, np.clip(rope_src, 0, QK_ROPE - 1)], 0.0)
    w1 = jnp.concatenate([w_in[:, o2:o4], wkr], axis=1).astype(BF16)
    wzt = w_in[:, o5:].T.astype(BF16)
    wqt = _to_slabs(w_q_b, QK_HEAD, src).T.astype(BF16)
    wkt = _to_slabs(w_kv_b, QK_NOPE + V_HEAD, nope_src).T.astype(BF16)
    wvt = w_kv_b.reshape(KV_LORA, MLA_HEADS, QK_NOPE + V_HEAD)[:, :, QK_NOPE:].reshape(KV_LORA, MLA_WIDTH).T.astype(BF16)
    q_scale = math.log2(math.e) / math.sqrt(QK_HEAD)
    slab_gain = lambda g: jnp.broadcast_to(jnp.where(valid, g[np.clip(src, 0, QK_HEAD - 1)], 0.0)[:, None],
                                           (LANES, LANES))
    gq = slab_gain(q_norm_g * q_scale)
    gk = slab_gain(k_norm_g)
    const = lambda shape: pl.BlockSpec(shape, lambda i: (0,) * len(shape))
    tok = lambda i: (i // per_b, 0, 0, i % per_b)
    return pl.pallas_call(
        _mla_proj_kernel,
        out_shape=(jax.ShapeDtypeStruct((n_b, MLA_HEADS, LANES, seq), BF16),
                   jax.ShapeDtypeStruct((n_b, MLA_HEADS, seq, LANES), BF16),
                   jax.ShapeDtypeStruct((n_b, MLA_HEADS, V_ROWS, seq), BF16),
                   jax.ShapeDtypeStruct((MLA_WIDTH, t), BF16),
                   jax.ShapeDtypeStruct((t, d), F32)),
        grid=(t // tm,),
        in_specs=[pl.BlockSpec((tm, d), lambda i: (i, 0)),
                  pl.BlockSpec((None, 1, 3 * d), lambda i: (i // per_b, 0, 0)),
                  const((1, d)), const(w1.shape), const(wzt.shape), const((1, Q_LORA)), const((1, KV_LORA)),
                  const(wqt.shape), const(wkt.shape), const(wvt.shape),
                  const((LANES, LANES)), const((LANES, LANES)),
                  pl.BlockSpec((ROPE_HALF, tm), lambda i: (0, i)),
                  pl.BlockSpec((ROPE_HALF, tm), lambda i: (0, i))],
        out_specs=(pl.BlockSpec((None, MLA_HEADS, LANES, tm), tok),
                   pl.BlockSpec((None, MLA_HEADS, tm, LANES), lambda i: (i // per_b, 0, i % per_b, 0)),
                   pl.BlockSpec((None, MLA_HEADS, V_ROWS, tm), tok),
                   pl.BlockSpec((MLA_WIDTH, tm), lambda i: (0, i)),
                   pl.BlockSpec((tm, d), lambda i: (i, 0))),
        compiler_params=pltpu.CompilerParams(dimension_semantics=("arbitrary",),
                                             vmem_limit_bytes=VMEM_LIMIT),
        name="mla_projection",
    )(x2, mod3, norm_g.reshape(1, d), w1, wzt, q_a_g.reshape(1, -1), kv_a_g.reshape(1, -1), wqt, wkt, wvt, gq, gk,
      cos_t, sin_t)


ATTN_TQ = 2048
ATTN_TK = 256
HEADS_PER_STEP = 2


def _attn_kernel(qt_ref, k_ref, vt_ref, zs_ref, o_ref, s_scr):
    tq, tk = ATTN_TQ, ATTN_TK
    r = tq // tk
    assert r * tk == tq and r % 2 == 0
    qi = pl.program_id(2)

    def scores(slot, t, lo=0):
        start = pl.multiple_of(t * tk, tk)
        for e in range(HEADS_PER_STEP):
            s_scr[slot, e, :, lo:] = jnp.dot(k_ref[e, pl.ds(start, tk), :], qt_ref[e, :, lo:],
                                             preferred_element_type=F32)

    def process(slot, t, carry, lo=0, diagonal=False):
        start = pl.multiple_of(t * tk, tk)
        out = []
        for e in range(HEADS_PER_STEP):
            m, acc = carry[e]
            s = s_scr[slot, e, :, lo:]
            if diagonal:
                blk = s[:, :tk]
                ok = lax.broadcasted_iota(jnp.int32, blk.shape, 0) <= lax.broadcasted_iota(jnp.int32, blk.shape, 1)
                blk = jnp.where(ok, blk, NEG_BIG)
                s = jnp.concatenate([blk, s[:, tk:]], axis=1) if s.shape[1] > tk else blk
            m_old = m[:, lo:]
            m_new = jnp.maximum(m_old, jnp.max(s, axis=0, keepdims=True))
            p = jnp.exp2(s - m_new).astype(BF16)
            acc_new = (jnp.exp2(m_old - m_new) * acc[:, lo:]
                       + jnp.dot(vt_ref[e, :, pl.ds(start, tk)], p, preferred_element_type=F32))
            if lo:
                m_new = jnp.concatenate([m[:, :lo], m_new], axis=1)
                acc_new = jnp.concatenate([acc[:, :lo], acc_new], axis=1)
            out.append((m_new, acc_new))
        return tuple(out)

    def body(i, carry):
        base = i * r
        for u in range(r):
            scores((u + 1) % 2, base + u + 1)
            carry = process(u % 2, base + u, carry)
        return carry

    init = tuple((jnp.full((1, tq), NEG_BIG, F32), jnp.zeros((V_ROWS, tq), F32)) for _ in range(HEADS_PER_STEP))
    scores(0, 0)
    carry = lax.fori_loop(0, qi, body, init)
    base = qi * r
    for u in range(r):
        if u + 1 < r:
            scores((u + 1) % 2, base + u + 1, lo=(u + 1) * tk)
        carry = process(u % 2, base + u, carry, lo=u * tk, diagonal=True)
    _attn_finish(tuple(c[1] for c in carry), zs_ref, o_ref)


def _attn_finish(accs, zs_ref, o_ref):
    outs = [acc[:V_HEAD] / acc[V_HEAD:V_HEAD + 1] for acc in accs]
    o_ref[...] = (jnp.concatenate(outs, axis=0) * zs_ref[...].astype(F32)).astype(o_ref.dtype)


def _attn_bounded_kernel(qt_ref, k_ref, vt_ref, zs_ref, o_ref, p_scr):
    tq, tk = ATTN_TQ, ATTN_TK
    r = tq // tk
    assert r * tk == tq and r % 2 == 0
    qi = pl.program_id(2)
    q0 = qi * tq

    def probs(slot, t, lo=0, may_cross=False):
        start = pl.multiple_of(t * tk, tk)
        for e in range(HEADS_PER_STEP):
            p = jnp.exp2(jnp.dot(k_ref[e, pl.ds(start, tk), :], qt_ref[e, :, lo:], preferred_element_type=F32))
            if may_cross:
                blk = p[:, :tk]
                key = start + lax.broadcasted_iota(jnp.int32, blk.shape, 0)
                qry = q0 + lo + lax.broadcasted_iota(jnp.int32, blk.shape, 1)
                blk = jnp.where(key <= qry, blk, 0.0)
                p = jnp.concatenate([blk, p[:, tk:]], axis=1) if p.shape[1] > tk else blk
            p_scr[slot, e, :, lo:] = p.astype(BF16)

    def accumulate(slot, t, accs, lo=0):
        start = pl.multiple_of(t * tk, tk)
        out = []
        for e in range(HEADS_PER_STEP):
            pv = jnp.dot(vt_ref[e, :, pl.ds(start, tk)], p_scr[slot, e, :, lo:], preferred_element_type=F32)
            acc = accs[e]
            out.append(acc + pv if lo == 0 else jnp.concatenate([acc[:, :lo], acc[:, lo:] + pv], axis=1))
        return tuple(out)

    def body(i, accs):
        base = i * r
        for u in range(r):
            probs((u + 1) % 2, base + u + 1, may_cross=(u == r - 1))
            accs = accumulate(u % 2, base + u, accs)
        return accs

    probs(0, 0, may_cross=True)
    accs = lax.fori_loop(0, qi, body, tuple(jnp.zeros((V_ROWS, tq), F32) for _ in range(HEADS_PER_STEP)))
    base = qi * r
    for u in range(r):
        if u + 1 < r:
            probs((u + 1) % 2, base + u + 1, lo=(u + 1) * tk, may_cross=True)
        accs = accumulate(u % 2, base + u, accs, lo=u * tk)
    _attn_finish(accs, zs_ref, o_ref)


SCORE_BOUND_LOG2 = 60.0


def _attention(qt, k, vt, zs, score_bound):
    n_b, n_h, seq, _ = k.shape
    tq = ATTN_TQ
    assert seq % tq == 0 and n_h % HEADS_PER_STEP == 0
    nq = seq // tq
    hp = HEADS_PER_STEP
    width = hp * V_HEAD

    def call(body, scratch):
        return pl.pallas_call(
            body,
            out_shape=jax.ShapeDtypeStruct(zs.shape, BF16),
            grid=(n_b, n_h // hp, nq),
            in_specs=[pl.BlockSpec((None, hp, LANES, tq), lambda b, h, i: (b, h, 0, i)),
                      pl.BlockSpec((None, hp, seq, LANES), lambda b, h, i: (b, h, 0, 0)),
                      pl.BlockSpec((None, hp, V_ROWS, seq), lambda b, h, i: (b, h, 0, 0)),
                      pl.BlockSpec((width, tq), lambda b, h, i: (h, b * nq + i))],
            out_specs=pl.BlockSpec((width, tq), lambda b, h, i: (h, b * nq + i)),
            scratch_shapes=scratch,
            compiler_params=pltpu.CompilerParams(dimension_semantics=("arbitrary", "arbitrary", "arbitrary"),
                                                 vmem_limit_bytes=VMEM_LIMIT),
            name="causal_attention",
        )(qt, k, vt, zs)

    tiles = (2, hp, ATTN_TK, tq)
    return lax.cond(score_bound <= SCORE_BOUND_LOG2,
                    lambda: call(_attn_bounded_kernel, [pltpu.VMEM(tiles, BF16)]),
                    lambda: call(_attn_kernel, [pltpu.VMEM(tiles, F32)]))


def _s5_scan_steps(n_top):
    return [CHUNK * m for m in range(SUPER)] + [CHUNK * SUPER * (1 << i) for i in range(n_top)]


def _s5_exponents(n_top):
    return sorted(set(range(CHUNK + 1)) | set(_s5_scan_steps(n_top)))


def _s5_coef_kernel(kk_ref, *refs, n_top):
    for gi in range(SCAN_GROUPS):
        _s5_coef_one_group(kk_ref, *(r.at[gi] for r in refs), n_top=n_top)


def _s5_coef_one_group(kk_ref, lre_ref, lim_ref, ldt_ref, bre_ref, bim_ref, cre_ref, cim_ref, d_ref,
                       lhs_ref, wout_ref, ap_ref, *, n_top):
    h, p, n = SSM_GROUP, SSM_STATE, CHUNK
    lre, lim = lre_ref[...], lim_ref[...]
    dt = jnp.exp(ldt_ref[...])
    exps = _s5_exponents(n_top)
    kcol = kk_ref[...]
    mag = jnp.exp(kcol * (lre * dt))
    ang = kcol * (lim * dt)
    pow_re, pow_im = mag * jnp.cos(ang), mag * jnp.sin(ang)

    def power(kk):
        i = exps.index(kk)
        return pow_re[i:i + 1], pow_im[i:i + 1]

    lb_re, lb_im = power(1)
    nr, ni = lb_re - 1.0, lb_im
    den = lre * lre + lim * lim
    f_re = (nr * lre + ni * lim) / den
    f_im = (ni * lre - nr * lim) / den
    bre, bim = bre_ref[...].T, bim_ref[...].T
    bb_re = f_re * bre - f_im * bim
    bb_im = f_re * bim + f_im * bre
    cre, cim = cre_ref[...], cim_ref[...]
    cp_re, cp_im = [], []
    for kk in range(n + 1):
        pr, pi = power(kk)
        cp_re.append(cre * pr - cim * pi)
        cp_im.append(cre * pi + cim * pr)

    cpw = [jnp.concatenate([cp_re[kk], -cp_im[kk]], axis=1) for kk in range(n + 1)]
    bb_a = jnp.concatenate([bb_re, bb_im], axis=1)
    kw = lax.dot_general(jnp.concatenate(cpw[:n], axis=0), jnp.concatenate([bb_a] * n, axis=0), _NT,
                         preferred_element_type=F32, precision=HIGHEST)
    lane = lax.broadcasted_iota(jnp.int32, (h, n * h), 1)
    row = lax.broadcasted_iota(jnp.int32, (h, n * h), 0)
    col_blk = lax.shift_right_logical(lane, 4)
    d_tiled = jnp.concatenate([d_ref[...]] * n, axis=1)
    lag = [kw[h * kk:h * (kk + 1)] for kk in range(n)]
    lag[0] = lag[0] + jnp.where((lane & (h - 1)) == row, d_tiled, 0.0)
    rows = []
    for j in range(n):
        acc = jnp.zeros((h, n * h), F32)
        for jp in range(j + 1):
            acc = jnp.where(col_blk == jp, lag[j - jp], acc)
        rows.append(acc)
    lhs_ref[0:n * h, :] = jnp.concatenate(rows, axis=0).astype(lhs_ref.dtype)

    bb_b = jnp.concatenate([-bb_im, bb_re], axis=1)
    win = []
    for j in range(n):
        pr, pi = power(n - 1 - j)
        win.append(jnp.concatenate([pr, pr], axis=1) * bb_a + jnp.concatenate([pi, pi], axis=1) * bb_b)
    lhs_ref[n * h:, :] = jnp.concatenate(win, axis=0).T.astype(lhs_ref.dtype)
    wout_ref[...] = jnp.concatenate(cpw[1:], axis=0).astype(wout_ref.dtype)
    for i, kk in enumerate(_s5_scan_steps(n_top)):
        pr, pi = power(kk)
        ap_ref[i:i + 1, :] = jnp.concatenate([pr, pi], axis=1)


def _s5_coefficients(log_dt, lam_re, lam_im, b_re, b_im, c_re, c_im, d_skip, n_top):
    g, p, h = SSM_GROUPS, SSM_STATE, SSM_GROUP
    n_ap = SUPER + n_top
    gb = SCAN_GROUPS
    grp = lambda *shape: pl.BlockSpec((gb,) + shape, lambda i: (i,) + (0,) * len(shape))
    exps = np.asarray(_s5_exponents(n_top), np.float32)
    kk = np.zeros((-(-exps.size // 8) * 8, 1), np.float32)
    kk[:exps.size, 0] = exps
    return pl.pallas_call(
        functools.partial(_s5_coef_kernel, n_top=n_top),
        out_shape=(jax.ShapeDtypeStruct((g, CHUNK * h + 2 * p, CHUNK * h), BF16),
                   jax.ShapeDtypeStruct((g, CHUNK * h, 2 * p), BF16),
                   jax.ShapeDtypeStruct((g, n_ap, 2 * p), F32)),
        grid=(g // gb,),
        in_specs=[pl.BlockSpec(kk.shape, lambda i: (0, 0)),
                  grp(1, p), grp(1, p), grp(1, 1), grp(p, h), grp(p, h), grp(h, p), grp(h, p), grp(1, h)],
        out_specs=(grp(CHUNK * h + 2 * p, CHUNK * h), grp(CHUNK * h, 2 * p), grp(n_ap, 2 * p)),
        name="s5_coefficients",
    )(jnp.asarray(kk), lam_re.reshape(g, 1, p), lam_im.reshape(g, 1, p), log_dt.reshape(g, 1, 1), b_re, b_im,
      c_re, c_im, d_skip.reshape(g, 1, h))


PHASES = 8


def _ssm_proj_kernel(h_ref, w_ref, u_ref, z_ref, rows_scr, *, n_b):
    for jj in range(PHASES):
        for b in range(n_b):
            rows_scr[b] = h_ref[b, :, jj, :]
        h = jnp.concatenate([rows_scr[b].astype(BF16) for b in range(n_b)], axis=0)
        r = lax.dot_general(w_ref[...], h, _NT, preferred_element_type=F32)
        u_ref[jj] = r[:SSM_WIDTH].astype(BF16)
        z_ref[jj] = _silu(r[SSM_WIDTH:]).astype(BF16)


def _ssm_projection(h4, w_in):
    n_b, c2, _, d = h4.shape
    lanes = n_b * c2
    halves = CHUNK // PHASES
    w_t = w_in[:, :2 * SSM_WIDTH].T.astype(BF16)
    out = jax.ShapeDtypeStruct((CHUNK, SSM_WIDTH, SUPER * lanes), BF16)
    out_spec = pl.BlockSpec((PHASES, SSM_WIDTH, lanes), lambda j2, hf: (hf, 0, j2))
    return pl.pallas_call(
        functools.partial(_ssm_proj_kernel, n_b=n_b),
        out_shape=(out, out),
        grid=(SUPER, halves),
        in_specs=[pl.BlockSpec((n_b, c2, PHASES, d), lambda j2, hf: (0, 0, j2 * halves + hf, 0)),
                  pl.BlockSpec(w_t.shape, lambda j2, hf: (0, 0))],
        out_specs=(out_spec, out_spec),
        scratch_shapes=[pltpu.VMEM((n_b, c2, d), F32)],
        compiler_params=pltpu.CompilerParams(dimension_semantics=("arbitrary", "arbitrary"),
                                             vmem_limit_bytes=VMEM_LIMIT),
        name="ssm_projection",
    )(h4, w_t)


def _cmul(ar, ai, xr, xi):
    return ar * xr - ai * xi, ar * xi + ai * xr


SCAN_GROUPS = 8


def _s5_group_kernel(x_ref, lhs_ref, wout_ref, ap_ref, y_ref, *, n_b, c2, n_top):
    p, gb = SSM_STATE, SCAN_GROUPS
    rows = CHUNK * SSM_GROUP
    n = SUPER * n_b * c2
    r = [jnp.dot(lhs_ref[gi], x_ref[:, SSM_GROUP * gi:SSM_GROUP * (gi + 1), :].reshape(rows, n),
                 preferred_element_type=F32) for gi in range(gb)]
    c_re = jnp.concatenate([r[gi][rows:rows + p] for gi in range(gb)], axis=0)
    c_im = jnp.concatenate([r[gi][rows + p:] for gi in range(gb)], axis=0)

    def mult(i):
        rep = [jnp.broadcast_to(ap_ref[gi, i:i + 1, :], (2 * p, 2 * p)).T for gi in range(gb)]
        return (jnp.concatenate([m[:p, :c2] for m in rep], axis=0),
                jnp.concatenate([m[p:, :c2] for m in rep], axis=0))

    lane = lax.broadcasted_iota(jnp.int32, (gb * p, c2), 1)

    def shifted(a, sh):
        return jnp.where(lane >= sh, pltpu.roll(a, sh, 1), 0.0)

    pieces = [[None] * n_b for _ in range(SUPER)]
    for b in range(n_b):
        piece = lambda a, j2: a[:, (j2 * n_b + b) * c2:(j2 * n_b + b + 1) * c2]
        a_re, a_im = mult(1)
        e_re = jnp.zeros((gb * p, c2), F32)
        e_im = jnp.zeros((gb * p, c2), F32)
        local = []
        for j2 in range(SUPER):
            local.append((e_re, e_im))
            t_re, t_im = _cmul(a_re, a_im, e_re, e_im)
            e_re, e_im = t_re + piece(c_re, j2), t_im + piece(c_im, j2)
        for i in range(n_top):
            m_re, m_im = mult(SUPER + i)
            t_re, t_im = _cmul(m_re, m_im, shifted(e_re, 1 << i), shifted(e_im, 1 << i))
            e_re, e_im = e_re + t_re, e_im + t_im
        s_re, s_im = shifted(e_re, 1), shifted(e_im, 1)
        for j2 in range(SUPER):
            if j2 == 0:
                pieces[j2][b] = (s_re, s_im)
            else:
                m_re, m_im = mult(j2)
                t_re, t_im = _cmul(m_re, m_im, s_re, s_im)
                pieces[j2][b] = (local[j2][0] + t_re, local[j2][1] + t_im)
    for gi in range(gb):
        sl = slice(gi * p, (gi + 1) * p)
        s_in = jnp.concatenate([jnp.concatenate([pieces[j2][b][0][sl], pieces[j2][b][1][sl]], axis=0)
                                for j2 in range(SUPER) for b in range(n_b)], axis=1)
        y = r[gi][:rows] + jnp.dot(wout_ref[gi], s_in.astype(BF16), preferred_element_type=F32)
        y_ref[gi] = y.reshape(CHUNK, SSM_GROUP, n).astype(y_ref.dtype)


def _s5_scan(u_t, lhs, wout, ap, n_b, c2, n_top):
    g, h = SSM_GROUPS, SSM_GROUP
    n = u_t.shape[-1]
    gb = SCAN_GROUPS
    grp = lambda *shape: pl.BlockSpec((gb,) + shape, lambda i: (i,) + (0,) * len(shape))
    return pl.pallas_call(
        functools.partial(_s5_group_kernel, n_b=n_b, c2=c2, n_top=n_top),
        out_shape=jax.ShapeDtypeStruct((g, CHUNK, h, n), BF16),
        grid=(g // gb,),
        in_specs=[pl.BlockSpec((CHUNK, gb * h, n), lambda i: (0, i, 0)),
                  grp(*lhs.shape[1:]), grp(*wout.shape[1:]), grp(*ap.shape[1:])],
        out_specs=grp(CHUNK, h, n),
        compiler_params=pltpu.CompilerParams(dimension_semantics=("arbitrary",),
                                             vmem_limit_bytes=VMEM_LIMIT),
        name="s5_chunk_scan",
    )(u_t, lhs, wout, ap)


def _glu_kernel(y_ref, zs_ref, w_ref, b_ref, o_ref, *, n_b, c2):
    g, _, h, n = y_ref.shape
    for jj in range(PHASES):
        y = jax.nn.gelu(y_ref[:, jj].reshape(g * h, n).astype(F32))
        t = jnp.dot(w_ref[...], y.astype(BF16), preferred_element_type=F32) + b_ref[...]
        gated = (y * jax.nn.sigmoid(t) * zs_ref[jj].astype(F32)).T
        for b in range(n_b):
            o_ref[b, :, jj, :] = gated[b * c2:(b + 1) * c2]


def _glu(y_t, zs_t, w_glu, b_glu, n_b, c2):
    g, _, h, _ = y_t.shape
    lanes = n_b * c2
    halves = CHUNK // PHASES
    w_t = w_glu.T.astype(BF16)
    return pl.pallas_call(
        functools.partial(_glu_kernel, n_b=n_b, c2=c2),
        out_shape=jax.ShapeDtypeStruct((n_b, c2, SUPER * CHUNK, SSM_WIDTH), F32),
        grid=(SUPER, halves),
        in_specs=[pl.BlockSpec((g, PHASES, h, lanes), lambda j2, hf: (0, hf, 0, j2)),
                  pl.BlockSpec((PHASES, SSM_WIDTH, lanes), lambda j2, hf: (hf, 0, j2)),
                  pl.BlockSpec(w_t.shape, lambda j2, hf: (0, 0)),
                  pl.BlockSpec((SSM_WIDTH, 1), lambda j2, hf: (0, 0))],
        out_specs=pl.BlockSpec((n_b, c2, PHASES, SSM_WIDTH), lambda j2, hf: (0, 0, j2 * halves + hf, 0)),
        compiler_params=pltpu.CompilerParams(dimension_semantics=("arbitrary", "arbitrary"),
                                             vmem_limit_bytes=VMEM_LIMIT),
        name="s5_glu",
    )(y_t, zs_t, w_t, b_glu.reshape(SSM_WIDTH, 1))


def _out_proj_kernel(ys_ref, ym_ref, x_ref, mod_ref, wa_ref, wb_ref, o_ref):
    y = (jnp.dot(ys_ref[...].astype(BF16), wa_ref[...], preferred_element_type=F32)
         + lax.dot_general(ym_ref[...], wb_ref[...], _TN, preferred_element_type=F32))
    gate = mod_ref[...][:, 2 * D_MODEL:]
    o_ref[...] = x_ref[...] + gate * y


def _out_projection(ys, ym, x2, mod3, w_out, seq):
    t, d = x2.shape
    tm = min(1024, seq)
    per_b = seq // tm
    wa = w_out[:SSM_WIDTH].astype(BF16)
    wb = w_out[SSM_WIDTH:].astype(BF16)
    tok = lambda w: pl.BlockSpec((tm, w), lambda i: (i, 0))
    return pl.pallas_call(
        _out_proj_kernel,
        out_shape=jax.ShapeDtypeStruct(x2.shape, F32),
        grid=(t // tm,),
        in_specs=[tok(SSM_WIDTH), pl.BlockSpec((MLA_WIDTH, tm), lambda i: (0, i)), tok(d),
                  pl.BlockSpec((None, 1, 3 * d), lambda i: (i // per_b, 0, 0)),
                  pl.BlockSpec(wa.shape, lambda i: (0, 0)),
                  pl.BlockSpec(wb.shape, lambda i: (0, 0))],
        out_specs=tok(d),
        compiler_params=pltpu.CompilerParams(dimension_semantics=("arbitrary",),
                                             vmem_limit_bytes=VMEM_LIMIT),
        name="output_projection",
    )(ys, ym, x2, mod3, wa, wb)


def kernel(x, c, positions, w_ada, b_ada, norm_g, w_in, log_dt, lam_re, lam_im, b_re, b_im, c_re, c_im, d_skip,
           w_glu, b_glu, q_a_g, w_q_b, kv_a_g, w_kv_b, q_norm_g, k_norm_g, w_out):
    n_b, seq, d = x.shape
    depth = w_ada.shape[0]
    c2 = seq // (CHUNK * SUPER)
    n_top = max(int(math.log2(c2)), 0)
    assert c2 * CHUNK * SUPER == seq and (1 << n_top) == c2
    rope = _rope_tables(positions)
    for l in range(depth):
        mod3 = _modulation(c, w_ada[l], b_ada[l]).reshape(n_b, 1, 3 * d)
        x2 = x.reshape(n_b * seq, d)
        lhs, wout, ap = _s5_coefficients(log_dt[l], lam_re[l], lam_im[l], b_re[l], b_im[l], c_re[l], c_im[l],
                                         d_skip[l], n_top)
        qt, k, vt, zm, h = _mla_projection(x2, mod3, norm_g[l], w_in[l], q_a_g[l], w_q_b[l],
                                           kv_a_g[l], w_kv_b[l], q_norm_g[l], k_norm_g[l], rope, n_b, seq)
        u_t, zs_t = _ssm_projection(h.reshape(n_b, c2, SUPER * CHUNK, d), w_in[l])
        y_t = _s5_scan(u_t, lhs, wout, ap, n_b, c2, n_top)
        ys = _glu(y_t, zs_t, w_glu[l], b_glu[l], n_b, c2).reshape(n_b * seq, SSM_WIDTH)
        score_bound = (1.05 * math.sqrt(QK_HEAD) * math.log2(math.e)
                       * jnp.max(jnp.abs(q_norm_g[l])) * jnp.max(jnp.abs(k_norm_g[l])))
        ym = _attention(qt, k, vt, zm, score_bound)
        x = _out_projection(ys, ym, x2, mod3, w_out[l], seq).reshape(n_b, seq, d)
    return x
```

```python
import functools
import math

import numpy as np
import jax
import jax.numpy as jnp
from jax import lax
from jax.experimental import pallas as pl
from jax.experimental.pallas import tpu as pltpu

F32 = jnp.float32
BF16 = jnp.bfloat16
HIGHEST = lax.Precision.HIGHEST

D_MODEL = 1024
SSM_WIDTH = 512
SSM_GROUP = 16
SSM_GROUPS = 32
SSM_STATE = 64
MLA_HEADS = 8
QK_NOPE = 64
QK_ROPE = 32
QK_HEAD = QK_NOPE + QK_ROPE
V_HEAD = 64
MLA_WIDTH = MLA_HEADS * V_HEAD
Q_LORA = 384
KV_LORA = 256
ROPE_THETA = 10000.0
EPS = 1e-6
NEG_BIG = -1e30

LANES = 128
VMEM_LIMIT = 60 * 1024 * 1024
CHUNK = 16
SUPER = 4
ROPE_HALF = QK_ROPE // 2

_NT = (((1,), (1,)), ((), ()))
_TN = (((0,), (0,)), ((), ()))


def _silu(v):
    return v * jax.nn.sigmoid(v)


def _rms(v, gain):
    return v * lax.rsqrt(jnp.mean(v * v, axis=-1, keepdims=True) + EPS) * gain


def _mod_kernel(c_ref, w_ref, b_ref, o_ref):
    act = _silu(c_ref[...])
    o_ref[...] = jnp.dot(act, w_ref[...], preferred_element_type=F32, precision=HIGHEST) + b_ref[...]


def _modulation(c, w, b):
    n_b, d = c.shape
    rows = 8
    c_pad = jnp.zeros((rows, d), F32).at[:n_b].set(c)
    tn = 512
    out = pl.pallas_call(
        _mod_kernel,
        out_shape=jax.ShapeDtypeStruct((rows, 3 * d), F32),
        grid=(3 * d // tn,),
        in_specs=[pl.BlockSpec((rows, d), lambda n: (0, 0)),
                  pl.BlockSpec((d, tn), lambda n: (0, n)),
                  pl.BlockSpec((1, tn), lambda n: (0, n))],
        out_specs=pl.BlockSpec((rows, tn), lambda n: (0, n)),
        name="adaln_modulation",
    )(c_pad, w, b.reshape(1, -1))
    return out[:n_b]


def _rope_kernel(pos_ref, freq_ref, cos_ref, sin_ref):
    ang = freq_ref[...] * pos_ref[...].astype(F32)
    cos_ref[...] = jnp.cos(ang)
    sin_ref[...] = jnp.sin(ang)


def _rope_tables(positions):
    t = positions.size
    tt = min(t, 2048)
    inv_freq = (ROPE_THETA ** (-np.arange(ROPE_HALF, dtype=np.float64) * 2.0 / QK_ROPE)).astype(np.float32)
    return pl.pallas_call(
        _rope_kernel,
        out_shape=(jax.ShapeDtypeStruct((ROPE_HALF, t), F32),) * 2,
        grid=(t // tt,),
        in_specs=[pl.BlockSpec((1, tt), lambda i: (0, i)),
                  pl.BlockSpec((ROPE_HALF, 1), lambda i: (0, 0))],
        out_specs=(pl.BlockSpec((ROPE_HALF, tt), lambda i: (0, i)),) * 2,
        name="rope_tables",
    )(positions.reshape(1, t), jnp.asarray(inv_freq).reshape(ROPE_HALF, 1))


def _slab_source():
    src = np.full((LANES,), -1, np.int64)
    src[0:16] = QK_NOPE + np.arange(16)
    src[16:64] = np.arange(48)
    src[64:80] = QK_NOPE + 16 + np.arange(16)
    src[80:96] = 48 + np.arange(16)
    return src


def _to_slabs(w, per_head, src):
    k = w.shape[0]
    w3 = w.reshape(k, MLA_HEADS, per_head)
    valid = src >= 0
    cols = jnp.where(valid[None, None, :], w3[:, :, np.clip(src, 0, per_head - 1)], 0.0)
    return cols.reshape(k, MLA_HEADS * LANES)


V_ROWS = V_HEAD + 16


def _mla_proj_kernel(x_ref, mod_ref, ng_ref, w1_ref, wzt_ref, qag_ref, kvag_ref, wqt_ref, wkt_ref, wvt_ref,
                     gq_ref, gk_ref, cost_ref, sint_ref, qt_ref, k_ref, vt_ref, zs_ref, h_ref):
    d = D_MODEL
    tm = x_ref.shape[0]
    x = x_ref[...]
    mod = mod_ref[...]
    shift, scale = mod[:, :d], mod[:, d:2 * d]
    h_ref[...] = _rms(x, ng_ref[...] * (1.0 + scale)) + shift
    h = h_ref[...].astype(BF16)
    p1 = jnp.dot(h, w1_ref[...], preferred_element_type=F32)
    o1, o2 = Q_LORA, Q_LORA + KV_LORA
    cqn = _rms(p1[:, :o1], qag_ref[...]).astype(BF16)
    ckvn = _rms(p1[:, o1:o2], kvag_ref[...]).astype(BF16)
    zs_ref[...] = _silu(lax.dot_general(wzt_ref[...], h, _NT, preferred_element_type=F32)).astype(BF16)

    qt = lax.dot_general(wqt_ref[...], cqn, _NT, preferred_element_type=F32)
    kt = lax.dot_general(wkt_ref[...], ckvn, _NT, preferred_element_type=F32)
    krt = p1[:, o2:].T
    vt = lax.dot_general(wvt_ref[...], ckvn, _NT, preferred_element_type=F32)
    cos_t, sin_t = cost_ref[...], sint_ref[...]
    gq = jnp.concatenate([gq_ref[...]] * (tm // LANES), axis=1)
    gk = jnp.concatenate([gk_ref[...]] * (tm // LANES), axis=1)
    first = lax.broadcasted_iota(jnp.int32, (V_ROWS - V_HEAD, tm), 0) == 0
    ones_rows = jnp.where(first, 1.0, 0.0).astype(BF16)
    r, half = ROPE_HALF, LANES // 2

    def norm_rope(slab, gain):
        ss = jnp.sum(slab * slab, axis=0, keepdims=True)
        n = slab * lax.rsqrt(ss * (1.0 / QK_HEAD) + EPS) * gain
        x1, x2 = n[0:r], n[half:half + r]
        return jnp.concatenate([x1 * cos_t - x2 * sin_t, n[r:half], x2 * cos_t + x1 * sin_t, n[half + r:]], axis=0)

    for head in range(MLA_HEADS):
        rows = slice(LANES * head, LANES * (head + 1))
        qt_ref[head] = norm_rope(qt[rows], gq).astype(BF16)
        k_ref[head] = norm_rope(kt[rows] + krt, gk).T.astype(BF16)
        vt_ref[head, 0:V_HEAD, :] = vt[V_HEAD * head:V_HEAD * (head + 1)].astype(BF16)
        vt_ref[head, V_HEAD:, :] = ones_rows


def _mla_projection(x2, mod3, norm_g, w_in, q_a_g, w_q_b, kv_a_g, w_kv_b, q_norm_g, k_norm_g, rope, n_b, seq):
    t, d = x2.shape
    cos_t, sin_t = rope
    tm = min(512, seq)
    per_b = seq // tm
    o2 = 2 * SSM_WIDTH
    o4, o5 = o2 + Q_LORA + KV_LORA, o2 + Q_LORA + KV_LORA + QK_ROPE
    src = _slab_source()
    valid = src >= 0
    rope_src = np.where(src >= QK_NOPE, src - QK_NOPE, -1)
    nope_src = np.where(valid & (src < QK_NOPE), src, -1)
    wkr = jnp.where((rope_src >= 0)[None, :], w_in[:, o4:o5][:, np.clip(rope_src, 0, QK_ROPE - 1)], 0.0)
    w1 = jnp.concatenate([w_in[:, o2:o4], wkr], axis=1).astype(BF16)
    wzt = w_in[:, o5:].T.astype(BF16)
    wqt = _to_slabs(w_q_b, QK_HEAD, src).T.astype(BF16)
    wkt = _to_slabs(w_kv_b, QK_NOPE + V_HEAD, nope_src).T.astype(BF16)
    wvt = w_kv_b.reshape(KV_LORA, MLA_HEADS, QK_NOPE + V_HEAD)[:, :, QK_NOPE:].reshape(KV_LORA, MLA_WIDTH).T.astype(BF16)
    q_scale = math.log2(math.e) / math.sqrt(QK_HEAD)
    slab_gain = lambda g: jnp.broadcast_to(jnp.where(valid, g[np.clip(src, 0, QK_HEAD - 1)], 0.0)[:, None],
                                           (LANES, LANES))
    gq = slab_gain(q_norm_g * q_scale)
    gk = slab_gain(k_norm_g)
    const = lambda shape: pl.BlockSpec(shape, lambda i: (0,) * len(shape))
    tok = lambda i: (i // per_b, 0, 0, i % per_b)
    return pl.pallas_call(
        _mla_proj_kernel,
        out_shape=(jax.ShapeDtypeStruct((n_b, MLA_HEADS, LANES, seq), BF16),
                   jax.ShapeDtypeStruct((n_b, MLA_HEADS, seq, LANES), BF16),
                   jax.ShapeDtypeStruct((n_b, MLA_HEADS, V_ROWS, seq), BF16),
                   jax.ShapeDtypeStruct((MLA_WIDTH, t), BF16),
                   jax.ShapeDtypeStruct((t, d), F32)),
        grid=(t // tm,),
        in_specs=[pl.BlockSpec((tm, d), lambda i: (i, 0)),
                  pl.BlockSpec((None, 1, 3 * d), lambda i: (i // per_b, 0, 0)),
                  const((1, d)), const(w1.shape), const(wzt.shape), const((1, Q_LORA)), const((1, KV_LORA)),
                  const(wqt.shape), const(wkt.shape), const(wvt.shape),
                  const((LANES, LANES)), const((LANES, LANES)),
                  pl.BlockSpec((ROPE_HALF, tm), lambda i: (0, i)),
                  pl.BlockSpec((ROPE_HALF, tm), lambda i: (0, i))],
        out_specs=(pl.BlockSpec((None, MLA_HEADS, LANES, tm), tok),
                   pl.BlockSpec((None, MLA_HEADS, tm, LANES), lambda i: (i // per_b, 0, i % per_b, 0)),
                   pl.BlockSpec((None, MLA_HEADS, V_ROWS, tm), tok),
                   pl.BlockSpec((MLA_WIDTH, tm), lambda i: (0, i)),
                   pl.BlockSpec((tm, d), lambda i: (i, 0))),
        compiler_params=pltpu.CompilerParams(dimension_semantics=("arbitrary",),
                                             vmem_limit_bytes=VMEM_LIMIT),
        name="mla_projection",
    )(x2, mod3, norm_g.reshape(1, d), w1, wzt, q_a_g.reshape(1, -1), kv_a_g.reshape(1, -1), wqt, wkt, wvt, gq, gk,
      cos_t, sin_t)


ATTN_TQ = 2048
ATTN_TK = 256
HEADS_PER_STEP = 2


def _attn_kernel(qt_ref, k_ref, vt_ref, zs_ref, o_ref, s_scr):
    tq, tk = ATTN_TQ, ATTN_TK
    r = tq // tk
    assert r * tk == tq and r % 2 == 0
    qi = pl.program_id(2)

    def scores(slot, t, lo=0):
        start = pl.multiple_of(t * tk, tk)
        for e in range(HEADS_PER_STEP):
            s_scr[slot, e, :, lo:] = jnp.dot(k_ref[e, pl.ds(start, tk), :], qt_ref[e, :, lo:],
                                             preferred_element_type=F32)

    def process(slot, t, carry, lo=0, diagonal=False):
        start = pl.multiple_of(t * tk, tk)
        out = []
        for e in range(HEADS_PER_STEP):
            m, acc = carry[e]
            s = s_scr[slot, e, :, lo:]
            if diagonal:
                blk = s[:, :tk]
                ok = lax.broadcasted_iota(jnp.int32, blk.shape, 0) <= lax.broadcasted_iota(jnp.int32, blk.shape, 1)
                blk = jnp.where(ok, blk, NEG_BIG)
                s = jnp.concatenate([blk, s[:, tk:]], axis=1) if s.shape[1] > tk else blk
            m_old = m[:, lo:]
            m_new = jnp.maximum(m_old, jnp.max(s, axis=0, keepdims=True))
            p = jnp.exp2(s - m_new).astype(BF16)
            acc_new = (jnp.exp2(m_old - m_new) * acc[:, lo:]
                       + jnp.dot(vt_ref[e, :, pl.ds(start, tk)], p, preferred_element_type=F32))
            if lo:
                m_new = jnp.concatenate([m[:, :lo], m_new], axis=1)
                acc_new = jnp.concatenate([acc[:, :lo], acc_new], axis=1)
            out.append((m_new, acc_new))
        return tuple(out)

    def body(i, carry):
        base = i * r
        for u in range(r):
            scores((u + 1) % 2, base + u + 1)
            carry = process(u % 2, base + u, carry)
        return carry

    init = tuple((jnp.full((1, tq), NEG_BIG, F32), jnp.zeros((V_ROWS, tq), F32)) for _ in range(HEADS_PER_STEP))
    scores(0, 0)
    carry = lax.fori_loop(0, qi, body, init)
    base = qi * r
    for u in range(r):
        if u + 1 < r:
            scores((u + 1) % 2, base + u + 1, lo=(u + 1) * tk)
        carry = process(u % 2, base + u, carry, lo=u * tk, diagonal=True)
    _attn_finish(tuple(c[1] for c in carry), zs_ref, o_ref)


def _attn_finish(accs, zs_ref, o_ref):
    outs = [acc[:V_HEAD] / acc[V_HEAD:V_HEAD + 1] for acc in accs]
    o_ref[...] = (jnp.concatenate(outs, axis=0) * zs_ref[...].astype(F32)).astype(o_ref.dtype)


def _attn_bounded_kernel(qt_ref, k_ref, vt_ref, zs_ref, o_ref, p_scr):
    tq, tk = ATTN_TQ, ATTN_TK
    r = tq // tk
    assert r * tk == tq and r % 2 == 0
    qi = pl.program_id(2)
    q0 = qi * tq

    def probs(slot, t, lo=0, may_cross=False):
        start = pl.multiple_of(t * tk, tk)
        for e in range(HEADS_PER_STEP):
            p = jnp.exp2(jnp.dot(k_ref[e, pl.ds(start, tk), :], qt_ref[e, :, lo:], preferred_element_type=F32))
            if may_cross:
                blk = p[:, :tk]
                key = start + lax.broadcasted_iota(jnp.int32, blk.shape, 0)
                qry = q0 + lo + lax.broadcasted_iota(jnp.int32, blk.shape, 1)
                blk = jnp.where(key <= qry, blk, 0.0)
                p = jnp.concatenate([blk, p[:, tk:]], axis=1) if p.shape[1] > tk else blk
            p_scr[slot, e, :, lo:] = p.astype(BF16)

    def accumulate(slot, t, accs, lo=0):
        start = pl.multiple_of(t * tk, tk)
        out = []
        for e in range(HEADS_PER_STEP):
            pv = jnp.dot(vt_ref[e, :, pl.ds(start, tk)], p_scr[slot, e, :, lo:], preferred_element_type=F32)
            acc = accs[e]
            out.append(acc + pv if lo == 0 else jnp.concatenate([acc[:, :lo], acc[:, lo:] + pv], axis=1))
        return tuple(out)

    def body(i, accs):
        base = i * r
        for u in range(r):
            probs((u + 1) % 2, base + u + 1, may_cross=(u == r - 1))
            accs = accumulate(u % 2, base + u, accs)
        return accs

    probs(0, 0, may_cross=True)
    accs = lax.fori_loop(0, qi, body, tuple(jnp.zeros((V_ROWS, tq), F32) for _ in range(HEADS_PER_STEP)))
    base = qi * r
    for u in range(r):
        if u + 1 < r:
            probs((u + 1) % 2, base + u + 1, lo=(u + 1) * tk, may_cross=True)
        accs = accumulate(u % 2, base + u, accs, lo=u * tk)
    _attn_finish(accs, zs_ref, o_ref)


SCORE_BOUND_LOG2 = 60.0


def _attention(qt, k, vt, zs, score_bound):
    n_b, n_h, seq, _ = k.shape
    tq = ATTN_TQ
    assert seq % tq == 0 and n_h % HEADS_PER_STEP == 0
    nq = seq // tq
    hp = HEADS_PER_STEP
    width = hp * V_HEAD

    def call(body, scratch):
        return pl.pallas_call(
            body,
            out_shape=jax.ShapeDtypeStruct(zs.shape, BF16),
            grid=(n_b, n_h // hp, nq),
            in_specs=[pl.BlockSpec((None, hp, LANES, tq), lambda b, h, i: (b, h, 0, i)),
                      pl.BlockSpec((None, hp, seq, LANES), lambda b, h, i: (b, h, 0, 0)),
                      pl.BlockSpec((None, hp, V_ROWS, seq), lambda b, h, i: (b, h, 0, 0)),
                      pl.BlockSpec((width, tq), lambda b, h, i: (h, b * nq + i))],
            out_specs=pl.BlockSpec((width, tq), lambda b, h, i: (h, b * nq + i)),
            scratch_shapes=scratch,
            compiler_params=pltpu.CompilerParams(dimension_semantics=("arbitrary", "arbitrary", "arbitrary"),
                                                 vmem_limit_bytes=VMEM_LIMIT),
            name="causal_attention",
        )(qt, k, vt, zs)

    tiles = (2, hp, ATTN_TK, tq)
    return lax.cond(score_bound <= SCORE_BOUND_LOG2,
                    lambda: call(_attn_bounded_kernel, [pltpu.VMEM(tiles, BF16)]),
                    lambda: call(_attn_kernel, [pltpu.VMEM(tiles, F32)]))


def _s5_scan_steps(n_top):
    return [CHUNK * m for m in range(SUPER)] + [CHUNK * SUPER * (1 << i) for i in range(n_top)]


def _s5_exponents(n_top):
    return sorted(set(range(CHUNK + 1)) | set(_s5_scan_steps(n_top)))


def _s5_coef_kernel(kk_ref, *refs, n_top):
    for gi in range(SCAN_GROUPS):
        _s5_coef_one_group(kk_ref, *(r.at[gi] for r in refs), n_top=n_top)


def _s5_coef_one_group(kk_ref, lre_ref, lim_ref, ldt_ref, bre_ref, bim_ref, cre_ref, cim_ref, d_ref,
                       lhs_ref, wout_ref, ap_ref, *, n_top):
    h, p, n = SSM_GROUP, SSM_STATE, CHUNK
    lre, lim = lre_ref[...], lim_ref[...]
    dt = jnp.exp(ldt_ref[...])
    exps = _s5_exponents(n_top)
    kcol = kk_ref[...]
    mag = jnp.exp(kcol * (lre * dt))
    ang = kcol * (lim * dt)
    pow_re, pow_im = mag * jnp.cos(ang), mag * jnp.sin(ang)

    def power(kk):
        i = exps.index(kk)
        return pow_re[i:i + 1], pow_im[i:i + 1]

    lb_re, lb_im = power(1)
    nr, ni = lb_re - 1.0, lb_im
    den = lre * lre + lim * lim
    f_re = (nr * lre + ni * lim) / den
    f_im = (ni * lre - nr * lim) / den
    bre, bim = bre_ref[...].T, bim_ref[...].T
    bb_re = f_re * bre - f_im * bim
    bb_im = f_re * bim + f_im * bre
    cre, cim = cre_ref[...], cim_ref[...]
    cp_re, cp_im = [], []
    for kk in range(n + 1):
        pr, pi = power(kk)
        cp_re.append(cre * pr - cim * pi)
        cp_im.append(cre * pi + cim * pr)

    cpw = [jnp.concatenate([cp_re[kk], -cp_im[kk]], axis=1) for kk in range(n + 1)]
    bb_a = jnp.concatenate([bb_re, bb_im], axis=1)
    kw = lax.dot_general(jnp.concatenate(cpw[:n], axis=0), jnp.concatenate([bb_a] * n, axis=0), _NT,
                         preferred_element_type=F32, precision=HIGHEST)
    lane = lax.broadcasted_iota(jnp.int32, (h, n * h), 1)
    row = lax.broadcasted_iota(jnp.int32, (h, n * h), 0)
    col_blk = lax.shift_right_logical(lane, 4)
    d_tiled = jnp.concatenate([d_ref[...]] * n, axis=1)
    lag = [kw[h * kk:h * (kk + 1)] for kk in range(n)]
    lag[0] = lag[0] + jnp.where((lane & (h - 1)) == row, d_tiled, 0.0)
    rows = []
    for j in range(n):
        acc = jnp.zeros((h, n * h), F32)
        for jp in range(j + 1):
            acc = jnp.where(col_blk == jp, lag[j - jp], acc)
        rows.append(acc)
    lhs_ref[0:n * h, :] = jnp.concatenate(rows, axis=0).astype(lhs_ref.dtype)

    bb_b = jnp.concatenate([-bb_im, bb_re], axis=1)
    win = []
    for j in range(n):
        pr, pi = power(n - 1 - j)
        win.append(jnp.concatenate([pr, pr], axis=1) * bb_a + jnp.concatenate([pi, pi], axis=1) * bb_b)
    lhs_ref[n * h:, :] = jnp.concatenate(win, axis=0).T.astype(lhs_ref.dtype)
    wout_ref[...] = jnp.concatenate(cpw[1:], axis=0).astype(wout_ref.dtype)
    for i, kk in enumerate(_s5_scan_steps(n_top)):
        pr, pi = power(kk)
        ap_ref[i:i + 1, :] = jnp.concatenate([pr, pi], axis=1)


def _s5_coefficients(log_dt, lam_re, lam_im, b_re, b_im, c_re, c_im, d_skip, n_top):
    g, p, h = SSM_GROUPS, SSM_STATE, SSM_GROUP
    n_ap = SUPER + n_top
    gb = SCAN_GROUPS
    grp = lambda *shape: pl.BlockSpec((gb,) + shape, lambda i: (i,) + (0,) * len(shape))
    exps = np.asarray(_s5_exponents(n_top), np.float32)
    kk = np.zeros((-(-exps.size // 8) * 8, 1), np.float32)
    kk[:exps.size, 0] = exps
    return pl.pallas_call(
        functools.partial(_s5_coef_kernel, n_top=n_top),
        out_shape=(jax.ShapeDtypeStruct((g, CHUNK * h + 2 * p, CHUNK * h), BF16),
                   jax.ShapeDtypeStruct((g, CHUNK * h, 2 * p), BF16),
                   jax.ShapeDtypeStruct((g, n_ap, 2 * p), F32)),
        grid=(g // gb,),
        in_specs=[pl.BlockSpec(kk.shape, lambda i: (0, 0)),
                  grp(1, p), grp(1, p), grp(1, 1), grp(p, h), grp(p, h), grp(h, p), grp(h, p), grp(1, h)],
        out_specs=(grp(CHUNK * h + 2 * p, CHUNK * h), grp(CHUNK * h, 2 * p), grp(n_ap, 2 * p)),
        name="s5_coefficients",
    )(jnp.asarray(kk), lam_re.reshape(g, 1, p), lam_im.reshape(g, 1, p), log_dt.reshape(g, 1, 1), b_re, b_im,
      c_re, c_im, d_skip.reshape(g, 1, h))


PHASES = 8


def _ssm_proj_kernel(h_ref, w_ref, u_ref, z_ref, rows_scr, *, n_b):
    for jj in range(PHASES):
        for b in range(n_b):
            rows_scr[b] = h_ref[b, :, jj, :]
        h = jnp.concatenate([rows_scr[b].astype(BF16) for b in range(n_b)], axis=0)
        r = lax.dot_general(w_ref[...], h, _NT, preferred_element_type=F32)
        u_ref[jj] = r[:SSM_WIDTH].astype(BF16)
        z_ref[jj] = _silu(r[SSM_WIDTH:]).astype(BF16)


def _ssm_projection(h4, w_in):
    n_b, c2, _, d = h4.shape
    lanes = n_b * c2
    halves = CHUNK // PHASES
    w_t = w_in[:, :2 * SSM_WIDTH].T.astype(BF16)
    out = jax.ShapeDtypeStruct((CHUNK, SSM_WIDTH, SUPER * lanes), BF16)
    out_spec = pl.BlockSpec((PHASES, SSM_WIDTH, lanes), lambda j2, hf: (hf, 0, j2))
    return pl.pallas_call(
        functools.partial(_ssm_proj_kernel, n_b=n_b),
        out_shape=(out, out),
        grid=(SUPER, halves),
        in_specs=[pl.BlockSpec((n_b, c2, PHASES, d), lambda j2, hf: (0, 0, j2 * halves + hf, 0)),
                  pl.BlockSpec(w_t.shape, lambda j2, hf: (0, 0))],
        out_specs=(out_spec, out_spec),
        scratch_shapes=[pltpu.VMEM((n_b, c2, d), F32)],
        compiler_params=pltpu.CompilerParams(dimension_semantics=("arbitrary", "arbitrary"),
                                             vmem_limit_bytes=VMEM_LIMIT),
        name="ssm_projection",
    )(h4, w_t)


def _cmul(ar, ai, xr, xi):
    return ar * xr - ai * xi, ar * xi + ai * xr


SCAN_GROUPS = 4


def _s5_group_kernel(x_ref, lhs_ref, wout_ref, ap_ref, y_ref, *, n_b, c2, n_top):
    p, gb = SSM_STATE, SCAN_GROUPS
    rows = CHUNK * SSM_GROUP
    n = SUPER * n_b * c2
    r = [jnp.dot(lhs_ref[gi], x_ref[:, SSM_GROUP * gi:SSM_GROUP * (gi + 1), :].reshape(rows, n),
                 preferred_element_type=F32) for gi in range(gb)]
    c_re = jnp.concatenate([r[gi][rows:rows + p] for gi in range(gb)], axis=0)
    c_im = jnp.concatenate([r[gi][rows + p:] for gi in range(gb)], axis=0)

    def mult(i):
        rep = [jnp.broadcast_to(ap_ref[gi, i:i + 1, :], (2 * p, 2 * p)).T for gi in range(gb)]
        return (jnp.concatenate([m[:p, :c2] for m in rep], axis=0),
                jnp.concatenate([m[p:, :c2] for m in rep], axis=0))

    lane = lax.broadcasted_iota(jnp.int32, (gb * p, c2), 1)

    def shifted(a, sh):
        return jnp.where(lane >= sh, pltpu.roll(a, sh, 1), 0.0)

    pieces = [[None] * n_b for _ in range(SUPER)]
    for b in range(n_b):
        piece = lambda a, j2: a[:, (j2 * n_b + b) * c2:(j2 * n_b + b + 1) * c2]
        a_re, a_im = mult(1)
        e_re = jnp.zeros((gb * p, c2), F32)
        e_im = jnp.zeros((gb * p, c2), F32)
        local = []
        for j2 in range(SUPER):
            local.append((e_re, e_im))
            t_re, t_im = _cmul(a_re, a_im, e_re, e_im)
            e_re, e_im = t_re + piece(c_re, j2), t_im + piece(c_im, j2)
        for i in range(n_top):
            m_re, m_im = mult(SUPER + i)
            t_re, t_im = _cmul(m_re, m_im, shifted(e_re, 1 << i), shifted(e_im, 1 << i))
            e_re, e_im = e_re + t_re, e_im + t_im
        s_re, s_im = shifted(e_re, 1), shifted(e_im, 1)
        for j2 in range(SUPER):
            if j2 == 0:
                pieces[j2][b] = (s_re, s_im)
            else:
                m_re, m_im = mult(j2)
                t_re, t_im = _cmul(m_re, m_im, s_re, s_im)
                pieces[j2][b] = (local[j2][0] + t_re, local[j2][1] + t_im)
    for gi in range(gb):
        sl = slice(gi * p, (gi + 1) * p)
        s_in = jnp.concatenate([jnp.concatenate([pieces[j2][b][0][sl], pieces[j2][b][1][sl]], axis=0)
                                for j2 in range(SUPER) for b in range(n_b)], axis=1)
        y = r[gi][:rows] + jnp.dot(wout_ref[gi], s_in.astype(BF16), preferred_element_type=F32)
        y_ref[gi] = y.reshape(CHUNK, SSM_GROUP, n).astype(y_ref.dtype)


def _s5_scan(u_t, lhs, wout, ap, n_b, c2, n_top):
    g, h = SSM_GROUPS, SSM_GROUP
    n = u_t.shape[-1]
    gb = SCAN_GROUPS
    grp = lambda *shape: pl.BlockSpec((gb,) + shape, lambda i: (i,) + (0,) * len(shape))
    return pl.pallas_call(
        functools.partial(_s5_group_kernel, n_b=n_b, c2=c2, n_top=n_top),
        out_shape=jax.ShapeDtypeStruct((g, CHUNK, h, n), BF16),
        grid=(g // gb,),
        in_specs=[pl.BlockSpec((CHUNK, gb * h, n), lambda i: (0, i, 0)),
                  grp(*lhs.shape[1:]), grp(*wout.shape[1:]), grp(*ap.shape[1:])],
        out_specs=grp(CHUNK, h, n),
        compiler_params=pltpu.CompilerParams(dimension_semantics=("arbitrary",),
                                             vmem_limit_bytes=VMEM_LIMIT),
        name="s5_chunk_scan",
    )(u_t, lhs, wout, ap)


def _glu_kernel(y_ref, zs_ref, w_ref, b_ref, o_ref, *, n_b, c2):
    g, _, h, n = y_ref.shape
    for jj in range(PHASES):
        y = jax.nn.gelu(y_ref[:, jj].reshape(g * h, n).astype(F32))
        t = jnp.dot(w_ref[...], y.astype(BF16), preferred_element_type=F32) + b_ref[...]
        gated = (y * jax.nn.sigmoid(t) * zs_ref[jj].astype(F32)).T
        for b in range(n_b):
            o_ref[b, :, jj, :] = gated[b * c2:(b + 1) * c2]


def _glu(y_t, zs_t, w_glu, b_glu, n_b, c2):
    g, _, h, _ = y_t.shape
    lanes = n_b * c2
    halves = CHUNK // PHASES
    w_t = w_glu.T.astype(BF16)
    return pl.pallas_call(
        functools.partial(_glu_kernel, n_b=n_b, c2=c2),
        out_shape=jax.ShapeDtypeStruct((n_b, c2, SUPER * CHUNK, SSM_WIDTH), F32),
        grid=(SUPER, halves),
        in_specs=[pl.BlockSpec((g, PHASES, h, lanes), lambda j2, hf: (0, hf, 0, j2)),
                  pl.BlockSpec((PHASES, SSM_WIDTH, lanes), lambda j2, hf: (hf, 0, j2)),
                  pl.BlockSpec(w_t.shape, lambda j2, hf: (0, 0)),
                  pl.BlockSpec((SSM_WIDTH, 1), lambda j2, hf: (0, 0))],
        out_specs=pl.BlockSpec((n_b, c2, PHASES, SSM_WIDTH), lambda j2, hf: (0, 0, j2 * halves + hf, 0)),
        compiler_params=pltpu.CompilerParams(dimension_semantics=("arbitrary", "arbitrary"),
                                             vmem_limit_bytes=VMEM_LIMIT),
        name="s5_glu",
    )(y_t, zs_t, w_t, b_glu.reshape(SSM_WIDTH, 1))


def _out_proj_kernel(ys_ref, ym_ref, x_ref, mod_ref, wa_ref, wb_ref, o_ref):
    y = (jnp.dot(ys_ref[...].astype(BF16), wa_ref[...], preferred_element_type=F32)
         + lax.dot_general(ym_ref[...], wb_ref[...], _TN, preferred_element_type=F32))
    gate = mod_ref[...][:, 2 * D_MODEL:]
    o_ref[...] = x_ref[...] + gate * y


def _out_projection(ys, ym, x2, mod3, w_out, seq):
    t, d = x2.shape
    tm = min(1024, seq)
    per_b = seq // tm
    wa = w_out[:SSM_WIDTH].astype(BF16)
    wb = w_out[SSM_WIDTH:].astype(BF16)
    tok = lambda w: pl.BlockSpec((tm, w), lambda i: (i, 0))
    return pl.pallas_call(
        _out_proj_kernel,
        out_shape=jax.ShapeDtypeStruct(x2.shape, F32),
        grid=(t // tm,),
        in_specs=[tok(SSM_WIDTH), pl.BlockSpec((MLA_WIDTH, tm), lambda i: (0, i)), tok(d),
                  pl.BlockSpec((None, 1, 3 * d), lambda i: (i // per_b, 0, 0)),
                  pl.BlockSpec(wa.shape, lambda i: (0, 0)),
                  pl.BlockSpec(wb.shape, lambda i: (0, 0))],
        out_specs=tok(d),
        compiler_params=pltpu.CompilerParams(dimension_semantics=("arbitrary",),
                                             vmem_limit_bytes=VMEM_LIMIT),
        name="output_projection",
    )(ys, ym, x2, mod3, wa, wb)


def kernel(x, c, positions, w_ada, b_ada, norm_g, w_in, log_dt, lam_re, lam_im, b_re, b_im, c_re, c_im, d_skip,
           w_glu, b_glu, q_a_g, w_q_b, kv_a_g, w_kv_b, q_norm_g, k_norm_g, w_out):
    n_b, seq, d = x.shape
    depth = w_ada.shape[0]
    c2 = seq // (CHUNK * SUPER)
    n_top = max(int(math.log2(c2)), 0)
    assert c2 * CHUNK * SUPER == seq and (1 << n_top) == c2
    rope = _rope_tables(positions)
    for l in range(depth):
        mod3 = _modulation(c, w_ada[l], b_ada[l]).reshape(n_b, 1, 3 * d)
        x2 = x.reshape(n_b * seq, d)
        lhs, wout, ap = _s5_coefficients(log_dt[l], lam_re[l], lam_im[l], b_re[l], b_im[l], c_re[l], c_im[l],
                                         d_skip[l], n_top)
        qt, k, vt, zm, h = _mla_projection(x2, mod3, norm_g[l], w_in[l], q_a_g[l], w_q_b[l],
                                           kv_a_g[l], w_kv_b[l], q_norm_g[l], k_norm_g[l], rope, n_b, seq)
        u_t, zs_t = _ssm_projection(h.reshape(n_b, c2, SUPER * CHUNK, d), w_in[l])
        y_t = _s5_scan(u_t, lhs, wout, ap, n_b, c2, n_top)
        ys = _glu(y_t, zs_t, w_glu[l], b_glu[l], n_b, c2).reshape(n_b * seq, SSM_WIDTH)
        score_bound = (1.05 * math.sqrt(QK_HEAD) * math.log2(math.e)
                       * jnp.max(jnp.abs(q_norm_g[l])) * jnp.max(jnp.abs(k_norm_g[l])))
        ym = _attention(qt, k, vt, zm, score_bound)
        x = _out_projection(ys, ym, x2, mod3, w_out[l], seq).reshape(n_b, seq, d)
    return x
```

```python
import functools
import math

import numpy as np
import jax
import jax.numpy as jnp
from jax import lax
from jax.experimental import pallas as pl
from jax.experimental.pallas import tpu as pltpu

F32 = jnp.float32
BF16 = jnp.bfloat16
HIGHEST = lax.Precision.HIGHEST

D_MODEL = 1024
SSM_WIDTH = 512
SSM_GROUP = 16
SSM_GROUPS = 32
SSM_STATE = 64
MLA_HEADS = 8
QK_NOPE = 64
QK_ROPE = 32
QK_HEAD = QK_NOPE + QK_ROPE
V_HEAD = 64
MLA_WIDTH = MLA_HEADS * V_HEAD
Q_LORA = 384
KV_LORA = 256
ROPE_THETA = 10000.0
EPS = 1e-6
NEG_BIG = -1e30

LANES = 128
VMEM_LIMIT = 60 * 1024 * 1024
CHUNK = 16
SUPER = 4
ROPE_HALF = QK_ROPE // 2

_NT = (((1,), (1,)), ((), ()))
_TN = (((0,), (0,)), ((), ()))


def _silu(v):
    return v * jax.nn.sigmoid(v)


def _rms(v, gain):
    return v * lax.rsqrt(jnp.mean(v * v, axis=-1, keepdims=True) + EPS) * gain


def _mod_kernel(c_ref, w_ref, b_ref, o_ref):
    act = _silu(c_ref[...])
    o_ref[...] = jnp.dot(act, w_ref[...], preferred_element_type=F32, precision=HIGHEST) + b_ref[...]


def _modulation(c, w, b):
    n_b, d = c.shape
    rows = 8
    c_pad = jnp.zeros((rows, d), F32).at[:n_b].set(c)
    tn = 512
    out = pl.pallas_call(
        _mod_kernel,
        out_shape=jax.ShapeDtypeStruct((rows, 3 * d), F32),
        grid=(3 * d // tn,),
        in_specs=[pl.BlockSpec((rows, d), lambda n: (0, 0)),
                  pl.BlockSpec((d, tn), lambda n: (0, n)),
                  pl.BlockSpec((1, tn), lambda n: (0, n))],
        out_specs=pl.BlockSpec((rows, tn), lambda n: (0, n)),
        name="adaln_modulation",
    )(c_pad, w, b.reshape(1, -1))
    return out[:n_b]


def _rope_kernel(pos_ref, freq_ref, cos_ref, sin_ref):
    ang = freq_ref[...] * pos_ref[...].astype(F32)
    cos_ref[...] = jnp.cos(ang)
    sin_ref[...] = jnp.sin(ang)


def _rope_tables(positions):
    t = positions.size
    tt = min(t, 2048)
    inv_freq = (ROPE_THETA ** (-np.arange(ROPE_HALF, dtype=np.float64) * 2.0 / QK_ROPE)).astype(np.float32)
    return pl.pallas_call(
        _rope_kernel,
        out_shape=(jax.ShapeDtypeStruct((ROPE_HALF, t), F32),) * 2,
        grid=(t // tt,),
        in_specs=[pl.BlockSpec((1, tt), lambda i: (0, i)),
                  pl.BlockSpec((ROPE_HALF, 1), lambda i: (0, 0))],
        out_specs=(pl.BlockSpec((ROPE_HALF, tt), lambda i: (0, i)),) * 2,
        name="rope_tables",
    )(positions.reshape(1, t), jnp.asarray(inv_freq).reshape(ROPE_HALF, 1))


def _slab_source():
    src = np.full((LANES,), -1, np.int64)
    src[0:16] = QK_NOPE + np.arange(16)
    src[16:64] = np.arange(48)
    src[64:80] = QK_NOPE + 16 + np.arange(16)
    src[80:96] = 48 + np.arange(16)
    return src


def _to_slabs(w, per_head, src):
    k = w.shape[0]
    w3 = w.reshape(k, MLA_HEADS, per_head)
    valid = src >= 0
    cols = jnp.where(valid[None, None, :], w3[:, :, np.clip(src, 0, per_head - 1)], 0.0)
    return cols.reshape(k, MLA_HEADS * LANES)


V_ROWS = V_HEAD + 16


def _mla_proj_kernel(x_ref, mod_ref, ng_ref, w1_ref, wzt_ref, qag_ref, kvag_ref, wqt_ref, wkt_ref, wvt_ref,
                     gq_ref, gk_ref, cost_ref, sint_ref, qt_ref, k_ref, vt_ref, zs_ref, h_ref):
    d = D_MODEL
    tm = x_ref.shape[0]
    x = x_ref[...]
    mod = mod_ref[...]
    shift, scale = mod[:, :d], mod[:, d:2 * d]
    h_ref[...] = _rms(x, ng_ref[...] * (1.0 + scale)) + shift
    h = h_ref[...].astype(BF16)
    p1 = jnp.dot(h, w1_ref[...], preferred_element_type=F32)
    o1, o2 = Q_LORA, Q_LORA + KV_LORA
    cqn = _rms(p1[:, :o1], qag_ref[...]).astype(BF16)
    ckvn = _rms(p1[:, o1:o2], kvag_ref[...]).astype(BF16)
    zs_ref[...] = _silu(lax.dot_general(wzt_ref[...], h, _NT, preferred_element_type=F32)).astype(BF16)

    qt = lax.dot_general(wqt_ref[...], cqn, _NT, preferred_element_type=F32)
    kt = lax.dot_general(wkt_ref[...], ckvn, _NT, preferred_element_type=F32)
    krt = p1[:, o2:].T
    vt = lax.dot_general(wvt_ref[...], ckvn, _NT, preferred_element_type=F32)
    cos_t, sin_t = cost_ref[...], sint_ref[...]
    gq = jnp.concatenate([gq_ref[...]] * (tm // LANES), axis=1)
    gk = jnp.concatenate([gk_ref[...]] * (tm // LANES), axis=1)
    first = lax.broadcasted_iota(jnp.int32, (V_ROWS - V_HEAD, tm), 0) == 0
    ones_rows = jnp.where(first, 1.0, 0.0).astype(BF16)
    r, half = ROPE_HALF, LANES // 2

    def norm_rope(slab, gain):
        ss = jnp.sum(slab * slab, axis=0, keepdims=True)
        n = slab * lax.rsqrt(ss * (1.0 / QK_HEAD) + EPS) * gain
        x1, x2 = n[0:r], n[half:half + r]
        return jnp.concatenate([x1 * cos_t - x2 * sin_t, n[r:half], x2 * cos_t + x1 * sin_t, n[half + r:]], axis=0)

    for head in range(MLA_HEADS):
        rows = slice(LANES * head, LANES * (head + 1))
        qt_ref[head] = norm_rope(qt[rows], gq).astype(BF16)
        k_ref[head] = norm_rope(kt[rows] + krt, gk).T.astype(BF16)
        vt_ref[head, 0:V_HEAD, :] = vt[V_HEAD * head:V_HEAD * (head + 1)].astype(BF16)
        vt_ref[head, V_HEAD:, :] = ones_rows


def _mla_projection(x2, mod3, norm_g, w_in, q_a_g, w_q_b, kv_a_g, w_kv_b, q_norm_g, k_norm_g, rope, n_b, seq):
    t, d = x2.shape
    cos_t, sin_t = rope
    tm = min(512, seq)
    per_b = seq // tm
    o2 = 2 * SSM_WIDTH
    o4, o5 = o2 + Q_LORA + KV_LORA, o2 + Q_LORA + KV_LORA + QK_ROPE
    src = _slab_source()
    valid = src >= 0
    rope_src = np.where(src >= QK_NOPE, src - QK_NOPE, -1)
    nope_src = np.where(valid & (src < QK_NOPE), src, -1)
    wkr = jnp.where((rope_src >= 0)[None, :], w_in[:, o4:o5][:, np.clip(rope_src, 0, QK_ROPE - 1)], 0.0)
    w1 = jnp.concatenate([w_in[:, o2:o4], wkr], axis=1).astype(BF16)
    wzt = w_in[:, o5:].T.astype(BF16)
    wqt = _to_slabs(w_q_b, QK_HEAD, src).T.astype(BF16)
    wkt = _to_slabs(w_kv_b, QK_NOPE + V_HEAD, nope_src).T.astype(BF16)
    wvt = w_kv_b.reshape(KV_LORA, MLA_HEADS, QK_NOPE + V_HEAD)[:, :, QK_NOPE:].reshape(KV_LORA, MLA_WIDTH).T.astype(BF16)
    q_scale = math.log2(math.e) / math.sqrt(QK_HEAD)
    slab_gain = lambda g: jnp.broadcast_to(jnp.where(valid, g[np.clip(src, 0, QK_HEAD - 1)], 0.0)[:, None],
                                           (LANES, LANES))
    gq = slab_gain(q_norm_g * q_scale)
    gk = slab_gain(k_norm_g)
    const = lambda shape: pl.BlockSpec(shape, lambda i: (0,) * len(shape))
    tok = lambda i: (i // per_b, 0, 0, i % per_b)
    return pl.pallas_call(
        _mla_proj_kernel,
        out_shape=(jax.ShapeDtypeStruct((n_b, MLA_HEADS, LANES, seq), BF16),
                   jax.ShapeDtypeStruct((n_b, MLA_HEADS, seq, LANES), BF16),
                   jax.ShapeDtypeStruct((n_b, MLA_HEADS, V_ROWS, seq), BF16),
                   jax.ShapeDtypeStruct((MLA_WIDTH, t), BF16),
                   jax.ShapeDtypeStruct((t, d), F32)),
        grid=(t // tm,),
        in_specs=[pl.BlockSpec((tm, d), lambda i: (i, 0)),
                  pl.BlockSpec((None, 1, 3 * d), lambda i: (i // per_b, 0, 0)),
                  const((1, d)), const(w1.shape), const(wzt.shape), const((1, Q_LORA)), const((1, KV_LORA)),
                  const(wqt.shape), const(wkt.shape), const(wvt.shape),
                  const((LANES, LANES)), const((LANES, LANES)),
                  pl.BlockSpec((ROPE_HALF, tm), lambda i: (0, i)),
                  pl.BlockSpec((ROPE_HALF, tm), lambda i: (0, i))],
        out_specs=(pl.BlockSpec((None, MLA_HEADS, LANES, tm), tok),
                   pl.BlockSpec((None, MLA_HEADS, tm, LANES), lambda i: (i // per_b, 0, i % per_b, 0)),
                   pl.BlockSpec((None, MLA_HEADS, V_ROWS, tm), tok),
                   pl.BlockSpec((MLA_WIDTH, tm), lambda i: (0, i)),
                   pl.BlockSpec((tm, d), lambda i: (i, 0))),
        compiler_params=pltpu.CompilerParams(dimension_semantics=("arbitrary",),
                                             vmem_limit_bytes=VMEM_LIMIT),
        name="mla_projection",
    )(x2, mod3, norm_g.reshape(1, d), w1, wzt, q_a_g.reshape(1, -1), kv_a_g.reshape(1, -1), wqt, wkt, wvt, gq, gk,
      cos_t, sin_t)


ATTN_TQ = 2048
ATTN_TK = 256
HEADS_PER_STEP = 2


def _attn_kernel(qt_ref, k_ref, vt_ref, zs_ref, o_ref, s_scr):
    tq, tk = ATTN_TQ, ATTN_TK
    r = tq // tk
    assert r * tk == tq and r % 2 == 0
    qi = pl.program_id(2)

    def scores(slot, t, lo=0):
        start = pl.multiple_of(t * tk, tk)
        for e in range(HEADS_PER_STEP):
            s_scr[slot, e, :, lo:] = jnp.dot(k_ref[e, pl.ds(start, tk), :], qt_ref[e, :, lo:],
                                             preferred_element_type=F32)

    def process(slot, t, carry, lo=0, diagonal=False):
        start = pl.multiple_of(t * tk, tk)
        out = []
        for e in range(HEADS_PER_STEP):
            m, acc = carry[e]
            s = s_scr[slot, e, :, lo:]
            if diagonal:
                blk = s[:, :tk]
                ok = lax.broadcasted_iota(jnp.int32, blk.shape, 0) <= lax.broadcasted_iota(jnp.int32, blk.shape, 1)
                blk = jnp.where(ok, blk, NEG_BIG)
                s = jnp.concatenate([blk, s[:, tk:]], axis=1) if s.shape[1] > tk else blk
            m_old = m[:, lo:]
            m_new = jnp.maximum(m_old, jnp.max(s, axis=0, keepdims=True))
            p = jnp.exp2(s - m_new).astype(BF16)
            acc_new = (jnp.exp2(m_old - m_new) * acc[:, lo:]
                       + jnp.dot(vt_ref[e, :, pl.ds(start, tk)], p, preferred_element_type=F32))
            if lo:
                m_new = jnp.concatenate([m[:, :lo], m_new], axis=1)
                acc_new = jnp.concatenate([acc[:, :lo], acc_new], axis=1)
            out.append((m_new, acc_new))
        return tuple(out)

    def body(i, carry):
        base = i * r
        for u in range(r):
            scores((u + 1) % 2, base + u + 1)
            carry = process(u % 2, base + u, carry)
        return carry

    init = tuple((jnp.full((1, tq), NEG_BIG, F32), jnp.zeros((V_ROWS, tq), F32)) for _ in range(HEADS_PER_STEP))
    scores(0, 0)
    carry = lax.fori_loop(0, qi, body, init)
    base = qi * r
    for u in range(r):
        if u + 1 < r:
            scores((u + 1) % 2, base + u + 1, lo=(u + 1) * tk)
        carry = process(u % 2, base + u, carry, lo=u * tk, diagonal=True)
    _attn_finish(tuple(c[1] for c in carry), zs_ref, o_ref)


def _attn_finish(accs, zs_ref, o_ref):
    outs = [acc[:V_HEAD] / acc[V_HEAD:V_HEAD + 1] for acc in accs]
    o_ref[...] = (jnp.concatenate(outs, axis=0) * zs_ref[...].astype(F32)).astype(o_ref.dtype)


def _attn_bounded_kernel(qt_ref, k_ref, vt_ref, zs_ref, o_ref, p_scr):
    tq, tk = ATTN_TQ, ATTN_TK
    r = tq // tk
    assert r * tk == tq and r % 2 == 0
    qi = pl.program_id(2)
    q0 = qi * tq

    def probs(slot, t, lo=0, may_cross=False):
        start = pl.multiple_of(t * tk, tk)
        for e in range(HEADS_PER_STEP):
            p = jnp.exp2(jnp.dot(k_ref[e, pl.ds(start, tk), :], qt_ref[e, :, lo:], preferred_element_type=F32))
            if may_cross:
                blk = p[:, :tk]
                key = start + lax.broadcasted_iota(jnp.int32, blk.shape, 0)
                qry = q0 + lo + lax.broadcasted_iota(jnp.int32, blk.shape, 1)
                blk = jnp.where(key <= qry, blk, 0.0)
                p = jnp.concatenate([blk, p[:, tk:]], axis=1) if p.shape[1] > tk else blk
            p_scr[slot, e, :, lo:] = p.astype(BF16)

    def accumulate(slot, t, accs, lo=0):
        start = pl.multiple_of(t * tk, tk)
        out = []
        for e in range(HEADS_PER_STEP):
            pv = jnp.dot(vt_ref[e, :, pl.ds(start, tk)], p_scr[slot, e, :, lo:], preferred_element_type=F32)
            acc = accs[e]
            out.append(acc + pv if lo == 0 else jnp.concatenate([acc[:, :lo], acc[:, lo:] + pv], axis=1))
        return tuple(out)

    def body(i, accs):
        base = i * r
        for u in range(r):
            probs((u + 1) % 2, base + u + 1, may_cross=(u == r - 1))
            accs = accumulate(u % 2, base + u, accs)
        return accs

    probs(0, 0, may_cross=True)
    accs = lax.fori_loop(0, qi, body, tuple(jnp.zeros((V_ROWS, tq), F32) for _ in range(HEADS_PER_STEP)))
    base = qi * r
    for u in range(r):
        if u + 1 < r:
            probs((u + 1) % 2, base + u + 1, lo=(u + 1) * tk, may_cross=True)
        accs = accumulate(u % 2, base + u, accs, lo=u * tk)
    _attn_finish(accs, zs_ref, o_ref)


SCORE_BOUND_LOG2 = 60.0


def _attention(qt, k, vt, zs, score_bound):
    n_b, n_h, seq, _ = k.shape
    tq = ATTN_TQ
    assert seq % tq == 0 and n_h % HEADS_PER_STEP == 0
    nq = seq // tq
    hp = HEADS_PER_STEP
    width = hp * V_HEAD

    def body(bounded_ref, qt_ref, k_ref, vt_ref, zs_ref, o_ref, p_scr, s_scr):
        @pl.when(bounded_ref[0] != 0)
        def _():
            _attn_bounded_kernel(qt_ref, k_ref, vt_ref, zs_ref, o_ref, p_scr)

        @pl.when(bounded_ref[0] == 0)
        def _():
            _attn_kernel(qt_ref, k_ref, vt_ref, zs_ref, o_ref, s_scr)

    tiles = (2, hp, ATTN_TK, tq)
    bounded = (score_bound <= SCORE_BOUND_LOG2).astype(jnp.int32).reshape(1)
    return pl.pallas_call(
        body,
        out_shape=jax.ShapeDtypeStruct(zs.shape, BF16),
        grid_spec=pltpu.PrefetchScalarGridSpec(
            num_scalar_prefetch=1,
            grid=(n_b, n_h // hp, nq),
            in_specs=[pl.BlockSpec((None, hp, LANES, tq), lambda b, h, i, _: (b, h, 0, i)),
                      pl.BlockSpec((None, hp, seq, LANES), lambda b, h, i, _: (b, h, 0, 0)),
                      pl.BlockSpec((None, hp, V_ROWS, seq), lambda b, h, i, _: (b, h, 0, 0)),
                      pl.BlockSpec((width, tq), lambda b, h, i, _: (h, b * nq + i))],
            out_specs=pl.BlockSpec((width, tq), lambda b, h, i, _: (h, b * nq + i)),
            scratch_shapes=[pltpu.VMEM(tiles, BF16), pltpu.VMEM(tiles, F32)]),
        compiler_params=pltpu.CompilerParams(dimension_semantics=("arbitrary", "arbitrary", "arbitrary"),
                                             vmem_limit_bytes=VMEM_LIMIT),
        name="causal_attention",
    )(bounded, qt, k, vt, zs)


def _s5_scan_steps(n_top):
    return [CHUNK * m for m in range(SUPER)] + [CHUNK * SUPER * (1 << i) for i in range(n_top)]


def _s5_exponents(n_top):
    return sorted(set(range(CHUNK + 1)) | set(_s5_scan_steps(n_top)))


def _s5_coef_kernel(kk_ref, *refs, n_top):
    for gi in range(SCAN_GROUPS):
        _s5_coef_one_group(kk_ref, *(r.at[gi] for r in refs), n_top=n_top)


def _s5_coef_one_group(kk_ref, lre_ref, lim_ref, ldt_ref, bre_ref, bim_ref, cre_ref, cim_ref, d_ref,
                       lhs_ref, wout_ref, ap_ref, *, n_top):
    h, p, n = SSM_GROUP, SSM_STATE, CHUNK
    lre, lim = lre_ref[...], lim_ref[...]
    dt = jnp.exp(ldt_ref[...])
    exps = _s5_exponents(n_top)
    kcol = kk_ref[...]
    mag = jnp.exp(kcol * (lre * dt))
    ang = kcol * (lim * dt)
    pow_re, pow_im = mag * jnp.cos(ang), mag * jnp.sin(ang)

    def power(kk):
        i = exps.index(kk)
        return pow_re[i:i + 1], pow_im[i:i + 1]

    lb_re, lb_im = power(1)
    nr, ni = lb_re - 1.0, lb_im
    den = lre * lre + lim * lim
    f_re = (nr * lre + ni * lim) / den
    f_im = (ni * lre - nr * lim) / den
    bre, bim = bre_ref[...].T, bim_ref[...].T
    bb_re = f_re * bre - f_im * bim
    bb_im = f_re * bim + f_im * bre
    cre, cim = cre_ref[...], cim_ref[...]
    cp_re, cp_im = [], []
    for kk in range(n + 1):
        pr, pi = power(kk)
        cp_re.append(cre * pr - cim * pi)
        cp_im.append(cre * pi + cim * pr)

    cpw = [jnp.concatenate([cp_re[kk], -cp_im[kk]], axis=1) for kk in range(n + 1)]
    bb_a = jnp.concatenate([bb_re, bb_im], axis=1)
    kw = lax.dot_general(jnp.concatenate(cpw[:n], axis=0), jnp.concatenate([bb_a] * n, axis=0), _NT,
                         preferred_element_type=F32, precision=HIGHEST)
    lane = lax.broadcasted_iota(jnp.int32, (h, n * h), 1)
    row = lax.broadcasted_iota(jnp.int32, (h, n * h), 0)
    col_blk = lax.shift_right_logical(lane, 4)
    d_tiled = jnp.concatenate([d_ref[...]] * n, axis=1)
    lag = [kw[h * kk:h * (kk + 1)] for kk in range(n)]
    lag[0] = lag[0] + jnp.where((lane & (h - 1)) == row, d_tiled, 0.0)
    rows = []
    for j in range(n):
        acc = jnp.zeros((h, n * h), F32)
        for jp in range(j + 1):
            acc = jnp.where(col_blk == jp, lag[j - jp], acc)
        rows.append(acc)
    lhs_ref[0:n * h, :] = jnp.concatenate(rows, axis=0).astype(lhs_ref.dtype)

    bb_b = jnp.concatenate([-bb_im, bb_re], axis=1)
    win = []
    for j in range(n):
        pr, pi = power(n - 1 - j)
        win.append(jnp.concatenate([pr, pr], axis=1) * bb_a + jnp.concatenate([pi, pi], axis=1) * bb_b)
    lhs_ref[n * h:, :] = jnp.concatenate(win, axis=0).T.astype(lhs_ref.dtype)
    wout_ref[...] = jnp.concatenate(cpw[1:], axis=0).astype(wout_ref.dtype)
    for i, kk in enumerate(_s5_scan_steps(n_top)):
        pr, pi = power(kk)
        ap_ref[i:i + 1, :] = jnp.concatenate([pr, pi], axis=1)


def _s5_coefficients(log_dt, lam_re, lam_im, b_re, b_im, c_re, c_im, d_skip, n_top):
    g, p, h = SSM_GROUPS, SSM_STATE, SSM_GROUP
    n_ap = SUPER + n_top
    gb = SCAN_GROUPS
    grp = lambda *shape: pl.BlockSpec((gb,) + shape, lambda i: (i,) + (0,) * len(shape))
    exps = np.asarray(_s5_exponents(n_top), np.float32)
    kk = np.zeros((-(-exps.size // 8) * 8, 1), np.float32)
    kk[:exps.size, 0] = exps
    return pl.pallas_call(
        functools.partial(_s5_coef_kernel, n_top=n_top),
        out_shape=(jax.ShapeDtypeStruct((g, CHUNK * h + 2 * p, CHUNK * h), BF16),
                   jax.ShapeDtypeStruct((g, CHUNK * h, 2 * p), BF16),
                   jax.ShapeDtypeStruct((g, n_ap, 2 * p), F32)),
        grid=(g // gb,),
        in_specs=[pl.BlockSpec(kk.shape, lambda i: (0, 0)),
                  grp(1, p), grp(1, p), grp(1, 1), grp(p, h), grp(p, h), grp(h, p), grp(h, p), grp(1, h)],
        out_specs=(grp(CHUNK * h + 2 * p, CHUNK * h), grp(CHUNK * h, 2 * p), grp(n_ap, 2 * p)),
        name="s5_coefficients",
    )(jnp.asarray(kk), lam_re.reshape(g, 1, p), lam_im.reshape(g, 1, p), log_dt.reshape(g, 1, 1), b_re, b_im,
      c_re, c_im, d_skip.reshape(g, 1, h))


PHASES = 8


def _ssm_proj_kernel(h_ref, w_ref, u_ref, z_ref, rows_scr, *, n_b):
    for jj in range(PHASES):
        for b in range(n_b):
            rows_scr[b] = h_ref[b, :, jj, :]
        h = jnp.concatenate([rows_scr[b].astype(BF16) for b in range(n_b)], axis=0)
        r = lax.dot_general(w_ref[...], h, _NT, preferred_element_type=F32)
        u_ref[jj] = r[:SSM_WIDTH].astype(BF16)
        z_ref[jj] = _silu(r[SSM_WIDTH:]).astype(BF16)


def _ssm_projection(h4, w_in):
    n_b, c2, _, d = h4.shape
    lanes = n_b * c2
    halves = CHUNK // PHASES
    w_t = w_in[:, :2 * SSM_WIDTH].T.astype(BF16)
    out = jax.ShapeDtypeStruct((CHUNK, SSM_WIDTH, SUPER * lanes), BF16)
    out_spec = pl.BlockSpec((PHASES, SSM_WIDTH, lanes), lambda j2, hf: (hf, 0, j2))
    return pl.pallas_call(
        functools.partial(_ssm_proj_kernel, n_b=n_b),
        out_shape=(out, out),
        grid=(SUPER, halves),
        in_specs=[pl.BlockSpec((n_b, c2, PHASES, d), lambda j2, hf: (0, 0, j2 * halves + hf, 0)),
                  pl.BlockSpec(w_t.shape, lambda j2, hf: (0, 0))],
        out_specs=(out_spec, out_spec),
        scratch_shapes=[pltpu.VMEM((n_b, c2, d), F32)],
        compiler_params=pltpu.CompilerParams(dimension_semantics=("arbitrary", "arbitrary"),
                                             vmem_limit_bytes=VMEM_LIMIT),
        name="ssm_projection",
    )(h4, w_t)


def _cmul(ar, ai, xr, xi):
    return ar * xr - ai * xi, ar * xi + ai * xr


SCAN_GROUPS = 8


def _s5_group_kernel(x_ref, lhs_ref, wout_ref, ap_ref, y_ref, *, n_b, c2, n_top):
    p, gb = SSM_STATE, SCAN_GROUPS
    rows = CHUNK * SSM_GROUP
    n = SUPER * n_b * c2
    r = [jnp.dot(lhs_ref[gi], x_ref[:, SSM_GROUP * gi:SSM_GROUP * (gi + 1), :].reshape(rows, n),
                 preferred_element_type=F32) for gi in range(gb)]
    c_re = jnp.concatenate([r[gi][rows:rows + p] for gi in range(gb)], axis=0)
    c_im = jnp.concatenate([r[gi][rows + p:] for gi in range(gb)], axis=0)

    def mult(i):
        rep = [jnp.broadcast_to(ap_ref[gi, i:i + 1, :], (2 * p, 2 * p)).T for gi in range(gb)]
        return (jnp.concatenate([m[:p, :c2] for m in rep], axis=0),
                jnp.concatenate([m[p:, :c2] for m in rep], axis=0))

    lane = lax.broadcasted_iota(jnp.int32, (gb * p, c2), 1)

    def shifted(a, sh):
        return jnp.where(lane >= sh, pltpu.roll(a, sh, 1), 0.0)

    pieces = [[None] * n_b for _ in range(SUPER)]
    for b in range(n_b):
        piece = lambda a, j2: a[:, (j2 * n_b + b) * c2:(j2 * n_b + b + 1) * c2]
        a_re, a_im = mult(1)
        e_re = jnp.zeros((gb * p, c2), F32)
        e_im = jnp.zeros((gb * p, c2), F32)
        local = []
        for j2 in range(SUPER):
            local.append((e_re, e_im))
            t_re, t_im = _cmul(a_re, a_im, e_re, e_im)
            e_re, e_im = t_re + piece(c_re, j2), t_im + piece(c_im, j2)
        for i in range(n_top):
            m_re, m_im = mult(SUPER + i)
            t_re, t_im = _cmul(m_re, m_im, shifted(e_re, 1 << i), shifted(e_im, 1 << i))
            e_re, e_im = e_re + t_re, e_im + t_im
        s_re, s_im = shifted(e_re, 1), shifted(e_im, 1)
        for j2 in range(SUPER):
            if j2 == 0:
                pieces[j2][b] = (s_re, s_im)
            else:
                m_re, m_im = mult(j2)
                t_re, t_im = _cmul(m_re, m_im, s_re, s_im)
                pieces[j2][b] = (local[j2][0] + t_re, local[j2][1] + t_im)
    for gi in range(gb):
        sl = slice(gi * p, (gi + 1) * p)
        s_in = jnp.concatenate([jnp.concatenate([pieces[j2][b][0][sl], pieces[j2][b][1][sl]], axis=0)
                                for j2 in range(SUPER) for b in range(n_b)], axis=1)
        y = r[gi][:rows] + jnp.dot(wout_ref[gi], s_in.astype(BF16), preferred_element_type=F32)
        y_ref[gi] = y.reshape(CHUNK, SSM_GROUP, n).astype(y_ref.dtype)


def _s5_scan(u_t, lhs, wout, ap, n_b, c2, n_top):
    g, h = SSM_GROUPS, SSM_GROUP
    n = u_t.shape[-1]
    gb = SCAN_GROUPS
    grp = lambda *shape: pl.BlockSpec((gb,) + shape, lambda i: (i,) + (0,) * len(shape))
    return pl.pallas_call(
        functools.partial(_s5_group_kernel, n_b=n_b, c2=c2, n_top=n_top),
        out_shape=jax.ShapeDtypeStruct((g, CHUNK, h, n), BF16),
        grid=(g // gb,),
        in_specs=[pl.BlockSpec((CHUNK, gb * h, n), lambda i: (0, i, 0)),
                  grp(*lhs.shape[1:]), grp(*wout.shape[1:]), grp(*ap.shape[1:])],
        out_specs=grp(CHUNK, h, n),
        compiler_params=pltpu.CompilerParams(dimension_semantics=("arbitrary",),
                                             vmem_limit_bytes=VMEM_LIMIT),
        name="s5_chunk_scan",
    )(u_t, lhs, wout, ap)


def _glu_kernel(y_ref, zs_ref, w_ref, b_ref, o_ref, *, n_b, c2):
    g, _, h, n = y_ref.shape
    for jj in range(PHASES):
        y = jax.nn.gelu(y_ref[:, jj].reshape(g * h, n).astype(F32))
        t = jnp.dot(w_ref[...], y.astype(BF16), preferred_element_type=F32) + b_ref[...]
        gated = (y * jax.nn.sigmoid(t) * zs_ref[jj].astype(F32)).T
        for b in range(n_b):
            o_ref[b, :, jj, :] = gated[b * c2:(b + 1) * c2]


def _glu(y_t, zs_t, w_glu, b_glu, n_b, c2):
    g, _, h, _ = y_t.shape
    lanes = n_b * c2
    halves = CHUNK // PHASES
    w_t = w_glu.T.astype(BF16)
    return pl.pallas_call(
        functools.partial(_glu_kernel, n_b=n_b, c2=c2),
        out_shape=jax.ShapeDtypeStruct((n_b, c2, SUPER * CHUNK, SSM_WIDTH), F32),
        grid=(SUPER, halves),
        in_specs=[pl.BlockSpec((g, PHASES, h, lanes), lambda j2, hf: (0, hf, 0, j2)),
                  pl.BlockSpec((PHASES, SSM_WIDTH, lanes), lambda j2, hf: (hf, 0, j2)),
                  pl.BlockSpec(w_t.shape, lambda j2, hf: (0, 0)),
                  pl.BlockSpec((SSM_WIDTH, 1), lambda j2, hf: (0, 0))],
        out_specs=pl.BlockSpec((n_b, c2, PHASES, SSM_WIDTH), lambda j2, hf: (0, 0, j2 * halves + hf, 0)),
        compiler_params=pltpu.CompilerParams(dimension_semantics=("arbitrary", "arbitrary"),
                                             vmem_limit_bytes=VMEM_LIMIT),
        name="s5_glu",
    )(y_t, zs_t, w_t, b_glu.reshape(SSM_WIDTH, 1))


def _out_proj_kernel(ys_ref, ym_ref, x_ref, mod_ref, wa_ref, wb_ref, o_ref):
    y = (jnp.dot(ys_ref[...].astype(BF16), wa_ref[...], preferred_element_type=F32)
         + lax.dot_general(ym_ref[...], wb_ref[...], _TN, preferred_element_type=F32))
    gate = mod_ref[...][:, 2 * D_MODEL:]
    o_ref[...] = x_ref[...] + gate * y


def _out_projection(ys, ym, x2, mod3, w_out, seq):
    t, d = x2.shape
    tm = min(1024, seq)
    per_b = seq // tm
    wa = w_out[:SSM_WIDTH].astype(BF16)
    wb = w_out[SSM_WIDTH:].astype(BF16)
    tok = lambda w: pl.BlockSpec((tm, w), lambda i: (i, 0))
    return pl.pallas_call(
        _out_proj_kernel,
        out_shape=jax.ShapeDtypeStruct(x2.shape, F32),
        grid=(t // tm,),
        in_specs=[tok(SSM_WIDTH), pl.BlockSpec((MLA_WIDTH, tm), lambda i: (0, i)), tok(d),
                  pl.BlockSpec((None, 1, 3 * d), lambda i: (i // per_b, 0, 0)),
                  pl.BlockSpec(wa.shape, lambda i: (0, 0)),
                  pl.BlockSpec(wb.shape, lambda i: (0, 0))],
        out_specs=tok(d),
        compiler_params=pltpu.CompilerParams(dimension_semantics=("arbitrary",),
                                             vmem_limit_bytes=VMEM_LIMIT),
        name="output_projection",
    )(ys, ym, x2, mod3, wa, wb)


def kernel(x, c, positions, w_ada, b_ada, norm_g, w_in, log_dt, lam_re, lam_im, b_re, b_im, c_re, c_im, d_skip,
           w_glu, b_glu, q_a_g, w_q_b, kv_a_g, w_kv_b, q_norm_g, k_norm_g, w_out):
    n_b, seq, d = x.shape
    depth = w_ada.shape[0]
    c2 = seq // (CHUNK * SUPER)
    n_top = max(int(math.log2(c2)), 0)
    assert c2 * CHUNK * SUPER == seq and (1 << n_top) == c2
    rope = _rope_tables(positions)
    for l in range(depth):
        mod3 = _modulation(c, w_ada[l], b_ada[l]).reshape(n_b, 1, 3 * d)
        x2 = x.reshape(n_b * seq, d)
        lhs, wout, ap = _s5_coefficients(log_dt[l], lam_re[l], lam_im[l], b_re[l], b_im[l], c_re[l], c_im[l],
                                         d_skip[l], n_top)
        qt, k, vt, zm, h = _mla_projection(x2, mod3, norm_g[l], w_in[l], q_a_g[l], w_q_b[l],
                                           kv_a_g[l], w_kv_b[l], q_norm_g[l], k_norm_g[l], rope, n_b, seq)
        u_t, zs_t = _ssm_projection(h.reshape(n_b, c2, SUPER * CHUNK, d), w_in[l])
        y_t = _s5_scan(u_t, lhs, wout, ap, n_b, c2, n_top)
        ys = _glu(y_t, zs_t, w_glu[l], b_glu[l], n_b, c2).reshape(n_b * seq, SSM_WIDTH)
        score_bound = (1.05 * math.sqrt(QK_HEAD) * math.log2(math.e)
                       * jnp.max(jnp.abs(q_norm_g[l])) * jnp.max(jnp.abs(k_norm_g[l])))
        ym = _attention(qt, k, vt, zm, score_bound)
        x = _out_projection(ys, ym, x2, mod3, w_out[l], seq).reshape(n_b, seq, d)
    return x
```

```python
import functools
import math

import numpy as np
import jax
import jax.numpy as jnp
from jax import lax
from jax.experimental import pallas as pl
from jax.experimental.pallas import tpu as pltpu

F32 = jnp.float32
BF16 = jnp.bfloat16
HIGHEST = lax.Precision.HIGHEST

D_MODEL = 1024
SSM_WIDTH = 512
SSM_GROUP = 16
SSM_GROUPS = 32
SSM_STATE = 64
MLA_HEADS = 8
QK_NOPE = 64
QK_ROPE = 32
QK_HEAD = QK_NOPE + QK_ROPE
V_HEAD = 64
MLA_WIDTH = MLA_HEADS * V_HEAD
Q_LORA = 384
KV_LORA = 256
ROPE_THETA = 10000.0
EPS = 1e-6
NEG_BIG = -1e30

LANES = 128
VMEM_LIMIT = 60 * 1024 * 1024
CHUNK = 16
SUPER = 4
ROPE_HALF = QK_ROPE // 2

_NT = (((1,), (1,)), ((), ()))
_TN = (((0,), (0,)), ((), ()))


def _silu(v):
    return v * jax.nn.sigmoid(v)


def _rms(v, gain):
    return v * lax.rsqrt(jnp.mean(v * v, axis=-1, keepdims=True) + EPS) * gain


def _mod_kernel(c_ref, w_ref, b_ref, o_ref):
    act = _silu(c_ref[...])
    o_ref[...] = jnp.dot(act, w_ref[...], preferred_element_type=F32, precision=HIGHEST) + b_ref[...]


def _modulation(c, w, b):
    n_b, d = c.shape
    rows = 8
    c_pad = jnp.zeros((rows, d), F32).at[:n_b].set(c)
    tn = 512
    out = pl.pallas_call(
        _mod_kernel,
        out_shape=jax.ShapeDtypeStruct((rows, 3 * d), F32),
        grid=(3 * d // tn,),
        in_specs=[pl.BlockSpec((rows, d), lambda n: (0, 0)),
                  pl.BlockSpec((d, tn), lambda n: (0, n)),
                  pl.BlockSpec((1, tn), lambda n: (0, n))],
        out_specs=pl.BlockSpec((rows, tn), lambda n: (0, n)),
        name="adaln_modulation",
    )(c_pad, w, b.reshape(1, -1))
    return out[:n_b]


def _rope_kernel(pos_ref, freq_ref, cos_ref, sin_ref):
    ang = freq_ref[...] * pos_ref[...].astype(F32)
    cos_ref[...] = jnp.cos(ang)
    sin_ref[...] = jnp.sin(ang)


def _rope_tables(positions):
    t = positions.size
    tt = min(t, 2048)
    inv_freq = (ROPE_THETA ** (-np.arange(ROPE_HALF, dtype=np.float64) * 2.0 / QK_ROPE)).astype(np.float32)
    return pl.pallas_call(
        _rope_kernel,
        out_shape=(jax.ShapeDtypeStruct((ROPE_HALF, t), F32),) * 2,
        grid=(t // tt,),
        in_specs=[pl.BlockSpec((1, tt), lambda i: (0, i)),
                  pl.BlockSpec((ROPE_HALF, 1), lambda i: (0, 0))],
        out_specs=(pl.BlockSpec((ROPE_HALF, tt), lambda i: (0, i)),) * 2,
        name="rope_tables",
    )(positions.reshape(1, t), jnp.asarray(inv_freq).reshape(ROPE_HALF, 1))


def _slab_source():
    src = np.full((LANES,), -1, np.int64)
    src[0:16] = QK_NOPE + np.arange(16)
    src[16:64] = np.arange(48)
    src[64:80] = QK_NOPE + 16 + np.arange(16)
    src[80:96] = 48 + np.arange(16)
    return src


def _to_slabs(w, per_head, src):
    k = w.shape[0]
    w3 = w.reshape(k, MLA_HEADS, per_head)
    valid = src >= 0
    cols = jnp.where(valid[None, None, :], w3[:, :, np.clip(src, 0, per_head - 1)], 0.0)
    return cols.reshape(k, MLA_HEADS * LANES)


V_ROWS = V_HEAD + 16


def _mla_proj_kernel(x_ref, mod_ref, ng_ref, w1_ref, wzt_ref, qag_ref, kvag_ref, wqt_ref, wkt_ref, wvt_ref,
                     gq_ref, gk_ref, cost_ref, sint_ref, qt_ref, k_ref, vt_ref, zs_ref, h_ref):
    d = D_MODEL
    tm = x_ref.shape[0]
    x = x_ref[...]
    mod = mod_ref[...]
    shift, scale = mod[:, :d], mod[:, d:2 * d]
    h_ref[...] = _rms(x, ng_ref[...] * (1.0 + scale)) + shift
    h = h_ref[...].astype(BF16)
    p1 = jnp.dot(h, w1_ref[...], preferred_element_type=F32)
    o1, o2 = Q_LORA, Q_LORA + KV_LORA
    cqn = _rms(p1[:, :o1], qag_ref[...]).astype(BF16)
    ckvn = _rms(p1[:, o1:o2], kvag_ref[...]).astype(BF16)
    zs_ref[...] = _silu(lax.dot_general(wzt_ref[...], h, _NT, preferred_element_type=F32)).astype(BF16)

    qt = lax.dot_general(wqt_ref[...], cqn, _NT, preferred_element_type=F32)
    kt = lax.dot_general(wkt_ref[...], ckvn, _NT, preferred_element_type=F32)
    krt = p1[:, o2:].T
    vt = lax.dot_general(wvt_ref[...], ckvn, _NT, preferred_element_type=F32)
    cos_t, sin_t = cost_ref[...], sint_ref[...]
    gq = jnp.concatenate([gq_ref[...]] * (tm // LANES), axis=1)
    gk = jnp.concatenate([gk_ref[...]] * (tm // LANES), axis=1)
    first = lax.broadcasted_iota(jnp.int32, (V_ROWS - V_HEAD, tm), 0) == 0
    ones_rows = jnp.where(first, 1.0, 0.0).astype(BF16)
    r, half = ROPE_HALF, LANES // 2

    def norm_rope(slab, gain):
        ss = jnp.sum(slab * slab, axis=0, keepdims=True)
        n = slab * lax.rsqrt(ss * (1.0 / QK_HEAD) + EPS) * gain
        x1, x2 = n[0:r], n[half:half + r]
        return jnp.concatenate([x1 * cos_t - x2 * sin_t, n[r:half], x2 * cos_t + x1 * sin_t, n[half + r:]], axis=0)

    for head in range(MLA_HEADS):
        rows = slice(LANES * head, LANES * (head + 1))
        qt_ref[head] = norm_rope(qt[rows], gq).astype(BF16)
        k_ref[head] = norm_rope(kt[rows] + krt, gk).T.astype(BF16)
        vt_ref[head, 0:V_HEAD, :] = vt[V_HEAD * head:V_HEAD * (head + 1)].astype(BF16)
        vt_ref[head, V_HEAD:, :] = ones_rows


def _mla_projection(x2, mod3, norm_g, w_in, q_a_g, w_q_b, kv_a_g, w_kv_b, q_norm_g, k_norm_g, rope, n_b, seq):
    t, d = x2.shape
    cos_t, sin_t = rope
    tm = min(512, seq)
    per_b = seq // tm
    o2 = 2 * SSM_WIDTH
    o4, o5 = o2 + Q_LORA + KV_LORA, o2 + Q_LORA + KV_LORA + QK_ROPE
    src = _slab_source()
    valid = src >= 0
    rope_src = np.where(src >= QK_NOPE, src - QK_NOPE, -1)
    nope_src = np.where(valid & (src < QK_NOPE), src, -1)
    wkr = jnp.where((rope_src >= 0)[None, :], w_in[:, o4:o5][:, np.clip(rope_src, 0, QK_ROPE - 1)], 0.0)
    w1 = jnp.concatenate([w_in[:, o2:o4], wkr], axis=1).astype(BF16)
    wzt = w_in[:, o5:].T.astype(BF16)
    wqt = _to_slabs(w_q_b, QK_HEAD, src).T.astype(BF16)
    wkt = _to_slabs(w_kv_b, QK_NOPE + V_HEAD, nope_src).T.astype(BF16)
    wvt = w_kv_b.reshape(KV_LORA, MLA_HEADS, QK_NOPE + V_HEAD)[:, :, QK_NOPE:].reshape(KV_LORA, MLA_WIDTH).T.astype(BF16)
    q_scale = math.log2(math.e) / math.sqrt(QK_HEAD)
    slab_gain = lambda g: jnp.broadcast_to(jnp.where(valid, g[np.clip(src, 0, QK_HEAD - 1)], 0.0)[:, None],
                                           (LANES, LANES))
    gq = slab_gain(q_norm_g * q_scale)
    gk = slab_gain(k_norm_g)
    const = lambda shape: pl.BlockSpec(shape, lambda i: (0,) * len(shape))
    tok = lambda i: (i // per_b, 0, 0, i % per_b)
    return pl.pallas_call(
        _mla_proj_kernel,
        out_shape=(jax.ShapeDtypeStruct((n_b, MLA_HEADS, LANES, seq), BF16),
                   jax.ShapeDtypeStruct((n_b, MLA_HEADS, seq, LANES), BF16),
                   jax.ShapeDtypeStruct((n_b, MLA_HEADS, V_ROWS, seq), BF16),
                   jax.ShapeDtypeStruct((MLA_WIDTH, t), BF16),
                   jax.ShapeDtypeStruct((t, d), F32)),
        grid=(t // tm,),
        in_specs=[pl.BlockSpec((tm, d), lambda i: (i, 0)),
                  pl.BlockSpec((None, 1, 3 * d), lambda i: (i // per_b, 0, 0)),
                  const((1, d)), const(w1.shape), const(wzt.shape), const((1, Q_LORA)), const((1, KV_LORA)),
                  const(wqt.shape), const(wkt.shape), const(wvt.shape),
                  const((LANES, LANES)), const((LANES, LANES)),
                  pl.BlockSpec((ROPE_HALF, tm), lambda i: (0, i)),
                  pl.BlockSpec((ROPE_HALF, tm), lambda i: (0, i))],
        out_specs=(pl.BlockSpec((None, MLA_HEADS, LANES, tm), tok),
                   pl.BlockSpec((None, MLA_HEADS, tm, LANES), lambda i: (i // per_b, 0, i % per_b, 0)),
                   pl.BlockSpec((None, MLA_HEADS, V_ROWS, tm), tok),
                   pl.BlockSpec((MLA_WIDTH, tm), lambda i: (0, i)),
                   pl.BlockSpec((tm, d), lambda i: (i, 0))),
        compiler_params=pltpu.CompilerParams(dimension_semantics=("arbitrary",),
                                             vmem_limit_bytes=VMEM_LIMIT),
        name="mla_projection",
    )(x2, mod3, norm_g.reshape(1, d), w1, wzt, q_a_g.reshape(1, -1), kv_a_g.reshape(1, -1), wqt, wkt, wvt, gq, gk,
      cos_t, sin_t)


ATTN_TQ = 2048
ATTN_TK = 256
HEADS_PER_STEP = 2


def _attn_kernel(qt_ref, k_ref, vt_ref, zs_ref, o_ref, s_scr):
    tq, tk = ATTN_TQ, ATTN_TK
    r = tq // tk
    assert r * tk == tq and r % 2 == 0
    qi = pl.program_id(2)

    def scores(slot, t, lo=0):
        start = pl.multiple_of(t * tk, tk)
        for e in range(HEADS_PER_STEP):
            s_scr[slot, e, :, lo:] = jnp.dot(k_ref[e, pl.ds(start, tk), :], qt_ref[e, :, lo:],
                                             preferred_element_type=F32)

    def process(slot, t, carry, lo=0, diagonal=False):
        start = pl.multiple_of(t * tk, tk)
        out = []
        for e in range(HEADS_PER_STEP):
            m, acc = carry[e]
            s = s_scr[slot, e, :, lo:]
            if diagonal:
                blk = s[:, :tk]
                ok = lax.broadcasted_iota(jnp.int32, blk.shape, 0) <= lax.broadcasted_iota(jnp.int32, blk.shape, 1)
                blk = jnp.where(ok, blk, NEG_BIG)
                s = jnp.concatenate([blk, s[:, tk:]], axis=1) if s.shape[1] > tk else blk
            m_old = m[:, lo:]
            m_new = jnp.maximum(m_old, jnp.max(s, axis=0, keepdims=True))
            p = jnp.exp2(s - m_new).astype(BF16)
            acc_new = (jnp.exp2(m_old - m_new) * acc[:, lo:]
                       + jnp.dot(vt_ref[e, :, pl.ds(start, tk)], p, preferred_element_type=F32))
            if lo:
                m_new = jnp.concatenate([m[:, :lo], m_new], axis=1)
                acc_new = jnp.concatenate([acc[:, :lo], acc_new], axis=1)
            out.append((m_new, acc_new))
        return tuple(out)

    def body(i, carry):
        base = i * r
        for u in range(r):
            scores((u + 1) % 2, base + u + 1)
            carry = process(u % 2, base + u, carry)
        return carry

    init = tuple((jnp.full((1, tq), NEG_BIG, F32), jnp.zeros((V_ROWS, tq), F32)) for _ in range(HEADS_PER_STEP))
    scores(0, 0)
    carry = lax.fori_loop(0, qi, body, init)
    base = qi * r
    for u in range(r):
        if u + 1 < r:
            scores((u + 1) % 2, base + u + 1, lo=(u + 1) * tk)
        carry = process(u % 2, base + u, carry, lo=u * tk, diagonal=True)
    _attn_finish(tuple(c[1] for c in carry), zs_ref, o_ref)


def _attn_finish(accs, zs_ref, o_ref):
    outs = [acc[:V_HEAD] / acc[V_HEAD:V_HEAD + 1] for acc in accs]
    o_ref[...] = (jnp.concatenate(outs, axis=0) * zs_ref[...].astype(F32)).astype(o_ref.dtype)


def _attn_bounded_kernel(qt_ref, k_ref, vt_ref, zs_ref, o_ref, p_scr):
    tq, tk = ATTN_TQ, ATTN_TK
    r = tq // tk
    assert r * tk == tq and r % 2 == 0
    qi = pl.program_id(2)
    q0 = qi * tq

    def probs(slot, t, lo=0, may_cross=False):
        start = pl.multiple_of(t * tk, tk)
        for e in range(HEADS_PER_STEP):
            p = jnp.exp2(jnp.dot(k_ref[e, pl.ds(start, tk), :], qt_ref[e, :, lo:], preferred_element_type=F32))
            if may_cross:
                blk = p[:, :tk]
                key = start + lax.broadcasted_iota(jnp.int32, blk.shape, 0)
                qry = q0 + lo + lax.broadcasted_iota(jnp.int32, blk.shape, 1)
                blk = jnp.where(key <= qry, blk, 0.0)
                p = jnp.concatenate([blk, p[:, tk:]], axis=1) if p.shape[1] > tk else blk
            p_scr[slot, e, :, lo:] = p.astype(BF16)

    def accumulate(slot, t, accs, lo=0):
        start = pl.multiple_of(t * tk, tk)
        out = []
        for e in range(HEADS_PER_STEP):
            pv = jnp.dot(vt_ref[e, :, pl.ds(start, tk)], p_scr[slot, e, :, lo:], preferred_element_type=F32)
            acc = accs[e]
            out.append(acc + pv if lo == 0 else jnp.concatenate([acc[:, :lo], acc[:, lo:] + pv], axis=1))
        return tuple(out)

    def body(i, accs):
        base = i * r
        for u in range(r):
            probs((u + 1) % 2, base + u + 1, may_cross=(u == r - 1))
            accs = accumulate(u % 2, base + u, accs)
        return accs

    probs(0, 0, may_cross=True)
    accs = lax.fori_loop(0, qi, body, tuple(jnp.zeros((V_ROWS, tq), F32) for _ in range(HEADS_PER_STEP)))
    base = qi * r
    for u in range(r):
        if u + 1 < r:
            probs((u + 1) % 2, base + u + 1, lo=(u + 1) * tk, may_cross=True)
        accs = accumulate(u % 2, base + u, accs, lo=u * tk)
    _attn_finish(accs, zs_ref, o_ref)


SCORE_BOUND_LOG2 = 60.0


def _attention(qt, k, vt, zs, score_bound):
    n_b, n_h, seq, _ = k.shape
    tq = ATTN_TQ
    assert seq % tq == 0 and n_h % HEADS_PER_STEP == 0
    nq = seq // tq
    hp = HEADS_PER_STEP
    width = hp * V_HEAD

    def call(body, scratch):
        return pl.pallas_call(
            body,
            out_shape=jax.ShapeDtypeStruct(zs.shape, BF16),
            grid=(n_b, n_h // hp, nq),
            in_specs=[pl.BlockSpec((None, hp, LANES, tq), lambda b, h, i: (b, h, 0, i)),
                      pl.BlockSpec((None, hp, seq, LANES), lambda b, h, i: (b, h, 0, 0)),
                      pl.BlockSpec((None, hp, V_ROWS, seq), lambda b, h, i: (b, h, 0, 0)),
                      pl.BlockSpec((width, tq), lambda b, h, i: (h, b * nq + i))],
            out_specs=pl.BlockSpec((width, tq), lambda b, h, i: (h, b * nq + i)),
            scratch_shapes=scratch,
            compiler_params=pltpu.CompilerParams(dimension_semantics=("arbitrary", "arbitrary", "arbitrary"),
                                                 vmem_limit_bytes=VMEM_LIMIT),
            name="causal_attention",
        )(qt, k, vt, zs)

    tiles = (2, hp, ATTN_TK, tq)
    return lax.cond(score_bound <= SCORE_BOUND_LOG2,
                    lambda: call(_attn_bounded_kernel, [pltpu.VMEM(tiles, BF16)]),
                    lambda: call(_attn_kernel, [pltpu.VMEM(tiles, F32)]))


def _s5_scan_steps(n_top):
    return [CHUNK * m for m in range(SUPER)] + [CHUNK * SUPER * (1 << i) for i in range(n_top)]


def _s5_exponents(n_top):
    return sorted(set(range(CHUNK + 1)) | set(_s5_scan_steps(n_top)))


def _s5_coef_kernel(kk_ref, *refs, n_top):
    for gi in range(SCAN_GROUPS):
        _s5_coef_one_group(kk_ref, *(r.at[gi] for r in refs), n_top=n_top)


def _s5_coef_one_group(kk_ref, lre_ref, lim_ref, ldt_ref, bre_ref, bim_ref, cre_ref, cim_ref, d_ref,
                       lhs_ref, wout_ref, ap_ref, *, n_top):
    h, p, n = SSM_GROUP, SSM_STATE, CHUNK
    lre, lim = lre_ref[...], lim_ref[...]
    dt = jnp.exp(ldt_ref[...])
    exps = _s5_exponents(n_top)
    kcol = kk_ref[...]
    mag = jnp.exp(kcol * (lre * dt))
    ang = kcol * (lim * dt)
    pow_re, pow_im = mag * jnp.cos(ang), mag * jnp.sin(ang)

    def power(kk):
        i = exps.index(kk)
        return pow_re[i:i + 1], pow_im[i:i + 1]

    lb_re, lb_im = power(1)
    nr, ni = lb_re - 1.0, lb_im
    den = lre * lre + lim * lim
    f_re = (nr * lre + ni * lim) / den
    f_im = (ni * lre - nr * lim) / den
    bre, bim = bre_ref[...].T, bim_ref[...].T
    bb_re = f_re * bre - f_im * bim
    bb_im = f_re * bim + f_im * bre
    cre, cim = cre_ref[...], cim_ref[...]
    cp_re, cp_im = [], []
    for kk in range(n + 1):
        pr, pi = power(kk)
        cp_re.append(cre * pr - cim * pi)
        cp_im.append(cre * pi + cim * pr)

    cpw = [jnp.concatenate([cp_re[kk], -cp_im[kk]], axis=1) for kk in range(n + 1)]
    bb_a = jnp.concatenate([bb_re, bb_im], axis=1)
    kw = lax.dot_general(jnp.concatenate(cpw[:n], axis=0), jnp.concatenate([bb_a] * n, axis=0), _NT,
                         preferred_element_type=F32, precision=HIGHEST)
    lane = lax.broadcasted_iota(jnp.int32, (h, n * h), 1)
    row = lax.broadcasted_iota(jnp.int32, (h, n * h), 0)
    col_blk = lax.shift_right_logical(lane, 4)
    d_tiled = jnp.concatenate([d_ref[...]] * n, axis=1)
    lag = [kw[h * kk:h * (kk + 1)] for kk in range(n)]
    lag[0] = lag[0] + jnp.where((lane & (h - 1)) == row, d_tiled, 0.0)
    rows = []
    for j in range(n):
        acc = jnp.zeros((h, n * h), F32)
        for jp in range(j + 1):
            acc = jnp.where(col_blk == jp, lag[j - jp], acc)
        rows.append(acc)
    lhs_ref[0:n * h, :] = jnp.concatenate(rows, axis=0).astype(lhs_ref.dtype)

    bb_b = jnp.concatenate([-bb_im, bb_re], axis=1)
    win = []
    for j in range(n):
        pr, pi = power(n - 1 - j)
        win.append(jnp.concatenate([pr, pr], axis=1) * bb_a + jnp.concatenate([pi, pi], axis=1) * bb_b)
    lhs_ref[n * h:, :] = jnp.concatenate(win, axis=0).T.astype(lhs_ref.dtype)
    wout_ref[...] = jnp.concatenate(cpw[1:], axis=0).astype(wout_ref.dtype)
    for i, kk in enumerate(_s5_scan_steps(n_top)):
        pr, pi = power(kk)
        ap_ref[i:i + 1, :] = jnp.concatenate([pr, pi], axis=1)


def _s5_coefficients(log_dt, lam_re, lam_im, b_re, b_im, c_re, c_im, d_skip, n_top):
    g, p, h = SSM_GROUPS, SSM_STATE, SSM_GROUP
    n_ap = SUPER + n_top
    gb = SCAN_GROUPS
    grp = lambda *shape: pl.BlockSpec((gb,) + shape, lambda i: (i,) + (0,) * len(shape))
    exps = np.asarray(_s5_exponents(n_top), np.float32)
    kk = np.zeros((-(-exps.size // 8) * 8, 1), np.float32)
    kk[:exps.size, 0] = exps
    return pl.pallas_call(
        functools.partial(_s5_coef_kernel, n_top=n_top),
        out_shape=(jax.ShapeDtypeStruct((g, CHUNK * h + 2 * p, CHUNK * h), BF16),
                   jax.ShapeDtypeStruct((g, CHUNK * h, 2 * p), BF16),
                   jax.ShapeDtypeStruct((g, n_ap, 2 * p), F32)),
        grid=(g // gb,),
        in_specs=[pl.BlockSpec(kk.shape, lambda i: (0, 0)),
                  grp(1, p), grp(1, p), grp(1, 1), grp(p, h), grp(p, h), grp(h, p), grp(h, p), grp(1, h)],
        out_specs=(grp(CHUNK * h + 2 * p, CHUNK * h), grp(CHUNK * h, 2 * p), grp(n_ap, 2 * p)),
        name="s5_coefficients",
    )(jnp.asarray(kk), lam_re.reshape(g, 1, p), lam_im.reshape(g, 1, p), log_dt.reshape(g, 1, 1), b_re, b_im,
      c_re, c_im, d_skip.reshape(g, 1, h))


PHASES = 8


def _ssm_proj_kernel(h_ref, w_ref, u_ref, z_ref, rows_scr, *, n_b):
    for jj in range(PHASES):
        for b in range(n_b):
            rows_scr[b] = h_ref[b, :, jj, :]
        h = jnp.concatenate([rows_scr[b].astype(BF16) for b in range(n_b)], axis=0)
        r = lax.dot_general(w_ref[...], h, _NT, preferred_element_type=F32)
        u_ref[jj] = r[:SSM_WIDTH].astype(BF16)
        z_ref[jj] = _silu(r[SSM_WIDTH:]).astype(BF16)


def _ssm_projection(h4, w_in):
    n_b, c2, _, d = h4.shape
    lanes = n_b * c2
    halves = CHUNK // PHASES
    w_t = w_in[:, :2 * SSM_WIDTH].T.astype(BF16)
    out = jax.ShapeDtypeStruct((CHUNK, SSM_WIDTH, SUPER * lanes), BF16)
    out_spec = pl.BlockSpec((PHASES, SSM_WIDTH, lanes), lambda j2, hf: (hf, 0, j2))
    return pl.pallas_call(
        functools.partial(_ssm_proj_kernel, n_b=n_b),
        out_shape=(out, out),
        grid=(SUPER, halves),
        in_specs=[pl.BlockSpec((n_b, c2, PHASES, d), lambda j2, hf: (0, 0, j2 * halves + hf, 0)),
                  pl.BlockSpec(w_t.shape, lambda j2, hf: (0, 0))],
        out_specs=(out_spec, out_spec),
        scratch_shapes=[pltpu.VMEM((n_b, c2, d), F32)],
        compiler_params=pltpu.CompilerParams(dimension_semantics=("arbitrary", "arbitrary"),
                                             vmem_limit_bytes=VMEM_LIMIT),
        name="ssm_projection",
    )(h4, w_t)


def _cmul(ar, ai, xr, xi):
    return ar * xr - ai * xi, ar * xi + ai * xr


SCAN_GROUPS = 8


def _s5_group_kernel(x_ref, lhs_ref, wout_ref, ap_ref, y_ref, *, n_b, c2, n_top):
    p, gb = SSM_STATE, SCAN_GROUPS
    rows = CHUNK * SSM_GROUP
    n = SUPER * n_b * c2
    r = [jnp.dot(lhs_ref[gi], x_ref[:, SSM_GROUP * gi:SSM_GROUP * (gi + 1), :].reshape(rows, n),
                 preferred_element_type=F32) for gi in range(gb)]
    c_re = jnp.concatenate([r[gi][rows:rows + p] for gi in range(gb)], axis=0)
    c_im = jnp.concatenate([r[gi][rows + p:] for gi in range(gb)], axis=0)

    def mult(i):
        rep = [jnp.broadcast_to(ap_ref[gi, i:i + 1, :], (2 * p, 2 * p)).T for gi in range(gb)]
        return (jnp.concatenate([m[:p, :c2] for m in rep], axis=0),
                jnp.concatenate([m[p:, :c2] for m in rep], axis=0))

    lane = lax.broadcasted_iota(jnp.int32, (gb * p, c2), 1)

    def shifted(a, sh):
        return jnp.where(lane >= sh, pltpu.roll(a, sh, 1), 0.0)

    pieces = [[None] * n_b for _ in range(SUPER)]
    for b in range(n_b):
        piece = lambda a, j2: a[:, (j2 * n_b + b) * c2:(j2 * n_b + b + 1) * c2]
        a_re, a_im = mult(1)
        e_re = jnp.zeros((gb * p, c2), F32)
        e_im = jnp.zeros((gb * p, c2), F32)
        local = []
        for j2 in range(SUPER):
            local.append((e_re, e_im))
            t_re, t_im = _cmul(a_re, a_im, e_re, e_im)
            e_re, e_im = t_re + piece(c_re, j2), t_im + piece(c_im, j2)
        for i in range(n_top):
            m_re, m_im = mult(SUPER + i)
            t_re, t_im = _cmul(m_re, m_im, shifted(e_re, 1 << i), shifted(e_im, 1 << i))
            e_re, e_im = e_re + t_re, e_im + t_im
        s_re, s_im = shifted(e_re, 1), shifted(e_im, 1)
        for j2 in range(SUPER):
            if j2 == 0:
                pieces[j2][b] = (s_re, s_im)
            else:
                m_re, m_im = mult(j2)
                t_re, t_im = _cmul(m_re, m_im, s_re, s_im)
                pieces[j2][b] = (local[j2][0] + t_re, local[j2][1] + t_im)
    for gi in range(gb):
        sl = slice(gi * p, (gi + 1) * p)
        s_in = jnp.concatenate([jnp.concatenate([pieces[j2][b][0][sl], pieces[j2][b][1][sl]], axis=0)
                                for j2 in range(SUPER) for b in range(n_b)], axis=1)
        y = r[gi][:rows] + jnp.dot(wout_ref[gi], s_in.astype(BF16), preferred_element_type=F32)
        y_ref[gi] = y.reshape(CHUNK, SSM_GROUP, n).astype(y_ref.dtype)


def _s5_scan(u_t, lhs, wout, ap, n_b, c2, n_top):
    g, h = SSM_GROUPS, SSM_GROUP
    n = u_t.shape[-1]
    gb = SCAN_GROUPS
    grp = lambda *shape: pl.BlockSpec((gb,) + shape, lambda i: (i,) + (0,) * len(shape))
    return pl.pallas_call(
        functools.partial(_s5_group_kernel, n_b=n_b, c2=c2, n_top=n_top),
        out_shape=jax.ShapeDtypeStruct((g, CHUNK, h, n), BF16),
        grid=(g // gb,),
        in_specs=[pl.BlockSpec((CHUNK, gb * h, n), lambda i: (0, i, 0)),
                  grp(*lhs.shape[1:]), grp(*wout.shape[1:]), grp(*ap.shape[1:])],
        out_specs=grp(CHUNK, h, n),
        compiler_params=pltpu.CompilerParams(dimension_semantics=("arbitrary",),
                                             vmem_limit_bytes=VMEM_LIMIT),
        name="s5_chunk_scan",
    )(u_t, lhs, wout, ap)


def _glu_kernel(y_ref, zs_ref, w_ref, b_ref, o_ref, *, n_b, c2):
    g, _, h, n = y_ref.shape
    for jj in range(PHASES):
        y = jax.nn.gelu(y_ref[:, jj].reshape(g * h, n).astype(F32))
        t = jnp.dot(w_ref[...], y.astype(BF16), preferred_element_type=F32) + b_ref[...]
        gated = (y * jax.nn.sigmoid(t) * zs_ref[jj].astype(F32)).T
        for b in range(n_b):
            o_ref[b, :, jj, :] = gated[b * c2:(b + 1) * c2]


def _glu(y_t, zs_t, w_glu, b_glu, n_b, c2):
    g, _, h, _ = y_t.shape
    lanes = n_b * c2
    halves = CHUNK // PHASES
    w_t = w_glu.T.astype(BF16)
    return pl.pallas_call(
        functools.partial(_glu_kernel, n_b=n_b, c2=c2),
        out_shape=jax.ShapeDtypeStruct((n_b, c2, SUPER * CHUNK, SSM_WIDTH), F32),
        grid=(SUPER, halves),
        in_specs=[pl.BlockSpec((g, PHASES, h, lanes), lambda j2, hf: (0, hf, 0, j2)),
                  pl.BlockSpec((PHASES, SSM_WIDTH, lanes), lambda j2, hf: (hf, 0, j2)),
                  pl.BlockSpec(w_t.shape, lambda j2, hf: (0, 0)),
                  pl.BlockSpec((SSM_WIDTH, 1), lambda j2, hf: (0, 0))],
        out_specs=pl.BlockSpec((n_b, c2, PHASES, SSM_WIDTH), lambda j2, hf: (0, 0, j2 * halves + hf, 0)),
        compiler_params=pltpu.CompilerParams(dimension_semantics=("arbitrary", "arbitrary"),
                                             vmem_limit_bytes=VMEM_LIMIT),
        name="s5_glu",
    )(y_t, zs_t, w_t, b_glu.reshape(SSM_WIDTH, 1))


def _out_proj_kernel(ys_ref, ym_ref, x_ref, mod_ref, wa_ref, wb_ref, o_ref):
    y = (jnp.dot(ys_ref[...].astype(BF16), wa_ref[...], preferred_element_type=F32)
         + lax.dot_general(ym_ref[...], wb_ref[...], _TN, preferred_element_type=F32))
    gate = mod_ref[...][:, 2 * D_MODEL:]
    o_ref[...] = x_ref[...] + gate * y


def _out_projection(ys, ym, x2, mod3, w_out, seq):
    t, d = x2.shape
    tm = min(1024, seq)
    per_b = seq // tm
    wa = w_out[:SSM_WIDTH].astype(BF16)
    wb = w_out[SSM_WIDTH:].astype(BF16)
    tok = lambda w: pl.BlockSpec((tm, w), lambda i: (i, 0))
    return pl.pallas_call(
        _out_proj_kernel,
        out_shape=jax.ShapeDtypeStruct(x2.shape, F32),
        grid=(t // tm,),
        in_specs=[tok(SSM_WIDTH), pl.BlockSpec((MLA_WIDTH, tm), lambda i: (0, i)), tok(d),
                  pl.BlockSpec((None, 1, 3 * d), lambda i: (i // per_b, 0, 0)),
                  pl.BlockSpec(wa.shape, lambda i: (0, 0)),
                  pl.BlockSpec(wb.shape, lambda i: (0, 0))],
        out_specs=tok(d),
        compiler_params=pltpu.CompilerParams(dimension_semantics=("arbitrary",),
                                             vmem_limit_bytes=VMEM_LIMIT),
        name="output_projection",
    )(ys, ym, x2, mod3, wa, wb)


def kernel(x, c, positions, w_ada, b_ada, norm_g, w_in, log_dt, lam_re, lam_im, b_re, b_im, c_re, c_im, d_skip,
           w_glu, b_glu, q_a_g, w_q_b, kv_a_g, w_kv_b, q_norm_g, k_norm_g, w_out):
    n_b, seq, d = x.shape
    depth = w_ada.shape[0]
    c2 = seq // (CHUNK * SUPER)
    n_top = max(int(math.log2(c2)), 0)
    assert c2 * CHUNK * SUPER == seq and (1 << n_top) == c2
    rope = _rope_tables(positions)
    for l in range(depth):
        mod3 = _modulation(c, w_ada[l], b_ada[l]).reshape(n_b, 1, 3 * d)
        x2 = x.reshape(n_b * seq, d)
        lhs, wout, ap = _s5_coefficients(log_dt[l], lam_re[l], lam_im[l], b_re[l], b_im[l], c_re[l], c_im[l],
                                         d_skip[l], n_top)
        qt, k, vt, zm, h = _mla_projection(x2, mod3, norm_g[l], w_in[l], q_a_g[l], w_q_b[l],
                                           kv_a_g[l], w_kv_b[l], q_norm_g[l], k_norm_g[l], rope, n_b, seq)
        score_bound = (1.05 * math.sqrt(QK_HEAD) * math.log2(math.e)
                       * jnp.max(jnp.abs(q_norm_g[l])) * jnp.max(jnp.abs(k_norm_g[l])))
        ym = _attention(qt, k, vt, zm, score_bound)
        u_t, zs_t = _ssm_projection(h.reshape(n_b, c2, SUPER * CHUNK, d), w_in[l])
        y_t = _s5_scan(u_t, lhs, wout, ap, n_b, c2, n_top)
        ys = _glu(y_t, zs_t, w_glu[l], b_glu[l], n_b, c2).reshape(n_b * seq, SSM_WIDTH)
        x = _out_projection(ys, ym, x2, mod3, w_out[l], seq).reshape(n_b, seq, d)
    return x
```

```python
import functools
import math

import numpy as np
import jax
import jax.numpy as jnp
from jax import lax
from jax.experimental import pallas as pl
from jax.experimental.pallas import tpu as pltpu

F32 = jnp.float32
BF16 = jnp.bfloat16
HIGHEST = lax.Precision.HIGHEST

D_MODEL = 1024
SSM_WIDTH = 512
SSM_GROUP = 16
SSM_GROUPS = 32
SSM_STATE = 64
MLA_HEADS = 8
QK_NOPE = 64
QK_ROPE = 32
QK_HEAD = QK_NOPE + QK_ROPE
V_HEAD = 64
MLA_WIDTH = MLA_HEADS * V_HEAD
Q_LORA = 384
KV_LORA = 256
ROPE_THETA = 10000.0
EPS = 1e-6
NEG_BIG = -1e30

LANES = 128
VMEM_LIMIT = 60 * 1024 * 1024
CHUNK = 16
SUPER = 4
ROPE_HALF = QK_ROPE // 2

_NT = (((1,), (1,)), ((), ()))
_TN = (((0,), (0,)), ((), ()))


def _silu(v):
    return v * jax.nn.sigmoid(v)


def _rms(v, gain):
    return v * lax.rsqrt(jnp.mean(v * v, axis=-1, keepdims=True) + EPS) * gain


def _mod_kernel(c_ref, w_ref, b_ref, o_ref, *, n_b):
    rows, d = c_ref.shape
    tn = w_ref.shape[1]
    act_t = _silu(c_ref[...]).T
    out = []
    for b in range(n_b):
        col = jnp.broadcast_to(act_t[:, b:b + 1], (d, LANES))
        out.append(jnp.concatenate([jnp.sum(w_ref[:, j:j + LANES] * col, axis=0, keepdims=True)
                                    for j in range(0, tn, LANES)], axis=1))
    out.append(jnp.zeros((rows - n_b, tn), F32))
    o_ref[...] = jnp.concatenate(out, axis=0) + b_ref[...]


def _modulation(c, w, b):
    n_b, d = c.shape
    rows = 8
    c_pad = jnp.zeros((rows, d), F32).at[:n_b].set(c)
    tn = 512
    out = pl.pallas_call(
        functools.partial(_mod_kernel, n_b=n_b),
        out_shape=jax.ShapeDtypeStruct((rows, 3 * d), F32),
        grid=(3 * d // tn,),
        in_specs=[pl.BlockSpec((rows, d), lambda n: (0, 0)),
                  pl.BlockSpec((d, tn), lambda n: (0, n)),
                  pl.BlockSpec((1, tn), lambda n: (0, n))],
        out_specs=pl.BlockSpec((rows, tn), lambda n: (0, n)),
        name="adaln_modulation",
    )(c_pad, w, b.reshape(1, -1))
    return out[:n_b]


def _rope_frequencies():
    inv_freq = ROPE_THETA ** (-np.arange(ROPE_HALF, dtype=np.float64) * 2.0 / QK_ROPE)
    return jnp.asarray(inv_freq.astype(np.float32)).reshape(ROPE_HALF, 1)


def _slab_source():
    src = np.full((LANES,), -1, np.int64)
    src[0:16] = QK_NOPE + np.arange(16)
    src[16:64] = np.arange(48)
    src[64:80] = QK_NOPE + 16 + np.arange(16)
    src[80:96] = 48 + np.arange(16)
    return src


def _to_slabs(w, per_head, src):
    k = w.shape[0]
    w3 = w.reshape(k, MLA_HEADS, per_head)
    valid = src >= 0
    cols = jnp.where(valid[None, None, :], w3[:, :, np.clip(src, 0, per_head - 1)], 0.0)
    return cols.reshape(k, MLA_HEADS * LANES)


V_ROWS = V_HEAD + 16


def _mla_proj_kernel(x_ref, mod_ref, ng_ref, w1_ref, wzt_ref, qag_ref, kvag_ref, wqt_ref, wkt_ref, wvt_ref,
                     gq_ref, gk_ref, pos_ref, freq_ref, qt_ref, k_ref, vt_ref, zs_ref, h_ref):
    d = D_MODEL
    tm = x_ref.shape[0]
    x = x_ref[...]
    mod = mod_ref[...]
    shift, scale = mod[:, :d], mod[:, d:2 * d]
    h_ref[...] = _rms(x, ng_ref[...] * (1.0 + scale)) + shift
    h = h_ref[...].astype(BF16)
    p1 = jnp.dot(h, w1_ref[...], preferred_element_type=F32)
    o1, o2 = Q_LORA, Q_LORA + KV_LORA
    cqn = _rms(p1[:, :o1], qag_ref[...]).astype(BF16)
    ckvn = _rms(p1[:, o1:o2], kvag_ref[...]).astype(BF16)
    zs_ref[...] = _silu(lax.dot_general(wzt_ref[...], h, _NT, preferred_element_type=F32)).astype(BF16)

    qt = lax.dot_general(wqt_ref[...], cqn, _NT, preferred_element_type=F32)
    kt = lax.dot_general(wkt_ref[...], ckvn, _NT, preferred_element_type=F32)
    krt = p1[:, o2:].T
    vt = lax.dot_general(wvt_ref[...], ckvn, _NT, preferred_element_type=F32)
    ang = freq_ref[...] * pos_ref[...].astype(F32)
    cos_t, sin_t = jnp.cos(ang), jnp.sin(ang)
    gq = jnp.concatenate([gq_ref[...]] * (tm // LANES), axis=1)
    gk = jnp.concatenate([gk_ref[...]] * (tm // LANES), axis=1)
    first = lax.broadcasted_iota(jnp.int32, (V_ROWS - V_HEAD, tm), 0) == 0
    ones_rows = jnp.where(first, 1.0, 0.0).astype(BF16)
    r, half = ROPE_HALF, LANES // 2

    def norm_rope(slab, gain):
        ss = jnp.sum(slab * slab, axis=0, keepdims=True)
        n = slab * lax.rsqrt(ss * (1.0 / QK_HEAD) + EPS) * gain
        x1, x2 = n[0:r], n[half:half + r]
        return jnp.concatenate([x1 * cos_t - x2 * sin_t, n[r:half], x2 * cos_t + x1 * sin_t, n[half + r:]], axis=0)

    for head in range(MLA_HEADS):
        rows = slice(LANES * head, LANES * (head + 1))
        qt_ref[head] = norm_rope(qt[rows], gq).astype(BF16)
        k_ref[head] = norm_rope(kt[rows] + krt, gk).T.astype(BF16)
        vt_ref[head, 0:V_HEAD, :] = vt[V_HEAD * head:V_HEAD * (head + 1)].astype(BF16)
        vt_ref[head, V_HEAD:, :] = ones_rows


def _mla_projection(x2, mod3, norm_g, w_in, q_a_g, w_q_b, kv_a_g, w_kv_b, q_norm_g, k_norm_g, positions, n_b, seq):
    t, d = x2.shape
    tm = min(512, seq)
    per_b = seq // tm
    o2 = 2 * SSM_WIDTH
    o4, o5 = o2 + Q_LORA + KV_LORA, o2 + Q_LORA + KV_LORA + QK_ROPE
    src = _slab_source()
    valid = src >= 0
    rope_src = np.where(src >= QK_NOPE, src - QK_NOPE, -1)
    nope_src = np.where(valid & (src < QK_NOPE), src, -1)
    wkr = jnp.where((rope_src >= 0)[None, :], w_in[:, o4:o5][:, np.clip(rope_src, 0, QK_ROPE - 1)], 0.0)
    w1 = jnp.concatenate([w_in[:, o2:o4], wkr], axis=1).astype(BF16)
    wzt = w_in[:, o5:].T.astype(BF16)
    wqt = _to_slabs(w_q_b, QK_HEAD, src).T.astype(BF16)
    wkt = _to_slabs(w_kv_b, QK_NOPE + V_HEAD, nope_src).T.astype(BF16)
    wvt = w_kv_b.reshape(KV_LORA, MLA_HEADS, QK_NOPE + V_HEAD)[:, :, QK_NOPE:].reshape(KV_LORA, MLA_WIDTH).T.astype(BF16)
    q_scale = math.log2(math.e) / math.sqrt(QK_HEAD)
    slab_gain = lambda g: jnp.broadcast_to(jnp.where(valid, g[np.clip(src, 0, QK_HEAD - 1)], 0.0)[:, None],
                                           (LANES, LANES))
    gq = slab_gain(q_norm_g * q_scale)
    gk = slab_gain(k_norm_g)
    const = lambda shape: pl.BlockSpec(shape, lambda i: (0,) * len(shape))
    tok = lambda i: (i // per_b, 0, 0, i % per_b)
    return pl.pallas_call(
        _mla_proj_kernel,
        out_shape=(jax.ShapeDtypeStruct((n_b, MLA_HEADS, LANES, seq), BF16),
                   jax.ShapeDtypeStruct((n_b, MLA_HEADS, seq, LANES), BF16),
                   jax.ShapeDtypeStruct((n_b, MLA_HEADS, V_ROWS, seq), BF16),
                   jax.ShapeDtypeStruct((MLA_WIDTH, t), BF16),
                   jax.ShapeDtypeStruct((t, d), F32)),
        grid=(t // tm,),
        in_specs=[pl.BlockSpec((tm, d), lambda i: (i, 0)),
                  pl.BlockSpec((None, 1, 3 * d), lambda i: (i // per_b, 0, 0)),
                  const((1, d)), const(w1.shape), const(wzt.shape), const((1, Q_LORA)), const((1, KV_LORA)),
                  const(wqt.shape), const(wkt.shape), const(wvt.shape),
                  const((LANES, LANES)), const((LANES, LANES)),
                  pl.BlockSpec((1, tm), lambda i: (0, i)),
                  const((ROPE_HALF, 1))],
        out_specs=(pl.BlockSpec((None, MLA_HEADS, LANES, tm), tok),
                   pl.BlockSpec((None, MLA_HEADS, tm, LANES), lambda i: (i // per_b, 0, i % per_b, 0)),
                   pl.BlockSpec((None, MLA_HEADS, V_ROWS, tm), tok),
                   pl.BlockSpec((MLA_WIDTH, tm), lambda i: (0, i)),
                   pl.BlockSpec((tm, d), lambda i: (i, 0))),
        compiler_params=pltpu.CompilerParams(dimension_semantics=("arbitrary",),
                                             vmem_limit_bytes=VMEM_LIMIT),
        name="mla_projection",
    )(x2, mod3, norm_g.reshape(1, d), w1, wzt, q_a_g.reshape(1, -1), kv_a_g.reshape(1, -1), wqt, wkt, wvt, gq, gk,
      positions.reshape(1, t), _rope_frequencies())


ATTN_TQ = 2048
ATTN_TK = 256
HEADS_PER_STEP = 2


def _attn_kernel(qt_ref, k_ref, vt_ref, zs_ref, o_ref, s_scr):
    tq, tk = ATTN_TQ, ATTN_TK
    r = tq // tk
    assert r * tk == tq and r % 2 == 0
    qi = pl.program_id(2)

    def scores(slot, t, lo=0):
        start = pl.multiple_of(t * tk, tk)
        for e in range(HEADS_PER_STEP):
            s_scr[slot, e, :, lo:] = jnp.dot(k_ref[e, pl.ds(start, tk), :], qt_ref[e, :, lo:],
                                             preferred_element_type=F32)

    def process(slot, t, carry, lo=0, diagonal=False):
        start = pl.multiple_of(t * tk, tk)
        out = []
        for e in range(HEADS_PER_STEP):
            m, acc = carry[e]
            s = s_scr[slot, e, :, lo:]
            if diagonal:
                blk = s[:, :tk]
                ok = lax.broadcasted_iota(jnp.int32, blk.shape, 0) <= lax.broadcasted_iota(jnp.int32, blk.shape, 1)
                blk = jnp.where(ok, blk, NEG_BIG)
                s = jnp.concatenate([blk, s[:, tk:]], axis=1) if s.shape[1] > tk else blk
            m_old = m[:, lo:]
            m_new = jnp.maximum(m_old, jnp.max(s, axis=0, keepdims=True))
            p = jnp.exp2(s - m_new).astype(BF16)
            acc_new = (jnp.exp2(m_old - m_new) * acc[:, lo:]
                       + jnp.dot(vt_ref[e, :, pl.ds(start, tk)], p, preferred_element_type=F32))
            if lo:
                m_new = jnp.concatenate([m[:, :lo], m_new], axis=1)
                acc_new = jnp.concatenate([acc[:, :lo], acc_new], axis=1)
            out.append((m_new, acc_new))
        return tuple(out)

    def body(i, carry):
        base = i * r
        for u in range(r):
            scores((u + 1) % 2, base + u + 1)
            carry = process(u % 2, base + u, carry)
        return carry

    init = tuple((jnp.full((1, tq), NEG_BIG, F32), jnp.zeros((V_ROWS, tq), F32)) for _ in range(HEADS_PER_STEP))
    scores(0, 0)
    carry = lax.fori_loop(0, qi, body, init)
    base = qi * r
    for u in range(r):
        if u + 1 < r:
            scores((u + 1) % 2, base + u + 1, lo=(u + 1) * tk)
        carry = process(u % 2, base + u, carry, lo=u * tk, diagonal=True)
    _attn_finish(tuple(c[1] for c in carry), zs_ref, o_ref)


def _attn_finish(accs, zs_ref, o_ref):
    outs = [acc[:V_HEAD] / acc[V_HEAD:V_HEAD + 1] for acc in accs]
    o_ref[...] = (jnp.concatenate(outs, axis=0) * zs_ref[...].astype(F32)).astype(o_ref.dtype)


def _attn_bounded_kernel(qt_ref, k_ref, vt_ref, zs_ref, o_ref, p_scr):
    tq, tk = ATTN_TQ, ATTN_TK
    r = tq // tk
    assert r * tk == tq and r % 2 == 0
    qi = pl.program_id(2)
    q0 = qi * tq

    def probs(slot, t, lo=0, may_cross=False):
        start = pl.multiple_of(t * tk, tk)
        for e in range(HEADS_PER_STEP):
            p = jnp.exp2(jnp.dot(k_ref[e, pl.ds(start, tk), :], qt_ref[e, :, lo:], preferred_element_type=F32))
            if may_cross:
                blk = p[:, :tk]
                key = start + lax.broadcasted_iota(jnp.int32, blk.shape, 0)
                qry = q0 + lo + lax.broadcasted_iota(jnp.int32, blk.shape, 1)
                blk = jnp.where(key <= qry, blk, 0.0)
                p = jnp.concatenate([blk, p[:, tk:]], axis=1) if p.shape[1] > tk else blk
            p_scr[slot, e, :, lo:] = p.astype(BF16)

    def accumulate(slot, t, accs, lo=0):
        start = pl.multiple_of(t * tk, tk)
        out = []
        for e in range(HEADS_PER_STEP):
            pv = jnp.dot(vt_ref[e, :, pl.ds(start, tk)], p_scr[slot, e, :, lo:], preferred_element_type=F32)
            acc = accs[e]
            out.append(acc + pv if lo == 0 else jnp.concatenate([acc[:, :lo], acc[:, lo:] + pv], axis=1))
        return tuple(out)

    def body(i, accs):
        base = i * r
        for u in range(r):
            probs((u + 1) % 2, base + u + 1, may_cross=(u == r - 1))
            accs = accumulate(u % 2, base + u, accs)
        return accs

    probs(0, 0, may_cross=True)
    accs = lax.fori_loop(0, qi, body, tuple(jnp.zeros((V_ROWS, tq), F32) for _ in range(HEADS_PER_STEP)))
    base = qi * r
    for u in range(r):
        if u + 1 < r:
            probs((u + 1) % 2, base + u + 1, lo=(u + 1) * tk, may_cross=True)
        accs = accumulate(u % 2, base + u, accs, lo=u * tk)
    _attn_finish(accs, zs_ref, o_ref)


SCORE_BOUND_LOG2 = 60.0


def _attention(qt, k, vt, zs, score_bound):
    n_b, n_h, seq, _ = k.shape
    tq = ATTN_TQ
    assert seq % tq == 0 and n_h % HEADS_PER_STEP == 0
    nq = seq // tq
    hp = HEADS_PER_STEP
    width = hp * V_HEAD

    def call(body, scratch):
        return pl.pallas_call(
            body,
            out_shape=jax.ShapeDtypeStruct(zs.shape, BF16),
            grid=(n_b, n_h // hp, nq),
            in_specs=[pl.BlockSpec((None, hp, LANES, tq), lambda b, h, i: (b, h, 0, i)),
                      pl.BlockSpec((None, hp, seq, LANES), lambda b, h, i: (b, h, 0, 0)),
                      pl.BlockSpec((None, hp, V_ROWS, seq), lambda b, h, i: (b, h, 0, 0)),
                      pl.BlockSpec((width, tq), lambda b, h, i: (h, b * nq + i))],
            out_specs=pl.BlockSpec((width, tq), lambda b, h, i: (h, b * nq + i)),
            scratch_shapes=scratch,
            compiler_params=pltpu.CompilerParams(dimension_semantics=("arbitrary", "arbitrary", "arbitrary"),
                                                 vmem_limit_bytes=VMEM_LIMIT),
            name="causal_attention",
        )(qt, k, vt, zs)

    tiles = (2, hp, ATTN_TK, tq)
    return lax.cond(score_bound <= SCORE_BOUND_LOG2,
                    lambda: call(_attn_bounded_kernel, [pltpu.VMEM(tiles, BF16)]),
                    lambda: call(_attn_kernel, [pltpu.VMEM(tiles, F32)]))


def _s5_scan_steps(n_top):
    return [CHUNK * m for m in range(SUPER)] + [CHUNK * SUPER * (1 << i) for i in range(n_top)]


def _s5_exponents(n_top):
    return sorted(set(range(CHUNK + 1)) | set(_s5_scan_steps(n_top)))


def _s5_coef_kernel(kk_ref, *refs, n_top):
    for gi in range(SCAN_GROUPS):
        _s5_coef_one_group(kk_ref, *(r.at[gi] for r in refs), n_top=n_top)


def _s5_coef_one_group(kk_ref, lre_ref, lim_ref, ldt_ref, bre_ref, bim_ref, cre_ref, cim_ref, d_ref,
                       lhs_ref, wout_ref, ap_ref, *, n_top):
    h, p, n = SSM_GROUP, SSM_STATE, CHUNK
    lre, lim = lre_ref[...], lim_ref[...]
    dt = jnp.exp(ldt_ref[...])
    exps = _s5_exponents(n_top)
    kcol = kk_ref[...]
    mag = jnp.exp(kcol * (lre * dt))
    ang = kcol * (lim * dt)
    pow_re, pow_im = mag * jnp.cos(ang), mag * jnp.sin(ang)

    def power(kk):
        i = exps.index(kk)
        return pow_re[i:i + 1], pow_im[i:i + 1]

    lb_re, lb_im = power(1)
    nr, ni = lb_re - 1.0, lb_im
    den = lre * lre + lim * lim
    f_re = (nr * lre + ni * lim) / den
    f_im = (ni * lre - nr * lim) / den
    bre, bim = bre_ref[...].T, bim_ref[...].T
    bb_re = f_re * bre - f_im * bim
    bb_im = f_re * bim + f_im * bre
    cre, cim = cre_ref[...], cim_ref[...]
    cp_re, cp_im = [], []
    for kk in range(n + 1):
        pr, pi = power(kk)
        cp_re.append(cre * pr - cim * pi)
        cp_im.append(cre * pi + cim * pr)

    cpw = [jnp.concatenate([cp_re[kk], -cp_im[kk]], axis=1) for kk in range(n + 1)]
    bb_a = jnp.concatenate([bb_re, bb_im], axis=1)
    kw = lax.dot_general(jnp.concatenate(cpw[:n], axis=0), jnp.concatenate([bb_a] * n, axis=0), _NT,
                         preferred_element_type=F32, precision=HIGHEST)
    lane = lax.broadcasted_iota(jnp.int32, (h, n * h), 1)
    row = lax.broadcasted_iota(jnp.int32, (h, n * h), 0)
    col_blk = lax.shift_right_logical(lane, 4)
    d_tiled = jnp.concatenate([d_ref[...]] * n, axis=1)
    lag = [kw[h * kk:h * (kk + 1)] for kk in range(n)]
    lag[0] = lag[0] + jnp.where((lane & (h - 1)) == row, d_tiled, 0.0)
    rows = []
    for j in range(n):
        acc = jnp.zeros((h, n * h), F32)
        for jp in range(j + 1):
            acc = jnp.where(col_blk == jp, lag[j - jp], acc)
        rows.append(acc)
    lhs_ref[0:n * h, :] = jnp.concatenate(rows, axis=0).astype(lhs_ref.dtype)

    bb_b = jnp.concatenate([-bb_im, bb_re], axis=1)
    win = []
    for j in range(n):
        pr, pi = power(n - 1 - j)
        win.append(jnp.concatenate([pr, pr], axis=1) * bb_a + jnp.concatenate([pi, pi], axis=1) * bb_b)
    lhs_ref[n * h:, :] = jnp.concatenate(win, axis=0).T.astype(lhs_ref.dtype)
    wout_ref[...] = jnp.concatenate(cpw[1:], axis=0).astype(wout_ref.dtype)
    for i, kk in enumerate(_s5_scan_steps(n_top)):
        pr, pi = power(kk)
        ap_ref[i:i + 1, :] = jnp.concatenate([pr, pi], axis=1)


def _s5_coefficients(log_dt, lam_re, lam_im, b_re, b_im, c_re, c_im, d_skip, n_top):
    g, p, h = SSM_GROUPS, SSM_STATE, SSM_GROUP
    n_ap = SUPER + n_top
    gb = SCAN_GROUPS
    grp = lambda *shape: pl.BlockSpec((gb,) + shape, lambda i: (i,) + (0,) * len(shape))
    exps = np.asarray(_s5_exponents(n_top), np.float32)
    kk = np.zeros((-(-exps.size // 8) * 8, 1), np.float32)
    kk[:exps.size, 0] = exps
    return pl.pallas_call(
        functools.partial(_s5_coef_kernel, n_top=n_top),
        out_shape=(jax.ShapeDtypeStruct((g, CHUNK * h + 2 * p, CHUNK * h), BF16),
                   jax.ShapeDtypeStruct((g, CHUNK * h, 2 * p), BF16),
                   jax.ShapeDtypeStruct((g, n_ap, 2 * p), F32)),
        grid=(g // gb,),
        in_specs=[pl.BlockSpec(kk.shape, lambda i: (0, 0)),
                  grp(1, p), grp(1, p), grp(1, 1), grp(p, h), grp(p, h), grp(h, p), grp(h, p), grp(1, h)],
        out_specs=(grp(CHUNK * h + 2 * p, CHUNK * h), grp(CHUNK * h, 2 * p), grp(n_ap, 2 * p)),
        name="s5_coefficients",
    )(jnp.asarray(kk), lam_re.reshape(g, 1, p), lam_im.reshape(g, 1, p), log_dt.reshape(g, 1, 1), b_re, b_im,
      c_re, c_im, d_skip.reshape(g, 1, h))


PHASES = 8


def _ssm_proj_kernel(h_ref, w_ref, u_ref, z_ref, rows_scr, *, n_b):
    for jj in range(PHASES):
        for b in range(n_b):
            rows_scr[b] = h_ref[b, :, jj, :]
        h = jnp.concatenate([rows_scr[b].astype(BF16) for b in range(n_b)], axis=0)
        r = lax.dot_general(w_ref[...], h, _NT, preferred_element_type=F32)
        u_ref[jj] = r[:SSM_WIDTH].astype(BF16)
        z_ref[jj] = _silu(r[SSM_WIDTH:]).astype(BF16)


def _ssm_projection(h4, w_in):
    n_b, c2, _, d = h4.shape
    lanes = n_b * c2
    halves = CHUNK // PHASES
    w_t = w_in[:, :2 * SSM_WIDTH].T.astype(BF16)
    out = jax.ShapeDtypeStruct((SUPER, CHUNK, SSM_WIDTH, lanes), BF16)
    out_spec = pl.BlockSpec((None, PHASES, SSM_WIDTH, lanes), lambda j2, hf: (j2, hf, 0, 0))
    return pl.pallas_call(
        functools.partial(_ssm_proj_kernel, n_b=n_b),
        out_shape=(out, out),
        grid=(SUPER, halves),
        in_specs=[pl.BlockSpec((n_b, c2, PHASES, d), lambda j2, hf: (0, 0, j2 * halves + hf, 0)),
                  pl.BlockSpec(w_t.shape, lambda j2, hf: (0, 0))],
        out_specs=(out_spec, out_spec),
        scratch_shapes=[pltpu.VMEM((n_b, c2, d), F32)],
        compiler_params=pltpu.CompilerParams(dimension_semantics=("arbitrary", "arbitrary"),
                                             vmem_limit_bytes=VMEM_LIMIT),
        name="ssm_projection",
    )(h4, w_t)


def _cmul(ar, ai, xr, xi):
    return ar * xr - ai * xi, ar * xi + ai * xr


SCAN_GROUPS = 8


def _s5_group_kernel(x_ref, lhs_ref, wout_ref, ap_ref, y_ref, *, n_b, c2, n_top):
    p, gb = SSM_STATE, SCAN_GROUPS
    rows = CHUNK * SSM_GROUP
    n = SUPER * n_b * c2
    lanes = n_b * c2

    def x_group(gi):
        return jnp.concatenate([x_ref[j2, :, SSM_GROUP * gi:SSM_GROUP * (gi + 1), :].reshape(rows, lanes)
                                for j2 in range(SUPER)], axis=1)

    r = [jnp.dot(lhs_ref[gi], x_group(gi), preferred_element_type=F32) for gi in range(gb)]
    c_re = jnp.concatenate([r[gi][rows:rows + p] for gi in range(gb)], axis=0)
    c_im = jnp.concatenate([r[gi][rows + p:] for gi in range(gb)], axis=0)

    def mult(i):
        rep = [jnp.broadcast_to(ap_ref[gi, i:i + 1, :], (2 * p, 2 * p)).T for gi in range(gb)]
        return (jnp.concatenate([m[:p, :c2] for m in rep], axis=0),
                jnp.concatenate([m[p:, :c2] for m in rep], axis=0))

    lane = lax.broadcasted_iota(jnp.int32, (gb * p, c2), 1)

    def shifted(a, sh):
        return jnp.where(lane >= sh, pltpu.roll(a, sh, 1), 0.0)

    pieces = [[None] * n_b for _ in range(SUPER)]
    for b in range(n_b):
        piece = lambda a, j2: a[:, (j2 * n_b + b) * c2:(j2 * n_b + b + 1) * c2]
        a_re, a_im = mult(1)
        e_re = jnp.zeros((gb * p, c2), F32)
        e_im = jnp.zeros((gb * p, c2), F32)
        local = []
        for j2 in range(SUPER):
            local.append((e_re, e_im))
            t_re, t_im = _cmul(a_re, a_im, e_re, e_im)
            e_re, e_im = t_re + piece(c_re, j2), t_im + piece(c_im, j2)
        for i in range(n_top):
            m_re, m_im = mult(SUPER + i)
            t_re, t_im = _cmul(m_re, m_im, shifted(e_re, 1 << i), shifted(e_im, 1 << i))
            e_re, e_im = e_re + t_re, e_im + t_im
        s_re, s_im = shifted(e_re, 1), shifted(e_im, 1)
        for j2 in range(SUPER):
            if j2 == 0:
                pieces[j2][b] = (s_re, s_im)
            else:
                m_re, m_im = mult(j2)
                t_re, t_im = _cmul(m_re, m_im, s_re, s_im)
                pieces[j2][b] = (local[j2][0] + t_re, local[j2][1] + t_im)
    for gi in range(gb):
        sl = slice(gi * p, (gi + 1) * p)
        s_in = jnp.concatenate([jnp.concatenate([pieces[j2][b][0][sl], pieces[j2][b][1][sl]], axis=0)
                                for j2 in range(SUPER) for b in range(n_b)], axis=1)
        y = r[gi][:rows] + jnp.dot(wout_ref[gi], s_in.astype(BF16), preferred_element_type=F32)
        for j2 in range(SUPER):
            y_ref[j2, gi] = y[:, j2 * lanes:(j2 + 1) * lanes].reshape(CHUNK, SSM_GROUP, lanes).astype(y_ref.dtype)


def _s5_scan(u_t, lhs, wout, ap, n_b, c2, n_top):
    g, h = SSM_GROUPS, SSM_GROUP
    lanes = u_t.shape[-1]
    gb = SCAN_GROUPS
    grp = lambda *shape: pl.BlockSpec((gb,) + shape, lambda i: (i,) + (0,) * len(shape))
    return pl.pallas_call(
        functools.partial(_s5_group_kernel, n_b=n_b, c2=c2, n_top=n_top),
        out_shape=jax.ShapeDtypeStruct((SUPER, g, CHUNK, h, lanes), BF16),
        grid=(g // gb,),
        in_specs=[pl.BlockSpec((SUPER, CHUNK, gb * h, lanes), lambda i: (0, 0, i, 0)),
                  grp(*lhs.shape[1:]), grp(*wout.shape[1:]), grp(*ap.shape[1:])],
        out_specs=pl.BlockSpec((SUPER, gb, CHUNK, h, lanes), lambda i: (0, i, 0, 0, 0)),
        compiler_params=pltpu.CompilerParams(dimension_semantics=("arbitrary",),
                                             vmem_limit_bytes=VMEM_LIMIT),
        name="s5_chunk_scan",
    )(u_t, lhs, wout, ap)


def _glu_kernel(y_ref, zs_ref, w_ref, b_ref, o_ref, *, n_b, c2):
    g, _, h, n = y_ref.shape
    for jj in range(PHASES):
        y = jax.nn.gelu(y_ref[:, jj].reshape(g * h, n).astype(F32))
        t = jnp.dot(w_ref[...], y.astype(BF16), preferred_element_type=F32) + b_ref[...]
        gated = (y * jax.nn.sigmoid(t) * zs_ref[jj].astype(F32)).T
        for b in range(n_b):
            o_ref[b, :, jj, :] = gated[b * c2:(b + 1) * c2]


def _glu(y_t, zs_t, w_glu, b_glu, n_b, c2):
    _, g, _, h, _ = y_t.shape
    lanes = n_b * c2
    halves = CHUNK // PHASES
    w_t = w_glu.T.astype(BF16)
    return pl.pallas_call(
        functools.partial(_glu_kernel, n_b=n_b, c2=c2),
        out_shape=jax.ShapeDtypeStruct((n_b, c2, SUPER * CHUNK, SSM_WIDTH), F32),
        grid=(SUPER, halves),
        in_specs=[pl.BlockSpec((None, g, PHASES, h, lanes), lambda j2, hf: (j2, 0, hf, 0, 0)),
                  pl.BlockSpec((None, PHASES, SSM_WIDTH, lanes), lambda j2, hf: (j2, hf, 0, 0)),
                  pl.BlockSpec(w_t.shape, lambda j2, hf: (0, 0)),
                  pl.BlockSpec((SSM_WIDTH, 1), lambda j2, hf: (0, 0))],
        out_specs=pl.BlockSpec((n_b, c2, PHASES, SSM_WIDTH), lambda j2, hf: (0, 0, j2 * halves + hf, 0)),
        compiler_params=pltpu.CompilerParams(dimension_semantics=("arbitrary", "arbitrary"),
                                             vmem_limit_bytes=VMEM_LIMIT),
        name="s5_glu",
    )(y_t, zs_t, w_t, b_glu.reshape(SSM_WIDTH, 1))


def _out_proj_kernel(ys_ref, ym_ref, x_ref, mod_ref, wa_ref, wb_ref, o_ref):
    y = (jnp.dot(ys_ref[...].astype(BF16), wa_ref[...], preferred_element_type=F32)
         + lax.dot_general(ym_ref[...], wb_ref[...], _TN, preferred_element_type=F32))
    gate = mod_ref[...][:, 2 * D_MODEL:]
    o_ref[...] = x_ref[...] + gate * y


def _out_projection(ys, ym, x2, mod3, w_out, seq):
    t, d = x2.shape
    tm = min(1024, seq)
    per_b = seq // tm
    wa = w_out[:SSM_WIDTH].astype(BF16)
    wb = w_out[SSM_WIDTH:].astype(BF16)
    tok = lambda w: pl.BlockSpec((tm, w), lambda i: (i, 0))
    return pl.pallas_call(
        _out_proj_kernel,
        out_shape=jax.ShapeDtypeStruct(x2.shape, F32),
        grid=(t // tm,),
        in_specs=[tok(SSM_WIDTH), pl.BlockSpec((MLA_WIDTH, tm), lambda i: (0, i)), tok(d),
                  pl.BlockSpec((None, 1, 3 * d), lambda i: (i // per_b, 0, 0)),
                  pl.BlockSpec(wa.shape, lambda i: (0, 0)),
                  pl.BlockSpec(wb.shape, lambda i: (0, 0))],
        out_specs=tok(d),
        compiler_params=pltpu.CompilerParams(dimension_semantics=("arbitrary",),
                                             vmem_limit_bytes=VMEM_LIMIT),
        name="output_projection",
    )(ys, ym, x2, mod3, wa, wb)


def kernel(x, c, positions, w_ada, b_ada, norm_g, w_in, log_dt, lam_re, lam_im, b_re, b_im, c_re, c_im, d_skip,
           w_glu, b_glu, q_a_g, w_q_b, kv_a_g, w_kv_b, q_norm_g, k_norm_g, w_out):
    n_b, seq, d = x.shape
    depth = w_ada.shape[0]
    c2 = seq // (CHUNK * SUPER)
    n_top = max(int(math.log2(c2)), 0)
    assert c2 * CHUNK * SUPER == seq and (1 << n_top) == c2
    for l in range(depth):
        mod3 = _modulation(c, w_ada[l], b_ada[l]).reshape(n_b, 1, 3 * d)
        x2 = x.reshape(n_b * seq, d)
        lhs, wout, ap = _s5_coefficients(log_dt[l], lam_re[l], lam_im[l], b_re[l], b_im[l], c_re[l], c_im[l],
                                         d_skip[l], n_top)
        qt, k, vt, zm, h = _mla_projection(x2, mod3, norm_g[l], w_in[l], q_a_g[l], w_q_b[l],
                                           kv_a_g[l], w_kv_b[l], q_norm_g[l], k_norm_g[l], positions, n_b, seq)
        score_bound = (1.05 * math.sqrt(QK_HEAD) * math.log2(math.e)
                       * jnp.max(jnp.abs(q_norm_g[l])) * jnp.max(jnp.abs(k_norm_g[l])))
        ym = _attention(qt, k, vt, zm, score_bound)
        u_t, zs_t = _ssm_projection(h.reshape(n_b, c2, SUPER * CHUNK, d), w_in[l])
        y_t = _s5_scan(u_t, lhs, wout, ap, n_b, c2, n_top)
        ys = _glu(y_t, zs_t, w_glu[l], b_glu[l], n_b, c2).reshape(n_b * seq, SSM_WIDTH)
        x = _out_projection(ys, ym, x2, mod3, w_out[l], seq).reshape(n_b, seq, d)
    return x
```

```python
import functools
import math

import numpy as np
import jax
import jax.numpy as jnp
from jax import lax
from jax.experimental import pallas as pl
from jax.experimental.pallas import tpu as pltpu

F32 = jnp.float32
BF16 = jnp.bfloat16
HIGHEST = lax.Precision.HIGHEST

D_MODEL = 1024
SSM_WIDTH = 512
SSM_GROUP = 16
SSM_GROUPS = 32
SSM_STATE = 64
MLA_HEADS = 8
QK_NOPE = 64
QK_ROPE = 32
QK_HEAD = QK_NOPE + QK_ROPE
V_HEAD = 64
MLA_WIDTH = MLA_HEADS * V_HEAD
Q_LORA = 384
KV_LORA = 256
ROPE_THETA = 10000.0
EPS = 1e-6
NEG_BIG = -1e30

LANES = 128
VMEM_LIMIT = 60 * 1024 * 1024
CHUNK = 16
SUPER = 4
ROPE_HALF = QK_ROPE // 2

_NT = (((1,), (1,)), ((), ()))
_TN = (((0,), (0,)), ((), ()))


def _silu(v):
    return v * jax.nn.sigmoid(v)


def _rms(v, gain):
    return v * lax.rsqrt(jnp.mean(v * v, axis=-1, keepdims=True) + EPS) * gain


def _mod_kernel(c_ref, w_ref, b_ref, o_ref, *, n_b):
    rows, d = c_ref.shape
    tn = w_ref.shape[1]
    act_t = _silu(c_ref[...]).T
    out = []
    for b in range(n_b):
        col = jnp.broadcast_to(act_t[:, b:b + 1], (d, LANES))
        out.append(jnp.concatenate([jnp.sum(w_ref[:, j:j + LANES] * col, axis=0, keepdims=True)
                                    for j in range(0, tn, LANES)], axis=1))
    out.append(jnp.zeros((rows - n_b, tn), F32))
    o_ref[...] = jnp.concatenate(out, axis=0) + b_ref[...]


def _modulation(c, w, b):
    n_b, d = c.shape
    rows = 8
    c_pad = jnp.zeros((rows, d), F32).at[:n_b].set(c)
    tn = 512
    out = pl.pallas_call(
        functools.partial(_mod_kernel, n_b=n_b),
        out_shape=jax.ShapeDtypeStruct((rows, 3 * d), F32),
        grid=(3 * d // tn,),
        in_specs=[pl.BlockSpec((rows, d), lambda n: (0, 0)),
                  pl.BlockSpec((d, tn), lambda n: (0, n)),
                  pl.BlockSpec((1, tn), lambda n: (0, n))],
        out_specs=pl.BlockSpec((rows, tn), lambda n: (0, n)),
        name="adaln_modulation",
    )(c_pad, w, b.reshape(1, -1))
    return out[:n_b]


def _rope_frequencies():
    inv_freq = ROPE_THETA ** (-np.arange(ROPE_HALF, dtype=np.float64) * 2.0 / QK_ROPE)
    return jnp.asarray(inv_freq.astype(np.float32)).reshape(ROPE_HALF, 1)


def _slab_source():
    src = np.full((LANES,), -1, np.int64)
    src[0:16] = QK_NOPE + np.arange(16)
    src[16:64] = np.arange(48)
    src[64:80] = QK_NOPE + 16 + np.arange(16)
    src[80:96] = 48 + np.arange(16)
    return src


def _to_slabs(w, per_head, src):
    k = w.shape[0]
    w3 = w.reshape(k, MLA_HEADS, per_head)
    valid = src >= 0
    cols = jnp.where(valid[None, None, :], w3[:, :, np.clip(src, 0, per_head - 1)], 0.0)
    return cols.reshape(k, MLA_HEADS * LANES)


V_ROWS = V_HEAD + 16


def _mla_proj_kernel(x_ref, mod_ref, ng_ref, w1_ref, wzt_ref, qag_ref, kvag_ref, wqt_ref, wkt_ref, wvt_ref,
                     gq_ref, gk_ref, pos_ref, freq_ref, qt_ref, k_ref, vt_ref, zs_ref, h_ref):
    d = D_MODEL
    tm = x_ref.shape[0]
    x = x_ref[...]
    mod = mod_ref[...]
    shift, scale = mod[:, :d], mod[:, d:2 * d]
    h_ref[...] = _rms(x, ng_ref[...] * (1.0 + scale)) + shift
    h = h_ref[...].astype(BF16)
    p1 = jnp.dot(h, w1_ref[...], preferred_element_type=F32)
    o1, o2 = Q_LORA, Q_LORA + KV_LORA
    cqn = _rms(p1[:, :o1], qag_ref[...]).astype(BF16)
    ckvn = _rms(p1[:, o1:o2], kvag_ref[...]).astype(BF16)
    zs_ref[...] = _silu(lax.dot_general(wzt_ref[...], h, _NT, preferred_element_type=F32)).astype(BF16)

    qt = lax.dot_general(wqt_ref[...], cqn, _NT, preferred_element_type=F32)
    kt = lax.dot_general(wkt_ref[...], ckvn, _NT, preferred_element_type=F32)
    krt = p1[:, o2:].T
    vt = lax.dot_general(wvt_ref[...], ckvn, _NT, preferred_element_type=F32)
    ang = freq_ref[...] * pos_ref[...].astype(F32)
    cos_t, sin_t = jnp.cos(ang), jnp.sin(ang)
    gq = jnp.concatenate([gq_ref[...]] * (tm // LANES), axis=1)
    gk = jnp.concatenate([gk_ref[...]] * (tm // LANES), axis=1)
    first = lax.broadcasted_iota(jnp.int32, (V_ROWS - V_HEAD, tm), 0) == 0
    ones_rows = jnp.where(first, 1.0, 0.0).astype(BF16)
    r, half = ROPE_HALF, LANES // 2

    def norm_rope(slab, gain):
        ss = jnp.sum(slab * slab, axis=0, keepdims=True)
        n = slab * lax.rsqrt(ss * (1.0 / QK_HEAD) + EPS) * gain
        x1, x2 = n[0:r], n[half:half + r]
        return jnp.concatenate([x1 * cos_t - x2 * sin_t, n[r:half], x2 * cos_t + x1 * sin_t, n[half + r:]], axis=0)

    for head in range(MLA_HEADS):
        rows = slice(LANES * head, LANES * (head + 1))
        qt_ref[head] = norm_rope(qt[rows], gq).astype(BF16)
        k_ref[head] = norm_rope(kt[rows] + krt, gk).T.astype(BF16)
        vt_ref[head, 0:V_HEAD, :] = vt[V_HEAD * head:V_HEAD * (head + 1)].astype(BF16)
        vt_ref[head, V_HEAD:, :] = ones_rows


def _mla_projection(x2, mod3, norm_g, w_in, q_a_g, w_q_b, kv_a_g, w_kv_b, q_norm_g, k_norm_g, positions, n_b, seq):
    t, d = x2.shape
    tm = min(512, seq)
    per_b = seq // tm
    o2 = 2 * SSM_WIDTH
    o4, o5 = o2 + Q_LORA + KV_LORA, o2 + Q_LORA + KV_LORA + QK_ROPE
    src = _slab_source()
    valid = src >= 0
    rope_src = np.where(src >= QK_NOPE, src - QK_NOPE, -1)
    nope_src = np.where(valid & (src < QK_NOPE), src, -1)
    wkr = jnp.where((rope_src >= 0)[None, :], w_in[:, o4:o5][:, np.clip(rope_src, 0, QK_ROPE - 1)], 0.0)
    w1 = jnp.concatenate([w_in[:, o2:o4], wkr], axis=1).astype(BF16)
    wzt = w_in[:, o5:].T.astype(BF16)
    wqt = _to_slabs(w_q_b, QK_HEAD, src).T.astype(BF16)
    wkt = _to_slabs(w_kv_b, QK_NOPE + V_HEAD, nope_src).T.astype(BF16)
    wvt = w_kv_b.reshape(KV_LORA, MLA_HEADS, QK_NOPE + V_HEAD)[:, :, QK_NOPE:].reshape(KV_LORA, MLA_WIDTH).T.astype(BF16)
    q_scale = math.log2(math.e) / math.sqrt(QK_HEAD)
    slab_gain = lambda g: jnp.broadcast_to(jnp.where(valid, g[np.clip(src, 0, QK_HEAD - 1)], 0.0)[:, None],
                                           (LANES, LANES))
    gq = slab_gain(q_norm_g * q_scale)
    gk = slab_gain(k_norm_g)
    const = lambda shape: pl.BlockSpec(shape, lambda i: (0,) * len(shape))
    tok = lambda i: (i // per_b, 0, 0, i % per_b)
    return pl.pallas_call(
        _mla_proj_kernel,
        out_shape=(jax.ShapeDtypeStruct((n_b, MLA_HEADS, LANES, seq), BF16),
                   jax.ShapeDtypeStruct((n_b, MLA_HEADS, seq, LANES), BF16),
                   jax.ShapeDtypeStruct((n_b, MLA_HEADS, V_ROWS, seq), BF16),
                   jax.ShapeDtypeStruct((MLA_WIDTH, t), BF16),
                   jax.ShapeDtypeStruct((t, d), F32)),
        grid=(t // tm,),
        in_specs=[pl.BlockSpec((tm, d), lambda i: (i, 0)),
                  pl.BlockSpec((None, 1, 3 * d), lambda i: (i // per_b, 0, 0)),
                  const((1, d)), const(w1.shape), const(wzt.shape), const((1, Q_LORA)), const((1, KV_LORA)),
                  const(wqt.shape), const(wkt.shape), const(wvt.shape),
                  const((LANES, LANES)), const((LANES, LANES)),
                  pl.BlockSpec((1, tm), lambda i: (0, i)),
                  const((ROPE_HALF, 1))],
        out_specs=(pl.BlockSpec((None, MLA_HEADS, LANES, tm), tok),
                   pl.BlockSpec((None, MLA_HEADS, tm, LANES), lambda i: (i // per_b, 0, i % per_b, 0)),
                   pl.BlockSpec((None, MLA_HEADS, V_ROWS, tm), tok),
                   pl.BlockSpec((MLA_WIDTH, tm), lambda i: (0, i)),
                   pl.BlockSpec((tm, d), lambda i: (i, 0))),
        compiler_params=pltpu.CompilerParams(dimension_semantics=("arbitrary",),
                                             vmem_limit_bytes=VMEM_LIMIT),
        name="mla_projection",
    )(x2, mod3, norm_g.reshape(1, d), w1, wzt, q_a_g.reshape(1, -1), kv_a_g.reshape(1, -1), wqt, wkt, wvt, gq, gk,
      positions.reshape(1, t), _rope_frequencies())


ATTN_TQ = 2048
ATTN_TK = 256
HEADS_PER_STEP = 2


def _attn_kernel(qt_ref, k_ref, vt_ref, zs_ref, o_ref, s_scr):
    tq, tk = ATTN_TQ, ATTN_TK
    r = tq // tk
    assert r * tk == tq and r % 2 == 0
    qi = pl.program_id(2)

    def scores(slot, t, lo=0):
        start = pl.multiple_of(t * tk, tk)
        for e in range(HEADS_PER_STEP):
            s_scr[slot, e, :, lo:] = jnp.dot(k_ref[e, pl.ds(start, tk), :], qt_ref[e, :, lo:],
                                             preferred_element_type=F32)

    def process(slot, t, carry, lo=0, diagonal=False):
        start = pl.multiple_of(t * tk, tk)
        out = []
        for e in range(HEADS_PER_STEP):
            m, acc = carry[e]
            s = s_scr[slot, e, :, lo:]
            if diagonal:
                blk = s[:, :tk]
                ok = lax.broadcasted_iota(jnp.int32, blk.shape, 0) <= lax.broadcasted_iota(jnp.int32, blk.shape, 1)
                blk = jnp.where(ok, blk, NEG_BIG)
                s = jnp.concatenate([blk, s[:, tk:]], axis=1) if s.shape[1] > tk else blk
            m_old = m[:, lo:]
            m_new = jnp.maximum(m_old, jnp.max(s, axis=0, keepdims=True))
            p = jnp.exp2(s - m_new).astype(BF16)
            acc_new = (jnp.exp2(m_old - m_new) * acc[:, lo:]
                       + jnp.dot(vt_ref[e, :, pl.ds(start, tk)], p, preferred_element_type=F32))
            if lo:
                m_new = jnp.concatenate([m[:, :lo], m_new], axis=1)
                acc_new = jnp.concatenate([acc[:, :lo], acc_new], axis=1)
            out.append((m_new, acc_new))
        return tuple(out)

    def body(i, carry):
        base = i * r
        for u in range(r):
            scores((u + 1) % 2, base + u + 1)
            carry = process(u % 2, base + u, carry)
        return carry

    init = tuple((jnp.full((1, tq), NEG_BIG, F32), jnp.zeros((V_ROWS, tq), F32)) for _ in range(HEADS_PER_STEP))
    scores(0, 0)
    carry = lax.fori_loop(0, qi, body, init)
    base = qi * r
    for u in range(r):
        if u + 1 < r:
            scores((u + 1) % 2, base + u + 1, lo=(u + 1) * tk)
        carry = process(u % 2, base + u, carry, lo=u * tk, diagonal=True)
    _attn_finish(tuple(c[1] for c in carry), zs_ref, o_ref)


def _attn_finish(accs, zs_ref, o_ref):
    outs = [acc[:V_HEAD] / acc[V_HEAD:V_HEAD + 1] for acc in accs]
    o_ref[...] = (jnp.concatenate(outs, axis=0) * zs_ref[...].astype(F32)).astype(o_ref.dtype)


def _attn_bounded_kernel(qt_ref, k_ref, vt_ref, zs_ref, o_ref, p_scr):
    tq, tk = ATTN_TQ, ATTN_TK
    r = tq // tk
    assert r * tk == tq and r % 2 == 0
    qi = pl.program_id(2)
    q0 = qi * tq

    def probs(slot, t, lo=0, may_cross=False):
        start = pl.multiple_of(t * tk, tk)
        for e in range(HEADS_PER_STEP):
            p = jnp.exp2(jnp.dot(k_ref[e, pl.ds(start, tk), :], qt_ref[e, :, lo:], preferred_element_type=F32))
            if may_cross:
                blk = p[:, :tk]
                key = start + lax.broadcasted_iota(jnp.int32, blk.shape, 0)
                qry = q0 + lo + lax.broadcasted_iota(jnp.int32, blk.shape, 1)
                blk = jnp.where(key <= qry, blk, 0.0)
                p = jnp.concatenate([blk, p[:, tk:]], axis=1) if p.shape[1] > tk else blk
            p_scr[slot, e, :, lo:] = p.astype(BF16)

    def accumulate(slot, t, accs, lo=0):
        start = pl.multiple_of(t * tk, tk)
        out = []
        for e in range(HEADS_PER_STEP):
            pv = jnp.dot(vt_ref[e, :, pl.ds(start, tk)], p_scr[slot, e, :, lo:], preferred_element_type=F32)
            acc = accs[e]
            out.append(acc + pv if lo == 0 else jnp.concatenate([acc[:, :lo], acc[:, lo:] + pv], axis=1))
        return tuple(out)

    def body(i, accs):
        base = i * r
        for u in range(r):
            probs((u + 1) % 2, base + u + 1, may_cross=(u == r - 1))
            accs = accumulate(u % 2, base + u, accs)
        return accs

    probs(0, 0, may_cross=True)
    accs = lax.fori_loop(0, qi, body, tuple(jnp.zeros((V_ROWS, tq), F32) for _ in range(HEADS_PER_STEP)))
    base = qi * r
    for u in range(r):
        if u + 1 < r:
            probs((u + 1) % 2, base + u + 1, lo=(u + 1) * tk, may_cross=True)
        accs = accumulate(u % 2, base + u, accs, lo=u * tk)
    _attn_finish(accs, zs_ref, o_ref)


SCORE_BOUND_LOG2 = 60.0


def _attention(qt, k, vt, zs, score_bound):
    n_b, n_h, seq, _ = k.shape
    tq = ATTN_TQ
    assert seq % tq == 0 and n_h % HEADS_PER_STEP == 0
    nq = seq // tq
    hp = HEADS_PER_STEP
    width = hp * V_HEAD

    def call(body, scratch):
        return pl.pallas_call(
            body,
            out_shape=jax.ShapeDtypeStruct(zs.shape, BF16),
            grid=(n_b, n_h // hp, nq),
            in_specs=[pl.BlockSpec((None, hp, LANES, tq), lambda b, h, i: (b, h, 0, i)),
                      pl.BlockSpec((None, hp, seq, LANES), lambda b, h, i: (b, h, 0, 0)),
                      pl.BlockSpec((None, hp, V_ROWS, seq), lambda b, h, i: (b, h, 0, 0)),
                      pl.BlockSpec((width, tq), lambda b, h, i: (h, b * nq + i))],
            out_specs=pl.BlockSpec((width, tq), lambda b, h, i: (h, b * nq + i)),
            scratch_shapes=scratch,
            compiler_params=pltpu.CompilerParams(dimension_semantics=("arbitrary", "arbitrary", "arbitrary"),
                                                 vmem_limit_bytes=VMEM_LIMIT),
            name="causal_attention",
        )(qt, k, vt, zs)

    tiles = (2, hp, ATTN_TK, tq)
    return lax.cond(score_bound <= SCORE_BOUND_LOG2,
                    lambda: call(_attn_bounded_kernel, [pltpu.VMEM(tiles, BF16)]),
                    lambda: call(_attn_kernel, [pltpu.VMEM(tiles, F32)]))


def _s5_scan_steps(n_top):
    return [CHUNK * m for m in range(SUPER)] + [CHUNK * SUPER * (1 << i) for i in range(n_top)]


def _s5_exponents(n_top):
    return sorted(set(range(CHUNK + 1)) | set(_s5_scan_steps(n_top)))


def _s5_coef_kernel(kk_ref, *refs, n_top):
    for gi in range(SCAN_GROUPS):
        _s5_coef_one_group(kk_ref, *(r.at[gi] for r in refs), n_top=n_top)


def _s5_coef_one_group(kk_ref, lre_ref, lim_ref, ldt_ref, bre_ref, bim_ref, cre_ref, cim_ref, d_ref,
                       lhs_ref, wout_ref, ap_ref, *, n_top):
    h, p, n = SSM_GROUP, SSM_STATE, CHUNK
    lre, lim = lre_ref[...], lim_ref[...]
    dt = jnp.exp(ldt_ref[...])
    exps = _s5_exponents(n_top)
    kcol = kk_ref[...]
    mag = jnp.exp(kcol * (lre * dt))
    ang = kcol * (lim * dt)
    pow_re, pow_im = mag * jnp.cos(ang), mag * jnp.sin(ang)

    def power(kk):
        i = exps.index(kk)
        return pow_re[i:i + 1], pow_im[i:i + 1]

    lb_re, lb_im = power(1)
    nr, ni = lb_re - 1.0, lb_im
    den = lre * lre + lim * lim
    f_re = (nr * lre + ni * lim) / den
    f_im = (ni * lre - nr * lim) / den
    bre, bim = bre_ref[...].T, bim_ref[...].T
    bb_re = f_re * bre - f_im * bim
    bb_im = f_re * bim + f_im * bre
    cre, cim = cre_ref[...], cim_ref[...]
    cp_re, cp_im = [], []
    for kk in range(n + 1):
        pr, pi = power(kk)
        cp_re.append(cre * pr - cim * pi)
        cp_im.append(cre * pi + cim * pr)

    cpw = [jnp.concatenate([cp_re[kk], -cp_im[kk]], axis=1) for kk in range(n + 1)]
    bb_a = jnp.concatenate([bb_re, bb_im], axis=1)
    kw = lax.dot_general(jnp.concatenate(cpw[:n], axis=0), jnp.concatenate([bb_a] * n, axis=0), _NT,
                         preferred_element_type=F32, precision=HIGHEST)
    lane = lax.broadcasted_iota(jnp.int32, (h, n * h), 1)
    row = lax.broadcasted_iota(jnp.int32, (h, n * h), 0)
    col_blk = lax.shift_right_logical(lane, 4)
    d_tiled = jnp.concatenate([d_ref[...]] * n, axis=1)
    lag = [kw[h * kk:h * (kk + 1)] for kk in range(n)]
    lag[0] = lag[0] + jnp.where((lane & (h - 1)) == row, d_tiled, 0.0)
    rows = []
    for j in range(n):
        acc = jnp.zeros((h, n * h), F32)
        for jp in range(j + 1):
            acc = jnp.where(col_blk == jp, lag[j - jp], acc)
        rows.append(acc)
    lhs_ref[0:n * h, :] = jnp.concatenate(rows, axis=0).astype(lhs_ref.dtype)

    bb_b = jnp.concatenate([-bb_im, bb_re], axis=1)
    win = []
    for j in range(n):
        pr, pi = power(n - 1 - j)
        win.append(jnp.concatenate([pr, pr], axis=1) * bb_a + jnp.concatenate([pi, pi], axis=1) * bb_b)
    lhs_ref[n * h:, :] = jnp.concatenate(win, axis=0).T.astype(lhs_ref.dtype)
    wout_ref[...] = jnp.concatenate(cpw[1:], axis=0).astype(wout_ref.dtype)
    for i, kk in enumerate(_s5_scan_steps(n_top)):
        pr, pi = power(kk)
        ap_ref[i:i + 1, :] = jnp.concatenate([pr, pi], axis=1)


def _s5_coefficients(log_dt, lam_re, lam_im, b_re, b_im, c_re, c_im, d_skip, n_top):
    g, p, h = SSM_GROUPS, SSM_STATE, SSM_GROUP
    n_ap = SUPER + n_top
    gb = SCAN_GROUPS
    grp = lambda *shape: pl.BlockSpec((gb,) + shape, lambda i: (i,) + (0,) * len(shape))
    exps = np.asarray(_s5_exponents(n_top), np.float32)
    kk = np.zeros((-(-exps.size // 8) * 8, 1), np.float32)
    kk[:exps.size, 0] = exps
    return pl.pallas_call(
        functools.partial(_s5_coef_kernel, n_top=n_top),
        out_shape=(jax.ShapeDtypeStruct((g, CHUNK * h + 2 * p, CHUNK * h), BF16),
                   jax.ShapeDtypeStruct((g, CHUNK * h, 2 * p), BF16),
                   jax.ShapeDtypeStruct((g, n_ap, 2 * p), F32)),
        grid=(g // gb,),
        in_specs=[pl.BlockSpec(kk.shape, lambda i: (0, 0)),
                  grp(1, p), grp(1, p), grp(1, 1), grp(p, h), grp(p, h), grp(h, p), grp(h, p), grp(1, h)],
        out_specs=(grp(CHUNK * h + 2 * p, CHUNK * h), grp(CHUNK * h, 2 * p), grp(n_ap, 2 * p)),
        name="s5_coefficients",
    )(jnp.asarray(kk), lam_re.reshape(g, 1, p), lam_im.reshape(g, 1, p), log_dt.reshape(g, 1, 1), b_re, b_im,
      c_re, c_im, d_skip.reshape(g, 1, h))


PHASES = 8


def _ssm_proj_kernel(h_ref, w_ref, u_ref, z_ref, rows_scr, *, n_b):
    for jj in range(PHASES):
        for b in range(n_b):
            rows_scr[b] = h_ref[b, :, jj, :]
        h = jnp.concatenate([rows_scr[b].astype(BF16) for b in range(n_b)], axis=0)
        r = lax.dot_general(w_ref[...], h, _NT, preferred_element_type=F32)
        u_ref[jj] = r[:SSM_WIDTH].astype(BF16)
        z_ref[jj] = _silu(r[SSM_WIDTH:]).astype(BF16)


def _ssm_projection(h4, w_in):
    n_b, c2, _, d = h4.shape
    lanes = n_b * c2
    halves = CHUNK // PHASES
    w_t = w_in[:, :2 * SSM_WIDTH].T.astype(BF16)
    out = jax.ShapeDtypeStruct((SUPER, CHUNK, SSM_WIDTH, lanes), BF16)
    out_spec = pl.BlockSpec((None, PHASES, SSM_WIDTH, lanes), lambda j2, hf: (j2, hf, 0, 0))
    return pl.pallas_call(
        functools.partial(_ssm_proj_kernel, n_b=n_b),
        out_shape=(out, out),
        grid=(SUPER, halves),
        in_specs=[pl.BlockSpec((n_b, c2, PHASES, d), lambda j2, hf: (0, 0, j2 * halves + hf, 0)),
                  pl.BlockSpec(w_t.shape, lambda j2, hf: (0, 0))],
        out_specs=(out_spec, out_spec),
        scratch_shapes=[pltpu.VMEM((n_b, c2, d), F32)],
        compiler_params=pltpu.CompilerParams(dimension_semantics=("arbitrary", "arbitrary"),
                                             vmem_limit_bytes=VMEM_LIMIT),
        name="ssm_projection",
    )(h4, w_t)


def _cmul(ar, ai, xr, xi):
    return ar * xr - ai * xi, ar * xi + ai * xr


SCAN_GROUPS = 8


def _s5_group_kernel(x_ref, lhs_ref, wout_ref, ap_ref, y_ref, *, n_b, c2, n_top):
    p, gb = SSM_STATE, SCAN_GROUPS
    rows = CHUNK * SSM_GROUP
    n = SUPER * n_b * c2
    lanes = n_b * c2

    def x_group(gi):
        return jnp.concatenate([x_ref[j2, :, SSM_GROUP * gi:SSM_GROUP * (gi + 1), :].reshape(rows, lanes)
                                for j2 in range(SUPER)], axis=1)

    r = [jnp.dot(lhs_ref[gi], x_group(gi), preferred_element_type=F32) for gi in range(gb)]
    c_re = jnp.concatenate([r[gi][rows:rows + p] for gi in range(gb)], axis=0)
    c_im = jnp.concatenate([r[gi][rows + p:] for gi in range(gb)], axis=0)

    def mult(i):
        rep = [jnp.broadcast_to(ap_ref[gi, i:i + 1, :], (2 * p, 2 * p)).T for gi in range(gb)]
        return (jnp.concatenate([m[:p, :c2] for m in rep], axis=0),
                jnp.concatenate([m[p:, :c2] for m in rep], axis=0))

    lane = lax.broadcasted_iota(jnp.int32, (gb * p, c2), 1)

    def shifted(a, sh):
        return jnp.where(lane >= sh, pltpu.roll(a, sh, 1), 0.0)

    pieces = [[None] * n_b for _ in range(SUPER)]
    for b in range(n_b):
        piece = lambda a, j2: a[:, (j2 * n_b + b) * c2:(j2 * n_b + b + 1) * c2]
        a_re, a_im = mult(1)
        e_re = jnp.zeros((gb * p, c2), F32)
        e_im = jnp.zeros((gb * p, c2), F32)
        local = []
        for j2 in range(SUPER):
            local.append((e_re, e_im))
            t_re, t_im = _cmul(a_re, a_im, e_re, e_im)
            e_re, e_im = t_re + piece(c_re, j2), t_im + piece(c_im, j2)
        for i in range(n_top):
            m_re, m_im = mult(SUPER + i)
            t_re, t_im = _cmul(m_re, m_im, shifted(e_re, 1 << i), shifted(e_im, 1 << i))
            e_re, e_im = e_re + t_re, e_im + t_im
        s_re, s_im = shifted(e_re, 1), shifted(e_im, 1)
        for j2 in range(SUPER):
            if j2 == 0:
                pieces[j2][b] = (s_re, s_im)
            else:
                m_re, m_im = mult(j2)
                t_re, t_im = _cmul(m_re, m_im, s_re, s_im)
                pieces[j2][b] = (local[j2][0] + t_re, local[j2][1] + t_im)
    for gi in range(gb):
        sl = slice(gi * p, (gi + 1) * p)
        s_in = jnp.concatenate([jnp.concatenate([pieces[j2][b][0][sl], pieces[j2][b][1][sl]], axis=0)
                                for j2 in range(SUPER) for b in range(n_b)], axis=1)
        y = r[gi][:rows] + jnp.dot(wout_ref[gi], s_in.astype(BF16), preferred_element_type=F32)
        for j2 in range(SUPER):
            y_ref[j2, gi] = y[:, j2 * lanes:(j2 + 1) * lanes].reshape(CHUNK, SSM_GROUP, lanes).astype(y_ref.dtype)


def _s5_scan(u_t, lhs, wout, ap, n_b, c2, n_top):
    g, h = SSM_GROUPS, SSM_GROUP
    lanes = u_t.shape[-1]
    gb = SCAN_GROUPS
    grp = lambda *shape: pl.BlockSpec((gb,) + shape, lambda i: (i,) + (0,) * len(shape))
    return pl.pallas_call(
        functools.partial(_s5_group_kernel, n_b=n_b, c2=c2, n_top=n_top),
        out_shape=jax.ShapeDtypeStruct((SUPER, g, CHUNK, h, lanes), BF16),
        grid=(g // gb,),
        in_specs=[pl.BlockSpec((SUPER, CHUNK, gb * h, lanes), lambda i: (0, 0, i, 0)),
                  grp(*lhs.shape[1:]), grp(*wout.shape[1:]), grp(*ap.shape[1:])],
        out_specs=pl.BlockSpec((SUPER, gb, CHUNK, h, lanes), lambda i: (0, i, 0, 0, 0)),
        compiler_params=pltpu.CompilerParams(dimension_semantics=("arbitrary",),
                                             vmem_limit_bytes=VMEM_LIMIT),
        name="s5_chunk_scan",
    )(u_t, lhs, wout, ap)


def _glu_kernel(y_ref, zs_ref, w_ref, b_ref, o_ref, *, n_b, c2):
    g, _, h, n = y_ref.shape
    for jj in range(PHASES):
        y = jax.nn.gelu(y_ref[:, jj].reshape(g * h, n).astype(F32))
        t = jnp.dot(w_ref[...], y.astype(BF16), preferred_element_type=F32) + b_ref[...]
        gated = (y * jax.nn.sigmoid(t) * zs_ref[jj].astype(F32)).T
        for b in range(n_b):
            o_ref[b, :, jj, :] = gated[b * c2:(b + 1) * c2]


def _glu(y_t, zs_t, w_glu, b_glu, n_b, c2):
    _, g, _, h, _ = y_t.shape
    lanes = n_b * c2
    halves = CHUNK // PHASES
    w_t = w_glu.T.astype(BF16)
    return pl.pallas_call(
        functools.partial(_glu_kernel, n_b=n_b, c2=c2),
        out_shape=jax.ShapeDtypeStruct((n_b, c2, SUPER * CHUNK, SSM_WIDTH), F32),
        grid=(SUPER, halves),
        in_specs=[pl.BlockSpec((None, g, PHASES, h, lanes), lambda j2, hf: (j2, 0, hf, 0, 0)),
                  pl.BlockSpec((None, PHASES, SSM_WIDTH, lanes), lambda j2, hf: (j2, hf, 0, 0)),
                  pl.BlockSpec(w_t.shape, lambda j2, hf: (0, 0)),
                  pl.BlockSpec((SSM_WIDTH, 1), lambda j2, hf: (0, 0))],
        out_specs=pl.BlockSpec((n_b, c2, PHASES, SSM_WIDTH), lambda j2, hf: (0, 0, j2 * halves + hf, 0)),
        compiler_params=pltpu.CompilerParams(dimension_semantics=("arbitrary", "arbitrary"),
                                             vmem_limit_bytes=VMEM_LIMIT),
        name="s5_glu",
    )(y_t, zs_t, w_t, b_glu.reshape(SSM_WIDTH, 1))


def _out_proj_kernel(ys_ref, ym_ref, x_ref, mod_ref, wa_ref, wb_ref, o_ref):
    y = (jnp.dot(ys_ref[...].astype(BF16), wa_ref[...], preferred_element_type=F32)
         + lax.dot_general(ym_ref[...], wb_ref[...], _TN, preferred_element_type=F32))
    gate = mod_ref[...][:, 2 * D_MODEL:]
    o_ref[...] = x_ref[...] + gate * y


def _out_projection(ys, ym, x2, mod3, w_out, seq):
    t, d = x2.shape
    tm = min(1024, seq)
    per_b = seq // tm
    wa = w_out[:SSM_WIDTH].astype(BF16)
    wb = w_out[SSM_WIDTH:].astype(BF16)
    tok = lambda w: pl.BlockSpec((tm, w), lambda i: (i, 0))
    return pl.pallas_call(
        _out_proj_kernel,
        out_shape=jax.ShapeDtypeStruct(x2.shape, F32),
        grid=(t // tm,),
        in_specs=[tok(SSM_WIDTH), pl.BlockSpec((MLA_WIDTH, tm), lambda i: (0, i)), tok(d),
                  pl.BlockSpec((None, 1, 3 * d), lambda i: (i // per_b, 0, 0)),
                  pl.BlockSpec(wa.shape, lambda i: (0, 0)),
                  pl.BlockSpec(wb.shape, lambda i: (0, 0))],
        out_specs=tok(d),
        compiler_params=pltpu.CompilerParams(dimension_semantics=("arbitrary",),
                                             vmem_limit_bytes=VMEM_LIMIT),
        name="output_projection",
    )(ys, ym, x2, mod3, wa, wb)


def kernel(x, c, positions, w_ada, b_ada, norm_g, w_in, log_dt, lam_re, lam_im, b_re, b_im, c_re, c_im, d_skip,
           w_glu, b_glu, q_a_g, w_q_b, kv_a_g, w_kv_b, q_norm_g, k_norm_g, w_out):
    n_b, seq, d = x.shape
    depth = w_ada.shape[0]
    c2 = seq // (CHUNK * SUPER)
    n_top = max(int(math.log2(c2)), 0)
    assert c2 * CHUNK * SUPER == seq and (1 << n_top) == c2
    for l in range(depth):
        mod3 = _modulation(c, w_ada[l], b_ada[l]).reshape(n_b, 1, 3 * d)
        x2 = x.reshape(n_b * seq, d)
        lhs, wout, ap = _s5_coefficients(log_dt[l], lam_re[l], lam_im[l], b_re[l], b_im[l], c_re[l], c_im[l],
                                         d_skip[l], n_top)
        qt, k, vt, zm, h = _mla_projection(x2, mod3, norm_g[l], w_in[l], q_a_g[l], w_q_b[l],
                                           kv_a_g[l], w_kv_b[l], q_norm_g[l], k_norm_g[l], positions, n_b, seq)
        score_bound = (1.05 * math.sqrt(QK_HEAD) * math.log2(math.e)
                       * jnp.max(jnp.abs(q_norm_g[l])) * jnp.max(jnp.abs(k_norm_g[l])))
        ym = _attention(qt, k, vt, zm, score_bound)
        ym, h = lax.optimization_barrier((ym, h))
        u_t, zs_t = _ssm_projection(h.reshape(n_b, c2, SUPER * CHUNK, d), w_in[l])
        y_t = _s5_scan(u_t, lhs, wout, ap, n_b, c2, n_top)
        ys = _glu(y_t, zs_t, w_glu[l], b_glu[l], n_b, c2).reshape(n_b * seq, SSM_WIDTH)
        x = _out_projection(ys, ym, x2, mod3, w_out[l], seq).reshape(n_b, seq, d)
    return x
```

```python
import functools
import math

import numpy as np
import jax
import jax.numpy as jnp
from jax import lax
from jax.experimental import pallas as pl
from jax.experimental.pallas import tpu as pltpu

F32 = jnp.float32
BF16 = jnp.bfloat16
HIGHEST = lax.Precision.HIGHEST

D_MODEL = 1024
SSM_WIDTH = 512
SSM_GROUP = 16
SSM_GROUPS = 32
SSM_STATE = 64
MLA_HEADS = 8
QK_NOPE = 64
QK_ROPE = 32
QK_HEAD = QK_NOPE + QK_ROPE
V_HEAD = 64
MLA_WIDTH = MLA_HEADS * V_HEAD
Q_LORA = 384
KV_LORA = 256
ROPE_THETA = 10000.0
EPS = 1e-6
NEG_BIG = -1e30

LANES = 128
VMEM_LIMIT = 60 * 1024 * 1024
CHUNK = 16
SUPER = 4
ROPE_HALF = QK_ROPE // 2

_NT = (((1,), (1,)), ((), ()))
_TN = (((0,), (0,)), ((), ()))


def _silu(v):
    return v * jax.nn.sigmoid(v)


def _rms(v, gain):
    return v * lax.rsqrt(jnp.mean(v * v, axis=-1, keepdims=True) + EPS) * gain


def _mod_kernel(c_ref, w_ref, b_ref, o_ref):
    n_b, d = c_ref.shape
    tn = w_ref.shape[1]
    act_t = _silu(c_ref[...]).T
    out = []
    for b in range(n_b):
        col = jnp.broadcast_to(act_t[:, b:b + 1], (d, LANES))
        out.append(jnp.concatenate([jnp.sum(w_ref[:, j:j + LANES] * col, axis=0, keepdims=True)
                                    for j in range(0, tn, LANES)], axis=1))
    o_ref[...] = jnp.concatenate(out, axis=0) + b_ref[...]


def _modulation(c, w, b):
    n_b, d = c.shape
    tn = 512
    return pl.pallas_call(
        _mod_kernel,
        out_shape=jax.ShapeDtypeStruct((n_b, 3 * d), F32),
        grid=(3 * d // tn,),
        in_specs=[pl.BlockSpec((n_b, d), lambda n: (0, 0)),
                  pl.BlockSpec((d, tn), lambda n: (0, n)),
                  pl.BlockSpec((1, tn), lambda n: (0, n))],
        out_specs=pl.BlockSpec((n_b, tn), lambda n: (0, n)),
        name="adaln_modulation",
    )(c, w, b.reshape(1, -1))


def _rope_frequencies():
    inv_freq = ROPE_THETA ** (-np.arange(ROPE_HALF, dtype=np.float64) * 2.0 / QK_ROPE)
    return jnp.asarray(inv_freq.astype(np.float32)).reshape(ROPE_HALF, 1)


def _slab_source():
    src = np.full((LANES,), -1, np.int64)
    src[0:16] = QK_NOPE + np.arange(16)
    src[16:64] = np.arange(48)
    src[64:80] = QK_NOPE + 16 + np.arange(16)
    src[80:96] = 48 + np.arange(16)
    return src


def _to_slabs(w, per_head, src):
    k = w.shape[0]
    w3 = w.reshape(k, MLA_HEADS, per_head)
    valid = src >= 0
    cols = jnp.where(valid[None, None, :], w3[:, :, np.clip(src, 0, per_head - 1)], 0.0)
    return cols.reshape(k, MLA_HEADS * LANES)


V_ROWS = V_HEAD + 16


def _mla_proj_kernel(x_ref, mod_ref, ng_ref, w1_ref, wzt_ref, qag_ref, kvag_ref, wqt_ref, wkt_ref, wvt_ref,
                     gq_ref, gk_ref, pos_ref, freq_ref, qt_ref, k_ref, vt_ref, zs_ref, h_ref):
    d = D_MODEL
    tm = x_ref.shape[0]
    x = x_ref[...]
    mod = mod_ref[...]
    shift, scale = mod[:, :d], mod[:, d:2 * d]
    h_ref[...] = _rms(x, ng_ref[...] * (1.0 + scale)) + shift
    h = h_ref[...].astype(BF16)
    p1 = jnp.dot(h, w1_ref[...], preferred_element_type=F32)
    o1, o2 = Q_LORA, Q_LORA + KV_LORA
    cqn = _rms(p1[:, :o1], qag_ref[...]).astype(BF16)
    ckvn = _rms(p1[:, o1:o2], kvag_ref[...]).astype(BF16)
    zs_ref[...] = _silu(lax.dot_general(wzt_ref[...], h, _NT, preferred_element_type=F32)).astype(BF16)

    qt = lax.dot_general(wqt_ref[...], cqn, _NT, preferred_element_type=F32)
    kt = lax.dot_general(wkt_ref[...], ckvn, _NT, preferred_element_type=F32)
    krt = p1[:, o2:].T
    vt = lax.dot_general(wvt_ref[...], ckvn, _NT, preferred_element_type=F32)
    ang = freq_ref[...] * pos_ref[...].astype(F32)
    cos_t, sin_t = jnp.cos(ang), jnp.sin(ang)
    gq = jnp.concatenate([gq_ref[...]] * (tm // LANES), axis=1)
    gk = jnp.concatenate([gk_ref[...]] * (tm // LANES), axis=1)
    first = lax.broadcasted_iota(jnp.int32, (V_ROWS - V_HEAD, tm), 0) == 0
    ones_rows = jnp.where(first, 1.0, 0.0).astype(BF16)
    r, half = ROPE_HALF, LANES // 2

    def norm_rope(slab, gain):
        ss = jnp.sum(slab * slab, axis=0, keepdims=True)
        n = slab * lax.rsqrt(ss * (1.0 / QK_HEAD) + EPS) * gain
        x1, x2 = n[0:r], n[half:half + r]
        return jnp.concatenate([x1 * cos_t - x2 * sin_t, n[r:half], x2 * cos_t + x1 * sin_t, n[half + r:]], axis=0)

    for head in range(MLA_HEADS):
        rows = slice(LANES * head, LANES * (head + 1))
        qt_ref[head] = norm_rope(qt[rows], gq).astype(BF16)
        k_ref[head] = norm_rope(kt[rows] + krt, gk).T.astype(BF16)
        vt_ref[head, 0:V_HEAD, :] = vt[V_HEAD * head:V_HEAD * (head + 1)].astype(BF16)
        vt_ref[head, V_HEAD:, :] = ones_rows


def _mla_projection(x2, mod3, norm_g, w_in, q_a_g, w_q_b, kv_a_g, w_kv_b, q_norm_g, k_norm_g, positions, n_b, seq):
    t, d = x2.shape
    tm = min(512, seq)
    per_b = seq // tm
    o2 = 2 * SSM_WIDTH
    o4, o5 = o2 + Q_LORA + KV_LORA, o2 + Q_LORA + KV_LORA + QK_ROPE
    src = _slab_source()
    valid = src >= 0
    rope_src = np.where(src >= QK_NOPE, src - QK_NOPE, -1)
    nope_src = np.where(valid & (src < QK_NOPE), src, -1)
    wkr = jnp.where((rope_src >= 0)[None, :], w_in[:, o4:o5][:, np.clip(rope_src, 0, QK_ROPE - 1)], 0.0)
    w1 = jnp.concatenate([w_in[:, o2:o4], wkr], axis=1).astype(BF16)
    wzt = w_in[:, o5:].T.astype(BF16)
    wqt = _to_slabs(w_q_b, QK_HEAD, src).T.astype(BF16)
    wkt = _to_slabs(w_kv_b, QK_NOPE + V_HEAD, nope_src).T.astype(BF16)
    wvt = w_kv_b.reshape(KV_LORA, MLA_HEADS, QK_NOPE + V_HEAD)[:, :, QK_NOPE:].reshape(KV_LORA, MLA_WIDTH).T.astype(BF16)
    q_scale = math.log2(math.e) / math.sqrt(QK_HEAD)
    slab_gain = lambda g: jnp.broadcast_to(jnp.where(valid, g[np.clip(src, 0, QK_HEAD - 1)], 0.0)[:, None],
                                           (LANES, LANES))
    gq = slab_gain(q_norm_g * q_scale)
    gk = slab_gain(k_norm_g)
    const = lambda shape: pl.BlockSpec(shape, lambda i: (0,) * len(shape))
    tok = lambda i: (i // per_b, 0, 0, i % per_b)
    return pl.pallas_call(
        _mla_proj_kernel,
        out_shape=(jax.ShapeDtypeStruct((n_b, MLA_HEADS, LANES, seq), BF16),
                   jax.ShapeDtypeStruct((n_b, MLA_HEADS, seq, LANES), BF16),
                   jax.ShapeDtypeStruct((n_b, MLA_HEADS, V_ROWS, seq), BF16),
                   jax.ShapeDtypeStruct((MLA_WIDTH, t), BF16),
                   jax.ShapeDtypeStruct((t, d), F32)),
        grid=(t // tm,),
        in_specs=[pl.BlockSpec((tm, d), lambda i: (i, 0)),
                  pl.BlockSpec((None, 1, 3 * d), lambda i: (i // per_b, 0, 0)),
                  const((1, d)), const(w1.shape), const(wzt.shape), const((1, Q_LORA)), const((1, KV_LORA)),
                  const(wqt.shape), const(wkt.shape), const(wvt.shape),
                  const((LANES, LANES)), const((LANES, LANES)),
                  pl.BlockSpec((1, tm), lambda i: (0, i)),
                  const((ROPE_HALF, 1))],
        out_specs=(pl.BlockSpec((None, MLA_HEADS, LANES, tm), tok),
                   pl.BlockSpec((None, MLA_HEADS, tm, LANES), lambda i: (i // per_b, 0, i % per_b, 0)),
                   pl.BlockSpec((None, MLA_HEADS, V_ROWS, tm), tok),
                   pl.BlockSpec((MLA_WIDTH, tm), lambda i: (0, i)),
                   pl.BlockSpec((tm, d), lambda i: (i, 0))),
        compiler_params=pltpu.CompilerParams(dimension_semantics=("arbitrary",),
                                             vmem_limit_bytes=VMEM_LIMIT),
        name="mla_projection",
    )(x2, mod3, norm_g.reshape(1, d), w1, wzt, q_a_g.reshape(1, -1), kv_a_g.reshape(1, -1), wqt, wkt, wvt, gq, gk,
      positions.reshape(1, t), _rope_frequencies())


ATTN_TQ = 2048
ATTN_TK = 256
HEADS_PER_STEP = 2


def _attn_kernel(qt_ref, k_ref, vt_ref, zs_ref, o_ref, s_scr):
    tq, tk = ATTN_TQ, ATTN_TK
    r = tq // tk
    assert r * tk == tq and r % 2 == 0
    qi = pl.program_id(2)

    def scores(slot, t, lo=0):
        start = pl.multiple_of(t * tk, tk)
        for e in range(HEADS_PER_STEP):
            s_scr[slot, e, :, lo:] = jnp.dot(k_ref[e, pl.ds(start, tk), :], qt_ref[e, :, lo:],
                                             preferred_element_type=F32)

    def process(slot, t, carry, lo=0, diagonal=False):
        start = pl.multiple_of(t * tk, tk)
        out = []
        for e in range(HEADS_PER_STEP):
            m, acc = carry[e]
            s = s_scr[slot, e, :, lo:]
            if diagonal:
                blk = s[:, :tk]
                ok = lax.broadcasted_iota(jnp.int32, blk.shape, 0) <= lax.broadcasted_iota(jnp.int32, blk.shape, 1)
                blk = jnp.where(ok, blk, NEG_BIG)
                s = jnp.concatenate([blk, s[:, tk:]], axis=1) if s.shape[1] > tk else blk
            m_old = m[:, lo:]
            m_new = jnp.maximum(m_old, jnp.max(s, axis=0, keepdims=True))
            p = jnp.exp2(s - m_new).astype(BF16)
            acc_new = (jnp.exp2(m_old - m_new) * acc[:, lo:]
                       + jnp.dot(vt_ref[e, :, pl.ds(start, tk)], p, preferred_element_type=F32))
            if lo:
                m_new = jnp.concatenate([m[:, :lo], m_new], axis=1)
                acc_new = jnp.concatenate([acc[:, :lo], acc_new], axis=1)
            out.append((m_new, acc_new))
        return tuple(out)

    def body(i, carry):
        base = i * r
        for u in range(r):
            scores((u + 1) % 2, base + u + 1)
            carry = process(u % 2, base + u, carry)
        return carry

    init = tuple((jnp.full((1, tq), NEG_BIG, F32), jnp.zeros((V_ROWS, tq), F32)) for _ in range(HEADS_PER_STEP))
    scores(0, 0)
    carry = lax.fori_loop(0, qi, body, init)
    base = qi * r
    for u in range(r):
        if u + 1 < r:
            scores((u + 1) % 2, base + u + 1, lo=(u + 1) * tk)
        carry = process(u % 2, base + u, carry, lo=u * tk, diagonal=True)
    _attn_finish(tuple(c[1] for c in carry), zs_ref, o_ref)


def _attn_finish(accs, zs_ref, o_ref):
    outs = [acc[:V_HEAD] / acc[V_HEAD:V_HEAD + 1] for acc in accs]
    o_ref[...] = (jnp.concatenate(outs, axis=0) * zs_ref[...].astype(F32)).astype(o_ref.dtype)


def _attn_bounded_kernel(qt_ref, k_ref, vt_ref, zs_ref, o_ref, p_scr):
    tq, tk = ATTN_TQ, ATTN_TK
    r = tq // tk
    assert r * tk == tq and r % 2 == 0
    qi = pl.program_id(2)
    q0 = qi * tq

    def probs(slot, t, lo=0, may_cross=False):
        start = pl.multiple_of(t * tk, tk)
        for e in range(HEADS_PER_STEP):
            p = jnp.exp2(jnp.dot(k_ref[e, pl.ds(start, tk), :], qt_ref[e, :, lo:], preferred_element_type=F32))
            if may_cross:
                blk = p[:, :tk]
                key = start + lax.broadcasted_iota(jnp.int32, blk.shape, 0)
                qry = q0 + lo + lax.broadcasted_iota(jnp.int32, blk.shape, 1)
                blk = jnp.where(key <= qry, blk, 0.0)
                p = jnp.concatenate([blk, p[:, tk:]], axis=1) if p.shape[1] > tk else blk
            p_scr[slot, e, :, lo:] = p.astype(BF16)

    def accumulate(slot, t, accs, lo=0):
        start = pl.multiple_of(t * tk, tk)
        out = []
        for e in range(HEADS_PER_STEP):
            pv = jnp.dot(vt_ref[e, :, pl.ds(start, tk)], p_scr[slot, e, :, lo:], preferred_element_type=F32)
            acc = accs[e]
            out.append(acc + pv if lo == 0 else jnp.concatenate([acc[:, :lo], acc[:, lo:] + pv], axis=1))
        return tuple(out)

    def body(i, accs):
        base = i * r
        for u in range(r):
            probs((u + 1) % 2, base + u + 1, may_cross=(u == r - 1))
            accs = accumulate(u % 2, base + u, accs)
        return accs

    probs(0, 0, may_cross=True)
    accs = lax.fori_loop(0, qi, body, tuple(jnp.zeros((V_ROWS, tq), F32) for _ in range(HEADS_PER_STEP)))
    base = qi * r
    for u in range(r):
        if u + 1 < r:
            probs((u + 1) % 2, base + u + 1, lo=(u + 1) * tk, may_cross=True)
        accs = accumulate(u % 2, base + u, accs, lo=u * tk)
    _attn_finish(accs, zs_ref, o_ref)


SCORE_BOUND_LOG2 = 60.0


def _attention(qt, k, vt, zs, score_bound):
    n_b, n_h, seq, _ = k.shape
    tq = ATTN_TQ
    assert seq % tq == 0 and n_h % HEADS_PER_STEP == 0
    nq = seq // tq
    hp = HEADS_PER_STEP
    width = hp * V_HEAD

    def call(body, scratch):
        return pl.pallas_call(
            body,
            out_shape=jax.ShapeDtypeStruct(zs.shape, BF16),
            grid=(n_b, n_h // hp, nq),
            in_specs=[pl.BlockSpec((None, hp, LANES, tq), lambda b, h, i: (b, h, 0, i)),
                      pl.BlockSpec((None, hp, seq, LANES), lambda b, h, i: (b, h, 0, 0)),
                      pl.BlockSpec((None, hp, V_ROWS, seq), lambda b, h, i: (b, h, 0, 0)),
                      pl.BlockSpec((width, tq), lambda b, h, i: (h, b * nq + i))],
            out_specs=pl.BlockSpec((width, tq), lambda b, h, i: (h, b * nq + i)),
            scratch_shapes=scratch,
            compiler_params=pltpu.CompilerParams(dimension_semantics=("arbitrary", "arbitrary", "arbitrary"),
                                                 vmem_limit_bytes=VMEM_LIMIT),
            name="causal_attention",
        )(qt, k, vt, zs)

    tiles = (2, hp, ATTN_TK, tq)
    return lax.cond(score_bound <= SCORE_BOUND_LOG2,
                    lambda: call(_attn_bounded_kernel, [pltpu.VMEM(tiles, BF16)]),
                    lambda: call(_attn_kernel, [pltpu.VMEM(tiles, F32)]))


def _s5_scan_steps(n_top):
    return [CHUNK * m for m in range(SUPER)] + [CHUNK * SUPER * (1 << i) for i in range(n_top)]


def _s5_exponents(n_top):
    return sorted(set(range(CHUNK + 1)) | set(_s5_scan_steps(n_top)))


def _s5_coef_kernel(kk_ref, *refs, n_top):
    for gi in range(SCAN_GROUPS):
        _s5_coef_one_group(kk_ref, *(r.at[gi] for r in refs), n_top=n_top)


def _s5_coef_one_group(kk_ref, lre_ref, lim_ref, ldt_ref, bre_ref, bim_ref, cre_ref, cim_ref, d_ref,
                       lhs_ref, wout_ref, ap_ref, *, n_top):
    h, p, n = SSM_GROUP, SSM_STATE, CHUNK
    lre, lim = lre_ref[...], lim_ref[...]
    dt = jnp.exp(ldt_ref[...])
    exps = _s5_exponents(n_top)
    kcol = kk_ref[...]
    mag = jnp.exp(kcol * (lre * dt))
    ang = kcol * (lim * dt)
    pow_re, pow_im = mag * jnp.cos(ang), mag * jnp.sin(ang)

    def power(kk):
        i = exps.index(kk)
        return pow_re[i:i + 1], pow_im[i:i + 1]

    lb_re, lb_im = power(1)
    nr, ni = lb_re - 1.0, lb_im
    den = lre * lre + lim * lim
    f_re = (nr * lre + ni * lim) / den
    f_im = (ni * lre - nr * lim) / den
    bre, bim = bre_ref[...].T, bim_ref[...].T
    bb_re = f_re * bre - f_im * bim
    bb_im = f_re * bim + f_im * bre
    cre, cim = cre_ref[...], cim_ref[...]
    cp_re, cp_im = [], []
    for kk in range(n + 1):
        pr, pi = power(kk)
        cp_re.append(cre * pr - cim * pi)
        cp_im.append(cre * pi + cim * pr)

    cpw = [jnp.concatenate([cp_re[kk], -cp_im[kk]], axis=1) for kk in range(n + 1)]
    bb_a = jnp.concatenate([bb_re, bb_im], axis=1)
    kw = lax.dot_general(jnp.concatenate(cpw[:n], axis=0), jnp.concatenate([bb_a] * n, axis=0), _NT,
                         preferred_element_type=F32, precision=HIGHEST)
    lane = lax.broadcasted_iota(jnp.int32, (h, n * h), 1)
    row = lax.broadcasted_iota(jnp.int32, (h, n * h), 0)
    col_blk = lax.shift_right_logical(lane, 4)
    d_tiled = jnp.concatenate([d_ref[...]] * n, axis=1)
    lag = [kw[h * kk:h * (kk + 1)] for kk in range(n)]
    lag[0] = lag[0] + jnp.where((lane & (h - 1)) == row, d_tiled, 0.0)
    rows = []
    for j in range(n):
        acc = jnp.zeros((h, n * h), F32)
        for jp in range(j + 1):
            acc = jnp.where(col_blk == jp, lag[j - jp], acc)
        rows.append(acc)
    lhs_ref[0:n * h, :] = jnp.concatenate(rows, axis=0).astype(lhs_ref.dtype)

    bb_b = jnp.concatenate([-bb_im, bb_re], axis=1)
    win = []
    for j in range(n):
        pr, pi = power(n - 1 - j)
        win.append(jnp.concatenate([pr, pr], axis=1) * bb_a + jnp.concatenate([pi, pi], axis=1) * bb_b)
    lhs_ref[n * h:, :] = jnp.concatenate(win, axis=0).T.astype(lhs_ref.dtype)
    wout_ref[...] = jnp.concatenate(cpw[1:], axis=0).astype(wout_ref.dtype)
    for i, kk in enumerate(_s5_scan_steps(n_top)):
        pr, pi = power(kk)
        ap_ref[i:i + 1, :] = jnp.concatenate([pr, pi], axis=1)


def _s5_coefficients(log_dt, lam_re, lam_im, b_re, b_im, c_re, c_im, d_skip, n_top):
    g, p, h = SSM_GROUPS, SSM_STATE, SSM_GROUP
    n_ap = SUPER + n_top
    gb = SCAN_GROUPS
    grp = lambda *shape: pl.BlockSpec((gb,) + shape, lambda i: (i,) + (0,) * len(shape))
    exps = np.asarray(_s5_exponents(n_top), np.float32)
    kk = np.zeros((-(-exps.size // 8) * 8, 1), np.float32)
    kk[:exps.size, 0] = exps
    return pl.pallas_call(
        functools.partial(_s5_coef_kernel, n_top=n_top),
        out_shape=(jax.ShapeDtypeStruct((g, CHUNK * h + 2 * p, CHUNK * h), BF16),
                   jax.ShapeDtypeStruct((g, CHUNK * h, 2 * p), BF16),
                   jax.ShapeDtypeStruct((g, n_ap, 2 * p), F32)),
        grid=(g // gb,),
        in_specs=[pl.BlockSpec(kk.shape, lambda i: (0, 0)),
                  grp(1, p), grp(1, p), grp(1, 1), grp(p, h), grp(p, h), grp(h, p), grp(h, p), grp(1, h)],
        out_specs=(grp(CHUNK * h + 2 * p, CHUNK * h), grp(CHUNK * h, 2 * p), grp(n_ap, 2 * p)),
        name="s5_coefficients",
    )(jnp.asarray(kk), lam_re.reshape(g, 1, p), lam_im.reshape(g, 1, p), log_dt.reshape(g, 1, 1), b_re, b_im,
      c_re, c_im, d_skip.reshape(g, 1, h))


PHASES = 8


def _ssm_proj_kernel(h_ref, w_ref, u_ref, z_ref, rows_scr, *, n_b):
    for jj in range(PHASES):
        for b in range(n_b):
            rows_scr[b] = h_ref[b, :, jj, :]
        h = jnp.concatenate([rows_scr[b].astype(BF16) for b in range(n_b)], axis=0)
        r = lax.dot_general(w_ref[...], h, _NT, preferred_element_type=F32)
        u_ref[jj] = r[:SSM_WIDTH].astype(BF16)
        z_ref[jj] = _silu(r[SSM_WIDTH:]).astype(BF16)


def _ssm_projection(h4, w_in):
    n_b, c2, _, d = h4.shape
    lanes = n_b * c2
    halves = CHUNK // PHASES
    w_t = w_in[:, :2 * SSM_WIDTH].T.astype(BF16)
    out = jax.ShapeDtypeStruct((SUPER, CHUNK, SSM_WIDTH, lanes), BF16)
    out_spec = pl.BlockSpec((None, PHASES, SSM_WIDTH, lanes), lambda j2, hf: (j2, hf, 0, 0))
    return pl.pallas_call(
        functools.partial(_ssm_proj_kernel, n_b=n_b),
        out_shape=(out, out),
        grid=(SUPER, halves),
        in_specs=[pl.BlockSpec((n_b, c2, PHASES, d), lambda j2, hf: (0, 0, j2 * halves + hf, 0)),
                  pl.BlockSpec(w_t.shape, lambda j2, hf: (0, 0))],
        out_specs=(out_spec, out_spec),
        scratch_shapes=[pltpu.VMEM((n_b, c2, d), F32)],
        compiler_params=pltpu.CompilerParams(dimension_semantics=("arbitrary", "arbitrary"),
                                             vmem_limit_bytes=VMEM_LIMIT),
        name="ssm_projection",
    )(h4, w_t)


def _cmul(ar, ai, xr, xi):
    return ar * xr - ai * xi, ar * xi + ai * xr


SCAN_GROUPS = 8


def _s5_group_kernel(x_ref, lhs_ref, wout_ref, ap_ref, y_ref, *, n_b, c2, n_top):
    p, gb = SSM_STATE, SCAN_GROUPS
    rows = CHUNK * SSM_GROUP
    n = SUPER * n_b * c2
    lanes = n_b * c2

    def x_group(gi):
        return jnp.concatenate([x_ref[j2, :, SSM_GROUP * gi:SSM_GROUP * (gi + 1), :].reshape(rows, lanes)
                                for j2 in range(SUPER)], axis=1)

    r = [jnp.dot(lhs_ref[gi], x_group(gi), preferred_element_type=F32) for gi in range(gb)]
    c_re = jnp.concatenate([r[gi][rows:rows + p] for gi in range(gb)], axis=0)
    c_im = jnp.concatenate([r[gi][rows + p:] for gi in range(gb)], axis=0)

    def mult(i):
        rep = [jnp.broadcast_to(ap_ref[gi, i:i + 1, :], (2 * p, 2 * p)).T for gi in range(gb)]
        return (jnp.concatenate([m[:p, :c2] for m in rep], axis=0),
                jnp.concatenate([m[p:, :c2] for m in rep], axis=0))

    lane = lax.broadcasted_iota(jnp.int32, (gb * p, c2), 1)

    def shifted(a, sh):
        return jnp.where(lane >= sh, pltpu.roll(a, sh, 1), 0.0)

    pieces = [[None] * n_b for _ in range(SUPER)]
    for b in range(n_b):
        piece = lambda a, j2: a[:, (j2 * n_b + b) * c2:(j2 * n_b + b + 1) * c2]
        a_re, a_im = mult(1)
        e_re = jnp.zeros((gb * p, c2), F32)
        e_im = jnp.zeros((gb * p, c2), F32)
        local = []
        for j2 in range(SUPER):
            local.append((e_re, e_im))
            t_re, t_im = _cmul(a_re, a_im, e_re, e_im)
            e_re, e_im = t_re + piece(c_re, j2), t_im + piece(c_im, j2)
        for i in range(n_top):
            m_re, m_im = mult(SUPER + i)
            t_re, t_im = _cmul(m_re, m_im, shifted(e_re, 1 << i), shifted(e_im, 1 << i))
            e_re, e_im = e_re + t_re, e_im + t_im
        s_re, s_im = shifted(e_re, 1), shifted(e_im, 1)
        for j2 in range(SUPER):
            if j2 == 0:
                pieces[j2][b] = (s_re, s_im)
            else:
                m_re, m_im = mult(j2)
                t_re, t_im = _cmul(m_re, m_im, s_re, s_im)
                pieces[j2][b] = (local[j2][0] + t_re, local[j2][1] + t_im)
    for gi in range(gb):
        sl = slice(gi * p, (gi + 1) * p)
        s_in = jnp.concatenate([jnp.concatenate([pieces[j2][b][0][sl], pieces[j2][b][1][sl]], axis=0)
                                for j2 in range(SUPER) for b in range(n_b)], axis=1)
        y = r[gi][:rows] + jnp.dot(wout_ref[gi], s_in.astype(BF16), preferred_element_type=F32)
        for j2 in range(SUPER):
            y_ref[j2, gi] = y[:, j2 * lanes:(j2 + 1) * lanes].reshape(CHUNK, SSM_GROUP, lanes).astype(y_ref.dtype)


def _s5_scan(u_t, lhs, wout, ap, n_b, c2, n_top):
    g, h = SSM_GROUPS, SSM_GROUP
    lanes = u_t.shape[-1]
    gb = SCAN_GROUPS
    grp = lambda *shape: pl.BlockSpec((gb,) + shape, lambda i: (i,) + (0,) * len(shape))
    return pl.pallas_call(
        functools.partial(_s5_group_kernel, n_b=n_b, c2=c2, n_top=n_top),
        out_shape=jax.ShapeDtypeStruct((SUPER, g, CHUNK, h, lanes), BF16),
        grid=(g // gb,),
        in_specs=[pl.BlockSpec((SUPER, CHUNK, gb * h, lanes), lambda i: (0, 0, i, 0)),
                  grp(*lhs.shape[1:]), grp(*wout.shape[1:]), grp(*ap.shape[1:])],
        out_specs=pl.BlockSpec((SUPER, gb, CHUNK, h, lanes), lambda i: (0, i, 0, 0, 0)),
        compiler_params=pltpu.CompilerParams(dimension_semantics=("arbitrary",),
                                             vmem_limit_bytes=VMEM_LIMIT),
        name="s5_chunk_scan",
    )(u_t, lhs, wout, ap)


def _glu_kernel(y_ref, zs_ref, w_ref, b_ref, o_ref, *, n_b, c2):
    g, _, h, n = y_ref.shape
    for jj in range(PHASES):
        y = jax.nn.gelu(y_ref[:, jj].reshape(g * h, n).astype(F32))
        t = jnp.dot(w_ref[...], y.astype(BF16), preferred_element_type=F32) + b_ref[...]
        gated = (y * jax.nn.sigmoid(t) * zs_ref[jj].astype(F32)).T
        for b in range(n_b):
            o_ref[b, :, jj, :] = gated[b * c2:(b + 1) * c2]


def _glu(y_t, zs_t, w_glu, b_glu, n_b, c2):
    _, g, _, h, _ = y_t.shape
    lanes = n_b * c2
    halves = CHUNK // PHASES
    w_t = w_glu.T.astype(BF16)
    return pl.pallas_call(
        functools.partial(_glu_kernel, n_b=n_b, c2=c2),
        out_shape=jax.ShapeDtypeStruct((n_b, c2, SUPER * CHUNK, SSM_WIDTH), F32),
        grid=(SUPER, halves),
        in_specs=[pl.BlockSpec((None, g, PHASES, h, lanes), lambda j2, hf: (j2, 0, hf, 0, 0)),
                  pl.BlockSpec((None, PHASES, SSM_WIDTH, lanes), lambda j2, hf: (j2, hf, 0, 0)),
                  pl.BlockSpec(w_t.shape, lambda j2, hf: (0, 0)),
                  pl.BlockSpec((SSM_WIDTH, 1), lambda j2, hf: (0, 0))],
        out_specs=pl.BlockSpec((n_b, c2, PHASES, SSM_WIDTH), lambda j2, hf: (0, 0, j2 * halves + hf, 0)),
        compiler_params=pltpu.CompilerParams(dimension_semantics=("arbitrary", "arbitrary"),
                                             vmem_limit_bytes=VMEM_LIMIT),
        name="s5_glu",
    )(y_t, zs_t, w_t, b_glu.reshape(SSM_WIDTH, 1))


def _out_proj_kernel(ys_ref, ym_ref, x_ref, mod_ref, wa_ref, wb_ref, o_ref):
    y = (jnp.dot(ys_ref[...].astype(BF16), wa_ref[...], preferred_element_type=F32)
         + lax.dot_general(ym_ref[...], wb_ref[...], _TN, preferred_element_type=F32))
    gate = mod_ref[...][:, 2 * D_MODEL:]
    o_ref[...] = x_ref[...] + gate * y


def _out_projection(ys, ym, x2, mod3, w_out, seq):
    t, d = x2.shape
    tm = min(2048, seq)
    per_b = seq // tm
    wa = w_out[:SSM_WIDTH].astype(BF16)
    wb = w_out[SSM_WIDTH:].astype(BF16)
    tok = lambda w: pl.BlockSpec((tm, w), lambda i: (i, 0))
    return pl.pallas_call(
        _out_proj_kernel,
        out_shape=jax.ShapeDtypeStruct(x2.shape, F32),
        grid=(t // tm,),
        in_specs=[tok(SSM_WIDTH), pl.BlockSpec((MLA_WIDTH, tm), lambda i: (0, i)), tok(d),
                  pl.BlockSpec((None, 1, 3 * d), lambda i: (i // per_b, 0, 0)),
                  pl.BlockSpec(wa.shape, lambda i: (0, 0)),
                  pl.BlockSpec(wb.shape, lambda i: (0, 0))],
        out_specs=tok(d),
        compiler_params=pltpu.CompilerParams(dimension_semantics=("arbitrary",),
                                             vmem_limit_bytes=VMEM_LIMIT),
        name="output_projection",
    )(ys, ym, x2, mod3, wa, wb)


def kernel(x, c, positions, w_ada, b_ada, norm_g, w_in, log_dt, lam_re, lam_im, b_re, b_im, c_re, c_im, d_skip,
           w_glu, b_glu, q_a_g, w_q_b, kv_a_g, w_kv_b, q_norm_g, k_norm_g, w_out):
    n_b, seq, d = x.shape
    depth = w_ada.shape[0]
    c2 = seq // (CHUNK * SUPER)
    n_top = max(int(math.log2(c2)), 0)
    assert c2 * CHUNK * SUPER == seq and (1 << n_top) == c2
    for l in range(depth):
        mod3 = _modulation(c, w_ada[l], b_ada[l]).reshape(n_b, 1, 3 * d)
        x2 = x.reshape(n_b * seq, d)
        lhs, wout, ap = _s5_coefficients(log_dt[l], lam_re[l], lam_im[l], b_re[l], b_im[l], c_re[l], c_im[l],
                                         d_skip[l], n_top)
        qt, k, vt, zm, h = _mla_projection(x2, mod3, norm_g[l], w_in[l], q_a_g[l], w_q_b[l],
                                           kv_a_g[l], w_kv_b[l], q_norm_g[l], k_norm_g[l], positions, n_b, seq)
        score_bound = (1.05 * math.sqrt(QK_HEAD) * math.log2(math.e)
                       * jnp.max(jnp.abs(q_norm_g[l])) * jnp.max(jnp.abs(k_norm_g[l])))
        ym = _attention(qt, k, vt, zm, score_bound)
        ym, h = lax.optimization_barrier((ym, h))
        u_t, zs_t = _ssm_projection(h.reshape(n_b, c2, SUPER * CHUNK, d), w_in[l])
        y_t = _s5_scan(u_t, lhs, wout, ap, n_b, c2, n_top)
        ys = _glu(y_t, zs_t, w_glu[l], b_glu[l], n_b, c2).reshape(n_b * seq, SSM_WIDTH)
        x = _out_projection(ys, ym, x2, mod3, w_out[l], seq).reshape(n_b, seq, d)
    return x
```

```python
import functools
import math

import numpy as np
import jax
import jax.numpy as jnp
from jax import lax
from jax.experimental import pallas as pl
from jax.experimental.pallas import tpu as pltpu

F32 = jnp.float32
BF16 = jnp.bfloat16
HIGHEST = lax.Precision.HIGHEST

D_MODEL = 1024
SSM_WIDTH = 512
SSM_GROUP = 16
SSM_GROUPS = 32
SSM_STATE = 64
MLA_HEADS = 8
QK_NOPE = 64
QK_ROPE = 32
QK_HEAD = QK_NOPE + QK_ROPE
V_HEAD = 64
MLA_WIDTH = MLA_HEADS * V_HEAD
Q_LORA = 384
KV_LORA = 256
ROPE_THETA = 10000.0
EPS = 1e-6
NEG_BIG = -1e30

LANES = 128
VMEM_LIMIT = 60 * 1024 * 1024
CHUNK = 16
SUPER = 4
ROPE_HALF = QK_ROPE // 2

_NT = (((1,), (1,)), ((), ()))
_TN = (((0,), (0,)), ((), ()))


def _silu(v):
    return v * jax.nn.sigmoid(v)


def _rms(v, gain):
    return v * lax.rsqrt(jnp.mean(v * v, axis=-1, keepdims=True) + EPS) * gain


def _mod_kernel(c_ref, w_ref, b_ref, o_ref):
    n_b, d = c_ref.shape
    tn = w_ref.shape[1]
    act_t = _silu(c_ref[...]).T
    out = []
    for b in range(n_b):
        col = jnp.broadcast_to(act_t[:, b:b + 1], (d, LANES))
        out.append(jnp.concatenate([jnp.sum(w_ref[:, j:j + LANES] * col, axis=0, keepdims=True)
                                    for j in range(0, tn, LANES)], axis=1))
    o_ref[...] = jnp.concatenate(out, axis=0) + b_ref[...]


def _modulation(c, w, b):
    n_b, d = c.shape
    tn = 512
    return pl.pallas_call(
        _mod_kernel,
        out_shape=jax.ShapeDtypeStruct((n_b, 3 * d), F32),
        grid=(3 * d // tn,),
        in_specs=[pl.BlockSpec((n_b, d), lambda n: (0, 0)),
                  pl.BlockSpec((d, tn), lambda n: (0, n)),
                  pl.BlockSpec((1, tn), lambda n: (0, n))],
        out_specs=pl.BlockSpec((n_b, tn), lambda n: (0, n)),
        name="adaln_modulation",
    )(c, w, b.reshape(1, -1))


def _rope_frequencies():
    inv_freq = ROPE_THETA ** (-np.arange(ROPE_HALF, dtype=np.float64) * 2.0 / QK_ROPE)
    return jnp.asarray(inv_freq.astype(np.float32)).reshape(ROPE_HALF, 1)


def _slab_source():
    src = np.full((LANES,), -1, np.int64)
    src[0:16] = QK_NOPE + np.arange(16)
    src[16:64] = np.arange(48)
    src[64:80] = QK_NOPE + 16 + np.arange(16)
    src[80:96] = 48 + np.arange(16)
    return src


def _to_slabs(w, per_head, src):
    k = w.shape[0]
    w3 = w.reshape(k, MLA_HEADS, per_head)
    valid = src >= 0
    cols = jnp.where(valid[None, None, :], w3[:, :, np.clip(src, 0, per_head - 1)], 0.0)
    return cols.reshape(k, MLA_HEADS * LANES)


V_ROWS = V_HEAD + 16


def _mla_proj_kernel(x_ref, mod_ref, ng_ref, w1_ref, wzt_ref, qag_ref, kvag_ref, wqt_ref, wkt_ref, wvt_ref,
                     gq_ref, gk_ref, pos_ref, freq_ref, qt_ref, k_ref, vt_ref, zs_ref, h_ref):
    d = D_MODEL
    tm = x_ref.shape[0]
    x = x_ref[...]
    mod = mod_ref[...]
    shift, scale = mod[:, :d], mod[:, d:2 * d]
    h_ref[...] = _rms(x, ng_ref[...] * (1.0 + scale)) + shift
    h = h_ref[...].astype(BF16)
    p1 = jnp.dot(h, w1_ref[...], preferred_element_type=F32)
    o1, o2 = Q_LORA, Q_LORA + KV_LORA
    cqn = _rms(p1[:, :o1], qag_ref[...]).astype(BF16)
    ckvn = _rms(p1[:, o1:o2], kvag_ref[...]).astype(BF16)
    zs_ref[...] = _silu(lax.dot_general(wzt_ref[...], h, _NT, preferred_element_type=F32)).astype(BF16)

    qt = lax.dot_general(wqt_ref[...], cqn, _NT, preferred_element_type=F32)
    kt = lax.dot_general(wkt_ref[...], ckvn, _NT, preferred_element_type=F32)
    krt = p1[:, o2:].T
    vt = lax.dot_general(wvt_ref[...], ckvn, _NT, preferred_element_type=F32)
    ang = freq_ref[...] * pos_ref[...].astype(F32)
    cos_t, sin_t = jnp.cos(ang), jnp.sin(ang)
    gq = jnp.concatenate([gq_ref[...]] * (tm // LANES), axis=1)
    gk = jnp.concatenate([gk_ref[...]] * (tm // LANES), axis=1)
    first = lax.broadcasted_iota(jnp.int32, (V_ROWS - V_HEAD, tm), 0) == 0
    ones_rows = jnp.where(first, 1.0, 0.0).astype(BF16)
    r, half = ROPE_HALF, LANES // 2

    def norm_rope(slab, gain):
        ss = jnp.sum(slab * slab, axis=0, keepdims=True)
        n = slab * lax.rsqrt(ss * (1.0 / QK_HEAD) + EPS) * gain
        x1, x2 = n[0:r], n[half:half + r]
        return jnp.concatenate([x1 * cos_t - x2 * sin_t, n[r:half], x2 * cos_t + x1 * sin_t, n[half + r:]], axis=0)

    for head in range(MLA_HEADS):
        rows = slice(LANES * head, LANES * (head + 1))
        qt_ref[head] = norm_rope(qt[rows], gq).astype(BF16)
        k_ref[head] = norm_rope(kt[rows] + krt, gk).T.astype(BF16)
        vt_ref[head, 0:V_HEAD, :] = vt[V_HEAD * head:V_HEAD * (head + 1)].astype(BF16)
        vt_ref[head, V_HEAD:, :] = ones_rows


def _mla_projection(x2, mod3, norm_g, w_in, q_a_g, w_q_b, kv_a_g, w_kv_b, q_norm_g, k_norm_g, positions, n_b, seq):
    t, d = x2.shape
    tm = min(512, seq)
    per_b = seq // tm
    o2 = 2 * SSM_WIDTH
    o4, o5 = o2 + Q_LORA + KV_LORA, o2 + Q_LORA + KV_LORA + QK_ROPE
    src = _slab_source()
    valid = src >= 0
    rope_src = np.where(src >= QK_NOPE, src - QK_NOPE, -1)
    nope_src = np.where(valid & (src < QK_NOPE), src, -1)
    wkr = jnp.where((rope_src >= 0)[None, :], w_in[:, o4:o5][:, np.clip(rope_src, 0, QK_ROPE - 1)], 0.0)
    w1 = jnp.concatenate([w_in[:, o2:o4], wkr], axis=1).astype(BF16)
    wzt = w_in[:, o5:].T.astype(BF16)
    wqt = _to_slabs(w_q_b, QK_HEAD, src).T.astype(BF16)
    wkt = _to_slabs(w_kv_b, QK_NOPE + V_HEAD, nope_src).T.astype(BF16)
    wvt = w_kv_b.reshape(KV_LORA, MLA_HEADS, QK_NOPE + V_HEAD)[:, :, QK_NOPE:].reshape(KV_LORA, MLA_WIDTH).T.astype(BF16)
    q_scale = math.log2(math.e) / math.sqrt(QK_HEAD)
    slab_gain = lambda g: jnp.broadcast_to(jnp.where(valid, g[np.clip(src, 0, QK_HEAD - 1)], 0.0)[:, None],
                                           (LANES, LANES))
    gq = slab_gain(q_norm_g * q_scale)
    gk = slab_gain(k_norm_g)
    const = lambda shape: pl.BlockSpec(shape, lambda i: (0,) * len(shape))
    tok = lambda i: (i // per_b, 0, 0, i % per_b)
    return pl.pallas_call(
        _mla_proj_kernel,
        out_shape=(jax.ShapeDtypeStruct((n_b, MLA_HEADS, LANES, seq), BF16),
                   jax.ShapeDtypeStruct((n_b, MLA_HEADS, seq, LANES), BF16),
                   jax.ShapeDtypeStruct((n_b, MLA_HEADS, V_ROWS, seq), BF16),
                   jax.ShapeDtypeStruct((MLA_WIDTH, t), BF16),
                   jax.ShapeDtypeStruct((t, d), F32)),
        grid=(t // tm,),
        in_specs=[pl.BlockSpec((tm, d), lambda i: (i, 0)),
                  pl.BlockSpec((None, 1, 3 * d), lambda i: (i // per_b, 0, 0)),
                  const((1, d)), const(w1.shape), const(wzt.shape), const((1, Q_LORA)), const((1, KV_LORA)),
                  const(wqt.shape), const(wkt.shape), const(wvt.shape),
                  const((LANES, LANES)), const((LANES, LANES)),
                  pl.BlockSpec((1, tm), lambda i: (0, i)),
                  const((ROPE_HALF, 1))],
        out_specs=(pl.BlockSpec((None, MLA_HEADS, LANES, tm), tok),
                   pl.BlockSpec((None, MLA_HEADS, tm, LANES), lambda i: (i // per_b, 0, i % per_b, 0)),
                   pl.BlockSpec((None, MLA_HEADS, V_ROWS, tm), tok),
                   pl.BlockSpec((MLA_WIDTH, tm), lambda i: (0, i)),
                   pl.BlockSpec((tm, d), lambda i: (i, 0))),
        compiler_params=pltpu.CompilerParams(dimension_semantics=("arbitrary",),
                                             vmem_limit_bytes=VMEM_LIMIT),
        name="mla_projection",
    )(x2, mod3, norm_g.reshape(1, d), w1, wzt, q_a_g.reshape(1, -1), kv_a_g.reshape(1, -1), wqt, wkt, wvt, gq, gk,
      positions.reshape(1, t), _rope_frequencies())


ATTN_TQ = 2048
ATTN_TK = 256
HEADS_PER_STEP = 2


def _attn_kernel(qt_ref, k_ref, vt_ref, zs_ref, o_ref, s_scr):
    tq, tk = ATTN_TQ, ATTN_TK
    r = tq // tk
    assert r * tk == tq and r % 2 == 0
    qi = pl.program_id(2)

    def scores(slot, t, lo=0):
        start = pl.multiple_of(t * tk, tk)
        for e in range(HEADS_PER_STEP):
            s_scr[slot, e, :, lo:] = jnp.dot(k_ref[e, pl.ds(start, tk), :], qt_ref[e, :, lo:],
                                             preferred_element_type=F32)

    def process(slot, t, carry, lo=0, diagonal=False):
        start = pl.multiple_of(t * tk, tk)
        out = []
        for e in range(HEADS_PER_STEP):
            m, acc = carry[e]
            s = s_scr[slot, e, :, lo:]
            if diagonal:
                blk = s[:, :tk]
                ok = lax.broadcasted_iota(jnp.int32, blk.shape, 0) <= lax.broadcasted_iota(jnp.int32, blk.shape, 1)
                blk = jnp.where(ok, blk, NEG_BIG)
                s = jnp.concatenate([blk, s[:, tk:]], axis=1) if s.shape[1] > tk else blk
            m_old = m[:, lo:]
            m_new = jnp.maximum(m_old, jnp.max(s, axis=0, keepdims=True))
            p = jnp.exp2(s - m_new).astype(BF16)
            acc_new = (jnp.exp2(m_old - m_new) * acc[:, lo:]
                       + jnp.dot(vt_ref[e, :, pl.ds(start, tk)], p, preferred_element_type=F32))
            if lo:
                m_new = jnp.concatenate([m[:, :lo], m_new], axis=1)
                acc_new = jnp.concatenate([acc[:, :lo], acc_new], axis=1)
            out.append((m_new, acc_new))
        return tuple(out)

    def body(i, carry):
        base = i * r
        for u in range(r):
            scores((u + 1) % 2, base + u + 1)
            carry = process(u % 2, base + u, carry)
        return carry

    init = tuple((jnp.full((1, tq), NEG_BIG, F32), jnp.zeros((V_ROWS, tq), F32)) for _ in range(HEADS_PER_STEP))
    scores(0, 0)
    carry = lax.fori_loop(0, qi, body, init)
    base = qi * r
    for u in range(r):
        if u + 1 < r:
            scores((u + 1) % 2, base + u + 1, lo=(u + 1) * tk)
        carry = process(u % 2, base + u, carry, lo=u * tk, diagonal=True)
    _attn_finish(tuple(c[1] for c in carry), zs_ref, o_ref)


def _attn_finish(accs, zs_ref, o_ref):
    outs = [acc[:V_HEAD] / acc[V_HEAD:V_HEAD + 1] for acc in accs]
    o_ref[...] = (jnp.concatenate(outs, axis=0) * zs_ref[...].astype(F32)).astype(o_ref.dtype)


def _attn_bounded_kernel(qt_ref, k_ref, vt_ref, zs_ref, o_ref, p_scr):
    tq, tk = ATTN_TQ, ATTN_TK
    r = tq // tk
    assert r * tk == tq and r % 2 == 0
    qi = pl.program_id(2)
    q0 = qi * tq

    def probs(slot, t, lo=0, may_cross=False):
        start = pl.multiple_of(t * tk, tk)
        for e in range(HEADS_PER_STEP):
            p = jnp.exp2(jnp.dot(k_ref[e, pl.ds(start, tk), :], qt_ref[e, :, lo:], preferred_element_type=F32))
            if may_cross:
                blk = p[:, :tk]
                key = start + lax.broadcasted_iota(jnp.int32, blk.shape, 0)
                qry = q0 + lo + lax.broadcasted_iota(jnp.int32, blk.shape, 1)
                blk = jnp.where(key <= qry, blk, 0.0)
                p = jnp.concatenate([blk, p[:, tk:]], axis=1) if p.shape[1] > tk else blk
            p_scr[slot, e, :, lo:] = p.astype(BF16)

    def accumulate(slot, t, accs, lo=0):
        start = pl.multiple_of(t * tk, tk)
        out = []
        for e in range(HEADS_PER_STEP):
            pv = jnp.dot(vt_ref[e, :, pl.ds(start, tk)], p_scr[slot, e, :, lo:], preferred_element_type=F32)
            acc = accs[e]
            out.append(acc + pv if lo == 0 else jnp.concatenate([acc[:, :lo], acc[:, lo:] + pv], axis=1))
        return tuple(out)

    def body(i, accs):
        base = i * r
        for u in range(r):
            probs((u + 1) % 2, base + u + 1, may_cross=(u == r - 1))
            accs = accumulate(u % 2, base + u, accs)
        return accs

    probs(0, 0, may_cross=True)
    accs = lax.fori_loop(0, qi, body, tuple(jnp.zeros((V_ROWS, tq), F32) for _ in range(HEADS_PER_STEP)))
    base = qi * r
    for u in range(r):
        if u + 1 < r:
            probs((u + 1) % 2, base + u + 1, lo=(u + 1) * tk, may_cross=True)
        accs = accumulate(u % 2, base + u, accs, lo=u * tk)
    _attn_finish(accs, zs_ref, o_ref)


SCORE_BOUND_LOG2 = 60.0


def _attention(qt, k, vt, zs, score_bound):
    n_b, n_h, seq, _ = k.shape
    tq = ATTN_TQ
    assert seq % tq == 0 and n_h % HEADS_PER_STEP == 0
    nq = seq // tq
    hp = HEADS_PER_STEP
    width = hp * V_HEAD

    def call(body, scratch):
        return pl.pallas_call(
            body,
            out_shape=jax.ShapeDtypeStruct(zs.shape, BF16),
            grid=(n_b, n_h // hp, nq),
            in_specs=[pl.BlockSpec((None, hp, LANES, tq), lambda b, h, i: (b, h, 0, i)),
                      pl.BlockSpec((None, hp, seq, LANES), lambda b, h, i: (b, h, 0, 0)),
                      pl.BlockSpec((None, hp, V_ROWS, seq), lambda b, h, i: (b, h, 0, 0)),
                      pl.BlockSpec((width, tq), lambda b, h, i: (h, b * nq + i))],
            out_specs=pl.BlockSpec((width, tq), lambda b, h, i: (h, b * nq + i)),
            scratch_shapes=scratch,
            compiler_params=pltpu.CompilerParams(dimension_semantics=("arbitrary", "arbitrary", "arbitrary"),
                                                 vmem_limit_bytes=VMEM_LIMIT),
            name="causal_attention",
        )(qt, k, vt, zs)

    tiles = (2, hp, ATTN_TK, tq)
    return lax.cond(score_bound <= SCORE_BOUND_LOG2,
                    lambda: call(_attn_bounded_kernel, [pltpu.VMEM(tiles, BF16)]),
                    lambda: call(_attn_kernel, [pltpu.VMEM(tiles, F32)]))


def _s5_scan_steps(n_top):
    return [CHUNK * m for m in range(SUPER)] + [CHUNK * SUPER * (1 << i) for i in range(n_top)]


def _s5_exponents(n_top):
    return sorted(set(range(CHUNK + 1)) | set(_s5_scan_steps(n_top)))


def _s5_coef_kernel(kk_ref, *refs, n_top):
    for gi in range(SCAN_GROUPS):
        _s5_coef_one_group(kk_ref, *(r.at[gi] for r in refs), n_top=n_top)


def _s5_coef_one_group(kk_ref, lre_ref, lim_ref, ldt_ref, bre_ref, bim_ref, cre_ref, cim_ref, d_ref,
                       lhs_ref, wout_ref, ap_ref, *, n_top):
    h, p, n = SSM_GROUP, SSM_STATE, CHUNK
    lre, lim = lre_ref[...], lim_ref[...]
    dt = jnp.exp(ldt_ref[...])
    exps = _s5_exponents(n_top)
    kcol = kk_ref[...]
    mag = jnp.exp(kcol * (lre * dt))
    ang = kcol * (lim * dt)
    pow_re, pow_im = mag * jnp.cos(ang), mag * jnp.sin(ang)

    def power(kk):
        i = exps.index(kk)
        return pow_re[i:i + 1], pow_im[i:i + 1]

    lb_re, lb_im = power(1)
    nr, ni = lb_re - 1.0, lb_im
    den = lre * lre + lim * lim
    f_re = (nr * lre + ni * lim) / den
    f_im = (ni * lre - nr * lim) / den
    bre, bim = bre_ref[...].T, bim_ref[...].T
    bb_re = f_re * bre - f_im * bim
    bb_im = f_re * bim + f_im * bre
    cre, cim = cre_ref[...], cim_ref[...]
    cp_re, cp_im = [], []
    for kk in range(n + 1):
        pr, pi = power(kk)
        cp_re.append(cre * pr - cim * pi)
        cp_im.append(cre * pi + cim * pr)

    cpw = [jnp.concatenate([cp_re[kk], -cp_im[kk]], axis=1) for kk in range(n + 1)]
    bb_a = jnp.concatenate([bb_re, bb_im], axis=1)
    kw = lax.dot_general(jnp.concatenate(cpw[:n], axis=0), jnp.concatenate([bb_a] * n, axis=0), _NT,
                         preferred_element_type=F32, precision=HIGHEST)
    lane = lax.broadcasted_iota(jnp.int32, (h, n * h), 1)
    row = lax.broadcasted_iota(jnp.int32, (h, n * h), 0)
    col_blk = lax.shift_right_logical(lane, 4)
    d_tiled = jnp.concatenate([d_ref[...]] * n, axis=1)
    lag = [kw[h * kk:h * (kk + 1)] for kk in range(n)]
    lag[0] = lag[0] + jnp.where((lane & (h - 1)) == row, d_tiled, 0.0)
    rows = []
    for j in range(n):
        acc = jnp.zeros((h, n * h), F32)
        for jp in range(j + 1):
            acc = jnp.where(col_blk == jp, lag[j - jp], acc)
        rows.append(acc)
    lhs_ref[0:n * h, :] = jnp.concatenate(rows, axis=0).astype(lhs_ref.dtype)

    bb_b = jnp.concatenate([-bb_im, bb_re], axis=1)
    win = []
    for j in range(n):
        pr, pi = power(n - 1 - j)
        win.append(jnp.concatenate([pr, pr], axis=1) * bb_a + jnp.concatenate([pi, pi], axis=1) * bb_b)
    lhs_ref[n * h:, :] = jnp.concatenate(win, axis=0).T.astype(lhs_ref.dtype)
    wout_ref[...] = jnp.concatenate(cpw[1:], axis=0).astype(wout_ref.dtype)
    for i, kk in enumerate(_s5_scan_steps(n_top)):
        pr, pi = power(kk)
        ap_ref[i:i + 1, :] = jnp.concatenate([pr, pi], axis=1)


def _s5_coefficients(log_dt, lam_re, lam_im, b_re, b_im, c_re, c_im, d_skip, n_top):
    g, p, h = SSM_GROUPS, SSM_STATE, SSM_GROUP
    n_ap = SUPER + n_top
    gb = SCAN_GROUPS
    grp = lambda *shape: pl.BlockSpec((gb,) + shape, lambda i: (i,) + (0,) * len(shape))
    exps = np.asarray(_s5_exponents(n_top), np.float32)
    kk = np.zeros((-(-exps.size // 8) * 8, 1), np.float32)
    kk[:exps.size, 0] = exps
    return pl.pallas_call(
        functools.partial(_s5_coef_kernel, n_top=n_top),
        out_shape=(jax.ShapeDtypeStruct((g, CHUNK * h + 2 * p, CHUNK * h), BF16),
                   jax.ShapeDtypeStruct((g, CHUNK * h, 2 * p), BF16),
                   jax.ShapeDtypeStruct((g, n_ap, 2 * p), F32)),
        grid=(g // gb,),
        in_specs=[pl.BlockSpec(kk.shape, lambda i: (0, 0)),
                  grp(1, p), grp(1, p), grp(1, 1), grp(p, h), grp(p, h), grp(h, p), grp(h, p), grp(1, h)],
        out_specs=(grp(CHUNK * h + 2 * p, CHUNK * h), grp(CHUNK * h, 2 * p), grp(n_ap, 2 * p)),
        name="s5_coefficients",
    )(jnp.asarray(kk), lam_re.reshape(g, 1, p), lam_im.reshape(g, 1, p), log_dt.reshape(g, 1, 1), b_re, b_im,
      c_re, c_im, d_skip.reshape(g, 1, h))


PHASES = 8


def _ssm_proj_kernel(h_ref, w_ref, u_ref, z_ref, rows_scr, *, n_b):
    for jj in range(PHASES):
        for b in range(n_b):
            rows_scr[b] = h_ref[b, :, jj, :]
        h = jnp.concatenate([rows_scr[b].astype(BF16) for b in range(n_b)], axis=0)
        r = lax.dot_general(w_ref[...], h, _NT, preferred_element_type=F32)
        u_ref[jj] = r[:SSM_WIDTH].astype(BF16)
        z_ref[jj] = _silu(r[SSM_WIDTH:]).astype(BF16)


def _ssm_projection(h4, w_in):
    n_b, c2, _, d = h4.shape
    lanes = n_b * c2
    halves = CHUNK // PHASES
    w_t = w_in[:, :2 * SSM_WIDTH].T.astype(BF16)
    out = jax.ShapeDtypeStruct((SUPER, CHUNK, SSM_WIDTH, lanes), BF16)
    out_spec = pl.BlockSpec((None, PHASES, SSM_WIDTH, lanes), lambda j2, hf: (j2, hf, 0, 0))
    return pl.pallas_call(
        functools.partial(_ssm_proj_kernel, n_b=n_b),
        out_shape=(out, out),
        grid=(SUPER, halves),
        in_specs=[pl.BlockSpec((n_b, c2, PHASES, d), lambda j2, hf: (0, 0, j2 * halves + hf, 0)),
                  pl.BlockSpec(w_t.shape, lambda j2, hf: (0, 0))],
        out_specs=(out_spec, out_spec),
        scratch_shapes=[pltpu.VMEM((n_b, c2, d), F32)],
        compiler_params=pltpu.CompilerParams(dimension_semantics=("arbitrary", "arbitrary"),
                                             vmem_limit_bytes=VMEM_LIMIT),
        name="ssm_projection",
    )(h4, w_t)


def _cmul(ar, ai, xr, xi):
    return ar * xr - ai * xi, ar * xi + ai * xr


SCAN_GROUPS = 8


def _s5_group_kernel(x_ref, lhs_ref, wout_ref, ap_ref, y_ref, *, n_b, c2, n_top):
    p, gb = SSM_STATE, SCAN_GROUPS
    rows = CHUNK * SSM_GROUP
    n = SUPER * n_b * c2
    lanes = n_b * c2

    def x_group(gi):
        return jnp.concatenate([x_ref[j2, :, SSM_GROUP * gi:SSM_GROUP * (gi + 1), :].reshape(rows, lanes)
                                for j2 in range(SUPER)], axis=1)

    r = [jnp.dot(lhs_ref[gi], x_group(gi), preferred_element_type=F32) for gi in range(gb)]
    c_re = jnp.concatenate([r[gi][rows:rows + p] for gi in range(gb)], axis=0)
    c_im = jnp.concatenate([r[gi][rows + p:] for gi in range(gb)], axis=0)

    def mult(i):
        rep = [jnp.broadcast_to(ap_ref[gi, i:i + 1, :], (2 * p, 2 * p)).T for gi in range(gb)]
        return (jnp.concatenate([m[:p, :c2] for m in rep], axis=0),
                jnp.concatenate([m[p:, :c2] for m in rep], axis=0))

    lane = lax.broadcasted_iota(jnp.int32, (gb * p, c2), 1)

    def shifted(a, sh):
        return jnp.where(lane >= sh, pltpu.roll(a, sh, 1), 0.0)

    pieces = [[None] * n_b for _ in range(SUPER)]
    for b in range(n_b):
        piece = lambda a, j2: a[:, (j2 * n_b + b) * c2:(j2 * n_b + b + 1) * c2]
        a_re, a_im = mult(1)
        e_re = jnp.zeros((gb * p, c2), F32)
        e_im = jnp.zeros((gb * p, c2), F32)
        local = []
        for j2 in range(SUPER):
            local.append((e_re, e_im))
            t_re, t_im = _cmul(a_re, a_im, e_re, e_im)
            e_re, e_im = t_re + piece(c_re, j2), t_im + piece(c_im, j2)
        for i in range(n_top):
            m_re, m_im = mult(SUPER + i)
            t_re, t_im = _cmul(m_re, m_im, shifted(e_re, 1 << i), shifted(e_im, 1 << i))
            e_re, e_im = e_re + t_re, e_im + t_im
        s_re, s_im = shifted(e_re, 1), shifted(e_im, 1)
        for j2 in range(SUPER):
            if j2 == 0:
                pieces[j2][b] = (s_re, s_im)
            else:
                m_re, m_im = mult(j2)
                t_re, t_im = _cmul(m_re, m_im, s_re, s_im)
                pieces[j2][b] = (local[j2][0] + t_re, local[j2][1] + t_im)
    for gi in range(gb):
        sl = slice(gi * p, (gi + 1) * p)
        s_in = jnp.concatenate([jnp.concatenate([pieces[j2][b][0][sl], pieces[j2][b][1][sl]], axis=0)
                                for j2 in range(SUPER) for b in range(n_b)], axis=1)
        y = r[gi][:rows] + jnp.dot(wout_ref[gi], s_in.astype(BF16), preferred_element_type=F32)
        for j2 in range(SUPER):
            y_ref[j2, gi] = y[:, j2 * lanes:(j2 + 1) * lanes].reshape(CHUNK, SSM_GROUP, lanes).astype(y_ref.dtype)


def _s5_scan(u_t, lhs, wout, ap, n_b, c2, n_top):
    g, h = SSM_GROUPS, SSM_GROUP
    lanes = u_t.shape[-1]
    gb = SCAN_GROUPS
    grp = lambda *shape: pl.BlockSpec((gb,) + shape, lambda i: (i,) + (0,) * len(shape))
    return pl.pallas_call(
        functools.partial(_s5_group_kernel, n_b=n_b, c2=c2, n_top=n_top),
        out_shape=jax.ShapeDtypeStruct((SUPER, g, CHUNK, h, lanes), BF16),
        grid=(g // gb,),
        in_specs=[pl.BlockSpec((SUPER, CHUNK, gb * h, lanes), lambda i: (0, 0, i, 0)),
                  grp(*lhs.shape[1:]), grp(*wout.shape[1:]), grp(*ap.shape[1:])],
        out_specs=pl.BlockSpec((SUPER, gb, CHUNK, h, lanes), lambda i: (0, i, 0, 0, 0)),
        compiler_params=pltpu.CompilerParams(dimension_semantics=("arbitrary",),
                                             vmem_limit_bytes=VMEM_LIMIT),
        name="s5_chunk_scan",
    )(u_t, lhs, wout, ap)


def _glu_kernel(y_ref, zs_ref, w_ref, b_ref, o_ref, *, n_b, c2):
    g, _, h, n = y_ref.shape
    for jj in range(PHASES):
        y = jax.nn.gelu(y_ref[:, jj].reshape(g * h, n).astype(F32))
        t = jnp.dot(w_ref[...], y.astype(BF16), preferred_element_type=F32) + b_ref[...]
        gated = (y * jax.nn.sigmoid(t) * zs_ref[jj].astype(F32)).T
        for b in range(n_b):
            o_ref[b, :, jj, :] = gated[b * c2:(b + 1) * c2]


def _glu(y_t, zs_t, w_glu, b_glu, n_b, c2):
    _, g, _, h, _ = y_t.shape
    lanes = n_b * c2
    halves = CHUNK // PHASES
    w_t = w_glu.T.astype(BF16)
    return pl.pallas_call(
        functools.partial(_glu_kernel, n_b=n_b, c2=c2),
        out_shape=jax.ShapeDtypeStruct((n_b, c2, SUPER * CHUNK, SSM_WIDTH), F32),
        grid=(SUPER, halves),
        in_specs=[pl.BlockSpec((None, g, PHASES, h, lanes), lambda j2, hf: (j2, 0, hf, 0, 0)),
                  pl.BlockSpec((None, PHASES, SSM_WIDTH, lanes), lambda j2, hf: (j2, hf, 0, 0)),
                  pl.BlockSpec(w_t.shape, lambda j2, hf: (0, 0)),
                  pl.BlockSpec((SSM_WIDTH, 1), lambda j2, hf: (0, 0))],
        out_specs=pl.BlockSpec((n_b, c2, PHASES, SSM_WIDTH), lambda j2, hf: (0, 0, j2 * halves + hf, 0)),
        compiler_params=pltpu.CompilerParams(dimension_semantics=("arbitrary", "arbitrary"),
                                             vmem_limit_bytes=VMEM_LIMIT),
        name="s5_glu",
    )(y_t, zs_t, w_t, b_glu.reshape(SSM_WIDTH, 1))


def _out_proj_kernel(ys_ref, ym_ref, x_ref, mod_ref, wa_ref, wb_ref, o_ref):
    y = (jnp.dot(ys_ref[...].astype(BF16), wa_ref[...], preferred_element_type=F32)
         + lax.dot_general(ym_ref[...], wb_ref[...], _TN, preferred_element_type=F32))
    gate = mod_ref[...][:, 2 * D_MODEL:]
    o_ref[...] = x_ref[...] + gate * y


def _out_projection(ys, ym, x2, mod3, w_out, seq):
    t, d = x2.shape
    tm = min(1024, seq)
    per_b = seq // tm
    wa = w_out[:SSM_WIDTH].astype(BF16)
    wb = w_out[SSM_WIDTH:].astype(BF16)
    tok = lambda w: pl.BlockSpec((tm, w), lambda i: (i, 0))
    return pl.pallas_call(
        _out_proj_kernel,
        out_shape=jax.ShapeDtypeStruct(x2.shape, F32),
        grid=(t // tm,),
        in_specs=[tok(SSM_WIDTH), pl.BlockSpec((MLA_WIDTH, tm), lambda i: (0, i)), tok(d),
                  pl.BlockSpec((None, 1, 3 * d), lambda i: (i // per_b, 0, 0)),
                  pl.BlockSpec(wa.shape, lambda i: (0, 0)),
                  pl.BlockSpec(wb.shape, lambda i: (0, 0))],
        out_specs=tok(d),
        compiler_params=pltpu.CompilerParams(dimension_semantics=("arbitrary",),
                                             vmem_limit_bytes=VMEM_LIMIT),
        name="output_projection",
    )(ys, ym, x2, mod3, wa, wb)


def kernel(x, c, positions, w_ada, b_ada, norm_g, w_in, log_dt, lam_re, lam_im, b_re, b_im, c_re, c_im, d_skip,
           w_glu, b_glu, q_a_g, w_q_b, kv_a_g, w_kv_b, q_norm_g, k_norm_g, w_out):
    n_b, seq, d = x.shape
    depth = w_ada.shape[0]
    c2 = seq // (CHUNK * SUPER)
    n_top = max(int(math.log2(c2)), 0)
    assert c2 * CHUNK * SUPER == seq and (1 << n_top) == c2
    for l in range(depth):
        mod3 = _modulation(c, w_ada[l], b_ada[l]).reshape(n_b, 1, 3 * d)
        x2 = x.reshape(n_b * seq, d)
        lhs, wout, ap = _s5_coefficients(log_dt[l], lam_re[l], lam_im[l], b_re[l], b_im[l], c_re[l], c_im[l],
                                         d_skip[l], n_top)
        qt, k, vt, zm, h = _mla_projection(x2, mod3, norm_g[l], w_in[l], q_a_g[l], w_q_b[l],
                                           kv_a_g[l], w_kv_b[l], q_norm_g[l], k_norm_g[l], positions, n_b, seq)
        score_bound = (1.05 * math.sqrt(QK_HEAD) * math.log2(math.e)
                       * jnp.max(jnp.abs(q_norm_g[l])) * jnp.max(jnp.abs(k_norm_g[l])))
        ym = _attention(qt, k, vt, zm, score_bound)
        ym, h = lax.optimization_barrier((ym, h))
        u_t, zs_t = _ssm_projection(h.reshape(n_b, c2, SUPER * CHUNK, d), w_in[l])
        y_t = _s5_scan(u_t, lhs, wout, ap, n_b, c2, n_top)
        ys = _glu(y_t, zs_t, w_glu[l], b_glu[l], n_b, c2).reshape(n_b * seq, SSM_WIDTH)
        x = _out_projection(ys, ym, x2, mod3, w_out[l], seq).reshape(n_b, seq, d)
    return x
```

```python
import functools
import math

import numpy as np
import jax
import jax.numpy as jnp
from jax import lax
from jax.experimental import pallas as pl
from jax.experimental.pallas import tpu as pltpu

F32 = jnp.float32
BF16 = jnp.bfloat16
HIGHEST = lax.Precision.HIGHEST

D_MODEL = 1024
SSM_WIDTH = 512
SSM_GROUP = 16
SSM_GROUPS = 32
SSM_STATE = 64
MLA_HEADS = 8
QK_NOPE = 64
QK_ROPE = 32
QK_HEAD = QK_NOPE + QK_ROPE
V_HEAD = 64
MLA_WIDTH = MLA_HEADS * V_HEAD
Q_LORA = 384
KV_LORA = 256
ROPE_THETA = 10000.0
EPS = 1e-6
NEG_BIG = -1e30

LANES = 128
VMEM_LIMIT = 60 * 1024 * 1024
CHUNK = 16
SUPER = 4
ROPE_HALF = QK_ROPE // 2

_NT = (((1,), (1,)), ((), ()))
_TN = (((0,), (0,)), ((), ()))


def _silu(v):
    return v * jax.nn.sigmoid(v)


def _rms(v, gain):
    return v * lax.rsqrt(jnp.mean(v * v, axis=-1, keepdims=True) + EPS) * gain


def _mod_kernel(c_ref, w_ref, b_ref, o_ref):
    n_b, d = c_ref.shape
    tn = w_ref.shape[1]
    act_t = _silu(c_ref[...]).T
    out = []
    for b in range(n_b):
        col = jnp.broadcast_to(act_t[:, b:b + 1], (d, LANES))
        out.append(jnp.concatenate([jnp.sum(w_ref[:, j:j + LANES] * col, axis=0, keepdims=True)
                                    for j in range(0, tn, LANES)], axis=1))
    o_ref[...] = jnp.concatenate(out, axis=0) + b_ref[...]


def _modulation(c, w, b):
    n_b, d = c.shape
    tn = 512
    return pl.pallas_call(
        _mod_kernel,
        out_shape=jax.ShapeDtypeStruct((n_b, 3 * d), F32),
        grid=(3 * d // tn,),
        in_specs=[pl.BlockSpec((n_b, d), lambda n: (0, 0)),
                  pl.BlockSpec((d, tn), lambda n: (0, n)),
                  pl.BlockSpec((1, tn), lambda n: (0, n))],
        out_specs=pl.BlockSpec((n_b, tn), lambda n: (0, n)),
        name="adaln_modulation",
    )(c, w, b.reshape(1, -1))


def _rope_frequencies():
    inv_freq = ROPE_THETA ** (-np.arange(ROPE_HALF, dtype=np.float64) * 2.0 / QK_ROPE)
    return jnp.asarray(inv_freq.astype(np.float32)).reshape(ROPE_HALF, 1)


def _slab_source():
    src = np.full((LANES,), -1, np.int64)
    src[0:16] = QK_NOPE + np.arange(16)
    src[16:64] = np.arange(48)
    src[64:80] = QK_NOPE + 16 + np.arange(16)
    src[80:96] = 48 + np.arange(16)
    return src


def _to_slabs(w, per_head, src):
    k = w.shape[0]
    w3 = w.reshape(k, MLA_HEADS, per_head)
    valid = src >= 0
    cols = jnp.where(valid[None, None, :], w3[:, :, np.clip(src, 0, per_head - 1)], 0.0)
    return cols.reshape(k, MLA_HEADS * LANES)


V_ROWS = V_HEAD + 16


def _mla_proj_kernel(x_ref, mod_ref, ng_ref, w1_ref, wzt_ref, qag_ref, kvag_ref, wqt_ref, wkt_ref, wvt_ref,
                     gq_ref, gk_ref, pos_ref, freq_ref, qt_ref, k_ref, vt_ref, zs_ref, h_ref):
    d = D_MODEL
    tm = x_ref.shape[0]
    x = x_ref[...]
    mod = mod_ref[...]
    shift, scale = mod[:, :d], mod[:, d:2 * d]
    h_ref[...] = _rms(x, ng_ref[...] * (1.0 + scale)) + shift
    h = h_ref[...].astype(BF16)
    p1 = jnp.dot(h, w1_ref[...], preferred_element_type=F32)
    o1, o2 = Q_LORA, Q_LORA + KV_LORA
    cqn = _rms(p1[:, :o1], qag_ref[...]).astype(BF16)
    ckvn = _rms(p1[:, o1:o2], kvag_ref[...]).astype(BF16)
    zs_ref[...] = _silu(lax.dot_general(wzt_ref[...], h, _NT, preferred_element_type=F32)).astype(BF16)

    qt = lax.dot_general(wqt_ref[...], cqn, _NT, preferred_element_type=F32)
    kt = lax.dot_general(wkt_ref[...], ckvn, _NT, preferred_element_type=F32)
    krt = p1[:, o2:].T
    vt = lax.dot_general(wvt_ref[...], ckvn, _NT, preferred_element_type=F32)
    ang = freq_ref[...] * pos_ref[...].astype(F32)
    cos_t, sin_t = jnp.cos(ang), jnp.sin(ang)
    gq = jnp.concatenate([gq_ref[...]] * (tm // LANES), axis=1)
    gk = jnp.concatenate([gk_ref[...]] * (tm // LANES), axis=1)
    first = lax.broadcasted_iota(jnp.int32, (V_ROWS - V_HEAD, tm), 0) == 0
    ones_rows = jnp.where(first, 1.0, 0.0).astype(BF16)
    r, half = ROPE_HALF, LANES // 2

    def norm_rope(slab, gain):
        ss = jnp.sum(slab * slab, axis=0, keepdims=True)
        n = slab * lax.rsqrt(ss * (1.0 / QK_HEAD) + EPS) * gain
        x1, x2 = n[0:r], n[half:half + r]
        return jnp.concatenate([x1 * cos_t - x2 * sin_t, n[r:half], x2 * cos_t + x1 * sin_t, n[half + r:]], axis=0)

    for head in range(MLA_HEADS):
        rows = slice(LANES * head, LANES * (head + 1))
        qt_ref[head] = norm_rope(qt[rows], gq).astype(BF16)
        k_ref[head] = norm_rope(kt[rows] + krt, gk).T.astype(BF16)
        vt_ref[head, 0:V_HEAD, :] = vt[V_HEAD * head:V_HEAD * (head + 1)].astype(BF16)
        vt_ref[head, V_HEAD:, :] = ones_rows


def _mla_projection(x2, mod3, norm_g, w_in, q_a_g, w_q_b, kv_a_g, w_kv_b, q_norm_g, k_norm_g, positions, n_b, seq):
    t, d = x2.shape
    tm = min(512, seq)
    per_b = seq // tm
    o2 = 2 * SSM_WIDTH
    o4, o5 = o2 + Q_LORA + KV_LORA, o2 + Q_LORA + KV_LORA + QK_ROPE
    src = _slab_source()
    valid = src >= 0
    rope_src = np.where(src >= QK_NOPE, src - QK_NOPE, -1)
    nope_src = np.where(valid & (src < QK_NOPE), src, -1)
    wkr = jnp.where((rope_src >= 0)[None, :], w_in[:, o4:o5][:, np.clip(rope_src, 0, QK_ROPE - 1)], 0.0)
    w1 = jnp.concatenate([w_in[:, o2:o4], wkr], axis=1).astype(BF16)
    wzt = w_in[:, o5:].T.astype(BF16)
    wqt = _to_slabs(w_q_b, QK_HEAD, src).T.astype(BF16)
    wkt = _to_slabs(w_kv_b, QK_NOPE + V_HEAD, nope_src).T.astype(BF16)
    wvt = w_kv_b.reshape(KV_LORA, MLA_HEADS, QK_NOPE + V_HEAD)[:, :, QK_NOPE:].reshape(KV_LORA, MLA_WIDTH).T.astype(BF16)
    q_scale = math.log2(math.e) / math.sqrt(QK_HEAD)
    slab_gain = lambda g: jnp.broadcast_to(jnp.where(valid, g[np.clip(src, 0, QK_HEAD - 1)], 0.0)[:, None],
                                           (LANES, LANES))
    gq = slab_gain(q_norm_g * q_scale)
    gk = slab_gain(k_norm_g)
    const = lambda shape: pl.BlockSpec(shape, lambda i: (0,) * len(shape))
    tok = lambda i: (i // per_b, 0, 0, i % per_b)
    return pl.pallas_call(
        _mla_proj_kernel,
        out_shape=(jax.ShapeDtypeStruct((n_b, MLA_HEADS, LANES, seq), BF16),
                   jax.ShapeDtypeStruct((n_b, MLA_HEADS, seq, LANES), BF16),
                   jax.ShapeDtypeStruct((n_b, MLA_HEADS, V_ROWS, seq), BF16),
                   jax.ShapeDtypeStruct((MLA_WIDTH, t), BF16),
                   jax.ShapeDtypeStruct((t, d), F32)),
        grid=(t // tm,),
        in_specs=[pl.BlockSpec((tm, d), lambda i: (i, 0)),
                  pl.BlockSpec((None, 1, 3 * d), lambda i: (i // per_b, 0, 0)),
                  const((1, d)), const(w1.shape), const(wzt.shape), const((1, Q_LORA)), const((1, KV_LORA)),
                  const(wqt.shape), const(wkt.shape), const(wvt.shape),
                  const((LANES, LANES)), const((LANES, LANES)),
                  pl.BlockSpec((1, tm), lambda i: (0, i)),
                  const((ROPE_HALF, 1))],
        out_specs=(pl.BlockSpec((None, MLA_HEADS, LANES, tm), tok),
                   pl.BlockSpec((None, MLA_HEADS, tm, LANES), lambda i: (i // per_b, 0, i % per_b, 0)),
                   pl.BlockSpec((None, MLA_HEADS, V_ROWS, tm), tok),
                   pl.BlockSpec((MLA_WIDTH, tm), lambda i: (0, i)),
                   pl.BlockSpec((tm, d), lambda i: (i, 0))),
        compiler_params=pltpu.CompilerParams(dimension_semantics=("arbitrary",),
                                             vmem_limit_bytes=VMEM_LIMIT),
        name="mla_projection",
    )(x2, mod3, norm_g.reshape(1, d), w1, wzt, q_a_g.reshape(1, -1), kv_a_g.reshape(1, -1), wqt, wkt, wvt, gq, gk,
      positions.reshape(1, t), _rope_frequencies())


ATTN_TQ = 2048
ATTN_TK = 256
HEADS_PER_STEP = 2


def _attn_kernel(qt_ref, k_ref, vt_ref, zs_ref, o_ref, s_scr):
    tq, tk = ATTN_TQ, ATTN_TK
    r = tq // tk
    assert r * tk == tq and r % 2 == 0
    qi = pl.program_id(2)

    def scores(slot, t, lo=0):
        start = pl.multiple_of(t * tk, tk)
        for e in range(HEADS_PER_STEP):
            s_scr[slot, e, :, lo:] = jnp.dot(k_ref[e, pl.ds(start, tk), :], qt_ref[e, :, lo:],
                                             preferred_element_type=F32)

    def process(slot, t, carry, lo=0, diagonal=False):
        start = pl.multiple_of(t * tk, tk)
        out = []
        for e in range(HEADS_PER_STEP):
            m, acc = carry[e]
            s = s_scr[slot, e, :, lo:]
            if diagonal:
                blk = s[:, :tk]
                ok = lax.broadcasted_iota(jnp.int32, blk.shape, 0) <= lax.broadcasted_iota(jnp.int32, blk.shape, 1)
                blk = jnp.where(ok, blk, NEG_BIG)
                s = jnp.concatenate([blk, s[:, tk:]], axis=1) if s.shape[1] > tk else blk
            m_old = m[:, lo:]
            m_new = jnp.maximum(m_old, jnp.max(s, axis=0, keepdims=True))
            p = jnp.exp2(s - m_new).astype(BF16)
            acc_new = (jnp.exp2(m_old - m_new) * acc[:, lo:]
                       + jnp.dot(vt_ref[e, :, pl.ds(start, tk)], p, preferred_element_type=F32))
            if lo:
                m_new = jnp.concatenate([m[:, :lo], m_new], axis=1)
                acc_new = jnp.concatenate([acc[:, :lo], acc_new], axis=1)
            out.append((m_new, acc_new))
        return tuple(out)

    def body(i, carry):
        base = i * r
        for u in range(r):
            scores((u + 1) % 2, base + u + 1)
            carry = process(u % 2, base + u, carry)
        return carry

    init = tuple((jnp.full((1, tq), NEG_BIG, F32), jnp.zeros((V_ROWS, tq), F32)) for _ in range(HEADS_PER_STEP))
    scores(0, 0)
    carry = lax.fori_loop(0, qi, body, init)
    base = qi * r
    for u in range(r):
        if u + 1 < r:
            scores((u + 1) % 2, base + u + 1, lo=(u + 1) * tk)
        carry = process(u % 2, base + u, carry, lo=u * tk, diagonal=True)
    _attn_finish(tuple(c[1] for c in carry), zs_ref, o_ref)


def _attn_finish(accs, zs_ref, o_ref):
    outs = [acc[:V_HEAD] / acc[V_HEAD:V_HEAD + 1] for acc in accs]
    o_ref[...] = (jnp.concatenate(outs, axis=0) * zs_ref[...].astype(F32)).astype(o_ref.dtype)


def _attn_bounded_kernel(qt_ref, k_ref, vt_ref, zs_ref, o_ref, p_scr, acc_scr):
    tq, tk = ATTN_TQ, ATTN_TK
    r = tq // tk
    assert r * tk == tq and r % 2 == 0
    qi = pl.program_id(2)
    q0 = qi * tq

    def probs(slot, t, lo=0, may_cross=False):
        start = pl.multiple_of(t * tk, tk)
        for e in range(HEADS_PER_STEP):
            p = jnp.exp2(jnp.dot(k_ref[e, pl.ds(start, tk), :], qt_ref[e, :, lo:], preferred_element_type=F32))
            if may_cross:
                blk = p[:, :tk]
                key = start + lax.broadcasted_iota(jnp.int32, blk.shape, 0)
                qry = q0 + lo + lax.broadcasted_iota(jnp.int32, blk.shape, 1)
                blk = jnp.where(key <= qry, blk, 0.0)
                p = jnp.concatenate([blk, p[:, tk:]], axis=1) if p.shape[1] > tk else blk
            p_scr[slot, e, :, lo:] = p.astype(BF16)

    def accumulate(slot, t, lo=0):
        start = pl.multiple_of(t * tk, tk)
        for e in range(HEADS_PER_STEP):
            acc_scr[e, :, lo:] += jnp.dot(vt_ref[e, :, pl.ds(start, tk)], p_scr[slot, e, :, lo:],
                                          preferred_element_type=F32)

    def body(i, carry):
        base = i * r
        for u in range(r):
            probs((u + 1) % 2, base + u + 1, may_cross=(u == r - 1))
            accumulate(u % 2, base + u)
        return carry

    acc_scr[...] = jnp.zeros_like(acc_scr)
    probs(0, 0, may_cross=True)
    lax.fori_loop(0, qi, body, 0)
    base = qi * r
    for u in range(r):
        if u + 1 < r:
            probs((u + 1) % 2, base + u + 1, lo=(u + 1) * tk, may_cross=True)
        accumulate(u % 2, base + u, lo=u * tk)
    _attn_finish(tuple(acc_scr[e] for e in range(HEADS_PER_STEP)), zs_ref, o_ref)


SCORE_BOUND_LOG2 = 60.0


def _attention(qt, k, vt, zs, score_bound):
    n_b, n_h, seq, _ = k.shape
    tq = ATTN_TQ
    assert seq % tq == 0 and n_h % HEADS_PER_STEP == 0
    nq = seq // tq
    hp = HEADS_PER_STEP
    width = hp * V_HEAD

    def call(body, scratch):
        return pl.pallas_call(
            body,
            out_shape=jax.ShapeDtypeStruct(zs.shape, BF16),
            grid=(n_b, n_h // hp, nq),
            in_specs=[pl.BlockSpec((None, hp, LANES, tq), lambda b, h, i: (b, h, 0, i)),
                      pl.BlockSpec((None, hp, seq, LANES), lambda b, h, i: (b, h, 0, 0)),
                      pl.BlockSpec((None, hp, V_ROWS, seq), lambda b, h, i: (b, h, 0, 0)),
                      pl.BlockSpec((width, tq), lambda b, h, i: (h, b * nq + i))],
            out_specs=pl.BlockSpec((width, tq), lambda b, h, i: (h, b * nq + i)),
            scratch_shapes=scratch,
            compiler_params=pltpu.CompilerParams(dimension_semantics=("arbitrary", "arbitrary", "arbitrary"),
                                                 vmem_limit_bytes=VMEM_LIMIT),
            name="causal_attention",
        )(qt, k, vt, zs)

    tiles = (2, hp, ATTN_TK, tq)
    return lax.cond(score_bound <= SCORE_BOUND_LOG2,
                    lambda: call(_attn_bounded_kernel, [pltpu.VMEM(tiles, BF16), pltpu.VMEM((hp, V_ROWS, tq), F32)]),
                    lambda: call(_attn_kernel, [pltpu.VMEM(tiles, F32)]))


def _s5_scan_steps(n_top):
    return [CHUNK * m for m in range(SUPER)] + [CHUNK * SUPER * (1 << i) for i in range(n_top)]


def _s5_exponents(n_top):
    return sorted(set(range(CHUNK + 1)) | set(_s5_scan_steps(n_top)))


def _s5_coef_kernel(kk_ref, *refs, n_top):
    for gi in range(SCAN_GROUPS):
        _s5_coef_one_group(kk_ref, *(r.at[gi] for r in refs), n_top=n_top)


def _s5_coef_one_group(kk_ref, lre_ref, lim_ref, ldt_ref, bre_ref, bim_ref, cre_ref, cim_ref, d_ref,
                       lhs_ref, wout_ref, ap_ref, *, n_top):
    h, p, n = SSM_GROUP, SSM_STATE, CHUNK
    lre, lim = lre_ref[...], lim_ref[...]
    dt = jnp.exp(ldt_ref[...])
    exps = _s5_exponents(n_top)
    kcol = kk_ref[...]
    mag = jnp.exp(kcol * (lre * dt))
    ang = kcol * (lim * dt)
    pow_re, pow_im = mag * jnp.cos(ang), mag * jnp.sin(ang)

    def power(kk):
        i = exps.index(kk)
        return pow_re[i:i + 1], pow_im[i:i + 1]

    lb_re, lb_im = power(1)
    nr, ni = lb_re - 1.0, lb_im
    den = lre * lre + lim * lim
    f_re = (nr * lre + ni * lim) / den
    f_im = (ni * lre - nr * lim) / den
    bre, bim = bre_ref[...].T, bim_ref[...].T
    bb_re = f_re * bre - f_im * bim
    bb_im = f_re * bim + f_im * bre
    cre, cim = cre_ref[...], cim_ref[...]
    cp_re, cp_im = [], []
    for kk in range(n + 1):
        pr, pi = power(kk)
        cp_re.append(cre * pr - cim * pi)
        cp_im.append(cre * pi + cim * pr)

    cpw = [jnp.concatenate([cp_re[kk], -cp_im[kk]], axis=1) for kk in range(n + 1)]
    bb_a = jnp.concatenate([bb_re, bb_im], axis=1)
    kw = lax.dot_general(jnp.concatenate(cpw[:n], axis=0), jnp.concatenate([bb_a] * n, axis=0), _NT,
                         preferred_element_type=F32, precision=HIGHEST)
    lane = lax.broadcasted_iota(jnp.int32, (h, n * h), 1)
    row = lax.broadcasted_iota(jnp.int32, (h, n * h), 0)
    col_blk = lax.shift_right_logical(lane, 4)
    d_tiled = jnp.concatenate([d_ref[...]] * n, axis=1)
    lag = [kw[h * kk:h * (kk + 1)] for kk in range(n)]
    lag[0] = lag[0] + jnp.where((lane & (h - 1)) == row, d_tiled, 0.0)
    rows = []
    for j in range(n):
        acc = jnp.zeros((h, n * h), F32)
        for jp in range(j + 1):
            acc = jnp.where(col_blk == jp, lag[j - jp], acc)
        rows.append(acc)
    lhs_ref[0:n * h, :] = jnp.concatenate(rows, axis=0).astype(lhs_ref.dtype)

    bb_b = jnp.concatenate([-bb_im, bb_re], axis=1)
    win = []
    for j in range(n):
        pr, pi = power(n - 1 - j)
        win.append(jnp.concatenate([pr, pr], axis=1) * bb_a + jnp.concatenate([pi, pi], axis=1) * bb_b)
    lhs_ref[n * h:, :] = jnp.concatenate(win, axis=0).T.astype(lhs_ref.dtype)
    wout_ref[...] = jnp.concatenate(cpw[1:], axis=0).astype(wout_ref.dtype)
    for i, kk in enumerate(_s5_scan_steps(n_top)):
        pr, pi = power(kk)
        ap_ref[i:i + 1, :] = jnp.concatenate([pr, pi], axis=1)


def _s5_coefficients(log_dt, lam_re, lam_im, b_re, b_im, c_re, c_im, d_skip, n_top):
    g, p, h = SSM_GROUPS, SSM_STATE, SSM_GROUP
    n_ap = SUPER + n_top
    gb = SCAN_GROUPS
    grp = lambda *shape: pl.BlockSpec((gb,) + shape, lambda i: (i,) + (0,) * len(shape))
    exps = np.asarray(_s5_exponents(n_top), np.float32)
    kk = np.zeros((-(-exps.size // 8) * 8, 1), np.float32)
    kk[:exps.size, 0] = exps
    return pl.pallas_call(
        functools.partial(_s5_coef_kernel, n_top=n_top),
        out_shape=(jax.ShapeDtypeStruct((g, CHUNK * h + 2 * p, CHUNK * h), BF16),
                   jax.ShapeDtypeStruct((g, CHUNK * h, 2 * p), BF16),
                   jax.ShapeDtypeStruct((g, n_ap, 2 * p), F32)),
        grid=(g // gb,),
        in_specs=[pl.BlockSpec(kk.shape, lambda i: (0, 0)),
                  grp(1, p), grp(1, p), grp(1, 1), grp(p, h), grp(p, h), grp(h, p), grp(h, p), grp(1, h)],
        out_specs=(grp(CHUNK * h + 2 * p, CHUNK * h), grp(CHUNK * h, 2 * p), grp(n_ap, 2 * p)),
        name="s5_coefficients",
    )(jnp.asarray(kk), lam_re.reshape(g, 1, p), lam_im.reshape(g, 1, p), log_dt.reshape(g, 1, 1), b_re, b_im,
      c_re, c_im, d_skip.reshape(g, 1, h))


PHASES = 8


def _ssm_proj_kernel(h_ref, w_ref, u_ref, z_ref, rows_scr, *, n_b):
    for jj in range(PHASES):
        for b in range(n_b):
            rows_scr[b] = h_ref[b, :, jj, :]
        h = jnp.concatenate([rows_scr[b].astype(BF16) for b in range(n_b)], axis=0)
        r = lax.dot_general(w_ref[...], h, _NT, preferred_element_type=F32)
        u_ref[jj] = r[:SSM_WIDTH].astype(BF16)
        z_ref[jj] = _silu(r[SSM_WIDTH:]).astype(BF16)


def _ssm_projection(h4, w_in):
    n_b, c2, _, d = h4.shape
    lanes = n_b * c2
    halves = CHUNK // PHASES
    w_t = w_in[:, :2 * SSM_WIDTH].T.astype(BF16)
    out = jax.ShapeDtypeStruct((SUPER, CHUNK, SSM_WIDTH, lanes), BF16)
    out_spec = pl.BlockSpec((None, PHASES, SSM_WIDTH, lanes), lambda j2, hf: (j2, hf, 0, 0))
    return pl.pallas_call(
        functools.partial(_ssm_proj_kernel, n_b=n_b),
        out_shape=(out, out),
        grid=(SUPER, halves),
        in_specs=[pl.BlockSpec((n_b, c2, PHASES, d), lambda j2, hf: (0, 0, j2 * halves + hf, 0)),
                  pl.BlockSpec(w_t.shape, lambda j2, hf: (0, 0))],
        out_specs=(out_spec, out_spec),
        scratch_shapes=[pltpu.VMEM((n_b, c2, d), F32)],
        compiler_params=pltpu.CompilerParams(dimension_semantics=("arbitrary", "arbitrary"),
                                             vmem_limit_bytes=VMEM_LIMIT),
        name="ssm_projection",
    )(h4, w_t)


def _cmul(ar, ai, xr, xi):
    return ar * xr - ai * xi, ar * xi + ai * xr


SCAN_GROUPS = 8


def _s5_group_kernel(x_ref, lhs_ref, wout_ref, ap_ref, y_ref, *, n_b, c2, n_top):
    p, gb = SSM_STATE, SCAN_GROUPS
    rows = CHUNK * SSM_GROUP
    n = SUPER * n_b * c2
    lanes = n_b * c2

    def x_group(gi):
        return jnp.concatenate([x_ref[j2, :, SSM_GROUP * gi:SSM_GROUP * (gi + 1), :].reshape(rows, lanes)
                                for j2 in range(SUPER)], axis=1)

    r = [jnp.dot(lhs_ref[gi], x_group(gi), preferred_element_type=F32) for gi in range(gb)]
    c_re = jnp.concatenate([r[gi][rows:rows + p] for gi in range(gb)], axis=0)
    c_im = jnp.concatenate([r[gi][rows + p:] for gi in range(gb)], axis=0)

    def mult(i):
        rep = [jnp.broadcast_to(ap_ref[gi, i:i + 1, :], (2 * p, 2 * p)).T for gi in range(gb)]
        return (jnp.concatenate([m[:p, :c2] for m in rep], axis=0),
                jnp.concatenate([m[p:, :c2] for m in rep], axis=0))

    lane = lax.broadcasted_iota(jnp.int32, (gb * p, c2), 1)

    def shifted(a, sh):
        return jnp.where(lane >= sh, pltpu.roll(a, sh, 1), 0.0)

    pieces = [[None] * n_b for _ in range(SUPER)]
    for b in range(n_b):
        piece = lambda a, j2: a[:, (j2 * n_b + b) * c2:(j2 * n_b + b + 1) * c2]
        a_re, a_im = mult(1)
        e_re = jnp.zeros((gb * p, c2), F32)
        e_im = jnp.zeros((gb * p, c2), F32)
        local = []
        for j2 in range(SUPER):
            local.append((e_re, e_im))
            t_re, t_im = _cmul(a_re, a_im, e_re, e_im)
            e_re, e_im = t_re + piece(c_re, j2), t_im + piece(c_im, j2)
        for i in range(n_top):
            m_re, m_im = mult(SUPER + i)
            t_re, t_im = _cmul(m_re, m_im, shifted(e_re, 1 << i), shifted(e_im, 1 << i))
            e_re, e_im = e_re + t_re, e_im + t_im
        s_re, s_im = shifted(e_re, 1), shifted(e_im, 1)
        for j2 in range(SUPER):
            if j2 == 0:
                pieces[j2][b] = (s_re, s_im)
            else:
                m_re, m_im = mult(j2)
                t_re, t_im = _cmul(m_re, m_im, s_re, s_im)
                pieces[j2][b] = (local[j2][0] + t_re, local[j2][1] + t_im)
    for gi in range(gb):
        sl = slice(gi * p, (gi + 1) * p)
        s_in = jnp.concatenate([jnp.concatenate([pieces[j2][b][0][sl], pieces[j2][b][1][sl]], axis=0)
                                for j2 in range(SUPER) for b in range(n_b)], axis=1)
        y = r[gi][:rows] + jnp.dot(wout_ref[gi], s_in.astype(BF16), preferred_element_type=F32)
        for j2 in range(SUPER):
            y_ref[j2, gi] = y[:, j2 * lanes:(j2 + 1) * lanes].reshape(CHUNK, SSM_GROUP, lanes).astype(y_ref.dtype)


def _s5_scan(u_t, lhs, wout, ap, n_b, c2, n_top):
    g, h = SSM_GROUPS, SSM_GROUP
    lanes = u_t.shape[-1]
    gb = SCAN_GROUPS
    grp = lambda *shape: pl.BlockSpec((gb,) + shape, lambda i: (i,) + (0,) * len(shape))
    return pl.pallas_call(
        functools.partial(_s5_group_kernel, n_b=n_b, c2=c2, n_top=n_top),
        out_shape=jax.ShapeDtypeStruct((SUPER, g, CHUNK, h, lanes), BF16),
        grid=(g // gb,),
        in_specs=[pl.BlockSpec((SUPER, CHUNK, gb * h, lanes), lambda i: (0, 0, i, 0)),
                  grp(*lhs.shape[1:]), grp(*wout.shape[1:]), grp(*ap.shape[1:])],
        out_specs=pl.BlockSpec((SUPER, gb, CHUNK, h, lanes), lambda i: (0, i, 0, 0, 0)),
        compiler_params=pltpu.CompilerParams(dimension_semantics=("arbitrary",),
                                             vmem_limit_bytes=VMEM_LIMIT),
        name="s5_chunk_scan",
    )(u_t, lhs, wout, ap)


def _glu_kernel(y_ref, zs_ref, w_ref, b_ref, o_ref, *, n_b, c2):
    g, _, h, n = y_ref.shape
    for jj in range(PHASES):
        y = jax.nn.gelu(y_ref[:, jj].reshape(g * h, n).astype(F32))
        t = jnp.dot(w_ref[...], y.astype(BF16), preferred_element_type=F32) + b_ref[...]
        gated = (y * jax.nn.sigmoid(t) * zs_ref[jj].astype(F32)).T
        for b in range(n_b):
            o_ref[b, :, jj, :] = gated[b * c2:(b + 1) * c2]


def _glu(y_t, zs_t, w_glu, b_glu, n_b, c2):
    _, g, _, h, _ = y_t.shape
    lanes = n_b * c2
    halves = CHUNK // PHASES
    w_t = w_glu.T.astype(BF16)
    return pl.pallas_call(
        functools.partial(_glu_kernel, n_b=n_b, c2=c2),
        out_shape=jax.ShapeDtypeStruct((n_b, c2, SUPER * CHUNK, SSM_WIDTH), F32),
        grid=(SUPER, halves),
        in_specs=[pl.BlockSpec((None, g, PHASES, h, lanes), lambda j2, hf: (j2, 0, hf, 0, 0)),
                  pl.BlockSpec((None, PHASES, SSM_WIDTH, lanes), lambda j2, hf: (j2, hf, 0, 0)),
                  pl.BlockSpec(w_t.shape, lambda j2, hf: (0, 0)),
                  pl.BlockSpec((SSM_WIDTH, 1), lambda j2, hf: (0, 0))],
        out_specs=pl.BlockSpec((n_b, c2, PHASES, SSM_WIDTH), lambda j2, hf: (0, 0, j2 * halves + hf, 0)),
        compiler_params=pltpu.CompilerParams(dimension_semantics=("arbitrary", "arbitrary"),
                                             vmem_limit_bytes=VMEM_LIMIT),
        name="s5_glu",
    )(y_t, zs_t, w_t, b_glu.reshape(SSM_WIDTH, 1))


def _out_proj_kernel(ys_ref, ym_ref, x_ref, mod_ref, wa_ref, wb_ref, o_ref):
    y = (jnp.dot(ys_ref[...].astype(BF16), wa_ref[...], preferred_element_type=F32)
         + lax.dot_general(ym_ref[...], wb_ref[...], _TN, preferred_element_type=F32))
    gate = mod_ref[...][:, 2 * D_MODEL:]
    o_ref[...] = x_ref[...] + gate * y


def _out_projection(ys, ym, x2, mod3, w_out, seq):
    t, d = x2.shape
    tm = min(1024, seq)
    per_b = seq // tm
    wa = w_out[:SSM_WIDTH].astype(BF16)
    wb = w_out[SSM_WIDTH:].astype(BF16)
    tok = lambda w: pl.BlockSpec((tm, w), lambda i: (i, 0))
    return pl.pallas_call(
        _out_proj_kernel,
        out_shape=jax.ShapeDtypeStruct(x2.shape, F32),
        grid=(t // tm,),
        in_specs=[tok(SSM_WIDTH), pl.BlockSpec((MLA_WIDTH, tm), lambda i: (0, i)), tok(d),
                  pl.BlockSpec((None, 1, 3 * d), lambda i: (i // per_b, 0, 0)),
                  pl.BlockSpec(wa.shape, lambda i: (0, 0)),
                  pl.BlockSpec(wb.shape, lambda i: (0, 0))],
        out_specs=tok(d),
        compiler_params=pltpu.CompilerParams(dimension_semantics=("arbitrary",),
                                             vmem_limit_bytes=VMEM_LIMIT),
        name="output_projection",
    )(ys, ym, x2, mod3, wa, wb)


def kernel(x, c, positions, w_ada, b_ada, norm_g, w_in, log_dt, lam_re, lam_im, b_re, b_im, c_re, c_im, d_skip,
           w_glu, b_glu, q_a_g, w_q_b, kv_a_g, w_kv_b, q_norm_g, k_norm_g, w_out):
    n_b, seq, d = x.shape
    depth = w_ada.shape[0]
    c2 = seq // (CHUNK * SUPER)
    n_top = max(int(math.log2(c2)), 0)
    assert c2 * CHUNK * SUPER == seq and (1 << n_top) == c2
    for l in range(depth):
        mod3 = _modulation(c, w_ada[l], b_ada[l]).reshape(n_b, 1, 3 * d)
        x2 = x.reshape(n_b * seq, d)
        lhs, wout, ap = _s5_coefficients(log_dt[l], lam_re[l], lam_im[l], b_re[l], b_im[l], c_re[l], c_im[l],
                                         d_skip[l], n_top)
        qt, k, vt, zm, h = _mla_projection(x2, mod3, norm_g[l], w_in[l], q_a_g[l], w_q_b[l],
                                           kv_a_g[l], w_kv_b[l], q_norm_g[l], k_norm_g[l], positions, n_b, seq)
        score_bound = (1.05 * math.sqrt(QK_HEAD) * math.log2(math.e)
                       * jnp.max(jnp.abs(q_norm_g[l])) * jnp.max(jnp.abs(k_norm_g[l])))
        ym = _attention(qt, k, vt, zm, score_bound)
        ym, h = lax.optimization_barrier((ym, h))
        u_t, zs_t = _ssm_projection(h.reshape(n_b, c2, SUPER * CHUNK, d), w_in[l])
        y_t = _s5_scan(u_t, lhs, wout, ap, n_b, c2, n_top)
        ys = _glu(y_t, zs_t, w_glu[l], b_glu[l], n_b, c2).reshape(n_b * seq, SSM_WIDTH)
        x = _out_projection(ys, ym, x2, mod3, w_out[l], seq).reshape(n_b, seq, d)
    return x
```

```python
import functools
import math

import numpy as np
import jax
import jax.numpy as jnp
from jax import lax
from jax.experimental import pallas as pl
from jax.experimental.pallas import tpu as pltpu

F32 = jnp.float32
BF16 = jnp.bfloat16
HIGHEST = lax.Precision.HIGHEST

D_MODEL = 1024
SSM_WIDTH = 512
SSM_GROUP = 16
SSM_GROUPS = 32
SSM_STATE = 64
MLA_HEADS = 8
QK_NOPE = 64
QK_ROPE = 32
QK_HEAD = QK_NOPE + QK_ROPE
V_HEAD = 64
MLA_WIDTH = MLA_HEADS * V_HEAD
Q_LORA = 384
KV_LORA = 256
ROPE_THETA = 10000.0
EPS = 1e-6
NEG_BIG = -1e30

LANES = 128
VMEM_LIMIT = 60 * 1024 * 1024
CHUNK = 16
SUPER = 4
ROPE_HALF = QK_ROPE // 2

_NT = (((1,), (1,)), ((), ()))
_TN = (((0,), (0,)), ((), ()))


def _silu(v):
    return v * jax.nn.sigmoid(v)


def _rms(v, gain):
    return v * lax.rsqrt(jnp.mean(v * v, axis=-1, keepdims=True) + EPS) * gain


def _mod_kernel(c_ref, w_ref, b_ref, o_ref):
    n_b, d = c_ref.shape
    tn = w_ref.shape[1]
    act_t = _silu(c_ref[...]).T
    out = []
    for b in range(n_b):
        col = jnp.broadcast_to(act_t[:, b:b + 1], (d, LANES))
        out.append(jnp.concatenate([jnp.sum(w_ref[:, j:j + LANES] * col, axis=0, keepdims=True)
                                    for j in range(0, tn, LANES)], axis=1))
    o_ref[...] = jnp.concatenate(out, axis=0) + b_ref[...]


def _modulation(c, w, b):
    n_b, d = c.shape
    tn = 512
    return pl.pallas_call(
        _mod_kernel,
        out_shape=jax.ShapeDtypeStruct((n_b, 3 * d), F32),
        grid=(3 * d // tn,),
        in_specs=[pl.BlockSpec((n_b, d), lambda n: (0, 0)),
                  pl.BlockSpec((d, tn), lambda n: (0, n)),
                  pl.BlockSpec((1, tn), lambda n: (0, n))],
        out_specs=pl.BlockSpec((n_b, tn), lambda n: (0, n)),
        name="adaln_modulation",
    )(c, w, b.reshape(1, -1))


def _rope_frequencies():
    inv_freq = ROPE_THETA ** (-np.arange(ROPE_HALF, dtype=np.float64) * 2.0 / QK_ROPE)
    return jnp.asarray(inv_freq.astype(np.float32)).reshape(ROPE_HALF, 1)


def _slab_source():
    src = np.full((LANES,), -1, np.int64)
    src[0:16] = QK_NOPE + np.arange(16)
    src[16:64] = np.arange(48)
    src[64:80] = QK_NOPE + 16 + np.arange(16)
    src[80:96] = 48 + np.arange(16)
    return src


def _take_static(w, src):
    parts, i = [], 0
    while i < len(src):
        j = i + 1
        while j < len(src) and (src[j] < 0 if src[i] < 0 else src[j] == src[j - 1] + 1):
            j += 1
        parts.append(jnp.zeros(w.shape[:-1] + (j - i,), w.dtype) if src[i] < 0
                     else w[..., int(src[i]):int(src[i]) + j - i])
        i = j
    return jnp.concatenate(parts, axis=-1)


def _to_slabs(w, per_head, src):
    k = w.shape[0]
    return _take_static(w.reshape(k, MLA_HEADS, per_head), src).reshape(k, MLA_HEADS * LANES)


V_ROWS = V_HEAD + 16


def _mla_proj_kernel(x_ref, mod_ref, ng_ref, w1_ref, wzt_ref, qag_ref, kvag_ref, wqt_ref, wkt_ref, wvt_ref,
                     gq_ref, gk_ref, pos_ref, freq_ref, qt_ref, k_ref, vt_ref, zs_ref, h_ref):
    d = D_MODEL
    tm = x_ref.shape[0]
    x = x_ref[...]
    mod = mod_ref[...]
    shift, scale = mod[:, :d], mod[:, d:2 * d]
    h_ref[...] = _rms(x, ng_ref[...] * (1.0 + scale)) + shift
    h = h_ref[...].astype(BF16)
    p1 = jnp.dot(h, w1_ref[...], preferred_element_type=F32)
    o1, o2 = Q_LORA, Q_LORA + KV_LORA
    cqn = _rms(p1[:, :o1], qag_ref[...]).astype(BF16)
    ckvn = _rms(p1[:, o1:o2], kvag_ref[...]).astype(BF16)
    zs_ref[...] = _silu(lax.dot_general(wzt_ref[...], h, _NT, preferred_element_type=F32)).astype(BF16)

    qt = lax.dot_general(wqt_ref[...], cqn, _NT, preferred_element_type=F32)
    kt = lax.dot_general(wkt_ref[...], ckvn, _NT, preferred_element_type=F32)
    krt = p1[:, o2:].T
    vt = lax.dot_general(wvt_ref[...], ckvn, _NT, preferred_element_type=F32)
    ang = freq_ref[...] * pos_ref[...].astype(F32)
    cos_t, sin_t = jnp.cos(ang), jnp.sin(ang)
    gq = jnp.concatenate([gq_ref[...]] * (tm // LANES), axis=1)
    gk = jnp.concatenate([gk_ref[...]] * (tm // LANES), axis=1)
    first = lax.broadcasted_iota(jnp.int32, (V_ROWS - V_HEAD, tm), 0) == 0
    ones_rows = jnp.where(first, 1.0, 0.0).astype(BF16)
    r, half = ROPE_HALF, LANES // 2

    def norm_rope(slab, gain):
        ss = jnp.sum(slab * slab, axis=0, keepdims=True)
        n = slab * lax.rsqrt(ss * (1.0 / QK_HEAD) + EPS) * gain
        x1, x2 = n[0:r], n[half:half + r]
        return jnp.concatenate([x1 * cos_t - x2 * sin_t, n[r:half], x2 * cos_t + x1 * sin_t, n[half + r:]], axis=0)

    for head in range(MLA_HEADS):
        rows = slice(LANES * head, LANES * (head + 1))
        qt_ref[head] = norm_rope(qt[rows], gq).astype(BF16)
        k_ref[head] = norm_rope(kt[rows] + krt, gk).T.astype(BF16)
        vt_ref[head, 0:V_HEAD, :] = vt[V_HEAD * head:V_HEAD * (head + 1)].astype(BF16)
        vt_ref[head, V_HEAD:, :] = ones_rows


def _mla_projection(x2, mod3, norm_g, w_in, q_a_g, w_q_b, kv_a_g, w_kv_b, q_norm_g, k_norm_g, positions, n_b, seq):
    t, d = x2.shape
    tm = min(512, seq)
    per_b = seq // tm
    o2 = 2 * SSM_WIDTH
    o4, o5 = o2 + Q_LORA + KV_LORA, o2 + Q_LORA + KV_LORA + QK_ROPE
    src = _slab_source()
    rope_src = np.where(src >= QK_NOPE, src - QK_NOPE, -1)
    nope_src = np.where((src >= 0) & (src < QK_NOPE), src, -1)
    wkr = _take_static(w_in[:, o4:o5], rope_src)
    w1 = jnp.concatenate([w_in[:, o2:o4], wkr], axis=1).astype(BF16)
    wzt = w_in[:, o5:].T.astype(BF16)
    wqt = _to_slabs(w_q_b, QK_HEAD, src).T.astype(BF16)
    wkt = _to_slabs(w_kv_b, QK_NOPE + V_HEAD, nope_src).T.astype(BF16)
    wvt = w_kv_b.reshape(KV_LORA, MLA_HEADS, QK_NOPE + V_HEAD)[:, :, QK_NOPE:].reshape(KV_LORA, MLA_WIDTH).T.astype(BF16)
    q_scale = math.log2(math.e) / math.sqrt(QK_HEAD)
    slab_gain = lambda g: jnp.broadcast_to(_take_static(g, src)[:, None], (LANES, LANES))
    gq = slab_gain(q_norm_g * q_scale)
    gk = slab_gain(k_norm_g)
    const = lambda shape: pl.BlockSpec(shape, lambda i: (0,) * len(shape))
    tok = lambda i: (i // per_b, 0, 0, i % per_b)
    return pl.pallas_call(
        _mla_proj_kernel,
        out_shape=(jax.ShapeDtypeStruct((n_b, MLA_HEADS, LANES, seq), BF16),
                   jax.ShapeDtypeStruct((n_b, MLA_HEADS, seq, LANES), BF16),
                   jax.ShapeDtypeStruct((n_b, MLA_HEADS, V_ROWS, seq), BF16),
                   jax.ShapeDtypeStruct((MLA_WIDTH, t), BF16),
                   jax.ShapeDtypeStruct((t, d), F32)),
        grid=(t // tm,),
        in_specs=[pl.BlockSpec((tm, d), lambda i: (i, 0)),
                  pl.BlockSpec((None, 1, 3 * d), lambda i: (i // per_b, 0, 0)),
                  const((1, d)), const(w1.shape), const(wzt.shape), const((1, Q_LORA)), const((1, KV_LORA)),
                  const(wqt.shape), const(wkt.shape), const(wvt.shape),
                  const((LANES, LANES)), const((LANES, LANES)),
                  pl.BlockSpec((1, tm), lambda i: (0, i)),
                  const((ROPE_HALF, 1))],
        out_specs=(pl.BlockSpec((None, MLA_HEADS, LANES, tm), tok),
                   pl.BlockSpec((None, MLA_HEADS, tm, LANES), lambda i: (i // per_b, 0, i % per_b, 0)),
                   pl.BlockSpec((None, MLA_HEADS, V_ROWS, tm), tok),
                   pl.BlockSpec((MLA_WIDTH, tm), lambda i: (0, i)),
                   pl.BlockSpec((tm, d), lambda i: (i, 0))),
        compiler_params=pltpu.CompilerParams(dimension_semantics=("arbitrary",),
                                             vmem_limit_bytes=VMEM_LIMIT),
        name="mla_projection",
    )(x2, mod3, norm_g.reshape(1, d), w1, wzt, q_a_g.reshape(1, -1), kv_a_g.reshape(1, -1), wqt, wkt, wvt, gq, gk,
      positions.reshape(1, t), _rope_frequencies())


ATTN_TQ = 2048
ATTN_TK = 256
HEADS_PER_STEP = 2


def _attn_kernel(qt_ref, k_ref, vt_ref, zs_ref, o_ref, s_scr):
    tq, tk = ATTN_TQ, ATTN_TK
    r = tq // tk
    assert r * tk == tq and r % 2 == 0
    qi = pl.program_id(2)

    def scores(slot, t, lo=0):
        start = pl.multiple_of(t * tk, tk)
        for e in range(HEADS_PER_STEP):
            s_scr[slot, e, :, lo:] = jnp.dot(k_ref[e, pl.ds(start, tk), :], qt_ref[e, :, lo:],
                                             preferred_element_type=F32)

    def process(slot, t, carry, lo=0, diagonal=False):
        start = pl.multiple_of(t * tk, tk)
        out = []
        for e in range(HEADS_PER_STEP):
            m, acc = carry[e]
            s = s_scr[slot, e, :, lo:]
            if diagonal:
                blk = s[:, :tk]
                ok = lax.broadcasted_iota(jnp.int32, blk.shape, 0) <= lax.broadcasted_iota(jnp.int32, blk.shape, 1)
                blk = jnp.where(ok, blk, NEG_BIG)
                s = jnp.concatenate([blk, s[:, tk:]], axis=1) if s.shape[1] > tk else blk
            m_old = m[:, lo:]
            m_new = jnp.maximum(m_old, jnp.max(s, axis=0, keepdims=True))
            p = jnp.exp2(s - m_new).astype(BF16)
            acc_new = (jnp.exp2(m_old - m_new) * acc[:, lo:]
                       + jnp.dot(vt_ref[e, :, pl.ds(start, tk)], p, preferred_element_type=F32))
            if lo:
                m_new = jnp.concatenate([m[:, :lo], m_new], axis=1)
                acc_new = jnp.concatenate([acc[:, :lo], acc_new], axis=1)
            out.append((m_new, acc_new))
        return tuple(out)

    def body(i, carry):
        base = i * r
        for u in range(r):
            scores((u + 1) % 2, base + u + 1)
            carry = process(u % 2, base + u, carry)
        return carry

    init = tuple((jnp.full((1, tq), NEG_BIG, F32), jnp.zeros((V_ROWS, tq), F32)) for _ in range(HEADS_PER_STEP))
    scores(0, 0)
    carry = lax.fori_loop(0, qi, body, init)
    base = qi * r
    for u in range(r):
        if u + 1 < r:
            scores((u + 1) % 2, base + u + 1, lo=(u + 1) * tk)
        carry = process(u % 2, base + u, carry, lo=u * tk, diagonal=True)
    _attn_finish(tuple(c[1] for c in carry), zs_ref, o_ref)


def _attn_finish(accs, zs_ref, o_ref):
    outs = [acc[:V_HEAD] / acc[V_HEAD:V_HEAD + 1] for acc in accs]
    o_ref[...] = (jnp.concatenate(outs, axis=0) * zs_ref[...].astype(F32)).astype(o_ref.dtype)


def _attn_bounded_kernel(qt_ref, k_ref, vt_ref, zs_ref, o_ref, p_scr, acc_scr):
    tq, tk = ATTN_TQ, ATTN_TK
    r = tq // tk
    assert r * tk == tq and r % 2 == 0
    qi = pl.program_id(2)
    q0 = qi * tq

    def probs(slot, t, lo=0, may_cross=False):
        start = pl.multiple_of(t * tk, tk)
        for e in range(HEADS_PER_STEP):
            p = jnp.exp2(jnp.dot(k_ref[e, pl.ds(start, tk), :], qt_ref[e, :, lo:], preferred_element_type=F32))
            if may_cross:
                blk = p[:, :tk]
                key = start + lax.broadcasted_iota(jnp.int32, blk.shape, 0)
                qry = q0 + lo + lax.broadcasted_iota(jnp.int32, blk.shape, 1)
                blk = jnp.where(key <= qry, blk, 0.0)
                p = jnp.concatenate([blk, p[:, tk:]], axis=1) if p.shape[1] > tk else blk
            p_scr[slot, e, :, lo:] = p.astype(BF16)

    def accumulate(slot, t, lo=0):
        start = pl.multiple_of(t * tk, tk)
        for e in range(HEADS_PER_STEP):
            acc_scr[e, :, lo:] += jnp.dot(vt_ref[e, :, pl.ds(start, tk)], p_scr[slot, e, :, lo:],
                                          preferred_element_type=F32)

    def body(i, carry):
        base = i * r
        for u in range(r):
            probs((u + 1) % 2, base + u + 1, may_cross=(u == r - 1))
            accumulate(u % 2, base + u)
        return carry

    acc_scr[...] = jnp.zeros_like(acc_scr)
    probs(0, 0, may_cross=True)
    lax.fori_loop(0, qi, body, 0)
    base = qi * r
    for u in range(r):
        if u + 1 < r:
            probs((u + 1) % 2, base + u + 1, lo=(u + 1) * tk, may_cross=True)
        accumulate(u % 2, base + u, lo=u * tk)
    _attn_finish(tuple(acc_scr[e] for e in range(HEADS_PER_STEP)), zs_ref, o_ref)


SCORE_BOUND_LOG2 = 60.0


def _attention(qt, k, vt, zs, score_bound):
    n_b, n_h, seq, _ = k.shape
    tq = ATTN_TQ
    assert seq % tq == 0 and n_h % HEADS_PER_STEP == 0
    nq = seq // tq
    hp = HEADS_PER_STEP
    width = hp * V_HEAD

    def call(body, scratch):
        return pl.pallas_call(
            body,
            out_shape=jax.ShapeDtypeStruct(zs.shape, BF16),
            grid=(n_b, n_h // hp, nq),
            in_specs=[pl.BlockSpec((None, hp, LANES, tq), lambda b, h, i: (b, h, 0, i)),
                      pl.BlockSpec((None, hp, seq, LANES), lambda b, h, i: (b, h, 0, 0)),
                      pl.BlockSpec((None, hp, V_ROWS, seq), lambda b, h, i: (b, h, 0, 0)),
                      pl.BlockSpec((width, tq), lambda b, h, i: (h, b * nq + i))],
            out_specs=pl.BlockSpec((width, tq), lambda b, h, i: (h, b * nq + i)),
            scratch_shapes=scratch,
            compiler_params=pltpu.CompilerParams(dimension_semantics=("arbitrary", "arbitrary", "arbitrary"),
                                                 vmem_limit_bytes=VMEM_LIMIT),
            name="causal_attention",
        )(qt, k, vt, zs)

    tiles = (2, hp, ATTN_TK, tq)
    return lax.cond(score_bound <= SCORE_BOUND_LOG2,
                    lambda: call(_attn_bounded_kernel, [pltpu.VMEM(tiles, BF16), pltpu.VMEM((hp, V_ROWS, tq), F32)]),
                    lambda: call(_attn_kernel, [pltpu.VMEM(tiles, F32)]))


def _s5_scan_steps(n_top):
    return [CHUNK * m for m in range(SUPER)] + [CHUNK * SUPER * (1 << i) for i in range(n_top)]


def _s5_exponents(n_top):
    return sorted(set(range(CHUNK + 1)) | set(_s5_scan_steps(n_top)))


def _s5_coef_kernel(kk_ref, *refs, n_top):
    for gi in range(SCAN_GROUPS):
        _s5_coef_one_group(kk_ref, *(r.at[gi] for r in refs), n_top=n_top)


def _s5_coef_one_group(kk_ref, lre_ref, lim_ref, ldt_ref, bre_ref, bim_ref, cre_ref, cim_ref, d_ref,
                       lhs_ref, wout_ref, ap_ref, *, n_top):
    h, p, n = SSM_GROUP, SSM_STATE, CHUNK
    lre, lim = lre_ref[...], lim_ref[...]
    dt = jnp.exp(ldt_ref[...])
    exps = _s5_exponents(n_top)
    kcol = kk_ref[...]
    mag = jnp.exp(kcol * (lre * dt))
    ang = kcol * (lim * dt)
    pow_re, pow_im = mag * jnp.cos(ang), mag * jnp.sin(ang)

    def power(kk):
        i = exps.index(kk)
        return pow_re[i:i + 1], pow_im[i:i + 1]

    lb_re, lb_im = power(1)
    nr, ni = lb_re - 1.0, lb_im
    den = lre * lre + lim * lim
    f_re = (nr * lre + ni * lim) / den
    f_im = (ni * lre - nr * lim) / den
    bre, bim = bre_ref[...].T, bim_ref[...].T
    bb_re = f_re * bre - f_im * bim
    bb_im = f_re * bim + f_im * bre
    cre, cim = cre_ref[...], cim_ref[...]
    cp_re, cp_im = [], []
    for kk in range(n + 1):
        pr, pi = power(kk)
        cp_re.append(cre * pr - cim * pi)
        cp_im.append(cre * pi + cim * pr)

    cpw = [jnp.concatenate([cp_re[kk], -cp_im[kk]], axis=1) for kk in range(n + 1)]
    bb_a = jnp.concatenate([bb_re, bb_im], axis=1)
    kw = lax.dot_general(jnp.concatenate(cpw[:n], axis=0), jnp.concatenate([bb_a] * n, axis=0), _NT,
                         preferred_element_type=F32, precision=HIGHEST)
    lane = lax.broadcasted_iota(jnp.int32, (h, n * h), 1)
    row = lax.broadcasted_iota(jnp.int32, (h, n * h), 0)
    col_blk = lax.shift_right_logical(lane, 4)
    d_tiled = jnp.concatenate([d_ref[...]] * n, axis=1)
    lag = [kw[h * kk:h * (kk + 1)] for kk in range(n)]
    lag[0] = lag[0] + jnp.where((lane & (h - 1)) == row, d_tiled, 0.0)
    rows = []
    for j in range(n):
        acc = jnp.zeros((h, n * h), F32)
        for jp in range(j + 1):
            acc = jnp.where(col_blk == jp, lag[j - jp], acc)
        rows.append(acc)
    lhs_ref[0:n * h, :] = jnp.concatenate(rows, axis=0).astype(lhs_ref.dtype)

    bb_b = jnp.concatenate([-bb_im, bb_re], axis=1)
    win = []
    for j in range(n):
        pr, pi = power(n - 1 - j)
        win.append(jnp.concatenate([pr, pr], axis=1) * bb_a + jnp.concatenate([pi, pi], axis=1) * bb_b)
    lhs_ref[n * h:, :] = jnp.concatenate(win, axis=0).T.astype(lhs_ref.dtype)
    wout_ref[...] = jnp.concatenate(cpw[1:], axis=0).astype(wout_ref.dtype)
    for i, kk in enumerate(_s5_scan_steps(n_top)):
        pr, pi = power(kk)
        ap_ref[i:i + 1, :] = jnp.concatenate([pr, pi], axis=1)


def _s5_coefficients(log_dt, lam_re, lam_im, b_re, b_im, c_re, c_im, d_skip, n_top):
    g, p, h = SSM_GROUPS, SSM_STATE, SSM_GROUP
    n_ap = SUPER + n_top
    gb = SCAN_GROUPS
    grp = lambda *shape: pl.BlockSpec((gb,) + shape, lambda i: (i,) + (0,) * len(shape))
    exps = np.asarray(_s5_exponents(n_top), np.float32)
    kk = np.zeros((-(-exps.size // 8) * 8, 1), np.float32)
    kk[:exps.size, 0] = exps
    return pl.pallas_call(
        functools.partial(_s5_coef_kernel, n_top=n_top),
        out_shape=(jax.ShapeDtypeStruct((g, CHUNK * h + 2 * p, CHUNK * h), BF16),
                   jax.ShapeDtypeStruct((g, CHUNK * h, 2 * p), BF16),
                   jax.ShapeDtypeStruct((g, n_ap, 2 * p), F32)),
        grid=(g // gb,),
        in_specs=[pl.BlockSpec(kk.shape, lambda i: (0, 0)),
                  grp(1, p), grp(1, p), grp(1, 1), grp(p, h), grp(p, h), grp(h, p), grp(h, p), grp(1, h)],
        out_specs=(grp(CHUNK * h + 2 * p, CHUNK * h), grp(CHUNK * h, 2 * p), grp(n_ap, 2 * p)),
        name="s5_coefficients",
    )(jnp.asarray(kk), lam_re.reshape(g, 1, p), lam_im.reshape(g, 1, p), log_dt.reshape(g, 1, 1), b_re, b_im,
      c_re, c_im, d_skip.reshape(g, 1, h))


PHASES = 8


def _ssm_proj_kernel(h_ref, w_ref, u_ref, z_ref, rows_scr, *, n_b):
    for jj in range(PHASES):
        for b in range(n_b):
            rows_scr[b] = h_ref[b, :, jj, :]
        h = jnp.concatenate([rows_scr[b].astype(BF16) for b in range(n_b)], axis=0)
        r = lax.dot_general(w_ref[...], h, _NT, preferred_element_type=F32)
        u_ref[jj] = r[:SSM_WIDTH].astype(BF16)
        z_ref[jj] = _silu(r[SSM_WIDTH:]).astype(BF16)


def _ssm_projection(h4, w_in):
    n_b, c2, _, d = h4.shape
    lanes = n_b * c2
    halves = CHUNK // PHASES
    w_t = w_in[:, :2 * SSM_WIDTH].T.astype(BF16)
    out = jax.ShapeDtypeStruct((SUPER, CHUNK, SSM_WIDTH, lanes), BF16)
    out_spec = pl.BlockSpec((None, PHASES, SSM_WIDTH, lanes), lambda j2, hf: (j2, hf, 0, 0))
    return pl.pallas_call(
        functools.partial(_ssm_proj_kernel, n_b=n_b),
        out_shape=(out, out),
        grid=(SUPER, halves),
        in_specs=[pl.BlockSpec((n_b, c2, PHASES, d), lambda j2, hf: (0, 0, j2 * halves + hf, 0)),
                  pl.BlockSpec(w_t.shape, lambda j2, hf: (0, 0))],
        out_specs=(out_spec, out_spec),
        scratch_shapes=[pltpu.VMEM((n_b, c2, d), F32)],
        compiler_params=pltpu.CompilerParams(dimension_semantics=("arbitrary", "arbitrary"),
                                             vmem_limit_bytes=VMEM_LIMIT),
        name="ssm_projection",
    )(h4, w_t)


def _cmul(ar, ai, xr, xi):
    return ar * xr - ai * xi, ar * xi + ai * xr


SCAN_GROUPS = 8


def _s5_group_kernel(x_ref, lhs_ref, wout_ref, ap_ref, y_ref, *, n_b, c2, n_top):
    p, gb = SSM_STATE, SCAN_GROUPS
    rows = CHUNK * SSM_GROUP
    n = SUPER * n_b * c2
    lanes = n_b * c2

    def x_group(gi):
        return jnp.concatenate([x_ref[j2, :, SSM_GROUP * gi:SSM_GROUP * (gi + 1), :].reshape(rows, lanes)
                                for j2 in range(SUPER)], axis=1)

    r = [jnp.dot(lhs_ref[gi], x_group(gi), preferred_element_type=F32) for gi in range(gb)]
    c_re = jnp.concatenate([r[gi][rows:rows + p] for gi in range(gb)], axis=0)
    c_im = jnp.concatenate([r[gi][rows + p:] for gi in range(gb)], axis=0)

    def mult(i):
        rep = [jnp.broadcast_to(ap_ref[gi, i:i + 1, :], (2 * p, 2 * p)).T for gi in range(gb)]
        return (jnp.concatenate([m[:p, :c2] for m in rep], axis=0),
                jnp.concatenate([m[p:, :c2] for m in rep], axis=0))

    lane = lax.broadcasted_iota(jnp.int32, (gb * p, c2), 1)

    def shifted(a, sh):
        return jnp.where(lane >= sh, pltpu.roll(a, sh, 1), 0.0)

    pieces = [[None] * n_b for _ in range(SUPER)]
    for b in range(n_b):
        piece = lambda a, j2: a[:, (j2 * n_b + b) * c2:(j2 * n_b + b + 1) * c2]
        a_re, a_im = mult(1)
        e_re = jnp.zeros((gb * p, c2), F32)
        e_im = jnp.zeros((gb * p, c2), F32)
        local = []
        for j2 in range(SUPER):
            local.append((e_re, e_im))
            t_re, t_im = _cmul(a_re, a_im, e_re, e_im)
            e_re, e_im = t_re + piece(c_re, j2), t_im + piece(c_im, j2)
        for i in range(n_top):
            m_re, m_im = mult(SUPER + i)
            t_re, t_im = _cmul(m_re, m_im, shifted(e_re, 1 << i), shifted(e_im, 1 << i))
            e_re, e_im = e_re + t_re, e_im + t_im
        s_re, s_im = shifted(e_re, 1), shifted(e_im, 1)
        for j2 in range(SUPER):
            if j2 == 0:
                pieces[j2][b] = (s_re, s_im)
            else:
                m_re, m_im = mult(j2)
                t_re, t_im = _cmul(m_re, m_im, s_re, s_im)
                pieces[j2][b] = (local[j2][0] + t_re, local[j2][1] + t_im)
    for gi in range(gb):
        sl = slice(gi * p, (gi + 1) * p)
        s_in = jnp.concatenate([jnp.concatenate([pieces[j2][b][0][sl], pieces[j2][b][1][sl]], axis=0)
                                for j2 in range(SUPER) for b in range(n_b)], axis=1)
        y = r[gi][:rows] + jnp.dot(wout_ref[gi], s_in.astype(BF16), preferred_element_type=F32)
        for j2 in range(SUPER):
            y_ref[j2, gi] = y[:, j2 * lanes:(j2 + 1) * lanes].reshape(CHUNK, SSM_GROUP, lanes).astype(y_ref.dtype)


def _s5_scan(u_t, lhs, wout, ap, n_b, c2, n_top):
    g, h = SSM_GROUPS, SSM_GROUP
    lanes = u_t.shape[-1]
    gb = SCAN_GROUPS
    grp = lambda *shape: pl.BlockSpec((gb,) + shape, lambda i: (i,) + (0,) * len(shape))
    return pl.pallas_call(
        functools.partial(_s5_group_kernel, n_b=n_b, c2=c2, n_top=n_top),
        out_shape=jax.ShapeDtypeStruct((SUPER, g, CHUNK, h, lanes), BF16),
        grid=(g // gb,),
        in_specs=[pl.BlockSpec((SUPER, CHUNK, gb * h, lanes), lambda i: (0, 0, i, 0)),
                  grp(*lhs.shape[1:]), grp(*wout.shape[1:]), grp(*ap.shape[1:])],
        out_specs=pl.BlockSpec((SUPER, gb, CHUNK, h, lanes), lambda i: (0, i, 0, 0, 0)),
        compiler_params=pltpu.CompilerParams(dimension_semantics=("arbitrary",),
                                             vmem_limit_bytes=VMEM_LIMIT),
        name="s5_chunk_scan",
    )(u_t, lhs, wout, ap)


def _glu_kernel(y_ref, zs_ref, w_ref, b_ref, o_ref, *, n_b, c2):
    g, _, h, n = y_ref.shape
    for jj in range(PHASES):
        y = jax.nn.gelu(y_ref[:, jj].reshape(g * h, n).astype(F32))
        t = jnp.dot(w_ref[...], y.astype(BF16), preferred_element_type=F32) + b_ref[...]
        gated = (y * jax.nn.sigmoid(t) * zs_ref[jj].astype(F32)).T
        for b in range(n_b):
            o_ref[b, :, jj, :] = gated[b * c2:(b + 1) * c2]


def _glu(y_t, zs_t, w_glu, b_glu, n_b, c2):
    _, g, _, h, _ = y_t.shape
    lanes = n_b * c2
    halves = CHUNK // PHASES
    w_t = w_glu.T.astype(BF16)
    return pl.pallas_call(
        functools.partial(_glu_kernel, n_b=n_b, c2=c2),
        out_shape=jax.ShapeDtypeStruct((n_b, c2, SUPER * CHUNK, SSM_WIDTH), F32),
        grid=(SUPER, halves),
        in_specs=[pl.BlockSpec((None, g, PHASES, h, lanes), lambda j2, hf: (j2, 0, hf, 0, 0)),
                  pl.BlockSpec((None, PHASES, SSM_WIDTH, lanes), lambda j2, hf: (j2, hf, 0, 0)),
                  pl.BlockSpec(w_t.shape, lambda j2, hf: (0, 0)),
                  pl.BlockSpec((SSM_WIDTH, 1), lambda j2, hf: (0, 0))],
        out_specs=pl.BlockSpec((n_b, c2, PHASES, SSM_WIDTH), lambda j2, hf: (0, 0, j2 * halves + hf, 0)),
        compiler_params=pltpu.CompilerParams(dimension_semantics=("arbitrary", "arbitrary"),
                                             vmem_limit_bytes=VMEM_LIMIT),
        name="s5_glu",
    )(y_t, zs_t, w_t, b_glu.reshape(SSM_WIDTH, 1))


def _out_proj_kernel(ys_ref, ym_ref, x_ref, mod_ref, wa_ref, wb_ref, o_ref):
    y = (jnp.dot(ys_ref[...].astype(BF16), wa_ref[...], preferred_element_type=F32)
         + lax.dot_general(ym_ref[...], wb_ref[...], _TN, preferred_element_type=F32))
    gate = mod_ref[...][:, 2 * D_MODEL:]
    o_ref[...] = x_ref[...] + gate * y


def _out_projection(ys, ym, x2, mod3, w_out, seq):
    t, d = x2.shape
    tm = min(1024, seq)
    per_b = seq // tm
    wa = w_out[:SSM_WIDTH].astype(BF16)
    wb = w_out[SSM_WIDTH:].astype(BF16)
    tok = lambda w: pl.BlockSpec((tm, w), lambda i: (i, 0))
    return pl.pallas_call(
        _out_proj_kernel,
        out_shape=jax.ShapeDtypeStruct(x2.shape, F32),
        grid=(t // tm,),
        in_specs=[tok(SSM_WIDTH), pl.BlockSpec((MLA_WIDTH, tm), lambda i: (0, i)), tok(d),
                  pl.BlockSpec((None, 1, 3 * d), lambda i: (i // per_b, 0, 0)),
                  pl.BlockSpec(wa.shape, lambda i: (0, 0)),
                  pl.BlockSpec(wb.shape, lambda i: (0, 0))],
        out_specs=tok(d),
        compiler_params=pltpu.CompilerParams(dimension_semantics=("arbitrary",),
                                             vmem_limit_bytes=VMEM_LIMIT),
        name="output_projection",
    )(ys, ym, x2, mod3, wa, wb)


def kernel(x, c, positions, w_ada, b_ada, norm_g, w_in, log_dt, lam_re, lam_im, b_re, b_im, c_re, c_im, d_skip,
           w_glu, b_glu, q_a_g, w_q_b, kv_a_g, w_kv_b, q_norm_g, k_norm_g, w_out):
    n_b, seq, d = x.shape
    depth = w_ada.shape[0]
    c2 = seq // (CHUNK * SUPER)
    n_top = max(int(math.log2(c2)), 0)
    assert c2 * CHUNK * SUPER == seq and (1 << n_top) == c2
    for l in range(depth):
        mod3 = _modulation(c, w_ada[l], b_ada[l]).reshape(n_b, 1, 3 * d)
        x2 = x.reshape(n_b * seq, d)
        lhs, wout, ap = _s5_coefficients(log_dt[l], lam_re[l], lam_im[l], b_re[l], b_im[l], c_re[l], c_im[l],
                                         d_skip[l], n_top)
        qt, k, vt, zm, h = _mla_projection(x2, mod3, norm_g[l], w_in[l], q_a_g[l], w_q_b[l],
                                           kv_a_g[l], w_kv_b[l], q_norm_g[l], k_norm_g[l], positions, n_b, seq)
        score_bound = (1.05 * math.sqrt(QK_HEAD) * math.log2(math.e)
                       * jnp.max(jnp.abs(q_norm_g[l])) * jnp.max(jnp.abs(k_norm_g[l])))
        ym = _attention(qt, k, vt, zm, score_bound)
        ym, h = lax.optimization_barrier((ym, h))
        u_t, zs_t = _ssm_projection(h.reshape(n_b, c2, SUPER * CHUNK, d), w_in[l])
        y_t = _s5_scan(u_t, lhs, wout, ap, n_b, c2, n_top)
        ys = _glu(y_t, zs_t, w_glu[l], b_glu[l], n_b, c2).reshape(n_b * seq, SSM_WIDTH)
        x = _out_projection(ys, ym, x2, mod3, w_out[l], seq).reshape(n_b, seq, d)
    return x
```

```python
import functools
import math

import numpy as np
import jax
import jax.numpy as jnp
from jax import lax
from jax.experimental import pallas as pl
from jax.experimental.pallas import tpu as pltpu

F32 = jnp.float32
BF16 = jnp.bfloat16
HIGHEST = lax.Precision.HIGHEST

D_MODEL = 1024
SSM_WIDTH = 512
SSM_GROUP = 16
SSM_GROUPS = 32
SSM_STATE = 64
MLA_HEADS = 8
QK_NOPE = 64
QK_ROPE = 32
QK_HEAD = QK_NOPE + QK_ROPE
V_HEAD = 64
MLA_WIDTH = MLA_HEADS * V_HEAD
Q_LORA = 384
KV_LORA = 256
ROPE_THETA = 10000.0
EPS = 1e-6
NEG_BIG = -1e30

LANES = 128
VMEM_LIMIT = 60 * 1024 * 1024
CHUNK = 16
SUPER = 4
ROPE_HALF = QK_ROPE // 2

_NT = (((1,), (1,)), ((), ()))
_TN = (((0,), (0,)), ((), ()))


def _silu(v):
    return v * jax.nn.sigmoid(v)


def _rms(v, gain):
    return v * lax.rsqrt(jnp.mean(v * v, axis=-1, keepdims=True) + EPS) * gain


def _mod_kernel(c_ref, w_ref, b_ref, o_ref):
    n_b, d = c_ref.shape
    tn = w_ref.shape[1]
    act_t = _silu(c_ref[...]).T
    out = []
    for b in range(n_b):
        col = jnp.broadcast_to(act_t[:, b:b + 1], (d, LANES))
        out.append(jnp.concatenate([jnp.sum(w_ref[:, j:j + LANES] * col, axis=0, keepdims=True)
                                    for j in range(0, tn, LANES)], axis=1))
    o_ref[...] = jnp.concatenate(out, axis=0) + b_ref[...]


def _modulation(c, w, b):
    n_b, d = c.shape
    tn = 512
    return pl.pallas_call(
        _mod_kernel,
        out_shape=jax.ShapeDtypeStruct((n_b, 3 * d), F32),
        grid=(3 * d // tn,),
        in_specs=[pl.BlockSpec((n_b, d), lambda n: (0, 0)),
                  pl.BlockSpec((d, tn), lambda n: (0, n)),
                  pl.BlockSpec((1, tn), lambda n: (0, n))],
        out_specs=pl.BlockSpec((n_b, tn), lambda n: (0, n)),
        name="adaln_modulation",
    )(c, w, b.reshape(1, -1))


def _rope_frequencies():
    inv_freq = ROPE_THETA ** (-np.arange(ROPE_HALF, dtype=np.float64) * 2.0 / QK_ROPE)
    return jnp.asarray(inv_freq.astype(np.float32)).reshape(ROPE_HALF, 1)


def _slab_source():
    src = np.full((LANES,), -1, np.int64)
    src[:QK_HEAD] = np.arange(QK_HEAD)
    return src


def _take_static(w, src):
    parts, i = [], 0
    while i < len(src):
        j = i + 1
        while j < len(src) and (src[j] < 0 if src[i] < 0 else src[j] == src[j - 1] + 1):
            j += 1
        parts.append(jnp.zeros(w.shape[:-1] + (j - i,), w.dtype) if src[i] < 0
                     else w[..., int(src[i]):int(src[i]) + j - i])
        i = j
    return jnp.concatenate(parts, axis=-1)


def _to_slabs(w, per_head, src):
    k = w.shape[0]
    return _take_static(w.reshape(k, MLA_HEADS, per_head), src).reshape(k, MLA_HEADS * LANES)


V_ROWS = V_HEAD + 16


def _mla_proj_kernel(x_ref, mod_ref, ng_ref, w1_ref, wzt_ref, qag_ref, kvag_ref, wqt_ref, wkt_ref, wvt_ref,
                     gq_ref, gk_ref, pos_ref, freq_ref, qt_ref, k_ref, vt_ref, zs_ref, h_ref):
    d = D_MODEL
    tm = x_ref.shape[0]
    x = x_ref[...]
    mod = mod_ref[...]
    shift, scale = mod[:, :d], mod[:, d:2 * d]
    h_ref[...] = _rms(x, ng_ref[...] * (1.0 + scale)) + shift
    h = h_ref[...].astype(BF16)
    p1 = jnp.dot(h, w1_ref[...], preferred_element_type=F32)
    o1, o2 = Q_LORA, Q_LORA + KV_LORA
    cqn = _rms(p1[:, :o1], qag_ref[...]).astype(BF16)
    ckvn = _rms(p1[:, o1:o2], kvag_ref[...]).astype(BF16)
    zs_ref[...] = _silu(lax.dot_general(wzt_ref[...], h, _NT, preferred_element_type=F32)).astype(BF16)

    qt = lax.dot_general(wqt_ref[...], cqn, _NT, preferred_element_type=F32)
    kt = lax.dot_general(wkt_ref[...], ckvn, _NT, preferred_element_type=F32)
    krt = p1[:, o2:].T
    vt = lax.dot_general(wvt_ref[...], ckvn, _NT, preferred_element_type=F32)
    ang = freq_ref[...] * pos_ref[...].astype(F32)
    cos_t, sin_t = jnp.cos(ang), jnp.sin(ang)
    gq = jnp.concatenate([gq_ref[...]] * (tm // LANES), axis=1)
    gk = jnp.concatenate([gk_ref[...]] * (tm // LANES), axis=1)
    first = lax.broadcasted_iota(jnp.int32, (V_ROWS - V_HEAD, tm), 0) == 0
    ones_rows = jnp.where(first, 1.0, 0.0).astype(BF16)
    r, lo = ROPE_HALF, QK_NOPE

    def norm_rope(slab, gain):
        ss = jnp.sum(slab * slab, axis=0, keepdims=True)
        n = slab * lax.rsqrt(ss * (1.0 / QK_HEAD) + EPS) * gain
        x1, x2 = n[lo:lo + r], n[lo + r:lo + 2 * r]
        return jnp.concatenate([n[:lo], x1 * cos_t - x2 * sin_t, x2 * cos_t + x1 * sin_t, n[lo + 2 * r:]], axis=0)

    for head in range(MLA_HEADS):
        rows = slice(LANES * head, LANES * (head + 1))
        qt_ref[head] = norm_rope(qt[rows], gq).astype(BF16)
        k_ref[head] = norm_rope(kt[rows] + krt, gk).T.astype(BF16)
        vt_ref[head, 0:V_HEAD, :] = vt[V_HEAD * head:V_HEAD * (head + 1)].astype(BF16)
        vt_ref[head, V_HEAD:, :] = ones_rows


def _mla_projection(x2, mod3, norm_g, w_in, q_a_g, w_q_b, kv_a_g, w_kv_b, q_norm_g, k_norm_g, positions, n_b, seq):
    t, d = x2.shape
    tm = min(512, seq)
    per_b = seq // tm
    o2 = 2 * SSM_WIDTH
    o4, o5 = o2 + Q_LORA + KV_LORA, o2 + Q_LORA + KV_LORA + QK_ROPE
    src = _slab_source()
    rope_src = np.where(src >= QK_NOPE, src - QK_NOPE, -1)
    nope_src = np.where((src >= 0) & (src < QK_NOPE), src, -1)
    wkr = _take_static(w_in[:, o4:o5], rope_src)
    w1 = jnp.concatenate([w_in[:, o2:o4], wkr], axis=1).astype(BF16)
    wzt = w_in[:, o5:].T.astype(BF16)
    wqt = _to_slabs(w_q_b, QK_HEAD, src).T.astype(BF16)
    wkt = _to_slabs(w_kv_b, QK_NOPE + V_HEAD, nope_src).T.astype(BF16)
    wvt = w_kv_b.reshape(KV_LORA, MLA_HEADS, QK_NOPE + V_HEAD)[:, :, QK_NOPE:].reshape(KV_LORA, MLA_WIDTH).T.astype(BF16)
    q_scale = math.log2(math.e) / math.sqrt(QK_HEAD)
    slab_gain = lambda g: jnp.broadcast_to(_take_static(g, src)[:, None], (LANES, LANES))
    gq = slab_gain(q_norm_g * q_scale)
    gk = slab_gain(k_norm_g)
    const = lambda shape: pl.BlockSpec(shape, lambda i: (0,) * len(shape))
    tok = lambda i: (i // per_b, 0, 0, i % per_b)
    return pl.pallas_call(
        _mla_proj_kernel,
        out_shape=(jax.ShapeDtypeStruct((n_b, MLA_HEADS, LANES, seq), BF16),
                   jax.ShapeDtypeStruct((n_b, MLA_HEADS, seq, LANES), BF16),
                   jax.ShapeDtypeStruct((n_b, MLA_HEADS, V_ROWS, seq), BF16),
                   jax.ShapeDtypeStruct((MLA_WIDTH, t), BF16),
                   jax.ShapeDtypeStruct((t, d), F32)),
        grid=(t // tm,),
        in_specs=[pl.BlockSpec((tm, d), lambda i: (i, 0)),
                  pl.BlockSpec((None, 1, 3 * d), lambda i: (i // per_b, 0, 0)),
                  const((1, d)), const(w1.shape), const(wzt.shape), const((1, Q_LORA)), const((1, KV_LORA)),
                  const(wqt.shape), const(wkt.shape), const(wvt.shape),
                  const((LANES, LANES)), const((LANES, LANES)),
                  pl.BlockSpec((1, tm), lambda i: (0, i)),
                  const((ROPE_HALF, 1))],
        out_specs=(pl.BlockSpec((None, MLA_HEADS, LANES, tm), tok),
                   pl.BlockSpec((None, MLA_HEADS, tm, LANES), lambda i: (i // per_b, 0, i % per_b, 0)),
                   pl.BlockSpec((None, MLA_HEADS, V_ROWS, tm), tok),
                   pl.BlockSpec((MLA_WIDTH, tm), lambda i: (0, i)),
                   pl.BlockSpec((tm, d), lambda i: (i, 0))),
        compiler_params=pltpu.CompilerParams(dimension_semantics=("arbitrary",),
                                             vmem_limit_bytes=VMEM_LIMIT),
        name="mla_projection",
    )(x2, mod3, norm_g.reshape(1, d), w1, wzt, q_a_g.reshape(1, -1), kv_a_g.reshape(1, -1), wqt, wkt, wvt, gq, gk,
      positions.reshape(1, t), _rope_frequencies())


ATTN_TQ = 2048
ATTN_TK = 256
HEADS_PER_STEP = 2


def _attn_kernel(qt_ref, k_ref, vt_ref, zs_ref, o_ref, s_scr):
    tq, tk = ATTN_TQ, ATTN_TK
    r = tq // tk
    assert r * tk == tq and r % 2 == 0
    qi = pl.program_id(2)

    def scores(slot, t, lo=0):
        start = pl.multiple_of(t * tk, tk)
        for e in range(HEADS_PER_STEP):
            s_scr[slot, e, :, lo:] = jnp.dot(k_ref[e, pl.ds(start, tk), :], qt_ref[e, :, lo:],
                                             preferred_element_type=F32)

    def process(slot, t, carry, lo=0, diagonal=False):
        start = pl.multiple_of(t * tk, tk)
        out = []
        for e in range(HEADS_PER_STEP):
            m, acc = carry[e]
            s = s_scr[slot, e, :, lo:]
            if diagonal:
                blk = s[:, :tk]
                ok = lax.broadcasted_iota(jnp.int32, blk.shape, 0) <= lax.broadcasted_iota(jnp.int32, blk.shape, 1)
                blk = jnp.where(ok, blk, NEG_BIG)
                s = jnp.concatenate([blk, s[:, tk:]], axis=1) if s.shape[1] > tk else blk
            m_old = m[:, lo:]
            m_new = jnp.maximum(m_old, jnp.max(s, axis=0, keepdims=True))
            p = jnp.exp2(s - m_new).astype(BF16)
            acc_new = (jnp.exp2(m_old - m_new) * acc[:, lo:]
                       + jnp.dot(vt_ref[e, :, pl.ds(start, tk)], p, preferred_element_type=F32))
            if lo:
                m_new = jnp.concatenate([m[:, :lo], m_new], axis=1)
                acc_new = jnp.concatenate([acc[:, :lo], acc_new], axis=1)
            out.append((m_new, acc_new))
        return tuple(out)

    def body(i, carry):
        base = i * r
        for u in range(r):
            scores((u + 1) % 2, base + u + 1)
            carry = process(u % 2, base + u, carry)
        return carry

    init = tuple((jnp.full((1, tq), NEG_BIG, F32), jnp.zeros((V_ROWS, tq), F32)) for _ in range(HEADS_PER_STEP))
    scores(0, 0)
    carry = lax.fori_loop(0, qi, body, init)
    base = qi * r
    for u in range(r):
        if u + 1 < r:
            scores((u + 1) % 2, base + u + 1, lo=(u + 1) * tk)
        carry = process(u % 2, base + u, carry, lo=u * tk, diagonal=True)
    _attn_finish(tuple(c[1] for c in carry), zs_ref, o_ref)


def _attn_finish(accs, zs_ref, o_ref):
    outs = [acc[:V_HEAD] / acc[V_HEAD:V_HEAD + 1] for acc in accs]
    o_ref[...] = (jnp.concatenate(outs, axis=0) * zs_ref[...].astype(F32)).astype(o_ref.dtype)


def _attn_bounded_kernel(qt_ref, k_ref, vt_ref, zs_ref, o_ref, p_scr, acc_scr):
    tq, tk = ATTN_TQ, ATTN_TK
    r = tq // tk
    assert r * tk == tq and r % 2 == 0
    qi = pl.program_id(2)
    q0 = qi * tq

    def probs(slot, t, lo=0, may_cross=False):
        start = pl.multiple_of(t * tk, tk)
        for e in range(HEADS_PER_STEP):
            p = jnp.exp2(jnp.dot(k_ref[e, pl.ds(start, tk), :], qt_ref[e, :, lo:], preferred_element_type=F32))
            if may_cross:
                blk = p[:, :tk]
                key = start + lax.broadcasted_iota(jnp.int32, blk.shape, 0)
                qry = q0 + lo + lax.broadcasted_iota(jnp.int32, blk.shape, 1)
                blk = jnp.where(key <= qry, blk, 0.0)
                p = jnp.concatenate([blk, p[:, tk:]], axis=1) if p.shape[1] > tk else blk
            p_scr[slot, e, :, lo:] = p.astype(BF16)

    def accumulate(slot, t, lo=0):
        start = pl.multiple_of(t * tk, tk)
        for e in range(HEADS_PER_STEP):
            acc_scr[e, :, lo:] += jnp.dot(vt_ref[e, :, pl.ds(start, tk)], p_scr[slot, e, :, lo:],
                                          preferred_element_type=F32)

    def body(i, carry):
        base = i * r
        for u in range(r):
            probs((u + 1) % 2, base + u + 1, may_cross=(u == r - 1))
            accumulate(u % 2, base + u)
        return carry

    acc_scr[...] = jnp.zeros_like(acc_scr)
    probs(0, 0, may_cross=True)
    lax.fori_loop(0, qi, body, 0)
    base = qi * r
    for u in range(r):
        if u + 1 < r:
            probs((u + 1) % 2, base + u + 1, lo=(u + 1) * tk, may_cross=True)
        accumulate(u % 2, base + u, lo=u * tk)
    _attn_finish(tuple(acc_scr[e] for e in range(HEADS_PER_STEP)), zs_ref, o_ref)


SCORE_BOUND_LOG2 = 60.0


def _attention(qt, k, vt, zs, score_bound):
    n_b, n_h, seq, _ = k.shape
    tq = ATTN_TQ
    assert seq % tq == 0 and n_h % HEADS_PER_STEP == 0
    nq = seq // tq
    hp = HEADS_PER_STEP
    width = hp * V_HEAD

    def call(body, scratch):
        return pl.pallas_call(
            body,
            out_shape=jax.ShapeDtypeStruct(zs.shape, BF16),
            grid=(n_b, n_h // hp, nq),
            in_specs=[pl.BlockSpec((None, hp, LANES, tq), lambda b, h, i: (b, h, 0, i)),
                      pl.BlockSpec((None, hp, seq, LANES), lambda b, h, i: (b, h, 0, 0)),
                      pl.BlockSpec((None, hp, V_ROWS, seq), lambda b, h, i: (b, h, 0, 0)),
                      pl.BlockSpec((width, tq), lambda b, h, i: (h, b * nq + i))],
            out_specs=pl.BlockSpec((width, tq), lambda b, h, i: (h, b * nq + i)),
            scratch_shapes=scratch,
            compiler_params=pltpu.CompilerParams(dimension_semantics=("arbitrary", "arbitrary", "arbitrary"),
                                                 vmem_limit_bytes=VMEM_LIMIT),
            name="causal_attention",
        )(qt, k, vt, zs)

    tiles = (2, hp, ATTN_TK, tq)
    return lax.cond(score_bound <= SCORE_BOUND_LOG2,
                    lambda: call(_attn_bounded_kernel, [pltpu.VMEM(tiles, BF16), pltpu.VMEM((hp, V_ROWS, tq), F32)]),
                    lambda: call(_attn_kernel, [pltpu.VMEM(tiles, F32)]))


def _s5_scan_steps(n_top):
    return [CHUNK * m for m in range(SUPER)] + [CHUNK * SUPER * (1 << i) for i in range(n_top)]


def _s5_exponents(n_top):
    return sorted(set(range(CHUNK + 1)) | set(_s5_scan_steps(n_top)))


def _s5_coef_kernel(kk_ref, *refs, n_top):
    for gi in range(SCAN_GROUPS):
        _s5_coef_one_group(kk_ref, *(r.at[gi] for r in refs), n_top=n_top)


def _s5_coef_one_group(kk_ref, lre_ref, lim_ref, ldt_ref, bre_ref, bim_ref, cre_ref, cim_ref, d_ref,
                       lhs_ref, wout_ref, ap_ref, *, n_top):
    h, p, n = SSM_GROUP, SSM_STATE, CHUNK
    lre, lim = lre_ref[...], lim_ref[...]
    dt = jnp.exp(ldt_ref[...])
    exps = _s5_exponents(n_top)
    kcol = kk_ref[...]
    mag = jnp.exp(kcol * (lre * dt))
    ang = kcol * (lim * dt)
    pow_re, pow_im = mag * jnp.cos(ang), mag * jnp.sin(ang)

    def power(kk):
        i = exps.index(kk)
        return pow_re[i:i + 1], pow_im[i:i + 1]

    lb_re, lb_im = power(1)
    nr, ni = lb_re - 1.0, lb_im
    den = lre * lre + lim * lim
    f_re = (nr * lre + ni * lim) / den
    f_im = (ni * lre - nr * lim) / den
    bre, bim = bre_ref[...].T, bim_ref[...].T
    bb_re = f_re * bre - f_im * bim
    bb_im = f_re * bim + f_im * bre
    cre, cim = cre_ref[...], cim_ref[...]
    cp_re, cp_im = [], []
    for kk in range(n + 1):
        pr, pi = power(kk)
        cp_re.append(cre * pr - cim * pi)
        cp_im.append(cre * pi + cim * pr)

    cpw = [jnp.concatenate([cp_re[kk], -cp_im[kk]], axis=1) for kk in range(n + 1)]
    bb_a = jnp.concatenate([bb_re, bb_im], axis=1)
    kw = lax.dot_general(jnp.concatenate(cpw[:n], axis=0), jnp.concatenate([bb_a] * n, axis=0), _NT,
                         preferred_element_type=F32, precision=HIGHEST)
    lane = lax.broadcasted_iota(jnp.int32, (h, n * h), 1)
    row = lax.broadcasted_iota(jnp.int32, (h, n * h), 0)
    col_blk = lax.shift_right_logical(lane, 4)
    d_tiled = jnp.concatenate([d_ref[...]] * n, axis=1)
    lag = [kw[h * kk:h * (kk + 1)] for kk in range(n)]
    lag[0] = lag[0] + jnp.where((lane & (h - 1)) == row, d_tiled, 0.0)
    rows = []
    for j in range(n):
        acc = jnp.zeros((h, n * h), F32)
        for jp in range(j + 1):
            acc = jnp.where(col_blk == jp, lag[j - jp], acc)
        rows.append(acc)
    lhs_ref[0:n * h, :] = jnp.concatenate(rows, axis=0).astype(lhs_ref.dtype)

    bb_b = jnp.concatenate([-bb_im, bb_re], axis=1)
    win = []
    for j in range(n):
        pr, pi = power(n - 1 - j)
        win.append(jnp.concatenate([pr, pr], axis=1) * bb_a + jnp.concatenate([pi, pi], axis=1) * bb_b)
    lhs_ref[n * h:, :] = jnp.concatenate(win, axis=0).T.astype(lhs_ref.dtype)
    wout_ref[...] = jnp.concatenate(cpw[1:], axis=0).astype(wout_ref.dtype)
    for i, kk in enumerate(_s5_scan_steps(n_top)):
        pr, pi = power(kk)
        ap_ref[i:i + 1, :] = jnp.concatenate([pr, pi], axis=1)


def _s5_coefficients(log_dt, lam_re, lam_im, b_re, b_im, c_re, c_im, d_skip, n_top):
    g, p, h = SSM_GROUPS, SSM_STATE, SSM_GROUP
    n_ap = SUPER + n_top
    gb = SCAN_GROUPS
    grp = lambda *shape: pl.BlockSpec((gb,) + shape, lambda i: (i,) + (0,) * len(shape))
    exps = np.asarray(_s5_exponents(n_top), np.float32)
    kk = np.zeros((-(-exps.size // 8) * 8, 1), np.float32)
    kk[:exps.size, 0] = exps
    return pl.pallas_call(
        functools.partial(_s5_coef_kernel, n_top=n_top),
        out_shape=(jax.ShapeDtypeStruct((g, CHUNK * h + 2 * p, CHUNK * h), BF16),
                   jax.ShapeDtypeStruct((g, CHUNK * h, 2 * p), BF16),
                   jax.ShapeDtypeStruct((g, n_ap, 2 * p), F32)),
        grid=(g // gb,),
        in_specs=[pl.BlockSpec(kk.shape, lambda i: (0, 0)),
                  grp(1, p), grp(1, p), grp(1, 1), grp(p, h), grp(p, h), grp(h, p), grp(h, p), grp(1, h)],
        out_specs=(grp(CHUNK * h + 2 * p, CHUNK * h), grp(CHUNK * h, 2 * p), grp(n_ap, 2 * p)),
        name="s5_coefficients",
    )(jnp.asarray(kk), lam_re.reshape(g, 1, p), lam_im.reshape(g, 1, p), log_dt.reshape(g, 1, 1), b_re, b_im,
      c_re, c_im, d_skip.reshape(g, 1, h))


PHASES = 8


def _ssm_proj_kernel(h_ref, w_ref, u_ref, z_ref, rows_scr, *, n_b):
    for jj in range(PHASES):
        for b in range(n_b):
            rows_scr[b] = h_ref[b, :, jj, :]
        h = jnp.concatenate([rows_scr[b].astype(BF16) for b in range(n_b)], axis=0)
        r = lax.dot_general(w_ref[...], h, _NT, preferred_element_type=F32)
        u_ref[jj] = r[:SSM_WIDTH].astype(BF16)
        z_ref[jj] = _silu(r[SSM_WIDTH:]).astype(BF16)


def _ssm_projection(h4, w_in):
    n_b, c2, _, d = h4.shape
    lanes = n_b * c2
    halves = CHUNK // PHASES
    w_t = w_in[:, :2 * SSM_WIDTH].T.astype(BF16)
    out = jax.ShapeDtypeStruct((SUPER, CHUNK, SSM_WIDTH, lanes), BF16)
    out_spec = pl.BlockSpec((None, PHASES, SSM_WIDTH, lanes), lambda j2, hf: (j2, hf, 0, 0))
    return pl.pallas_call(
        functools.partial(_ssm_proj_kernel, n_b=n_b),
        out_shape=(out, out),
        grid=(SUPER, halves),
        in_specs=[pl.BlockSpec((n_b, c2, PHASES, d), lambda j2, hf: (0, 0, j2 * halves + hf, 0)),
                  pl.BlockSpec(w_t.shape, lambda j2, hf: (0, 0))],
        out_specs=(out_spec, out_spec),
        scratch_shapes=[pltpu.VMEM((n_b, c2, d), F32)],
        compiler_params=pltpu.CompilerParams(dimension_semantics=("arbitrary", "arbitrary"),
                                             vmem_limit_bytes=VMEM_LIMIT),
        name="ssm_projection",
    )(h4, w_t)


def _cmul(ar, ai, xr, xi):
    return ar * xr - ai * xi, ar * xi + ai * xr


SCAN_GROUPS = 8


def _s5_group_kernel(x_ref, lhs_ref, wout_ref, ap_ref, y_ref, *, n_b, c2, n_top):
    p, gb = SSM_STATE, SCAN_GROUPS
    rows = CHUNK * SSM_GROUP
    n = SUPER * n_b * c2
    lanes = n_b * c2

    def x_group(gi):
        return jnp.concatenate([x_ref[j2, :, SSM_GROUP * gi:SSM_GROUP * (gi + 1), :].reshape(rows, lanes)
                                for j2 in range(SUPER)], axis=1)

    r = [jnp.dot(lhs_ref[gi], x_group(gi), preferred_element_type=F32) for gi in range(gb)]
    c_re = jnp.concatenate([r[gi][rows:rows + p] for gi in range(gb)], axis=0)
    c_im = jnp.concatenate([r[gi][rows + p:] for gi in range(gb)], axis=0)

    def mult(i):
        rep = [jnp.broadcast_to(ap_ref[gi, i:i + 1, :], (2 * p, 2 * p)).T for gi in range(gb)]
        return (jnp.concatenate([m[:p, :c2] for m in rep], axis=0),
                jnp.concatenate([m[p:, :c2] for m in rep], axis=0))

    lane = lax.broadcasted_iota(jnp.int32, (gb * p, c2), 1)

    def shifted(a, sh):
        return jnp.where(lane >= sh, pltpu.roll(a, sh, 1), 0.0)

    pieces = [[None] * n_b for _ in range(SUPER)]
    for b in range(n_b):
        piece = lambda a, j2: a[:, (j2 * n_b + b) * c2:(j2 * n_b + b + 1) * c2]
        a_re, a_im = mult(1)
        e_re = jnp.zeros((gb * p, c2), F32)
        e_im = jnp.zeros((gb * p, c2), F32)
        local = []
        for j2 in range(SUPER):
            local.append((e_re, e_im))
            t_re, t_im = _cmul(a_re, a_im, e_re, e_im)
            e_re, e_im = t_re + piece(c_re, j2), t_im + piece(c_im, j2)
        for i in range(n_top):
            m_re, m_im = mult(SUPER + i)
            t_re, t_im = _cmul(m_re, m_im, shifted(e_re, 1 << i), shifted(e_im, 1 << i))
            e_re, e_im = e_re + t_re, e_im + t_im
        s_re, s_im = shifted(e_re, 1), shifted(e_im, 1)
        for j2 in range(SUPER):
            if j2 == 0:
                pieces[j2][b] = (s_re, s_im)
            else:
                m_re, m_im = mult(j2)
                t_re, t_im = _cmul(m_re, m_im, s_re, s_im)
                pieces[j2][b] = (local[j2][0] + t_re, local[j2][1] + t_im)
    for gi in range(gb):
        sl = slice(gi * p, (gi + 1) * p)
        s_in = jnp.concatenate([jnp.concatenate([pieces[j2][b][0][sl], pieces[j2][b][1][sl]], axis=0)
                                for j2 in range(SUPER) for b in range(n_b)], axis=1)
        y = r[gi][:rows] + jnp.dot(wout_ref[gi], s_in.astype(BF16), preferred_element_type=F32)
        for j2 in range(SUPER):
            y_ref[j2, gi] = y[:, j2 * lanes:(j2 + 1) * lanes].reshape(CHUNK, SSM_GROUP, lanes).astype(y_ref.dtype)


def _s5_scan(u_t, lhs, wout, ap, n_b, c2, n_top):
    g, h = SSM_GROUPS, SSM_GROUP
    lanes = u_t.shape[-1]
    gb = SCAN_GROUPS
    grp = lambda *shape: pl.BlockSpec((gb,) + shape, lambda i: (i,) + (0,) * len(shape))
    return pl.pallas_call(
        functools.partial(_s5_group_kernel, n_b=n_b, c2=c2, n_top=n_top),
        out_shape=jax.ShapeDtypeStruct((SUPER, g, CHUNK, h, lanes), BF16),
        grid=(g // gb,),
        in_specs=[pl.BlockSpec((SUPER, CHUNK, gb * h, lanes), lambda i: (0, 0, i, 0)),
                  grp(*lhs.shape[1:]), grp(*wout.shape[1:]), grp(*ap.shape[1:])],
        out_specs=pl.BlockSpec((SUPER, gb, CHUNK, h, lanes), lambda i: (0, i, 0, 0, 0)),
        compiler_params=pltpu.CompilerParams(dimension_semantics=("arbitrary",),
                                             vmem_limit_bytes=VMEM_LIMIT),
        name="s5_chunk_scan",
    )(u_t, lhs, wout, ap)


def _glu_kernel(y_ref, zs_ref, w_ref, b_ref, o_ref, *, n_b, c2):
    g, _, h, n = y_ref.shape
    for jj in range(PHASES):
        y = jax.nn.gelu(y_ref[:, jj].reshape(g * h, n).astype(F32))
        t = jnp.dot(w_ref[...], y.astype(BF16), preferred_element_type=F32) + b_ref[...]
        gated = (y * jax.nn.sigmoid(t) * zs_ref[jj].astype(F32)).T
        for b in range(n_b):
            o_ref[b, :, jj, :] = gated[b * c2:(b + 1) * c2]


def _glu(y_t, zs_t, w_glu, b_glu, n_b, c2):
    _, g, _, h, _ = y_t.shape
    lanes = n_b * c2
    halves = CHUNK // PHASES
    w_t = w_glu.T.astype(BF16)
    return pl.pallas_call(
        functools.partial(_glu_kernel, n_b=n_b, c2=c2),
        out_shape=jax.ShapeDtypeStruct((n_b, c2, SUPER * CHUNK, SSM_WIDTH), F32),
        grid=(SUPER, halves),
        in_specs=[pl.BlockSpec((None, g, PHASES, h, lanes), lambda j2, hf: (j2, 0, hf, 0, 0)),
                  pl.BlockSpec((None, PHASES, SSM_WIDTH, lanes), lambda j2, hf: (j2, hf, 0, 0)),
                  pl.BlockSpec(w_t.shape, lambda j2, hf: (0, 0)),
                  pl.BlockSpec((SSM_WIDTH, 1), lambda j2, hf: (0, 0))],
        out_specs=pl.BlockSpec((n_b, c2, PHASES, SSM_WIDTH), lambda j2, hf: (0, 0, j2 * halves + hf, 0)),
        compiler_params=pltpu.CompilerParams(dimension_semantics=("arbitrary", "arbitrary"),
                                             vmem_limit_bytes=VMEM_LIMIT),
        name="s5_glu",
    )(y_t, zs_t, w_t, b_glu.reshape(SSM_WIDTH, 1))


def _out_proj_kernel(ys_ref, ym_ref, x_ref, mod_ref, wa_ref, wb_ref, o_ref):
    y = (jnp.dot(ys_ref[...].astype(BF16), wa_ref[...], preferred_element_type=F32)
         + lax.dot_general(ym_ref[...], wb_ref[...], _TN, preferred_element_type=F32))
    gate = mod_ref[...][:, 2 * D_MODEL:]
    o_ref[...] = x_ref[...] + gate * y


def _out_projection(ys, ym, x2, mod3, w_out, seq):
    t, d = x2.shape
    tm = min(1024, seq)
    per_b = seq // tm
    wa = w_out[:SSM_WIDTH].astype(BF16)
    wb = w_out[SSM_WIDTH:].astype(BF16)
    tok = lambda w: pl.BlockSpec((tm, w), lambda i: (i, 0))
    return pl.pallas_call(
        _out_proj_kernel,
        out_shape=jax.ShapeDtypeStruct(x2.shape, F32),
        grid=(t // tm,),
        in_specs=[tok(SSM_WIDTH), pl.BlockSpec((MLA_WIDTH, tm), lambda i: (0, i)), tok(d),
                  pl.BlockSpec((None, 1, 3 * d), lambda i: (i // per_b, 0, 0)),
                  pl.BlockSpec(wa.shape, lambda i: (0, 0)),
                  pl.BlockSpec(wb.shape, lambda i: (0, 0))],
        out_specs=tok(d),
        compiler_params=pltpu.CompilerParams(dimension_semantics=("arbitrary",),
                                             vmem_limit_bytes=VMEM_LIMIT),
        name="output_projection",
    )(ys, ym, x2, mod3, wa, wb)


def kernel(x, c, positions, w_ada, b_ada, norm_g, w_in, log_dt, lam_re, lam_im, b_re, b_im, c_re, c_im, d_skip,
           w_glu, b_glu, q_a_g, w_q_b, kv_a_g, w_kv_b, q_norm_g, k_norm_g, w_out):
    n_b, seq, d = x.shape
    depth = w_ada.shape[0]
    c2 = seq // (CHUNK * SUPER)
    n_top = max(int(math.log2(c2)), 0)
    assert c2 * CHUNK * SUPER == seq and (1 << n_top) == c2
    for l in range(depth):
        mod3 = _modulation(c, w_ada[l], b_ada[l]).reshape(n_b, 1, 3 * d)
        x2 = x.reshape(n_b * seq, d)
        lhs, wout, ap = _s5_coefficients(log_dt[l], lam_re[l], lam_im[l], b_re[l], b_im[l], c_re[l], c_im[l],
                                         d_skip[l], n_top)
        qt, k, vt, zm, h = _mla_projection(x2, mod3, norm_g[l], w_in[l], q_a_g[l], w_q_b[l],
                                           kv_a_g[l], w_kv_b[l], q_norm_g[l], k_norm_g[l], positions, n_b, seq)
        score_bound = (1.05 * math.sqrt(QK_HEAD) * math.log2(math.e)
                       * jnp.max(jnp.abs(q_norm_g[l])) * jnp.max(jnp.abs(k_norm_g[l])))
        ym = _attention(qt, k, vt, zm, score_bound)
        ym, h = lax.optimization_barrier((ym, h))
        u_t, zs_t = _ssm_projection(h.reshape(n_b, c2, SUPER * CHUNK, d), w_in[l])
        y_t = _s5_scan(u_t, lhs, wout, ap, n_b, c2, n_top)
        ys = _glu(y_t, zs_t, w_glu[l], b_glu[l], n_b, c2).reshape(n_b * seq, SSM_WIDTH)
        x = _out_projection(ys, ym, x2, mod3, w_out[l], seq).reshape(n_b, seq, d)
    return x
```

```python
import functools
import math

import numpy as np
import jax
import jax.numpy as jnp
from jax import lax
from jax.experimental import pallas as pl
from jax.experimental.pallas import tpu as pltpu

F32 = jnp.float32
BF16 = jnp.bfloat16
HIGHEST = lax.Precision.HIGHEST

D_MODEL = 1024
SSM_WIDTH = 512
SSM_GROUP = 16
SSM_GROUPS = 32
SSM_STATE = 64
MLA_HEADS = 8
QK_NOPE = 64
QK_ROPE = 32
QK_HEAD = QK_NOPE + QK_ROPE
V_HEAD = 64
MLA_WIDTH = MLA_HEADS * V_HEAD
Q_LORA = 384
KV_LORA = 256
ROPE_THETA = 10000.0
EPS = 1e-6
NEG_BIG = -1e30

LANES = 128
VMEM_LIMIT = 60 * 1024 * 1024
CHUNK = 16
SUPER = 4
ROPE_HALF = QK_ROPE // 2

_NT = (((1,), (1,)), ((), ()))
_TN = (((0,), (0,)), ((), ()))


def _silu(v):
    return v * jax.nn.sigmoid(v)


def _rms(v, gain):
    return v * lax.rsqrt(jnp.mean(v * v, axis=-1, keepdims=True) + EPS) * gain


def _mod_kernel(c_ref, w_ref, b_ref, o_ref):
    n_b, d = c_ref.shape
    tn = w_ref.shape[1]
    act_t = _silu(c_ref[...]).T
    out = []
    for b in range(n_b):
        col = jnp.broadcast_to(act_t[:, b:b + 1], (d, LANES))
        out.append(jnp.concatenate([jnp.sum(w_ref[:, j:j + LANES] * col, axis=0, keepdims=True)
                                    for j in range(0, tn, LANES)], axis=1))
    o_ref[...] = jnp.concatenate(out, axis=0) + b_ref[...]


def _modulation(c, w, b):
    n_b, d = c.shape
    tn = 512
    return pl.pallas_call(
        _mod_kernel,
        out_shape=jax.ShapeDtypeStruct((n_b, 3 * d), F32),
        grid=(3 * d // tn,),
        in_specs=[pl.BlockSpec((n_b, d), lambda n: (0, 0)),
                  pl.BlockSpec((d, tn), lambda n: (0, n)),
                  pl.BlockSpec((1, tn), lambda n: (0, n))],
        out_specs=pl.BlockSpec((n_b, tn), lambda n: (0, n)),
        name="adaln_modulation",
    )(c, w, b.reshape(1, -1))


def _rope_frequencies():
    inv_freq = ROPE_THETA ** (-np.arange(ROPE_HALF, dtype=np.float64) * 2.0 / QK_ROPE)
    return jnp.asarray(inv_freq.astype(np.float32)).reshape(ROPE_HALF, 1)


def _slab_source():
    src = np.full((LANES,), -1, np.int64)
    src[:QK_HEAD] = np.arange(QK_HEAD)
    return src


def _take_static(w, src):
    parts, i = [], 0
    while i < len(src):
        j = i + 1
        while j < len(src) and (src[j] < 0 if src[i] < 0 else src[j] == src[j - 1] + 1):
            j += 1
        parts.append(jnp.zeros(w.shape[:-1] + (j - i,), w.dtype) if src[i] < 0
                     else w[..., int(src[i]):int(src[i]) + j - i])
        i = j
    return jnp.concatenate(parts, axis=-1)


def _to_slabs(w, per_head, src):
    k = w.shape[0]
    return _take_static(w.reshape(k, MLA_HEADS, per_head), src).reshape(k, MLA_HEADS * LANES)


V_ROWS = V_HEAD + 16


def _mla_proj_kernel(x_ref, mod_ref, ng_ref, w1_ref, wzt_ref, qag_ref, kvag_ref, wqt_ref, wkt_ref, wvt_ref,
                     gq_ref, gk_ref, pos_ref, freq_ref, qt_ref, k_ref, vt_ref, zs_ref, h_ref):
    d = D_MODEL
    tm = x_ref.shape[0]
    x = x_ref[...]
    mod = mod_ref[...]
    shift, scale = mod[:, :d], mod[:, d:2 * d]
    h_ref[...] = _rms(x, ng_ref[...] * (1.0 + scale)) + shift
    h = h_ref[...].astype(BF16)
    p1 = jnp.dot(h, w1_ref[...], preferred_element_type=F32)
    o1, o2 = Q_LORA, Q_LORA + KV_LORA
    cqn = _rms(p1[:, :o1], qag_ref[...]).astype(BF16)
    ckvn = _rms(p1[:, o1:o2], kvag_ref[...]).astype(BF16)
    zs_ref[...] = _silu(lax.dot_general(wzt_ref[...], h, _NT, preferred_element_type=F32)).astype(BF16)

    qt = lax.dot_general(wqt_ref[...], cqn, _NT, preferred_element_type=F32)
    kt = lax.dot_general(wkt_ref[...], ckvn, _NT, preferred_element_type=F32)
    krt = p1[:, o2:].T
    vt = lax.dot_general(wvt_ref[...], ckvn, _NT, preferred_element_type=F32)
    ang = freq_ref[...] * pos_ref[...].astype(F32)
    cos_t, sin_t = jnp.cos(ang), jnp.sin(ang)
    gq = jnp.concatenate([gq_ref[...]] * (tm // LANES), axis=1)
    gk = jnp.concatenate([gk_ref[...]] * (tm // LANES), axis=1)
    first = lax.broadcasted_iota(jnp.int32, (V_ROWS - V_HEAD, tm), 0) == 0
    ones_rows = jnp.where(first, 1.0, 0.0).astype(BF16)
    r, lo = ROPE_HALF, QK_NOPE

    def norm_rope(slab, gain):
        ss = jnp.sum(slab * slab, axis=0, keepdims=True)
        n = slab * lax.rsqrt(ss * (1.0 / QK_HEAD) + EPS) * gain
        x1, x2 = n[lo:lo + r], n[lo + r:lo + 2 * r]
        return jnp.concatenate([n[:lo], x1 * cos_t - x2 * sin_t, x2 * cos_t + x1 * sin_t, n[lo + 2 * r:]], axis=0)

    for head in range(MLA_HEADS):
        rows = slice(LANES * head, LANES * (head + 1))
        qt_ref[head] = norm_rope(qt[rows], gq).astype(BF16)
        k_ref[head] = norm_rope(kt[rows] + krt, gk).T.astype(BF16)
        vt_ref[head, 0:V_HEAD, :] = vt[V_HEAD * head:V_HEAD * (head + 1)].astype(BF16)
        vt_ref[head, V_HEAD:, :] = ones_rows


def _mla_projection(x2, mod3, norm_g, w_in, q_a_g, w_q_b, kv_a_g, w_kv_b, q_norm_g, k_norm_g, positions, n_b, seq):
    t, d = x2.shape
    tm = min(512, seq)
    per_b = seq // tm
    o2 = 2 * SSM_WIDTH
    o4, o5 = o2 + Q_LORA + KV_LORA, o2 + Q_LORA + KV_LORA + QK_ROPE
    src = _slab_source()
    rope_src = np.where(src >= QK_NOPE, src - QK_NOPE, -1)
    nope_src = np.where((src >= 0) & (src < QK_NOPE), src, -1)
    wkr = _take_static(w_in[:, o4:o5], rope_src)
    w1 = jnp.concatenate([w_in[:, o2:o4], wkr], axis=1).astype(BF16)
    wzt = w_in[:, o5:].T.astype(BF16)
    wqt = _to_slabs(w_q_b, QK_HEAD, src).T.astype(BF16)
    wkt = _to_slabs(w_kv_b, QK_NOPE + V_HEAD, nope_src).T.astype(BF16)
    wvt = w_kv_b.reshape(KV_LORA, MLA_HEADS, QK_NOPE + V_HEAD)[:, :, QK_NOPE:].reshape(KV_LORA, MLA_WIDTH).T.astype(BF16)
    q_scale = math.log2(math.e) / math.sqrt(QK_HEAD)
    slab_gain = lambda g: jnp.broadcast_to(_take_static(g, src)[:, None], (LANES, LANES))
    gq = slab_gain(q_norm_g * q_scale)
    gk = slab_gain(k_norm_g)
    const = lambda shape: pl.BlockSpec(shape, lambda i: (0,) * len(shape))
    tok = lambda i: (i // per_b, 0, 0, i % per_b)
    return pl.pallas_call(
        _mla_proj_kernel,
        out_shape=(jax.ShapeDtypeStruct((n_b, MLA_HEADS, LANES, seq), BF16),
                   jax.ShapeDtypeStruct((n_b, MLA_HEADS, seq, LANES), BF16),
                   jax.ShapeDtypeStruct((n_b, MLA_HEADS, V_ROWS, seq), BF16),
                   jax.ShapeDtypeStruct((MLA_WIDTH, t), BF16),
                   jax.ShapeDtypeStruct((t, d), F32)),
        grid=(t // tm,),
        in_specs=[pl.BlockSpec((tm, d), lambda i: (i, 0)),
                  pl.BlockSpec((None, 1, 3 * d), lambda i: (i // per_b, 0, 0)),
                  const((1, d)), const(w1.shape), const(wzt.shape), const((1, Q_LORA)), const((1, KV_LORA)),
                  const(wqt.shape), const(wkt.shape), const(wvt.shape),
                  const((LANES, LANES)), const((LANES, LANES)),
                  pl.BlockSpec((1, tm), lambda i: (0, i)),
                  const((ROPE_HALF, 1))],
        out_specs=(pl.BlockSpec((None, MLA_HEADS, LANES, tm), tok),
                   pl.BlockSpec((None, MLA_HEADS, tm, LANES), lambda i: (i // per_b, 0, i % per_b, 0)),
                   pl.BlockSpec((None, MLA_HEADS, V_ROWS, tm), tok),
                   pl.BlockSpec((MLA_WIDTH, tm), lambda i: (0, i)),
                   pl.BlockSpec((tm, d), lambda i: (i, 0))),
        compiler_params=pltpu.CompilerParams(dimension_semantics=("arbitrary",),
                                             vmem_limit_bytes=VMEM_LIMIT),
        name="mla_projection",
    )(x2, mod3, norm_g.reshape(1, d), w1, wzt, q_a_g.reshape(1, -1), kv_a_g.reshape(1, -1), wqt, wkt, wvt, gq, gk,
      positions.reshape(1, t), _rope_frequencies())


ATTN_TQ = 2048
ATTN_TK = 256
HEADS_PER_STEP = 2


def _attn_kernel(qt_ref, k_ref, vt_ref, zs_ref, o_ref, s_scr):
    tq, tk = ATTN_TQ, ATTN_TK
    r = tq // tk
    assert r * tk == tq and r % 2 == 0
    qi = pl.program_id(2)

    def scores(slot, t, lo=0):
        start = pl.multiple_of(t * tk, tk)
        for e in range(HEADS_PER_STEP):
            s_scr[slot, e, :, lo:] = jnp.dot(k_ref[e, pl.ds(start, tk), :], qt_ref[e, :, lo:],
                                             preferred_element_type=F32)

    def process(slot, t, carry, lo=0, diagonal=False):
        start = pl.multiple_of(t * tk, tk)
        out = []
        for e in range(HEADS_PER_STEP):
            m, acc = carry[e]
            s = s_scr[slot, e, :, lo:]
            if diagonal:
                blk = s[:, :tk]
                ok = lax.broadcasted_iota(jnp.int32, blk.shape, 0) <= lax.broadcasted_iota(jnp.int32, blk.shape, 1)
                blk = jnp.where(ok, blk, NEG_BIG)
                s = jnp.concatenate([blk, s[:, tk:]], axis=1) if s.shape[1] > tk else blk
            m_old = m[:, lo:]
            m_new = jnp.maximum(m_old, jnp.max(s, axis=0, keepdims=True))
            p = jnp.exp2(s - m_new).astype(BF16)
            acc_new = (jnp.exp2(m_old - m_new) * acc[:, lo:]
                       + jnp.dot(vt_ref[e, :, pl.ds(start, tk)], p, preferred_element_type=F32))
            if lo:
                m_new = jnp.concatenate([m[:, :lo], m_new], axis=1)
                acc_new = jnp.concatenate([acc[:, :lo], acc_new], axis=1)
            out.append((m_new, acc_new))
        return tuple(out)

    def body(i, carry):
        base = i * r
        for u in range(r):
            scores((u + 1) % 2, base + u + 1)
            carry = process(u % 2, base + u, carry)
        return carry

    init = tuple((jnp.full((1, tq), NEG_BIG, F32), jnp.zeros((V_ROWS, tq), F32)) for _ in range(HEADS_PER_STEP))
    scores(0, 0)
    carry = lax.fori_loop(0, qi, body, init)
    base = qi * r
    for u in range(r):
        if u + 1 < r:
            scores((u + 1) % 2, base + u + 1, lo=(u + 1) * tk)
        carry = process(u % 2, base + u, carry, lo=u * tk, diagonal=True)
    _attn_finish(tuple(c[1] for c in carry), zs_ref, o_ref)


def _attn_finish(accs, zs_ref, o_ref):
    outs = [acc[:V_HEAD] / acc[V_HEAD:V_HEAD + 1] for acc in accs]
    o_ref[...] = (jnp.concatenate(outs, axis=0) * zs_ref[...].astype(F32)).astype(o_ref.dtype)


def _attn_bounded_kernel(qt_ref, k_ref, vt_ref, zs_ref, o_ref, p_scr, acc_scr):
    tq, tk = ATTN_TQ, ATTN_TK
    r = tq // tk
    assert r * tk == tq and r % 2 == 0
    qi = pl.program_id(2)
    q0 = qi * tq

    def probs(slot, t, lo=0, may_cross=False):
        start = pl.multiple_of(t * tk, tk)
        for e in range(HEADS_PER_STEP):
            p = jnp.exp2(jnp.dot(k_ref[e, pl.ds(start, tk), :], qt_ref[e, :, lo:], preferred_element_type=F32))
            if may_cross:
                blk = p[:, :tk]
                key = start + lax.broadcasted_iota(jnp.int32, blk.shape, 0)
                qry = q0 + lo + lax.broadcasted_iota(jnp.int32, blk.shape, 1)
                blk = jnp.where(key <= qry, blk, 0.0)
                p = jnp.concatenate([blk, p[:, tk:]], axis=1) if p.shape[1] > tk else blk
            p_scr[slot, e, :, lo:] = p.astype(BF16)

    def accumulate(slot, t, lo=0):
        start = pl.multiple_of(t * tk, tk)
        for e in range(HEADS_PER_STEP):
            acc_scr[e, :, lo:] += jnp.dot(vt_ref[e, :, pl.ds(start, tk)], p_scr[slot, e, :, lo:],
                                          preferred_element_type=F32)

    def body(i, carry):
        base = i * r
        for u in range(r):
            probs((u + 1) % 2, base + u + 1, may_cross=(u == r - 1))
            accumulate(u % 2, base + u)
        return carry

    acc_scr[...] = jnp.zeros_like(acc_scr)
    probs(0, 0, may_cross=True)
    lax.fori_loop(0, qi, body, 0)
    base = qi * r
    for u in range(r):
        if u + 1 < r:
            probs((u + 1) % 2, base + u + 1, lo=(u + 1) * tk, may_cross=True)
        accumulate(u % 2, base + u, lo=u * tk)
    _attn_finish(tuple(acc_scr[e] for e in range(HEADS_PER_STEP)), zs_ref, o_ref)


SCORE_BOUND_LOG2 = 60.0


def _attention(qt, k, vt, zs, score_bound):
    n_b, n_h, seq, _ = k.shape
    tq = ATTN_TQ
    assert seq % tq == 0 and n_h % HEADS_PER_STEP == 0
    nq = seq // tq
    hp = HEADS_PER_STEP
    width = hp * V_HEAD

    def call(body, scratch):
        return pl.pallas_call(
            body,
            out_shape=jax.ShapeDtypeStruct(zs.shape, BF16),
            grid=(n_b, n_h // hp, nq),
            in_specs=[pl.BlockSpec((None, hp, LANES, tq), lambda b, h, i: (b, h, 0, i)),
                      pl.BlockSpec((None, hp, seq, LANES), lambda b, h, i: (b, h, 0, 0)),
                      pl.BlockSpec((None, hp, V_ROWS, seq), lambda b, h, i: (b, h, 0, 0)),
                      pl.BlockSpec((width, tq), lambda b, h, i: (h, b * nq + i))],
            out_specs=pl.BlockSpec((width, tq), lambda b, h, i: (h, b * nq + i)),
            scratch_shapes=scratch,
            compiler_params=pltpu.CompilerParams(dimension_semantics=("arbitrary", "arbitrary", "arbitrary"),
                                                 vmem_limit_bytes=VMEM_LIMIT),
            name="causal_attention",
        )(qt, k, vt, zs)

    tiles = (2, hp, ATTN_TK, tq)
    return lax.cond(score_bound <= SCORE_BOUND_LOG2,
                    lambda: call(_attn_bounded_kernel, [pltpu.VMEM(tiles, BF16), pltpu.VMEM((hp, V_ROWS, tq), F32)]),
                    lambda: call(_attn_kernel, [pltpu.VMEM(tiles, F32)]))


def _s5_scan_steps(n_top):
    return [CHUNK * m for m in range(SUPER)] + [CHUNK * SUPER * (1 << i) for i in range(n_top)]


def _s5_exponents(n_top):
    return sorted(set(range(CHUNK + 1)) | set(_s5_scan_steps(n_top)))


def _s5_coef_kernel(kk_ref, *refs, n_top):
    for gi in range(SCAN_GROUPS):
        _s5_coef_one_group(kk_ref, *(r.at[gi] for r in refs), n_top=n_top)


def _s5_coef_one_group(kk_ref, lre_ref, lim_ref, ldt_ref, bre_ref, bim_ref, cre_ref, cim_ref, d_ref,
                       lhs_ref, wout_ref, ap_ref, *, n_top):
    h, p, n = SSM_GROUP, SSM_STATE, CHUNK
    lre, lim = lre_ref[...], lim_ref[...]
    dt = jnp.exp(ldt_ref[...])
    exps = _s5_exponents(n_top)
    kcol = kk_ref[...]
    mag = jnp.exp(kcol * (lre * dt))
    ang = kcol * (lim * dt)
    pow_re, pow_im = mag * jnp.cos(ang), mag * jnp.sin(ang)

    def power(kk):
        i = exps.index(kk)
        return pow_re[i:i + 1], pow_im[i:i + 1]

    lb_re, lb_im = power(1)
    nr, ni = lb_re - 1.0, lb_im
    den = lre * lre + lim * lim
    f_re = (nr * lre + ni * lim) / den
    f_im = (ni * lre - nr * lim) / den
    bre, bim = bre_ref[...].T, bim_ref[...].T
    bb_re = f_re * bre - f_im * bim
    bb_im = f_re * bim + f_im * bre
    cre, cim = cre_ref[...], cim_ref[...]
    cp_re, cp_im = [], []
    for kk in range(n + 1):
        pr, pi = power(kk)
        cp_re.append(cre * pr - cim * pi)
        cp_im.append(cre * pi + cim * pr)

    cpw = [jnp.concatenate([cp_re[kk], -cp_im[kk]], axis=1) for kk in range(n + 1)]
    bb_a = jnp.concatenate([bb_re, bb_im], axis=1)
    kw = lax.dot_general(jnp.concatenate(cpw[:n], axis=0), jnp.concatenate([bb_a] * n, axis=0), _NT,
                         preferred_element_type=F32, precision=HIGHEST)
    lane = lax.broadcasted_iota(jnp.int32, (h, n * h), 1)
    row = lax.broadcasted_iota(jnp.int32, (h, n * h), 0)
    col_blk = lax.shift_right_logical(lane, 4)
    d_tiled = jnp.concatenate([d_ref[...]] * n, axis=1)
    lag = [kw[h * kk:h * (kk + 1)] for kk in range(n)]
    lag[0] = lag[0] + jnp.where((lane & (h - 1)) == row, d_tiled, 0.0)
    rows = []
    for j in range(n):
        acc = jnp.zeros((h, n * h), F32)
        for jp in range(j + 1):
            acc = jnp.where(col_blk == jp, lag[j - jp], acc)
        rows.append(acc)
    lhs_ref[0:n * h, :] = jnp.concatenate(rows, axis=0).astype(lhs_ref.dtype)

    bb_b = jnp.concatenate([-bb_im, bb_re], axis=1)
    win = []
    for j in range(n):
        pr, pi = power(n - 1 - j)
        win.append(jnp.concatenate([pr, pr], axis=1) * bb_a + jnp.concatenate([pi, pi], axis=1) * bb_b)
    lhs_ref[n * h:, :] = jnp.concatenate(win, axis=0).T.astype(lhs_ref.dtype)
    wout_ref[...] = jnp.concatenate(cpw[1:], axis=0).astype(wout_ref.dtype)
    for i, kk in enumerate(_s5_scan_steps(n_top)):
        pr, pi = power(kk)
        ap_ref[i:i + 1, :] = jnp.concatenate([pr, pi], axis=1)


def _s5_coefficients(log_dt, lam_re, lam_im, b_re, b_im, c_re, c_im, d_skip, n_top):
    g, p, h = SSM_GROUPS, SSM_STATE, SSM_GROUP
    n_ap = SUPER + n_top
    gb = SCAN_GROUPS
    grp = lambda *shape: pl.BlockSpec((gb,) + shape, lambda i: (i,) + (0,) * len(shape))
    exps = np.asarray(_s5_exponents(n_top), np.float32)
    kk = np.zeros((-(-exps.size // 8) * 8, 1), np.float32)
    kk[:exps.size, 0] = exps
    return pl.pallas_call(
        functools.partial(_s5_coef_kernel, n_top=n_top),
        out_shape=(jax.ShapeDtypeStruct((g, CHUNK * h + 2 * p, CHUNK * h), BF16),
                   jax.ShapeDtypeStruct((g, CHUNK * h, 2 * p), BF16),
                   jax.ShapeDtypeStruct((g, n_ap, 2 * p), F32)),
        grid=(g // gb,),
        in_specs=[pl.BlockSpec(kk.shape, lambda i: (0, 0)),
                  grp(1, p), grp(1, p), grp(1, 1), grp(p, h), grp(p, h), grp(h, p), grp(h, p), grp(1, h)],
        out_specs=(grp(CHUNK * h + 2 * p, CHUNK * h), grp(CHUNK * h, 2 * p), grp(n_ap, 2 * p)),
        name="s5_coefficients",
    )(jnp.asarray(kk), lam_re.reshape(g, 1, p), lam_im.reshape(g, 1, p), log_dt.reshape(g, 1, 1), b_re, b_im,
      c_re, c_im, d_skip.reshape(g, 1, h))


PHASES = 8


def _ssm_proj_kernel(h_ref, w_ref, u_ref, z_ref, rows_scr, *, n_b):
    for jj in range(PHASES):
        for b in range(n_b):
            rows_scr[b] = h_ref[b, :, jj, :]
        h = jnp.concatenate([rows_scr[b].astype(BF16) for b in range(n_b)], axis=0)
        r = lax.dot_general(w_ref[...], h, _NT, preferred_element_type=F32)
        u_ref[jj] = r[:SSM_WIDTH].astype(BF16)
        z_ref[jj] = _silu(r[SSM_WIDTH:]).astype(BF16)


def _ssm_projection(h4, w_in):
    n_b, c2, _, d = h4.shape
    lanes = n_b * c2
    halves = CHUNK // PHASES
    w_t = w_in[:, :2 * SSM_WIDTH].T.astype(BF16)
    out = jax.ShapeDtypeStruct((SUPER, CHUNK, SSM_WIDTH, lanes), BF16)
    out_spec = pl.BlockSpec((None, PHASES, SSM_WIDTH, lanes), lambda j2, hf: (j2, hf, 0, 0))
    return pl.pallas_call(
        functools.partial(_ssm_proj_kernel, n_b=n_b),
        out_shape=(out, out),
        grid=(SUPER, halves),
        in_specs=[pl.BlockSpec((n_b, c2, PHASES, d), lambda j2, hf: (0, 0, j2 * halves + hf, 0)),
                  pl.BlockSpec(w_t.shape, lambda j2, hf: (0, 0))],
        out_specs=(out_spec, out_spec),
        scratch_shapes=[pltpu.VMEM((n_b, c2, d), F32)],
        compiler_params=pltpu.CompilerParams(dimension_semantics=("arbitrary", "arbitrary"),
                                             vmem_limit_bytes=VMEM_LIMIT),
        name="ssm_projection",
    )(h4, w_t)


def _cmul(ar, ai, xr, xi):
    return ar * xr - ai * xi, ar * xi + ai * xr


SCAN_GROUPS = 8


def _s5_group_kernel(x_ref, lhs_ref, wout_ref, ap_ref, y_ref, *, n_b, c2, n_top):
    p, gb = SSM_STATE, SCAN_GROUPS
    rows = CHUNK * SSM_GROUP
    n = SUPER * n_b * c2
    lanes = n_b * c2

    def x_group(gi):
        return jnp.concatenate([x_ref[j2, :, SSM_GROUP * gi:SSM_GROUP * (gi + 1), :].reshape(rows, lanes)
                                for j2 in range(SUPER)], axis=1)

    r = [jnp.dot(lhs_ref[gi], x_group(gi), preferred_element_type=F32) for gi in range(gb)]
    c_re = jnp.concatenate([r[gi][rows:rows + p] for gi in range(gb)], axis=0)
    c_im = jnp.concatenate([r[gi][rows + p:] for gi in range(gb)], axis=0)

    def mult(i):
        rep = [jnp.broadcast_to(ap_ref[gi, i:i + 1, :], (2 * p, 2 * p)).T for gi in range(gb)]
        return (jnp.concatenate([m[:p, :c2] for m in rep], axis=0),
                jnp.concatenate([m[p:, :c2] for m in rep], axis=0))

    lane = lax.broadcasted_iota(jnp.int32, (gb * p, c2), 1)

    def shifted(a, sh):
        return jnp.where(lane >= sh, pltpu.roll(a, sh, 1), 0.0)

    pieces = [[None] * n_b for _ in range(SUPER)]
    for b in range(n_b):
        piece = lambda a, j2: a[:, (j2 * n_b + b) * c2:(j2 * n_b + b + 1) * c2]
        a_re, a_im = mult(1)
        e_re = jnp.zeros((gb * p, c2), F32)
        e_im = jnp.zeros((gb * p, c2), F32)
        local = []
        for j2 in range(SUPER):
            local.append((e_re, e_im))
            t_re, t_im = _cmul(a_re, a_im, e_re, e_im)
            e_re, e_im = t_re + piece(c_re, j2), t_im + piece(c_im, j2)
        for i in range(n_top):
            m_re, m_im = mult(SUPER + i)
            t_re, t_im = _cmul(m_re, m_im, shifted(e_re, 1 << i), shifted(e_im, 1 << i))
            e_re, e_im = e_re + t_re, e_im + t_im
        s_re, s_im = shifted(e_re, 1), shifted(e_im, 1)
        for j2 in range(SUPER):
            if j2 == 0:
                pieces[j2][b] = (s_re, s_im)
            else:
                m_re, m_im = mult(j2)
                t_re, t_im = _cmul(m_re, m_im, s_re, s_im)
                pieces[j2][b] = (local[j2][0] + t_re, local[j2][1] + t_im)
    for gi in range(gb):
        sl = slice(gi * p, (gi + 1) * p)
        s_in = jnp.concatenate([jnp.concatenate([pieces[j2][b][0][sl], pieces[j2][b][1][sl]], axis=0)
                                for j2 in range(SUPER) for b in range(n_b)], axis=1)
        y = r[gi][:rows] + jnp.dot(wout_ref[gi], s_in.astype(BF16), preferred_element_type=F32)
        for j2 in range(SUPER):
            y_ref[j2, gi] = y[:, j2 * lanes:(j2 + 1) * lanes].reshape(CHUNK, SSM_GROUP, lanes).astype(y_ref.dtype)


def _s5_scan(u_t, lhs, wout, ap, n_b, c2, n_top):
    g, h = SSM_GROUPS, SSM_GROUP
    lanes = u_t.shape[-1]
    gb = SCAN_GROUPS
    grp = lambda *shape: pl.BlockSpec((gb,) + shape, lambda i: (i,) + (0,) * len(shape))
    return pl.pallas_call(
        functools.partial(_s5_group_kernel, n_b=n_b, c2=c2, n_top=n_top),
        out_shape=jax.ShapeDtypeStruct((SUPER, g, CHUNK, h, lanes), BF16),
        grid=(g // gb,),
        in_specs=[pl.BlockSpec((SUPER, CHUNK, gb * h, lanes), lambda i: (0, 0, i, 0)),
                  grp(*lhs.shape[1:]), grp(*wout.shape[1:]), grp(*ap.shape[1:])],
        out_specs=pl.BlockSpec((SUPER, gb, CHUNK, h, lanes), lambda i: (0, i, 0, 0, 0)),
        compiler_params=pltpu.CompilerParams(dimension_semantics=("arbitrary",),
                                             vmem_limit_bytes=VMEM_LIMIT),
        name="s5_chunk_scan",
    )(u_t, lhs, wout, ap)


def _glu_kernel(y_ref, zs_ref, w_ref, b_ref, o_ref, rows_scr, *, n_b, c2):
    g, _, h, n = y_ref.shape
    for jj in range(CHUNK):
        y = jax.nn.gelu(y_ref[:, jj].reshape(g * h, n).astype(F32))
        t = jnp.dot(w_ref[...], y.astype(BF16), preferred_element_type=F32) + b_ref[...]
        gated = (y * jax.nn.sigmoid(t) * zs_ref[jj].astype(F32)).T
        for b in range(n_b):
            rows_scr[b, :, jj, :] = gated[b * c2:(b + 1) * c2]
    o_ref[...] = rows_scr[...].astype(BF16)


def _glu(y_t, zs_t, w_glu, b_glu, n_b, c2):
    _, g, _, h, _ = y_t.shape
    lanes = n_b * c2
    w_t = w_glu.T.astype(BF16)
    return pl.pallas_call(
        functools.partial(_glu_kernel, n_b=n_b, c2=c2),
        out_shape=jax.ShapeDtypeStruct((n_b, c2, SUPER * CHUNK, SSM_WIDTH), BF16),
        grid=(SUPER,),
        in_specs=[pl.BlockSpec((None, g, CHUNK, h, lanes), lambda j2: (j2, 0, 0, 0, 0)),
                  pl.BlockSpec((None, CHUNK, SSM_WIDTH, lanes), lambda j2: (j2, 0, 0, 0)),
                  pl.BlockSpec(w_t.shape, lambda j2: (0, 0)),
                  pl.BlockSpec((SSM_WIDTH, 1), lambda j2: (0, 0))],
        out_specs=pl.BlockSpec((n_b, c2, CHUNK, SSM_WIDTH), lambda j2: (0, 0, j2, 0)),
        scratch_shapes=[pltpu.VMEM((n_b, c2, CHUNK, SSM_WIDTH), F32)],
        compiler_params=pltpu.CompilerParams(dimension_semantics=("arbitrary",),
                                             vmem_limit_bytes=VMEM_LIMIT),
        name="s5_glu",
    )(y_t, zs_t, w_t, b_glu.reshape(SSM_WIDTH, 1))


def _out_proj_kernel(ys_ref, ym_ref, x_ref, mod_ref, wa_ref, wb_ref, o_ref):
    y = (jnp.dot(ys_ref[...], wa_ref[...], preferred_element_type=F32)
         + lax.dot_general(ym_ref[...], wb_ref[...], _TN, preferred_element_type=F32))
    gate = mod_ref[...][:, 2 * D_MODEL:]
    o_ref[...] = x_ref[...] + gate * y


def _out_projection(ys, ym, x2, mod3, w_out, seq):
    t, d = x2.shape
    tm = min(1024, seq)
    per_b = seq // tm
    wa = w_out[:SSM_WIDTH].astype(BF16)
    wb = w_out[SSM_WIDTH:].astype(BF16)
    tok = lambda w: pl.BlockSpec((tm, w), lambda i: (i, 0))
    return pl.pallas_call(
        _out_proj_kernel,
        out_shape=jax.ShapeDtypeStruct(x2.shape, F32),
        grid=(t // tm,),
        in_specs=[tok(SSM_WIDTH), pl.BlockSpec((MLA_WIDTH, tm), lambda i: (0, i)), tok(d),
                  pl.BlockSpec((None, 1, 3 * d), lambda i: (i // per_b, 0, 0)),
                  pl.BlockSpec(wa.shape, lambda i: (0, 0)),
                  pl.BlockSpec(wb.shape, lambda i: (0, 0))],
        out_specs=tok(d),
        compiler_params=pltpu.CompilerParams(dimension_semantics=("arbitrary",),
                                             vmem_limit_bytes=VMEM_LIMIT),
        name="output_projection",
    )(ys, ym, x2, mod3, wa, wb)


def kernel(x, c, positions, w_ada, b_ada, norm_g, w_in, log_dt, lam_re, lam_im, b_re, b_im, c_re, c_im, d_skip,
           w_glu, b_glu, q_a_g, w_q_b, kv_a_g, w_kv_b, q_norm_g, k_norm_g, w_out):
    n_b, seq, d = x.shape
    depth = w_ada.shape[0]
    c2 = seq // (CHUNK * SUPER)
    n_top = max(int(math.log2(c2)), 0)
    assert c2 * CHUNK * SUPER == seq and (1 << n_top) == c2
    for l in range(depth):
        mod3 = _modulation(c, w_ada[l], b_ada[l]).reshape(n_b, 1, 3 * d)
        x2 = x.reshape(n_b * seq, d)
        lhs, wout, ap = _s5_coefficients(log_dt[l], lam_re[l], lam_im[l], b_re[l], b_im[l], c_re[l], c_im[l],
                                         d_skip[l], n_top)
        qt, k, vt, zm, h = _mla_projection(x2, mod3, norm_g[l], w_in[l], q_a_g[l], w_q_b[l],
                                           kv_a_g[l], w_kv_b[l], q_norm_g[l], k_norm_g[l], positions, n_b, seq)
        score_bound = (1.05 * math.sqrt(QK_HEAD) * math.log2(math.e)
                       * jnp.max(jnp.abs(q_norm_g[l])) * jnp.max(jnp.abs(k_norm_g[l])))
        ym = _attention(qt, k, vt, zm, score_bound)
        ym, h = lax.optimization_barrier((ym, h))
        u_t, zs_t = _ssm_projection(h.reshape(n_b, c2, SUPER * CHUNK, d), w_in[l])
        y_t = _s5_scan(u_t, lhs, wout, ap, n_b, c2, n_top)
        ys = _glu(y_t, zs_t, w_glu[l], b_glu[l], n_b, c2).reshape(n_b * seq, SSM_WIDTH)
        x = _out_projection(ys, ym, x2, mod3, w_out[l], seq).reshape(n_b, seq, d)
    return x
```

```python
import functools
import math

import numpy as np
import jax
import jax.numpy as jnp
from jax import lax
from jax.experimental import pallas as pl
from jax.experimental.pallas import tpu as pltpu

F32 = jnp.float32
BF16 = jnp.bfloat16
HIGHEST = lax.Precision.HIGHEST

D_MODEL = 1024
SSM_WIDTH = 512
SSM_GROUP = 16
SSM_GROUPS = 32
SSM_STATE = 64
MLA_HEADS = 8
QK_NOPE = 64
QK_ROPE = 32
QK_HEAD = QK_NOPE + QK_ROPE
V_HEAD = 64
MLA_WIDTH = MLA_HEADS * V_HEAD
Q_LORA = 384
KV_LORA = 256
ROPE_THETA = 10000.0
EPS = 1e-6
NEG_BIG = -1e30

LANES = 128
VMEM_LIMIT = 60 * 1024 * 1024
CHUNK = 16
SUPER = 4
ROPE_HALF = QK_ROPE // 2

_NT = (((1,), (1,)), ((), ()))
_TN = (((0,), (0,)), ((), ()))


def _silu(v):
    return v * jax.nn.sigmoid(v)


def _rms(v, gain):
    return v * lax.rsqrt(jnp.mean(v * v, axis=-1, keepdims=True) + EPS) * gain


def _mod_kernel(c_ref, w_ref, b_ref, o_ref):
    n_b, d = c_ref.shape
    tn = w_ref.shape[1]
    act_t = _silu(c_ref[...]).T
    out = []
    for b in range(n_b):
        col = jnp.broadcast_to(act_t[:, b:b + 1], (d, LANES))
        out.append(jnp.concatenate([jnp.sum(w_ref[:, j:j + LANES] * col, axis=0, keepdims=True)
                                    for j in range(0, tn, LANES)], axis=1))
    o_ref[...] = jnp.concatenate(out, axis=0) + b_ref[...]


def _modulation(c, w, b):
    n_b, d = c.shape
    tn = 512
    return pl.pallas_call(
        _mod_kernel,
        out_shape=jax.ShapeDtypeStruct((n_b, 3 * d), F32),
        grid=(3 * d // tn,),
        in_specs=[pl.BlockSpec((n_b, d), lambda n: (0, 0)),
                  pl.BlockSpec((d, tn), lambda n: (0, n)),
                  pl.BlockSpec((1, tn), lambda n: (0, n))],
        out_specs=pl.BlockSpec((n_b, tn), lambda n: (0, n)),
        name="adaln_modulation",
    )(c, w, b.reshape(1, -1))


def _rope_frequencies():
    inv_freq = ROPE_THETA ** (-np.arange(ROPE_HALF, dtype=np.float64) * 2.0 / QK_ROPE)
    return jnp.asarray(inv_freq.astype(np.float32)).reshape(ROPE_HALF, 1)


def _slab_source():
    src = np.full((LANES,), -1, np.int64)
    src[:QK_HEAD] = np.arange(QK_HEAD)
    return src


def _take_static(w, src):
    parts, i = [], 0
    while i < len(src):
        j = i + 1
        while j < len(src) and (src[j] < 0 if src[i] < 0 else src[j] == src[j - 1] + 1):
            j += 1
        parts.append(jnp.zeros(w.shape[:-1] + (j - i,), w.dtype) if src[i] < 0
                     else w[..., int(src[i]):int(src[i]) + j - i])
        i = j
    return jnp.concatenate(parts, axis=-1)


def _to_slabs(w, per_head, src):
    k = w.shape[0]
    return _take_static(w.reshape(k, MLA_HEADS, per_head), src).reshape(k, MLA_HEADS * LANES)


V_ROWS = V_HEAD + 16


def _mla_proj_kernel(x_ref, mod_ref, ng_ref, w1_ref, wzt_ref, qag_ref, kvag_ref, wqt_ref, wkt_ref, wvt_ref,
                     gq_ref, gk_ref, pos_ref, freq_ref, qt_ref, k_ref, vt_ref, zs_ref, h_ref):
    d = D_MODEL
    tm = x_ref.shape[0]
    x = x_ref[...]
    mod = mod_ref[...]
    shift, scale = mod[:, :d], mod[:, d:2 * d]
    h_ref[...] = _rms(x, ng_ref[...] * (1.0 + scale)) + shift
    h = h_ref[...].astype(BF16)
    p1 = jnp.dot(h, w1_ref[...], preferred_element_type=F32)
    o1, o2 = Q_LORA, Q_LORA + KV_LORA
    cqn = _rms(p1[:, :o1], qag_ref[...]).astype(BF16)
    ckvn = _rms(p1[:, o1:o2], kvag_ref[...]).astype(BF16)
    zs_ref[...] = _silu(lax.dot_general(wzt_ref[...], h, _NT, preferred_element_type=F32)).astype(BF16)

    qt = lax.dot_general(wqt_ref[...], cqn, _NT, preferred_element_type=F32)
    kt = lax.dot_general(wkt_ref[...], ckvn, _NT, preferred_element_type=F32)
    krt = p1[:, o2:].T
    vt = lax.dot_general(wvt_ref[...], ckvn, _NT, preferred_element_type=F32)
    ang = freq_ref[...] * pos_ref[...].astype(F32)
    cos_t, sin_t = jnp.cos(ang), jnp.sin(ang)
    gq = jnp.concatenate([gq_ref[...]] * (tm // LANES), axis=1)
    gk = jnp.concatenate([gk_ref[...]] * (tm // LANES), axis=1)
    first = lax.broadcasted_iota(jnp.int32, (V_ROWS - V_HEAD, tm), 0) == 0
    ones_rows = jnp.where(first, 1.0, 0.0).astype(BF16)
    r, lo = ROPE_HALF, QK_NOPE

    def norm_rope(slab, gain):
        ss = jnp.sum(slab * slab, axis=0, keepdims=True)
        n = slab * lax.rsqrt(ss * (1.0 / QK_HEAD) + EPS) * gain
        x1, x2 = n[lo:lo + r], n[lo + r:lo + 2 * r]
        return jnp.concatenate([n[:lo], x1 * cos_t - x2 * sin_t, x2 * cos_t + x1 * sin_t, n[lo + 2 * r:]], axis=0)

    for head in range(MLA_HEADS):
        rows = slice(LANES * head, LANES * (head + 1))
        qt_ref[head] = norm_rope(qt[rows], gq).astype(BF16)
        k_ref[head] = norm_rope(kt[rows] + krt, gk).T.astype(BF16)
        vt_ref[head, 0:V_HEAD, :] = vt[V_HEAD * head:V_HEAD * (head + 1)].astype(BF16)
        vt_ref[head, V_HEAD:, :] = ones_rows


def _mla_projection(x2, mod3, norm_g, w_in, q_a_g, w_q_b, kv_a_g, w_kv_b, q_norm_g, k_norm_g, positions, n_b, seq):
    t, d = x2.shape
    tm = min(512, seq)
    per_b = seq // tm
    o2 = 2 * SSM_WIDTH
    o4, o5 = o2 + Q_LORA + KV_LORA, o2 + Q_LORA + KV_LORA + QK_ROPE
    src = _slab_source()
    rope_src = np.where(src >= QK_NOPE, src - QK_NOPE, -1)
    nope_src = np.where((src >= 0) & (src < QK_NOPE), src, -1)
    wkr = _take_static(w_in[:, o4:o5], rope_src)
    w1 = jnp.concatenate([w_in[:, o2:o4], wkr], axis=1).astype(BF16)
    wzt = w_in[:, o5:].T.astype(BF16)
    wqt = _to_slabs(w_q_b, QK_HEAD, src).T.astype(BF16)
    wkt = _to_slabs(w_kv_b, QK_NOPE + V_HEAD, nope_src).T.astype(BF16)
    wvt = w_kv_b.reshape(KV_LORA, MLA_HEADS, QK_NOPE + V_HEAD)[:, :, QK_NOPE:].reshape(KV_LORA, MLA_WIDTH).T.astype(BF16)
    q_scale = math.log2(math.e) / math.sqrt(QK_HEAD)
    slab_gain = lambda g: jnp.broadcast_to(_take_static(g, src)[:, None], (LANES, LANES))
    gq = slab_gain(q_norm_g * q_scale)
    gk = slab_gain(k_norm_g)
    const = lambda shape: pl.BlockSpec(shape, lambda i: (0,) * len(shape))
    tok = lambda i: (i // per_b, 0, 0, i % per_b)
    return pl.pallas_call(
        _mla_proj_kernel,
        out_shape=(jax.ShapeDtypeStruct((n_b, MLA_HEADS, LANES, seq), BF16),
                   jax.ShapeDtypeStruct((n_b, MLA_HEADS, seq, LANES), BF16),
                   jax.ShapeDtypeStruct((n_b, MLA_HEADS, V_ROWS, seq), BF16),
                   jax.ShapeDtypeStruct((MLA_WIDTH, t), BF16),
                   jax.ShapeDtypeStruct((t, d), F32)),
        grid=(t // tm,),
        in_specs=[pl.BlockSpec((tm, d), lambda i: (i, 0)),
                  pl.BlockSpec((None, 1, 3 * d), lambda i: (i // per_b, 0, 0)),
                  const((1, d)), const(w1.shape), const(wzt.shape), const((1, Q_LORA)), const((1, KV_LORA)),
                  const(wqt.shape), const(wkt.shape), const(wvt.shape),
                  const((LANES, LANES)), const((LANES, LANES)),
                  pl.BlockSpec((1, tm), lambda i: (0, i)),
                  const((ROPE_HALF, 1))],
        out_specs=(pl.BlockSpec((None, MLA_HEADS, LANES, tm), tok),
                   pl.BlockSpec((None, MLA_HEADS, tm, LANES), lambda i: (i // per_b, 0, i % per_b, 0)),
                   pl.BlockSpec((None, MLA_HEADS, V_ROWS, tm), tok),
                   pl.BlockSpec((MLA_WIDTH, tm), lambda i: (0, i)),
                   pl.BlockSpec((tm, d), lambda i: (i, 0))),
        compiler_params=pltpu.CompilerParams(dimension_semantics=("arbitrary",),
                                             vmem_limit_bytes=VMEM_LIMIT),
        name="mla_projection",
    )(x2, mod3, norm_g.reshape(1, d), w1, wzt, q_a_g.reshape(1, -1), kv_a_g.reshape(1, -1), wqt, wkt, wvt, gq, gk,
      positions.reshape(1, t), _rope_frequencies())


ATTN_TQ = 2048
ATTN_TK = 256
HEADS_PER_STEP = 2


def _attn_kernel(qt_ref, k_ref, vt_ref, zs_ref, o_ref, s_scr):
    tq, tk = ATTN_TQ, ATTN_TK
    r = tq // tk
    assert r * tk == tq and r % 2 == 0
    qi = pl.program_id(2)

    def scores(slot, t, lo=0):
        start = pl.multiple_of(t * tk, tk)
        for e in range(HEADS_PER_STEP):
            s_scr[slot, e, :, lo:] = jnp.dot(k_ref[e, pl.ds(start, tk), :], qt_ref[e, :, lo:],
                                             preferred_element_type=F32)

    def process(slot, t, carry, lo=0, diagonal=False):
        start = pl.multiple_of(t * tk, tk)
        out = []
        for e in range(HEADS_PER_STEP):
            m, acc = carry[e]
            s = s_scr[slot, e, :, lo:]
            if diagonal:
                blk = s[:, :tk]
                ok = lax.broadcasted_iota(jnp.int32, blk.shape, 0) <= lax.broadcasted_iota(jnp.int32, blk.shape, 1)
                blk = jnp.where(ok, blk, NEG_BIG)
                s = jnp.concatenate([blk, s[:, tk:]], axis=1) if s.shape[1] > tk else blk
            m_old = m[:, lo:]
            m_new = jnp.maximum(m_old, jnp.max(s, axis=0, keepdims=True))
            p = jnp.exp2(s - m_new).astype(BF16)
            acc_new = (jnp.exp2(m_old - m_new) * acc[:, lo:]
                       + jnp.dot(vt_ref[e, :, pl.ds(start, tk)], p, preferred_element_type=F32))
            if lo:
                m_new = jnp.concatenate([m[:, :lo], m_new], axis=1)
                acc_new = jnp.concatenate([acc[:, :lo], acc_new], axis=1)
            out.append((m_new, acc_new))
        return tuple(out)

    def body(i, carry):
        base = i * r
        for u in range(r):
            scores((u + 1) % 2, base + u + 1)
            carry = process(u % 2, base + u, carry)
        return carry

    init = tuple((jnp.full((1, tq), NEG_BIG, F32), jnp.zeros((V_ROWS, tq), F32)) for _ in range(HEADS_PER_STEP))
    scores(0, 0)
    carry = lax.fori_loop(0, qi, body, init)
    base = qi * r
    for u in range(r):
        if u + 1 < r:
            scores((u + 1) % 2, base + u + 1, lo=(u + 1) * tk)
        carry = process(u % 2, base + u, carry, lo=u * tk, diagonal=True)
    _attn_finish(tuple(c[1] for c in carry), zs_ref, o_ref)


def _attn_finish(accs, zs_ref, o_ref):
    outs = [acc[:V_HEAD] / acc[V_HEAD:V_HEAD + 1] for acc in accs]
    o_ref[...] = (jnp.concatenate(outs, axis=0) * zs_ref[...].astype(F32)).astype(o_ref.dtype)


def _attn_bounded_kernel(qt_ref, k_ref, vt_ref, zs_ref, o_ref, p_scr, acc_scr):
    tq, tk = ATTN_TQ, ATTN_TK
    r = tq // tk
    assert r * tk == tq and r % 2 == 0
    qi = pl.program_id(2)
    q0 = qi * tq

    def probs(slot, t, lo=0, may_cross=False):
        start = pl.multiple_of(t * tk, tk)
        for e in range(HEADS_PER_STEP):
            p = jnp.exp2(jnp.dot(k_ref[e, pl.ds(start, tk), :], qt_ref[e, :, lo:], preferred_element_type=F32))
            if may_cross:
                blk = p[:, :tk]
                key = start + lax.broadcasted_iota(jnp.int32, blk.shape, 0)
                qry = q0 + lo + lax.broadcasted_iota(jnp.int32, blk.shape, 1)
                blk = jnp.where(key <= qry, blk, 0.0)
                p = jnp.concatenate([blk, p[:, tk:]], axis=1) if p.shape[1] > tk else blk
            p_scr[slot, e, :, lo:] = p.astype(BF16)

    def accumulate(slot, t, lo=0):
        start = pl.multiple_of(t * tk, tk)
        for e in range(HEADS_PER_STEP):
            acc_scr[e, :, lo:] += jnp.dot(vt_ref[e, :, pl.ds(start, tk)], p_scr[slot, e, :, lo:],
                                          preferred_element_type=F32)

    def body(i, carry):
        base = i * r
        for u in range(r):
            probs((u + 1) % 2, base + u + 1, may_cross=(u == r - 1))
            accumulate(u % 2, base + u)
        return carry

    acc_scr[...] = jnp.zeros_like(acc_scr)
    probs(0, 0, may_cross=True)
    lax.fori_loop(0, qi, body, 0)
    base = qi * r
    for u in range(r):
        if u + 1 < r:
            probs((u + 1) % 2, base + u + 1, lo=(u + 1) * tk, may_cross=True)
        accumulate(u % 2, base + u, lo=u * tk)
    _attn_finish(tuple(acc_scr[e] for e in range(HEADS_PER_STEP)), zs_ref, o_ref)


SCORE_BOUND_LOG2 = 60.0


def _attention(qt, k, vt, zs, score_bound):
    n_b, n_h, seq, _ = k.shape
    tq = ATTN_TQ
    assert seq % tq == 0 and n_h % HEADS_PER_STEP == 0
    nq = seq // tq
    hp = HEADS_PER_STEP
    width = hp * V_HEAD

    def call(body, scratch):
        return pl.pallas_call(
            body,
            out_shape=jax.ShapeDtypeStruct(zs.shape, BF16),
            grid=(n_b, n_h // hp, nq),
            in_specs=[pl.BlockSpec((None, hp, LANES, tq), lambda b, h, i: (b, h, 0, i)),
                      pl.BlockSpec((None, hp, seq, LANES), lambda b, h, i: (b, h, 0, 0)),
                      pl.BlockSpec((None, hp, V_ROWS, seq), lambda b, h, i: (b, h, 0, 0)),
                      pl.BlockSpec((width, tq), lambda b, h, i: (h, b * nq + i))],
            out_specs=pl.BlockSpec((width, tq), lambda b, h, i: (h, b * nq + i)),
            scratch_shapes=scratch,
            compiler_params=pltpu.CompilerParams(dimension_semantics=("arbitrary", "arbitrary", "arbitrary"),
                                                 vmem_limit_bytes=VMEM_LIMIT),
            name="causal_attention",
        )(qt, k, vt, zs)

    tiles = (2, hp, ATTN_TK, tq)
    return lax.cond(score_bound <= SCORE_BOUND_LOG2,
                    lambda: call(_attn_bounded_kernel, [pltpu.VMEM(tiles, BF16), pltpu.VMEM((hp, V_ROWS, tq), F32)]),
                    lambda: call(_attn_kernel, [pltpu.VMEM(tiles, F32)]))


def _s5_scan_steps(n_top):
    return [CHUNK * m for m in range(SUPER)] + [CHUNK * SUPER * (1 << i) for i in range(n_top)]


def _s5_exponents(n_top):
    return sorted(set(range(CHUNK + 1)) | set(_s5_scan_steps(n_top)))


def _s5_coef_kernel(kk_ref, *refs, n_top):
    for gi in range(SCAN_GROUPS):
        _s5_coef_one_group(kk_ref, *(r.at[gi] for r in refs), n_top=n_top)


def _s5_coef_one_group(kk_ref, lre_ref, lim_ref, ldt_ref, bre_ref, bim_ref, cre_ref, cim_ref, d_ref,
                       lhs_ref, wout_ref, ap_ref, *, n_top):
    h, p, n = SSM_GROUP, SSM_STATE, CHUNK
    lre, lim = lre_ref[...], lim_ref[...]
    dt = jnp.exp(ldt_ref[...])
    exps = _s5_exponents(n_top)
    kcol = kk_ref[...]
    mag = jnp.exp(kcol * (lre * dt))
    ang = kcol * (lim * dt)
    pow_re, pow_im = mag * jnp.cos(ang), mag * jnp.sin(ang)

    def power(kk):
        i = exps.index(kk)
        return pow_re[i:i + 1], pow_im[i:i + 1]

    lb_re, lb_im = power(1)
    nr, ni = lb_re - 1.0, lb_im
    den = lre * lre + lim * lim
    f_re = (nr * lre + ni * lim) / den
    f_im = (ni * lre - nr * lim) / den
    bre, bim = bre_ref[...].T, bim_ref[...].T
    bb_re = f_re * bre - f_im * bim
    bb_im = f_re * bim + f_im * bre
    cre, cim = cre_ref[...], cim_ref[...]
    cp_re, cp_im = [], []
    for kk in range(n + 1):
        pr, pi = power(kk)
        cp_re.append(cre * pr - cim * pi)
        cp_im.append(cre * pi + cim * pr)

    cpw = [jnp.concatenate([cp_re[kk], -cp_im[kk]], axis=1) for kk in range(n + 1)]
    bb_a = jnp.concatenate([bb_re, bb_im], axis=1)
    kw = lax.dot_general(jnp.concatenate(cpw[:n], axis=0), jnp.concatenate([bb_a] * n, axis=0), _NT,
                         preferred_element_type=F32, precision=HIGHEST)
    lane = lax.broadcasted_iota(jnp.int32, (h, n * h), 1)
    row = lax.broadcasted_iota(jnp.int32, (h, n * h), 0)
    col_blk = lax.shift_right_logical(lane, 4)
    d_tiled = jnp.concatenate([d_ref[...]] * n, axis=1)
    lag = [kw[h * kk:h * (kk + 1)] for kk in range(n)]
    lag[0] = lag[0] + jnp.where((lane & (h - 1)) == row, d_tiled, 0.0)
    rows = []
    for j in range(n):
        acc = jnp.zeros((h, n * h), F32)
        for jp in range(j + 1):
            acc = jnp.where(col_blk == jp, lag[j - jp], acc)
        rows.append(acc)
    lhs_ref[0:n * h, :] = jnp.concatenate(rows, axis=0).astype(lhs_ref.dtype)

    bb_b = jnp.concatenate([-bb_im, bb_re], axis=1)
    win = []
    for j in range(n):
        pr, pi = power(n - 1 - j)
        win.append(jnp.concatenate([pr, pr], axis=1) * bb_a + jnp.concatenate([pi, pi], axis=1) * bb_b)
    lhs_ref[n * h:, :] = jnp.concatenate(win, axis=0).T.astype(lhs_ref.dtype)
    wout_ref[...] = jnp.concatenate(cpw[1:], axis=0).astype(wout_ref.dtype)
    for i, kk in enumerate(_s5_scan_steps(n_top)):
        pr, pi = power(kk)
        ap_ref[i:i + 1, :] = jnp.concatenate([pr, pi], axis=1)


def _s5_coefficients(log_dt, lam_re, lam_im, b_re, b_im, c_re, c_im, d_skip, n_top):
    g, p, h = SSM_GROUPS, SSM_STATE, SSM_GROUP
    n_ap = SUPER + n_top
    gb = SCAN_GROUPS
    grp = lambda *shape: pl.BlockSpec((gb,) + shape, lambda i: (i,) + (0,) * len(shape))
    exps = np.asarray(_s5_exponents(n_top), np.float32)
    kk = np.zeros((-(-exps.size // 8) * 8, 1), np.float32)
    kk[:exps.size, 0] = exps
    return pl.pallas_call(
        functools.partial(_s5_coef_kernel, n_top=n_top),
        out_shape=(jax.ShapeDtypeStruct((g, CHUNK * h + 2 * p, CHUNK * h), BF16),
                   jax.ShapeDtypeStruct((g, CHUNK * h, 2 * p), BF16),
                   jax.ShapeDtypeStruct((g, n_ap, 2 * p), F32)),
        grid=(g // gb,),
        in_specs=[pl.BlockSpec(kk.shape, lambda i: (0, 0)),
                  grp(1, p), grp(1, p), grp(1, 1), grp(p, h), grp(p, h), grp(h, p), grp(h, p), grp(1, h)],
        out_specs=(grp(CHUNK * h + 2 * p, CHUNK * h), grp(CHUNK * h, 2 * p), grp(n_ap, 2 * p)),
        name="s5_coefficients",
    )(jnp.asarray(kk), lam_re.reshape(g, 1, p), lam_im.reshape(g, 1, p), log_dt.reshape(g, 1, 1), b_re, b_im,
      c_re, c_im, d_skip.reshape(g, 1, h))


PHASES = 8


def _ssm_proj_kernel(h_ref, w_ref, u_ref, z_ref, rows_scr, *, n_b):
    for jj in range(PHASES):
        for b in range(n_b):
            rows_scr[b] = h_ref[b, :, jj, :]
        h = jnp.concatenate([rows_scr[b].astype(BF16) for b in range(n_b)], axis=0)
        r = lax.dot_general(w_ref[...], h, _NT, preferred_element_type=F32)
        u_ref[jj] = r[:SSM_WIDTH].astype(BF16)
        z_ref[jj] = _silu(r[SSM_WIDTH:]).astype(BF16)


def _ssm_projection(h4, w_in):
    n_b, c2, _, d = h4.shape
    lanes = n_b * c2
    halves = CHUNK // PHASES
    w_t = w_in[:, :2 * SSM_WIDTH].T.astype(BF16)
    out = jax.ShapeDtypeStruct((SUPER, CHUNK, SSM_WIDTH, lanes), BF16)
    out_spec = pl.BlockSpec((None, PHASES, SSM_WIDTH, lanes), lambda j2, hf: (j2, hf, 0, 0))
    return pl.pallas_call(
        functools.partial(_ssm_proj_kernel, n_b=n_b),
        out_shape=(out, out),
        grid=(SUPER, halves),
        in_specs=[pl.BlockSpec((n_b, c2, PHASES, d), lambda j2, hf: (0, 0, j2 * halves + hf, 0)),
                  pl.BlockSpec(w_t.shape, lambda j2, hf: (0, 0))],
        out_specs=(out_spec, out_spec),
        scratch_shapes=[pltpu.VMEM((n_b, c2, d), F32)],
        compiler_params=pltpu.CompilerParams(dimension_semantics=("arbitrary", "arbitrary"),
                                             vmem_limit_bytes=VMEM_LIMIT),
        name="ssm_projection",
    )(h4, w_t)


def _cmul(ar, ai, xr, xi):
    return ar * xr - ai * xi, ar * xi + ai * xr


SCAN_GROUPS = 8


def _s5_group_kernel(x_ref, lhs_ref, wout_ref, ap_ref, y_ref, *, n_b, c2, n_top):
    p, gb = SSM_STATE, SCAN_GROUPS
    rows = CHUNK * SSM_GROUP
    n = SUPER * n_b * c2
    lanes = n_b * c2

    def x_group(gi):
        return jnp.concatenate([x_ref[j2, :, SSM_GROUP * gi:SSM_GROUP * (gi + 1), :].reshape(rows, lanes)
                                for j2 in range(SUPER)], axis=1)

    r = [jnp.dot(lhs_ref[gi], x_group(gi), preferred_element_type=F32) for gi in range(gb)]
    c_re = jnp.concatenate([r[gi][rows:rows + p] for gi in range(gb)], axis=0)
    c_im = jnp.concatenate([r[gi][rows + p:] for gi in range(gb)], axis=0)

    def mult(i):
        rep = [jnp.broadcast_to(ap_ref[gi, i:i + 1, :], (2 * p, 2 * p)).T for gi in range(gb)]
        return (jnp.concatenate([m[:p, :c2] for m in rep], axis=0),
                jnp.concatenate([m[p:, :c2] for m in rep], axis=0))

    lane = lax.broadcasted_iota(jnp.int32, (gb * p, c2), 1)

    def shifted(a, sh):
        return jnp.where(lane >= sh, pltpu.roll(a, sh, 1), 0.0)

    pieces = [[None] * n_b for _ in range(SUPER)]
    for b in range(n_b):
        piece = lambda a, j2: a[:, (j2 * n_b + b) * c2:(j2 * n_b + b + 1) * c2]
        a_re, a_im = mult(1)
        e_re = jnp.zeros((gb * p, c2), F32)
        e_im = jnp.zeros((gb * p, c2), F32)
        local = []
        for j2 in range(SUPER):
            local.append((e_re, e_im))
            t_re, t_im = _cmul(a_re, a_im, e_re, e_im)
            e_re, e_im = t_re + piece(c_re, j2), t_im + piece(c_im, j2)
        for i in range(n_top):
            m_re, m_im = mult(SUPER + i)
            t_re, t_im = _cmul(m_re, m_im, shifted(e_re, 1 << i), shifted(e_im, 1 << i))
            e_re, e_im = e_re + t_re, e_im + t_im
        s_re, s_im = shifted(e_re, 1), shifted(e_im, 1)
        for j2 in range(SUPER):
            if j2 == 0:
                pieces[j2][b] = (s_re, s_im)
            else:
                m_re, m_im = mult(j2)
                t_re, t_im = _cmul(m_re, m_im, s_re, s_im)
                pieces[j2][b] = (local[j2][0] + t_re, local[j2][1] + t_im)
    for gi in range(gb):
        sl = slice(gi * p, (gi + 1) * p)
        s_in = jnp.concatenate([jnp.concatenate([pieces[j2][b][0][sl], pieces[j2][b][1][sl]], axis=0)
                                for j2 in range(SUPER) for b in range(n_b)], axis=1)
        y = r[gi][:rows] + jnp.dot(wout_ref[gi], s_in.astype(BF16), preferred_element_type=F32)
        for j2 in range(SUPER):
            y_ref[j2, gi] = y[:, j2 * lanes:(j2 + 1) * lanes].reshape(CHUNK, SSM_GROUP, lanes).astype(y_ref.dtype)


def _s5_scan(u_t, lhs, wout, ap, n_b, c2, n_top):
    g, h = SSM_GROUPS, SSM_GROUP
    lanes = u_t.shape[-1]
    gb = SCAN_GROUPS
    grp = lambda *shape: pl.BlockSpec((gb,) + shape, lambda i: (i,) + (0,) * len(shape))
    return pl.pallas_call(
        functools.partial(_s5_group_kernel, n_b=n_b, c2=c2, n_top=n_top),
        out_shape=jax.ShapeDtypeStruct((SUPER, g, CHUNK, h, lanes), BF16),
        grid=(g // gb,),
        in_specs=[pl.BlockSpec((SUPER, CHUNK, gb * h, lanes), lambda i: (0, 0, i, 0)),
                  grp(*lhs.shape[1:]), grp(*wout.shape[1:]), grp(*ap.shape[1:])],
        out_specs=pl.BlockSpec((SUPER, gb, CHUNK, h, lanes), lambda i: (0, i, 0, 0, 0)),
        compiler_params=pltpu.CompilerParams(dimension_semantics=("arbitrary",),
                                             vmem_limit_bytes=VMEM_LIMIT),
        name="s5_chunk_scan",
    )(u_t, lhs, wout, ap)


def _glu_kernel(y_ref, zs_ref, w_ref, b_ref, o_ref, rows_scr):
    g, _, h, n = y_ref.shape
    for jj in range(CHUNK):
        y = jax.nn.gelu(y_ref[:, jj].reshape(g * h, n).astype(F32))
        t = jnp.dot(w_ref[...], y.astype(BF16), preferred_element_type=F32) + b_ref[...]
        rows_scr[:, jj, :] = (y * jax.nn.sigmoid(t) * zs_ref[jj].astype(F32)).T
    o_ref[...] = rows_scr[...].astype(BF16)


def _glu(y_t, zs_t, w_glu, b_glu, n_b, c2):
    _, g, _, h, _ = y_t.shape
    lanes = n_b * c2
    assert lanes == y_t.shape[-1] and c2 % LANES == 0
    w_t = w_glu.T.astype(BF16)
    return pl.pallas_call(
        _glu_kernel,
        out_shape=jax.ShapeDtypeStruct((n_b, c2, SUPER * CHUNK, SSM_WIDTH), BF16),
        grid=(SUPER, n_b),
        in_specs=[pl.BlockSpec((None, g, CHUNK, h, c2), lambda j2, b: (j2, 0, 0, 0, b)),
                  pl.BlockSpec((None, CHUNK, SSM_WIDTH, c2), lambda j2, b: (j2, 0, 0, b)),
                  pl.BlockSpec(w_t.shape, lambda j2, b: (0, 0)),
                  pl.BlockSpec((SSM_WIDTH, 1), lambda j2, b: (0, 0))],
        out_specs=pl.BlockSpec((None, c2, CHUNK, SSM_WIDTH), lambda j2, b: (b, 0, j2, 0)),
        scratch_shapes=[pltpu.VMEM((c2, CHUNK, SSM_WIDTH), F32)],
        compiler_params=pltpu.CompilerParams(dimension_semantics=("arbitrary", "arbitrary"),
                                             vmem_limit_bytes=VMEM_LIMIT),
        name="s5_glu",
    )(y_t, zs_t, w_t, b_glu.reshape(SSM_WIDTH, 1))


def _out_proj_kernel(ys_ref, ym_ref, x_ref, mod_ref, wa_ref, wb_ref, o_ref):
    y = (jnp.dot(ys_ref[...], wa_ref[...], preferred_element_type=F32)
         + lax.dot_general(ym_ref[...], wb_ref[...], _TN, preferred_element_type=F32))
    gate = mod_ref[...][:, 2 * D_MODEL:]
    o_ref[...] = x_ref[...] + gate * y


def _out_projection(ys, ym, x2, mod3, w_out, seq):
    t, d = x2.shape
    tm = min(1024, seq)
    per_b = seq // tm
    wa = w_out[:SSM_WIDTH].astype(BF16)
    wb = w_out[SSM_WIDTH:].astype(BF16)
    tok = lambda w: pl.BlockSpec((tm, w), lambda i: (i, 0))
    return pl.pallas_call(
        _out_proj_kernel,
        out_shape=jax.ShapeDtypeStruct(x2.shape, F32),
        grid=(t // tm,),
        in_specs=[tok(SSM_WIDTH), pl.BlockSpec((MLA_WIDTH, tm), lambda i: (0, i)), tok(d),
                  pl.BlockSpec((None, 1, 3 * d), lambda i: (i // per_b, 0, 0)),
                  pl.BlockSpec(wa.shape, lambda i: (0, 0)),
                  pl.BlockSpec(wb.shape, lambda i: (0, 0))],
        out_specs=tok(d),
        compiler_params=pltpu.CompilerParams(dimension_semantics=("arbitrary",),
                                             vmem_limit_bytes=VMEM_LIMIT),
        name="output_projection",
    )(ys, ym, x2, mod3, wa, wb)


def kernel(x, c, positions, w_ada, b_ada, norm_g, w_in, log_dt, lam_re, lam_im, b_re, b_im, c_re, c_im, d_skip,
           w_glu, b_glu, q_a_g, w_q_b, kv_a_g, w_kv_b, q_norm_g, k_norm_g, w_out):
    n_b, seq, d = x.shape
    depth = w_ada.shape[0]
    c2 = seq // (CHUNK * SUPER)
    n_top = max(int(math.log2(c2)), 0)
    assert c2 * CHUNK * SUPER == seq and (1 << n_top) == c2
    for l in range(depth):
        mod3 = _modulation(c, w_ada[l], b_ada[l]).reshape(n_b, 1, 3 * d)
        x2 = x.reshape(n_b * seq, d)
        lhs, wout, ap = _s5_coefficients(log_dt[l], lam_re[l], lam_im[l], b_re[l], b_im[l], c_re[l], c_im[l],
                                         d_skip[l], n_top)
        qt, k, vt, zm, h = _mla_projection(x2, mod3, norm_g[l], w_in[l], q_a_g[l], w_q_b[l],
                                           kv_a_g[l], w_kv_b[l], q_norm_g[l], k_norm_g[l], positions, n_b, seq)
        score_bound = (1.05 * math.sqrt(QK_HEAD) * math.log2(math.e)
                       * jnp.max(jnp.abs(q_norm_g[l])) * jnp.max(jnp.abs(k_norm_g[l])))
        ym = _attention(qt, k, vt, zm, score_bound)
        ym, h = lax.optimization_barrier((ym, h))
        u_t, zs_t = _ssm_projection(h.reshape(n_b, c2, SUPER * CHUNK, d), w_in[l])
        y_t = _s5_scan(u_t, lhs, wout, ap, n_b, c2, n_top)
        ys = _glu(y_t, zs_t, w_glu[l], b_glu[l], n_b, c2).reshape(n_b * seq, SSM_WIDTH)
        x = _out_projection(ys, ym, x2, mod3, w_out[l], seq).reshape(n_b, seq, d)
    return x
```
